```python
import math
import jax, jax.numpy as jnp
from jax import lax
import numpy as np

D_MODEL = 1024
BATCH = 8
SEQ = 4096
DEPTH = 1

MIX_WIDTH = D_MODEL
A_HEADS = 8
A_HEAD_DIM = 64
A_WIDTH = A_HEADS * A_HEAD_DIM
A_PATTERNS = ((128, 1), (512, 4), (2048, 16))
A_BLOCK = 64
REL_BUCKETS = 32
REL_MAX_DIST = 1024
B_HEADS = 8
B_NOPE = 64
B_ROPE = 32
B_VDIM = 64
B_WIDTH = B_HEADS * B_VDIM
Q_LORA = 384
KV_LORA = 256
ROPE_THETA = 10000.0
MLA_QBLOCK = 128
IN_COLS = 3 * A_WIDTH + Q_LORA + KV_LORA + B_ROPE
N_EXPERTS = 256
TOP_K = 8
N_GROUPS = 8
TOPK_GROUPS = 4
EXPERT_FF = 256
SHARED_FF = 256
ROUTED_SCALE = 2.5
MOE_BLOCK = 128
EPS = 1e-6
NEG_INF = -1e30

kernel_name = "hybrid_dilated_mla_moe_adaln_encoder"


def rmsnorm(x, g):
    xf = x.astype(jnp.float32)
    y = xf * lax.rsqrt(jnp.mean(xf * xf, axis=-1, keepdims=True) + EPS)
    return (y * g.astype(jnp.float32)).astype(x.dtype)


def t5_bucket(rel):
    half = REL_BUCKETS // 2
    max_exact = half // 2
    ret = jnp.where(rel > 0, half, 0)
    n = jnp.abs(rel)
    nf = jnp.maximum(n, 1).astype(jnp.float32)
    large = max_exact + (jnp.log(nf / max_exact) / math.log(REL_MAX_DIST / max_exact)
                         * (half - max_exact)).astype(jnp.int32)
    large = jnp.minimum(large, half - 1)
    return ret + jnp.where(n < max_exact, n, large)


def dilated_branch(q, k, v, rel_table, window, dilation):
    b, s, h, dh = q.shape
    radius = window // (2 * dilation)
    L = s // dilation
    nb = -(-L // A_BLOCK)
    Lp = nb * A_BLOCK

    def to_sub(t):
        t = t.reshape(b, L, dilation, h, dh).transpose(0, 2, 1, 3, 4).reshape(b * dilation, L, h, dh)
        return jnp.pad(t, ((0, 0), (0, Lp - L), (0, 0), (0, 0)))

    qs, ks, vs = to_sub(q), to_sub(k), to_sub(v)
    pad = ((0, 0), (A_BLOCK, A_BLOCK), (0, 0), (0, 0))
    kp, vp = jnp.pad(ks, pad), jnp.pad(vs, pad)

    def band(t):
        parts = [t[:, i * A_BLOCK: i * A_BLOCK + Lp].reshape(-1, nb, A_BLOCK, h, dh) for i in range(3)]
        return jnp.concatenate(parts, axis=2)

    kb, vb = band(kp), band(vp)
    qb = qs.reshape(-1, nb, A_BLOCK, h, dh)

    qi = jnp.arange(A_BLOCK, dtype=jnp.int32)[:, None]
    ki = jnp.arange(3 * A_BLOCK, dtype=jnp.int32)[None, :]
    off = ki - A_BLOCK - qi
    bias = rel_table[t5_bucket(off * dilation)].transpose(2, 0, 1).astype(jnp.float32)
    key_pos = jnp.arange(nb, dtype=jnp.int32)[:, None] * A_BLOCK + ki - A_BLOCK
    valid = (jnp.abs(off) <= radius)[None] & ((key_pos >= 0) & (key_pos < L))[:, None, :]

    scores = jnp.einsum('znqhd,znkhd->znhqk', qb, kb, preferred_element_type=jnp.float32) / math.sqrt(dh)
    scores = jnp.where(valid[None, :, None], scores + bias[None, None], NEG_INF)
    lse = jax.nn.logsumexp(scores, axis=-1)
    p = jnp.exp(scores - lse[..., None])
    o = jnp.einsum('znhqk,znkhd->znqhd', p.astype(v.dtype), vb)

    o = o.reshape(b, dilation, Lp, h, dh)[:, :, :L].transpose(0, 2, 1, 3, 4).reshape(b, s, h, dh)
    lse = lse.transpose(0, 1, 3, 2).reshape(b, dilation, Lp, h)[:, :, :L].transpose(0, 2, 1, 3).reshape(b, s, h)
    return o, lse


def dilated_attention(q, k, v, rel_table):
    outs, lses = [], []
    for window, dilation in A_PATTERNS:
        o, l = dilated_branch(q, k, v, rel_table, window, dilation)
        outs.append(o)
        lses.append(l)
    w = jax.nn.softmax(jnp.stack(lses, axis=0), axis=0)
    out = jnp.sum(w[..., None] * jnp.stack(outs, axis=0).astype(jnp.float32), axis=0)
    return out.astype(q.dtype)


def rope(t):
    s, r = t.shape[1], t.shape[-1]
    inv = ROPE_THETA ** (-jnp.arange(0, r, 2, dtype=jnp.float32) / r)
    ang = jnp.arange(s, dtype=jnp.float32)[:, None] * inv[None, :]
    cos, sin = jnp.cos(ang)[None, :, None, :], jnp.sin(ang)[None, :, None, :]
    t1, t2 = jnp.split(t.astype(jnp.float32), 2, axis=-1)
    return jnp.concatenate([t1 * cos - t2 * sin, t2 * cos + t1 * sin], axis=-1).astype(t.dtype)


def mla_attention(q, k, v):
    b, s, h, dq = q.shape
    nq = s // MLA_QBLOCK
    qb = q.reshape(b, nq, MLA_QBLOCK, h, dq).transpose(1, 0, 2, 3, 4)
    scale = 1.0 / math.sqrt(dq)

    def block(qblk):
        sc = jnp.einsum('bqhd,bkhd->bhqk', qblk, k, preferred_element_type=jnp.float32) * scale
        p = jax.nn.softmax(sc, axis=-1)
        return jnp.einsum('bhqk,bkhd->bqhd', p.astype(v.dtype), v)

    o = lax.map(block, qb)
    return o.transpose(1, 0, 2, 3, 4).reshape(b, s, h, v.shape[-1])


def routed_experts(xf, w_router, e_bias, w1, w3, w2):
    n, d = xf.shape
    scores = jax.nn.sigmoid(jnp.einsum('nd,de->ne', xf, w_router, preferred_element_type=jnp.float32))
    sel = scores + e_bias.astype(jnp.float32)
    grp = lax.top_k(sel.reshape(n, N_GROUPS, N_EXPERTS // N_GROUPS), 2)[0].sum(-1)
    _, gidx = lax.top_k(grp, TOPK_GROUPS)
    gmask = jnp.any(gidx[:, :, None] == jnp.arange(N_GROUPS, dtype=gidx.dtype)[None, None, :], axis=1)
    emask = jnp.repeat(gmask, N_EXPERTS // N_GROUPS, axis=1)
    _, eidx = lax.top_k(jnp.where(emask, sel, -jnp.inf), TOP_K)
    g = jnp.take_along_axis(scores, eidx, axis=1)
    g = g / jnp.sum(g, axis=-1, keepdims=True) * ROUTED_SCALE

    nk = n * TOP_K
    flat_e = eidx.reshape(nk).astype(jnp.int32)
    flat_t = jnp.repeat(jnp.arange(n, dtype=jnp.int32), TOP_K)
    flat_g = g.reshape(nk)
    order = jnp.argsort(flat_e)
    se, st, sg = flat_e[order], flat_t[order], flat_g[order]
    counts = jnp.zeros((N_EXPERTS,), jnp.int32).at[flat_e].add(1)
    padded = (counts + MOE_BLOCK - 1) // MOE_BLOCK * MOE_BLOCK
    start = jnp.cumsum(counts) - counts
    pends = jnp.cumsum(padded)
    pstart = pends - padded
    dest = pstart[se] + jnp.arange(nk, dtype=jnp.int32) - start[se]
    nblk = -(-(nk + N_EXPERTS * (MOE_BLOCK - 1)) // MOE_BLOCK)
    total = nblk * MOE_BLOCK
    tok_buf = jnp.zeros((total,), jnp.int32).at[dest].set(st)
    w_buf = jnp.zeros((total,), jnp.float32).at[dest].set(sg)
    blk_e = jnp.searchsorted(pends, jnp.arange(nblk, dtype=jnp.int32) * MOE_BLOCK, side='right')
    blk_e = jnp.minimum(blk_e, N_EXPERTS - 1)

    def expert_block(args):
        tok, wt, e = args
        xb = xf[tok]
        hdn = jax.nn.silu(xb @ w1[e]) * (xb @ w3[e])
        return (hdn @ w2[e]) * wt[:, None].astype(xf.dtype)

    yb = lax.map(expert_block, (tok_buf.reshape(nblk, MOE_BLOCK), w_buf.reshape(nblk, MOE_BLOCK), blk_e))
    return jax.ops.segment_sum(yb.reshape(total, d), tok_buf, num_segments=n)


def setup_inputs(seed: int = 0) -> dict:
    key = jax.random.key(seed)
    ks = jax.random.split(key, 24)
    D = D_MODEL

    def nrm(k, shape, fan_in, scale=1.0):
        return jax.random.normal(k, shape, jnp.float32) * (scale * fan_in ** -0.5)

    def gain(k, shape):
        return 1.0 + 0.05 * jax.random.normal(k, shape, jnp.float32)

    return {
        "x": jax.random.normal(ks[0], (BATCH, SEQ, D), jnp.float32),
        "c": jax.random.normal(ks[1], (BATCH, D), jnp.float32),
        "w_ada": nrm(ks[2], (DEPTH, D, 6 * D), D, 0.5),
        "b_ada": 0.02 * jax.random.normal(ks[3], (DEPTH, 6 * D), jnp.float32),
        "norm1_g": gain(ks[4], (DEPTH, D)),
        "w_in": nrm(ks[5], (DEPTH, D, IN_COLS), D),
        "q_norm_g": gain(ks[6], (DEPTH, Q_LORA)),
        "w_uq": nrm(ks[7], (DEPTH, Q_LORA, B_HEADS * (B_NOPE + B_ROPE)), Q_LORA),
        "kv_norm_g": gain(ks[8], (DEPTH, KV_LORA)),
        "w_ukv": nrm(ks[9], (DEPTH, KV_LORA, B_HEADS * (B_NOPE + B_VDIM)), KV_LORA),
        "rel_table": 0.5 * jax.random.normal(ks[10], (REL_BUCKETS, A_HEADS), jnp.float32),
        "a_out_g": gain(ks[11], (DEPTH, A_WIDTH)),
        "b_out_g": gain(ks[12], (DEPTH, B_WIDTH)),
        "w_o": nrm(ks[13], (DEPTH, MIX_WIDTH, D), MIX_WIDTH),
        "norm2_g": gain(ks[14], (DEPTH, D)),
        "w_router": nrm(ks[15], (DEPTH, D, N_EXPERTS), D),
        "e_bias": 0.01 * jax.random.normal(ks[16], (DEPTH, N_EXPERTS), jnp.float32),
        "w1": nrm(ks[17], (DEPTH, N_EXPERTS, D, EXPERT_FF), D),
        "w3": nrm(ks[18], (DEPTH, N_EXPERTS, D, EXPERT_FF), D),
        "w2": nrm(ks[19], (DEPTH, N_EXPERTS, EXPERT_FF, D), EXPERT_FF),
        "ws1": nrm(ks[20], (DEPTH, D, SHARED_FF), D),
        "ws3": nrm(ks[21], (DEPTH, D, SHARED_FF), D),
        "ws2": nrm(ks[22], (DEPTH, SHARED_FF, D), SHARED_FF),
        "final_g": gain(ks[23], (D,)),
    }


def reference(x, c, w_ada, b_ada, norm1_g, w_in, q_norm_g, w_uq, kv_norm_g, w_ukv, rel_table,
              a_out_g, b_out_g, w_o, norm2_g, w_router, e_bias, w1, w3, w2, ws1, ws3, ws2, final_g):
    b, s, d = x.shape
    for l in range(DEPTH):
        mod = jnp.einsum('bd,de->be', jax.nn.silu(c), w_ada[l]) + b_ada[l]
        sh1, sc1, g1, sh2, sc2, g2 = [m[:, None, :] for m in jnp.split(mod, 6, axis=-1)]

        h = rmsnorm(x, norm1_g[l]) * (1 + sc1) + sh1
        proj = jnp.einsum('bsd,dc->bsc', h, w_in[l])
        qkv_a, q_lat, kv_lat, k_pe = jnp.split(
            proj, [3 * A_WIDTH, 3 * A_WIDTH + Q_LORA, 3 * A_WIDTH + Q_LORA + KV_LORA], axis=-1)

        qkv_a = qkv_a.reshape(b, s, 3, A_HEADS, A_HEAD_DIM)
        a_out = dilated_attention(qkv_a[:, :, 0], qkv_a[:, :, 1], qkv_a[:, :, 2], rel_table)
        a_out = a_out.reshape(b, s, A_WIDTH)

        qm = jnp.einsum('bsr,rc->bsc', rmsnorm(q_lat, q_norm_g[l]), w_uq[l]).reshape(b, s, B_HEADS, B_NOPE + B_ROPE)
        kv = jnp.einsum('bsr,rc->bsc', rmsnorm(kv_lat, kv_norm_g[l]), w_ukv[l]).reshape(b, s, B_HEADS, B_NOPE + B_VDIM)
        q_nope, q_pe = jnp.split(qm, [B_NOPE], axis=-1)
        k_nope, v_b = jnp.split(kv, [B_NOPE], axis=-1)
        q_b = jnp.concatenate([q_nope, rope(q_pe)], axis=-1)
        k_pe = jnp.broadcast_to(rope(k_pe[:, :, None, :]), (b, s, B_HEADS, B_ROPE))
        k_b = jnp.concatenate([k_nope, k_pe], axis=-1)
        b_out = mla_attention(q_b, k_b, v_b).reshape(b, s, B_WIDTH)

        mix = jnp.concatenate([rmsnorm(a_out, a_out_g[l]), rmsnorm(b_out, b_out_g[l])], axis=-1)
        x = x + g1 * jnp.einsum('bsc,cd->bsd', mix, w_o[l])

        h2 = rmsnorm(x, norm2_g[l]) * (1 + sc2) + sh2
        xf = h2.reshape(b * s, d)
        shared = (jax.nn.silu(xf @ ws1[l]) * (xf @ ws3[l])) @ ws2[l]
        routed = routed_experts(xf, w_router[l], e_bias[l], w1[l], w3[l], w2[l])
        x = x + g2 * (shared + routed).reshape(b, s, d)
    return rmsnorm(x, final_g)
```

```python
import functools
import math

import jax
import jax.numpy as jnp
from jax import lax
from jax.experimental import pallas as pl
from jax.experimental.pallas import tpu as pltpu

F32 = jnp.float32
BF16 = jnp.bfloat16
HIGHEST = lax.Precision.HIGHEST

D_MODEL = 1024
A_HEADS = 8
A_HEAD_DIM = 64
A_WIDTH = A_HEADS * A_HEAD_DIM
A_PATTERNS = ((128, 1), (512, 4), (2048, 16))
A_RADIUS = 64
REL_BUCKETS = 32
REL_MAX_DIST = 1024
B_HEADS = 8
B_NOPE = 64
B_ROPE = 32
B_VDIM = 64
B_WIDTH = B_HEADS * B_VDIM
Q_LORA = 384
KV_LORA = 256
ROPE_THETA = 10000.0
N_EXPERTS = 256
TOP_K = 8
N_GROUPS = 8
GROUP_SIZE = N_EXPERTS // N_GROUPS
TOPK_GROUPS = 4
EXPERT_FF = 256
SHARED_FF = 256
ROUTED_SCALE = 2.5
EPS = 1e-6
NEG_INF = -1e30

LANES = 128
HEAD_PAD = 128
IN_COLS_EXT = 3 * A_WIDTH + Q_LORA + KV_LORA + HEAD_PAD

TM_INPROJ = 512
TM_OUTPROJ = 256
DIL_QB = 128
DIL_KW = DIL_QB + 2 * A_RADIUS
MLA_TQ = 256
ROUTE_T = 256
MOE_BLK = 256
DISP_T = 256
COMB_T = 128
VMEM_LIMIT = 56 * 1024 * 1024


def _cparams(sem):
    return pltpu.CompilerParams(dimension_semantics=sem, vmem_limit_bytes=VMEM_LIMIT)


def _rms(x, g):
    return x * lax.rsqrt(jnp.mean(x * x, axis=-1, keepdims=True) + EPS) * g


def _silu(x):
    return x * jax.nn.sigmoid(x)


def _ada_kernel(c_ref, w_ref, b_ref, o_ref):
    o_ref[...] = jnp.dot(_silu(c_ref[...]), w_ref[...], precision=HIGHEST,
                         preferred_element_type=F32) + b_ref[...]


def _ada(c, w_ada, b_ada):
    b, d = c.shape
    n6 = w_ada.shape[1] // d
    return pl.pallas_call(
        _ada_kernel,
        grid=(n6,),
        in_specs=[pl.BlockSpec((b, d), lambda j: (0, 0)),
                  pl.BlockSpec((d, d), lambda j: (0, j)),
                  pl.BlockSpec((1, d), lambda j: (0, j))],
        out_specs=pl.BlockSpec((b, d), lambda j: (0, j)),
        out_shape=jax.ShapeDtypeStruct((b, n6 * d), F32),
        compiler_params=_cparams(("parallel",)),
        name="ada",
    )(c, w_ada, b_ada.reshape(1, -1))


def _t5_bucket(rel):
    half = REL_BUCKETS // 2
    max_exact = half // 2
    ret = jnp.where(rel > 0, half, 0)
    n = jnp.abs(rel)
    nf = jnp.maximum(n, 1).astype(jnp.float32)
    large = max_exact + (jnp.log(nf / max_exact) / math.log(REL_MAX_DIST / max_exact)
                         * (half - max_exact)).astype(jnp.int32)
    large = jnp.minimum(large, half - 1)
    return ret + jnp.where(n < max_exact, n, large)


DIL_SHIFTS = (A_RADIUS, 0, -A_RADIUS)


def _bucket_tiles():
    qi = jnp.arange(DIL_QB, dtype=jnp.int32)[:, None]
    ki = jnp.arange(DIL_KW, dtype=jnp.int32)[None, :]
    tiles = []
    for _, dilation in A_PATTERNS:
        for shift in DIL_SHIFTS:
            off = ki + shift - A_RADIUS - qi
            bkt = _t5_bucket(off * dilation)
            tiles.append(jnp.where(jnp.abs(off) <= A_RADIUS, bkt, -1))
    return jnp.stack(tiles, axis=0)


def _bias_kernel(tab_ref, bkt_ref, o_ref):
    bkt = bkt_ref[0]
    for h in range(A_HEADS):
        acc = jnp.full(bkt.shape, NEG_INF, F32)
        for b in range(REL_BUCKETS):
            acc = jnp.where(bkt == b, tab_ref[b, h], acc)
        o_ref[0, h] = acc


def _bias_tiles(rel_table):
    bkt = _bucket_tiles()
    nt = bkt.shape[0]
    return pl.pallas_call(
        _bias_kernel,
        grid=(nt,),
        in_specs=[pl.BlockSpec(memory_space=pltpu.SMEM),
                  pl.BlockSpec((1, DIL_QB, DIL_KW), lambda t: (t, 0, 0))],
        out_specs=pl.BlockSpec((1, A_HEADS, DIL_QB, DIL_KW), lambda t: (t, 0, 0, 0)),
        out_shape=jax.ShapeDtypeStruct((nt, A_HEADS, DIL_QB, DIL_KW), F32),
        compiler_params=_cparams(("parallel",)),
        name="bias",
    )(rel_table, bkt)


def _rope(x, cos, sin, lane_lt_mid):
    half = B_ROPE // 2
    rot = jnp.where(lane_lt_mid, pltpu.roll(x, HEAD_PAD - half, 1), pltpu.roll(x, half, 1))
    return x * cos + rot * sin


def _inproj_kernel(x_ref, sc_ref, sh_ref, g1_ref, win_ref, qg_ref, wuq_ref, kvg_ref, wuk_ref, wv_ref,
                   cos_ref, sin_ref, qa_ref, ka_ref, va_ref, qb_ref, kb_ref, vb_ref):
    x = x_ref[...]
    h = _rms(x, g1_ref[...]) * (1.0 + sc_ref[...]) + sh_ref[...]
    proj = jnp.dot(h.astype(BF16), win_ref[...], preferred_element_type=F32)
    aw = A_WIDTH
    qa_ref[...] = proj[:, 0:aw] * (1.0 / math.sqrt(A_HEAD_DIM))
    ka_ref[...] = proj[:, aw:2 * aw]
    va_ref[...] = proj[:, 2 * aw:3 * aw]
    c0 = 3 * aw
    q_lat = proj[:, c0:c0 + Q_LORA]
    kv_lat = proj[:, c0 + Q_LORA:c0 + Q_LORA + KV_LORA]
    kpe = proj[:, c0 + Q_LORA + KV_LORA:]
    qn = _rms(q_lat, qg_ref[...]).astype(BF16)
    kvn = _rms(kv_lat, kvg_ref[...]).astype(BF16)
    qm = jnp.dot(qn, wuq_ref[...], preferred_element_type=F32)
    kn = jnp.dot(kvn, wuk_ref[...], preferred_element_type=F32)
    vv = jnp.dot(kvn, wv_ref[...], preferred_element_type=F32)
    cos = cos_ref[...]
    sin = sin_ref[...]
    lane = lax.broadcasted_iota(jnp.int32, cos.shape, 1)
    lt_mid = lane < (B_NOPE + B_ROPE // 2)
    for hd in range(B_HEADS):
        sl = slice(hd * HEAD_PAD, (hd + 1) * HEAD_PAD)
        qb_ref[hd] = _rope(qm[:, sl], cos, sin, lt_mid).astype(BF16)
        kb_ref[hd] = _rope(kn[:, sl] + kpe, cos, sin, lt_mid).astype(BF16)
    for p in range(B_HEADS // 2):
        vb_ref[p] = vv[:, p * LANES:(p + 1) * LANES].astype(BF16)


def _inproj(x2, mod3, norm1_g, w_in_ext, q_norm_g, w_uq_p, kv_norm_g, w_uk_p, w_v, cos_t, sin_t, b, s):
    n, d = x2.shape
    tm = TM_INPROJ
    tpb = s // tm
    row = lambda i: (i, 0)
    const = lambda i: (0, 0)
    hm = lambda i: (i // tpb, 0, i % tpb, 0)
    return pl.pallas_call(
        _inproj_kernel,
        grid=(n // tm,),
        in_specs=[pl.BlockSpec((tm, d), row),
                  pl.BlockSpec((None, 1, d), lambda i: ((i // tpb) * 6 + 1, 0, 0)),
                  pl.BlockSpec((None, 1, d), lambda i: ((i // tpb) * 6 + 0, 0, 0)),
                  pl.BlockSpec((1, d), const),
                  pl.BlockSpec(w_in_ext.shape, const),
                  pl.BlockSpec((1, Q_LORA), const),
                  pl.BlockSpec(w_uq_p.shape, const),
                  pl.BlockSpec((1, KV_LORA), const),
                  pl.BlockSpec(w_uk_p.shape, const),
                  pl.BlockSpec(w_v.shape, const),
                  pl.BlockSpec((tm, HEAD_PAD), lambda i: (i % tpb, 0)),
                  pl.BlockSpec((tm, HEAD_PAD), lambda i: (i % tpb, 0))],
        out_specs=[pl.BlockSpec((tm, A_WIDTH), row),
                   pl.BlockSpec((tm, A_WIDTH), row),
                   pl.BlockSpec((tm, A_WIDTH), row),
                   pl.BlockSpec((None, B_HEADS, tm, HEAD_PAD), hm),
                   pl.BlockSpec((None, B_HEADS, tm, HEAD_PAD), hm),
                   pl.BlockSpec((None, B_HEADS // 2, tm, LANES), hm)],
        out_shape=[jax.ShapeDtypeStruct((n, A_WIDTH), F32),
                   jax.ShapeDtypeStruct((n, A_WIDTH), F32),
                   jax.ShapeDtypeStruct((n, A_WIDTH), F32),
                   jax.ShapeDtypeStruct((b, B_HEADS, s, HEAD_PAD), BF16),
                   jax.ShapeDtypeStruct((b, B_HEADS, s, HEAD_PAD), BF16),
                   jax.ShapeDtypeStruct((b, B_HEADS // 2, s, LANES), BF16)],
        compiler_params=_cparams(("parallel",)),
        name="inproj",
    )(x2, mod3, mod3, norm1_g, w_in_ext, q_norm_g, w_uq_p, kv_norm_g, w_uk_p, w_v, cos_t, sin_t)


def _dil_block(q_ref, k_ref, v_ref, bias_ref, o_scr, l_scr, pi, dil, r, q0, ws, var):
    if dil == 1:
        qsl = pl.ds(q0, DIL_QB)
        ksl = pl.ds(ws, DIL_KW)
    else:
        qsl = pl.ds(r + dil * q0, DIL_QB, stride=dil)
        ksl = pl.ds(r + dil * ws, DIL_KW, stride=dil)
    q = q_ref[qsl, :]
    kw = k_ref[ksl, :].astype(BF16)
    vw = v_ref[ksl, :].astype(BF16)
    lo = lax.broadcasted_iota(jnp.int32, q.shape, 1) < A_HEAD_DIM
    outs, lses = [], []
    for hh in range(2):
        qm = jnp.where(lo if hh == 0 else jnp.logical_not(lo), q, 0.0).astype(BF16)
        sc = lax.dot_general(qm, kw, (((1,), (1,)), ((), ())), preferred_element_type=F32)
        sc = sc + bias_ref[pi * 3 + var, hh]
        m = jnp.max(sc, axis=-1, keepdims=True)
        p = jnp.exp(sc - m)
        l = jnp.sum(p, axis=-1, keepdims=True)
        o = jnp.dot(p.astype(BF16), vw, preferred_element_type=F32)
        outs.append(o / l)
        lses.append(m + jnp.log(l))
    o_scr[pi, qsl, :] = jnp.where(lo, outs[0], outs[1])
    l_scr[pi, qsl, :] = jnp.where(lo, lses[0], lses[1])


def _dilated_kernel(q_ref, k_ref, v_ref, bias_ref, out_ref, o_scr, l_scr):
    s = q_ref.shape[0]
    for pi, (_, dil) in enumerate(A_PATTERNS):
        sub_len = s // dil
        nblk = sub_len // DIL_QB
        blk = functools.partial(_dil_block, q_ref, k_ref, v_ref, bias_ref, o_scr, l_scr, pi, dil)

        def per_residue(r, carry, blk=blk, sub_len=sub_len, nblk=nblk):
            blk(r, 0, 0, 0)

            def mid(i, c):
                q0 = pl.multiple_of(i * DIL_QB, DIL_QB)
                blk(r, q0, q0 - A_RADIUS, 1)
                return c

            lax.fori_loop(1, nblk - 1, mid, 0)
            blk(r, sub_len - DIL_QB, sub_len - DIL_KW, 2)
            return carry

        if dil == 1:
            per_residue(0, 0)
        else:
            lax.fori_loop(0, dil, per_residue, 0)

    chunk = 512

    def comb(i, c):
        rows = pl.ds(pl.multiple_of(i * chunk, chunk), chunk)
        l0, l1, l2 = l_scr[0, rows, :], l_scr[1, rows, :], l_scr[2, rows, :]
        mx = jnp.maximum(jnp.maximum(l0, l1), l2)
        e0, e1, e2 = jnp.exp(l0 - mx), jnp.exp(l1 - mx), jnp.exp(l2 - mx)
        num = e0 * o_scr[0, rows, :] + e1 * o_scr[1, rows, :] + e2 * o_scr[2, rows, :]
        out_ref[rows, :] = num / (e0 + e1 + e2)
        return c

    lax.fori_loop(0, s // chunk, comb, 0)


def _dilated(qa, ka, va, bias):
    b, s, _ = qa.shape
    npair = A_WIDTH // LANES
    for _, dil in A_PATTERNS:
        assert (s // dil) % DIL_QB == 0 and s // dil >= DIL_KW
    blk = pl.BlockSpec((None, s, LANES), lambda bi, p: (bi, 0, p))
    return pl.pallas_call(
        _dilated_kernel,
        grid=(b, npair),
        in_specs=[blk, blk, blk,
                  pl.BlockSpec((bias.shape[0], 2, DIL_QB, DIL_KW), lambda bi, p: (0, p, 0, 0))],
        out_specs=blk,
        out_shape=jax.ShapeDtypeStruct((b, s, A_WIDTH), F32),
        scratch_shapes=[pltpu.VMEM((len(A_PATTERNS), s, LANES), F32),
                        pltpu.VMEM((len(A_PATTERNS), s, LANES), F32)],
        compiler_params=_cparams(("parallel", "parallel")),
        name="dilated",
    )(qa, ka, va, bias)


def _mla_kernel(q_ref, k_ref, v_ref, o_ref):
    scale = 1.0 / math.sqrt(B_NOPE + B_ROPE)
    v = v_ref[...]
    outs = []
    for hh in range(2):
        sc = lax.dot_general(q_ref[hh], k_ref[hh], (((1,), (1,)), ((), ())),
                             preferred_element_type=F32) * scale
        m = jnp.max(sc, axis=-1, keepdims=True)
        p = jnp.exp(sc - m)
        l = jnp.sum(p, axis=-1, keepdims=True)
        outs.append(jnp.dot(p.astype(BF16), v, preferred_element_type=F32) / l)
    lo = lax.broadcasted_iota(jnp.int32, outs[0].shape, 1) < B_VDIM
    o_ref[...] = jnp.where(lo, outs[0], outs[1])


def _mla(qb, kb, vb):
    b, h, s, _ = qb.shape
    npair = h // 2
    tq = MLA_TQ
    return pl.pallas_call(
        _mla_kernel,
        grid=(b, npair, s // tq),
        in_specs=[pl.BlockSpec((None, 2, tq, HEAD_PAD), lambda bi, p, qi: (bi, p, qi, 0)),
                  pl.BlockSpec((None, 2, s, HEAD_PAD), lambda bi, p, qi: (bi, p, 0, 0)),
                  pl.BlockSpec((None, None, s, LANES), lambda bi, p, qi: (bi, p, 0, 0))],
        out_specs=pl.BlockSpec((None, tq, LANES), lambda bi, p, qi: (bi, qi, p)),
        out_shape=jax.ShapeDtypeStruct((b, s, B_WIDTH), F32),
        compiler_params=_cparams(("parallel", "parallel", "arbitrary")),
        name="mla",
    )(qb, kb, vb)


def _outproj_kernel(x_ref, a_ref, b_ref, ag_ref, bg_ref, wo_ref, g1_ref, n2_ref, sc_ref, sh_ref, g2_ref,
                    wr_ref, ws1_ref, ws3_ref, ws2_ref, x1s_ref, h2_ref, scores_ref):
    an = _rms(a_ref[...], ag_ref[...])
    bn = _rms(b_ref[...], bg_ref[...])
    mix = jnp.concatenate([an, bn], axis=-1).astype(BF16)
    x1 = x_ref[...] + g1_ref[...] * jnp.dot(mix, wo_ref[...], preferred_element_type=F32)
    h2 = _rms(x1, n2_ref[...]) * (1.0 + sc_ref[...]) + sh_ref[...]
    h2_ref[...] = h2
    logits = jnp.dot(h2, wr_ref[...], precision=HIGHEST, preferred_element_type=F32)
    scores_ref[...] = jax.nn.sigmoid(logits)
    h2b = h2.astype(BF16)
    hid = _silu(jnp.dot(h2b, ws1_ref[...], preferred_element_type=F32)) * jnp.dot(
        h2b, ws3_ref[...], preferred_element_type=F32)
    shared = jnp.dot(hid.astype(BF16), ws2_ref[...], preferred_element_type=F32)
    x1s_ref[...] = x1 + g2_ref[...] * shared


def _outproj(x2, a_out, b_out, a_out_g, b_out_g, w_o, mod3, norm2_g, w_router, ws1, ws3, ws2, s):
    n, d = x2.shape
    tm = TM_OUTPROJ
    tpb = s // tm
    row = lambda i: (i, 0)
    const = lambda i: (0, 0)
    modspec = lambda j: pl.BlockSpec((None, 1, d), lambda i: ((i // tpb) * 6 + j, 0, 0))
    return pl.pallas_call(
        _outproj_kernel,
        grid=(n // tm,),
        in_specs=[pl.BlockSpec((tm, d), row),
                  pl.BlockSpec((tm, A_WIDTH), row),
                  pl.BlockSpec((tm, B_WIDTH), row),
                  pl.BlockSpec((1, A_WIDTH), const),
                  pl.BlockSpec((1, B_WIDTH), const),
                  pl.BlockSpec(w_o.shape, const),
                  modspec(2),
                  pl.BlockSpec((1, d), const),
                  modspec(4), modspec(3), modspec(5),
                  pl.BlockSpec(w_router.shape, const),
                  pl.BlockSpec(ws1.shape, const),
                  pl.BlockSpec(ws3.shape, const),
                  pl.BlockSpec(ws2.shape, const)],
        out_specs=[pl.BlockSpec((tm, d), row),
                   pl.BlockSpec((tm, d), row),
                   pl.BlockSpec((tm, N_EXPERTS), row)],
        out_shape=[jax.ShapeDtypeStruct((n, d), F32),
                   jax.ShapeDtypeStruct((n, d), F32),
                   jax.ShapeDtypeStruct((n, N_EXPERTS), F32)],
        compiler_params=_cparams(("parallel",)),
        name="outproj",
    )(x2, a_out, b_out, a_out_g, b_out_g, w_o, mod3, norm2_g, mod3, mod3, mod3, w_router, ws1, ws3, ws2)


def _route_kernel(scores_ref, bias_ref, eidx_ref, gate_ref, cnt_ref):
    sc = scores_ref[...]
    sel = sc + bias_ref[...]
    t = sc.shape[0]
    lane = lax.broadcasted_iota(jnp.int32, sc.shape, 1)
    grp = lane // GROUP_SIZE
    ninf = -jnp.inf
    big = jnp.int32(1 << 30)

    gscore = []
    for g in range(N_GROUPS):
        mg = jnp.where(grp == g, sel, ninf)
        m1 = jnp.max(mg, axis=-1, keepdims=True)
        i1 = jnp.min(jnp.where(mg == m1, lane, big), axis=-1, keepdims=True)
        m2 = jnp.max(jnp.where(lane == i1, ninf, mg), axis=-1, keepdims=True)
        gscore.append(m1 + m2)

    emask = jnp.zeros(sc.shape, jnp.bool_)
    for g in range(N_GROUPS):
        rank = jnp.zeros((t, 1), jnp.int32)
        for g2 in range(N_GROUPS):
            if g2 == g:
                continue
            ahead = (gscore[g2] > gscore[g]) | ((gscore[g2] == gscore[g]) & (g2 < g))
            rank = rank + ahead.astype(jnp.int32)
        emask = emask | ((grp == g) & (rank < TOPK_GROUPS))

    msel = jnp.where(emask, sel, ninf)
    k_iota = lax.broadcasted_iota(jnp.int32, (t, TOP_K), 1)
    eidx = jnp.zeros((t, TOP_K), jnp.int32)
    gates = jnp.zeros((t, TOP_K), F32)
    onehot = jnp.zeros(sc.shape, jnp.int32)
    for k in range(TOP_K):
        m = jnp.max(msel, axis=-1, keepdims=True)
        idx = jnp.min(jnp.where(msel == m, lane, big), axis=-1, keepdims=True)
        hit = lane == idx
        gk = jnp.sum(jnp.where(hit, sc, 0.0), axis=-1, keepdims=True)
        msel = jnp.where(hit, ninf, msel)
        onehot = onehot + hit.astype(jnp.int32)
        eidx = jnp.where(k_iota == k, idx, eidx)
        gates = jnp.where(k_iota == k, gk, gates)
    gsum = jnp.sum(gates, axis=-1, keepdims=True)
    eidx_ref[...] = eidx
    gate_ref[...] = gates / gsum * ROUTED_SCALE

    @pl.when(pl.program_id(0) == 0)
    def _():
        cnt_ref[...] = jnp.zeros_like(cnt_ref)

    cnt_ref[...] += jnp.sum(onehot, axis=0, keepdims=True)


def _route(scores, e_bias):
    n, e = scores.shape
    t = ROUTE_T
    return pl.pallas_call(
        _route_kernel,
        grid=(n // t,),
        in_specs=[pl.BlockSpec((t, e), lambda i: (i, 0)),
                  pl.BlockSpec((1, e), lambda i: (0, 0))],
        out_specs=[pl.BlockSpec((t, TOP_K), lambda i: (i, 0)),
                   pl.BlockSpec((t, TOP_K), lambda i: (i, 0)),
                   pl.BlockSpec((1, e), lambda i: (0, 0))],
        out_shape=[jax.ShapeDtypeStruct((n, TOP_K), jnp.int32),
                   jax.ShapeDtypeStruct((n, TOP_K), F32),
                   jax.ShapeDtypeStruct((1, e), jnp.int32)],
        compiler_params=_cparams(("arbitrary",)),
        name="route",
    )(scores, e_bias)


def _dest_kernel(eidx_ref, pstart_ref, dest_ref, carry_ref):
    @pl.when(pl.program_id(0) == 0)
    def _():
        carry_ref[...] = jnp.zeros_like(carry_ref)

    eidx = eidx_ref[...]
    t = eidx.shape[0]
    lane = lax.broadcasted_iota(jnp.int32, (t, N_EXPERTS), 1)
    hits = [lane == eidx[:, k:k + 1] for k in range(TOP_K)]
    onehot = jnp.zeros((t, N_EXPERTS), F32)
    for hit in hits:
        onehot = onehot + hit.astype(F32)
    row = lax.broadcasted_iota(jnp.int32, (t, t), 0)
    col = lax.broadcasted_iota(jnp.int32, (t, t), 1)
    tri = (row > col).astype(BF16)
    before = jnp.dot(tri, onehot.astype(BF16), preferred_element_type=F32)
    base = before + carry_ref[...] + pstart_ref[...]
    k_iota = lax.broadcasted_iota(jnp.int32, (t, TOP_K), 1)
    dest = jnp.zeros((t, TOP_K), jnp.int32)
    for k in range(TOP_K):
        dk = jnp.sum(jnp.where(hits[k], base, 0.0), axis=-1, keepdims=True).astype(jnp.int32)
        dest = jnp.where(k_iota == k, dk, dest)
    dest_ref[...] = dest
    carry_ref[...] += jnp.sum(onehot, axis=0, keepdims=True)


def _dest(eidx, pstart):
    n = eidx.shape[0]
    t = ROUTE_T
    return pl.pallas_call(
        _dest_kernel,
        grid=(n // t,),
        in_specs=[pl.BlockSpec((t, TOP_K), lambda i: (i, 0)),
                  pl.BlockSpec((1, N_EXPERTS), lambda i: (0, 0))],
        out_specs=pl.BlockSpec((t, TOP_K), lambda i: (i, 0)),
        out_shape=jax.ShapeDtypeStruct((n, TOP_K), jnp.int32),
        scratch_shapes=[pltpu.VMEM((1, N_EXPERTS), F32)],
        compiler_params=_cparams(("arbitrary",)),
        name="dest",
    )(eidx, pstart)


def _dispatch_kernel(dest_ref, h_ref, xg_in_ref, xg_ref, sem):
    del xg_in_ref
    t = h_ref.shape[0]

    def row_copy(tok, k):
        return pltpu.make_async_copy(h_ref.at[pl.ds(tok, 1)],
                                     xg_ref.at[pl.ds(dest_ref[tok * TOP_K + k], 1)], sem)

    def issue(tok, c):
        for k in range(TOP_K):
            row_copy(tok, k).start()
        return c

    def drain(tok, c):
        for k in range(TOP_K):
            row_copy(tok, k).wait()
        return c

    lax.fori_loop(0, t, issue, 0)
    lax.fori_loop(0, t, drain, 0)


def _dispatch(dest_flat, h2, rows_total):
    n, d = h2.shape
    t = DISP_T
    xg0 = jnp.zeros((rows_total, d), F32)
    return pl.pallas_call(
        _dispatch_kernel,
        grid=(n // t,),
        in_specs=[pl.BlockSpec((t * TOP_K,), lambda i: (i,), memory_space=pltpu.SMEM),
                  pl.BlockSpec((t, d), lambda i: (i, 0)),
                  pl.BlockSpec(memory_space=pl.ANY)],
        out_specs=pl.BlockSpec(memory_space=pl.ANY),
        out_shape=jax.ShapeDtypeStruct((rows_total, d), F32),
        scratch_shapes=[pltpu.SemaphoreType.DMA],
        input_output_aliases={2: 0},
        compiler_params=_cparams(("arbitrary",)),
        name="dispatch",
    )(dest_flat, h2, xg0)


def _expert_kernel(blk_e_ref, x_ref, w1_ref, w3_ref, w2_ref, y_ref, w1b, w3b, w2b):
    i = pl.program_id(0)
    e = blk_e_ref[i]
    e_prev = blk_e_ref[jnp.maximum(i - 1, 0)]

    @pl.when((i == 0) | (e != e_prev))
    def _():
        w1b[...] = w1_ref[...].astype(BF16)
        w3b[...] = w3_ref[...].astype(BF16)
        w2b[...] = w2_ref[...].astype(BF16)

    xb = x_ref[...].astype(BF16)
    hid = _silu(jnp.dot(xb, w1b[...], preferred_element_type=F32)) * jnp.dot(
        xb, w3b[...], preferred_element_type=F32)
    y_ref[...] = jnp.dot(hid.astype(BF16), w2b[...], preferred_element_type=F32)


def _experts(blk_e, xg, w1, w3, w2):
    rows, d = xg.shape
    f = w1.shape[-1]
    nblk = rows // MOE_BLK
    grid_spec = pltpu.PrefetchScalarGridSpec(
        num_scalar_prefetch=1,
        grid=(nblk,),
        in_specs=[pl.BlockSpec((MOE_BLK, d), lambda i, be: (i, 0)),
                  pl.BlockSpec((None, d, f), lambda i, be: (be[i], 0, 0)),
                  pl.BlockSpec((None, d, f), lambda i, be: (be[i], 0, 0)),
                  pl.BlockSpec((None, f, d), lambda i, be: (be[i], 0, 0))],
        out_specs=pl.BlockSpec((MOE_BLK, d), lambda i, be: (i, 0)),
        scratch_shapes=[pltpu.VMEM((d, f), BF16), pltpu.VMEM((d, f), BF16), pltpu.VMEM((f, d), BF16)],
    )
    return pl.pallas_call(
        _expert_kernel,
        grid_spec=grid_spec,
        out_shape=jax.ShapeDtypeStruct((rows, d), F32),
        compiler_params=_cparams(("arbitrary",)),
        name="experts",
    )(blk_e, xg, w1, w3, w2)


def _combine_kernel(dest_ref, gate_ref, x1s_ref, g2_ref, fg_ref, y_ref, out_ref, buf, sem):
    t = x1s_ref.shape[0]

    def row_copy(tok, k):
        return pltpu.make_async_copy(y_ref.at[pl.ds(dest_ref[tok * TOP_K + k], 1)],
                                     buf.at[pl.ds(k * t + tok, 1)], sem)

    def issue(tok, c):
        for k in range(TOP_K):
            row_copy(tok, k).start()
        return c

    def drain(tok, c):
        for k in range(TOP_K):
            row_copy(tok, k).wait()
        return c

    lax.fori_loop(0, t, issue, 0)
    lax.fori_loop(0, t, drain, 0)
    gates = gate_ref[...]
    routed = gates[:, 0:1] * buf[0:t, :]
    for k in range(1, TOP_K):
        routed = routed + gates[:, k:k + 1] * buf[k * t:(k + 1) * t, :]
    x2 = x1s_ref[...] + g2_ref[...] * routed
    out_ref[...] = _rms(x2, fg_ref[...])


def _combine(dest_flat, gates, x1s, mod3, final_g, y, s):
    n, d = x1s.shape
    t = COMB_T
    tpb = s // t
    return pl.pallas_call(
        _combine_kernel,
        grid=(n // t,),
        in_specs=[pl.BlockSpec((t * TOP_K,), lambda i: (i,), memory_space=pltpu.SMEM),
                  pl.BlockSpec((t, TOP_K), lambda i: (i, 0)),
                  pl.BlockSpec((t, d), lambda i: (i, 0)),
                  pl.BlockSpec((None, 1, d), lambda i: ((i // tpb) * 6 + 5, 0, 0)),
                  pl.BlockSpec((1, d), lambda i: (0, 0)),
                  pl.BlockSpec(memory_space=pl.ANY)],
        out_specs=pl.BlockSpec((t, d), lambda i: (i, 0)),
        out_shape=jax.ShapeDtypeStruct((n, d), F32),
        scratch_shapes=[pltpu.VMEM((TOP_K * t, d), F32), pltpu.SemaphoreType.DMA],
        compiler_params=_cparams(("arbitrary",)),
        name="combine",
    )(dest_flat, gates, x1s, mod3, final_g, y)


def _place_heads(w, per_head, keep):
    r = w.shape[0]
    w = w.reshape(r, B_HEADS, per_head)[:, :, :keep]
    return jnp.pad(w, ((0, 0), (0, 0), (0, HEAD_PAD - keep))).reshape(r, B_HEADS * HEAD_PAD)


def _rope_tables(s):
    half = B_ROPE // 2
    inv = ROPE_THETA ** (-jnp.arange(0, B_ROPE, 2, dtype=jnp.float32) / B_ROPE)
    ang = jnp.arange(s, dtype=jnp.float32)[:, None] * inv[None, :]
    cos, sin = jnp.cos(ang), jnp.sin(ang)
    ones = jnp.ones((s, B_NOPE), F32)
    zeros = jnp.zeros((s, B_NOPE), F32)
    tail1 = jnp.ones((s, HEAD_PAD - B_NOPE - B_ROPE), F32)
    tail0 = jnp.zeros((s, HEAD_PAD - B_NOPE - B_ROPE), F32)
    del half
    return (jnp.concatenate([ones, cos, cos, tail1], axis=1),
            jnp.concatenate([zeros, -sin, sin, tail0], axis=1))


def kernel(x, c, w_ada, b_ada, norm1_g, w_in, q_norm_g, w_uq, kv_norm_g, w_ukv, rel_table, a_out_g, b_out_g,
           w_o, norm2_g, w_router, e_bias, w1, w3, w2, ws1, ws3, ws2, final_g):
    b, s, d = x.shape
    n = b * s
    assert w_ada.shape[0] == 1, "single layer"
    x2 = x.reshape(n, d)

    mod = _ada(c, w_ada[0], b_ada[0])
    mod3 = mod.reshape(b * 6, 1, d)
    bias = _bias_tiles(rel_table)

    wi = w_in[0]
    c_kpe = 3 * A_WIDTH + Q_LORA + KV_LORA
    kpe_cols = jnp.pad(wi[:, c_kpe:], ((0, 0), (B_NOPE, HEAD_PAD - B_NOPE - B_ROPE)))
    w_in_ext = jnp.concatenate([wi[:, :c_kpe], kpe_cols], axis=1).astype(BF16)
    w_uq_p = _place_heads(w_uq[0], B_NOPE + B_ROPE, B_NOPE + B_ROPE).astype(BF16)
    w_uk_p = _place_heads(w_ukv[0], B_NOPE + B_VDIM, B_NOPE).astype(BF16)
    w_v = w_ukv[0].reshape(KV_LORA, B_HEADS, B_NOPE + B_VDIM)[:, :, B_NOPE:].reshape(KV_LORA, B_WIDTH).astype(BF16)
    cos_t, sin_t = _rope_tables(s)

    qa, ka, va, qb, kb, vb = _inproj(x2, mod3, norm1_g, w_in_ext, q_norm_g, w_uq_p, kv_norm_g, w_uk_p, w_v,
                                     cos_t, sin_t, b, s)
    a_out = _dilated(qa.reshape(b, s, A_WIDTH), ka.reshape(b, s, A_WIDTH), va.reshape(b, s, A_WIDTH), bias)
    b_out = _mla(qb, kb, vb)

    x1s, h2, scores = _outproj(x2, a_out.reshape(n, A_WIDTH), b_out.reshape(n, B_WIDTH), a_out_g, b_out_g,
                               w_o[0].astype(BF16), mod3, norm2_g, w_router[0],
                               ws1[0].astype(BF16), ws3[0].astype(BF16), ws2[0].astype(BF16), s)

    eidx, gates, counts = _route(scores, e_bias)

    counts = counts[0]
    padded = (counts + MOE_BLK - 1) // MOE_BLK * MOE_BLK
    pends = jnp.cumsum(padded)
    pstart = pends - padded
    nk = n * TOP_K
    nblk = -(-(nk + N_EXPERTS * (MOE_BLK - 1)) // MOE_BLK)
    rows_total = nblk * MOE_BLK
    blk_e = jnp.searchsorted(pends, jnp.arange(nblk, dtype=jnp.int32) * MOE_BLK, side='right')
    blk_e = jnp.minimum(blk_e, N_EXPERTS - 1).astype(jnp.int32)

    dest = _dest(eidx, pstart.astype(F32).reshape(1, N_EXPERTS))
    dest_flat = dest.reshape(nk)
    xg = _dispatch(dest_flat, h2, rows_total)
    y = _experts(blk_e, xg, w1[0], w3[0], w2[0])
    out = _combine(dest_flat, gates, x1s, mod3, final_g.reshape(1, d), y, s)
    return out.reshape(b, s, d)
```

```python
import functools
import math

import jax
import jax.numpy as jnp
from jax import lax
from jax.experimental import pallas as pl
from jax.experimental.pallas import tpu as pltpu

F32 = jnp.float32
BF16 = jnp.bfloat16
U32 = jnp.uint32
I32 = jnp.int32
HIGHEST = lax.Precision.HIGHEST

D_MODEL = 1024
A_HEADS = 8
A_HEAD_DIM = 64
A_WIDTH = A_HEADS * A_HEAD_DIM
A_PATTERNS = ((128, 1), (512, 4), (2048, 16))
A_RADIUS = 64
REL_BUCKETS = 32
REL_MAX_DIST = 1024
B_HEADS = 8
B_NOPE = 64
B_ROPE = 32
B_VDIM = 64
B_WIDTH = B_HEADS * B_VDIM
Q_LORA = 384
KV_LORA = 256
ROPE_THETA = 10000.0
N_EXPERTS = 256
TOP_K = 8
N_GROUPS = 8
GROUP_SIZE = N_EXPERTS // N_GROUPS
TOPK_GROUPS = 4
EXPERT_FF = 256
SHARED_FF = 256
ROUTED_SCALE = 2.5
EPS = 1e-6
NEG_INF = -1e30
LOG2E = math.log2(math.e)

LANES = 128
SUBLANES = 8
HEAD_PAD = 128
IN_COLS_EXT = 3 * A_WIDTH + Q_LORA + KV_LORA + HEAD_PAD

TM_INPROJ = 512
TM_OUTPROJ = 256
DIL_QB = 128
DIL_KW = DIL_QB + 2 * A_RADIUS
DIL_UNROLL = 4
MLA_TQ = 256
ROUTE_T = 256
MOE_BLK = 256
DISP_T = 256
COMB_T = 128
VMEM_LIMIT = 56 * 1024 * 1024


def _cparams(sem):
    return pltpu.CompilerParams(dimension_semantics=sem, vmem_limit_bytes=VMEM_LIMIT)


def _rms(x, g):
    return x * lax.rsqrt(jnp.mean(x * x, axis=-1, keepdims=True) + EPS) * g


def _silu(x):
    return x * jax.nn.sigmoid(x)


def _pack_rows(x):
    half = x.shape[1] // 2
    bits = lax.bitcast_convert_type(x.astype(BF16).astype(F32), U32)
    return (bits[:, :half] >> 16) | bits[:, half:]


def _unpack_rows(w):
    lo = lax.bitcast_convert_type(w << 16, F32)
    hi = lax.bitcast_convert_type(w & jnp.uint32(0xFFFF0000), F32)
    return lo, hi


def _ada_kernel(c_ref, w_ref, b_ref, o_ref):
    o_ref[...] = jnp.dot(_silu(c_ref[...]), w_ref[...], precision=HIGHEST,
                         preferred_element_type=F32) + b_ref[...]


def _ada(c, w_ada, b_ada):
    b, d = c.shape
    n6 = w_ada.shape[1] // d
    return pl.pallas_call(
        _ada_kernel,
        grid=(n6,),
        in_specs=[pl.BlockSpec((b, d), lambda j: (0, 0)),
                  pl.BlockSpec((d, d), lambda j: (0, j)),
                  pl.BlockSpec((1, d), lambda j: (0, j))],
        out_specs=pl.BlockSpec((b, d), lambda j: (0, j)),
        out_shape=jax.ShapeDtypeStruct((b, n6 * d), F32),
        compiler_params=_cparams(("parallel",)),
        name="ada",
    )(c, w_ada, b_ada.reshape(1, -1))


def _t5_bucket(rel):
    half = REL_BUCKETS // 2
    max_exact = half // 2
    ret = jnp.where(rel > 0, half, 0)
    n = jnp.abs(rel)
    nf = jnp.maximum(n, 1).astype(jnp.float32)
    large = max_exact + (jnp.log(nf / max_exact) / math.log(REL_MAX_DIST / max_exact)
                         * (half - max_exact)).astype(jnp.int32)
    large = jnp.minimum(large, half - 1)
    return ret + jnp.where(n < max_exact, n, large)


DIL_SHIFTS = (A_RADIUS, 0, -A_RADIUS)


def _bucket_tiles():
    qi = jnp.arange(DIL_QB, dtype=jnp.int32)[:, None]
    ki = jnp.arange(DIL_KW, dtype=jnp.int32)[None, :]
    tiles = []
    for _, dilation in A_PATTERNS:
        for shift in DIL_SHIFTS:
            off = ki + shift - A_RADIUS - qi
            bkt = _t5_bucket(off * dilation)
            tiles.append(jnp.where(jnp.abs(off) <= A_RADIUS, bkt, -1))
    return jnp.stack(tiles, axis=0)


def _bias_kernel(tab_ref, bkt_ref, o_ref):
    bkt = bkt_ref[0]
    for h in range(A_HEADS):
        acc = jnp.full(bkt.shape, NEG_INF, F32)
        for b in range(REL_BUCKETS):
            acc = jnp.where(bkt == b, tab_ref[b, h] * LOG2E, acc)
        o_ref[0, h] = acc


def _bias_tiles(rel_table):
    bkt = _bucket_tiles()
    nt = bkt.shape[0]
    return pl.pallas_call(
        _bias_kernel,
        grid=(nt,),
        in_specs=[pl.BlockSpec(memory_space=pltpu.SMEM),
                  pl.BlockSpec((1, DIL_QB, DIL_KW), lambda t: (t, 0, 0))],
        out_specs=pl.BlockSpec((1, A_HEADS, DIL_QB, DIL_KW), lambda t: (t, 0, 0, 0)),
        out_shape=jax.ShapeDtypeStruct((nt, A_HEADS, DIL_QB, DIL_KW), F32),
        compiler_params=_cparams(("parallel",)),
        name="bias",
    )(rel_table, bkt)


def _rope(x, cos, sin, lane_lt_mid):
    half = B_ROPE // 2
    rot = jnp.where(lane_lt_mid, pltpu.roll(x, HEAD_PAD - half, 1), pltpu.roll(x, half, 1))
    return x * cos + rot * sin


def _inproj_kernel(x_ref, sc_ref, sh_ref, g1_ref, win_ref, qg_ref, wuq_ref, kvg_ref, wuk_ref, wv_ref,
                   cos_ref, sin_ref, qa_ref, ka_ref, va_ref, qb_ref, kb_ref, vb_ref):
    x = x_ref[...]
    h = _rms(x, g1_ref[...]) * (1.0 + sc_ref[...]) + sh_ref[...]
    proj = jnp.dot(h.astype(BF16), win_ref[...], preferred_element_type=F32)
    aw = A_WIDTH
    qa_ref[...] = proj[:, 0:aw] * (LOG2E / math.sqrt(A_HEAD_DIM))
    ka_ref[...] = proj[:, aw:2 * aw]
    va_ref[...] = proj[:, 2 * aw:3 * aw]
    c0 = 3 * aw
    q_lat = proj[:, c0:c0 + Q_LORA]
    kv_lat = proj[:, c0 + Q_LORA:c0 + Q_LORA + KV_LORA]
    kpe = proj[:, c0 + Q_LORA + KV_LORA:]
    qn = _rms(q_lat, qg_ref[...]).astype(BF16)
    kvn = _rms(kv_lat, kvg_ref[...]).astype(BF16)
    qm = jnp.dot(qn, wuq_ref[...], preferred_element_type=F32)
    kn = jnp.dot(kvn, wuk_ref[...], preferred_element_type=F32)
    vv = jnp.dot(kvn, wv_ref[...], preferred_element_type=F32)
    cos = cos_ref[...]
    sin = sin_ref[...]
    lane = lax.broadcasted_iota(jnp.int32, cos.shape, 1)
    lt_mid = lane < (B_NOPE + B_ROPE // 2)
    qscale = LOG2E / math.sqrt(B_NOPE + B_ROPE)
    for hd in range(B_HEADS):
        sl = slice(hd * HEAD_PAD, (hd + 1) * HEAD_PAD)
        qb_ref[hd] = (_rope(qm[:, sl], cos, sin, lt_mid) * qscale).astype(BF16)
        kb_ref[hd] = _rope(kn[:, sl] + kpe, cos, sin, lt_mid).astype(BF16)
    for p in range(B_HEADS // 2):
        vb_ref[p] = vv[:, p * LANES:(p + 1) * LANES].astype(BF16)


def _inproj(x2, mod3, norm1_g, w_in_ext, q_norm_g, w_uq_p, kv_norm_g, w_uk_p, w_v, cos_t, sin_t, b, s):
    n, d = x2.shape
    tm = TM_INPROJ
    tpb = s // tm
    row = lambda i: (i, 0)
    const = lambda i: (0, 0)
    hm = lambda i: (i // tpb, 0, i % tpb, 0)
    return pl.pallas_call(
        _inproj_kernel,
        grid=(n // tm,),
        in_specs=[pl.BlockSpec((tm, d), row),
                  pl.BlockSpec((None, 1, d), lambda i: ((i // tpb) * 6 + 1, 0, 0)),
                  pl.BlockSpec((None, 1, d), lambda i: ((i // tpb) * 6 + 0, 0, 0)),
                  pl.BlockSpec((1, d), const),
                  pl.BlockSpec(w_in_ext.shape, const),
                  pl.BlockSpec((1, Q_LORA), const),
                  pl.BlockSpec(w_uq_p.shape, const),
                  pl.BlockSpec((1, KV_LORA), const),
                  pl.BlockSpec(w_uk_p.shape, const),
                  pl.BlockSpec(w_v.shape, const),
                  pl.BlockSpec((tm, HEAD_PAD), lambda i: (i % tpb, 0)),
                  pl.BlockSpec((tm, HEAD_PAD), lambda i: (i % tpb, 0))],
        out_specs=[pl.BlockSpec((tm, A_WIDTH), row),
                   pl.BlockSpec((tm, A_WIDTH), row),
                   pl.BlockSpec((tm, A_WIDTH), row),
                   pl.BlockSpec((None, B_HEADS, tm, HEAD_PAD), hm),
                   pl.BlockSpec((None, B_HEADS, tm, HEAD_PAD), hm),
                   pl.BlockSpec((None, B_HEADS // 2, tm, LANES), hm)],
        out_shape=[jax.ShapeDtypeStruct((n, A_WIDTH), F32),
                   jax.ShapeDtypeStruct((n, A_WIDTH), F32),
                   jax.ShapeDtypeStruct((n, A_WIDTH), F32),
                   jax.ShapeDtypeStruct((b, B_HEADS, s, HEAD_PAD), BF16),
                   jax.ShapeDtypeStruct((b, B_HEADS, s, HEAD_PAD), BF16),
                   jax.ShapeDtypeStruct((b, B_HEADS // 2, s, LANES), BF16)],
        compiler_params=_cparams(("parallel",)),
        name="inproj",
    )(x2, mod3, mod3, norm1_g, w_in_ext, q_norm_g, w_uq_p, kv_norm_g, w_uk_p, w_v, cos_t, sin_t)


def _dil_block(q_ref, k_ref, v_ref, bias_ref, o_scr, l_scr, pi, dil, nblk, job):
    sub_len = nblk * DIL_QB
    r = job // nblk
    bi = job % nblk
    q0 = bi * DIL_QB
    ws = jnp.clip(q0 - A_RADIUS, 0, sub_len - DIL_KW)
    var = jnp.where(bi == 0, 0, jnp.where(bi == nblk - 1, 2, 1))
    if dil == 1:
        qsl = pl.ds(pl.multiple_of(q0, DIL_QB), DIL_QB)
        ksl = pl.ds(pl.multiple_of(ws, A_RADIUS), DIL_KW)
    else:
        qsl = pl.ds(r + dil * q0, DIL_QB, stride=dil)
        ksl = pl.ds(r + dil * ws, DIL_KW, stride=dil)
    q = q_ref[qsl, :]
    kw = k_ref[ksl, :].astype(BF16)
    vw = v_ref[ksl, :].astype(BF16)
    lo = lax.broadcasted_iota(jnp.int32, q.shape, 1) < A_HEAD_DIM
    outs, lses = [], []
    for hh in range(2):
        qm = jnp.where(lo if hh == 0 else jnp.logical_not(lo), q, 0.0).astype(BF16)
        sc = lax.dot_general(qm, kw, (((1,), (1,)), ((), ())), preferred_element_type=F32)
        sc = sc + bias_ref[pi * 3 + var, hh]
        m = jnp.max(sc, axis=-1, keepdims=True)
        p = jnp.exp2(sc - m)
        l = jnp.sum(p, axis=-1, keepdims=True)
        o = jnp.dot(p.astype(BF16), vw, preferred_element_type=F32)
        outs.append(o / l)
        lses.append(m + jnp.log2(l))
    o_scr[pi, qsl, :] = jnp.where(lo, outs[0], outs[1])
    l_scr[pi, qsl, :] = jnp.where(lo, lses[0], lses[1])


def _dilated_kernel(q_ref, k_ref, v_ref, bias_ref, out_ref, o_scr, l_scr):
    s = q_ref.shape[0]
    njobs = s // DIL_QB
    for pi, (_, dil) in enumerate(A_PATTERNS):
        blk = functools.partial(_dil_block, q_ref, k_ref, v_ref, bias_ref, o_scr, l_scr, pi, dil,
                                s // dil // DIL_QB)

        def group(g, c, blk=blk):
            for u in range(DIL_UNROLL):
                blk(g * DIL_UNROLL + u)
            return c

        lax.fori_loop(0, njobs // DIL_UNROLL, group, 0)

    chunk = 512

    def comb(i, c):
        rows = pl.ds(pl.multiple_of(i * chunk, chunk), chunk)
        l0, l1, l2 = l_scr[0, rows, :], l_scr[1, rows, :], l_scr[2, rows, :]
        mx = jnp.maximum(jnp.maximum(l0, l1), l2)
        e0, e1, e2 = jnp.exp2(l0 - mx), jnp.exp2(l1 - mx), jnp.exp2(l2 - mx)
        num = e0 * o_scr[0, rows, :] + e1 * o_scr[1, rows, :] + e2 * o_scr[2, rows, :]
        out_ref[rows, :] = num / (e0 + e1 + e2)
        return c

    lax.fori_loop(0, s // chunk, comb, 0)


def _dilated(qa, ka, va, bias):
    b, s, _ = qa.shape
    npair = A_WIDTH // LANES
    assert (s // DIL_QB) % DIL_UNROLL == 0
    for _, dil in A_PATTERNS:
        assert (s // dil) % DIL_QB == 0 and s // dil >= DIL_KW
    blk = pl.BlockSpec((None, s, LANES), lambda bi, p: (bi, 0, p))
    return pl.pallas_call(
        _dilated_kernel,
        grid=(b, npair),
        in_specs=[blk, blk, blk,
                  pl.BlockSpec((bias.shape[0], 2, DIL_QB, DIL_KW), lambda bi, p: (0, p, 0, 0))],
        out_specs=blk,
        out_shape=jax.ShapeDtypeStruct((b, s, A_WIDTH), F32),
        scratch_shapes=[pltpu.VMEM((len(A_PATTERNS), s, LANES), F32),
                        pltpu.VMEM((len(A_PATTERNS), s, LANES), F32)],
        compiler_params=_cparams(("parallel", "parallel")),
        name="dilated",
    )(qa, ka, va, bias)


def _mla_kernel(q_ref, k_ref, v_ref, o_ref):
    v = v_ref[...]
    outs = []
    for hh in range(2):
        sc = lax.dot_general(q_ref[hh], k_ref[hh], (((1,), (1,)), ((), ())), preferred_element_type=F32)
        m = jnp.max(sc, axis=-1, keepdims=True)
        p = jnp.exp2(sc - m)
        l = jnp.sum(p, axis=-1, keepdims=True)
        outs.append(jnp.dot(p.astype(BF16), v, preferred_element_type=F32) / l)
    lo = lax.broadcasted_iota(jnp.int32, outs[0].shape, 1) < B_VDIM
    o_ref[...] = jnp.where(lo, outs[0], outs[1])


def _mla(qb, kb, vb):
    b, h, s, _ = qb.shape
    npair = h // 2
    tq = MLA_TQ
    return pl.pallas_call(
        _mla_kernel,
        grid=(b, npair, s // tq),
        in_specs=[pl.BlockSpec((None, 2, tq, HEAD_PAD), lambda bi, p, qi: (bi, p, qi, 0)),
                  pl.BlockSpec((None, 2, s, HEAD_PAD), lambda bi, p, qi: (bi, p, 0, 0)),
                  pl.BlockSpec((None, None, s, LANES), lambda bi, p, qi: (bi, p, 0, 0))],
        out_specs=pl.BlockSpec((None, tq, LANES), lambda bi, p, qi: (bi, qi, p)),
        out_shape=jax.ShapeDtypeStruct((b, s, B_WIDTH), F32),
        compiler_params=_cparams(("parallel", "parallel", "arbitrary")),
        name="mla",
    )(qb, kb, vb)


def _outproj_kernel(x_ref, a_ref, b_ref, ag_ref, bg_ref, wo_ref, g1_ref, n2_ref, sc_ref, sh_ref, g2_ref,
                    wrt_ref, ws1_ref, ws3_ref, ws2_ref, x1s_ref, h2p_ref, scores_ref):
    an = _rms(a_ref[...], ag_ref[...])
    bn = _rms(b_ref[...], bg_ref[...])
    mix = jnp.concatenate([an, bn], axis=-1).astype(BF16)
    x1 = x_ref[...] + g1_ref[...] * jnp.dot(mix, wo_ref[...], preferred_element_type=F32)
    h2 = _rms(x1, n2_ref[...]) * (1.0 + sc_ref[...]) + sh_ref[...]
    h2p_ref[...] = _pack_rows(h2)
    logits = lax.dot_general(wrt_ref[...], h2, (((1,), (1,)), ((), ())), precision=HIGHEST,
                             preferred_element_type=F32)
    scores_ref[...] = jax.nn.sigmoid(logits)
    h2b = h2.astype(BF16)
    hid = _silu(jnp.dot(h2b, ws1_ref[...], preferred_element_type=F32)) * jnp.dot(
        h2b, ws3_ref[...], preferred_element_type=F32)
    shared = jnp.dot(hid.astype(BF16), ws2_ref[...], preferred_element_type=F32)
    x1s_ref[...] = x1 + g2_ref[...] * shared


def _outproj(x2, a_out, b_out, a_out_g, b_out_g, w_o, mod3, norm2_g, w_router_t, ws1, ws3, ws2, s):
    n, d = x2.shape
    tm = TM_OUTPROJ
    tpb = s // tm
    row = lambda i: (i, 0)
    const = lambda i: (0, 0)
    modspec = lambda j: pl.BlockSpec((None, 1, d), lambda i: ((i // tpb) * 6 + j, 0, 0))
    return pl.pallas_call(
        _outproj_kernel,
        grid=(n // tm,),
        in_specs=[pl.BlockSpec((tm, d), row),
                  pl.BlockSpec((tm, A_WIDTH), row),
                  pl.BlockSpec((tm, B_WIDTH), row),
                  pl.BlockSpec((1, A_WIDTH), const),
                  pl.BlockSpec((1, B_WIDTH), const),
                  pl.BlockSpec(w_o.shape, const),
                  modspec(2),
                  pl.BlockSpec((1, d), const),
                  modspec(4), modspec(3), modspec(5),
                  pl.BlockSpec(w_router_t.shape, const),
                  pl.BlockSpec(ws1.shape, const),
                  pl.BlockSpec(ws3.shape, const),
                  pl.BlockSpec(ws2.shape, const)],
        out_specs=[pl.BlockSpec((tm, d), row),
                   pl.BlockSpec((tm, d // 2), row),
                   pl.BlockSpec((N_EXPERTS, tm), lambda i: (0, i))],
        out_shape=[jax.ShapeDtypeStruct((n, d), F32),
                   jax.ShapeDtypeStruct((n, d // 2), U32),
                   jax.ShapeDtypeStruct((N_EXPERTS, n), F32)],
        compiler_params=_cparams(("parallel",)),
        name="outproj",
    )(x2, a_out, b_out, a_out_g, b_out_g, w_o, mod3, norm2_g, mod3, mod3, mod3, w_router_t, ws1, ws3, ws2)


def _slab_order(v):
    return v.reshape((N_GROUPS, GROUP_SIZE) + v.shape[1:]).swapaxes(0, 1).reshape(v.shape)


def _expert_order(v):
    return v.reshape((GROUP_SIZE, N_GROUPS) + v.shape[1:]).swapaxes(0, 1).reshape(v.shape)


def _sublane_all(x, op):
    for sh in (4, 2, 1):
        x = op(x, pltpu.roll(x, sh, 0))
    return x


def _route_kernel(st_ref, bias_ref, eidx_ref, gate_ref, cnt_ref):
    nsl = GROUP_SIZE
    t = st_ref.shape[1]
    sub = lax.broadcasted_iota(I32, (SUBLANES, t), 0)
    ninf = -jnp.inf
    big = jnp.int32(1 << 30)
    sc = [st_ref[j * SUBLANES:(j + 1) * SUBLANES, :] for j in range(nsl)]
    sel = [sc[j] + bias_ref[j * SUBLANES:(j + 1) * SUBLANES, :] for j in range(nsl)]
    eid = [sub * GROUP_SIZE + j for j in range(nsl)]

    m1 = sel[0]
    m2 = jnp.full_like(m1, ninf)
    for j in range(1, nsl):
        m2 = jnp.maximum(m2, jnp.minimum(m1, sel[j]))
        m1 = jnp.maximum(m1, sel[j])
    gs = m1 + m2

    rank = jnp.zeros((SUBLANES, t), I32)
    for sh in range(1, N_GROUPS):
        other = pltpu.roll(gs, sh, 0)
        ahead = (other > gs) | ((other == gs) & (sub >= sh))
        rank = rank + ahead.astype(I32)
    gmask = rank < TOPK_GROUPS

    msel = [jnp.where(gmask, sel[j], ninf) for j in range(nsl)]
    hits = [jnp.zeros((SUBLANES, t), I32) for _ in range(nsl)]
    eidx = jnp.zeros((TOP_K, t), I32)
    gates = jnp.zeros((TOP_K, t), F32)
    for k in range(TOP_K):
        mx = msel[0]
        for j in range(1, nsl):
            mx = jnp.maximum(mx, msel[j])
        mx = _sublane_all(mx, jnp.maximum)
        cand = jnp.where(msel[0] == mx, eid[0], big)
        for j in range(1, nsl):
            cand = jnp.minimum(cand, jnp.where(msel[j] == mx, eid[j], big))
        idx = _sublane_all(cand, jnp.minimum)
        gk = jnp.zeros((SUBLANES, t), F32)
        for j in range(nsl):
            hit = eid[j] == idx
            gk = gk + jnp.where(hit, sc[j], 0.0)
            msel[j] = jnp.where(hit, ninf, msel[j])
            hits[j] = hits[j] + hit.astype(I32)
        gk = _sublane_all(gk, jnp.add)
        eidx = jnp.where(sub == k, idx, eidx)
        gates = jnp.where(sub == k, gk, gates)
    gsum = _sublane_all(gates, jnp.add)
    eidx_ref[...] = eidx
    gate_ref[...] = gates / gsum * ROUTED_SCALE

    @pl.when(pl.program_id(0) == 0)
    def _():
        cnt_ref[...] = jnp.zeros_like(cnt_ref)

    for j in range(nsl):
        cnt_ref[j * SUBLANES:(j + 1) * SUBLANES, :] += jnp.sum(hits[j].astype(F32), axis=1,
                                                               keepdims=True).astype(I32)


def _route(scores_t, e_bias_slab):
    e, n = scores_t.shape
    t = ROUTE_T
    return pl.pallas_call(
        _route_kernel,
        grid=(n // t,),
        in_specs=[pl.BlockSpec((e, t), lambda i: (0, i)),
                  pl.BlockSpec((e, 1), lambda i: (0, 0))],
        out_specs=[pl.BlockSpec((TOP_K, t), lambda i: (0, i)),
                   pl.BlockSpec((TOP_K, t), lambda i: (0, i)),
                   pl.BlockSpec((e, 1), lambda i: (0, 0))],
        out_shape=[jax.ShapeDtypeStruct((TOP_K, n), I32),
                   jax.ShapeDtypeStruct((TOP_K, n), F32),
                   jax.ShapeDtypeStruct((e, 1), I32)],
        compiler_params=_cparams(("arbitrary",)),
        name="route",
    )(scores_t, e_bias_slab)


def _dest_kernel(eidx_ref, pstart_ref, dest_ref, carry_ref):
    @pl.when(pl.program_id(0) == 0)
    def _():
        carry_ref[...] = jnp.zeros_like(carry_ref)

    nsl = GROUP_SIZE
    t = eidx_ref.shape[1]
    sub = lax.broadcasted_iota(I32, (SUBLANES, t), 0)
    eid = [sub * GROUP_SIZE + j for j in range(nsl)]
    ek = [eidx_ref[k:k + 1, :] for k in range(TOP_K)]
    slabs = []
    for j in range(nsl):
        oh = jnp.zeros((SUBLANES, t), F32)
        for k in range(TOP_K):
            oh = oh + (eid[j] == ek[k]).astype(F32)
        slabs.append(oh)
    onehot = jnp.concatenate(slabs, axis=0)
    row = lax.broadcasted_iota(I32, (t, t), 0)
    col = lax.broadcasted_iota(I32, (t, t), 1)
    upper = (row < col).astype(BF16)
    before = jnp.dot(onehot.astype(BF16), upper, preferred_element_type=F32)
    base = before + carry_ref[...] + pstart_ref[...]
    dest = jnp.zeros((TOP_K, t), I32)
    for k in range(TOP_K):
        acc = jnp.zeros((SUBLANES, t), F32)
        for j in range(nsl):
            acc = acc + jnp.where(eid[j] == ek[k], base[j * SUBLANES:(j + 1) * SUBLANES, :], 0.0)
        dk = _sublane_all(acc, jnp.add).astype(I32)
        dest = jnp.where(sub == k, dk, dest)
    dest_ref[...] = dest
    carry_ref[...] += jnp.sum(onehot, axis=1, keepdims=True)


def _dest(eidx_t, pstart_slab):
    n = eidx_t.shape[1]
    t = ROUTE_T
    return pl.pallas_call(
        _dest_kernel,
        grid=(n // t,),
        in_specs=[pl.BlockSpec((TOP_K, t), lambda i: (0, i)),
                  pl.BlockSpec((N_EXPERTS, 1), lambda i: (0, 0))],
        out_specs=pl.BlockSpec((TOP_K, t), lambda i: (0, i)),
        out_shape=jax.ShapeDtypeStruct((TOP_K, n), I32),
        scratch_shapes=[pltpu.VMEM((N_EXPERTS, 1), F32)],
        compiler_params=_cparams(("arbitrary",)),
        name="dest",
    )(eidx_t, pstart_slab)


def _dispatch_kernel(dest_ref, zero_ref, h_ref, xg_ref, zbuf, sem, zsem):
    t = h_ref.shape[0]
    nblk = zero_ref.shape[0]

    @pl.when(pl.program_id(0) == 0)
    def _():
        zbuf[...] = jnp.zeros_like(zbuf)

        def zero_copy(i):
            return pltpu.make_async_copy(zbuf, xg_ref.at[pl.ds(pl.multiple_of(i * MOE_BLK, MOE_BLK), MOE_BLK)],
                                         zsem)

        def zstart(i, c):
            @pl.when(zero_ref[i] != 0)
            def _():
                zero_copy(i).start()
            return c

        def zwait(i, c):
            @pl.when(zero_ref[i] != 0)
            def _():
                zero_copy(i).wait()
            return c

        lax.fori_loop(0, nblk, zstart, 0)
        lax.fori_loop(0, nblk, zwait, 0)

    def row_copy(tok, k):
        return pltpu.make_async_copy(h_ref.at[pl.ds(tok, 1)], xg_ref.at[pl.ds(dest_ref[k, tok], 1)], sem)

    def issue(tok, c):
        for k in range(TOP_K):
            row_copy(tok, k).start()
        return c

    def drain(tok, c):
        for k in range(TOP_K):
            row_copy(tok, k).wait()
        return c

    lax.fori_loop(0, t, issue, 0)
    lax.fori_loop(0, t, drain, 0)


def _dispatch(dest_t, zero_blk, h2p, rows_total):
    n, dh = h2p.shape
    t = DISP_T
    return pl.pallas_call(
        _dispatch_kernel,
        grid=(n // t,),
        in_specs=[pl.BlockSpec((TOP_K, t), lambda i: (0, i), memory_space=pltpu.SMEM),
                  pl.BlockSpec(memory_space=pltpu.SMEM),
                  pl.BlockSpec((t, dh), lambda i: (i, 0))],
        out_specs=pl.BlockSpec(memory_space=pl.ANY),
        out_shape=jax.ShapeDtypeStruct((rows_total, dh), U32),
        scratch_shapes=[pltpu.VMEM((MOE_BLK, dh), U32), pltpu.SemaphoreType.DMA, pltpu.SemaphoreType.DMA],
        compiler_params=_cparams(("arbitrary",)),
        name="dispatch",
    )(dest_t, zero_blk, h2p)


def _expert_kernel(blk_e_ref, nused_ref, x_ref, w1_ref, w3_ref, w2_ref, y_ref, w1b, w3b, w2b):
    i = pl.program_id(0)
    used = i < nused_ref[0]

    @pl.when(used)
    def _():
        e = blk_e_ref[i]
        e_prev = blk_e_ref[jnp.maximum(i - 1, 0)]

        @pl.when((i == 0) | (e != e_prev))
        def _():
            w1b[...] = w1_ref[...].astype(BF16)
            w3b[...] = w3_ref[...].astype(BF16)
            w2b[...] = w2_ref[...].astype(BF16)

        lo, hi = _unpack_rows(x_ref[...])
        xb = jnp.concatenate([lo, hi], axis=1).astype(BF16)
        hid = _silu(jnp.dot(xb, w1b[...], preferred_element_type=F32)) * jnp.dot(
            xb, w3b[...], preferred_element_type=F32)
        y_ref[...] = _pack_rows(jnp.dot(hid.astype(BF16), w2b[...], preferred_element_type=F32))

    @pl.when(jnp.logical_not(used))
    def _():
        y_ref[...] = jnp.zeros_like(y_ref)


def _experts(blk_e, nused, xg, w1, w3, w2):
    rows, dh = xg.shape
    d, f = w1.shape[-2:]
    nblk = rows // MOE_BLK
    grid_spec = pltpu.PrefetchScalarGridSpec(
        num_scalar_prefetch=2,
        grid=(nblk,),
        in_specs=[pl.BlockSpec((MOE_BLK, dh), lambda i, be, nu: (i, 0)),
                  pl.BlockSpec((None, d, f), lambda i, be, nu: (be[i], 0, 0)),
                  pl.BlockSpec((None, d, f), lambda i, be, nu: (be[i], 0, 0)),
                  pl.BlockSpec((None, f, d), lambda i, be, nu: (be[i], 0, 0))],
        out_specs=pl.BlockSpec((MOE_BLK, dh), lambda i, be, nu: (i, 0)),
        scratch_shapes=[pltpu.VMEM((d, f), BF16), pltpu.VMEM((d, f), BF16), pltpu.VMEM((f, d), BF16)],
    )
    return pl.pallas_call(
        _expert_kernel,
        grid_spec=grid_spec,
        out_shape=jax.ShapeDtypeStruct((rows, dh), U32),
        compiler_params=_cparams(("arbitrary",)),
        name="experts",
    )(blk_e, nused, xg, w1, w3, w2)


def _combine_kernel(dest_ref, dnext_ref, gate_ref, x1s_ref, g2_ref, fg_ref, y_ref, out_ref, buf, sems):
    t = x1s_ref.shape[0]
    i = pl.program_id(0)
    nsteps = pl.num_programs(0)
    slot = i % 2

    def row_copy(d_ref, sl, tok, k):
        return pltpu.make_async_copy(y_ref.at[pl.ds(d_ref[k, tok], 1)],
                                     buf.at[sl, pl.ds(k * t + tok, 1)], sems.at[sl])

    def issue(d_ref, sl):
        def body(tok, c):
            for k in range(TOP_K):
                row_copy(d_ref, sl, tok, k).start()
            return c
        lax.fori_loop(0, t, body, 0)

    @pl.when(i == 0)
    def _():
        issue(dest_ref, 0)

    @pl.when(i + 1 < nsteps)
    def _():
        issue(dnext_ref, 1 - slot)

    def drain(tok, c):
        for k in range(TOP_K):
            row_copy(dest_ref, slot, tok, k).wait()
        return c

    lax.fori_loop(0, t, drain, 0)
    gates = gate_ref[...]
    half = x1s_ref.shape[1] // 2
    r_lo = jnp.zeros((t, half), F32)
    r_hi = jnp.zeros((t, half), F32)
    for k in range(TOP_K):
        lo, hi = _unpack_rows(buf[slot, k * t:(k + 1) * t, :])
        gk = gates[:, k:k + 1]
        r_lo = r_lo + gk * lo
        r_hi = r_hi + gk * hi
    routed = jnp.concatenate([r_lo, r_hi], axis=1)
    x2 = x1s_ref[...] + g2_ref[...] * routed
    out_ref[...] = _rms(x2, fg_ref[...])


def _combine(dest_t, gates, x1s, mod3, final_g, y, s):
    n, d = x1s.shape
    t = COMB_T
    tpb = s // t
    last = n // t - 1
    return pl.pallas_call(
        _combine_kernel,
        grid=(n // t,),
        in_specs=[pl.BlockSpec((TOP_K, t), lambda i: (0, i), memory_space=pltpu.SMEM),
                  pl.BlockSpec((TOP_K, t), lambda i: (0, jnp.minimum(i + 1, last)), memory_space=pltpu.SMEM),
                  pl.BlockSpec((t, TOP_K), lambda i: (i, 0)),
                  pl.BlockSpec((t, d), lambda i: (i, 0)),
                  pl.BlockSpec((None, 1, d), lambda i: ((i // tpb) * 6 + 5, 0, 0)),
                  pl.BlockSpec((1, d), lambda i: (0, 0)),
                  pl.BlockSpec(memory_space=pl.ANY)],
        out_specs=pl.BlockSpec((t, d), lambda i: (i, 0)),
        out_shape=jax.ShapeDtypeStruct((n, d), F32),
        scratch_shapes=[pltpu.VMEM((2, TOP_K * t, d // 2), U32), pltpu.SemaphoreType.DMA((2,))],
        compiler_params=_cparams(("arbitrary",)),
        name="combine",
    )(dest_t, dest_t, gates, x1s, mod3, final_g, y)


def _place_heads(w, per_head, keep):
    r = w.shape[0]
    w = w.reshape(r, B_HEADS, per_head)[:, :, :keep]
    return jnp.pad(w, ((0, 0), (0, 0), (0, HEAD_PAD - keep))).reshape(r, B_HEADS * HEAD_PAD)


def _rope_tables(s):
    inv = ROPE_THETA ** (-jnp.arange(0, B_ROPE, 2, dtype=jnp.float32) / B_ROPE)
    ang = jnp.arange(s, dtype=jnp.float32)[:, None] * inv[None, :]
    cos, sin = jnp.cos(ang), jnp.sin(ang)
    ones = jnp.ones((s, B_NOPE), F32)
    zeros = jnp.zeros((s, B_NOPE), F32)
    tail1 = jnp.ones((s, HEAD_PAD - B_NOPE - B_ROPE), F32)
    tail0 = jnp.zeros((s, HEAD_PAD - B_NOPE - B_ROPE), F32)
    return (jnp.concatenate([ones, cos, cos, tail1], axis=1),
            jnp.concatenate([zeros, -sin, sin, tail0], axis=1))


def kernel(x, c, w_ada, b_ada, norm1_g, w_in, q_norm_g, w_uq, kv_norm_g, w_ukv, rel_table, a_out_g, b_out_g,
           w_o, norm2_g, w_router, e_bias, w1, w3, w2, ws1, ws3, ws2, final_g):
    b, s, d = x.shape
    n = b * s
    assert w_ada.shape[0] == 1, "single layer"
    x2 = x.reshape(n, d)

    mod = _ada(c, w_ada[0], b_ada[0])
    mod3 = mod.reshape(b * 6, 1, d)
    bias = _bias_tiles(rel_table)

    wi = w_in[0]
    c_kpe = 3 * A_WIDTH + Q_LORA + KV_LORA
    kpe_cols = jnp.pad(wi[:, c_kpe:], ((0, 0), (B_NOPE, HEAD_PAD - B_NOPE - B_ROPE)))
    w_in_ext = jnp.concatenate([wi[:, :c_kpe], kpe_cols], axis=1).astype(BF16)
    w_uq_p = _place_heads(w_uq[0], B_NOPE + B_ROPE, B_NOPE + B_ROPE).astype(BF16)
    w_uk_p = _place_heads(w_ukv[0], B_NOPE + B_VDIM, B_NOPE).astype(BF16)
    w_v = w_ukv[0].reshape(KV_LORA, B_HEADS, B_NOPE + B_VDIM)[:, :, B_NOPE:].reshape(KV_LORA, B_WIDTH).astype(BF16)
    cos_t, sin_t = _rope_tables(s)

    qa, ka, va, qb, kb, vb = _inproj(x2, mod3, norm1_g, w_in_ext, q_norm_g, w_uq_p, kv_norm_g, w_uk_p, w_v,
                                     cos_t, sin_t, b, s)
    a_out = _dilated(qa.reshape(b, s, A_WIDTH), ka.reshape(b, s, A_WIDTH), va.reshape(b, s, A_WIDTH), bias)
    b_out = _mla(qb, kb, vb)

    w_router_t = _slab_order(w_router[0].T)
    x1s, h2p, scores_t = _outproj(x2, a_out.reshape(n, A_WIDTH), b_out.reshape(n, B_WIDTH), a_out_g, b_out_g,
                                  w_o[0].astype(BF16), mod3, norm2_g, w_router_t,
                                  ws1[0].astype(BF16), ws3[0].astype(BF16), ws2[0].astype(BF16), s)

    eidx_t, gates_t, counts_slab = _route(scores_t, _slab_order(e_bias.reshape(N_EXPERTS, 1)))

    counts = _expert_order(counts_slab)[:, 0]
    padded = (counts + MOE_BLK - 1) // MOE_BLK * MOE_BLK
    pends = jnp.cumsum(padded)
    pstart = pends - padded
    nk = n * TOP_K
    nblk = -(-(nk + N_EXPERTS * (MOE_BLK - 1)) // MOE_BLK)
    rows_total = nblk * MOE_BLK
    blk_row = jnp.arange(nblk, dtype=I32) * MOE_BLK
    blk_e = jnp.minimum(jnp.sum((pends[None, :] <= blk_row[:, None]).astype(I32), axis=1), N_EXPERTS - 1)
    seg_end = (pstart + counts)[blk_e]
    zero_blk = ((blk_row >= pends[-1]) | (seg_end < blk_row + MOE_BLK)).astype(I32)
    nused = (pends[-1] // MOE_BLK).astype(I32).reshape(1)

    dest_t = _dest(eidx_t, _slab_order(pstart.astype(F32).reshape(N_EXPERTS, 1)))
    xg = _dispatch(dest_t, zero_blk, h2p, rows_total)
    y = _experts(blk_e.astype(I32), nused, xg, w1[0], w3[0], w2[0])
    out = _combine(dest_t, gates_t.T, x1s, mod3, final_g.reshape(1, d), y, s)
    return out.reshape(b, s, d)
```

```python
import functools
import math

import jax
import jax.numpy as jnp
from jax import lax
from jax.experimental import pallas as pl
from jax.experimental.pallas import tpu as pltpu

F32 = jnp.float32
BF16 = jnp.bfloat16
U32 = jnp.uint32
I32 = jnp.int32
HIGHEST = lax.Precision.HIGHEST

D_MODEL = 1024
A_HEADS = 8
A_HEAD_DIM = 64
A_WIDTH = A_HEADS * A_HEAD_DIM
A_PATTERNS = ((128, 1), (512, 4), (2048, 16))
A_RADIUS = 64
REL_BUCKETS = 32
REL_MAX_DIST = 1024
B_HEADS = 8
B_NOPE = 64
B_ROPE = 32
B_VDIM = 64
B_WIDTH = B_HEADS * B_VDIM
Q_LORA = 384
KV_LORA = 256
ROPE_THETA = 10000.0
N_EXPERTS = 256
TOP_K = 8
N_GROUPS = 8
GROUP_SIZE = N_EXPERTS // N_GROUPS
TOPK_GROUPS = 4
EXPERT_FF = 256
SHARED_FF = 256
ROUTED_SCALE = 2.5
EPS = 1e-6
NEG_INF = -1e30
LOG2E = math.log2(math.e)

LANES = 128
SUBLANES = 8
HEAD_PAD = 128
IN_COLS_EXT = 3 * A_WIDTH + Q_LORA + KV_LORA + HEAD_PAD

TM_INPROJ = 512
TM_OUTPROJ = 256
DIL_QB = 128
DIL_KW = DIL_QB + 2 * A_RADIUS
DIL_UNROLL = 4
MLA_TQ = 256
ROUTE_T = 256
MOE_BLK = 256
DISP_T = 256
COMB_T = 128
VMEM_LIMIT = 56 * 1024 * 1024


def _cparams(sem):
    return pltpu.CompilerParams(dimension_semantics=sem, vmem_limit_bytes=VMEM_LIMIT)


def _rms(x, g):
    return x * lax.rsqrt(jnp.mean(x * x, axis=-1, keepdims=True) + EPS) * g


def _silu(x):
    return x * jax.nn.sigmoid(x)


def _pack_rows(x):
    half = x.shape[1] // 2
    bits = lax.bitcast_convert_type(x.astype(BF16).astype(F32), U32)
    return (bits[:, :half] >> 16) | bits[:, half:]


def _unpack_rows(w):
    lo = lax.bitcast_convert_type(w << 16, F32)
    hi = lax.bitcast_convert_type(w & jnp.uint32(0xFFFF0000), F32)
    return lo, hi


def _ada_kernel(c_ref, w_ref, b_ref, o_ref):
    o_ref[...] = jnp.dot(_silu(c_ref[...]), w_ref[...], precision=HIGHEST,
                         preferred_element_type=F32) + b_ref[...]


def _ada(c, w_ada, b_ada):
    b, d = c.shape
    n6 = w_ada.shape[1] // d
    return pl.pallas_call(
        _ada_kernel,
        grid=(n6,),
        in_specs=[pl.BlockSpec((b, d), lambda j: (0, 0)),
                  pl.BlockSpec((d, d), lambda j: (0, j)),
                  pl.BlockSpec((1, d), lambda j: (0, j))],
        out_specs=pl.BlockSpec((b, d), lambda j: (0, j)),
        out_shape=jax.ShapeDtypeStruct((b, n6 * d), F32),
        compiler_params=_cparams(("parallel",)),
        name="ada",
    )(c, w_ada, b_ada.reshape(1, -1))


def _t5_bucket(rel):
    half = REL_BUCKETS // 2
    max_exact = half // 2
    ret = jnp.where(rel > 0, half, 0)
    n = jnp.abs(rel)
    nf = jnp.maximum(n, 1).astype(jnp.float32)
    large = max_exact + (jnp.log(nf / max_exact) / math.log(REL_MAX_DIST / max_exact)
                         * (half - max_exact)).astype(jnp.int32)
    large = jnp.minimum(large, half - 1)
    return ret + jnp.where(n < max_exact, n, large)


DIL_SHIFTS = (A_RADIUS, 0, -A_RADIUS)


def _bucket_tiles():
    qi = jnp.arange(DIL_QB, dtype=jnp.int32)[:, None]
    ki = jnp.arange(DIL_KW, dtype=jnp.int32)[None, :]
    tiles = []
    for _, dilation in A_PATTERNS:
        for shift in DIL_SHIFTS:
            off = ki + shift - A_RADIUS - qi
            bkt = _t5_bucket(off * dilation)
            tiles.append(jnp.where(jnp.abs(off) <= A_RADIUS, bkt, -1))
    return jnp.stack(tiles, axis=0)


def _bias_kernel(tab_ref, bkt_ref, o_ref):
    bkt = bkt_ref[0]
    for h in range(A_HEADS):
        acc = jnp.full(bkt.shape, NEG_INF, F32)
        for b in range(REL_BUCKETS):
            acc = jnp.where(bkt == b, tab_ref[b, h] * LOG2E, acc)
        o_ref[0, h] = acc


def _bias_tiles(rel_table):
    bkt = _bucket_tiles()
    nt = bkt.shape[0]
    return pl.pallas_call(
        _bias_kernel,
        grid=(nt,),
        in_specs=[pl.BlockSpec(memory_space=pltpu.SMEM),
                  pl.BlockSpec((1, DIL_QB, DIL_KW), lambda t: (t, 0, 0))],
        out_specs=pl.BlockSpec((1, A_HEADS, DIL_QB, DIL_KW), lambda t: (t, 0, 0, 0)),
        out_shape=jax.ShapeDtypeStruct((nt, A_HEADS, DIL_QB, DIL_KW), F32),
        compiler_params=_cparams(("parallel",)),
        name="bias",
    )(rel_table, bkt)


def _rope(x, cos, sin, lane_lt_mid):
    half = B_ROPE // 2
    rot = jnp.where(lane_lt_mid, pltpu.roll(x, HEAD_PAD - half, 1), pltpu.roll(x, half, 1))
    return x * cos + rot * sin


def _inproj_kernel(x_ref, sc_ref, sh_ref, g1_ref, win_ref, qg_ref, wuq_ref, kvg_ref, wuk_ref, wv_ref,
                   cos_ref, sin_ref, qa_ref, ka_ref, va_ref, qb_ref, kb_ref, vb_ref):
    x = x_ref[...]
    h = _rms(x, g1_ref[...]) * (1.0 + sc_ref[...]) + sh_ref[...]
    proj = jnp.dot(h.astype(BF16), win_ref[...], preferred_element_type=F32)
    aw = A_WIDTH
    qa_ref[...] = proj[:, 0:aw] * (LOG2E / math.sqrt(A_HEAD_DIM))
    ka_ref[...] = proj[:, aw:2 * aw]
    va_ref[...] = proj[:, 2 * aw:3 * aw]
    c0 = 3 * aw
    q_lat = proj[:, c0:c0 + Q_LORA]
    kv_lat = proj[:, c0 + Q_LORA:c0 + Q_LORA + KV_LORA]
    kpe = proj[:, c0 + Q_LORA + KV_LORA:]
    qn = _rms(q_lat, qg_ref[...]).astype(BF16)
    kvn = _rms(kv_lat, kvg_ref[...]).astype(BF16)
    qm = jnp.dot(qn, wuq_ref[...], preferred_element_type=F32)
    kn = jnp.dot(kvn, wuk_ref[...], preferred_element_type=F32)
    vv = jnp.dot(kvn, wv_ref[...], preferred_element_type=F32)
    cos = cos_ref[...]
    sin = sin_ref[...]
    lane = lax.broadcasted_iota(jnp.int32, cos.shape, 1)
    lt_mid = lane < (B_NOPE + B_ROPE // 2)
    qscale = LOG2E / math.sqrt(B_NOPE + B_ROPE)
    for hd in range(B_HEADS):
        sl = slice(hd * HEAD_PAD, (hd + 1) * HEAD_PAD)
        qb_ref[hd] = (_rope(qm[:, sl], cos, sin, lt_mid) * qscale).astype(BF16)
        kb_ref[hd] = _rope(kn[:, sl] + kpe, cos, sin, lt_mid).astype(BF16)
    for p in range(B_HEADS // 2):
        vb_ref[p] = vv[:, p * LANES:(p + 1) * LANES].astype(BF16)


def _inproj(x2, mod3, norm1_g, w_in_ext, q_norm_g, w_uq_p, kv_norm_g, w_uk_p, w_v, cos_t, sin_t, b, s):
    n, d = x2.shape
    tm = TM_INPROJ
    tpb = s // tm
    row = lambda i: (i, 0)
    const = lambda i: (0, 0)
    hm = lambda i: (i // tpb, 0, i % tpb, 0)
    return pl.pallas_call(
        _inproj_kernel,
        grid=(n // tm,),
        in_specs=[pl.BlockSpec((tm, d), row),
                  pl.BlockSpec((None, 1, d), lambda i: ((i // tpb) * 6 + 1, 0, 0)),
                  pl.BlockSpec((None, 1, d), lambda i: ((i // tpb) * 6 + 0, 0, 0)),
                  pl.BlockSpec((1, d), const),
                  pl.BlockSpec(w_in_ext.shape, const),
                  pl.BlockSpec((1, Q_LORA), const),
                  pl.BlockSpec(w_uq_p.shape, const),
                  pl.BlockSpec((1, KV_LORA), const),
                  pl.BlockSpec(w_uk_p.shape, const),
                  pl.BlockSpec(w_v.shape, const),
                  pl.BlockSpec((tm, HEAD_PAD), lambda i: (i % tpb, 0)),
                  pl.BlockSpec((tm, HEAD_PAD), lambda i: (i % tpb, 0))],
        out_specs=[pl.BlockSpec((tm, A_WIDTH), row),
                   pl.BlockSpec((tm, A_WIDTH), row),
                   pl.BlockSpec((tm, A_WIDTH), row),
                   pl.BlockSpec((None, B_HEADS, tm, HEAD_PAD), hm),
                   pl.BlockSpec((None, B_HEADS, tm, HEAD_PAD), hm),
                   pl.BlockSpec((None, B_HEADS // 2, tm, LANES), hm)],
        out_shape=[jax.ShapeDtypeStruct((n, A_WIDTH), F32),
                   jax.ShapeDtypeStruct((n, A_WIDTH), F32),
                   jax.ShapeDtypeStruct((n, A_WIDTH), F32),
                   jax.ShapeDtypeStruct((b, B_HEADS, s, HEAD_PAD), BF16),
                   jax.ShapeDtypeStruct((b, B_HEADS, s, HEAD_PAD), BF16),
                   jax.ShapeDtypeStruct((b, B_HEADS // 2, s, LANES), BF16)],
        compiler_params=_cparams(("parallel",)),
        name="inproj",
    )(x2, mod3, mod3, norm1_g, w_in_ext, q_norm_g, w_uq_p, kv_norm_g, w_uk_p, w_v, cos_t, sin_t)


def _dil_block(q_ref, k_ref, v_ref, bias_ref, o_scr, l_scr, pi, dil, nblk, job):
    sub_len = nblk * DIL_QB
    r = job // nblk
    bi = job % nblk
    q0 = bi * DIL_QB
    ws = jnp.clip(q0 - A_RADIUS, 0, sub_len - DIL_KW)
    var = jnp.where(bi == 0, 0, jnp.where(bi == nblk - 1, 2, 1))
    if dil == 1:
        qsl = pl.ds(pl.multiple_of(q0, DIL_QB), DIL_QB)
        ksl = pl.ds(pl.multiple_of(ws, A_RADIUS), DIL_KW)
    else:
        qsl = pl.ds(r + dil * q0, DIL_QB, stride=dil)
        ksl = pl.ds(r + dil * ws, DIL_KW, stride=dil)
    q = q_ref[qsl, :]
    kw = k_ref[ksl, :].astype(BF16)
    vw = v_ref[ksl, :].astype(BF16)
    lo = lax.broadcasted_iota(jnp.int32, q.shape, 1) < A_HEAD_DIM
    outs, lses = [], []
    for hh in range(2):
        qm = jnp.where(lo if hh == 0 else jnp.logical_not(lo), q, 0.0).astype(BF16)
        sc = lax.dot_general(qm, kw, (((1,), (1,)), ((), ())), preferred_element_type=F32)
        sc = sc + bias_ref[pi * 3 + var, hh]
        m = jnp.max(sc, axis=-1, keepdims=True)
        p = jnp.exp2(sc - m)
        l = jnp.sum(p, axis=-1, keepdims=True)
        o = jnp.dot(p.astype(BF16), vw, preferred_element_type=F32)
        outs.append(o / l)
        lses.append(m + jnp.log2(l))
    o_scr[pi, qsl, :] = jnp.where(lo, outs[0], outs[1])
    l_scr[pi, qsl, :] = jnp.where(lo, lses[0], lses[1])


def _dilated_kernel(q_ref, k_ref, v_ref, bias_ref, out_ref, o_scr, l_scr):
    s = q_ref.shape[0]
    njobs = s // DIL_QB
    for pi, (_, dil) in enumerate(A_PATTERNS):
        blk = functools.partial(_dil_block, q_ref, k_ref, v_ref, bias_ref, o_scr, l_scr, pi, dil,
                                s // dil // DIL_QB)

        def group(g, c, blk=blk):
            for u in range(DIL_UNROLL):
                blk(g * DIL_UNROLL + u)
            return c

        lax.fori_loop(0, njobs // DIL_UNROLL, group, 0)

    chunk = 512

    def comb(i, c):
        rows = pl.ds(pl.multiple_of(i * chunk, chunk), chunk)
        l0, l1, l2 = l_scr[0, rows, :], l_scr[1, rows, :], l_scr[2, rows, :]
        mx = jnp.maximum(jnp.maximum(l0, l1), l2)
        e0, e1, e2 = jnp.exp2(l0 - mx), jnp.exp2(l1 - mx), jnp.exp2(l2 - mx)
        num = e0 * o_scr[0, rows, :] + e1 * o_scr[1, rows, :] + e2 * o_scr[2, rows, :]
        out_ref[rows, :] = num / (e0 + e1 + e2)
        return c

    lax.fori_loop(0, s // chunk, comb, 0)


def _dilated(qa, ka, va, bias):
    b, s, _ = qa.shape
    npair = A_WIDTH // LANES
    assert (s // DIL_QB) % DIL_UNROLL == 0
    for _, dil in A_PATTERNS:
        assert (s // dil) % DIL_QB == 0 and s // dil >= DIL_KW
    blk = pl.BlockSpec((None, s, LANES), lambda bi, p: (bi, 0, p))
    return pl.pallas_call(
        _dilated_kernel,
        grid=(b, npair),
        in_specs=[blk, blk, blk,
                  pl.BlockSpec((bias.shape[0], 2, DIL_QB, DIL_KW), lambda bi, p: (0, p, 0, 0))],
        out_specs=blk,
        out_shape=jax.ShapeDtypeStruct((b, s, A_WIDTH), F32),
        scratch_shapes=[pltpu.VMEM((len(A_PATTERNS), s, LANES), F32),
                        pltpu.VMEM((len(A_PATTERNS), s, LANES), F32)],
        compiler_params=_cparams(("parallel", "parallel")),
        name="dilated",
    )(qa, ka, va, bias)


def _mla_kernel(q_ref, k_ref, v_ref, o_ref):
    v = v_ref[...]
    outs = []
    for hh in range(2):
        sc = lax.dot_general(q_ref[hh], k_ref[hh], (((1,), (1,)), ((), ())), preferred_element_type=F32)
        m = jnp.max(sc, axis=-1, keepdims=True)
        p = jnp.exp2(sc - m)
        l = jnp.sum(p, axis=-1, keepdims=True)
        outs.append(jnp.dot(p.astype(BF16), v, preferred_element_type=F32) / l)
    lo = lax.broadcasted_iota(jnp.int32, outs[0].shape, 1) < B_VDIM
    o_ref[...] = jnp.where(lo, outs[0], outs[1])


def _mla(qb, kb, vb):
    b, h, s, _ = qb.shape
    npair = h // 2
    tq = MLA_TQ
    return pl.pallas_call(
        _mla_kernel,
        grid=(b, npair, s // tq),
        in_specs=[pl.BlockSpec((None, 2, tq, HEAD_PAD), lambda bi, p, qi: (bi, p, qi, 0)),
                  pl.BlockSpec((None, 2, s, HEAD_PAD), lambda bi, p, qi: (bi, p, 0, 0)),
                  pl.BlockSpec((None, None, s, LANES), lambda bi, p, qi: (bi, p, 0, 0))],
        out_specs=pl.BlockSpec((None, tq, LANES), lambda bi, p, qi: (bi, qi, p)),
        out_shape=jax.ShapeDtypeStruct((b, s, B_WIDTH), F32),
        compiler_params=_cparams(("parallel", "parallel", "arbitrary")),
        name="mla",
    )(qb, kb, vb)


def _outproj_kernel(x_ref, a_ref, b_ref, ag_ref, bg_ref, wo_ref, g1_ref, n2_ref, sc_ref, sh_ref, g2_ref,
                    wrt_ref, ws1_ref, ws3_ref, ws2_ref, x1s_ref, h2p_ref, scores_ref):
    an = _rms(a_ref[...], ag_ref[...])
    bn = _rms(b_ref[...], bg_ref[...])
    mix = jnp.concatenate([an, bn], axis=-1).astype(BF16)
    x1 = x_ref[...] + g1_ref[...] * jnp.dot(mix, wo_ref[...], preferred_element_type=F32)
    h2 = _rms(x1, n2_ref[...]) * (1.0 + sc_ref[...]) + sh_ref[...]
    h2p_ref[...] = _pack_rows(h2)
    logits = lax.dot_general(wrt_ref[...], h2, (((1,), (1,)), ((), ())), precision=HIGHEST,
                             preferred_element_type=F32)
    scores_ref[...] = jax.nn.sigmoid(logits)
    h2b = h2.astype(BF16)
    hid = _silu(jnp.dot(h2b, ws1_ref[...], preferred_element_type=F32)) * jnp.dot(
        h2b, ws3_ref[...], preferred_element_type=F32)
    shared = jnp.dot(hid.astype(BF16), ws2_ref[...], preferred_element_type=F32)
    x1s_ref[...] = x1 + g2_ref[...] * shared


def _outproj(x2, a_out, b_out, a_out_g, b_out_g, w_o, mod3, norm2_g, w_router_t, ws1, ws3, ws2, s):
    n, d = x2.shape
    tm = TM_OUTPROJ
    tpb = s // tm
    row = lambda i: (i, 0)
    const = lambda i: (0, 0)
    modspec = lambda j: pl.BlockSpec((None, 1, d), lambda i: ((i // tpb) * 6 + j, 0, 0))
    return pl.pallas_call(
        _outproj_kernel,
        grid=(n // tm,),
        in_specs=[pl.BlockSpec((tm, d), row),
                  pl.BlockSpec((tm, A_WIDTH), row),
                  pl.BlockSpec((tm, B_WIDTH), row),
                  pl.BlockSpec((1, A_WIDTH), const),
                  pl.BlockSpec((1, B_WIDTH), const),
                  pl.BlockSpec(w_o.shape, const),
                  modspec(2),
                  pl.BlockSpec((1, d), const),
                  modspec(4), modspec(3), modspec(5),
                  pl.BlockSpec(w_router_t.shape, const),
                  pl.BlockSpec(ws1.shape, const),
                  pl.BlockSpec(ws3.shape, const),
                  pl.BlockSpec(ws2.shape, const)],
        out_specs=[pl.BlockSpec((tm, d), row),
                   pl.BlockSpec((tm, d // 2), row),
                   pl.BlockSpec((N_EXPERTS, tm), lambda i: (0, i))],
        out_shape=[jax.ShapeDtypeStruct((n, d), F32),
                   jax.ShapeDtypeStruct((n, d // 2), U32),
                   jax.ShapeDtypeStruct((N_EXPERTS, n), F32)],
        compiler_params=_cparams(("parallel",)),
        name="outproj",
    )(x2, a_out, b_out, a_out_g, b_out_g, w_o, mod3, norm2_g, mod3, mod3, mod3, w_router_t, ws1, ws3, ws2)


def _slab_order(v):
    return v.reshape((N_GROUPS, GROUP_SIZE) + v.shape[1:]).swapaxes(0, 1).reshape(v.shape)


def _expert_order(v):
    return v.reshape((GROUP_SIZE, N_GROUPS) + v.shape[1:]).swapaxes(0, 1).reshape(v.shape)


def _sublane_all(x, op):
    for sh in (4, 2, 1):
        x = op(x, pltpu.roll(x, sh, 0))
    return x


def _route_kernel(st_ref, bias_ref, eidx_ref, gate_ref, cnt_ref):
    nsl = GROUP_SIZE
    t = st_ref.shape[1]
    sub = lax.broadcasted_iota(I32, (SUBLANES, t), 0)
    ninf = -jnp.inf
    big = jnp.int32(1 << 30)
    sc = [st_ref[j * SUBLANES:(j + 1) * SUBLANES, :] for j in range(nsl)]
    sel = [sc[j] + bias_ref[j * SUBLANES:(j + 1) * SUBLANES, :] for j in range(nsl)]
    eid = [sub * GROUP_SIZE + j for j in range(nsl)]

    m1 = sel[0]
    m2 = jnp.full_like(m1, ninf)
    for j in range(1, nsl):
        m2 = jnp.maximum(m2, jnp.minimum(m1, sel[j]))
        m1 = jnp.maximum(m1, sel[j])
    gs = m1 + m2

    rank = jnp.zeros((SUBLANES, t), I32)
    for sh in range(1, N_GROUPS):
        other = pltpu.roll(gs, sh, 0)
        ahead = (other > gs) | ((other == gs) & (sub >= sh))
        rank = rank + ahead.astype(I32)
    gmask = rank < TOPK_GROUPS

    msel = [jnp.where(gmask, sel[j], ninf) for j in range(nsl)]
    hits = [jnp.zeros((SUBLANES, t), I32) for _ in range(nsl)]
    eidx = jnp.zeros((TOP_K, t), I32)
    gates = jnp.zeros((TOP_K, t), F32)
    for k in range(TOP_K):
        mx = msel[0]
        for j in range(1, nsl):
            mx = jnp.maximum(mx, msel[j])
        mx = _sublane_all(mx, jnp.maximum)
        cand = jnp.where(msel[0] == mx, eid[0], big)
        for j in range(1, nsl):
            cand = jnp.minimum(cand, jnp.where(msel[j] == mx, eid[j], big))
        idx = _sublane_all(cand, jnp.minimum)
        gk = jnp.zeros((SUBLANES, t), F32)
        for j in range(nsl):
            hit = eid[j] == idx
            gk = gk + jnp.where(hit, sc[j], 0.0)
            msel[j] = jnp.where(hit, ninf, msel[j])
            hits[j] = hits[j] + hit.astype(I32)
        gk = _sublane_all(gk, jnp.add)
        eidx = jnp.where(sub == k, idx, eidx)
        gates = jnp.where(sub == k, gk, gates)
    gsum = _sublane_all(gates, jnp.add)
    eidx_ref[...] = eidx
    gate_ref[...] = gates / gsum * ROUTED_SCALE

    @pl.when(pl.program_id(0) == 0)
    def _():
        cnt_ref[...] = jnp.zeros_like(cnt_ref)

    for j in range(nsl):
        cnt_ref[j * SUBLANES:(j + 1) * SUBLANES, :] += jnp.sum(hits[j].astype(F32), axis=1,
                                                               keepdims=True).astype(I32)


def _route(scores_t, e_bias_slab):
    e, n = scores_t.shape
    t = ROUTE_T
    return pl.pallas_call(
        _route_kernel,
        grid=(n // t,),
        in_specs=[pl.BlockSpec((e, t), lambda i: (0, i)),
                  pl.BlockSpec((e, 1), lambda i: (0, 0))],
        out_specs=[pl.BlockSpec((TOP_K, t), lambda i: (0, i)),
                   pl.BlockSpec((TOP_K, t), lambda i: (0, i)),
                   pl.BlockSpec((e, 1), lambda i: (0, 0))],
        out_shape=[jax.ShapeDtypeStruct((TOP_K, n), I32),
                   jax.ShapeDtypeStruct((TOP_K, n), F32),
                   jax.ShapeDtypeStruct((e, 1), I32)],
        compiler_params=_cparams(("arbitrary",)),
        name="route",
    )(scores_t, e_bias_slab)


def _dest_kernel(eidx_ref, pstart_ref, dest_ref, carry_ref):
    @pl.when(pl.program_id(0) == 0)
    def _():
        carry_ref[...] = jnp.zeros_like(carry_ref)

    nsl = GROUP_SIZE
    t = eidx_ref.shape[1]
    sub = lax.broadcasted_iota(I32, (SUBLANES, t), 0)
    eid = [sub * GROUP_SIZE + j for j in range(nsl)]
    ek = [eidx_ref[k:k + 1, :] for k in range(TOP_K)]
    slabs = []
    for j in range(nsl):
        oh = jnp.zeros((SUBLANES, t), F32)
        for k in range(TOP_K):
            oh = oh + (eid[j] == ek[k]).astype(F32)
        slabs.append(oh)
    onehot = jnp.concatenate(slabs, axis=0)
    row = lax.broadcasted_iota(I32, (t, t), 0)
    col = lax.broadcasted_iota(I32, (t, t), 1)
    upper = (row < col).astype(BF16)
    before = jnp.dot(onehot.astype(BF16), upper, preferred_element_type=F32)
    base = before + carry_ref[...] + pstart_ref[...]
    dest = jnp.zeros((TOP_K, t), I32)
    for k in range(TOP_K):
        acc = jnp.zeros((SUBLANES, t), F32)
        for j in range(nsl):
            acc = acc + jnp.where(eid[j] == ek[k], base[j * SUBLANES:(j + 1) * SUBLANES, :], 0.0)
        dk = _sublane_all(acc, jnp.add).astype(I32)
        dest = jnp.where(sub == k, dk, dest)
    dest_ref[...] = dest
    carry_ref[...] += jnp.sum(onehot, axis=1, keepdims=True)


def _dest(eidx_t, pstart_slab):
    n = eidx_t.shape[1]
    t = ROUTE_T
    return pl.pallas_call(
        _dest_kernel,
        grid=(n // t,),
        in_specs=[pl.BlockSpec((TOP_K, t), lambda i: (0, i)),
                  pl.BlockSpec((N_EXPERTS, 1), lambda i: (0, 0))],
        out_specs=pl.BlockSpec((TOP_K, t), lambda i: (0, i)),
        out_shape=jax.ShapeDtypeStruct((TOP_K, n), I32),
        scratch_shapes=[pltpu.VMEM((N_EXPERTS, 1), F32)],
        compiler_params=_cparams(("arbitrary",)),
        name="dest",
    )(eidx_t, pstart_slab)


def _dispatch_kernel(dest_ref, zero_ref, h_ref, xg_ref, zbuf, sem, zsem):
    t = h_ref.shape[0]
    nblk = zero_ref.shape[0]

    @pl.when(pl.program_id(0) == 0)
    def _():
        zbuf[...] = jnp.zeros_like(zbuf)

        def zero_copy(i):
            return pltpu.make_async_copy(zbuf, xg_ref.at[pl.ds(pl.multiple_of(i * MOE_BLK, MOE_BLK), MOE_BLK)],
                                         zsem)

        def zstart(i, c):
            @pl.when(zero_ref[i] != 0)
            def _():
                zero_copy(i).start()
            return c

        def zwait(i, c):
            @pl.when(zero_ref[i] != 0)
            def _():
                zero_copy(i).wait()
            return c

        lax.fori_loop(0, nblk, zstart, 0)
        lax.fori_loop(0, nblk, zwait, 0)

    def row_copy(tok, k):
        return pltpu.make_async_copy(h_ref.at[pl.ds(tok, 1)], xg_ref.at[pl.ds(dest_ref[k, tok], 1)], sem)

    def issue(tb, c):
        base = pl.multiple_of(tb * SUBLANES, SUBLANES)
        for u in range(SUBLANES):
            for k in range(TOP_K):
                row_copy(base + u, k).start()
        return c

    def drain(tok, c):
        for k in range(TOP_K):
            row_copy(tok, k).wait()
        return c

    lax.fori_loop(0, t // SUBLANES, issue, 0)
    lax.fori_loop(0, t, drain, 0)


def _dispatch(dest_t, zero_blk, h2p, rows_total):
    n, dh = h2p.shape
    t = DISP_T
    return pl.pallas_call(
        _dispatch_kernel,
        grid=(n // t,),
        in_specs=[pl.BlockSpec((TOP_K, t), lambda i: (0, i), memory_space=pltpu.SMEM),
                  pl.BlockSpec(memory_space=pltpu.SMEM),
                  pl.BlockSpec((t, dh), lambda i: (i, 0))],
        out_specs=pl.BlockSpec(memory_space=pl.ANY),
        out_shape=jax.ShapeDtypeStruct((rows_total, dh), U32),
        scratch_shapes=[pltpu.VMEM((MOE_BLK, dh), U32), pltpu.SemaphoreType.DMA, pltpu.SemaphoreType.DMA],
        compiler_params=_cparams(("arbitrary",)),
        name="dispatch",
    )(dest_t, zero_blk, h2p)


def _expert_kernel(cstart_ref, nused_ref, w1_ref, w3_ref, w2_ref, xg_ref, y_ref,
                   w1b, w3b, w2b, xbuf, ybuf, xsem, ysem):
    e = pl.program_id(0)
    c0 = cstart_ref[e]
    c1 = cstart_ref[e + 1]
    nused = nused_ref[0]
    nchunks = xg_ref.shape[0] // MOE_BLK

    def rows(g):
        return pl.ds(pl.multiple_of(g * MOE_BLK, MOE_BLK), MOE_BLK)

    def xcopy(g, slot):
        return pltpu.make_async_copy(xg_ref.at[rows(g)], xbuf.at[slot], xsem.at[slot])

    def ycopy(g, slot):
        return pltpu.make_async_copy(ybuf.at[slot], y_ref.at[rows(g)], ysem.at[slot])

    @pl.when((e == 0) & (nused > 0))
    def _():
        xcopy(0, 0).start()

    @pl.when(c1 > c0)
    def _():
        w1b[...] = w1_ref[...].astype(BF16)
        w3b[...] = w3_ref[...].astype(BF16)
        w2b[...] = w2_ref[...].astype(BF16)

        def chunk(g, carry):
            slot = g % 2

            @pl.when(g + 1 < nused)
            def _():
                xcopy(g + 1, 1 - slot).start()

            xcopy(g, slot).wait()
            lo, hi = _unpack_rows(xbuf[slot])
            xb = jnp.concatenate([lo, hi], axis=1).astype(BF16)
            hid = _silu(jnp.dot(xb, w1b[...], preferred_element_type=F32)) * jnp.dot(
                xb, w3b[...], preferred_element_type=F32)
            yp = _pack_rows(jnp.dot(hid.astype(BF16), w2b[...], preferred_element_type=F32))

            @pl.when(g >= 2)
            def _():
                ycopy(g - 2, slot).wait()

            ybuf[slot] = yp
            ycopy(g, slot).start()
            return carry

        lax.fori_loop(c0, c1, chunk, 0)

    @pl.when(e == pl.num_programs(0) - 1)
    def _():
        for back in (2, 1):
            g = nused - back

            @pl.when(g >= 0)
            def _():
                ycopy(g, g % 2).wait()

        ybuf[0] = jnp.zeros(ybuf.shape[1:], ybuf.dtype)

        def zstart(g, c):
            ycopy(g, 0).start()
            return c

        def zwait(g, c):
            ycopy(g, 0).wait()
            return c

        lax.fori_loop(nused, nchunks, zstart, 0)
        lax.fori_loop(nused, nchunks, zwait, 0)


def _experts(cstart, nused, xg, w1, w3, w2):
    rows, dh = xg.shape
    ne, d, f = w1.shape
    grid_spec = pltpu.PrefetchScalarGridSpec(
        num_scalar_prefetch=2,
        grid=(ne,),
        in_specs=[pl.BlockSpec((None, d, f), lambda e, cs, nu: (e, 0, 0)),
                  pl.BlockSpec((None, d, f), lambda e, cs, nu: (e, 0, 0)),
                  pl.BlockSpec((None, f, d), lambda e, cs, nu: (e, 0, 0)),
                  pl.BlockSpec(memory_space=pl.ANY)],
        out_specs=pl.BlockSpec(memory_space=pl.ANY),
        scratch_shapes=[pltpu.VMEM((d, f), BF16), pltpu.VMEM((d, f), BF16), pltpu.VMEM((f, d), BF16),
                        pltpu.VMEM((2, MOE_BLK, dh), U32), pltpu.VMEM((2, MOE_BLK, dh), U32),
                        pltpu.SemaphoreType.DMA((2,)), pltpu.SemaphoreType.DMA((2,))],
    )
    return pl.pallas_call(
        _expert_kernel,
        grid_spec=grid_spec,
        out_shape=jax.ShapeDtypeStruct((rows, dh), U32),
        compiler_params=_cparams(("arbitrary",)),
        name="experts",
    )(cstart, nused, w1, w3, w2, xg)


def _combine_kernel(dest_ref, dnext_ref, gate_ref, x1s_ref, g2_ref, fg_ref, y_ref, out_ref, buf, sems):
    t = x1s_ref.shape[0]
    i = pl.program_id(0)
    nsteps = pl.num_programs(0)
    slot = i % 2

    def row_copy(d_ref, sl, tok, k):
        return pltpu.make_async_copy(y_ref.at[pl.ds(d_ref[k, tok], 1)],
                                     buf.at[sl, pl.ds(k * t + tok, 1)], sems.at[sl])

    def issue(d_ref, sl):
        def body(tb, c):
            base = pl.multiple_of(tb * SUBLANES, SUBLANES)
            for u in range(SUBLANES):
                for k in range(TOP_K):
                    row_copy(d_ref, sl, base + u, k).start()
            return c
        lax.fori_loop(0, t // SUBLANES, body, 0)

    @pl.when(i == 0)
    def _():
        issue(dest_ref, 0)

    @pl.when(i + 1 < nsteps)
    def _():
        issue(dnext_ref, 1 - slot)

    def drain(tok, c):
        for k in range(TOP_K):
            row_copy(dest_ref, slot, tok, k).wait()
        return c

    lax.fori_loop(0, t, drain, 0)
    gates = gate_ref[...]
    half = x1s_ref.shape[1] // 2
    r_lo = jnp.zeros((t, half), F32)
    r_hi = jnp.zeros((t, half), F32)
    for k in range(TOP_K):
        lo, hi = _unpack_rows(buf[slot, k * t:(k + 1) * t, :])
        gk = gates[:, k:k + 1]
        r_lo = r_lo + gk * lo
        r_hi = r_hi + gk * hi
    routed = jnp.concatenate([r_lo, r_hi], axis=1)
    x2 = x1s_ref[...] + g2_ref[...] * routed
    out_ref[...] = _rms(x2, fg_ref[...])


def _combine(dest_t, gates, x1s, mod3, final_g, y, s):
    n, d = x1s.shape
    t = COMB_T
    tpb = s // t
    last = n // t - 1
    return pl.pallas_call(
        _combine_kernel,
        grid=(n // t,),
        in_specs=[pl.BlockSpec((TOP_K, t), lambda i: (0, i), memory_space=pltpu.SMEM),
                  pl.BlockSpec((TOP_K, t), lambda i: (0, jnp.minimum(i + 1, last)), memory_space=pltpu.SMEM),
                  pl.BlockSpec((t, TOP_K), lambda i: (i, 0)),
                  pl.BlockSpec((t, d), lambda i: (i, 0)),
                  pl.BlockSpec((None, 1, d), lambda i: ((i // tpb) * 6 + 5, 0, 0)),
                  pl.BlockSpec((1, d), lambda i: (0, 0)),
                  pl.BlockSpec(memory_space=pl.ANY)],
        out_specs=pl.BlockSpec((t, d), lambda i: (i, 0)),
        out_shape=jax.ShapeDtypeStruct((n, d), F32),
        scratch_shapes=[pltpu.VMEM((2, TOP_K * t, d // 2), U32), pltpu.SemaphoreType.DMA((2,))],
        compiler_params=_cparams(("arbitrary",)),
        name="combine",
    )(dest_t, dest_t, gates, x1s, mod3, final_g, y)


def _place_heads(w, per_head, keep):
    r = w.shape[0]
    w = w.reshape(r, B_HEADS, per_head)[:, :, :keep]
    return jnp.pad(w, ((0, 0), (0, 0), (0, HEAD_PAD - keep))).reshape(r, B_HEADS * HEAD_PAD)


def _rope_tables(s):
    inv = ROPE_THETA ** (-jnp.arange(0, B_ROPE, 2, dtype=jnp.float32) / B_ROPE)
    ang = jnp.arange(s, dtype=jnp.float32)[:, None] * inv[None, :]
    cos, sin = jnp.cos(ang), jnp.sin(ang)
    ones = jnp.ones((s, B_NOPE), F32)
    zeros = jnp.zeros((s, B_NOPE), F32)
    tail1 = jnp.ones((s, HEAD_PAD - B_NOPE - B_ROPE), F32)
    tail0 = jnp.zeros((s, HEAD_PAD - B_NOPE - B_ROPE), F32)
    return (jnp.concatenate([ones, cos, cos, tail1], axis=1),
            jnp.concatenate([zeros, -sin, sin, tail0], axis=1))


def kernel(x, c, w_ada, b_ada, norm1_g, w_in, q_norm_g, w_uq, kv_norm_g, w_ukv, rel_table, a_out_g, b_out_g,
           w_o, norm2_g, w_router, e_bias, w1, w3, w2, ws1, ws3, ws2, final_g):
    b, s, d = x.shape
    n = b * s
    assert w_ada.shape[0] == 1, "single layer"
    x2 = x.reshape(n, d)

    mod = _ada(c, w_ada[0], b_ada[0])
    mod3 = mod.reshape(b * 6, 1, d)
    bias = _bias_tiles(rel_table)

    wi = w_in[0]
    c_kpe = 3 * A_WIDTH + Q_LORA + KV_LORA
    kpe_cols = jnp.pad(wi[:, c_kpe:], ((0, 0), (B_NOPE, HEAD_PAD - B_NOPE - B_ROPE)))
    w_in_ext = jnp.concatenate([wi[:, :c_kpe], kpe_cols], axis=1).astype(BF16)
    w_uq_p = _place_heads(w_uq[0], B_NOPE + B_ROPE, B_NOPE + B_ROPE).astype(BF16)
    w_uk_p = _place_heads(w_ukv[0], B_NOPE + B_VDIM, B_NOPE).astype(BF16)
    w_v = w_ukv[0].reshape(KV_LORA, B_HEADS, B_NOPE + B_VDIM)[:, :, B_NOPE:].reshape(KV_LORA, B_WIDTH).astype(BF16)
    cos_t, sin_t = _rope_tables(s)

    qa, ka, va, qb, kb, vb = _inproj(x2, mod3, norm1_g, w_in_ext, q_norm_g, w_uq_p, kv_norm_g, w_uk_p, w_v,
                                     cos_t, sin_t, b, s)
    a_out = _dilated(qa.reshape(b, s, A_WIDTH), ka.reshape(b, s, A_WIDTH), va.reshape(b, s, A_WIDTH), bias)
    b_out = _mla(qb, kb, vb)

    w_router_t = _slab_order(w_router[0].T)
    x1s, h2p, scores_t = _outproj(x2, a_out.reshape(n, A_WIDTH), b_out.reshape(n, B_WIDTH), a_out_g, b_out_g,
                                  w_o[0].astype(BF16), mod3, norm2_g, w_router_t,
                                  ws1[0].astype(BF16), ws3[0].astype(BF16), ws2[0].astype(BF16), s)

    eidx_t, gates_t, counts_slab = _route(scores_t, _slab_order(e_bias.reshape(N_EXPERTS, 1)))

    counts = _expert_order(counts_slab)[:, 0]
    padded = (counts + MOE_BLK - 1) // MOE_BLK * MOE_BLK
    pends = jnp.cumsum(padded)
    pstart = pends - padded
    nk = n * TOP_K
    nblk = -(-(nk + N_EXPERTS * (MOE_BLK - 1)) // MOE_BLK)
    rows_total = nblk * MOE_BLK
    blk_row = jnp.arange(nblk, dtype=I32) * MOE_BLK
    blk_e = jnp.minimum(jnp.sum((pends[None, :] <= blk_row[:, None]).astype(I32), axis=1), N_EXPERTS - 1)
    seg_end = (pstart + counts)[blk_e]
    zero_blk = ((blk_row >= pends[-1]) | (seg_end < blk_row + MOE_BLK)).astype(I32)
    nused = (pends[-1] // MOE_BLK).astype(I32).reshape(1)

    dest_t = _dest(eidx_t, _slab_order(pstart.astype(F32).reshape(N_EXPERTS, 1)))
    xg = _dispatch(dest_t, zero_blk, h2p, rows_total)
    cstart = jnp.concatenate([jnp.zeros((1,), I32), (pends // MOE_BLK).astype(I32)])
    y = _experts(cstart, nused, xg, w1[0], w3[0], w2[0])
    out = _combine(dest_t, gates_t.T, x1s, mod3, final_g.reshape(1, d), y, s)
    return out.reshape(b, s, d)
```

```python
import functools
import math

import jax
import jax.numpy as jnp
from jax import lax
from jax.experimental import pallas as pl
from jax.experimental.pallas import tpu as pltpu
from jax.experimental.pallas import tpu_sc as plsc

F32 = jnp.float32
BF16 = jnp.bfloat16
U32 = jnp.uint32
I32 = jnp.int32
HIGHEST = lax.Precision.HIGHEST

D_MODEL = 1024
A_HEADS = 8
A_HEAD_DIM = 64
A_WIDTH = A_HEADS * A_HEAD_DIM
A_PATTERNS = ((128, 1), (512, 4), (2048, 16))
A_RADIUS = 64
REL_BUCKETS = 32
REL_MAX_DIST = 1024
B_HEADS = 8
B_NOPE = 64
B_ROPE = 32
B_VDIM = 64
B_WIDTH = B_HEADS * B_VDIM
Q_LORA = 384
KV_LORA = 256
ROPE_THETA = 10000.0
N_EXPERTS = 256
TOP_K = 8
N_GROUPS = 8
GROUP_SIZE = N_EXPERTS // N_GROUPS
TOPK_GROUPS = 4
EXPERT_FF = 256
SHARED_FF = 256
ROUTED_SCALE = 2.5
EPS = 1e-6
NEG_INF = -1e30
LOG2E = math.log2(math.e)

LANES = 128
SUBLANES = 8
HEAD_PAD = 128
IN_COLS_EXT = 3 * A_WIDTH + Q_LORA + KV_LORA + HEAD_PAD

TM_INPROJ = 512
TM_OUTPROJ = 256
DIL_QB = 128
DIL_KW = DIL_QB + 2 * A_RADIUS
DIL_UNROLL = 4
MLA_TQ = 256
ROUTE_T = 256
MOE_BLK = 256
DISP_T = 256
EXP_XBUF = 4
SC_WINDOW = 128
COMB_T = 256
VMEM_LIMIT = 56 * 1024 * 1024


def _cparams(sem):
    return pltpu.CompilerParams(dimension_semantics=sem, vmem_limit_bytes=VMEM_LIMIT)


def _rms(x, g):
    return x * lax.rsqrt(jnp.mean(x * x, axis=-1, keepdims=True) + EPS) * g


def _silu(x):
    return x * jax.nn.sigmoid(x)


def _pack_rows(x):
    half = x.shape[1] // 2
    bits = lax.bitcast_convert_type(x.astype(BF16).astype(F32), U32)
    return (bits[:, :half] >> 16) | bits[:, half:]


def _unpack_rows(w):
    lo = lax.bitcast_convert_type(w << 16, F32)
    hi = lax.bitcast_convert_type(w & jnp.uint32(0xFFFF0000), F32)
    return lo, hi


def _ada_kernel(c_ref, w_ref, b_ref, o_ref):
    o_ref[...] = jnp.dot(_silu(c_ref[...]), w_ref[...], precision=HIGHEST,
                         preferred_element_type=F32) + b_ref[...]


def _ada(c, w_ada, b_ada):
    b, d = c.shape
    n6 = w_ada.shape[1] // d
    return pl.pallas_call(
        _ada_kernel,
        grid=(n6,),
        in_specs=[pl.BlockSpec((b, d), lambda j: (0, 0)),
                  pl.BlockSpec((d, d), lambda j: (0, j)),
                  pl.BlockSpec((1, d), lambda j: (0, j))],
        out_specs=pl.BlockSpec((b, d), lambda j: (0, j)),
        out_shape=jax.ShapeDtypeStruct((b, n6 * d), F32),
        compiler_params=_cparams(("parallel",)),
        name="ada",
    )(c, w_ada, b_ada.reshape(1, -1))


def _t5_bucket(rel):
    half = REL_BUCKETS // 2
    max_exact = half // 2
    ret = jnp.where(rel > 0, half, 0)
    n = jnp.abs(rel)
    nf = jnp.maximum(n, 1).astype(jnp.float32)
    large = max_exact + (jnp.log(nf / max_exact) / math.log(REL_MAX_DIST / max_exact)
                         * (half - max_exact)).astype(jnp.int32)
    large = jnp.minimum(large, half - 1)
    return ret + jnp.where(n < max_exact, n, large)


DIL_SHIFTS = (A_RADIUS, 0, -A_RADIUS)


def _bucket_tiles():
    qi = jnp.arange(DIL_QB, dtype=jnp.int32)[:, None]
    ki = jnp.arange(DIL_KW, dtype=jnp.int32)[None, :]
    tiles = []
    for _, dilation in A_PATTERNS:
        for shift in DIL_SHIFTS:
            off = ki + shift - A_RADIUS - qi
            bkt = _t5_bucket(off * dilation)
            tiles.append(jnp.where(jnp.abs(off) <= A_RADIUS, bkt, -1))
    return jnp.stack(tiles, axis=0)


def _bias_kernel(tab_ref, bkt_ref, o_ref):
    bkt = bkt_ref[0]
    for h in range(A_HEADS):
        acc = jnp.full(bkt.shape, NEG_INF, F32)
        for b in range(REL_BUCKETS):
            acc = jnp.where(bkt == b, tab_ref[b, h] * LOG2E, acc)
        o_ref[0, h] = acc


def _bias_tiles(rel_table):
    bkt = _bucket_tiles()
    nt = bkt.shape[0]
    return pl.pallas_call(
        _bias_kernel,
        grid=(nt,),
        in_specs=[pl.BlockSpec(memory_space=pltpu.SMEM),
                  pl.BlockSpec((1, DIL_QB, DIL_KW), lambda t: (t, 0, 0))],
        out_specs=pl.BlockSpec((1, A_HEADS, DIL_QB, DIL_KW), lambda t: (t, 0, 0, 0)),
        out_shape=jax.ShapeDtypeStruct((nt, A_HEADS, DIL_QB, DIL_KW), F32),
        compiler_params=_cparams(("parallel",)),
        name="bias",
    )(rel_table, bkt)


def _rope(x, cos, sin, lane_lt_mid):
    half = B_ROPE // 2
    rot = jnp.where(lane_lt_mid, pltpu.roll(x, HEAD_PAD - half, 1), pltpu.roll(x, half, 1))
    return x * cos + rot * sin


def _inproj_kernel(x_ref, sc_ref, sh_ref, g1_ref, win_ref, qg_ref, wuq_ref, kvg_ref, wuk_ref, wv_ref,
                   cos_ref, sin_ref, qa_ref, ka_ref, va_ref, qb_ref, kb_ref, vb_ref):
    x = x_ref[...]
    h = _rms(x, g1_ref[...]) * (1.0 + sc_ref[...]) + sh_ref[...]
    proj = jnp.dot(h.astype(BF16), win_ref[...], preferred_element_type=F32)
    aw = A_WIDTH
    qa_ref[...] = proj[:, 0:aw] * (LOG2E / math.sqrt(A_HEAD_DIM))
    ka_ref[...] = proj[:, aw:2 * aw]
    va_ref[...] = proj[:, 2 * aw:3 * aw]
    c0 = 3 * aw
    q_lat = proj[:, c0:c0 + Q_LORA]
    kv_lat = proj[:, c0 + Q_LORA:c0 + Q_LORA + KV_LORA]
    kpe = proj[:, c0 + Q_LORA + KV_LORA:]
    qn = _rms(q_lat, qg_ref[...]).astype(BF16)
    kvn = _rms(kv_lat, kvg_ref[...]).astype(BF16)
    qm = jnp.dot(qn, wuq_ref[...], preferred_element_type=F32)
    kn = jnp.dot(kvn, wuk_ref[...], preferred_element_type=F32)
    vv = jnp.dot(kvn, wv_ref[...], preferred_element_type=F32)
    cos = cos_ref[...]
    sin = sin_ref[...]
    lane = lax.broadcasted_iota(jnp.int32, cos.shape, 1)
    lt_mid = lane < (B_NOPE + B_ROPE // 2)
    qscale = LOG2E / math.sqrt(B_NOPE + B_ROPE)
    for hd in range(B_HEADS):
        sl = slice(hd * HEAD_PAD, (hd + 1) * HEAD_PAD)
        qb_ref[hd] = (_rope(qm[:, sl], cos, sin, lt_mid) * qscale).astype(BF16)
        kb_ref[hd] = _rope(kn[:, sl] + kpe, cos, sin, lt_mid).astype(BF16)
    for p in range(B_HEADS // 2):
        vb_ref[p] = vv[:, p * LANES:(p + 1) * LANES].astype(BF16)


def _inproj(x2, mod3, norm1_g, w_in_ext, q_norm_g, w_uq_p, kv_norm_g, w_uk_p, w_v, cos_t, sin_t, b, s):
    n, d = x2.shape
    tm = TM_INPROJ
    tpb = s // tm
    row = lambda i: (i, 0)
    const = lambda i: (0, 0)
    hm = lambda i: (i // tpb, 0, i % tpb, 0)
    return pl.pallas_call(
        _inproj_kernel,
        grid=(n // tm,),
        in_specs=[pl.BlockSpec((tm, d), row),
                  pl.BlockSpec((None, 1, d), lambda i: ((i // tpb) * 6 + 1, 0, 0)),
                  pl.BlockSpec((None, 1, d), lambda i: ((i // tpb) * 6 + 0, 0, 0)),
                  pl.BlockSpec((1, d), const),
                  pl.BlockSpec(w_in_ext.shape, const),
                  pl.BlockSpec((1, Q_LORA), const),
                  pl.BlockSpec(w_uq_p.shape, const),
                  pl.BlockSpec((1, KV_LORA), const),
                  pl.BlockSpec(w_uk_p.shape, const),
                  pl.BlockSpec(w_v.shape, const),
                  pl.BlockSpec((tm, HEAD_PAD), lambda i: (i % tpb, 0)),
                  pl.BlockSpec((tm, HEAD_PAD), lambda i: (i % tpb, 0))],
        out_specs=[pl.BlockSpec((tm, A_WIDTH), row),
                   pl.BlockSpec((tm, A_WIDTH), row),
                   pl.BlockSpec((tm, A_WIDTH), row),
                   pl.BlockSpec((None, B_HEADS, tm, HEAD_PAD), hm),
                   pl.BlockSpec((None, B_HEADS, tm, HEAD_PAD), hm),
                   pl.BlockSpec((None, B_HEADS // 2, tm, LANES), hm)],
        out_shape=[jax.ShapeDtypeStruct((n, A_WIDTH), F32),
                   jax.ShapeDtypeStruct((n, A_WIDTH), F32),
                   jax.ShapeDtypeStruct((n, A_WIDTH), F32),
                   jax.ShapeDtypeStruct((b, B_HEADS, s, HEAD_PAD), BF16),
                   jax.ShapeDtypeStruct((b, B_HEADS, s, HEAD_PAD), BF16),
                   jax.ShapeDtypeStruct((b, B_HEADS // 2, s, LANES), BF16)],
        compiler_params=_cparams(("parallel",)),
        name="inproj",
    )(x2, mod3, mod3, norm1_g, w_in_ext, q_norm_g, w_uq_p, kv_norm_g, w_uk_p, w_v, cos_t, sin_t)


def _dil_block(q_ref, k_ref, v_ref, bias_ref, o_scr, l_scr, pi, dil, nblk, job):
    sub_len = nblk * DIL_QB
    r = job // nblk
    bi = job % nblk
    q0 = bi * DIL_QB
    ws = jnp.clip(q0 - A_RADIUS, 0, sub_len - DIL_KW)
    var = jnp.where(bi == 0, 0, jnp.where(bi == nblk - 1, 2, 1))
    if dil == 1:
        qsl = pl.ds(pl.multiple_of(q0, DIL_QB), DIL_QB)
        ksl = pl.ds(pl.multiple_of(ws, A_RADIUS), DIL_KW)
    else:
        qsl = pl.ds(r + dil * q0, DIL_QB, stride=dil)
        ksl = pl.ds(r + dil * ws, DIL_KW, stride=dil)
    q = q_ref[qsl, :]
    kw = k_ref[ksl, :].astype(BF16)
    vw = v_ref[ksl, :].astype(BF16)
    lo = lax.broadcasted_iota(jnp.int32, q.shape, 1) < A_HEAD_DIM
    outs, lses = [], []
    for hh in range(2):
        qm = jnp.where(lo if hh == 0 else jnp.logical_not(lo), q, 0.0).astype(BF16)
        sc = lax.dot_general(qm, kw, (((1,), (1,)), ((), ())), preferred_element_type=F32)
        sc = sc + bias_ref[pi * 3 + var, hh]
        m = jnp.max(sc, axis=-1, keepdims=True)
        p = jnp.exp2(sc - m)
        l = jnp.sum(p, axis=-1, keepdims=True)
        o = jnp.dot(p.astype(BF16), vw, preferred_element_type=F32)
        outs.append(o / l)
        lses.append(m + jnp.log2(l))
    o_scr[pi, qsl, :] = jnp.where(lo, outs[0], outs[1])
    l_scr[pi, qsl, :] = jnp.where(lo, lses[0], lses[1])


def _dilated_kernel(q_ref, k_ref, v_ref, bias_ref, out_ref, o_scr, l_scr):
    s = q_ref.shape[0]
    njobs = s // DIL_QB
    for pi, (_, dil) in enumerate(A_PATTERNS):
        blk = functools.partial(_dil_block, q_ref, k_ref, v_ref, bias_ref, o_scr, l_scr, pi, dil,
                                s // dil // DIL_QB)

        def group(g, c, blk=blk):
            for u in range(DIL_UNROLL):
                blk(g * DIL_UNROLL + u)
            return c

        lax.fori_loop(0, njobs // DIL_UNROLL, group, 0)

    chunk = 512

    def comb(i, c):
        rows = pl.ds(pl.multiple_of(i * chunk, chunk), chunk)
        l0, l1, l2 = l_scr[0, rows, :], l_scr[1, rows, :], l_scr[2, rows, :]
        mx = jnp.maximum(jnp.maximum(l0, l1), l2)
        e0, e1, e2 = jnp.exp2(l0 - mx), jnp.exp2(l1 - mx), jnp.exp2(l2 - mx)
        num = e0 * o_scr[0, rows, :] + e1 * o_scr[1, rows, :] + e2 * o_scr[2, rows, :]
        out_ref[rows, :] = num / (e0 + e1 + e2)
        return c

    lax.fori_loop(0, s // chunk, comb, 0)


def _dilated(qa, ka, va, bias):
    b, s, _ = qa.shape
    npair = A_WIDTH // LANES
    assert (s // DIL_QB) % DIL_UNROLL == 0
    for _, dil in A_PATTERNS:
        assert (s // dil) % DIL_QB == 0 and s // dil >= DIL_KW
    blk = pl.BlockSpec((None, s, LANES), lambda bi, p: (bi, 0, p))
    return pl.pallas_call(
        _dilated_kernel,
        grid=(b, npair),
        in_specs=[blk, blk, blk,
                  pl.BlockSpec((bias.shape[0], 2, DIL_QB, DIL_KW), lambda bi, p: (0, p, 0, 0))],
        out_specs=blk,
        out_shape=jax.ShapeDtypeStruct((b, s, A_WIDTH), F32),
        scratch_shapes=[pltpu.VMEM((len(A_PATTERNS), s, LANES), F32),
                        pltpu.VMEM((len(A_PATTERNS), s, LANES), F32)],
        compiler_params=_cparams(("parallel", "parallel")),
        name="dilated",
    )(qa, ka, va, bias)


def _mla_kernel(q_ref, k_ref, v_ref, o_ref):
    v = v_ref[...]
    outs = []
    for hh in range(2):
        sc = lax.dot_general(q_ref[hh], k_ref[hh], (((1,), (1,)), ((), ())), preferred_element_type=F32)
        m = jnp.max(sc, axis=-1, keepdims=True)
        p = jnp.exp2(sc - m)
        l = jnp.sum(p, axis=-1, keepdims=True)
        outs.append(jnp.dot(p.astype(BF16), v, preferred_element_type=F32) / l)
    lo = lax.broadcasted_iota(jnp.int32, outs[0].shape, 1) < B_VDIM
    o_ref[...] = jnp.where(lo, outs[0], outs[1])


def _mla(qb, kb, vb):
    b, h, s, _ = qb.shape
    npair = h // 2
    tq = MLA_TQ
    return pl.pallas_call(
        _mla_kernel,
        grid=(b, npair, s // tq),
        in_specs=[pl.BlockSpec((None, 2, tq, HEAD_PAD), lambda bi, p, qi: (bi, p, qi, 0)),
                  pl.BlockSpec((None, 2, s, HEAD_PAD), lambda bi, p, qi: (bi, p, 0, 0)),
                  pl.BlockSpec((None, None, s, LANES), lambda bi, p, qi: (bi, p, 0, 0))],
        out_specs=pl.BlockSpec((None, tq, LANES), lambda bi, p, qi: (bi, qi, p)),
        out_shape=jax.ShapeDtypeStruct((b, s, B_WIDTH), F32),
        compiler_params=_cparams(("parallel", "parallel", "arbitrary")),
        name="mla",
    )(qb, kb, vb)


def _outproj_kernel(x_ref, a_ref, b_ref, ag_ref, bg_ref, wo_ref, g1_ref, n2_ref, sc_ref, sh_ref, g2_ref,
                    wrt_ref, ws1_ref, ws3_ref, ws2_ref, x1s_ref, h2p_ref, scores_ref):
    an = _rms(a_ref[...], ag_ref[...])
    bn = _rms(b_ref[...], bg_ref[...])
    mix = jnp.concatenate([an, bn], axis=-1).astype(BF16)
    x1 = x_ref[...] + g1_ref[...] * jnp.dot(mix, wo_ref[...], preferred_element_type=F32)
    h2 = _rms(x1, n2_ref[...]) * (1.0 + sc_ref[...]) + sh_ref[...]
    h2p_ref[...] = _pack_rows(h2)
    logits = lax.dot_general(wrt_ref[...], h2, (((1,), (1,)), ((), ())), precision=HIGHEST,
                             preferred_element_type=F32)
    scores_ref[...] = jax.nn.sigmoid(logits)
    h2b = h2.astype(BF16)
    hid = _silu(jnp.dot(h2b, ws1_ref[...], preferred_element_type=F32)) * jnp.dot(
        h2b, ws3_ref[...], preferred_element_type=F32)
    shared = jnp.dot(hid.astype(BF16), ws2_ref[...], preferred_element_type=F32)
    x1s_ref[...] = x1 + g2_ref[...] * shared


def _outproj(x2, a_out, b_out, a_out_g, b_out_g, w_o, mod3, norm2_g, w_router_t, ws1, ws3, ws2, s):
    n, d = x2.shape
    tm = TM_OUTPROJ
    tpb = s // tm
    row = lambda i: (i, 0)
    const = lambda i: (0, 0)
    modspec = lambda j: pl.BlockSpec((None, 1, d), lambda i: ((i // tpb) * 6 + j, 0, 0))
    return pl.pallas_call(
        _outproj_kernel,
        grid=(n // tm,),
        in_specs=[pl.BlockSpec((tm, d), row),
                  pl.BlockSpec((tm, A_WIDTH), row),
                  pl.BlockSpec((tm, B_WIDTH), row),
                  pl.BlockSpec((1, A_WIDTH), const),
                  pl.BlockSpec((1, B_WIDTH), const),
                  pl.BlockSpec(w_o.shape, const),
                  modspec(2),
                  pl.BlockSpec((1, d), const),
                  modspec(4), modspec(3), modspec(5),
                  pl.BlockSpec(w_router_t.shape, const),
                  pl.BlockSpec(ws1.shape, const),
                  pl.BlockSpec(ws3.shape, const),
                  pl.BlockSpec(ws2.shape, const)],
        out_specs=[pl.BlockSpec((tm, d), row),
                   pl.BlockSpec((tm, d // 2), row),
                   pl.BlockSpec((N_EXPERTS, tm), lambda i: (0, i))],
        out_shape=[jax.ShapeDtypeStruct((n, d), F32),
                   jax.ShapeDtypeStruct((n, d // 2), U32),
                   jax.ShapeDtypeStruct((N_EXPERTS, n), F32)],
        compiler_params=_cparams(("parallel",)),
        name="outproj",
    )(x2, a_out, b_out, a_out_g, b_out_g, w_o, mod3, norm2_g, mod3, mod3, mod3, w_router_t, ws1, ws3, ws2)


def _slab_order(v):
    return v.reshape((N_GROUPS, GROUP_SIZE) + v.shape[1:]).swapaxes(0, 1).reshape(v.shape)


def _expert_order(v):
    return v.reshape((GROUP_SIZE, N_GROUPS) + v.shape[1:]).swapaxes(0, 1).reshape(v.shape)


def _sublane_all(x, op):
    for sh in (4, 2, 1):
        x = op(x, pltpu.roll(x, sh, 0))
    return x


def _route_kernel(st_ref, bias_ref, eidx_ref, gate_ref, cnt_ref):
    nsl = GROUP_SIZE
    t = st_ref.shape[1]
    sub = lax.broadcasted_iota(I32, (SUBLANES, t), 0)
    ninf = -jnp.inf
    big = jnp.int32(1 << 30)
    sc = [st_ref[j * SUBLANES:(j + 1) * SUBLANES, :] for j in range(nsl)]
    sel = [sc[j] + bias_ref[j * SUBLANES:(j + 1) * SUBLANES, :] for j in range(nsl)]
    eid = [sub * GROUP_SIZE + j for j in range(nsl)]

    m1 = sel[0]
    m2 = jnp.full_like(m1, ninf)
    for j in range(1, nsl):
        m2 = jnp.maximum(m2, jnp.minimum(m1, sel[j]))
        m1 = jnp.maximum(m1, sel[j])
    gs = m1 + m2

    rank = jnp.zeros((SUBLANES, t), I32)
    for sh in range(1, N_GROUPS):
        other = pltpu.roll(gs, sh, 0)
        ahead = (other > gs) | ((other == gs) & (sub >= sh))
        rank = rank + ahead.astype(I32)
    gmask = rank < TOPK_GROUPS

    msel = [jnp.where(gmask, sel[j], ninf) for j in range(nsl)]
    hits = [jnp.zeros((SUBLANES, t), I32) for _ in range(nsl)]
    eidx = jnp.zeros((TOP_K, t), I32)
    gates = jnp.zeros((TOP_K, t), F32)
    for k in range(TOP_K):
        mx = msel[0]
        for j in range(1, nsl):
            mx = jnp.maximum(mx, msel[j])
        mx = _sublane_all(mx, jnp.maximum)
        cand = jnp.where(msel[0] == mx, eid[0], big)
        for j in range(1, nsl):
            cand = jnp.minimum(cand, jnp.where(msel[j] == mx, eid[j], big))
        idx = _sublane_all(cand, jnp.minimum)
        gk = jnp.zeros((SUBLANES, t), F32)
        for j in range(nsl):
            hit = eid[j] == idx
            gk = gk + jnp.where(hit, sc[j], 0.0)
            msel[j] = jnp.where(hit, ninf, msel[j])
            hits[j] = hits[j] + hit.astype(I32)
        gk = _sublane_all(gk, jnp.add)
        eidx = jnp.where(sub == k, idx, eidx)
        gates = jnp.where(sub == k, gk, gates)
    gsum = _sublane_all(gates, jnp.add)
    eidx_ref[...] = eidx
    gate_ref[...] = gates / gsum * ROUTED_SCALE

    @pl.when(pl.program_id(0) == 0)
    def _():
        cnt_ref[...] = jnp.zeros_like(cnt_ref)

    for j in range(nsl):
        cnt_ref[j * SUBLANES:(j + 1) * SUBLANES, :] += jnp.sum(hits[j].astype(F32), axis=1,
                                                               keepdims=True).astype(I32)


def _route(scores_t, e_bias_slab):
    e, n = scores_t.shape
    t = ROUTE_T
    return pl.pallas_call(
        _route_kernel,
        grid=(n // t,),
        in_specs=[pl.BlockSpec((e, t), lambda i: (0, i)),
                  pl.BlockSpec((e, 1), lambda i: (0, 0))],
        out_specs=[pl.BlockSpec((TOP_K, t), lambda i: (0, i)),
                   pl.BlockSpec((TOP_K, t), lambda i: (0, i)),
                   pl.BlockSpec((e, 1), lambda i: (0, 0))],
        out_shape=[jax.ShapeDtypeStruct((TOP_K, n), I32),
                   jax.ShapeDtypeStruct((TOP_K, n), F32),
                   jax.ShapeDtypeStruct((e, 1), I32)],
        compiler_params=_cparams(("arbitrary",)),
        name="route",
    )(scores_t, e_bias_slab)


def _dest_kernel(eidx_ref, pstart_ref, dest_ref, carry_ref):
    @pl.when(pl.program_id(0) == 0)
    def _():
        carry_ref[...] = jnp.zeros_like(carry_ref)

    nsl = GROUP_SIZE
    t = eidx_ref.shape[1]
    sub = lax.broadcasted_iota(I32, (SUBLANES, t), 0)
    eid = [sub * GROUP_SIZE + j for j in range(nsl)]
    ek = [eidx_ref[k:k + 1, :] for k in range(TOP_K)]
    slabs = []
    for j in range(nsl):
        oh = jnp.zeros((SUBLANES, t), F32)
        for k in range(TOP_K):
            oh = oh + (eid[j] == ek[k]).astype(F32)
        slabs.append(oh)
    onehot = jnp.concatenate(slabs, axis=0)
    row = lax.broadcasted_iota(I32, (t, t), 0)
    col = lax.broadcasted_iota(I32, (t, t), 1)
    upper = (row < col).astype(BF16)
    before = jnp.dot(onehot.astype(BF16), upper, preferred_element_type=F32)
    base = before + carry_ref[...] + pstart_ref[...]
    dest = jnp.zeros((TOP_K, t), I32)
    for k in range(TOP_K):
        acc = jnp.zeros((SUBLANES, t), F32)
        for j in range(nsl):
            acc = acc + jnp.where(eid[j] == ek[k], base[j * SUBLANES:(j + 1) * SUBLANES, :], 0.0)
        dk = _sublane_all(acc, jnp.add).astype(I32)
        dest = jnp.where(sub == k, dk, dest)
    dest_ref[...] = dest
    carry_ref[...] += jnp.sum(onehot, axis=1, keepdims=True)


def _dest(eidx_t, pstart_slab):
    n = eidx_t.shape[1]
    t = ROUTE_T
    return pl.pallas_call(
        _dest_kernel,
        grid=(n // t,),
        in_specs=[pl.BlockSpec((TOP_K, t), lambda i: (0, i)),
                  pl.BlockSpec((N_EXPERTS, 1), lambda i: (0, 0))],
        out_specs=pl.BlockSpec((TOP_K, t), lambda i: (0, i)),
        out_shape=jax.ShapeDtypeStruct((TOP_K, n), I32),
        scratch_shapes=[pltpu.VMEM((N_EXPERTS, 1), F32)],
        compiler_params=_cparams(("arbitrary",)),
        name="dest",
    )(eidx_t, pstart_slab)


def _dispatch_kernel(dest_ref, zero_ref, h_ref, xg_ref, zbuf, sem, zsem):
    t = h_ref.shape[0]
    nblk = zero_ref.shape[0]

    @pl.when(pl.program_id(0) == 0)
    def _():
        zbuf[...] = jnp.zeros_like(zbuf)

        def zero_copy(i):
            return pltpu.make_async_copy(zbuf, xg_ref.at[pl.ds(pl.multiple_of(i * MOE_BLK, MOE_BLK), MOE_BLK)],
                                         zsem)

        def zstart(i, c):
            @pl.when(zero_ref[i] != 0)
            def _():
                zero_copy(i).start()
            return c

        def zwait(i, c):
            @pl.when(zero_ref[i] != 0)
            def _():
                zero_copy(i).wait()
            return c

        lax.fori_loop(0, nblk, zstart, 0)
        lax.fori_loop(0, nblk, zwait, 0)

    def row_copy(tok, k):
        return pltpu.make_async_copy(h_ref.at[pl.ds(tok, 1)], xg_ref.at[pl.ds(dest_ref[k, tok], 1)], sem)

    def issue(tb, c):
        base = pl.multiple_of(tb * SUBLANES, SUBLANES)
        for u in range(SUBLANES):
            for k in range(TOP_K):
                row_copy(base + u, k).start()
        return c

    def drain(tok, c):
        for k in range(TOP_K):
            row_copy(tok, k).wait()
        return c

    lax.fori_loop(0, t // SUBLANES, issue, 0)
    lax.fori_loop(0, t, drain, 0)


def _dispatch(dest_t, zero_blk, h2p, rows_total):
    n, dh = h2p.shape
    t = DISP_T
    return pl.pallas_call(
        _dispatch_kernel,
        grid=(n // t,),
        in_specs=[pl.BlockSpec((TOP_K, t), lambda i: (0, i), memory_space=pltpu.SMEM),
                  pl.BlockSpec(memory_space=pltpu.SMEM),
                  pl.BlockSpec((t, dh), lambda i: (i, 0))],
        out_specs=pl.BlockSpec(memory_space=pl.ANY),
        out_shape=jax.ShapeDtypeStruct((rows_total, dh), U32),
        scratch_shapes=[pltpu.VMEM((MOE_BLK, dh), U32), pltpu.SemaphoreType.DMA, pltpu.SemaphoreType.DMA],
        compiler_params=_cparams(("arbitrary",)),
        name="dispatch",
    )(dest_t, zero_blk, h2p)


def _expert_kernel(cstart_ref, nused_ref, w1_ref, w3_ref, w2_ref, xg_ref, y_ref,
                   w1b, w3b, w2b, xbuf, ybuf, xsem, ysem):
    e = pl.program_id(0)
    c0 = cstart_ref[e]
    c1 = cstart_ref[e + 1]
    nused = nused_ref[0]
    nchunks = xg_ref.shape[0] // MOE_BLK

    def rows(g):
        start = g * MOE_BLK
        return pl.ds(start if isinstance(g, int) else pl.multiple_of(start, MOE_BLK), MOE_BLK)

    def xcopy(g):
        slot = g % EXP_XBUF
        return pltpu.make_async_copy(xg_ref.at[rows(g)], xbuf.at[slot], xsem.at[slot])

    def ycopies(g, slot):
        return [pltpu.make_async_copy(ybuf.at[slot, pl.ds(0, MOE_BLK), pl.ds(c * LANES, LANES)],
                                      y_ref.at[c, rows(g)], ysem.at[slot])
                for c in range(y_ref.shape[0])]

    @pl.when(e == 0)
    def _():
        for j in range(EXP_XBUF - 1):
            @pl.when(j < nused)
            def _():
                xcopy(j).start()

    @pl.when(c1 > c0)
    def _():
        w1b[...] = w1_ref[...].astype(BF16)
        w3b[...] = w3_ref[...].astype(BF16)
        w2b[...] = w2_ref[...].astype(BF16)

        def chunk(g, carry):
            @pl.when(g + (EXP_XBUF - 1) < nused)
            def _():
                xcopy(g + (EXP_XBUF - 1)).start()

            xcopy(g).wait()
            lo, hi = _unpack_rows(xbuf[g % EXP_XBUF])
            xb = jnp.concatenate([lo, hi], axis=1).astype(BF16)
            hid = _silu(jnp.dot(xb, w1b[...], preferred_element_type=F32)) * jnp.dot(
                xb, w3b[...], preferred_element_type=F32)
            yp = _pack_rows(jnp.dot(hid.astype(BF16), w2b[...], preferred_element_type=F32))
            yslot = g % 2

            @pl.when(g >= 2)
            def _():
                for cp in ycopies(g - 2, yslot):
                    cp.wait()

            ybuf[yslot] = yp
            for cp in ycopies(g, yslot):
                cp.start()
            return carry

        lax.fori_loop(c0, c1, chunk, 0)

    @pl.when(e == pl.num_programs(0) - 1)
    def _():
        for back in (2, 1):
            g = nused - back

            @pl.when(g >= 0)
            def _():
                for cp in ycopies(g, g % 2):
                    cp.wait()

        ybuf[0] = jnp.zeros(ybuf.shape[1:], ybuf.dtype)

        def zstart(g, c):
            for cp in ycopies(g, 0):
                cp.start()
            return c

        def zwait(g, c):
            for cp in ycopies(g, 0):
                cp.wait()
            return c

        lax.fori_loop(nused, nchunks, zstart, 0)
        lax.fori_loop(nused, nchunks, zwait, 0)


def _experts(cstart, nused, xg, w1, w3, w2):
    rows, dh = xg.shape
    ne, d, f = w1.shape
    grid_spec = pltpu.PrefetchScalarGridSpec(
        num_scalar_prefetch=2,
        grid=(ne,),
        in_specs=[pl.BlockSpec((None, d, f), lambda e, cs, nu: (e, 0, 0)),
                  pl.BlockSpec((None, d, f), lambda e, cs, nu: (e, 0, 0)),
                  pl.BlockSpec((None, f, d), lambda e, cs, nu: (e, 0, 0)),
                  pl.BlockSpec(memory_space=pl.ANY)],
        out_specs=pl.BlockSpec(memory_space=pl.ANY),
        scratch_shapes=[pltpu.VMEM((d, f), BF16), pltpu.VMEM((d, f), BF16), pltpu.VMEM((f, d), BF16),
                        pltpu.VMEM((EXP_XBUF, MOE_BLK, dh), U32), pltpu.VMEM((2, MOE_BLK, dh), U32),
                        pltpu.SemaphoreType.DMA((EXP_XBUF,)), pltpu.SemaphoreType.DMA((2,))],
    )
    return pl.pallas_call(
        _expert_kernel,
        grid_spec=grid_spec,
        out_shape=jax.ShapeDtypeStruct((dh // LANES, rows, LANES), U32),
        compiler_params=_cparams(("arbitrary",)),
        name="experts",
    )(cstart, nused, w1, w3, w2, xg)


def _sc_gather(x, idx):
    num = idx.shape[0]
    mesh = plsc.VectorSubcoreMesh(core_axis_name="core", subcore_axis_name="subcore")

    @pl.kernel(out_type=jax.ShapeDtypeStruct((num, x.shape[1]), x.dtype), mesh=mesh)
    def gather(x_hbm, i_hbm, o_hbm):
        def body(i_vmem, o_vmem):
            pltpu.sync_copy(x_hbm.at[i_vmem.at[0]], o_vmem)

        pltpu.emit_pipeline(
            body,
            grid=(num // SC_WINDOW,),
            in_specs=[pl.BlockSpec((1, SC_WINDOW), index_map=lambda i: (0, i))],
            out_specs=[pl.BlockSpec((SC_WINDOW, x.shape[1]), index_map=lambda i: (i, 0))],
            core_axis_name=("core", "subcore"),
            dimension_semantics=(pltpu.PARALLEL,),
        )(i_hbm, o_hbm)

    return gather(x, idx.reshape(1, num))


def _combine_kernel(yg_ref, gate_ref, x1s_ref, g2_ref, fg_ref, out_ref):
    gates = gate_ref[...]
    nch = yg_ref.shape[0]
    r_lo = [None] * nch
    r_hi = [None] * nch
    for k in range(TOP_K):
        gk = gates[:, k:k + 1]
        for c in range(nch):
            lo, hi = _unpack_rows(yg_ref[c, k])
            r_lo[c] = gk * lo if k == 0 else r_lo[c] + gk * lo
            r_hi[c] = gk * hi if k == 0 else r_hi[c] + gk * hi
    routed = jnp.concatenate(r_lo + r_hi, axis=1)
    x2 = x1s_ref[...] + g2_ref[...] * routed
    out_ref[...] = _rms(x2, fg_ref[...])


def _combine(yg, gates, x1s, mod3, final_g, s):
    n, d = x1s.shape
    t = COMB_T
    tpb = s // t
    nch = yg.shape[0]
    return pl.pallas_call(
        _combine_kernel,
        grid=(n // t,),
        in_specs=[pl.BlockSpec((nch, TOP_K, t, LANES), lambda i: (0, 0, i, 0)),
                  pl.BlockSpec((t, TOP_K), lambda i: (i, 0)),
                  pl.BlockSpec((t, d), lambda i: (i, 0)),
                  pl.BlockSpec((None, 1, d), lambda i: ((i // tpb) * 6 + 5, 0, 0)),
                  pl.BlockSpec((1, d), lambda i: (0, 0))],
        out_specs=pl.BlockSpec((t, d), lambda i: (i, 0)),
        out_shape=jax.ShapeDtypeStruct((n, d), F32),
        compiler_params=_cparams(("parallel",)),
        name="combine",
    )(yg, gates, x1s, mod3, final_g)


def _place_heads(w, per_head, keep):
    r = w.shape[0]
    w = w.reshape(r, B_HEADS, per_head)[:, :, :keep]
    return jnp.pad(w, ((0, 0), (0, 0), (0, HEAD_PAD - keep))).reshape(r, B_HEADS * HEAD_PAD)


def _rope_tables(s):
    inv = ROPE_THETA ** (-jnp.arange(0, B_ROPE, 2, dtype=jnp.float32) / B_ROPE)
    ang = jnp.arange(s, dtype=jnp.float32)[:, None] * inv[None, :]
    cos, sin = jnp.cos(ang), jnp.sin(ang)
    ones = jnp.ones((s, B_NOPE), F32)
    zeros = jnp.zeros((s, B_NOPE), F32)
    tail1 = jnp.ones((s, HEAD_PAD - B_NOPE - B_ROPE), F32)
    tail0 = jnp.zeros((s, HEAD_PAD - B_NOPE - B_ROPE), F32)
    return (jnp.concatenate([ones, cos, cos, tail1], axis=1),
            jnp.concatenate([zeros, -sin, sin, tail0], axis=1))


def kernel(x, c, w_ada, b_ada, norm1_g, w_in, q_norm_g, w_uq, kv_norm_g, w_ukv, rel_table, a_out_g, b_out_g,
           w_o, norm2_g, w_router, e_bias, w1, w3, w2, ws1, ws3, ws2, final_g):
    b, s, d = x.shape
    n = b * s
    assert w_ada.shape[0] == 1, "single layer"
    x2 = x.reshape(n, d)

    mod = _ada(c, w_ada[0], b_ada[0])
    mod3 = mod.reshape(b * 6, 1, d)
    bias = _bias_tiles(rel_table)

    wi = w_in[0]
    c_kpe = 3 * A_WIDTH + Q_LORA + KV_LORA
    kpe_cols = jnp.pad(wi[:, c_kpe:], ((0, 0), (B_NOPE, HEAD_PAD - B_NOPE - B_ROPE)))
    w_in_ext = jnp.concatenate([wi[:, :c_kpe], kpe_cols], axis=1).astype(BF16)
    w_uq_p = _place_heads(w_uq[0], B_NOPE + B_ROPE, B_NOPE + B_ROPE).astype(BF16)
    w_uk_p = _place_heads(w_ukv[0], B_NOPE + B_VDIM, B_NOPE).astype(BF16)
    w_v = w_ukv[0].reshape(KV_LORA, B_HEADS, B_NOPE + B_VDIM)[:, :, B_NOPE:].reshape(KV_LORA, B_WIDTH).astype(BF16)
    cos_t, sin_t = _rope_tables(s)

    qa, ka, va, qb, kb, vb = _inproj(x2, mod3, norm1_g, w_in_ext, q_norm_g, w_uq_p, kv_norm_g, w_uk_p, w_v,
                                     cos_t, sin_t, b, s)
    a_out = _dilated(qa.reshape(b, s, A_WIDTH), ka.reshape(b, s, A_WIDTH), va.reshape(b, s, A_WIDTH), bias)
    b_out = _mla(qb, kb, vb)

    w_router_t = _slab_order(w_router[0].T)
    x1s, h2p, scores_t = _outproj(x2, a_out.reshape(n, A_WIDTH), b_out.reshape(n, B_WIDTH), a_out_g, b_out_g,
                                  w_o[0].astype(BF16), mod3, norm2_g, w_router_t,
                                  ws1[0].astype(BF16), ws3[0].astype(BF16), ws2[0].astype(BF16), s)

    eidx_t, gates_t, counts_slab = _route(scores_t, _slab_order(e_bias.reshape(N_EXPERTS, 1)))

    counts = _expert_order(counts_slab)[:, 0]
    padded = (counts + MOE_BLK - 1) // MOE_BLK * MOE_BLK
    pends = jnp.cumsum(padded)
    pstart = pends - padded
    nk = n * TOP_K
    nblk = -(-(nk + N_EXPERTS * (MOE_BLK - 1)) // MOE_BLK)
    rows_total = nblk * MOE_BLK
    blk_row = jnp.arange(nblk, dtype=I32) * MOE_BLK
    blk_e = jnp.minimum(jnp.sum((pends[None, :] <= blk_row[:, None]).astype(I32), axis=1), N_EXPERTS - 1)
    seg_end = (pstart + counts)[blk_e]
    zero_blk = ((blk_row >= pends[-1]) | (seg_end < blk_row + MOE_BLK)).astype(I32)
    nused = (pends[-1] // MOE_BLK).astype(I32).reshape(1)

    dest_t = _dest(eidx_t, _slab_order(pstart.astype(F32).reshape(N_EXPERTS, 1)))
    xg = _dispatch(dest_t, zero_blk, h2p, rows_total)
    cstart = jnp.concatenate([jnp.zeros((1,), I32), (pends // MOE_BLK).astype(I32)])
    y = _experts(cstart, nused, xg, w1[0], w3[0], w2[0])
    nplane = y.shape[0]
    gidx = (dest_t.reshape(1, nk) + (jnp.arange(nplane, dtype=I32) * rows_total)[:, None]).reshape(nplane * nk)
    yg = _sc_gather(y.reshape(nplane * rows_total, LANES), gidx).reshape(nplane, TOP_K, n, LANES)
    out = _combine(yg, gates_t.T, x1s, mod3, final_g.reshape(1, d), s)
    return out.reshape(b, s, d)
```

```python
import functools
import math

import jax
import jax.numpy as jnp
from jax import lax
from jax.experimental import pallas as pl
from jax.experimental.pallas import tpu as pltpu
from jax.experimental.pallas import tpu_sc as plsc

F32 = jnp.float32
BF16 = jnp.bfloat16
U32 = jnp.uint32
I32 = jnp.int32
HIGHEST = lax.Precision.HIGHEST

D_MODEL = 1024
A_HEADS = 8
A_HEAD_DIM = 64
A_WIDTH = A_HEADS * A_HEAD_DIM
A_PATTERNS = ((128, 1), (512, 4), (2048, 16))
A_RADIUS = 64
REL_BUCKETS = 32
REL_MAX_DIST = 1024
B_HEADS = 8
B_NOPE = 64
B_ROPE = 32
B_VDIM = 64
B_WIDTH = B_HEADS * B_VDIM
Q_LORA = 384
KV_LORA = 256
ROPE_THETA = 10000.0
N_EXPERTS = 256
TOP_K = 8
N_GROUPS = 8
GROUP_SIZE = N_EXPERTS // N_GROUPS
TOPK_GROUPS = 4
EXPERT_FF = 256
SHARED_FF = 256
ROUTED_SCALE = 2.5
EPS = 1e-6
NEG_INF = -1e30
LOG2E = math.log2(math.e)

LANES = 128
SUBLANES = 8
HEAD_PAD = 128
IN_COLS_EXT = 3 * A_WIDTH + Q_LORA + KV_LORA + HEAD_PAD

TM_INPROJ = 512
TM_OUTPROJ = 256
DIL_QB = 128
DIL_KW = DIL_QB + 2 * A_RADIUS
DIL_UNROLL = 4
MLA_TQ = 256
ROUTE_T = 256
MOE_BLK = 256
EXP_XBUF = 4
SC_WINDOW = 128
COMB_T = 256
VMEM_LIMIT = 56 * 1024 * 1024


def _cparams(sem):
    return pltpu.CompilerParams(dimension_semantics=sem, vmem_limit_bytes=VMEM_LIMIT)


def _rms(x, g):
    return x * lax.rsqrt(jnp.mean(x * x, axis=-1, keepdims=True) + EPS) * g


def _silu(x):
    return x * jax.nn.sigmoid(x)


def _pack_rows(x):
    half = x.shape[1] // 2
    bits = lax.bitcast_convert_type(x.astype(BF16).astype(F32), U32)
    return (bits[:, :half] >> 16) | bits[:, half:]


def _unpack_rows(w):
    lo = lax.bitcast_convert_type(w << 16, F32)
    hi = lax.bitcast_convert_type(w & jnp.uint32(0xFFFF0000), F32)
    return lo, hi


def _ada_kernel(c_ref, w_ref, b_ref, o_ref):
    o_ref[...] = jnp.dot(_silu(c_ref[...]), w_ref[...], precision=HIGHEST,
                         preferred_element_type=F32) + b_ref[...]


def _ada(c, w_ada, b_ada):
    b, d = c.shape
    n6 = w_ada.shape[1] // d
    return pl.pallas_call(
        _ada_kernel,
        grid=(n6,),
        in_specs=[pl.BlockSpec((b, d), lambda j: (0, 0)),
                  pl.BlockSpec((d, d), lambda j: (0, j)),
                  pl.BlockSpec((1, d), lambda j: (0, j))],
        out_specs=pl.BlockSpec((b, d), lambda j: (0, j)),
        out_shape=jax.ShapeDtypeStruct((b, n6 * d), F32),
        compiler_params=_cparams(("parallel",)),
        name="ada",
    )(c, w_ada, b_ada.reshape(1, -1))


def _t5_bucket(rel):
    half = REL_BUCKETS // 2
    max_exact = half // 2
    ret = jnp.where(rel > 0, half, 0)
    n = jnp.abs(rel)
    nf = jnp.maximum(n, 1).astype(jnp.float32)
    large = max_exact + (jnp.log(nf / max_exact) / math.log(REL_MAX_DIST / max_exact)
                         * (half - max_exact)).astype(jnp.int32)
    large = jnp.minimum(large, half - 1)
    return ret + jnp.where(n < max_exact, n, large)


DIL_SHIFTS = (A_RADIUS, 0, -A_RADIUS)


def _bucket_tiles():
    qi = jnp.arange(DIL_QB, dtype=jnp.int32)[:, None]
    ki = jnp.arange(DIL_KW, dtype=jnp.int32)[None, :]
    tiles = []
    for _, dilation in A_PATTERNS:
        for shift in DIL_SHIFTS:
            off = ki + shift - A_RADIUS - qi
            bkt = _t5_bucket(off * dilation)
            tiles.append(jnp.where(jnp.abs(off) <= A_RADIUS, bkt, -1))
    return jnp.stack(tiles, axis=0)


def _bias_kernel(tab_ref, bkt_ref, o_ref):
    bkt = bkt_ref[0]
    for h in range(A_HEADS):
        acc = jnp.full(bkt.shape, NEG_INF, F32)
        for b in range(REL_BUCKETS):
            acc = jnp.where(bkt == b, tab_ref[b, h] * LOG2E, acc)
        o_ref[0, h] = acc


def _bias_tiles(rel_table):
    bkt = _bucket_tiles()
    nt = bkt.shape[0]
    return pl.pallas_call(
        _bias_kernel,
        grid=(nt,),
        in_specs=[pl.BlockSpec(memory_space=pltpu.SMEM),
                  pl.BlockSpec((1, DIL_QB, DIL_KW), lambda t: (t, 0, 0))],
        out_specs=pl.BlockSpec((1, A_HEADS, DIL_QB, DIL_KW), lambda t: (t, 0, 0, 0)),
        out_shape=jax.ShapeDtypeStruct((nt, A_HEADS, DIL_QB, DIL_KW), F32),
        compiler_params=_cparams(("parallel",)),
        name="bias",
    )(rel_table, bkt)


def _rope(x, cos, sin, lane_lt_mid):
    half = B_ROPE // 2
    rot = jnp.where(lane_lt_mid, pltpu.roll(x, HEAD_PAD - half, 1), pltpu.roll(x, half, 1))
    return x * cos + rot * sin


def _inproj_kernel(x_ref, sc_ref, sh_ref, g1_ref, win_ref, qg_ref, wuq_ref, kvg_ref, wuk_ref, wv_ref,
                   cos_ref, sin_ref, qa_ref, ka_ref, va_ref, qb_ref, kb_ref, vb_ref):
    x = x_ref[...]
    h = _rms(x, g1_ref[...]) * (1.0 + sc_ref[...]) + sh_ref[...]
    proj = jnp.dot(h.astype(BF16), win_ref[...], preferred_element_type=F32)
    aw = A_WIDTH
    qa_ref[...] = proj[:, 0:aw] * (LOG2E / math.sqrt(A_HEAD_DIM))
    ka_ref[...] = proj[:, aw:2 * aw]
    va_ref[...] = proj[:, 2 * aw:3 * aw]
    c0 = 3 * aw
    q_lat = proj[:, c0:c0 + Q_LORA]
    kv_lat = proj[:, c0 + Q_LORA:c0 + Q_LORA + KV_LORA]
    kpe = proj[:, c0 + Q_LORA + KV_LORA:]
    qn = _rms(q_lat, qg_ref[...]).astype(BF16)
    kvn = _rms(kv_lat, kvg_ref[...]).astype(BF16)
    qm = jnp.dot(qn, wuq_ref[...], preferred_element_type=F32)
    kn = jnp.dot(kvn, wuk_ref[...], preferred_element_type=F32)
    vv = jnp.dot(kvn, wv_ref[...], preferred_element_type=F32)
    cos = cos_ref[...]
    sin = sin_ref[...]
    lane = lax.broadcasted_iota(jnp.int32, cos.shape, 1)
    lt_mid = lane < (B_NOPE + B_ROPE // 2)
    qscale = LOG2E / math.sqrt(B_NOPE + B_ROPE)
    for hd in range(B_HEADS):
        sl = slice(hd * HEAD_PAD, (hd + 1) * HEAD_PAD)
        qb_ref[hd] = (_rope(qm[:, sl], cos, sin, lt_mid) * qscale).astype(BF16)
        kb_ref[hd] = _rope(kn[:, sl] + kpe, cos, sin, lt_mid).astype(BF16)
    for p in range(B_HEADS // 2):
        vb_ref[p] = vv[:, p * LANES:(p + 1) * LANES].astype(BF16)


def _inproj(x2, mod3, norm1_g, w_in_ext, q_norm_g, w_uq_p, kv_norm_g, w_uk_p, w_v, cos_t, sin_t, b, s):
    n, d = x2.shape
    tm = TM_INPROJ
    tpb = s // tm
    row = lambda i: (i, 0)
    const = lambda i: (0, 0)
    hm = lambda i: (i // tpb, 0, i % tpb, 0)
    return pl.pallas_call(
        _inproj_kernel,
        grid=(n // tm,),
        in_specs=[pl.BlockSpec((tm, d), row),
                  pl.BlockSpec((None, 1, d), lambda i: ((i // tpb) * 6 + 1, 0, 0)),
                  pl.BlockSpec((None, 1, d), lambda i: ((i // tpb) * 6 + 0, 0, 0)),
                  pl.BlockSpec((1, d), const),
                  pl.BlockSpec(w_in_ext.shape, const),
                  pl.BlockSpec((1, Q_LORA), const),
                  pl.BlockSpec(w_uq_p.shape, const),
                  pl.BlockSpec((1, KV_LORA), const),
                  pl.BlockSpec(w_uk_p.shape, const),
                  pl.BlockSpec(w_v.shape, const),
                  pl.BlockSpec((tm, HEAD_PAD), lambda i: (i % tpb, 0)),
                  pl.BlockSpec((tm, HEAD_PAD), lambda i: (i % tpb, 0))],
        out_specs=[pl.BlockSpec((tm, A_WIDTH), row),
                   pl.BlockSpec((tm, A_WIDTH), row),
                   pl.BlockSpec((tm, A_WIDTH), row),
                   pl.BlockSpec((None, B_HEADS, tm, HEAD_PAD), hm),
                   pl.BlockSpec((None, B_HEADS, tm, HEAD_PAD), hm),
                   pl.BlockSpec((None, B_HEADS // 2, tm, LANES), hm)],
        out_shape=[jax.ShapeDtypeStruct((n, A_WIDTH), F32),
                   jax.ShapeDtypeStruct((n, A_WIDTH), F32),
                   jax.ShapeDtypeStruct((n, A_WIDTH), F32),
                   jax.ShapeDtypeStruct((b, B_HEADS, s, HEAD_PAD), BF16),
                   jax.ShapeDtypeStruct((b, B_HEADS, s, HEAD_PAD), BF16),
                   jax.ShapeDtypeStruct((b, B_HEADS // 2, s, LANES), BF16)],
        compiler_params=_cparams(("parallel",)),
        name="inproj",
    )(x2, mod3, mod3, norm1_g, w_in_ext, q_norm_g, w_uq_p, kv_norm_g, w_uk_p, w_v, cos_t, sin_t)


def _dil_block(q_ref, k_ref, v_ref, bias_ref, o_scr, l_scr, pi, dil, nblk, job):
    sub_len = nblk * DIL_QB
    r = job // nblk
    bi = job % nblk
    q0 = bi * DIL_QB
    ws = jnp.clip(q0 - A_RADIUS, 0, sub_len - DIL_KW)
    var = jnp.where(bi == 0, 0, jnp.where(bi == nblk - 1, 2, 1))
    if dil == 1:
        qsl = pl.ds(pl.multiple_of(q0, DIL_QB), DIL_QB)
        ksl = pl.ds(pl.multiple_of(ws, A_RADIUS), DIL_KW)
    else:
        qsl = pl.ds(r + dil * q0, DIL_QB, stride=dil)
        ksl = pl.ds(r + dil * ws, DIL_KW, stride=dil)
    q = q_ref[qsl, :]
    kw = k_ref[ksl, :].astype(BF16)
    vw = v_ref[ksl, :].astype(BF16)
    lo = lax.broadcasted_iota(jnp.int32, q.shape, 1) < A_HEAD_DIM
    outs, lses = [], []
    for hh in range(2):
        qm = jnp.where(lo if hh == 0 else jnp.logical_not(lo), q, 0.0).astype(BF16)
        sc = lax.dot_general(qm, kw, (((1,), (1,)), ((), ())), preferred_element_type=F32)
        sc = sc + bias_ref[pi * 3 + var, hh]
        m = jnp.max(sc, axis=-1, keepdims=True)
        p = jnp.exp2(sc - m)
        l = jnp.sum(p, axis=-1, keepdims=True)
        o = jnp.dot(p.astype(BF16), vw, preferred_element_type=F32)
        outs.append(o / l)
        lses.append(m + jnp.log2(l))
    o_scr[pi, qsl, :] = jnp.where(lo, outs[0], outs[1])
    l_scr[pi, qsl, :] = jnp.where(lo, lses[0], lses[1])


def _dilated_kernel(q_ref, k_ref, v_ref, bias_ref, out_ref, o_scr, l_scr):
    s = q_ref.shape[0]
    njobs = s // DIL_QB
    for pi, (_, dil) in enumerate(A_PATTERNS):
        blk = functools.partial(_dil_block, q_ref, k_ref, v_ref, bias_ref, o_scr, l_scr, pi, dil,
                                s // dil // DIL_QB)

        def group(g, c, blk=blk):
            for u in range(DIL_UNROLL):
                blk(g * DIL_UNROLL + u)
            return c

        lax.fori_loop(0, njobs // DIL_UNROLL, group, 0)

    chunk = 512

    def comb(i, c):
        rows = pl.ds(pl.multiple_of(i * chunk, chunk), chunk)
        l0, l1, l2 = l_scr[0, rows, :], l_scr[1, rows, :], l_scr[2, rows, :]
        mx = jnp.maximum(jnp.maximum(l0, l1), l2)
        e0, e1, e2 = jnp.exp2(l0 - mx), jnp.exp2(l1 - mx), jnp.exp2(l2 - mx)
        num = e0 * o_scr[0, rows, :] + e1 * o_scr[1, rows, :] + e2 * o_scr[2, rows, :]
        out_ref[rows, :] = num / (e0 + e1 + e2)
        return c

    lax.fori_loop(0, s // chunk, comb, 0)


def _dilated(qa, ka, va, bias):
    b, s, _ = qa.shape
    npair = A_WIDTH // LANES
    assert (s // DIL_QB) % DIL_UNROLL == 0
    for _, dil in A_PATTERNS:
        assert (s // dil) % DIL_QB == 0 and s // dil >= DIL_KW
    blk = pl.BlockSpec((None, s, LANES), lambda bi, p: (bi, 0, p))
    return pl.pallas_call(
        _dilated_kernel,
        grid=(b, npair),
        in_specs=[blk, blk, blk,
                  pl.BlockSpec((bias.shape[0], 2, DIL_QB, DIL_KW), lambda bi, p: (0, p, 0, 0))],
        out_specs=blk,
        out_shape=jax.ShapeDtypeStruct((b, s, A_WIDTH), F32),
        scratch_shapes=[pltpu.VMEM((len(A_PATTERNS), s, LANES), F32),
                        pltpu.VMEM((len(A_PATTERNS), s, LANES), F32)],
        compiler_params=_cparams(("parallel", "parallel")),
        name="dilated",
    )(qa, ka, va, bias)


def _mla_kernel(q_ref, k_ref, v_ref, o_ref):
    v = v_ref[...]
    outs = []
    for hh in range(2):
        sc = lax.dot_general(q_ref[hh], k_ref[hh], (((1,), (1,)), ((), ())), preferred_element_type=F32)
        m = jnp.max(sc, axis=-1, keepdims=True)
        p = jnp.exp2(sc - m)
        l = jnp.sum(p, axis=-1, keepdims=True)
        outs.append(jnp.dot(p.astype(BF16), v, preferred_element_type=F32) / l)
    lo = lax.broadcasted_iota(jnp.int32, outs[0].shape, 1) < B_VDIM
    o_ref[...] = jnp.where(lo, outs[0], outs[1])


def _mla(qb, kb, vb):
    b, h, s, _ = qb.shape
    npair = h // 2
    tq = MLA_TQ
    return pl.pallas_call(
        _mla_kernel,
        grid=(b, npair, s // tq),
        in_specs=[pl.BlockSpec((None, 2, tq, HEAD_PAD), lambda bi, p, qi: (bi, p, qi, 0)),
                  pl.BlockSpec((None, 2, s, HEAD_PAD), lambda bi, p, qi: (bi, p, 0, 0)),
                  pl.BlockSpec((None, None, s, LANES), lambda bi, p, qi: (bi, p, 0, 0))],
        out_specs=pl.BlockSpec((None, tq, LANES), lambda bi, p, qi: (bi, qi, p)),
        out_shape=jax.ShapeDtypeStruct((b, s, B_WIDTH), F32),
        compiler_params=_cparams(("parallel", "parallel", "arbitrary")),
        name="mla",
    )(qb, kb, vb)


def _outproj_kernel(x_ref, a_ref, b_ref, ag_ref, bg_ref, wo_ref, g1_ref, n2_ref, sc_ref, sh_ref, g2_ref,
                    wrt_ref, ws1_ref, ws3_ref, ws2_ref, x1s_ref, h2p_ref, scores_ref):
    an = _rms(a_ref[...], ag_ref[...])
    bn = _rms(b_ref[...], bg_ref[...])
    mix = jnp.concatenate([an, bn], axis=-1).astype(BF16)
    x1 = x_ref[...] + g1_ref[...] * jnp.dot(mix, wo_ref[...], preferred_element_type=F32)
    h2 = _rms(x1, n2_ref[...]) * (1.0 + sc_ref[...]) + sh_ref[...]
    packed = _pack_rows(h2)
    for cg in range(h2p_ref.shape[0]):
        h2p_ref[cg] = packed[:, cg * LANES:(cg + 1) * LANES]
    logits = lax.dot_general(wrt_ref[...], h2, (((1,), (1,)), ((), ())), precision=HIGHEST,
                             preferred_element_type=F32)
    scores_ref[...] = jax.nn.sigmoid(logits)
    h2b = h2.astype(BF16)
    hid = _silu(jnp.dot(h2b, ws1_ref[...], preferred_element_type=F32)) * jnp.dot(
        h2b, ws3_ref[...], preferred_element_type=F32)
    shared = jnp.dot(hid.astype(BF16), ws2_ref[...], preferred_element_type=F32)
    x1s_ref[...] = x1 + g2_ref[...] * shared


def _outproj(x2, a_out, b_out, a_out_g, b_out_g, w_o, mod3, norm2_g, w_router_t, ws1, ws3, ws2, s):
    n, d = x2.shape
    tm = TM_OUTPROJ
    tpb = s // tm
    row = lambda i: (i, 0)
    const = lambda i: (0, 0)
    modspec = lambda j: pl.BlockSpec((None, 1, d), lambda i: ((i // tpb) * 6 + j, 0, 0))
    return pl.pallas_call(
        _outproj_kernel,
        grid=(n // tm,),
        in_specs=[pl.BlockSpec((tm, d), row),
                  pl.BlockSpec((tm, A_WIDTH), row),
                  pl.BlockSpec((tm, B_WIDTH), row),
                  pl.BlockSpec((1, A_WIDTH), const),
                  pl.BlockSpec((1, B_WIDTH), const),
                  pl.BlockSpec(w_o.shape, const),
                  modspec(2),
                  pl.BlockSpec((1, d), const),
                  modspec(4), modspec(3), modspec(5),
                  pl.BlockSpec(w_router_t.shape, const),
                  pl.BlockSpec(ws1.shape, const),
                  pl.BlockSpec(ws3.shape, const),
                  pl.BlockSpec(ws2.shape, const)],
        out_specs=[pl.BlockSpec((tm, d), row),
                   pl.BlockSpec((d // 2 // LANES, tm, LANES), lambda i: (0, i, 0)),
                   pl.BlockSpec((N_EXPERTS, tm), lambda i: (0, i))],
        out_shape=[jax.ShapeDtypeStruct((n, d), F32),
                   jax.ShapeDtypeStruct((d // 2 // LANES, n, LANES), U32),
                   jax.ShapeDtypeStruct((N_EXPERTS, n), F32)],
        compiler_params=_cparams(("parallel",)),
        name="outproj",
    )(x2, a_out, b_out, a_out_g, b_out_g, w_o, mod3, norm2_g, mod3, mod3, mod3, w_router_t, ws1, ws3, ws2)


def _slab_order(v):
    return v.reshape((N_GROUPS, GROUP_SIZE) + v.shape[1:]).swapaxes(0, 1).reshape(v.shape)


def _expert_order(v):
    return v.reshape((GROUP_SIZE, N_GROUPS) + v.shape[1:]).swapaxes(0, 1).reshape(v.shape)


def _sublane_all(x, op):
    for sh in (4, 2, 1):
        x = op(x, pltpu.roll(x, sh, 0))
    return x


def _route_kernel(st_ref, bias_ref, eidx_ref, gate_ref, cnt_ref):
    nsl = GROUP_SIZE
    t = st_ref.shape[1]
    sub = lax.broadcasted_iota(I32, (SUBLANES, t), 0)
    ninf = -jnp.inf
    big = jnp.int32(1 << 30)
    sc = [st_ref[j * SUBLANES:(j + 1) * SUBLANES, :] for j in range(nsl)]
    sel = [sc[j] + bias_ref[j * SUBLANES:(j + 1) * SUBLANES, :] for j in range(nsl)]
    eid = [sub * GROUP_SIZE + j for j in range(nsl)]

    m1 = sel[0]
    m2 = jnp.full_like(m1, ninf)
    for j in range(1, nsl):
        m2 = jnp.maximum(m2, jnp.minimum(m1, sel[j]))
        m1 = jnp.maximum(m1, sel[j])
    gs = m1 + m2

    rank = jnp.zeros((SUBLANES, t), I32)
    for sh in range(1, N_GROUPS):
        other = pltpu.roll(gs, sh, 0)
        ahead = (other > gs) | ((other == gs) & (sub >= sh))
        rank = rank + ahead.astype(I32)
    gmask = rank < TOPK_GROUPS

    msel = [jnp.where(gmask, sel[j], ninf) for j in range(nsl)]
    hits = [jnp.zeros((SUBLANES, t), I32) for _ in range(nsl)]
    eidx = jnp.zeros((TOP_K, t), I32)
    gates = jnp.zeros((TOP_K, t), F32)
    for k in range(TOP_K):
        mx = msel[0]
        for j in range(1, nsl):
            mx = jnp.maximum(mx, msel[j])
        mx = _sublane_all(mx, jnp.maximum)
        cand = jnp.where(msel[0] == mx, eid[0], big)
        for j in range(1, nsl):
            cand = jnp.minimum(cand, jnp.where(msel[j] == mx, eid[j], big))
        idx = _sublane_all(cand, jnp.minimum)
        gk = jnp.zeros((SUBLANES, t), F32)
        for j in range(nsl):
            hit = eid[j] == idx
            gk = gk + jnp.where(hit, sc[j], 0.0)
            msel[j] = jnp.where(hit, ninf, msel[j])
            hits[j] = hits[j] + hit.astype(I32)
        gk = _sublane_all(gk, jnp.add)
        eidx = jnp.where(sub == k, idx, eidx)
        gates = jnp.where(sub == k, gk, gates)
    gsum = _sublane_all(gates, jnp.add)
    eidx_ref[...] = eidx
    gate_ref[...] = gates / gsum * ROUTED_SCALE

    @pl.when(pl.program_id(0) == 0)
    def _():
        cnt_ref[...] = jnp.zeros_like(cnt_ref)

    for j in range(nsl):
        cnt_ref[j * SUBLANES:(j + 1) * SUBLANES, :] += jnp.sum(hits[j].astype(F32), axis=1,
                                                               keepdims=True).astype(I32)


def _route(scores_t, e_bias_slab):
    e, n = scores_t.shape
    t = ROUTE_T
    return pl.pallas_call(
        _route_kernel,
        grid=(n // t,),
        in_specs=[pl.BlockSpec((e, t), lambda i: (0, i)),
                  pl.BlockSpec((e, 1), lambda i: (0, 0))],
        out_specs=[pl.BlockSpec((TOP_K, t), lambda i: (0, i)),
                   pl.BlockSpec((TOP_K, t), lambda i: (0, i)),
                   pl.BlockSpec((e, 1), lambda i: (0, 0))],
        out_shape=[jax.ShapeDtypeStruct((TOP_K, n), I32),
                   jax.ShapeDtypeStruct((TOP_K, n), F32),
                   jax.ShapeDtypeStruct((e, 1), I32)],
        compiler_params=_cparams(("arbitrary",)),
        name="route",
    )(scores_t, e_bias_slab)


def _dest_kernel(eidx_ref, pstart_ref, dest_ref, carry_ref):
    @pl.when(pl.program_id(0) == 0)
    def _():
        carry_ref[...] = jnp.zeros_like(carry_ref)

    nsl = GROUP_SIZE
    t = eidx_ref.shape[1]
    sub = lax.broadcasted_iota(I32, (SUBLANES, t), 0)
    eid = [sub * GROUP_SIZE + j for j in range(nsl)]
    ek = [eidx_ref[k:k + 1, :] for k in range(TOP_K)]
    slabs = []
    for j in range(nsl):
        oh = jnp.zeros((SUBLANES, t), F32)
        for k in range(TOP_K):
            oh = oh + (eid[j] == ek[k]).astype(F32)
        slabs.append(oh)
    onehot = jnp.concatenate(slabs, axis=0)
    row = lax.broadcasted_iota(I32, (t, t), 0)
    col = lax.broadcasted_iota(I32, (t, t), 1)
    upper = (row < col).astype(BF16)
    before = jnp.dot(onehot.astype(BF16), upper, preferred_element_type=F32)
    base = before + carry_ref[...] + pstart_ref[...]
    dest = jnp.zeros((TOP_K, t), I32)
    for k in range(TOP_K):
        acc = jnp.zeros((SUBLANES, t), F32)
        for j in range(nsl):
            acc = acc + jnp.where(eid[j] == ek[k], base[j * SUBLANES:(j + 1) * SUBLANES, :], 0.0)
        dk = _sublane_all(acc, jnp.add).astype(I32)
        dest = jnp.where(sub == k, dk, dest)
    dest_ref[...] = dest
    carry_ref[...] += jnp.sum(onehot, axis=1, keepdims=True)


def _dest(eidx_t, pstart_slab):
    n = eidx_t.shape[1]
    t = ROUTE_T
    return pl.pallas_call(
        _dest_kernel,
        grid=(n // t,),
        in_specs=[pl.BlockSpec((TOP_K, t), lambda i: (0, i)),
                  pl.BlockSpec((N_EXPERTS, 1), lambda i: (0, 0))],
        out_specs=pl.BlockSpec((TOP_K, t), lambda i: (0, i)),
        out_shape=jax.ShapeDtypeStruct((TOP_K, n), I32),
        scratch_shapes=[pltpu.VMEM((N_EXPERTS, 1), F32)],
        compiler_params=_cparams(("arbitrary",)),
        name="dest",
    )(eidx_t, pstart_slab)


def _sc_scatter(x, idx, rows_out, nplane, nslot):
    num = idx.shape[0]
    nwin = x.shape[0] // nplane // SC_WINDOW
    mesh = plsc.VectorSubcoreMesh(core_axis_name="core", subcore_axis_name="subcore")

    @pl.kernel(out_type=jax.ShapeDtypeStruct((rows_out, x.shape[1]), x.dtype), mesh=mesh, scratch_types=[])
    def scatter(x_hbm, i_hbm, o_hbm):
        def body(x_vmem, i_vmem):
            pltpu.sync_copy(x_vmem, o_hbm.at[i_vmem.at[0]])

        pltpu.emit_pipeline(
            body,
            grid=(num // SC_WINDOW,),
            in_specs=[pl.BlockSpec((SC_WINDOW, x.shape[1]),
                                   index_map=lambda i: ((i // (nslot * nwin)) * nwin + i % nwin, 0)),
                      pl.BlockSpec((1, SC_WINDOW), index_map=lambda i: (0, i))],
            out_specs=[],
            core_axis_name=("core", "subcore"),
            dimension_semantics=(pltpu.PARALLEL,),
        )(x_hbm, i_hbm)

    return scatter(x, idx.reshape(1, num))


def _expert_kernel(cstart_ref, nused_ref, valid_ref, w1_ref, w3_ref, w2_ref, xg_ref, y_ref,
                   w1b, w3b, w2b, xbuf, ybuf, xsem, ysem):
    e = pl.program_id(0)
    c0 = cstart_ref[e]
    c1 = cstart_ref[e + 1]
    nused = nused_ref[0]
    nchunks = xg_ref.shape[1] // MOE_BLK

    def rows(g):
        start = g * MOE_BLK
        return pl.ds(start if isinstance(g, int) else pl.multiple_of(start, MOE_BLK), MOE_BLK)

    def xcopies(g):
        slot = g % EXP_XBUF
        return [pltpu.make_async_copy(xg_ref.at[c, rows(g)],
                                      xbuf.at[slot, pl.ds(0, MOE_BLK), pl.ds(c * LANES, LANES)], xsem.at[slot])
                for c in range(xg_ref.shape[0])]

    def ycopies(g, slot):
        return [pltpu.make_async_copy(ybuf.at[slot, pl.ds(0, MOE_BLK), pl.ds(c * LANES, LANES)],
                                      y_ref.at[c, rows(g)], ysem.at[slot])
                for c in range(y_ref.shape[0])]

    @pl.when(e == 0)
    def _():
        for j in range(EXP_XBUF - 1):
            @pl.when(j < nused)
            def _():
                for cp in xcopies(j):
                    cp.start()

    @pl.when(c1 > c0)
    def _():
        w1b[...] = w1_ref[...].astype(BF16)
        w3b[...] = w3_ref[...].astype(BF16)
        w2b[...] = w2_ref[...].astype(BF16)

        def chunk(g, carry):
            @pl.when(g + (EXP_XBUF - 1) < nused)
            def _():
                for cp in xcopies(g + (EXP_XBUF - 1)):
                    cp.start()

            for cp in xcopies(g):
                cp.wait()
            words = xbuf[g % EXP_XBUF]
            row_id = lax.broadcasted_iota(I32, words.shape, 0)
            lo, hi = _unpack_rows(jnp.where(row_id < valid_ref[g], words, jnp.uint32(0)))
            xb = jnp.concatenate([lo, hi], axis=1).astype(BF16)
            hid = _silu(jnp.dot(xb, w1b[...], preferred_element_type=F32)) * jnp.dot(
                xb, w3b[...], preferred_element_type=F32)
            yp = _pack_rows(jnp.dot(hid.astype(BF16), w2b[...], preferred_element_type=F32))
            yslot = g % 2

            @pl.when(g >= 2)
            def _():
                for cp in ycopies(g - 2, yslot):
                    cp.wait()

            ybuf[yslot] = yp
            for cp in ycopies(g, yslot):
                cp.start()
            return carry

        lax.fori_loop(c0, c1, chunk, 0)

    @pl.when(e == pl.num_programs(0) - 1)
    def _():
        for back in (2, 1):
            g = nused - back

            @pl.when(g >= 0)
            def _():
                for cp in ycopies(g, g % 2):
                    cp.wait()

        ybuf[0] = jnp.zeros(ybuf.shape[1:], ybuf.dtype)

        def zstart(g, c):
            for cp in ycopies(g, 0):
                cp.start()
            return c

        def zwait(g, c):
            for cp in ycopies(g, 0):
                cp.wait()
            return c

        lax.fori_loop(nused, nchunks, zstart, 0)
        lax.fori_loop(nused, nchunks, zwait, 0)


def _experts(cstart, nused, valid, xg, w1, w3, w2):
    nplane, rows, _ = xg.shape
    dh = nplane * LANES
    ne, d, f = w1.shape
    grid_spec = pltpu.PrefetchScalarGridSpec(
        num_scalar_prefetch=3,
        grid=(ne,),
        in_specs=[pl.BlockSpec((None, d, f), lambda e, cs, nu, va: (e, 0, 0)),
                  pl.BlockSpec((None, d, f), lambda e, cs, nu, va: (e, 0, 0)),
                  pl.BlockSpec((None, f, d), lambda e, cs, nu, va: (e, 0, 0)),
                  pl.BlockSpec(memory_space=pl.ANY)],
        out_specs=pl.BlockSpec(memory_space=pl.ANY),
        scratch_shapes=[pltpu.VMEM((d, f), BF16), pltpu.VMEM((d, f), BF16), pltpu.VMEM((f, d), BF16),
                        pltpu.VMEM((EXP_XBUF, MOE_BLK, dh), U32), pltpu.VMEM((2, MOE_BLK, dh), U32),
                        pltpu.SemaphoreType.DMA((EXP_XBUF,)), pltpu.SemaphoreType.DMA((2,))],
    )
    return pl.pallas_call(
        _expert_kernel,
        grid_spec=grid_spec,
        out_shape=jax.ShapeDtypeStruct((dh // LANES, rows, LANES), U32),
        compiler_params=_cparams(("arbitrary",)),
        name="experts",
    )(cstart, nused, valid, w1, w3, w2, xg)


def _sc_gather(x, idx):
    num = idx.shape[0]
    mesh = plsc.VectorSubcoreMesh(core_axis_name="core", subcore_axis_name="subcore")

    @pl.kernel(out_type=jax.ShapeDtypeStruct((num, x.shape[1]), x.dtype), mesh=mesh)
    def gather(x_hbm, i_hbm, o_hbm):
        def body(i_vmem, o_vmem):
            pltpu.sync_copy(x_hbm.at[i_vmem.at[0]], o_vmem)

        pltpu.emit_pipeline(
            body,
            grid=(num // SC_WINDOW,),
            in_specs=[pl.BlockSpec((1, SC_WINDOW), index_map=lambda i: (0, i))],
            out_specs=[pl.BlockSpec((SC_WINDOW, x.shape[1]), index_map=lambda i: (i, 0))],
            core_axis_name=("core", "subcore"),
            dimension_semantics=(pltpu.PARALLEL,),
        )(i_hbm, o_hbm)

    return gather(x, idx.reshape(1, num))


def _combine_kernel(yg_ref, gate_ref, x1s_ref, g2_ref, fg_ref, out_ref):
    gates = gate_ref[...]
    nch = yg_ref.shape[0]
    r_lo = [None] * nch
    r_hi = [None] * nch
    for k in range(TOP_K):
        gk = gates[:, k:k + 1]
        for c in range(nch):
            lo, hi = _unpack_rows(yg_ref[c, k])
            r_lo[c] = gk * lo if k == 0 else r_lo[c] + gk * lo
            r_hi[c] = gk * hi if k == 0 else r_hi[c] + gk * hi
    routed = jnp.concatenate(r_lo + r_hi, axis=1)
    x2 = x1s_ref[...] + g2_ref[...] * routed
    out_ref[...] = _rms(x2, fg_ref[...])


def _combine(yg, gates, x1s, mod3, final_g, s):
    n, d = x1s.shape
    t = COMB_T
    tpb = s // t
    nch = yg.shape[0]
    return pl.pallas_call(
        _combine_kernel,
        grid=(n // t,),
        in_specs=[pl.BlockSpec((nch, TOP_K, t, LANES), lambda i: (0, 0, i, 0)),
                  pl.BlockSpec((t, TOP_K), lambda i: (i, 0)),
                  pl.BlockSpec((t, d), lambda i: (i, 0)),
                  pl.BlockSpec((None, 1, d), lambda i: ((i // tpb) * 6 + 5, 0, 0)),
                  pl.BlockSpec((1, d), lambda i: (0, 0))],
        out_specs=pl.BlockSpec((t, d), lambda i: (i, 0)),
        out_shape=jax.ShapeDtypeStruct((n, d), F32),
        compiler_params=_cparams(("parallel",)),
        name="combine",
    )(yg, gates, x1s, mod3, final_g)


def _place_heads(w, per_head, keep):
    r = w.shape[0]
    w = w.reshape(r, B_HEADS, per_head)[:, :, :keep]
    return jnp.pad(w, ((0, 0), (0, 0), (0, HEAD_PAD - keep))).reshape(r, B_HEADS * HEAD_PAD)


def _rope_tables(s):
    inv = ROPE_THETA ** (-jnp.arange(0, B_ROPE, 2, dtype=jnp.float32) / B_ROPE)
    ang = jnp.arange(s, dtype=jnp.float32)[:, None] * inv[None, :]
    cos, sin = jnp.cos(ang), jnp.sin(ang)
    ones = jnp.ones((s, B_NOPE), F32)
    zeros = jnp.zeros((s, B_NOPE), F32)
    tail1 = jnp.ones((s, HEAD_PAD - B_NOPE - B_ROPE), F32)
    tail0 = jnp.zeros((s, HEAD_PAD - B_NOPE - B_ROPE), F32)
    return (jnp.concatenate([ones, cos, cos, tail1], axis=1),
            jnp.concatenate([zeros, -sin, sin, tail0], axis=1))


def kernel(x, c, w_ada, b_ada, norm1_g, w_in, q_norm_g, w_uq, kv_norm_g, w_ukv, rel_table, a_out_g, b_out_g,
           w_o, norm2_g, w_router, e_bias, w1, w3, w2, ws1, ws3, ws2, final_g):
    b, s, d = x.shape
    n = b * s
    assert w_ada.shape[0] == 1, "single layer"
    x2 = x.reshape(n, d)

    mod = _ada(c, w_ada[0], b_ada[0])
    mod3 = mod.reshape(b * 6, 1, d)
    bias = _bias_tiles(rel_table)

    wi = w_in[0]
    c_kpe = 3 * A_WIDTH + Q_LORA + KV_LORA
    kpe_cols = jnp.pad(wi[:, c_kpe:], ((0, 0), (B_NOPE, HEAD_PAD - B_NOPE - B_ROPE)))
    w_in_ext = jnp.concatenate([wi[:, :c_kpe], kpe_cols], axis=1).astype(BF16)
    w_uq_p = _place_heads(w_uq[0], B_NOPE + B_ROPE, B_NOPE + B_ROPE).astype(BF16)
    w_uk_p = _place_heads(w_ukv[0], B_NOPE + B_VDIM, B_NOPE).astype(BF16)
    w_v = w_ukv[0].reshape(KV_LORA, B_HEADS, B_NOPE + B_VDIM)[:, :, B_NOPE:].reshape(KV_LORA, B_WIDTH).astype(BF16)
    cos_t, sin_t = _rope_tables(s)

    qa, ka, va, qb, kb, vb = _inproj(x2, mod3, norm1_g, w_in_ext, q_norm_g, w_uq_p, kv_norm_g, w_uk_p, w_v,
                                     cos_t, sin_t, b, s)
    a_out = _dilated(qa.reshape(b, s, A_WIDTH), ka.reshape(b, s, A_WIDTH), va.reshape(b, s, A_WIDTH), bias)
    b_out = _mla(qb, kb, vb)

    w_router_t = _slab_order(w_router[0].T)
    x1s, h2p, scores_t = _outproj(x2, a_out.reshape(n, A_WIDTH), b_out.reshape(n, B_WIDTH), a_out_g, b_out_g,
                                  w_o[0].astype(BF16), mod3, norm2_g, w_router_t,
                                  ws1[0].astype(BF16), ws3[0].astype(BF16), ws2[0].astype(BF16), s)

    eidx_t, gates_t, counts_slab = _route(scores_t, _slab_order(e_bias.reshape(N_EXPERTS, 1)))

    counts = _expert_order(counts_slab)[:, 0]
    padded = (counts + MOE_BLK - 1) // MOE_BLK * MOE_BLK
    pends = jnp.cumsum(padded)
    pstart = pends - padded
    nk = n * TOP_K
    nblk = -(-(nk + N_EXPERTS * (MOE_BLK - 1)) // MOE_BLK)
    rows_total = nblk * MOE_BLK
    blk_row = jnp.arange(nblk, dtype=I32) * MOE_BLK
    blk_e = jnp.minimum(jnp.sum((pends[None, :] <= blk_row[:, None]).astype(I32), axis=1), N_EXPERTS - 1)
    seg_end = (pstart + counts)[blk_e]
    valid = jnp.clip(seg_end - blk_row, 0, MOE_BLK).astype(I32)
    nused = (pends[-1] // MOE_BLK).astype(I32).reshape(1)
    cstart = jnp.concatenate([jnp.zeros((1,), I32), (pends // MOE_BLK).astype(I32)])

    dest_t = _dest(eidx_t, _slab_order(pstart.astype(F32).reshape(N_EXPERTS, 1)))
    nplane = h2p.shape[0]
    gidx = (dest_t.reshape(1, nk) + (jnp.arange(nplane, dtype=I32) * rows_total)[:, None]).reshape(nplane * nk)
    xg = _sc_scatter(h2p.reshape(nplane * n, LANES), gidx, nplane * rows_total, nplane, TOP_K)
    y = _experts(cstart, nused, valid, xg.reshape(nplane, rows_total, LANES), w1[0], w3[0], w2[0])
    yg = _sc_gather(y.reshape(nplane * rows_total, LANES), gidx).reshape(nplane, TOP_K, n, LANES)
    out = _combine(yg, gates_t.T, x1s, mod3, final_g.reshape(1, d), s)
    return out.reshape(b, s, d)
```

```python
import functools
import math

import jax
import jax.numpy as jnp
from jax import lax
from jax.experimental import pallas as pl
from jax.experimental.pallas import tpu as pltpu
from jax.experimental.pallas import tpu_sc as plsc

F32 = jnp.float32
BF16 = jnp.bfloat16
U32 = jnp.uint32
I32 = jnp.int32
HIGHEST = lax.Precision.HIGHEST

D_MODEL = 1024
A_HEADS = 8
A_HEAD_DIM = 64
A_WIDTH = A_HEADS * A_HEAD_DIM
A_PATTERNS = ((128, 1), (512, 4), (2048, 16))
A_RADIUS = 64
REL_BUCKETS = 32
REL_MAX_DIST = 1024
B_HEADS = 8
B_NOPE = 64
B_ROPE = 32
B_VDIM = 64
B_WIDTH = B_HEADS * B_VDIM
Q_LORA = 384
KV_LORA = 256
ROPE_THETA = 10000.0
N_EXPERTS = 256
TOP_K = 8
N_GROUPS = 8
GROUP_SIZE = N_EXPERTS // N_GROUPS
TOPK_GROUPS = 4
EXPERT_FF = 256
SHARED_FF = 256
ROUTED_SCALE = 2.5
EPS = 1e-6
NEG_INF = -1e30
LOG2E = math.log2(math.e)

LANES = 128
SUBLANES = 8
HEAD_PAD = 128
IN_COLS_EXT = 3 * A_WIDTH + Q_LORA + KV_LORA + HEAD_PAD

TM_INPROJ = 512
TM_OUTPROJ = 256
DIL_QB = 128
DIL_KW = DIL_QB + 2 * A_RADIUS
DIL_UNROLL = 8
MLA_TQ = 256
ROUTE_T = 256
MOE_BLK = 256
EXP_XBUF = 4
SC_WINDOW = 128
COMB_T = 256
COMB_SPLIT = 4
VMEM_LIMIT = 56 * 1024 * 1024


def _cparams(sem):
    return pltpu.CompilerParams(dimension_semantics=sem, vmem_limit_bytes=VMEM_LIMIT)


def _rms(x, g):
    return x * lax.rsqrt(jnp.mean(x * x, axis=-1, keepdims=True) + EPS) * g


def _silu(x):
    return x * jax.nn.sigmoid(x)


def _pack_rows(x):
    half = x.shape[1] // 2
    bits = lax.bitcast_convert_type(x.astype(BF16).astype(F32), U32)
    return (bits[:, :half] >> 16) | bits[:, half:]


def _unpack_rows(w):
    lo = lax.bitcast_convert_type(w << 16, F32)
    hi = lax.bitcast_convert_type(w & jnp.uint32(0xFFFF0000), F32)
    return lo, hi


def _ada_kernel(c_ref, w_ref, b_ref, o_ref):
    o_ref[...] = jnp.dot(_silu(c_ref[...]), w_ref[...], precision=HIGHEST,
                         preferred_element_type=F32) + b_ref[...]


def _ada(c, w_ada, b_ada):
    b, d = c.shape
    n6 = w_ada.shape[1] // d
    return pl.pallas_call(
        _ada_kernel,
        grid=(n6,),
        in_specs=[pl.BlockSpec((b, d), lambda j: (0, 0)),
                  pl.BlockSpec((d, d), lambda j: (0, j)),
                  pl.BlockSpec((1, d), lambda j: (0, j))],
        out_specs=pl.BlockSpec((b, d), lambda j: (0, j)),
        out_shape=jax.ShapeDtypeStruct((b, n6 * d), F32),
        compiler_params=_cparams(("parallel",)),
        name="ada",
    )(c, w_ada, b_ada.reshape(1, -1))


def _t5_bucket(rel):
    half = REL_BUCKETS // 2
    max_exact = half // 2
    ret = jnp.where(rel > 0, half, 0)
    n = jnp.abs(rel)
    nf = jnp.maximum(n, 1).astype(jnp.float32)
    large = max_exact + (jnp.log(nf / max_exact) / math.log(REL_MAX_DIST / max_exact)
                         * (half - max_exact)).astype(jnp.int32)
    large = jnp.minimum(large, half - 1)
    return ret + jnp.where(n < max_exact, n, large)


DIL_SHIFTS = (A_RADIUS, 0, -A_RADIUS)


def _bucket_tiles():
    qi = jnp.arange(DIL_QB, dtype=jnp.int32)[:, None]
    ki = jnp.arange(DIL_KW, dtype=jnp.int32)[None, :]
    tiles = []
    for _, dilation in A_PATTERNS:
        for shift in DIL_SHIFTS:
            off = ki + shift - A_RADIUS - qi
            bkt = _t5_bucket(off * dilation)
            tiles.append(jnp.where(jnp.abs(off) <= A_RADIUS, bkt, -1))
    return jnp.stack(tiles, axis=0)


def _bias_kernel(tab_ref, bkt_ref, o_ref):
    bkt = bkt_ref[0]
    for h in range(A_HEADS):
        acc = jnp.full(bkt.shape, NEG_INF, F32)
        for b in range(REL_BUCKETS):
            acc = jnp.where(bkt == b, tab_ref[b, h] * LOG2E, acc)
        o_ref[0, h] = acc


def _bias_tiles(rel_table):
    bkt = _bucket_tiles()
    nt = bkt.shape[0]
    return pl.pallas_call(
        _bias_kernel,
        grid=(nt,),
        in_specs=[pl.BlockSpec(memory_space=pltpu.SMEM),
                  pl.BlockSpec((1, DIL_QB, DIL_KW), lambda t: (t, 0, 0))],
        out_specs=pl.BlockSpec((1, A_HEADS, DIL_QB, DIL_KW), lambda t: (t, 0, 0, 0)),
        out_shape=jax.ShapeDtypeStruct((nt, A_HEADS, DIL_QB, DIL_KW), F32),
        compiler_params=_cparams(("parallel",)),
        name="bias",
    )(rel_table, bkt)


def _rope(x, cos, sin, lane_lt_mid):
    half = B_ROPE // 2
    rot = jnp.where(lane_lt_mid, pltpu.roll(x, HEAD_PAD - half, 1), pltpu.roll(x, half, 1))
    return x * cos + rot * sin


def _inproj_kernel(x_ref, sc_ref, sh_ref, g1_ref, win_ref, qg_ref, wuq_ref, kvg_ref, wuk_ref, wv_ref,
                   cos_ref, sin_ref, qa_ref, ka_ref, va_ref, qb_ref, kb_ref, vb_ref):
    x = x_ref[...]
    h = _rms(x, g1_ref[...]) * (1.0 + sc_ref[...]) + sh_ref[...]
    proj = jnp.dot(h.astype(BF16), win_ref[...], preferred_element_type=F32)
    aw = A_WIDTH
    qa_ref[...] = proj[:, 0:aw] * (LOG2E / math.sqrt(A_HEAD_DIM))
    ka_ref[...] = proj[:, aw:2 * aw]
    va_ref[...] = proj[:, 2 * aw:3 * aw]
    c0 = 3 * aw
    q_lat = proj[:, c0:c0 + Q_LORA]
    kv_lat = proj[:, c0 + Q_LORA:c0 + Q_LORA + KV_LORA]
    kpe = proj[:, c0 + Q_LORA + KV_LORA:]
    qn = _rms(q_lat, qg_ref[...]).astype(BF16)
    kvn = _rms(kv_lat, kvg_ref[...]).astype(BF16)
    qm = jnp.dot(qn, wuq_ref[...], preferred_element_type=F32)
    kn = jnp.dot(kvn, wuk_ref[...], preferred_element_type=F32)
    vv = jnp.dot(kvn, wv_ref[...], preferred_element_type=F32)
    cos = cos_ref[...]
    sin = sin_ref[...]
    lane = lax.broadcasted_iota(jnp.int32, cos.shape, 1)
    lt_mid = lane < (B_NOPE + B_ROPE // 2)
    qscale = LOG2E / math.sqrt(B_NOPE + B_ROPE)
    for hd in range(B_HEADS):
        sl = slice(hd * HEAD_PAD, (hd + 1) * HEAD_PAD)
        qb_ref[hd] = (_rope(qm[:, sl], cos, sin, lt_mid) * qscale).astype(BF16)
        kb_ref[hd] = _rope(kn[:, sl] + kpe, cos, sin, lt_mid).astype(BF16)
    for p in range(B_HEADS // 2):
        vb_ref[p] = vv[:, p * LANES:(p + 1) * LANES].astype(BF16)


def _inproj(x2, mod3, norm1_g, w_in_ext, q_norm_g, w_uq_p, kv_norm_g, w_uk_p, w_v, cos_t, sin_t, b, s):
    n, d = x2.shape
    tm = TM_INPROJ
    tpb = s // tm
    row = lambda i: (i, 0)
    const = lambda i: (0, 0)
    hm = lambda i: (i // tpb, 0, i % tpb, 0)
    return pl.pallas_call(
        _inproj_kernel,
        grid=(n // tm,),
        in_specs=[pl.BlockSpec((tm, d), row),
                  pl.BlockSpec((None, 1, d), lambda i: ((i // tpb) * 6 + 1, 0, 0)),
                  pl.BlockSpec((None, 1, d), lambda i: ((i // tpb) * 6 + 0, 0, 0)),
                  pl.BlockSpec((1, d), const),
                  pl.BlockSpec(w_in_ext.shape, const),
                  pl.BlockSpec((1, Q_LORA), const),
                  pl.BlockSpec(w_uq_p.shape, const),
                  pl.BlockSpec((1, KV_LORA), const),
                  pl.BlockSpec(w_uk_p.shape, const),
                  pl.BlockSpec(w_v.shape, const),
                  pl.BlockSpec((tm, HEAD_PAD), lambda i: (i % tpb, 0)),
                  pl.BlockSpec((tm, HEAD_PAD), lambda i: (i % tpb, 0))],
        out_specs=[pl.BlockSpec((tm, A_WIDTH), row),
                   pl.BlockSpec((tm, A_WIDTH), row),
                   pl.BlockSpec((tm, A_WIDTH), row),
                   pl.BlockSpec((None, B_HEADS, tm, HEAD_PAD), hm),
                   pl.BlockSpec((None, B_HEADS, tm, HEAD_PAD), hm),
                   pl.BlockSpec((None, B_HEADS // 2, tm, LANES), hm)],
        out_shape=[jax.ShapeDtypeStruct((n, A_WIDTH), F32),
                   jax.ShapeDtypeStruct((n, A_WIDTH), F32),
                   jax.ShapeDtypeStruct((n, A_WIDTH), F32),
                   jax.ShapeDtypeStruct((b, B_HEADS, s, HEAD_PAD), BF16),
                   jax.ShapeDtypeStruct((b, B_HEADS, s, HEAD_PAD), BF16),
                   jax.ShapeDtypeStruct((b, B_HEADS // 2, s, LANES), BF16)],
        compiler_params=_cparams(("parallel",)),
        name="inproj",
    )(x2, mod3, mod3, norm1_g, w_in_ext, q_norm_g, w_uq_p, kv_norm_g, w_uk_p, w_v, cos_t, sin_t)


def _dil_block(q_ref, k_ref, v_ref, bias_ref, o_scr, l_scr, pi, dil, nblk, job):
    sub_len = nblk * DIL_QB
    r = job // nblk
    bi = job % nblk
    q0 = bi * DIL_QB
    ws = jnp.clip(q0 - A_RADIUS, 0, sub_len - DIL_KW)
    var = jnp.where(bi == 0, 0, jnp.where(bi == nblk - 1, 2, 1))
    if dil == 1:
        qsl = pl.ds(pl.multiple_of(q0, DIL_QB), DIL_QB)
        ksl = pl.ds(pl.multiple_of(ws, A_RADIUS), DIL_KW)
    else:
        qsl = pl.ds(r + dil * q0, DIL_QB, stride=dil)
        ksl = pl.ds(r + dil * ws, DIL_KW, stride=dil)
    q = q_ref[qsl, :]
    kw = k_ref[ksl, :].astype(BF16)
    vw = v_ref[ksl, :].astype(BF16)
    lo = lax.broadcasted_iota(jnp.int32, q.shape, 1) < A_HEAD_DIM
    outs, lses = [], []
    for hh in range(2):
        qm = jnp.where(lo if hh == 0 else jnp.logical_not(lo), q, 0.0).astype(BF16)
        sc = lax.dot_general(qm, kw, (((1,), (1,)), ((), ())), preferred_element_type=F32)
        sc = sc + bias_ref[pi * 3 + var, hh]
        m = jnp.max(sc, axis=-1, keepdims=True)
        p = jnp.exp2(sc - m)
        l = jnp.sum(p, axis=-1, keepdims=True)
        o = jnp.dot(p.astype(BF16), vw, preferred_element_type=F32)
        outs.append(o / l)
        lses.append(m + jnp.log2(l))
    o_scr[pi, qsl, :] = jnp.where(lo, outs[0], outs[1])
    l_scr[pi, qsl, :] = jnp.where(lo, lses[0], lses[1])


def _dilated_kernel(q_ref, k_ref, v_ref, bias_ref, out_ref, o_scr, l_scr):
    s = q_ref.shape[0]
    njobs = s // DIL_QB
    for pi, (_, dil) in enumerate(A_PATTERNS):
        blk = functools.partial(_dil_block, q_ref, k_ref, v_ref, bias_ref, o_scr, l_scr, pi, dil,
                                s // dil // DIL_QB)

        def group(g, c, blk=blk):
            for u in range(DIL_UNROLL):
                blk(g * DIL_UNROLL + u)
            return c

        lax.fori_loop(0, njobs // DIL_UNROLL, group, 0)

    chunk = 512

    def comb(i, c):
        rows = pl.ds(pl.multiple_of(i * chunk, chunk), chunk)
        l0, l1, l2 = l_scr[0, rows, :], l_scr[1, rows, :], l_scr[2, rows, :]
        mx = jnp.maximum(jnp.maximum(l0, l1), l2)
        e0, e1, e2 = jnp.exp2(l0 - mx), jnp.exp2(l1 - mx), jnp.exp2(l2 - mx)
        num = e0 * o_scr[0, rows, :] + e1 * o_scr[1, rows, :] + e2 * o_scr[2, rows, :]
        out_ref[rows, :] = num / (e0 + e1 + e2)
        return c

    lax.fori_loop(0, s // chunk, comb, 0)


def _dilated(qa, ka, va, bias):
    b, s, _ = qa.shape
    npair = A_WIDTH // LANES
    assert (s // DIL_QB) % DIL_UNROLL == 0
    for _, dil in A_PATTERNS:
        assert (s // dil) % DIL_QB == 0 and s // dil >= DIL_KW
    blk = pl.BlockSpec((None, s, LANES), lambda bi, p: (bi, 0, p))
    return pl.pallas_call(
        _dilated_kernel,
        grid=(b, npair),
        in_specs=[blk, blk, blk,
                  pl.BlockSpec((bias.shape[0], 2, DIL_QB, DIL_KW), lambda bi, p: (0, p, 0, 0))],
        out_specs=blk,
        out_shape=jax.ShapeDtypeStruct((b, s, A_WIDTH), F32),
        scratch_shapes=[pltpu.VMEM((len(A_PATTERNS), s, LANES), F32),
                        pltpu.VMEM((len(A_PATTERNS), s, LANES), F32)],
        compiler_params=_cparams(("parallel", "parallel")),
        name="dilated",
    )(qa, ka, va, bias)


def _mla_kernel(q_ref, k_ref, v_ref, o_ref):
    v = v_ref[...]
    outs = []
    for hh in range(2):
        sc = lax.dot_general(q_ref[hh], k_ref[hh], (((1,), (1,)), ((), ())), preferred_element_type=F32)
        m = jnp.max(sc, axis=-1, keepdims=True)
        p = jnp.exp2(sc - m)
        l = jnp.sum(p, axis=-1, keepdims=True)
        outs.append(jnp.dot(p.astype(BF16), v, preferred_element_type=F32) / l)
    lo = lax.broadcasted_iota(jnp.int32, outs[0].shape, 1) < B_VDIM
    o_ref[...] = jnp.where(lo, outs[0], outs[1])


def _mla(qb, kb, vb):
    b, h, s, _ = qb.shape
    npair = h // 2
    tq = MLA_TQ
    return pl.pallas_call(
        _mla_kernel,
        grid=(b, npair, s // tq),
        in_specs=[pl.BlockSpec((None, 2, tq, HEAD_PAD), lambda bi, p, qi: (bi, p, qi, 0)),
                  pl.BlockSpec((None, 2, s, HEAD_PAD), lambda bi, p, qi: (bi, p, 0, 0)),
                  pl.BlockSpec((None, None, s, LANES), lambda bi, p, qi: (bi, p, 0, 0))],
        out_specs=pl.BlockSpec((None, tq, LANES), lambda bi, p, qi: (bi, qi, p)),
        out_shape=jax.ShapeDtypeStruct((b, s, B_WIDTH), F32),
        compiler_params=_cparams(("parallel", "parallel", "arbitrary")),
        name="mla",
    )(qb, kb, vb)


def _outproj_kernel(x_ref, a_ref, b_ref, ag_ref, bg_ref, wo_ref, g1_ref, n2_ref, sc_ref, sh_ref, g2_ref,
                    wrt_ref, ws1_ref, ws3_ref, ws2_ref, x1s_ref, h2p_ref, scores_ref):
    an = _rms(a_ref[...], ag_ref[...])
    bn = _rms(b_ref[...], bg_ref[...])
    mix = jnp.concatenate([an, bn], axis=-1).astype(BF16)
    x1 = x_ref[...] + g1_ref[...] * jnp.dot(mix, wo_ref[...], preferred_element_type=F32)
    h2 = _rms(x1, n2_ref[...]) * (1.0 + sc_ref[...]) + sh_ref[...]
    packed = _pack_rows(h2)
    for cg in range(h2p_ref.shape[0]):
        h2p_ref[cg] = packed[:, cg * LANES:(cg + 1) * LANES]
    logits = lax.dot_general(wrt_ref[...], h2, (((1,), (1,)), ((), ())), precision=HIGHEST,
                             preferred_element_type=F32)
    scores_ref[...] = jax.nn.sigmoid(logits)
    h2b = h2.astype(BF16)
    hid = _silu(jnp.dot(h2b, ws1_ref[...], preferred_element_type=F32)) * jnp.dot(
        h2b, ws3_ref[...], preferred_element_type=F32)
    shared = jnp.dot(hid.astype(BF16), ws2_ref[...], preferred_element_type=F32)
    x1s_ref[...] = x1 + g2_ref[...] * shared


def _outproj(x2, a_out, b_out, a_out_g, b_out_g, w_o, mod3, norm2_g, w_router_t, ws1, ws3, ws2, s):
    n, d = x2.shape
    tm = TM_OUTPROJ
    tpb = s // tm
    row = lambda i: (i, 0)
    const = lambda i: (0, 0)
    modspec = lambda j: pl.BlockSpec((None, 1, d), lambda i: ((i // tpb) * 6 + j, 0, 0))
    return pl.pallas_call(
        _outproj_kernel,
        grid=(n // tm,),
        in_specs=[pl.BlockSpec((tm, d), row),
                  pl.BlockSpec((tm, A_WIDTH), row),
                  pl.BlockSpec((tm, B_WIDTH), row),
                  pl.BlockSpec((1, A_WIDTH), const),
                  pl.BlockSpec((1, B_WIDTH), const),
                  pl.BlockSpec(w_o.shape, const),
                  modspec(2),
                  pl.BlockSpec((1, d), const),
                  modspec(4), modspec(3), modspec(5),
                  pl.BlockSpec(w_router_t.shape, const),
                  pl.BlockSpec(ws1.shape, const),
                  pl.BlockSpec(ws3.shape, const),
                  pl.BlockSpec(ws2.shape, const)],
        out_specs=[pl.BlockSpec((tm, d), row),
                   pl.BlockSpec((d // 2 // LANES, tm, LANES), lambda i: (0, i, 0)),
                   pl.BlockSpec((N_EXPERTS, tm), lambda i: (0, i))],
        out_shape=[jax.ShapeDtypeStruct((n, d), F32),
                   jax.ShapeDtypeStruct((d // 2 // LANES, n, LANES), U32),
                   jax.ShapeDtypeStruct((N_EXPERTS, n), F32)],
        compiler_params=_cparams(("parallel",)),
        name="outproj",
    )(x2, a_out, b_out, a_out_g, b_out_g, w_o, mod3, norm2_g, mod3, mod3, mod3, w_router_t, ws1, ws3, ws2)


def _slab_order(v):
    return v.reshape((N_GROUPS, GROUP_SIZE) + v.shape[1:]).swapaxes(0, 1).reshape(v.shape)


def _expert_order(v):
    return v.reshape((GROUP_SIZE, N_GROUPS) + v.shape[1:]).swapaxes(0, 1).reshape(v.shape)


def _sublane_all(x, op):
    for sh in (4, 2, 1):
        x = op(x, pltpu.roll(x, sh, 0))
    return x


def _route_kernel(st_ref, bias_ref, eidx_ref, gate_ref, cnt_ref):
    nsl = GROUP_SIZE
    t = st_ref.shape[1]
    sub = lax.broadcasted_iota(I32, (SUBLANES, t), 0)
    ninf = -jnp.inf
    big = jnp.int32(1 << 30)
    sc = [st_ref[j * SUBLANES:(j + 1) * SUBLANES, :] for j in range(nsl)]
    sel = [sc[j] + bias_ref[j * SUBLANES:(j + 1) * SUBLANES, :] for j in range(nsl)]
    eid = [sub * GROUP_SIZE + j for j in range(nsl)]

    m1 = sel[0]
    m2 = jnp.full_like(m1, ninf)
    for j in range(1, nsl):
        m2 = jnp.maximum(m2, jnp.minimum(m1, sel[j]))
        m1 = jnp.maximum(m1, sel[j])
    gs = m1 + m2

    rank = jnp.zeros((SUBLANES, t), I32)
    for sh in range(1, N_GROUPS):
        other = pltpu.roll(gs, sh, 0)
        ahead = (other > gs) | ((other == gs) & (sub >= sh))
        rank = rank + ahead.astype(I32)
    gmask = rank < TOPK_GROUPS

    msel = [jnp.where(gmask, sel[j], ninf) for j in range(nsl)]
    hits = [jnp.zeros((SUBLANES, t), I32) for _ in range(nsl)]
    eidx = jnp.zeros((TOP_K, t), I32)
    gates = jnp.zeros((TOP_K, t), F32)
    for k in range(TOP_K):
        mx = msel[0]
        for j in range(1, nsl):
            mx = jnp.maximum(mx, msel[j])
        mx = _sublane_all(mx, jnp.maximum)
        cand = jnp.where(msel[0] == mx, eid[0], big)
        for j in range(1, nsl):
            cand = jnp.minimum(cand, jnp.where(msel[j] == mx, eid[j], big))
        idx = _sublane_all(cand, jnp.minimum)
        gk = jnp.zeros((SUBLANES, t), F32)
        for j in range(nsl):
            hit = eid[j] == idx
            gk = gk + jnp.where(hit, sc[j], 0.0)
            msel[j] = jnp.where(hit, ninf, msel[j])
            hits[j] = hits[j] + hit.astype(I32)
        gk = _sublane_all(gk, jnp.add)
        eidx = jnp.where(sub == k, idx, eidx)
        gates = jnp.where(sub == k, gk, gates)
    gsum = _sublane_all(gates, jnp.add)
    eidx_ref[...] = eidx
    gate_ref[...] = gates / gsum * ROUTED_SCALE

    @pl.when(pl.program_id(0) == 0)
    def _():
        cnt_ref[...] = jnp.zeros_like(cnt_ref)

    for j in range(nsl):
        cnt_ref[j * SUBLANES:(j + 1) * SUBLANES, :] += jnp.sum(hits[j].astype(F32), axis=1,
                                                               keepdims=True).astype(I32)


def _route(scores_t, e_bias_slab):
    e, n = scores_t.shape
    t = ROUTE_T
    return pl.pallas_call(
        _route_kernel,
        grid=(n // t,),
        in_specs=[pl.BlockSpec((e, t), lambda i: (0, i)),
                  pl.BlockSpec((e, 1), lambda i: (0, 0))],
        out_specs=[pl.BlockSpec((TOP_K, t), lambda i: (0, i)),
                   pl.BlockSpec((TOP_K, t), lambda i: (0, i)),
                   pl.BlockSpec((e, 1), lambda i: (0, 0))],
        out_shape=[jax.ShapeDtypeStruct((TOP_K, n), I32),
                   jax.ShapeDtypeStruct((TOP_K, n), F32),
                   jax.ShapeDtypeStruct((e, 1), I32)],
        compiler_params=_cparams(("arbitrary",)),
        name="route",
    )(scores_t, e_bias_slab)


def _dest_kernel(eidx_ref, pstart_ref, dest_ref, carry_ref):
    @pl.when(pl.program_id(0) == 0)
    def _():
        carry_ref[...] = jnp.zeros_like(carry_ref)

    nsl = GROUP_SIZE
    t = eidx_ref.shape[1]
    sub = lax.broadcasted_iota(I32, (SUBLANES, t), 0)
    eid = [sub * GROUP_SIZE + j for j in range(nsl)]
    ek = [eidx_ref[k:k + 1, :] for k in range(TOP_K)]
    slabs = []
    for j in range(nsl):
        oh = jnp.zeros((SUBLANES, t), F32)
        for k in range(TOP_K):
            oh = oh + (eid[j] == ek[k]).astype(F32)
        slabs.append(oh)
    onehot = jnp.concatenate(slabs, axis=0)
    row = lax.broadcasted_iota(I32, (t, t), 0)
    col = lax.broadcasted_iota(I32, (t, t), 1)
    upper = (row < col).astype(BF16)
    before = jnp.dot(onehot.astype(BF16), upper, preferred_element_type=F32)
    base = before + carry_ref[...] + pstart_ref[...]
    dest = jnp.zeros((TOP_K, t), I32)
    for k in range(TOP_K):
        acc = jnp.zeros((SUBLANES, t), F32)
        for j in range(nsl):
            acc = acc + jnp.where(eid[j] == ek[k], base[j * SUBLANES:(j + 1) * SUBLANES, :], 0.0)
        dk = _sublane_all(acc, jnp.add).astype(I32)
        dest = jnp.where(sub == k, dk, dest)
    dest_ref[...] = dest
    carry_ref[...] += jnp.sum(onehot, axis=1, keepdims=True)


def _dest(eidx_t, pstart_slab):
    n = eidx_t.shape[1]
    t = ROUTE_T
    return pl.pallas_call(
        _dest_kernel,
        grid=(n // t,),
        in_specs=[pl.BlockSpec((TOP_K, t), lambda i: (0, i)),
                  pl.BlockSpec((N_EXPERTS, 1), lambda i: (0, 0))],
        out_specs=pl.BlockSpec((TOP_K, t), lambda i: (0, i)),
        out_shape=jax.ShapeDtypeStruct((TOP_K, n), I32),
        scratch_shapes=[pltpu.VMEM((N_EXPERTS, 1), F32)],
        compiler_params=_cparams(("arbitrary",)),
        name="dest",
    )(eidx_t, pstart_slab)


def _sc_scatter(x, idx, rows_out, nplane, nslot):
    num = idx.shape[0]
    nwin = x.shape[0] // nplane // SC_WINDOW
    mesh = plsc.VectorSubcoreMesh(core_axis_name="core", subcore_axis_name="subcore")

    @pl.kernel(out_type=jax.ShapeDtypeStruct((rows_out, x.shape[1]), x.dtype), mesh=mesh, scratch_types=[])
    def scatter(x_hbm, i_hbm, o_hbm):
        def body(x_vmem, i_vmem):
            pltpu.sync_copy(x_vmem, o_hbm.at[i_vmem.at[0]])

        pltpu.emit_pipeline(
            body,
            grid=(num // SC_WINDOW,),
            in_specs=[pl.BlockSpec((SC_WINDOW, x.shape[1]),
                                   index_map=lambda i: ((i // (nslot * nwin)) * nwin + i % nwin, 0)),
                      pl.BlockSpec((1, SC_WINDOW), index_map=lambda i: (0, i))],
            out_specs=[],
            core_axis_name=("core", "subcore"),
            dimension_semantics=(pltpu.PARALLEL,),
        )(x_hbm, i_hbm)

    return scatter(x, idx.reshape(1, num))


def _expert_kernel(cstart_ref, nused_ref, valid_ref, w1_ref, w3_ref, w2_ref, xg_ref, y_ref,
                   w1b, w3b, w2b, xbuf, ybuf, xsem, ysem):
    e = pl.program_id(0)
    c0 = cstart_ref[e]
    c1 = cstart_ref[e + 1]
    nused = nused_ref[0]
    nchunks = xg_ref.shape[1] // MOE_BLK

    def rows(g):
        start = g * MOE_BLK
        return pl.ds(start if isinstance(g, int) else pl.multiple_of(start, MOE_BLK), MOE_BLK)

    def xcopies(g):
        slot = g % EXP_XBUF
        return [pltpu.make_async_copy(xg_ref.at[c, rows(g)],
                                      xbuf.at[slot, pl.ds(0, MOE_BLK), pl.ds(c * LANES, LANES)], xsem.at[slot])
                for c in range(xg_ref.shape[0])]

    def ycopies(g, slot):
        return [pltpu.make_async_copy(ybuf.at[slot, pl.ds(0, MOE_BLK), pl.ds(c * LANES, LANES)],
                                      y_ref.at[c, rows(g)], ysem.at[slot])
                for c in range(y_ref.shape[0])]

    @pl.when(e == 0)
    def _():
        for j in range(EXP_XBUF - 1):
            @pl.when(j < nused)
            def _():
                for cp in xcopies(j):
                    cp.start()

    @pl.when(c1 > c0)
    def _():
        w1b[...] = w1_ref[...].astype(BF16)
        w3b[...] = w3_ref[...].astype(BF16)
        w2b[...] = w2_ref[...].astype(BF16)

        def chunk(g, carry):
            @pl.when(g + (EXP_XBUF - 1) < nused)
            def _():
                for cp in xcopies(g + (EXP_XBUF - 1)):
                    cp.start()

            for cp in xcopies(g):
                cp.wait()
            words = xbuf[g % EXP_XBUF]
            row_id = lax.broadcasted_iota(I32, words.shape, 0)
            lo, hi = _unpack_rows(jnp.where(row_id < valid_ref[g], words, jnp.uint32(0)))
            xb = jnp.concatenate([lo, hi], axis=1).astype(BF16)
            hid = _silu(jnp.dot(xb, w1b[...], preferred_element_type=F32)) * jnp.dot(
                xb, w3b[...], preferred_element_type=F32)
            yp = _pack_rows(jnp.dot(hid.astype(BF16), w2b[...], preferred_element_type=F32))
            yslot = g % 2

            @pl.when(g >= 2)
            def _():
                for cp in ycopies(g - 2, yslot):
                    cp.wait()

            ybuf[yslot] = yp
            for cp in ycopies(g, yslot):
                cp.start()
            return carry

        lax.fori_loop(c0, c1, chunk, 0)

    @pl.when(e == pl.num_programs(0) - 1)
    def _():
        for back in (2, 1):
            g = nused - back

            @pl.when(g >= 0)
            def _():
                for cp in ycopies(g, g % 2):
                    cp.wait()

        ybuf[0] = jnp.zeros(ybuf.shape[1:], ybuf.dtype)

        def zstart(g, c):
            for cp in ycopies(g, 0):
                cp.start()
            return c

        def zwait(g, c):
            for cp in ycopies(g, 0):
                cp.wait()
            return c

        lax.fori_loop(nused, nchunks, zstart, 0)
        lax.fori_loop(nused, nchunks, zwait, 0)


def _experts(cstart, nused, valid, xg, w1, w3, w2):
    nplane, rows, _ = xg.shape
    dh = nplane * LANES
    ne, d, f = w1.shape
    grid_spec = pltpu.PrefetchScalarGridSpec(
        num_scalar_prefetch=3,
        grid=(ne,),
        in_specs=[pl.BlockSpec((None, d, f), lambda e, cs, nu, va: (e, 0, 0)),
                  pl.BlockSpec((None, d, f), lambda e, cs, nu, va: (e, 0, 0)),
                  pl.BlockSpec((None, f, d), lambda e, cs, nu, va: (e, 0, 0)),
                  pl.BlockSpec(memory_space=pl.ANY)],
        out_specs=pl.BlockSpec(memory_space=pl.ANY),
        scratch_shapes=[pltpu.VMEM((d, f), BF16), pltpu.VMEM((d, f), BF16), pltpu.VMEM((f, d), BF16),
                        pltpu.VMEM((EXP_XBUF, MOE_BLK, dh), U32), pltpu.VMEM((2, MOE_BLK, dh), U32),
                        pltpu.SemaphoreType.DMA((EXP_XBUF,)), pltpu.SemaphoreType.DMA((2,))],
    )
    return pl.pallas_call(
        _expert_kernel,
        grid_spec=grid_spec,
        out_shape=jax.ShapeDtypeStruct((dh // LANES, rows, LANES), U32),
        compiler_params=_cparams(("arbitrary",)),
        name="experts",
    )(cstart, nused, valid, w1, w3, w2, xg)


def _sc_gather(x, idx):
    num = idx.shape[0]
    mesh = plsc.VectorSubcoreMesh(core_axis_name="core", subcore_axis_name="subcore")

    @pl.kernel(out_type=jax.ShapeDtypeStruct((num, x.shape[1]), x.dtype), mesh=mesh)
    def gather(x_hbm, i_hbm, o_hbm):
        def body(i_vmem, o_vmem):
            pltpu.sync_copy(x_hbm.at[i_vmem.at[0]], o_vmem)

        pltpu.emit_pipeline(
            body,
            grid=(num // SC_WINDOW,),
            in_specs=[pl.BlockSpec((1, SC_WINDOW), index_map=lambda i: (0, i))],
            out_specs=[pl.BlockSpec((SC_WINDOW, x.shape[1]), index_map=lambda i: (i, 0))],
            core_axis_name=("core", "subcore"),
            dimension_semantics=(pltpu.PARALLEL,),
        )(i_hbm, o_hbm)

    return gather(x, idx.reshape(1, num))


def _combine_kernel(yg_ref, gate_ref, x1s_ref, g2_ref, fg_ref, *rest):
    out_ref = rest[-1]
    gates = gate_ref[...]
    nch = yg_ref.shape[0]
    r_lo = [None] * nch
    r_hi = [None] * nch
    for k in range(TOP_K):
        gk = gates[:, k:k + 1]
        for c in range(nch):
            lo, hi = _unpack_rows(yg_ref[c, k])
            r_lo[c] = gk * lo if k == 0 else r_lo[c] + gk * lo
            r_hi[c] = gk * hi if k == 0 else r_hi[c] + gk * hi
    routed = jnp.concatenate(r_lo + r_hi, axis=1)
    x2 = x1s_ref[...] + g2_ref[...] * routed
    out_ref[...] = _rms(x2, fg_ref[...])


def _combine(yg, gates, x1s, mod3, final_g, s, part, prev_out):
    n, d = x1s.shape
    t = COMB_T
    tpb = s // t
    nch, _, npart, _ = yg.shape
    off = part * (npart // t)
    in_specs = [pl.BlockSpec((nch, TOP_K, t, LANES), lambda i: (0, 0, i, 0)),
                pl.BlockSpec((t, TOP_K), lambda i: (i + off, 0)),
                pl.BlockSpec((t, d), lambda i: (i + off, 0)),
                pl.BlockSpec((None, 1, d), lambda i: (((i + off) // tpb) * 6 + 5, 0, 0)),
                pl.BlockSpec((1, d), lambda i: (0, 0))]
    args = [yg, gates, x1s, mod3, final_g]
    aliases = {}
    if prev_out is not None:
        in_specs.append(pl.BlockSpec(memory_space=pl.ANY))
        args.append(prev_out)
        aliases = {len(args) - 1: 0}
    return pl.pallas_call(
        _combine_kernel,
        grid=(npart // t,),
        in_specs=in_specs,
        out_specs=pl.BlockSpec((t, d), lambda i: (i + off, 0)),
        out_shape=jax.ShapeDtypeStruct((n, d), F32),
        input_output_aliases=aliases,
        compiler_params=_cparams(("parallel",)),
        name="combine",
    )(*args)


def _place_heads(w, per_head, keep):
    r = w.shape[0]
    w = w.reshape(r, B_HEADS, per_head)[:, :, :keep]
    return jnp.pad(w, ((0, 0), (0, 0), (0, HEAD_PAD - keep))).reshape(r, B_HEADS * HEAD_PAD)


def _rope_tables(s):
    inv = ROPE_THETA ** (-jnp.arange(0, B_ROPE, 2, dtype=jnp.float32) / B_ROPE)
    ang = jnp.arange(s, dtype=jnp.float32)[:, None] * inv[None, :]
    cos, sin = jnp.cos(ang), jnp.sin(ang)
    ones = jnp.ones((s, B_NOPE), F32)
    zeros = jnp.zeros((s, B_NOPE), F32)
    tail1 = jnp.ones((s, HEAD_PAD - B_NOPE - B_ROPE), F32)
    tail0 = jnp.zeros((s, HEAD_PAD - B_NOPE - B_ROPE), F32)
    return (jnp.concatenate([ones, cos, cos, tail1], axis=1),
            jnp.concatenate([zeros, -sin, sin, tail0], axis=1))


def kernel(x, c, w_ada, b_ada, norm1_g, w_in, q_norm_g, w_uq, kv_norm_g, w_ukv, rel_table, a_out_g, b_out_g,
           w_o, norm2_g, w_router, e_bias, w1, w3, w2, ws1, ws3, ws2, final_g):
    b, s, d = x.shape
    n = b * s
    assert w_ada.shape[0] == 1, "single layer"
    x2 = x.reshape(n, d)

    mod = _ada(c, w_ada[0], b_ada[0])
    mod3 = mod.reshape(b * 6, 1, d)
    bias = _bias_tiles(rel_table)

    wi = w_in[0]
    c_kpe = 3 * A_WIDTH + Q_LORA + KV_LORA
    kpe_cols = jnp.pad(wi[:, c_kpe:], ((0, 0), (B_NOPE, HEAD_PAD - B_NOPE - B_ROPE)))
    w_in_ext = jnp.concatenate([wi[:, :c_kpe], kpe_cols], axis=1).astype(BF16)
    w_uq_p = _place_heads(w_uq[0], B_NOPE + B_ROPE, B_NOPE + B_ROPE).astype(BF16)
    w_uk_p = _place_heads(w_ukv[0], B_NOPE + B_VDIM, B_NOPE).astype(BF16)
    w_v = w_ukv[0].reshape(KV_LORA, B_HEADS, B_NOPE + B_VDIM)[:, :, B_NOPE:].reshape(KV_LORA, B_WIDTH).astype(BF16)
    cos_t, sin_t = _rope_tables(s)

    qa, ka, va, qb, kb, vb = _inproj(x2, mod3, norm1_g, w_in_ext, q_norm_g, w_uq_p, kv_norm_g, w_uk_p, w_v,
                                     cos_t, sin_t, b, s)
    a_out = _dilated(qa.reshape(b, s, A_WIDTH), ka.reshape(b, s, A_WIDTH), va.reshape(b, s, A_WIDTH), bias)
    b_out = _mla(qb, kb, vb)

    w_router_t = _slab_order(w_router[0].T)
    x1s, h2p, scores_t = _outproj(x2, a_out.reshape(n, A_WIDTH), b_out.reshape(n, B_WIDTH), a_out_g, b_out_g,
                                  w_o[0].astype(BF16), mod3, norm2_g, w_router_t,
                                  ws1[0].astype(BF16), ws3[0].astype(BF16), ws2[0].astype(BF16), s)

    eidx_t, gates_t, counts_slab = _route(scores_t, _slab_order(e_bias.reshape(N_EXPERTS, 1)))

    counts = _expert_order(counts_slab)[:, 0]
    padded = (counts + MOE_BLK - 1) // MOE_BLK * MOE_BLK
    pends = jnp.cumsum(padded)
    pstart = pends - padded
    nk = n * TOP_K
    nblk = -(-(nk + N_EXPERTS * (MOE_BLK - 1)) // MOE_BLK)
    rows_total = nblk * MOE_BLK
    blk_row = jnp.arange(nblk, dtype=I32) * MOE_BLK
    blk_e = jnp.minimum(jnp.sum((pends[None, :] <= blk_row[:, None]).astype(I32), axis=1), N_EXPERTS - 1)
    seg_end = (pstart + counts)[blk_e]
    valid = jnp.clip(seg_end - blk_row, 0, MOE_BLK).astype(I32)
    nused = (pends[-1] // MOE_BLK).astype(I32).reshape(1)
    cstart = jnp.concatenate([jnp.zeros((1,), I32), (pends // MOE_BLK).astype(I32)])

    dest_t = _dest(eidx_t, _slab_order(pstart.astype(F32).reshape(N_EXPERTS, 1)))
    nplane = h2p.shape[0]
    gidx = (dest_t.reshape(1, nk) + (jnp.arange(nplane, dtype=I32) * rows_total)[:, None]).reshape(nplane * nk)
    xg = _sc_scatter(h2p.reshape(nplane * n, LANES), gidx, nplane * rows_total, nplane, TOP_K)
    y = _experts(cstart, nused, valid, xg.reshape(nplane, rows_total, LANES), w1[0], w3[0], w2[0])
    y_flat = y.reshape(nplane * rows_total, LANES)
    gidx3 = gidx.reshape(nplane, TOP_K, n)
    gates = gates_t.T
    npart = n // COMB_SPLIT
    out = None
    for part in range(COMB_SPLIT):
        pidx = gidx3[:, :, part * npart:(part + 1) * npart].reshape(nplane * TOP_K * npart)
        yg = _sc_gather(y_flat, pidx).reshape(nplane, TOP_K, npart, LANES)
        out = _combine(yg, gates, x1s, mod3, final_g.reshape(1, d), s, part, out)
    return out.reshape(b, s, d)
```

```python
import functools
import math

import jax
import jax.numpy as jnp
from jax import lax
from jax.experimental import pallas as pl
from jax.experimental.pallas import tpu as pltpu
from jax.experimental.pallas import tpu_sc as plsc

F32 = jnp.float32
BF16 = jnp.bfloat16
U32 = jnp.uint32
I32 = jnp.int32
HIGHEST = lax.Precision.HIGHEST

D_MODEL = 1024
A_HEADS = 8
A_HEAD_DIM = 64
A_WIDTH = A_HEADS * A_HEAD_DIM
A_PATTERNS = ((128, 1), (512, 4), (2048, 16))
A_RADIUS = 64
REL_BUCKETS = 32
REL_MAX_DIST = 1024
B_HEADS = 8
B_NOPE = 64
B_ROPE = 32
B_VDIM = 64
B_WIDTH = B_HEADS * B_VDIM
Q_LORA = 384
KV_LORA = 256
ROPE_THETA = 10000.0
N_EXPERTS = 256
TOP_K = 8
N_GROUPS = 8
GROUP_SIZE = N_EXPERTS // N_GROUPS
TOPK_GROUPS = 4
EXPERT_FF = 256
SHARED_FF = 256
ROUTED_SCALE = 2.5
EPS = 1e-6
NEG_INF = -1e30
LOG2E = math.log2(math.e)

LANES = 128
SUBLANES = 8
HEAD_PAD = 128
IN_COLS_EXT = 3 * A_WIDTH + Q_LORA + KV_LORA + HEAD_PAD

TM_INPROJ = 512
TM_OUTPROJ = 256
DIL_QB = 128
DIL_KW = DIL_QB + 2 * A_RADIUS
DIL_UNROLL = 8
MLA_TQ = 1024
MLA_KC = 2048
ROUTE_T = 256
MOE_BLK = 256
EXP_AHEAD = 3
EXP_XBUF = EXP_AHEAD + 2
SC_WINDOW = 128
COMB_T = 256
COMB_SPLIT = 4
VMEM_LIMIT = 56 * 1024 * 1024


def _cparams(sem):
    return pltpu.CompilerParams(dimension_semantics=sem, vmem_limit_bytes=VMEM_LIMIT)


def _rms(x, g):
    return x * lax.rsqrt(jnp.mean(x * x, axis=-1, keepdims=True) + EPS) * g


def _silu(x):
    return x * jax.nn.sigmoid(x)


def _pack_rows(x):
    half = x.shape[1] // 2
    bits = lax.bitcast_convert_type(x.astype(BF16).astype(F32), U32)
    return (bits[:, :half] >> 16) | bits[:, half:]


def _unpack_rows(w):
    lo = lax.bitcast_convert_type(w << 16, F32)
    hi = lax.bitcast_convert_type(w & jnp.uint32(0xFFFF0000), F32)
    return lo, hi


def _ada_kernel(c_ref, w_ref, b_ref, o_ref):
    o_ref[...] = jnp.dot(_silu(c_ref[...]), w_ref[...], precision=HIGHEST,
                         preferred_element_type=F32) + b_ref[...]


def _ada(c, w_ada, b_ada):
    b, d = c.shape
    n6 = w_ada.shape[1] // d
    return pl.pallas_call(
        _ada_kernel,
        grid=(n6,),
        in_specs=[pl.BlockSpec((b, d), lambda j: (0, 0)),
                  pl.BlockSpec((d, d), lambda j: (0, j)),
                  pl.BlockSpec((1, d), lambda j: (0, j))],
        out_specs=pl.BlockSpec((b, d), lambda j: (0, j)),
        out_shape=jax.ShapeDtypeStruct((b, n6 * d), F32),
        compiler_params=_cparams(("parallel",)),
        name="ada",
    )(c, w_ada, b_ada.reshape(1, -1))


def _t5_bucket(rel):
    half = REL_BUCKETS // 2
    max_exact = half // 2
    ret = jnp.where(rel > 0, half, 0)
    n = jnp.abs(rel)
    nf = jnp.maximum(n, 1).astype(jnp.float32)
    large = max_exact + (jnp.log(nf / max_exact) / math.log(REL_MAX_DIST / max_exact)
                         * (half - max_exact)).astype(jnp.int32)
    large = jnp.minimum(large, half - 1)
    return ret + jnp.where(n < max_exact, n, large)


DIL_SHIFTS = (A_RADIUS, 0, -A_RADIUS)


def _bucket_tiles():
    qi = jnp.arange(DIL_QB, dtype=jnp.int32)[:, None]
    ki = jnp.arange(DIL_KW, dtype=jnp.int32)[None, :]
    tiles = []
    for _, dilation in A_PATTERNS:
        for shift in DIL_SHIFTS:
            off = ki + shift - A_RADIUS - qi
            bkt = _t5_bucket(off * dilation)
            tiles.append(jnp.where(jnp.abs(off) <= A_RADIUS, bkt, -1))
    return jnp.stack(tiles, axis=0)


def _bias_kernel(tab_ref, bkt_ref, o_ref):
    bkt = bkt_ref[0]
    for h in range(A_HEADS):
        acc = jnp.full(bkt.shape, NEG_INF, F32)
        for b in range(REL_BUCKETS):
            acc = jnp.where(bkt == b, tab_ref[b, h] * LOG2E, acc)
        o_ref[0, h] = acc


def _bias_tiles(rel_table):
    bkt = _bucket_tiles()
    nt = bkt.shape[0]
    return pl.pallas_call(
        _bias_kernel,
        grid=(nt,),
        in_specs=[pl.BlockSpec(memory_space=pltpu.SMEM),
                  pl.BlockSpec((1, DIL_QB, DIL_KW), lambda t: (t, 0, 0))],
        out_specs=pl.BlockSpec((1, A_HEADS, DIL_QB, DIL_KW), lambda t: (t, 0, 0, 0)),
        out_shape=jax.ShapeDtypeStruct((nt, A_HEADS, DIL_QB, DIL_KW), F32),
        compiler_params=_cparams(("parallel",)),
        name="bias",
    )(rel_table, bkt)


def _rope(x, cos, sin, lane_lt_mid):
    half = B_ROPE // 2
    rot = jnp.where(lane_lt_mid, pltpu.roll(x, HEAD_PAD - half, 1), pltpu.roll(x, half, 1))
    return x * cos + rot * sin


def _inproj_kernel(x_ref, sc_ref, sh_ref, g1_ref, win_ref, qg_ref, wuq_ref, kvg_ref, wuk_ref, wv_ref,
                   cos_ref, sin_ref, qa_ref, ka_ref, va_ref, qb_ref, kb_ref, vb_ref):
    x = x_ref[...]
    h = _rms(x, g1_ref[...]) * (1.0 + sc_ref[...]) + sh_ref[...]
    proj = jnp.dot(h.astype(BF16), win_ref[...], preferred_element_type=F32)
    aw = A_WIDTH
    qa_ref[...] = proj[:, 0:aw] * (LOG2E / math.sqrt(A_HEAD_DIM))
    ka_ref[...] = proj[:, aw:2 * aw]
    va_ref[...] = proj[:, 2 * aw:3 * aw]
    c0 = 3 * aw
    q_lat = proj[:, c0:c0 + Q_LORA]
    kv_lat = proj[:, c0 + Q_LORA:c0 + Q_LORA + KV_LORA]
    kpe = proj[:, c0 + Q_LORA + KV_LORA:]
    qn = _rms(q_lat, qg_ref[...]).astype(BF16)
    kvn = _rms(kv_lat, kvg_ref[...]).astype(BF16)
    qm = jnp.dot(qn, wuq_ref[...], preferred_element_type=F32)
    kn = jnp.dot(kvn, wuk_ref[...], preferred_element_type=F32)
    vv = jnp.dot(kvn, wv_ref[...], preferred_element_type=F32)
    cos = cos_ref[...]
    sin = sin_ref[...]
    lane = lax.broadcasted_iota(jnp.int32, cos.shape, 1)
    lt_mid = lane < (B_NOPE + B_ROPE // 2)
    qscale = LOG2E / math.sqrt(B_NOPE + B_ROPE)
    for hd in range(B_HEADS):
        sl = slice(hd * HEAD_PAD, (hd + 1) * HEAD_PAD)
        qb_ref[hd] = (_rope(qm[:, sl], cos, sin, lt_mid) * qscale).astype(BF16)
        kb_ref[hd] = _rope(kn[:, sl] + kpe, cos, sin, lt_mid).astype(BF16)
    for p in range(B_HEADS // 2):
        vb_ref[p] = vv[:, p * LANES:(p + 1) * LANES].astype(BF16)


def _inproj(x2, mod3, norm1_g, w_in_ext, q_norm_g, w_uq_p, kv_norm_g, w_uk_p, w_v, cos_t, sin_t, b, s):
    n, d = x2.shape
    tm = TM_INPROJ
    tpb = s // tm
    row = lambda i: (i, 0)
    const = lambda i: (0, 0)
    hm = lambda i: (i // tpb, 0, i % tpb, 0)
    return pl.pallas_call(
        _inproj_kernel,
        grid=(n // tm,),
        in_specs=[pl.BlockSpec((tm, d), row),
                  pl.BlockSpec((None, 1, d), lambda i: ((i // tpb) * 6 + 1, 0, 0)),
                  pl.BlockSpec((None, 1, d), lambda i: ((i // tpb) * 6 + 0, 0, 0)),
                  pl.BlockSpec((1, d), const),
                  pl.BlockSpec(w_in_ext.shape, const),
                  pl.BlockSpec((1, Q_LORA), const),
                  pl.BlockSpec(w_uq_p.shape, const),
                  pl.BlockSpec((1, KV_LORA), const),
                  pl.BlockSpec(w_uk_p.shape, const),
                  pl.BlockSpec(w_v.shape, const),
                  pl.BlockSpec((tm, HEAD_PAD), lambda i: (i % tpb, 0)),
                  pl.BlockSpec((tm, HEAD_PAD), lambda i: (i % tpb, 0))],
        out_specs=[pl.BlockSpec((tm, A_WIDTH), row),
                   pl.BlockSpec((tm, A_WIDTH), row),
                   pl.BlockSpec((tm, A_WIDTH), row),
                   pl.BlockSpec((None, B_HEADS, tm, HEAD_PAD), hm),
                   pl.BlockSpec((None, B_HEADS, tm, HEAD_PAD), hm),
                   pl.BlockSpec((None, B_HEADS // 2, tm, LANES), hm)],
        out_shape=[jax.ShapeDtypeStruct((n, A_WIDTH), F32),
                   jax.ShapeDtypeStruct((n, A_WIDTH), F32),
                   jax.ShapeDtypeStruct((n, A_WIDTH), F32),
                   jax.ShapeDtypeStruct((b, B_HEADS, s, HEAD_PAD), BF16),
                   jax.ShapeDtypeStruct((b, B_HEADS, s, HEAD_PAD), BF16),
                   jax.ShapeDtypeStruct((b, B_HEADS // 2, s, LANES), BF16)],
        compiler_params=_cparams(("parallel",)),
        name="inproj",
    )(x2, mod3, mod3, norm1_g, w_in_ext, q_norm_g, w_uq_p, kv_norm_g, w_uk_p, w_v, cos_t, sin_t)


def _dil_block(q_ref, k_ref, v_ref, bias_ref, o_scr, l_scr, pi, dil, nblk, job):
    sub_len = nblk * DIL_QB
    r = job // nblk
    bi = job % nblk
    q0 = bi * DIL_QB
    ws = jnp.clip(q0 - A_RADIUS, 0, sub_len - DIL_KW)
    var = jnp.where(bi == 0, 0, jnp.where(bi == nblk - 1, 2, 1))
    if dil == 1:
        qsl = pl.ds(pl.multiple_of(q0, DIL_QB), DIL_QB)
        ksl = pl.ds(pl.multiple_of(ws, A_RADIUS), DIL_KW)
    else:
        qsl = pl.ds(r + dil * q0, DIL_QB, stride=dil)
        ksl = pl.ds(r + dil * ws, DIL_KW, stride=dil)
    q = q_ref[qsl, :]
    kw = k_ref[ksl, :].astype(BF16)
    vw = v_ref[ksl, :].astype(BF16)
    lo = lax.broadcasted_iota(jnp.int32, q.shape, 1) < A_HEAD_DIM
    outs, lses = [], []
    for hh in range(2):
        qm = jnp.where(lo if hh == 0 else jnp.logical_not(lo), q, 0.0).astype(BF16)
        sc = lax.dot_general(qm, kw, (((1,), (1,)), ((), ())), preferred_element_type=F32)
        sc = sc + bias_ref[pi * 3 + var, hh]
        m = jnp.max(sc, axis=-1, keepdims=True)
        p = jnp.exp2(sc - m)
        l = jnp.sum(p, axis=-1, keepdims=True)
        o = jnp.dot(p.astype(BF16), vw, preferred_element_type=F32)
        outs.append(o / l)
        lses.append(m + jnp.log2(l))
    o_scr[pi, qsl, :] = jnp.where(lo, outs[0], outs[1])
    l_scr[pi, qsl, :] = jnp.where(lo, lses[0], lses[1])


def _dilated_kernel(q_ref, k_ref, v_ref, bias_ref, out_ref, o_scr, l_scr):
    s = q_ref.shape[0]
    njobs = s // DIL_QB
    for pi, (_, dil) in enumerate(A_PATTERNS):
        blk = functools.partial(_dil_block, q_ref, k_ref, v_ref, bias_ref, o_scr, l_scr, pi, dil,
                                s // dil // DIL_QB)

        def group(g, c, blk=blk):
            for u in range(DIL_UNROLL):
                blk(g * DIL_UNROLL + u)
            return c

        lax.fori_loop(0, njobs // DIL_UNROLL, group, 0)

    chunk = 512

    def comb(i, c):
        rows = pl.ds(pl.multiple_of(i * chunk, chunk), chunk)
        l0, l1, l2 = l_scr[0, rows, :], l_scr[1, rows, :], l_scr[2, rows, :]
        mx = jnp.maximum(jnp.maximum(l0, l1), l2)
        e0, e1, e2 = jnp.exp2(l0 - mx), jnp.exp2(l1 - mx), jnp.exp2(l2 - mx)
        num = e0 * o_scr[0, rows, :] + e1 * o_scr[1, rows, :] + e2 * o_scr[2, rows, :]
        out_ref[rows, :] = num / (e0 + e1 + e2)
        return c

    lax.fori_loop(0, s // chunk, comb, 0)


def _dilated(qa, ka, va, bias):
    b, s, _ = qa.shape
    npair = A_WIDTH // LANES
    assert (s // DIL_QB) % DIL_UNROLL == 0
    for _, dil in A_PATTERNS:
        assert (s // dil) % DIL_QB == 0 and s // dil >= DIL_KW
    blk = pl.BlockSpec((None, s, LANES), lambda bi, p: (bi, 0, p))
    return pl.pallas_call(
        _dilated_kernel,
        grid=(b, npair),
        in_specs=[blk, blk, blk,
                  pl.BlockSpec((bias.shape[0], 2, DIL_QB, DIL_KW), lambda bi, p: (0, p, 0, 0))],
        out_specs=blk,
        out_shape=jax.ShapeDtypeStruct((b, s, A_WIDTH), F32),
        scratch_shapes=[pltpu.VMEM((len(A_PATTERNS), s, LANES), F32),
                        pltpu.VMEM((len(A_PATTERNS), s, LANES), F32)],
        compiler_params=_cparams(("parallel", "parallel")),
        name="dilated",
    )(qa, ka, va, bias)


def _mla_kernel(q_ref, k_ref, v_ref, o_ref):
    tq = q_ref.shape[1]
    nkc = k_ref.shape[1] // MLA_KC
    outs = []
    for hh in range(2):
        q = q_ref[hh]
        m = jnp.full((tq, 1), -jnp.inf, F32)
        l = jnp.zeros((tq, 1), F32)
        acc = jnp.zeros((tq, LANES), F32)
        for c in range(nkc):
            keys = slice(c * MLA_KC, (c + 1) * MLA_KC)
            sc = lax.dot_general(q, k_ref[hh, keys, :], (((1,), (1,)), ((), ())), preferred_element_type=F32)
            m_new = jnp.maximum(m, jnp.max(sc, axis=-1, keepdims=True))
            alpha = jnp.exp2(m - m_new)
            p = jnp.exp2(sc - m_new)
            l = alpha * l + jnp.sum(p, axis=-1, keepdims=True)
            acc = alpha * acc + jnp.dot(p.astype(BF16), v_ref[keys, :], preferred_element_type=F32)
            m = m_new
        outs.append(acc / l)
    lo = lax.broadcasted_iota(jnp.int32, outs[0].shape, 1) < B_VDIM
    o_ref[...] = jnp.where(lo, outs[0], outs[1])


def _mla(qb, kb, vb):
    b, h, s, _ = qb.shape
    npair = h // 2
    tq = MLA_TQ
    return pl.pallas_call(
        _mla_kernel,
        grid=(b, npair, s // tq),
        in_specs=[pl.BlockSpec((None, 2, tq, HEAD_PAD), lambda bi, p, qi: (bi, p, qi, 0)),
                  pl.BlockSpec((None, 2, s, HEAD_PAD), lambda bi, p, qi: (bi, p, 0, 0)),
                  pl.BlockSpec((None, None, s, LANES), lambda bi, p, qi: (bi, p, 0, 0))],
        out_specs=pl.BlockSpec((None, tq, LANES), lambda bi, p, qi: (bi, qi, p)),
        out_shape=jax.ShapeDtypeStruct((b, s, B_WIDTH), F32),
        compiler_params=_cparams(("parallel", "parallel", "arbitrary")),
        name="mla",
    )(qb, kb, vb)


def _outproj_kernel(x_ref, a_ref, b_ref, ag_ref, bg_ref, wo_ref, g1_ref, n2_ref, sc_ref, sh_ref, g2_ref,
                    wrt_ref, ws1_ref, ws3_ref, ws2_ref, x1s_ref, h2p_ref, scores_ref):
    an = _rms(a_ref[...], ag_ref[...])
    bn = _rms(b_ref[...], bg_ref[...])
    mix = jnp.concatenate([an, bn], axis=-1).astype(BF16)
    x1 = x_ref[...] + g1_ref[...] * jnp.dot(mix, wo_ref[...], preferred_element_type=F32)
    h2 = _rms(x1, n2_ref[...]) * (1.0 + sc_ref[...]) + sh_ref[...]
    packed = _pack_rows(h2)
    for cg in range(h2p_ref.shape[0]):
        h2p_ref[cg] = packed[:, cg * LANES:(cg + 1) * LANES]
    logits = lax.dot_general(wrt_ref[...], h2, (((1,), (1,)), ((), ())), precision=HIGHEST,
                             preferred_element_type=F32)
    scores_ref[...] = jax.nn.sigmoid(logits)
    h2b = h2.astype(BF16)
    hid = _silu(jnp.dot(h2b, ws1_ref[...], preferred_element_type=F32)) * jnp.dot(
        h2b, ws3_ref[...], preferred_element_type=F32)
    shared = jnp.dot(hid.astype(BF16), ws2_ref[...], preferred_element_type=F32)
    x1s_ref[...] = x1 + g2_ref[...] * shared


def _outproj(x2, a_out, b_out, a_out_g, b_out_g, w_o, mod3, norm2_g, w_router_t, ws1, ws3, ws2, s):
    n, d = x2.shape
    tm = TM_OUTPROJ
    tpb = s // tm
    row = lambda i: (i, 0)
    const = lambda i: (0, 0)
    modspec = lambda j: pl.BlockSpec((None, 1, d), lambda i: ((i // tpb) * 6 + j, 0, 0))
    return pl.pallas_call(
        _outproj_kernel,
        grid=(n // tm,),
        in_specs=[pl.BlockSpec((tm, d), row),
                  pl.BlockSpec((tm, A_WIDTH), row),
                  pl.BlockSpec((tm, B_WIDTH), row),
                  pl.BlockSpec((1, A_WIDTH), const),
                  pl.BlockSpec((1, B_WIDTH), const),
                  pl.BlockSpec(w_o.shape, const),
                  modspec(2),
                  pl.BlockSpec((1, d), const),
                  modspec(4), modspec(3), modspec(5),
                  pl.BlockSpec(w_router_t.shape, const),
                  pl.BlockSpec(ws1.shape, const),
                  pl.BlockSpec(ws3.shape, const),
                  pl.BlockSpec(ws2.shape, const)],
        out_specs=[pl.BlockSpec((tm, d), row),
                   pl.BlockSpec((d // 2 // LANES, tm, LANES), lambda i: (0, i, 0)),
                   pl.BlockSpec((N_EXPERTS, tm), lambda i: (0, i))],
        out_shape=[jax.ShapeDtypeStruct((n, d), F32),
                   jax.ShapeDtypeStruct((d // 2 // LANES, n, LANES), U32),
                   jax.ShapeDtypeStruct((N_EXPERTS, n), F32)],
        compiler_params=_cparams(("parallel",)),
        name="outproj",
    )(x2, a_out, b_out, a_out_g, b_out_g, w_o, mod3, norm2_g, mod3, mod3, mod3, w_router_t, ws1, ws3, ws2)


def _slab_order(v):
    return v.reshape((N_GROUPS, GROUP_SIZE) + v.shape[1:]).swapaxes(0, 1).reshape(v.shape)


def _expert_order(v):
    return v.reshape((GROUP_SIZE, N_GROUPS) + v.shape[1:]).swapaxes(0, 1).reshape(v.shape)


def _sublane_all(x, op):
    for sh in (4, 2, 1):
        x = op(x, pltpu.roll(x, sh, 0))
    return x


def _route_kernel(st_ref, bias_ref, eidx_ref, gate_ref, cnt_ref):
    nsl = GROUP_SIZE
    t = st_ref.shape[1]
    sub = lax.broadcasted_iota(I32, (SUBLANES, t), 0)
    ninf = -jnp.inf
    big = jnp.int32(1 << 30)
    sc = [st_ref[j * SUBLANES:(j + 1) * SUBLANES, :] for j in range(nsl)]
    sel = [sc[j] + bias_ref[j * SUBLANES:(j + 1) * SUBLANES, :] for j in range(nsl)]
    eid = [sub * GROUP_SIZE + j for j in range(nsl)]

    m1 = sel[0]
    m2 = jnp.full_like(m1, ninf)
    for j in range(1, nsl):
        m2 = jnp.maximum(m2, jnp.minimum(m1, sel[j]))
        m1 = jnp.maximum(m1, sel[j])
    gs = m1 + m2

    rank = jnp.zeros((SUBLANES, t), I32)
    for sh in range(1, N_GROUPS):
        other = pltpu.roll(gs, sh, 0)
        ahead = (other > gs) | ((other == gs) & (sub >= sh))
        rank = rank + ahead.astype(I32)
    gmask = rank < TOPK_GROUPS

    msel = [jnp.where(gmask, sel[j], ninf) for j in range(nsl)]
    hits = [jnp.zeros((SUBLANES, t), I32) for _ in range(nsl)]
    eidx = jnp.zeros((TOP_K, t), I32)
    gates = jnp.zeros((TOP_K, t), F32)
    for k in range(TOP_K):
        mx = msel[0]
        for j in range(1, nsl):
            mx = jnp.maximum(mx, msel[j])
        mx = _sublane_all(mx, jnp.maximum)
        cand = jnp.where(msel[0] == mx, eid[0], big)
        for j in range(1, nsl):
            cand = jnp.minimum(cand, jnp.where(msel[j] == mx, eid[j], big))
        idx = _sublane_all(cand, jnp.minimum)
        gk = jnp.zeros((SUBLANES, t), F32)
        for j in range(nsl):
            hit = eid[j] == idx
            gk = gk + jnp.where(hit, sc[j], 0.0)
            msel[j] = jnp.where(hit, ninf, msel[j])
            hits[j] = hits[j] + hit.astype(I32)
        gk = _sublane_all(gk, jnp.add)
        eidx = jnp.where(sub == k, idx, eidx)
        gates = jnp.where(sub == k, gk, gates)
    gsum = _sublane_all(gates, jnp.add)
    eidx_ref[...] = eidx
    gate_ref[...] = gates / gsum * ROUTED_SCALE

    @pl.when(pl.program_id(0) == 0)
    def _():
        cnt_ref[...] = jnp.zeros_like(cnt_ref)

    for j in range(nsl):
        cnt_ref[j * SUBLANES:(j + 1) * SUBLANES, :] += jnp.sum(hits[j].astype(F32), axis=1,
                                                               keepdims=True).astype(I32)


def _route(scores_t, e_bias_slab):
    e, n = scores_t.shape
    t = ROUTE_T
    return pl.pallas_call(
        _route_kernel,
        grid=(n // t,),
        in_specs=[pl.BlockSpec((e, t), lambda i: (0, i)),
                  pl.BlockSpec((e, 1), lambda i: (0, 0))],
        out_specs=[pl.BlockSpec((TOP_K, t), lambda i: (0, i)),
                   pl.BlockSpec((TOP_K, t), lambda i: (0, i)),
                   pl.BlockSpec((e, 1), lambda i: (0, 0))],
        out_shape=[jax.ShapeDtypeStruct((TOP_K, n), I32),
                   jax.ShapeDtypeStruct((TOP_K, n), F32),
                   jax.ShapeDtypeStruct((e, 1), I32)],
        compiler_params=_cparams(("arbitrary",)),
        name="route",
    )(scores_t, e_bias_slab)


def _dest_kernel(eidx_ref, pstart_ref, dest_ref, carry_ref):
    @pl.when(pl.program_id(0) == 0)
    def _():
        carry_ref[...] = jnp.zeros_like(carry_ref)

    nsl = GROUP_SIZE
    t = eidx_ref.shape[1]
    sub = lax.broadcasted_iota(I32, (SUBLANES, t), 0)
    eid = [sub * GROUP_SIZE + j for j in range(nsl)]
    ek = [eidx_ref[k:k + 1, :] for k in range(TOP_K)]
    slabs = []
    for j in range(nsl):
        oh = jnp.zeros((SUBLANES, t), F32)
        for k in range(TOP_K):
            oh = oh + (eid[j] == ek[k]).astype(F32)
        slabs.append(oh)
    onehot = jnp.concatenate(slabs, axis=0)
    row = lax.broadcasted_iota(I32, (t, t), 0)
    col = lax.broadcasted_iota(I32, (t, t), 1)
    upper = (row < col).astype(BF16)
    before = jnp.dot(onehot.astype(BF16), upper, preferred_element_type=F32)
    base = before + carry_ref[...] + pstart_ref[...]
    dest = jnp.zeros((TOP_K, t), I32)
    for k in range(TOP_K):
        acc = jnp.zeros((SUBLANES, t), F32)
        for j in range(nsl):
            acc = acc + jnp.where(eid[j] == ek[k], base[j * SUBLANES:(j + 1) * SUBLANES, :], 0.0)
        dk = _sublane_all(acc, jnp.add).astype(I32)
        dest = jnp.where(sub == k, dk, dest)
    dest_ref[...] = dest
    carry_ref[...] += jnp.sum(onehot, axis=1, keepdims=True)


def _dest(eidx_t, pstart_slab):
    n = eidx_t.shape[1]
    t = ROUTE_T
    return pl.pallas_call(
        _dest_kernel,
        grid=(n // t,),
        in_specs=[pl.BlockSpec((TOP_K, t), lambda i: (0, i)),
                  pl.BlockSpec((N_EXPERTS, 1), lambda i: (0, 0))],
        out_specs=pl.BlockSpec((TOP_K, t), lambda i: (0, i)),
        out_shape=jax.ShapeDtypeStruct((TOP_K, n), I32),
        scratch_shapes=[pltpu.VMEM((N_EXPERTS, 1), F32)],
        compiler_params=_cparams(("arbitrary",)),
        name="dest",
    )(eidx_t, pstart_slab)


def _sc_scatter(x, idx, rows_out, nplane, nslot):
    num = idx.shape[0]
    nwin = x.shape[0] // nplane // SC_WINDOW
    mesh = plsc.VectorSubcoreMesh(core_axis_name="core", subcore_axis_name="subcore")

    @pl.kernel(out_type=jax.ShapeDtypeStruct((rows_out, x.shape[1]), x.dtype), mesh=mesh, scratch_types=[])
    def scatter(x_hbm, i_hbm, o_hbm):
        def body(x_vmem, i_vmem):
            pltpu.sync_copy(x_vmem, o_hbm.at[i_vmem.at[0]])

        pltpu.emit_pipeline(
            body,
            grid=(num // SC_WINDOW,),
            in_specs=[pl.BlockSpec((SC_WINDOW, x.shape[1]),
                                   index_map=lambda i: ((i // (nslot * nwin)) * nwin + i % nwin, 0)),
                      pl.BlockSpec((1, SC_WINDOW), index_map=lambda i: (0, i))],
            out_specs=[],
            core_axis_name=("core", "subcore"),
            dimension_semantics=(pltpu.PARALLEL,),
        )(x_hbm, i_hbm)

    return scatter(x, idx.reshape(1, num))


def _expert_kernel(cstart_ref, nused_ref, valid_ref, w1_ref, w3_ref, w2_ref, xg_ref, y_ref,
                   w1b, w3b, w2b, xbuf, ybuf, xsem, ysem):
    e = pl.program_id(0)
    c0 = cstart_ref[e]
    c1 = cstart_ref[e + 1]
    nused = nused_ref[0]
    nchunks = xg_ref.shape[1] // MOE_BLK

    def rows(g):
        start = g * MOE_BLK
        return pl.ds(start if isinstance(g, int) else pl.multiple_of(start, MOE_BLK), MOE_BLK)

    def xcopies(g):
        slot = g % EXP_XBUF
        return [pltpu.make_async_copy(xg_ref.at[c, rows(g)],
                                      xbuf.at[slot, pl.ds(0, MOE_BLK), pl.ds(c * LANES, LANES)], xsem.at[slot])
                for c in range(xg_ref.shape[0])]

    def ycopies(g, slot):
        return [pltpu.make_async_copy(ybuf.at[slot, pl.ds(0, MOE_BLK), pl.ds(c * LANES, LANES)],
                                      y_ref.at[c, rows(g)], ysem.at[slot])
                for c in range(y_ref.shape[0])]

    @pl.when(e == 0)
    def _():
        for j in range(EXP_AHEAD):
            @pl.when(j < nused)
            def _():
                for cp in xcopies(j):
                    cp.start()

    @pl.when(c1 > c0)
    def _():
        w1b[...] = w1_ref[...].astype(BF16)
        w3b[...] = w3_ref[...].astype(BF16)
        w2b[...] = w2_ref[...].astype(BF16)

        def ffn(g):
            words = xbuf[g % EXP_XBUF]
            row_id = lax.broadcasted_iota(I32, words.shape, 0)
            lo, hi = _unpack_rows(jnp.where(row_id < valid_ref[g], words, jnp.uint32(0)))
            xb = jnp.concatenate([lo, hi], axis=1).astype(BF16)
            hid = _silu(jnp.dot(xb, w1b[...], preferred_element_type=F32)) * jnp.dot(
                xb, w3b[...], preferred_element_type=F32)
            return _pack_rows(jnp.dot(hid.astype(BF16), w2b[...], preferred_element_type=F32))

        def chunks(g0, cnt):
            gs = [g0 + j for j in range(cnt)]
            for g in gs:
                @pl.when(g + EXP_AHEAD < nused)
                def _():
                    for cp in xcopies(g + EXP_AHEAD):
                        cp.start()
            for g in gs:
                for cp in xcopies(g):
                    cp.wait()
            yps = [ffn(g) for g in gs]
            for g in gs:
                @pl.when(g >= 2)
                def _():
                    for cp in ycopies(g - 2, g % 2):
                        cp.wait()
            for g, yp in zip(gs, yps):
                ybuf[g % 2] = yp
                for cp in ycopies(g, g % 2):
                    cp.start()

        def pair(i, carry):
            chunks(c0 + 2 * i, 2)
            return carry

        lax.fori_loop(0, (c1 - c0) // 2, pair, 0)

        @pl.when((c1 - c0) % 2 == 1)
        def _():
            chunks(c1 - 1, 1)

    @pl.when(e == pl.num_programs(0) - 1)
    def _():
        for back in (2, 1):
            g = nused - back

            @pl.when(g >= 0)
            def _():
                for cp in ycopies(g, g % 2):
                    cp.wait()

        ybuf[0] = jnp.zeros(ybuf.shape[1:], ybuf.dtype)

        def zstart(g, c):
            for cp in ycopies(g, 0):
                cp.start()
            return c

        def zwait(g, c):
            for cp in ycopies(g, 0):
                cp.wait()
            return c

        lax.fori_loop(nused, nchunks, zstart, 0)
        lax.fori_loop(nused, nchunks, zwait, 0)


def _experts(cstart, nused, valid, xg, w1, w3, w2):
    nplane, rows, _ = xg.shape
    dh = nplane * LANES
    ne, d, f = w1.shape
    grid_spec = pltpu.PrefetchScalarGridSpec(
        num_scalar_prefetch=3,
        grid=(ne,),
        in_specs=[pl.BlockSpec((None, d, f), lambda e, cs, nu, va: (e, 0, 0)),
                  pl.BlockSpec((None, d, f), lambda e, cs, nu, va: (e, 0, 0)),
                  pl.BlockSpec((None, f, d), lambda e, cs, nu, va: (e, 0, 0)),
                  pl.BlockSpec(memory_space=pl.ANY)],
        out_specs=pl.BlockSpec(memory_space=pl.ANY),
        scratch_shapes=[pltpu.VMEM((d, f), BF16), pltpu.VMEM((d, f), BF16), pltpu.VMEM((f, d), BF16),
                        pltpu.VMEM((EXP_XBUF, MOE_BLK, dh), U32), pltpu.VMEM((2, MOE_BLK, dh), U32),
                        pltpu.SemaphoreType.DMA((EXP_XBUF,)), pltpu.SemaphoreType.DMA((2,))],
    )
    return pl.pallas_call(
        _expert_kernel,
        grid_spec=grid_spec,
        out_shape=jax.ShapeDtypeStruct((dh // LANES, rows, LANES), U32),
        compiler_params=_cparams(("arbitrary",)),
        name="experts",
    )(cstart, nused, valid, w1, w3, w2, xg)


def _sc_gather(x, idx):
    num = idx.shape[0]
    mesh = plsc.VectorSubcoreMesh(core_axis_name="core", subcore_axis_name="subcore")

    @pl.kernel(out_type=jax.ShapeDtypeStruct((num, x.shape[1]), x.dtype), mesh=mesh)
    def gather(x_hbm, i_hbm, o_hbm):
        def body(i_vmem, o_vmem):
            pltpu.sync_copy(x_hbm.at[i_vmem.at[0]], o_vmem)

        pltpu.emit_pipeline(
            body,
            grid=(num // SC_WINDOW,),
            in_specs=[pl.BlockSpec((1, SC_WINDOW), index_map=lambda i: (0, i))],
            out_specs=[pl.BlockSpec((SC_WINDOW, x.shape[1]), index_map=lambda i: (i, 0))],
            core_axis_name=("core", "subcore"),
            dimension_semantics=(pltpu.PARALLEL,),
        )(i_hbm, o_hbm)

    return gather(x, idx.reshape(1, num))


def _combine_kernel(yg_ref, gate_ref, x1s_ref, g2_ref, fg_ref, *rest):
    out_ref = rest[-1]
    gates = gate_ref[...]
    nch = yg_ref.shape[0]
    r_lo = [None] * nch
    r_hi = [None] * nch
    for k in range(TOP_K):
        gk = gates[:, k:k + 1]
        for c in range(nch):
            lo, hi = _unpack_rows(yg_ref[c, k])
            r_lo[c] = gk * lo if k == 0 else r_lo[c] + gk * lo
            r_hi[c] = gk * hi if k == 0 else r_hi[c] + gk * hi
    routed = jnp.concatenate(r_lo + r_hi, axis=1)
    x2 = x1s_ref[...] + g2_ref[...] * routed
    out_ref[...] = _rms(x2, fg_ref[...])


def _combine(yg, gates, x1s, mod3, final_g, s, part, prev_out):
    n, d = x1s.shape
    t = COMB_T
    tpb = s // t
    nch, _, npart, _ = yg.shape
    off = part * (npart // t)
    in_specs = [pl.BlockSpec((nch, TOP_K, t, LANES), lambda i: (0, 0, i, 0)),
                pl.BlockSpec((t, TOP_K), lambda i: (i + off, 0)),
                pl.BlockSpec((t, d), lambda i: (i + off, 0)),
                pl.BlockSpec((None, 1, d), lambda i: (((i + off) // tpb) * 6 + 5, 0, 0)),
                pl.BlockSpec((1, d), lambda i: (0, 0))]
    args = [yg, gates, x1s, mod3, final_g]
    aliases = {}
    if prev_out is not None:
        in_specs.append(pl.BlockSpec(memory_space=pl.ANY))
        args.append(prev_out)
        aliases = {len(args) - 1: 0}
    return pl.pallas_call(
        _combine_kernel,
        grid=(npart // t,),
        in_specs=in_specs,
        out_specs=pl.BlockSpec((t, d), lambda i: (i + off, 0)),
        out_shape=jax.ShapeDtypeStruct((n, d), F32),
        input_output_aliases=aliases,
        compiler_params=_cparams(("parallel",)),
        name="combine",
    )(*args)


def _place_heads(w, per_head, keep):
    r = w.shape[0]
    w = w.reshape(r, B_HEADS, per_head)[:, :, :keep]
    return jnp.pad(w, ((0, 0), (0, 0), (0, HEAD_PAD - keep))).reshape(r, B_HEADS * HEAD_PAD)


def _rope_tables(s):
    inv = ROPE_THETA ** (-jnp.arange(0, B_ROPE, 2, dtype=jnp.float32) / B_ROPE)
    ang = jnp.arange(s, dtype=jnp.float32)[:, None] * inv[None, :]
    cos, sin = jnp.cos(ang), jnp.sin(ang)
    ones = jnp.ones((s, B_NOPE), F32)
    zeros = jnp.zeros((s, B_NOPE), F32)
    tail1 = jnp.ones((s, HEAD_PAD - B_NOPE - B_ROPE), F32)
    tail0 = jnp.zeros((s, HEAD_PAD - B_NOPE - B_ROPE), F32)
    return (jnp.concatenate([ones, cos, cos, tail1], axis=1),
            jnp.concatenate([zeros, -sin, sin, tail0], axis=1))


def kernel(x, c, w_ada, b_ada, norm1_g, w_in, q_norm_g, w_uq, kv_norm_g, w_ukv, rel_table, a_out_g, b_out_g,
           w_o, norm2_g, w_router, e_bias, w1, w3, w2, ws1, ws3, ws2, final_g):
    b, s, d = x.shape
    n = b * s
    assert w_ada.shape[0] == 1, "single layer"
    x2 = x.reshape(n, d)

    mod = _ada(c, w_ada[0], b_ada[0])
    mod3 = mod.reshape(b * 6, 1, d)
    bias = _bias_tiles(rel_table)

    wi = w_in[0]
    c_kpe = 3 * A_WIDTH + Q_LORA + KV_LORA
    kpe_cols = jnp.pad(wi[:, c_kpe:], ((0, 0), (B_NOPE, HEAD_PAD - B_NOPE - B_ROPE)))
    w_in_ext = jnp.concatenate([wi[:, :c_kpe], kpe_cols], axis=1).astype(BF16)
    w_uq_p = _place_heads(w_uq[0], B_NOPE + B_ROPE, B_NOPE + B_ROPE).astype(BF16)
    w_uk_p = _place_heads(w_ukv[0], B_NOPE + B_VDIM, B_NOPE).astype(BF16)
    w_v = w_ukv[0].reshape(KV_LORA, B_HEADS, B_NOPE + B_VDIM)[:, :, B_NOPE:].reshape(KV_LORA, B_WIDTH).astype(BF16)
    cos_t, sin_t = _rope_tables(s)

    qa, ka, va, qb, kb, vb = _inproj(x2, mod3, norm1_g, w_in_ext, q_norm_g, w_uq_p, kv_norm_g, w_uk_p, w_v,
                                     cos_t, sin_t, b, s)
    a_out = _dilated(qa.reshape(b, s, A_WIDTH), ka.reshape(b, s, A_WIDTH), va.reshape(b, s, A_WIDTH), bias)
    b_out = _mla(qb, kb, vb)

    w_router_t = _slab_order(w_router[0].T)
    x1s, h2p, scores_t = _outproj(x2, a_out.reshape(n, A_WIDTH), b_out.reshape(n, B_WIDTH), a_out_g, b_out_g,
                                  w_o[0].astype(BF16), mod3, norm2_g, w_router_t,
                                  ws1[0].astype(BF16), ws3[0].astype(BF16), ws2[0].astype(BF16), s)

    eidx_t, gates_t, counts_slab = _route(scores_t, _slab_order(e_bias.reshape(N_EXPERTS, 1)))

    counts = _expert_order(counts_slab)[:, 0]
    padded = (counts + MOE_BLK - 1) // MOE_BLK * MOE_BLK
    pends = jnp.cumsum(padded)
    pstart = pends - padded
    nk = n * TOP_K
    nblk = -(-(nk + N_EXPERTS * (MOE_BLK - 1)) // MOE_BLK)
    rows_total = nblk * MOE_BLK
    blk_row = jnp.arange(nblk, dtype=I32) * MOE_BLK
    blk_e = jnp.minimum(jnp.sum((pends[None, :] <= blk_row[:, None]).astype(I32), axis=1), N_EXPERTS - 1)
    seg_end = (pstart + counts)[blk_e]
    valid = jnp.clip(seg_end - blk_row, 0, MOE_BLK).astype(I32)
    nused = (pends[-1] // MOE_BLK).astype(I32).reshape(1)
    cstart = jnp.concatenate([jnp.zeros((1,), I32), (pends // MOE_BLK).astype(I32)])

    dest_t = _dest(eidx_t, _slab_order(pstart.astype(F32).reshape(N_EXPERTS, 1)))
    nplane = h2p.shape[0]
    gidx = (dest_t.reshape(1, nk) + (jnp.arange(nplane, dtype=I32) * rows_total)[:, None]).reshape(nplane * nk)
    xg = _sc_scatter(h2p.reshape(nplane * n, LANES), gidx, nplane * rows_total, nplane, TOP_K)
    y = _experts(cstart, nused, valid, xg.reshape(nplane, rows_total, LANES), w1[0], w3[0], w2[0])
    y_flat = y.reshape(nplane * rows_total, LANES)
    gidx3 = gidx.reshape(nplane, TOP_K, n)
    gates = gates_t.T
    npart = n // COMB_SPLIT
    out = None
    for part in range(COMB_SPLIT):
        pidx = gidx3[:, :, part * npart:(part + 1) * npart].reshape(nplane * TOP_K * npart)
        yg = _sc_gather(y_flat, pidx).reshape(nplane, TOP_K, npart, LANES)
        out = _combine(yg, gates, x1s, mod3, final_g.reshape(1, d), s, part, out)
    return out.reshape(b, s, d)
```

```python
import functools
import math

import jax
import jax.numpy as jnp
from jax import lax
from jax.experimental import pallas as pl
from jax.experimental.pallas import tpu as pltpu
from jax.experimental.pallas import tpu_sc as plsc

F32 = jnp.float32
BF16 = jnp.bfloat16
U32 = jnp.uint32
I32 = jnp.int32
HIGHEST = lax.Precision.HIGHEST

D_MODEL = 1024
A_HEADS = 8
A_HEAD_DIM = 64
A_WIDTH = A_HEADS * A_HEAD_DIM
A_PATTERNS = ((128, 1), (512, 4), (2048, 16))
A_RADIUS = 64
REL_BUCKETS = 32
REL_MAX_DIST = 1024
B_HEADS = 8
B_NOPE = 64
B_ROPE = 32
B_VDIM = 64
B_WIDTH = B_HEADS * B_VDIM
Q_LORA = 384
KV_LORA = 256
ROPE_THETA = 10000.0
N_EXPERTS = 256
TOP_K = 8
N_GROUPS = 8
GROUP_SIZE = N_EXPERTS // N_GROUPS
TOPK_GROUPS = 4
EXPERT_FF = 256
SHARED_FF = 256
ROUTED_SCALE = 2.5
EPS = 1e-6
NEG_INF = -1e30
LOG2E = math.log2(math.e)

LANES = 128
SUBLANES = 8
HEAD_PAD = 128
IN_COLS_EXT = 3 * A_WIDTH + Q_LORA + KV_LORA + HEAD_PAD

TM_INPROJ = 512
TM_OUTPROJ = 256
DIL_QB = 128
DIL_KW = DIL_QB + 2 * A_RADIUS
DIL_UNROLL = 8
MLA_TQ = 1024
MLA_KC = 2048
ROUTE_T = 256
MOE_BLK = 256
EXP_AHEAD = 3
EXP_XBUF = EXP_AHEAD + 1
SC_WINDOW = 128
COMB_T = 256
COMB_SPLIT = 4
VMEM_LIMIT = 56 * 1024 * 1024


def _cparams(sem):
    return pltpu.CompilerParams(dimension_semantics=sem, vmem_limit_bytes=VMEM_LIMIT)


def _rms(x, g):
    return x * lax.rsqrt(jnp.mean(x * x, axis=-1, keepdims=True) + EPS) * g


def _silu(x):
    return x * jax.nn.sigmoid(x)


def _pack_rows(x):
    half = x.shape[1] // 2
    bits = lax.bitcast_convert_type(x.astype(BF16).astype(F32), U32)
    return (bits[:, :half] >> 16) | bits[:, half:]


def _unpack_rows(w):
    lo = lax.bitcast_convert_type(w << 16, F32)
    hi = lax.bitcast_convert_type(w & jnp.uint32(0xFFFF0000), F32)
    return lo, hi


def _ada_kernel(c_ref, w_ref, b_ref, o_ref):
    o_ref[...] = jnp.dot(_silu(c_ref[...]), w_ref[...], precision=HIGHEST,
                         preferred_element_type=F32) + b_ref[...]


def _ada(c, w_ada, b_ada):
    b, d = c.shape
    n6 = w_ada.shape[1] // d
    return pl.pallas_call(
        _ada_kernel,
        grid=(n6,),
        in_specs=[pl.BlockSpec((b, d), lambda j: (0, 0)),
                  pl.BlockSpec((d, d), lambda j: (0, j)),
                  pl.BlockSpec((1, d), lambda j: (0, j))],
        out_specs=pl.BlockSpec((b, d), lambda j: (0, j)),
        out_shape=jax.ShapeDtypeStruct((b, n6 * d), F32),
        compiler_params=_cparams(("parallel",)),
        name="ada",
    )(c, w_ada, b_ada.reshape(1, -1))


def _t5_bucket(rel):
    half = REL_BUCKETS // 2
    max_exact = half // 2
    ret = jnp.where(rel > 0, half, 0)
    n = jnp.abs(rel)
    nf = jnp.maximum(n, 1).astype(jnp.float32)
    large = max_exact + (jnp.log(nf / max_exact) / math.log(REL_MAX_DIST / max_exact)
                         * (half - max_exact)).astype(jnp.int32)
    large = jnp.minimum(large, half - 1)
    return ret + jnp.where(n < max_exact, n, large)


DIL_SHIFTS = (A_RADIUS, 0, -A_RADIUS)


def _bucket_tiles():
    qi = jnp.arange(DIL_QB, dtype=jnp.int32)[:, None]
    ki = jnp.arange(DIL_KW, dtype=jnp.int32)[None, :]
    tiles = []
    for _, dilation in A_PATTERNS:
        for shift in DIL_SHIFTS:
            off = ki + shift - A_RADIUS - qi
            bkt = _t5_bucket(off * dilation)
            tiles.append(jnp.where(jnp.abs(off) <= A_RADIUS, bkt, -1))
    return jnp.stack(tiles, axis=0)


def _bias_kernel(tab_ref, bkt_ref, o_ref):
    bkt = bkt_ref[0]
    for h in range(A_HEADS):
        acc = jnp.full(bkt.shape, NEG_INF, F32)
        for b in range(REL_BUCKETS):
            acc = jnp.where(bkt == b, tab_ref[b, h] * LOG2E, acc)
        o_ref[0, h] = acc


def _bias_tiles(rel_table):
    bkt = _bucket_tiles()
    nt = bkt.shape[0]
    return pl.pallas_call(
        _bias_kernel,
        grid=(nt,),
        in_specs=[pl.BlockSpec(memory_space=pltpu.SMEM),
                  pl.BlockSpec((1, DIL_QB, DIL_KW), lambda t: (t, 0, 0))],
        out_specs=pl.BlockSpec((1, A_HEADS, DIL_QB, DIL_KW), lambda t: (t, 0, 0, 0)),
        out_shape=jax.ShapeDtypeStruct((nt, A_HEADS, DIL_QB, DIL_KW), F32),
        compiler_params=_cparams(("parallel",)),
        name="bias",
    )(rel_table, bkt)


def _rope(x, cos, sin, lane_lt_mid):
    half = B_ROPE // 2
    rot = jnp.where(lane_lt_mid, pltpu.roll(x, HEAD_PAD - half, 1), pltpu.roll(x, half, 1))
    return x * cos + rot * sin


def _inproj_kernel(x_ref, sc_ref, sh_ref, g1_ref, win_ref, qg_ref, wuq_ref, kvg_ref, wuk_ref, wv_ref,
                   cos_ref, sin_ref, qa_ref, ka_ref, va_ref, qb_ref, kb_ref, vb_ref):
    x = x_ref[...]
    h = _rms(x, g1_ref[...]) * (1.0 + sc_ref[...]) + sh_ref[...]
    proj = jnp.dot(h.astype(BF16), win_ref[...], preferred_element_type=F32)
    aw = A_WIDTH
    qa_ref[...] = proj[:, 0:aw] * (LOG2E / math.sqrt(A_HEAD_DIM))
    ka_ref[...] = proj[:, aw:2 * aw]
    va_ref[...] = proj[:, 2 * aw:3 * aw]
    c0 = 3 * aw
    q_lat = proj[:, c0:c0 + Q_LORA]
    kv_lat = proj[:, c0 + Q_LORA:c0 + Q_LORA + KV_LORA]
    kpe = proj[:, c0 + Q_LORA + KV_LORA:]
    qn = _rms(q_lat, qg_ref[...]).astype(BF16)
    kvn = _rms(kv_lat, kvg_ref[...]).astype(BF16)
    qm = jnp.dot(qn, wuq_ref[...], preferred_element_type=F32)
    kn = jnp.dot(kvn, wuk_ref[...], preferred_element_type=F32)
    vv = jnp.dot(kvn, wv_ref[...], preferred_element_type=F32)
    cos = cos_ref[...]
    sin = sin_ref[...]
    lane = lax.broadcasted_iota(jnp.int32, cos.shape, 1)
    lt_mid = lane < (B_NOPE + B_ROPE // 2)
    qscale = LOG2E / math.sqrt(B_NOPE + B_ROPE)
    for hd in range(B_HEADS):
        sl = slice(hd * HEAD_PAD, (hd + 1) * HEAD_PAD)
        qb_ref[hd] = (_rope(qm[:, sl], cos, sin, lt_mid) * qscale).astype(BF16)
        kb_ref[hd] = _rope(kn[:, sl] + kpe, cos, sin, lt_mid).astype(BF16)
    for p in range(B_HEADS // 2):
        vb_ref[p] = vv[:, p * LANES:(p + 1) * LANES].astype(BF16)


def _inproj(x2, mod3, norm1_g, w_in_ext, q_norm_g, w_uq_p, kv_norm_g, w_uk_p, w_v, cos_t, sin_t, b, s):
    n, d = x2.shape
    tm = TM_INPROJ
    tpb = s // tm
    row = lambda i: (i, 0)
    const = lambda i: (0, 0)
    hm = lambda i: (i // tpb, 0, i % tpb, 0)
    return pl.pallas_call(
        _inproj_kernel,
        grid=(n // tm,),
        in_specs=[pl.BlockSpec((tm, d), row),
                  pl.BlockSpec((None, 1, d), lambda i: ((i // tpb) * 6 + 1, 0, 0)),
                  pl.BlockSpec((None, 1, d), lambda i: ((i // tpb) * 6 + 0, 0, 0)),
                  pl.BlockSpec((1, d), const),
                  pl.BlockSpec(w_in_ext.shape, const),
                  pl.BlockSpec((1, Q_LORA), const),
                  pl.BlockSpec(w_uq_p.shape, const),
                  pl.BlockSpec((1, KV_LORA), const),
                  pl.BlockSpec(w_uk_p.shape, const),
                  pl.BlockSpec(w_v.shape, const),
                  pl.BlockSpec((tm, HEAD_PAD), lambda i: (i % tpb, 0)),
                  pl.BlockSpec((tm, HEAD_PAD), lambda i: (i % tpb, 0))],
        out_specs=[pl.BlockSpec((tm, A_WIDTH), row),
                   pl.BlockSpec((tm, A_WIDTH), row),
                   pl.BlockSpec((tm, A_WIDTH), row),
                   pl.BlockSpec((None, B_HEADS, tm, HEAD_PAD), hm),
                   pl.BlockSpec((None, B_HEADS, tm, HEAD_PAD), hm),
                   pl.BlockSpec((None, B_HEADS // 2, tm, LANES), hm)],
        out_shape=[jax.ShapeDtypeStruct((n, A_WIDTH), F32),
                   jax.ShapeDtypeStruct((n, A_WIDTH), F32),
                   jax.ShapeDtypeStruct((n, A_WIDTH), F32),
                   jax.ShapeDtypeStruct((b, B_HEADS, s, HEAD_PAD), BF16),
                   jax.ShapeDtypeStruct((b, B_HEADS, s, HEAD_PAD), BF16),
                   jax.ShapeDtypeStruct((b, B_HEADS // 2, s, LANES), BF16)],
        compiler_params=_cparams(("parallel",)),
        name="inproj",
    )(x2, mod3, mod3, norm1_g, w_in_ext, q_norm_g, w_uq_p, kv_norm_g, w_uk_p, w_v, cos_t, sin_t)


def _dil_block(q_ref, k_ref, v_ref, bias_ref, o_scr, l_scr, pi, dil, nblk, job):
    sub_len = nblk * DIL_QB
    r = job // nblk
    bi = job % nblk
    q0 = bi * DIL_QB
    ws = jnp.clip(q0 - A_RADIUS, 0, sub_len - DIL_KW)
    var = jnp.where(bi == 0, 0, jnp.where(bi == nblk - 1, 2, 1))
    if dil == 1:
        qsl = pl.ds(pl.multiple_of(q0, DIL_QB), DIL_QB)
        ksl = pl.ds(pl.multiple_of(ws, A_RADIUS), DIL_KW)
    else:
        qsl = pl.ds(r + dil * q0, DIL_QB, stride=dil)
        ksl = pl.ds(r + dil * ws, DIL_KW, stride=dil)
    q = q_ref[qsl, :]
    kw = k_ref[ksl, :].astype(BF16)
    vw = v_ref[ksl, :].astype(BF16)
    lo = lax.broadcasted_iota(jnp.int32, q.shape, 1) < A_HEAD_DIM
    outs, lses = [], []
    for hh in range(2):
        qm = jnp.where(lo if hh == 0 else jnp.logical_not(lo), q, 0.0).astype(BF16)
        sc = lax.dot_general(qm, kw, (((1,), (1,)), ((), ())), preferred_element_type=F32)
        sc = sc + bias_ref[pi * 3 + var, hh]
        m = jnp.max(sc, axis=-1, keepdims=True)
        p = jnp.exp2(sc - m)
        l = jnp.sum(p, axis=-1, keepdims=True)
        o = jnp.dot(p.astype(BF16), vw, preferred_element_type=F32)
        outs.append(o / l)
        lses.append(m + jnp.log2(l))
    o_scr[pi, qsl, :] = jnp.where(lo, outs[0], outs[1])
    l_scr[pi, qsl, :] = jnp.where(lo, lses[0], lses[1])


def _dilated_kernel(q_ref, k_ref, v_ref, bias_ref, out_ref, o_scr, l_scr):
    s = q_ref.shape[0]
    njobs = s // DIL_QB
    for pi, (_, dil) in enumerate(A_PATTERNS):
        blk = functools.partial(_dil_block, q_ref, k_ref, v_ref, bias_ref, o_scr, l_scr, pi, dil,
                                s // dil // DIL_QB)

        def group(g, c, blk=blk):
            for u in range(DIL_UNROLL):
                blk(g * DIL_UNROLL + u)
            return c

        lax.fori_loop(0, njobs // DIL_UNROLL, group, 0)

    chunk = 512

    def comb(i, c):
        rows = pl.ds(pl.multiple_of(i * chunk, chunk), chunk)
        l0, l1, l2 = l_scr[0, rows, :], l_scr[1, rows, :], l_scr[2, rows, :]
        mx = jnp.maximum(jnp.maximum(l0, l1), l2)
        e0, e1, e2 = jnp.exp2(l0 - mx), jnp.exp2(l1 - mx), jnp.exp2(l2 - mx)
        num = e0 * o_scr[0, rows, :] + e1 * o_scr[1, rows, :] + e2 * o_scr[2, rows, :]
        out_ref[rows, :] = num / (e0 + e1 + e2)
        return c

    lax.fori_loop(0, s // chunk, comb, 0)


def _dilated(qa, ka, va, bias):
    b, s, _ = qa.shape
    npair = A_WIDTH // LANES
    assert (s // DIL_QB) % DIL_UNROLL == 0
    for _, dil in A_PATTERNS:
        assert (s // dil) % DIL_QB == 0 and s // dil >= DIL_KW
    blk = pl.BlockSpec((None, s, LANES), lambda bi, p: (bi, 0, p))
    return pl.pallas_call(
        _dilated_kernel,
        grid=(b, npair),
        in_specs=[blk, blk, blk,
                  pl.BlockSpec((bias.shape[0], 2, DIL_QB, DIL_KW), lambda bi, p: (0, p, 0, 0))],
        out_specs=blk,
        out_shape=jax.ShapeDtypeStruct((b, s, A_WIDTH), F32),
        scratch_shapes=[pltpu.VMEM((len(A_PATTERNS), s, LANES), F32),
                        pltpu.VMEM((len(A_PATTERNS), s, LANES), F32)],
        compiler_params=_cparams(("parallel", "parallel")),
        name="dilated",
    )(qa, ka, va, bias)


def _mla_kernel(q_ref, k_ref, v_ref, o_ref):
    tq = q_ref.shape[1]
    nkc = k_ref.shape[1] // MLA_KC
    outs = []
    for hh in range(2):
        q = q_ref[hh]
        m = jnp.full((tq, 1), -jnp.inf, F32)
        l = jnp.zeros((tq, 1), F32)
        acc = jnp.zeros((tq, LANES), F32)
        for c in range(nkc):
            keys = slice(c * MLA_KC, (c + 1) * MLA_KC)
            sc = lax.dot_general(q, k_ref[hh, keys, :], (((1,), (1,)), ((), ())), preferred_element_type=F32)
            m_new = jnp.maximum(m, jnp.max(sc, axis=-1, keepdims=True))
            alpha = jnp.exp2(m - m_new)
            p = jnp.exp2(sc - m_new)
            l = alpha * l + jnp.sum(p, axis=-1, keepdims=True)
            acc = alpha * acc + jnp.dot(p.astype(BF16), v_ref[keys, :], preferred_element_type=F32)
            m = m_new
        outs.append(acc / l)
    lo = lax.broadcasted_iota(jnp.int32, outs[0].shape, 1) < B_VDIM
    o_ref[...] = jnp.where(lo, outs[0], outs[1])


def _mla(qb, kb, vb):
    b, h, s, _ = qb.shape
    npair = h // 2
    tq = MLA_TQ
    return pl.pallas_call(
        _mla_kernel,
        grid=(b, npair, s // tq),
        in_specs=[pl.BlockSpec((None, 2, tq, HEAD_PAD), lambda bi, p, qi: (bi, p, qi, 0)),
                  pl.BlockSpec((None, 2, s, HEAD_PAD), lambda bi, p, qi: (bi, p, 0, 0)),
                  pl.BlockSpec((None, None, s, LANES), lambda bi, p, qi: (bi, p, 0, 0))],
        out_specs=pl.BlockSpec((None, tq, LANES), lambda bi, p, qi: (bi, qi, p)),
        out_shape=jax.ShapeDtypeStruct((b, s, B_WIDTH), F32),
        compiler_params=_cparams(("parallel", "parallel", "arbitrary")),
        name="mla",
    )(qb, kb, vb)


def _outproj_kernel(x_ref, a_ref, b_ref, ag_ref, bg_ref, wo_ref, g1_ref, n2_ref, sc_ref, sh_ref, g2_ref,
                    wrt_ref, ws1_ref, ws3_ref, ws2_ref, x1s_ref, h2p_ref, scores_ref):
    an = _rms(a_ref[...], ag_ref[...])
    bn = _rms(b_ref[...], bg_ref[...])
    mix = jnp.concatenate([an, bn], axis=-1).astype(BF16)
    x1 = x_ref[...] + g1_ref[...] * jnp.dot(mix, wo_ref[...], preferred_element_type=F32)
    h2 = _rms(x1, n2_ref[...]) * (1.0 + sc_ref[...]) + sh_ref[...]
    packed = _pack_rows(h2)
    for cg in range(h2p_ref.shape[0]):
        h2p_ref[cg] = packed[:, cg * LANES:(cg + 1) * LANES]
    logits = lax.dot_general(wrt_ref[...], h2, (((1,), (1,)), ((), ())), precision=HIGHEST,
                             preferred_element_type=F32)
    scores_ref[...] = jax.nn.sigmoid(logits)
    h2b = h2.astype(BF16)
    hid = _silu(jnp.dot(h2b, ws1_ref[...], preferred_element_type=F32)) * jnp.dot(
        h2b, ws3_ref[...], preferred_element_type=F32)
    shared = jnp.dot(hid.astype(BF16), ws2_ref[...], preferred_element_type=F32)
    x1s_ref[...] = x1 + g2_ref[...] * shared


def _outproj(x2, a_out, b_out, a_out_g, b_out_g, w_o, mod3, norm2_g, w_router_t, ws1, ws3, ws2, s):
    n, d = x2.shape
    tm = TM_OUTPROJ
    tpb = s // tm
    row = lambda i: (i, 0)
    const = lambda i: (0, 0)
    modspec = lambda j: pl.BlockSpec((None, 1, d), lambda i: ((i // tpb) * 6 + j, 0, 0))
    return pl.pallas_call(
        _outproj_kernel,
        grid=(n // tm,),
        in_specs=[pl.BlockSpec((tm, d), row),
                  pl.BlockSpec((tm, A_WIDTH), row),
                  pl.BlockSpec((tm, B_WIDTH), row),
                  pl.BlockSpec((1, A_WIDTH), const),
                  pl.BlockSpec((1, B_WIDTH), const),
                  pl.BlockSpec(w_o.shape, const),
                  modspec(2),
                  pl.BlockSpec((1, d), const),
                  modspec(4), modspec(3), modspec(5),
                  pl.BlockSpec(w_router_t.shape, const),
                  pl.BlockSpec(ws1.shape, const),
                  pl.BlockSpec(ws3.shape, const),
                  pl.BlockSpec(ws2.shape, const)],
        out_specs=[pl.BlockSpec((tm, d), row),
                   pl.BlockSpec((d // 2 // LANES, tm, LANES), lambda i: (0, i, 0)),
                   pl.BlockSpec((N_EXPERTS, tm), lambda i: (0, i))],
        out_shape=[jax.ShapeDtypeStruct((n, d), F32),
                   jax.ShapeDtypeStruct((d // 2 // LANES, n, LANES), U32),
                   jax.ShapeDtypeStruct((N_EXPERTS, n), F32)],
        compiler_params=_cparams(("parallel",)),
        name="outproj",
    )(x2, a_out, b_out, a_out_g, b_out_g, w_o, mod3, norm2_g, mod3, mod3, mod3, w_router_t, ws1, ws3, ws2)


def _slab_order(v):
    return v.reshape((N_GROUPS, GROUP_SIZE) + v.shape[1:]).swapaxes(0, 1).reshape(v.shape)


def _expert_order(v):
    return v.reshape((GROUP_SIZE, N_GROUPS) + v.shape[1:]).swapaxes(0, 1).reshape(v.shape)


def _sublane_all(x, op):
    for sh in (4, 2, 1):
        x = op(x, pltpu.roll(x, sh, 0))
    return x


def _route_kernel(st_ref, bias_ref, eidx_ref, gate_ref, cnt_ref):
    nsl = GROUP_SIZE
    t = st_ref.shape[1]
    sub = lax.broadcasted_iota(I32, (SUBLANES, t), 0)
    ninf = -jnp.inf
    big = jnp.int32(1 << 30)
    sc = [st_ref[j * SUBLANES:(j + 1) * SUBLANES, :] for j in range(nsl)]
    sel = [sc[j] + bias_ref[j * SUBLANES:(j + 1) * SUBLANES, :] for j in range(nsl)]
    eid = [sub * GROUP_SIZE + j for j in range(nsl)]

    m1 = sel[0]
    m2 = jnp.full_like(m1, ninf)
    for j in range(1, nsl):
        m2 = jnp.maximum(m2, jnp.minimum(m1, sel[j]))
        m1 = jnp.maximum(m1, sel[j])
    gs = m1 + m2

    rank = jnp.zeros((SUBLANES, t), I32)
    for sh in range(1, N_GROUPS):
        other = pltpu.roll(gs, sh, 0)
        ahead = (other > gs) | ((other == gs) & (sub >= sh))
        rank = rank + ahead.astype(I32)
    gmask = rank < TOPK_GROUPS

    msel = [jnp.where(gmask, sel[j], ninf) for j in range(nsl)]
    hits = [jnp.zeros((SUBLANES, t), I32) for _ in range(nsl)]
    eidx = jnp.zeros((TOP_K, t), I32)
    gates = jnp.zeros((TOP_K, t), F32)
    for k in range(TOP_K):
        mx = msel[0]
        for j in range(1, nsl):
            mx = jnp.maximum(mx, msel[j])
        mx = _sublane_all(mx, jnp.maximum)
        cand = jnp.where(msel[0] == mx, eid[0], big)
        for j in range(1, nsl):
            cand = jnp.minimum(cand, jnp.where(msel[j] == mx, eid[j], big))
        idx = _sublane_all(cand, jnp.minimum)
        gk = jnp.zeros((SUBLANES, t), F32)
        for j in range(nsl):
            hit = eid[j] == idx
            gk = gk + jnp.where(hit, sc[j], 0.0)
            msel[j] = jnp.where(hit, ninf, msel[j])
            hits[j] = hits[j] + hit.astype(I32)
        gk = _sublane_all(gk, jnp.add)
        eidx = jnp.where(sub == k, idx, eidx)
        gates = jnp.where(sub == k, gk, gates)
    gsum = _sublane_all(gates, jnp.add)
    eidx_ref[...] = eidx
    gate_ref[...] = gates / gsum * ROUTED_SCALE

    @pl.when(pl.program_id(0) == 0)
    def _():
        cnt_ref[...] = jnp.zeros_like(cnt_ref)

    for j in range(nsl):
        cnt_ref[j * SUBLANES:(j + 1) * SUBLANES, :] += jnp.sum(hits[j].astype(F32), axis=1,
                                                               keepdims=True).astype(I32)


def _route(scores_t, e_bias_slab):
    e, n = scores_t.shape
    t = ROUTE_T
    return pl.pallas_call(
        _route_kernel,
        grid=(n // t,),
        in_specs=[pl.BlockSpec((e, t), lambda i: (0, i)),
                  pl.BlockSpec((e, 1), lambda i: (0, 0))],
        out_specs=[pl.BlockSpec((TOP_K, t), lambda i: (0, i)),
                   pl.BlockSpec((TOP_K, t), lambda i: (0, i)),
                   pl.BlockSpec((e, 1), lambda i: (0, 0))],
        out_shape=[jax.ShapeDtypeStruct((TOP_K, n), I32),
                   jax.ShapeDtypeStruct((TOP_K, n), F32),
                   jax.ShapeDtypeStruct((e, 1), I32)],
        compiler_params=_cparams(("arbitrary",)),
        name="route",
    )(scores_t, e_bias_slab)


def _dest_kernel(eidx_ref, pstart_ref, dest_ref, carry_ref):
    @pl.when(pl.program_id(0) == 0)
    def _():
        carry_ref[...] = jnp.zeros_like(carry_ref)

    nsl = GROUP_SIZE
    t = eidx_ref.shape[1]
    sub = lax.broadcasted_iota(I32, (SUBLANES, t), 0)
    eid = [sub * GROUP_SIZE + j for j in range(nsl)]
    ek = [eidx_ref[k:k + 1, :] for k in range(TOP_K)]
    slabs = []
    for j in range(nsl):
        oh = jnp.zeros((SUBLANES, t), F32)
        for k in range(TOP_K):
            oh = oh + (eid[j] == ek[k]).astype(F32)
        slabs.append(oh)
    onehot = jnp.concatenate(slabs, axis=0)
    row = lax.broadcasted_iota(I32, (t, t), 0)
    col = lax.broadcasted_iota(I32, (t, t), 1)
    upper = (row < col).astype(BF16)
    before = jnp.dot(onehot.astype(BF16), upper, preferred_element_type=F32)
    base = before + carry_ref[...] + pstart_ref[...]
    dest = jnp.zeros((TOP_K, t), I32)
    for k in range(TOP_K):
        acc = jnp.zeros((SUBLANES, t), F32)
        for j in range(nsl):
            acc = acc + jnp.where(eid[j] == ek[k], base[j * SUBLANES:(j + 1) * SUBLANES, :], 0.0)
        dk = _sublane_all(acc, jnp.add).astype(I32)
        dest = jnp.where(sub == k, dk, dest)
    dest_ref[...] = dest
    carry_ref[...] += jnp.sum(onehot, axis=1, keepdims=True)


def _dest(eidx_t, pstart_slab):
    n = eidx_t.shape[1]
    t = ROUTE_T
    return pl.pallas_call(
        _dest_kernel,
        grid=(n // t,),
        in_specs=[pl.BlockSpec((TOP_K, t), lambda i: (0, i)),
                  pl.BlockSpec((N_EXPERTS, 1), lambda i: (0, 0))],
        out_specs=pl.BlockSpec((TOP_K, t), lambda i: (0, i)),
        out_shape=jax.ShapeDtypeStruct((TOP_K, n), I32),
        scratch_shapes=[pltpu.VMEM((N_EXPERTS, 1), F32)],
        compiler_params=_cparams(("arbitrary",)),
        name="dest",
    )(eidx_t, pstart_slab)


def _sc_scatter(x, idx, rows_out, nplane, nslot):
    del nplane
    num = idx.shape[0]
    mesh = plsc.VectorSubcoreMesh(core_axis_name="core", subcore_axis_name="subcore")

    @pl.kernel(out_type=jax.ShapeDtypeStruct((rows_out, x.shape[1]), x.dtype), mesh=mesh, scratch_types=[])
    def scatter(x_hbm, i_hbm, o_hbm):
        def body(x_vmem, i_vmem):
            pltpu.sync_copy(x_vmem, o_hbm.at[i_vmem.at[0]])

        pltpu.emit_pipeline(
            body,
            grid=(num // SC_WINDOW,),
            in_specs=[pl.BlockSpec((SC_WINDOW, x.shape[1]), index_map=lambda i: (i // nslot, 0)),
                      pl.BlockSpec((1, SC_WINDOW), index_map=lambda i: (0, i))],
            out_specs=[],
            core_axis_name=("core", "subcore"),
            dimension_semantics=(pltpu.PARALLEL,),
        )(x_hbm, i_hbm)

    return scatter(x, idx.reshape(1, num))


def _expert_kernel(cstart_ref, nused_ref, valid_ref, w1_ref, w3_ref, w2_ref, xg_ref, y_ref,
                   w1b, w3b, w2b, xbuf, ybuf, xsem, ysem):
    e = pl.program_id(0)
    c0 = cstart_ref[e]
    c1 = cstart_ref[e + 1]
    nused = nused_ref[0]
    nchunks = xg_ref.shape[1] // MOE_BLK

    def rows(g):
        start = g * MOE_BLK
        return pl.ds(start if isinstance(g, int) else pl.multiple_of(start, MOE_BLK), MOE_BLK)

    nplane = xg_ref.shape[0]

    def xcopies(g):
        slot = g % EXP_XBUF
        return [pltpu.make_async_copy(xg_ref.at[pl.ds(0, nplane), rows(g)], xbuf.at[slot], xsem.at[slot])]

    def ycopies(g, slot):
        return [pltpu.make_async_copy(ybuf.at[slot], y_ref.at[pl.ds(0, nplane), rows(g)], ysem.at[slot])]

    @pl.when(e == 0)
    def _():
        for j in range(EXP_AHEAD):
            @pl.when(j < nused)
            def _():
                for cp in xcopies(j):
                    cp.start()

    @pl.when(c1 > c0)
    def _():
        w1b[...] = w1_ref[...].astype(BF16)
        w3b[...] = w3_ref[...].astype(BF16)
        w2b[...] = w2_ref[...].astype(BF16)

        def ffn(g):
            row_id = lax.broadcasted_iota(I32, (MOE_BLK, LANES), 0)
            keep = row_id < valid_ref[g]
            halves = [_unpack_rows(jnp.where(keep, xbuf[g % EXP_XBUF, c], jnp.uint32(0))) for c in range(nplane)]
            xb = jnp.concatenate([h[0] for h in halves] + [h[1] for h in halves], axis=1).astype(BF16)
            hid = _silu(jnp.dot(xb, w1b[...], preferred_element_type=F32)) * jnp.dot(
                xb, w3b[...], preferred_element_type=F32)
            return _pack_rows(jnp.dot(hid.astype(BF16), w2b[...], preferred_element_type=F32))

        def chunk(g, carry):
            @pl.when(g + EXP_AHEAD < nused)
            def _():
                for cp in xcopies(g + EXP_AHEAD):
                    cp.start()

            for cp in xcopies(g):
                cp.wait()
            yp = ffn(g)
            yslot = g % 2

            @pl.when(g >= 2)
            def _():
                for cp in ycopies(g - 2, yslot):
                    cp.wait()

            for c in range(nplane):
                ybuf[yslot, c] = yp[:, c * LANES:(c + 1) * LANES]
            for cp in ycopies(g, yslot):
                cp.start()
            return carry

        lax.fori_loop(c0, c1, chunk, 0)

    @pl.when(e == pl.num_programs(0) - 1)
    def _():
        for back in (2, 1):
            g = nused - back

            @pl.when(g >= 0)
            def _():
                for cp in ycopies(g, g % 2):
                    cp.wait()

        ybuf[0] = jnp.zeros(ybuf.shape[1:], ybuf.dtype)

        def zstart(g, c):
            for cp in ycopies(g, 0):
                cp.start()
            return c

        def zwait(g, c):
            for cp in ycopies(g, 0):
                cp.wait()
            return c

        lax.fori_loop(nused, nchunks, zstart, 0)
        lax.fori_loop(nused, nchunks, zwait, 0)


def _experts(cstart, nused, valid, xg, w1, w3, w2):
    nplane, rows, _ = xg.shape
    dh = nplane * LANES
    ne, d, f = w1.shape
    grid_spec = pltpu.PrefetchScalarGridSpec(
        num_scalar_prefetch=3,
        grid=(ne,),
        in_specs=[pl.BlockSpec((None, d, f), lambda e, cs, nu, va: (e, 0, 0)),
                  pl.BlockSpec((None, d, f), lambda e, cs, nu, va: (e, 0, 0)),
                  pl.BlockSpec((None, f, d), lambda e, cs, nu, va: (e, 0, 0)),
                  pl.BlockSpec(memory_space=pl.ANY)],
        out_specs=pl.BlockSpec(memory_space=pl.ANY),
        scratch_shapes=[pltpu.VMEM((d, f), BF16), pltpu.VMEM((d, f), BF16), pltpu.VMEM((f, d), BF16),
                        pltpu.VMEM((EXP_XBUF, nplane, MOE_BLK, LANES), U32),
                        pltpu.VMEM((2, nplane, MOE_BLK, LANES), U32),
                        pltpu.SemaphoreType.DMA((EXP_XBUF,)), pltpu.SemaphoreType.DMA((2,))],
    )
    return pl.pallas_call(
        _expert_kernel,
        grid_spec=grid_spec,
        out_shape=jax.ShapeDtypeStruct((dh // LANES, rows, LANES), U32),
        compiler_params=_cparams(("arbitrary",)),
        name="experts",
    )(cstart, nused, valid, w1, w3, w2, xg)


def _sc_gather(x, idx):
    num = idx.shape[0]
    mesh = plsc.VectorSubcoreMesh(core_axis_name="core", subcore_axis_name="subcore")

    @pl.kernel(out_type=jax.ShapeDtypeStruct((num, x.shape[1]), x.dtype), mesh=mesh)
    def gather(x_hbm, i_hbm, o_hbm):
        def body(i_vmem, o_vmem):
            pltpu.sync_copy(x_hbm.at[i_vmem.at[0]], o_vmem)

        pltpu.emit_pipeline(
            body,
            grid=(num // SC_WINDOW,),
            in_specs=[pl.BlockSpec((1, SC_WINDOW), index_map=lambda i: (0, i))],
            out_specs=[pl.BlockSpec((SC_WINDOW, x.shape[1]), index_map=lambda i: (i, 0))],
            core_axis_name=("core", "subcore"),
            dimension_semantics=(pltpu.PARALLEL,),
        )(i_hbm, o_hbm)

    return gather(x, idx.reshape(1, num))


def _combine_kernel(yg_ref, gate_ref, x1s_ref, g2_ref, fg_ref, *rest):
    out_ref = rest[-1]
    gates = gate_ref[...]
    nch = yg_ref.shape[0]
    r_lo = [None] * nch
    r_hi = [None] * nch
    for k in range(TOP_K):
        gk = gates[:, k:k + 1]
        for c in range(nch):
            lo, hi = _unpack_rows(yg_ref[c, k])
            r_lo[c] = gk * lo if k == 0 else r_lo[c] + gk * lo
            r_hi[c] = gk * hi if k == 0 else r_hi[c] + gk * hi
    routed = jnp.concatenate(r_lo + r_hi, axis=1)
    x2 = x1s_ref[...] + g2_ref[...] * routed
    out_ref[...] = _rms(x2, fg_ref[...])


def _combine(yg, gates, x1s, mod3, final_g, s, part, prev_out):
    n, d = x1s.shape
    t = COMB_T
    tpb = s // t
    nch, _, npart, _ = yg.shape
    off = part * (npart // t)
    in_specs = [pl.BlockSpec((nch, TOP_K, t, LANES), lambda i: (0, 0, i, 0)),
                pl.BlockSpec((t, TOP_K), lambda i: (i + off, 0)),
                pl.BlockSpec((t, d), lambda i: (i + off, 0)),
                pl.BlockSpec((None, 1, d), lambda i: (((i + off) // tpb) * 6 + 5, 0, 0)),
                pl.BlockSpec((1, d), lambda i: (0, 0))]
    args = [yg, gates, x1s, mod3, final_g]
    aliases = {}
    if prev_out is not None:
        in_specs.append(pl.BlockSpec(memory_space=pl.ANY))
        args.append(prev_out)
        aliases = {len(args) - 1: 0}
    return pl.pallas_call(
        _combine_kernel,
        grid=(npart // t,),
        in_specs=in_specs,
        out_specs=pl.BlockSpec((t, d), lambda i: (i + off, 0)),
        out_shape=jax.ShapeDtypeStruct((n, d), F32),
        input_output_aliases=aliases,
        compiler_params=_cparams(("parallel",)),
        name="combine",
    )(*args)


def _place_heads(w, per_head, keep):
    r = w.shape[0]
    w = w.reshape(r, B_HEADS, per_head)[:, :, :keep]
    return jnp.pad(w, ((0, 0), (0, 0), (0, HEAD_PAD - keep))).reshape(r, B_HEADS * HEAD_PAD)


def _rope_tables(s):
    inv = ROPE_THETA ** (-jnp.arange(0, B_ROPE, 2, dtype=jnp.float32) / B_ROPE)
    ang = jnp.arange(s, dtype=jnp.float32)[:, None] * inv[None, :]
    cos, sin = jnp.cos(ang), jnp.sin(ang)
    ones = jnp.ones((s, B_NOPE), F32)
    zeros = jnp.zeros((s, B_NOPE), F32)
    tail1 = jnp.ones((s, HEAD_PAD - B_NOPE - B_ROPE), F32)
    tail0 = jnp.zeros((s, HEAD_PAD - B_NOPE - B_ROPE), F32)
    return (jnp.concatenate([ones, cos, cos, tail1], axis=1),
            jnp.concatenate([zeros, -sin, sin, tail0], axis=1))


def kernel(x, c, w_ada, b_ada, norm1_g, w_in, q_norm_g, w_uq, kv_norm_g, w_ukv, rel_table, a_out_g, b_out_g,
           w_o, norm2_g, w_router, e_bias, w1, w3, w2, ws1, ws3, ws2, final_g):
    b, s, d = x.shape
    n = b * s
    assert w_ada.shape[0] == 1, "single layer"
    x2 = x.reshape(n, d)

    mod = _ada(c, w_ada[0], b_ada[0])
    mod3 = mod.reshape(b * 6, 1, d)
    bias = _bias_tiles(rel_table)

    wi = w_in[0]
    c_kpe = 3 * A_WIDTH + Q_LORA + KV_LORA
    kpe_cols = jnp.pad(wi[:, c_kpe:], ((0, 0), (B_NOPE, HEAD_PAD - B_NOPE - B_ROPE)))
    w_in_ext = jnp.concatenate([wi[:, :c_kpe], kpe_cols], axis=1).astype(BF16)
    w_uq_p = _place_heads(w_uq[0], B_NOPE + B_ROPE, B_NOPE + B_ROPE).astype(BF16)
    w_uk_p = _place_heads(w_ukv[0], B_NOPE + B_VDIM, B_NOPE).astype(BF16)
    w_v = w_ukv[0].reshape(KV_LORA, B_HEADS, B_NOPE + B_VDIM)[:, :, B_NOPE:].reshape(KV_LORA, B_WIDTH).astype(BF16)
    cos_t, sin_t = _rope_tables(s)

    qa, ka, va, qb, kb, vb = _inproj(x2, mod3, norm1_g, w_in_ext, q_norm_g, w_uq_p, kv_norm_g, w_uk_p, w_v,
                                     cos_t, sin_t, b, s)
    a_out = _dilated(qa.reshape(b, s, A_WIDTH), ka.reshape(b, s, A_WIDTH), va.reshape(b, s, A_WIDTH), bias)
    b_out = _mla(qb, kb, vb)

    w_router_t = _slab_order(w_router[0].T)
    x1s, h2p, scores_t = _outproj(x2, a_out.reshape(n, A_WIDTH), b_out.reshape(n, B_WIDTH), a_out_g, b_out_g,
                                  w_o[0].astype(BF16), mod3, norm2_g, w_router_t,
                                  ws1[0].astype(BF16), ws3[0].astype(BF16), ws2[0].astype(BF16), s)

    eidx_t, gates_t, counts_slab = _route(scores_t, _slab_order(e_bias.reshape(N_EXPERTS, 1)))

    counts = _expert_order(counts_slab)[:, 0]
    padded = (counts + MOE_BLK - 1) // MOE_BLK * MOE_BLK
    pends = jnp.cumsum(padded)
    pstart = pends - padded
    nk = n * TOP_K
    nblk = -(-(nk + N_EXPERTS * (MOE_BLK - 1)) // MOE_BLK)
    rows_total = nblk * MOE_BLK
    blk_row = jnp.arange(nblk, dtype=I32) * MOE_BLK
    blk_e = jnp.minimum(jnp.sum((pends[None, :] <= blk_row[:, None]).astype(I32), axis=1), N_EXPERTS - 1)
    seg_end = (pstart + counts)[blk_e]
    valid = jnp.clip(seg_end - blk_row, 0, MOE_BLK).astype(I32)
    nused = (pends[-1] // MOE_BLK).astype(I32).reshape(1)
    cstart = jnp.concatenate([jnp.zeros((1,), I32), (pends // MOE_BLK).astype(I32)])

    dest_t = _dest(eidx_t, _slab_order(pstart.astype(F32).reshape(N_EXPERTS, 1)))
    nplane = h2p.shape[0]
    gidx = (dest_t.reshape(1, nk) + (jnp.arange(nplane, dtype=I32) * rows_total)[:, None]).reshape(nplane * nk)
    nwin = n // SC_WINDOW
    sidx = dest_t.reshape(TOP_K, nwin, SC_WINDOW).transpose(1, 0, 2)[None] + (
        jnp.arange(nplane, dtype=I32) * rows_total)[:, None, None, None]
    xg = _sc_scatter(h2p.reshape(nplane * n, LANES), sidx.reshape(nplane * nk), nplane * rows_total, nplane, TOP_K)
    y = _experts(cstart, nused, valid, xg.reshape(nplane, rows_total, LANES), w1[0], w3[0], w2[0])
    y_flat = y.reshape(nplane * rows_total, LANES)
    gidx3 = gidx.reshape(nplane, TOP_K, n)
    gates = gates_t.T
    npart = n // COMB_SPLIT
    out = None
    for part in range(COMB_SPLIT):
        pidx = gidx3[:, :, part * npart:(part + 1) * npart].reshape(nplane * TOP_K * npart)
        yg = _sc_gather(y_flat, pidx).reshape(nplane, TOP_K, npart, LANES)
        out = _combine(yg, gates, x1s, mod3, final_g.reshape(1, d), s, part, out)
    return out.reshape(b, s, d)
```

```python
import functools
import math

import jax
import jax.numpy as jnp
from jax import lax
from jax.experimental import pallas as pl
from jax.experimental.pallas import tpu as pltpu
from jax.experimental.pallas import tpu_sc as plsc

F32 = jnp.float32
BF16 = jnp.bfloat16
U32 = jnp.uint32
I32 = jnp.int32
HIGHEST = lax.Precision.HIGHEST

D_MODEL = 1024
A_HEADS = 8
A_HEAD_DIM = 64
A_WIDTH = A_HEADS * A_HEAD_DIM
A_PATTERNS = ((128, 1), (512, 4), (2048, 16))
A_RADIUS = 64
REL_BUCKETS = 32
REL_MAX_DIST = 1024
B_HEADS = 8
B_NOPE = 64
B_ROPE = 32
B_VDIM = 64
B_WIDTH = B_HEADS * B_VDIM
Q_LORA = 384
KV_LORA = 256
ROPE_THETA = 10000.0
N_EXPERTS = 256
TOP_K = 8
N_GROUPS = 8
GROUP_SIZE = N_EXPERTS // N_GROUPS
TOPK_GROUPS = 4
EXPERT_FF = 256
SHARED_FF = 256
ROUTED_SCALE = 2.5
EPS = 1e-6
NEG_INF = -1e30
LOG2E = math.log2(math.e)

LANES = 128
SUBLANES = 8
HEAD_PAD = 128
IN_COLS_EXT = 3 * A_WIDTH + Q_LORA + KV_LORA + HEAD_PAD

TM_INPROJ = 512
TM_OUTPROJ = 1024
DIL_QB = 128
DIL_KW = DIL_QB + 2 * A_RADIUS
DIL_UNROLL = 8
MLA_TQ = 1024
MLA_KC = 2048
ROUTE_T = 512
MOE_BLK = 256
EXP_AHEAD = 6
EXP_XBUF = EXP_AHEAD + 1
SC_WINDOW = 128
COMB_T = 256
COMB_SPLIT = 4
VMEM_LIMIT = 56 * 1024 * 1024


def _cparams(sem):
    return pltpu.CompilerParams(dimension_semantics=sem, vmem_limit_bytes=VMEM_LIMIT)


def _rms(x, g):
    return x * lax.rsqrt(jnp.mean(x * x, axis=-1, keepdims=True) + EPS) * g


def _silu(x):
    return x * jax.nn.sigmoid(x)


def _pack_rows(x):
    half = x.shape[1] // 2
    bits = lax.bitcast_convert_type(x.astype(BF16).astype(F32), U32)
    return (bits[:, :half] >> 16) | bits[:, half:]


def _unpack_rows(w):
    lo = lax.bitcast_convert_type(w << 16, F32)
    hi = lax.bitcast_convert_type(w & jnp.uint32(0xFFFF0000), F32)
    return lo, hi


def _ada_kernel(c_ref, w_ref, b_ref, o_ref):
    o_ref[...] = jnp.dot(_silu(c_ref[...]), w_ref[...], precision=HIGHEST,
                         preferred_element_type=F32) + b_ref[...]


def _ada(c, w_ada, b_ada):
    b, d = c.shape
    n6 = w_ada.shape[1] // d
    return pl.pallas_call(
        _ada_kernel,
        grid=(n6,),
        in_specs=[pl.BlockSpec((b, d), lambda j: (0, 0)),
                  pl.BlockSpec((d, d), lambda j: (0, j)),
                  pl.BlockSpec((1, d), lambda j: (0, j))],
        out_specs=pl.BlockSpec((b, d), lambda j: (0, j)),
        out_shape=jax.ShapeDtypeStruct((b, n6 * d), F32),
        compiler_params=_cparams(("parallel",)),
        name="ada",
    )(c, w_ada, b_ada.reshape(1, -1))


def _t5_bucket(rel):
    half = REL_BUCKETS // 2
    max_exact = half // 2
    ret = jnp.where(rel > 0, half, 0)
    n = jnp.abs(rel)
    nf = jnp.maximum(n, 1).astype(jnp.float32)
    large = max_exact + (jnp.log(nf / max_exact) / math.log(REL_MAX_DIST / max_exact)
                         * (half - max_exact)).astype(jnp.int32)
    large = jnp.minimum(large, half - 1)
    return ret + jnp.where(n < max_exact, n, large)


DIL_SHIFTS = (A_RADIUS, 0, -A_RADIUS)


def _bucket_tiles():
    qi = jnp.arange(DIL_QB, dtype=jnp.int32)[:, None]
    ki = jnp.arange(DIL_KW, dtype=jnp.int32)[None, :]
    tiles = []
    for _, dilation in A_PATTERNS:
        for shift in DIL_SHIFTS:
            off = ki + shift - A_RADIUS - qi
            bkt = _t5_bucket(off * dilation)
            tiles.append(jnp.where(jnp.abs(off) <= A_RADIUS, bkt, -1))
    return jnp.stack(tiles, axis=0)


def _bias_kernel(tab_ref, bkt_ref, o_ref):
    bkt = bkt_ref[0]
    for h in range(A_HEADS):
        acc = jnp.full(bkt.shape, NEG_INF, F32)
        for b in range(REL_BUCKETS):
            acc = jnp.where(bkt == b, tab_ref[b, h] * LOG2E, acc)
        o_ref[0, h] = acc


def _bias_tiles(rel_table):
    bkt = _bucket_tiles()
    nt = bkt.shape[0]
    return pl.pallas_call(
        _bias_kernel,
        grid=(nt,),
        in_specs=[pl.BlockSpec(memory_space=pltpu.SMEM),
                  pl.BlockSpec((1, DIL_QB, DIL_KW), lambda t: (t, 0, 0))],
        out_specs=pl.BlockSpec((1, A_HEADS, DIL_QB, DIL_KW), lambda t: (t, 0, 0, 0)),
        out_shape=jax.ShapeDtypeStruct((nt, A_HEADS, DIL_QB, DIL_KW), F32),
        compiler_params=_cparams(("parallel",)),
        name="bias",
    )(rel_table, bkt)


def _rope(x, cos, sin, lane_lt_mid):
    half = B_ROPE // 2
    rot = jnp.where(lane_lt_mid, pltpu.roll(x, HEAD_PAD - half, 1), pltpu.roll(x, half, 1))
    return x * cos + rot * sin


def _inproj_kernel(x_ref, sc_ref, sh_ref, g1_ref, win_ref, qg_ref, wuq_ref, kvg_ref, wuk_ref, wv_ref,
                   cos_ref, sin_ref, qa_ref, ka_ref, va_ref, qb_ref, kb_ref, vb_ref):
    x = x_ref[...]
    h = _rms(x, g1_ref[...]) * (1.0 + sc_ref[...]) + sh_ref[...]
    proj = jnp.dot(h.astype(BF16), win_ref[...], preferred_element_type=F32)
    aw = A_WIDTH
    qa_ref[...] = proj[:, 0:aw] * (LOG2E / math.sqrt(A_HEAD_DIM))
    ka_ref[...] = proj[:, aw:2 * aw]
    va_ref[...] = proj[:, 2 * aw:3 * aw]
    c0 = 3 * aw
    q_lat = proj[:, c0:c0 + Q_LORA]
    kv_lat = proj[:, c0 + Q_LORA:c0 + Q_LORA + KV_LORA]
    kpe = proj[:, c0 + Q_LORA + KV_LORA:]
    qn = _rms(q_lat, qg_ref[...]).astype(BF16)
    kvn = _rms(kv_lat, kvg_ref[...]).astype(BF16)
    qm = jnp.dot(qn, wuq_ref[...], preferred_element_type=F32)
    kn = jnp.dot(kvn, wuk_ref[...], preferred_element_type=F32)
    vv = jnp.dot(kvn, wv_ref[...], preferred_element_type=F32)
    cos = cos_ref[...]
    sin = sin_ref[...]
    lane = lax.broadcasted_iota(jnp.int32, cos.shape, 1)
    lt_mid = lane < (B_NOPE + B_ROPE // 2)
    qscale = LOG2E / math.sqrt(B_NOPE + B_ROPE)
    for hd in range(B_HEADS):
        sl = slice(hd * HEAD_PAD, (hd + 1) * HEAD_PAD)
        qb_ref[hd] = (_rope(qm[:, sl], cos, sin, lt_mid) * qscale).astype(BF16)
        kb_ref[hd] = _rope(kn[:, sl] + kpe, cos, sin, lt_mid).astype(BF16)
    for p in range(B_HEADS // 2):
        vb_ref[p] = vv[:, p * LANES:(p + 1) * LANES].astype(BF16)


def _inproj(x2, mod3, norm1_g, w_in_ext, q_norm_g, w_uq_p, kv_norm_g, w_uk_p, w_v, cos_t, sin_t, b, s):
    n, d = x2.shape
    tm = TM_INPROJ
    tpb = s // tm
    row = lambda i: (i, 0)
    const = lambda i: (0, 0)
    hm = lambda i: (i // tpb, 0, i % tpb, 0)
    return pl.pallas_call(
        _inproj_kernel,
        grid=(n // tm,),
        in_specs=[pl.BlockSpec((tm, d), row),
                  pl.BlockSpec((None, 1, d), lambda i: ((i // tpb) * 6 + 1, 0, 0)),
                  pl.BlockSpec((None, 1, d), lambda i: ((i // tpb) * 6 + 0, 0, 0)),
                  pl.BlockSpec((1, d), const),
                  pl.BlockSpec(w_in_ext.shape, const),
                  pl.BlockSpec((1, Q_LORA), const),
                  pl.BlockSpec(w_uq_p.shape, const),
                  pl.BlockSpec((1, KV_LORA), const),
                  pl.BlockSpec(w_uk_p.shape, const),
                  pl.BlockSpec(w_v.shape, const),
                  pl.BlockSpec((tm, HEAD_PAD), lambda i: (i % tpb, 0)),
                  pl.BlockSpec((tm, HEAD_PAD), lambda i: (i % tpb, 0))],
        out_specs=[pl.BlockSpec((tm, A_WIDTH), row),
                   pl.BlockSpec((tm, A_WIDTH), row),
                   pl.BlockSpec((tm, A_WIDTH), row),
                   pl.BlockSpec((None, B_HEADS, tm, HEAD_PAD), hm),
                   pl.BlockSpec((None, B_HEADS, tm, HEAD_PAD), hm),
                   pl.BlockSpec((None, B_HEADS // 2, tm, LANES), hm)],
        out_shape=[jax.ShapeDtypeStruct((n, A_WIDTH), F32),
                   jax.ShapeDtypeStruct((n, A_WIDTH), F32),
                   jax.ShapeDtypeStruct((n, A_WIDTH), F32),
                   jax.ShapeDtypeStruct((b, B_HEADS, s, HEAD_PAD), BF16),
                   jax.ShapeDtypeStruct((b, B_HEADS, s, HEAD_PAD), BF16),
                   jax.ShapeDtypeStruct((b, B_HEADS // 2, s, LANES), BF16)],
        compiler_params=_cparams(("parallel",)),
        name="inproj",
    )(x2, mod3, mod3, norm1_g, w_in_ext, q_norm_g, w_uq_p, kv_norm_g, w_uk_p, w_v, cos_t, sin_t)


def _dil_block(q_ref, k_ref, v_ref, bias_ref, o_scr, l_scr, pi, dil, nblk, job):
    sub_len = nblk * DIL_QB
    r = job // nblk
    bi = job % nblk
    q0 = bi * DIL_QB
    ws = jnp.clip(q0 - A_RADIUS, 0, sub_len - DIL_KW)
    var = jnp.where(bi == 0, 0, jnp.where(bi == nblk - 1, 2, 1))
    if dil == 1:
        qsl = pl.ds(pl.multiple_of(q0, DIL_QB), DIL_QB)
        ksl = pl.ds(pl.multiple_of(ws, A_RADIUS), DIL_KW)
    else:
        qsl = pl.ds(r + dil * q0, DIL_QB, stride=dil)
        ksl = pl.ds(r + dil * ws, DIL_KW, stride=dil)
    q = q_ref[qsl, :]
    kw = k_ref[ksl, :].astype(BF16)
    vw = v_ref[ksl, :].astype(BF16)
    lo = lax.broadcasted_iota(jnp.int32, q.shape, 1) < A_HEAD_DIM
    outs, lses = [], []
    for hh in range(2):
        qm = jnp.where(lo if hh == 0 else jnp.logical_not(lo), q, 0.0).astype(BF16)
        sc = lax.dot_general(qm, kw, (((1,), (1,)), ((), ())), preferred_element_type=F32)
        sc = sc + bias_ref[pi * 3 + var, hh]
        m = jnp.max(sc, axis=-1, keepdims=True)
        p = jnp.exp2(sc - m)
        l = jnp.sum(p, axis=-1, keepdims=True)
        o = jnp.dot(p.astype(BF16), vw, preferred_element_type=F32)
        outs.append(o / l)
        lses.append(m + jnp.log2(l))
    o_scr[pi, qsl, :] = jnp.where(lo, outs[0], outs[1])
    l_scr[pi, qsl, :] = jnp.where(lo, lses[0], lses[1])


def _dilated_kernel(q_ref, k_ref, v_ref, bias_ref, out_ref, o_scr, l_scr):
    s = q_ref.shape[0]
    njobs = s // DIL_QB
    for pi, (_, dil) in enumerate(A_PATTERNS):
        blk = functools.partial(_dil_block, q_ref, k_ref, v_ref, bias_ref, o_scr, l_scr, pi, dil,
                                s // dil // DIL_QB)

        def group(g, c, blk=blk):
            for u in range(DIL_UNROLL):
                blk(g * DIL_UNROLL + u)
            return c

        lax.fori_loop(0, njobs // DIL_UNROLL, group, 0)

    chunk = 512

    def comb(i, c):
        rows = pl.ds(pl.multiple_of(i * chunk, chunk), chunk)
        l0, l1, l2 = l_scr[0, rows, :], l_scr[1, rows, :], l_scr[2, rows, :]
        mx = jnp.maximum(jnp.maximum(l0, l1), l2)
        e0, e1, e2 = jnp.exp2(l0 - mx), jnp.exp2(l1 - mx), jnp.exp2(l2 - mx)
        num = e0 * o_scr[0, rows, :] + e1 * o_scr[1, rows, :] + e2 * o_scr[2, rows, :]
        out_ref[rows, :] = num / (e0 + e1 + e2)
        return c

    lax.fori_loop(0, s // chunk, comb, 0)


def _dilated(qa, ka, va, bias):
    b, s, _ = qa.shape
    npair = A_WIDTH // LANES
    assert (s // DIL_QB) % DIL_UNROLL == 0
    for _, dil in A_PATTERNS:
        assert (s // dil) % DIL_QB == 0 and s // dil >= DIL_KW
    blk = pl.BlockSpec((None, s, LANES), lambda bi, p: (bi, 0, p))
    return pl.pallas_call(
        _dilated_kernel,
        grid=(b, npair),
        in_specs=[blk, blk, blk,
                  pl.BlockSpec((bias.shape[0], 2, DIL_QB, DIL_KW), lambda bi, p: (0, p, 0, 0))],
        out_specs=blk,
        out_shape=jax.ShapeDtypeStruct((b, s, A_WIDTH), F32),
        scratch_shapes=[pltpu.VMEM((len(A_PATTERNS), s, LANES), F32),
                        pltpu.VMEM((len(A_PATTERNS), s, LANES), F32)],
        compiler_params=_cparams(("parallel", "parallel")),
        name="dilated",
    )(qa, ka, va, bias)


def _mla_kernel(q_ref, k_ref, v_ref, o_ref):
    tq = q_ref.shape[1]
    nkc = k_ref.shape[1] // MLA_KC
    outs = []
    for hh in range(2):
        q = q_ref[hh]
        m = jnp.full((tq, 1), -jnp.inf, F32)
        l = jnp.zeros((tq, 1), F32)
        acc = jnp.zeros((tq, LANES), F32)
        for c in range(nkc):
            keys = slice(c * MLA_KC, (c + 1) * MLA_KC)
            sc = lax.dot_general(q, k_ref[hh, keys, :], (((1,), (1,)), ((), ())), preferred_element_type=F32)
            m_new = jnp.maximum(m, jnp.max(sc, axis=-1, keepdims=True))
            alpha = jnp.exp2(m - m_new)
            p = jnp.exp2(sc - m_new)
            l = alpha * l + jnp.sum(p, axis=-1, keepdims=True)
            acc = alpha * acc + jnp.dot(p.astype(BF16), v_ref[keys, :], preferred_element_type=F32)
            m = m_new
        outs.append(acc / l)
    lo = lax.broadcasted_iota(jnp.int32, outs[0].shape, 1) < B_VDIM
    o_ref[...] = jnp.where(lo, outs[0], outs[1])


def _mla(qb, kb, vb):
    b, h, s, _ = qb.shape
    npair = h // 2
    tq = MLA_TQ
    return pl.pallas_call(
        _mla_kernel,
        grid=(b, npair, s // tq),
        in_specs=[pl.BlockSpec((None, 2, tq, HEAD_PAD), lambda bi, p, qi: (bi, p, qi, 0)),
                  pl.BlockSpec((None, 2, s, HEAD_PAD), lambda bi, p, qi: (bi, p, 0, 0)),
                  pl.BlockSpec((None, None, s, LANES), lambda bi, p, qi: (bi, p, 0, 0))],
        out_specs=pl.BlockSpec((None, tq, LANES), lambda bi, p, qi: (bi, qi, p)),
        out_shape=jax.ShapeDtypeStruct((b, s, B_WIDTH), F32),
        compiler_params=_cparams(("parallel", "parallel", "arbitrary")),
        name="mla",
    )(qb, kb, vb)


def _outproj_kernel(x_ref, a_ref, b_ref, ag_ref, bg_ref, wo_ref, g1_ref, n2_ref, sc_ref, sh_ref, g2_ref,
                    wrt_ref, ws1_ref, ws3_ref, ws2_ref, x1s_ref, h2p_ref, scores_ref):
    an = _rms(a_ref[...], ag_ref[...])
    bn = _rms(b_ref[...], bg_ref[...])
    mix = jnp.concatenate([an, bn], axis=-1).astype(BF16)
    x1 = x_ref[...] + g1_ref[...] * jnp.dot(mix, wo_ref[...], preferred_element_type=F32)
    h2 = _rms(x1, n2_ref[...]) * (1.0 + sc_ref[...]) + sh_ref[...]
    packed = _pack_rows(h2)
    for cg in range(h2p_ref.shape[0]):
        h2p_ref[cg] = packed[:, cg * LANES:(cg + 1) * LANES]
    logits = lax.dot_general(wrt_ref[...], h2, (((1,), (1,)), ((), ())), precision=HIGHEST,
                             preferred_element_type=F32)
    scores_ref[...] = jax.nn.sigmoid(logits)
    h2b = h2.astype(BF16)
    hid = _silu(jnp.dot(h2b, ws1_ref[...], preferred_element_type=F32)) * jnp.dot(
        h2b, ws3_ref[...], preferred_element_type=F32)
    shared = jnp.dot(hid.astype(BF16), ws2_ref[...], preferred_element_type=F32)
    x1s_ref[...] = x1 + g2_ref[...] * shared


def _outproj(x2, a_out, b_out, a_out_g, b_out_g, w_o, mod3, norm2_g, w_router_t, ws1, ws3, ws2, s):
    n, d = x2.shape
    tm = TM_OUTPROJ
    tpb = s // tm
    row = lambda i: (i, 0)
    const = lambda i: (0, 0)
    modspec = lambda j: pl.BlockSpec((None, 1, d), lambda i: ((i // tpb) * 6 + j, 0, 0))
    return pl.pallas_call(
        _outproj_kernel,
        grid=(n // tm,),
        in_specs=[pl.BlockSpec((tm, d), row),
                  pl.BlockSpec((tm, A_WIDTH), row),
                  pl.BlockSpec((tm, B_WIDTH), row),
                  pl.BlockSpec((1, A_WIDTH), const),
                  pl.BlockSpec((1, B_WIDTH), const),
                  pl.BlockSpec(w_o.shape, const),
                  modspec(2),
                  pl.BlockSpec((1, d), const),
                  modspec(4), modspec(3), modspec(5),
                  pl.BlockSpec(w_router_t.shape, const),
                  pl.BlockSpec(ws1.shape, const),
                  pl.BlockSpec(ws3.shape, const),
                  pl.BlockSpec(ws2.shape, const)],
        out_specs=[pl.BlockSpec((tm, d), row),
                   pl.BlockSpec((d // 2 // LANES, tm, LANES), lambda i: (0, i, 0)),
                   pl.BlockSpec((N_EXPERTS, tm), lambda i: (0, i))],
        out_shape=[jax.ShapeDtypeStruct((n, d), F32),
                   jax.ShapeDtypeStruct((d // 2 // LANES, n, LANES), U32),
                   jax.ShapeDtypeStruct((N_EXPERTS, n), F32)],
        compiler_params=_cparams(("parallel",)),
        name="outproj",
    )(x2, a_out, b_out, a_out_g, b_out_g, w_o, mod3, norm2_g, mod3, mod3, mod3, w_router_t, ws1, ws3, ws2)


def _slab_order(v):
    return v.reshape((N_GROUPS, GROUP_SIZE) + v.shape[1:]).swapaxes(0, 1).reshape(v.shape)


def _expert_order(v):
    return v.reshape((GROUP_SIZE, N_GROUPS) + v.shape[1:]).swapaxes(0, 1).reshape(v.shape)


def _sublane_all(x, op):
    for sh in (4, 2, 1):
        x = op(x, pltpu.roll(x, sh, 0))
    return x


def _route_kernel(st_ref, bias_ref, eidx_ref, gate_ref, cnt_ref):
    nsl = GROUP_SIZE
    t = st_ref.shape[1]
    sub = lax.broadcasted_iota(I32, (SUBLANES, t), 0)
    ninf = -jnp.inf
    big = jnp.int32(1 << 30)
    sc = [st_ref[j * SUBLANES:(j + 1) * SUBLANES, :] for j in range(nsl)]
    sel = [sc[j] + bias_ref[j * SUBLANES:(j + 1) * SUBLANES, :] for j in range(nsl)]
    eid = [sub * GROUP_SIZE + j for j in range(nsl)]

    m1 = sel[0]
    m2 = jnp.full_like(m1, ninf)
    for j in range(1, nsl):
        m2 = jnp.maximum(m2, jnp.minimum(m1, sel[j]))
        m1 = jnp.maximum(m1, sel[j])
    gs = m1 + m2

    rank = jnp.zeros((SUBLANES, t), I32)
    for sh in range(1, N_GROUPS):
        other = pltpu.roll(gs, sh, 0)
        ahead = (other > gs) | ((other == gs) & (sub >= sh))
        rank = rank + ahead.astype(I32)
    gmask = rank < TOPK_GROUPS

    msel = [jnp.where(gmask, sel[j], ninf) for j in range(nsl)]
    hits = [jnp.zeros((SUBLANES, t), I32) for _ in range(nsl)]
    eidx = jnp.zeros((TOP_K, t), I32)
    gates = jnp.zeros((TOP_K, t), F32)
    for k in range(TOP_K):
        mx = msel[0]
        for j in range(1, nsl):
            mx = jnp.maximum(mx, msel[j])
        mx = _sublane_all(mx, jnp.maximum)
        cand = jnp.where(msel[0] == mx, eid[0], big)
        for j in range(1, nsl):
            cand = jnp.minimum(cand, jnp.where(msel[j] == mx, eid[j], big))
        idx = _sublane_all(cand, jnp.minimum)
        gk = jnp.zeros((SUBLANES, t), F32)
        for j in range(nsl):
            hit = eid[j] == idx
            gk = gk + jnp.where(hit, sc[j], 0.0)
            msel[j] = jnp.where(hit, ninf, msel[j])
            hits[j] = hits[j] + hit.astype(I32)
        gk = _sublane_all(gk, jnp.add)
        eidx = jnp.where(sub == k, idx, eidx)
        gates = jnp.where(sub == k, gk, gates)
    gsum = _sublane_all(gates, jnp.add)
    eidx_ref[...] = eidx
    gate_ref[...] = gates / gsum * ROUTED_SCALE

    @pl.when(pl.program_id(0) == 0)
    def _():
        cnt_ref[...] = jnp.zeros_like(cnt_ref)

    for j in range(nsl):
        cnt_ref[j * SUBLANES:(j + 1) * SUBLANES, :] += jnp.sum(hits[j].astype(F32), axis=1,
                                                               keepdims=True).astype(I32)


def _route(scores_t, e_bias_slab):
    e, n = scores_t.shape
    t = ROUTE_T
    return pl.pallas_call(
        _route_kernel,
        grid=(n // t,),
        in_specs=[pl.BlockSpec((e, t), lambda i: (0, i)),
                  pl.BlockSpec((e, 1), lambda i: (0, 0))],
        out_specs=[pl.BlockSpec((TOP_K, t), lambda i: (0, i)),
                   pl.BlockSpec((TOP_K, t), lambda i: (0, i)),
                   pl.BlockSpec((e, 1), lambda i: (0, 0))],
        out_shape=[jax.ShapeDtypeStruct((TOP_K, n), I32),
                   jax.ShapeDtypeStruct((TOP_K, n), F32),
                   jax.ShapeDtypeStruct((e, 1), I32)],
        compiler_params=_cparams(("arbitrary",)),
        name="route",
    )(scores_t, e_bias_slab)


def _dest_kernel(eidx_ref, pstart_ref, dest_ref, carry_ref):
    @pl.when(pl.program_id(0) == 0)
    def _():
        carry_ref[...] = jnp.zeros_like(carry_ref)

    nsl = GROUP_SIZE
    t = eidx_ref.shape[1]
    sub = lax.broadcasted_iota(I32, (SUBLANES, t), 0)
    eid = [sub * GROUP_SIZE + j for j in range(nsl)]
    ek = [eidx_ref[k:k + 1, :] for k in range(TOP_K)]
    slabs = []
    for j in range(nsl):
        oh = jnp.zeros((SUBLANES, t), F32)
        for k in range(TOP_K):
            oh = oh + (eid[j] == ek[k]).astype(F32)
        slabs.append(oh)
    onehot = jnp.concatenate(slabs, axis=0)
    row = lax.broadcasted_iota(I32, (t, t), 0)
    col = lax.broadcasted_iota(I32, (t, t), 1)
    upper = (row < col).astype(BF16)
    before = jnp.dot(onehot.astype(BF16), upper, preferred_element_type=F32)
    base = before + carry_ref[...] + pstart_ref[...]
    dest = jnp.zeros((TOP_K, t), I32)
    for k in range(TOP_K):
        acc = jnp.zeros((SUBLANES, t), F32)
        for j in range(nsl):
            acc = acc + jnp.where(eid[j] == ek[k], base[j * SUBLANES:(j + 1) * SUBLANES, :], 0.0)
        dk = _sublane_all(acc, jnp.add).astype(I32)
        dest = jnp.where(sub == k, dk, dest)
    dest_ref[...] = dest
    carry_ref[...] += jnp.sum(onehot, axis=1, keepdims=True)


def _dest(eidx_t, pstart_slab):
    n = eidx_t.shape[1]
    t = ROUTE_T
    return pl.pallas_call(
        _dest_kernel,
        grid=(n // t,),
        in_specs=[pl.BlockSpec((TOP_K, t), lambda i: (0, i)),
                  pl.BlockSpec((N_EXPERTS, 1), lambda i: (0, 0))],
        out_specs=pl.BlockSpec((TOP_K, t), lambda i: (0, i)),
        out_shape=jax.ShapeDtypeStruct((TOP_K, n), I32),
        scratch_shapes=[pltpu.VMEM((N_EXPERTS, 1), F32)],
        compiler_params=_cparams(("arbitrary",)),
        name="dest",
    )(eidx_t, pstart_slab)


def _sc_scatter(x, idx, rows_out, nplane, nslot):
    del nplane
    num = idx.shape[0]
    mesh = plsc.VectorSubcoreMesh(core_axis_name="core", subcore_axis_name="subcore")

    @pl.kernel(out_type=jax.ShapeDtypeStruct((rows_out, x.shape[1]), x.dtype), mesh=mesh, scratch_types=[])
    def scatter(x_hbm, i_hbm, o_hbm):
        def body(x_vmem, i_vmem):
            pltpu.sync_copy(x_vmem, o_hbm.at[i_vmem.at[0]])

        pltpu.emit_pipeline(
            body,
            grid=(num // SC_WINDOW,),
            in_specs=[pl.BlockSpec((SC_WINDOW, x.shape[1]), index_map=lambda i: (i // nslot, 0)),
                      pl.BlockSpec((1, SC_WINDOW), index_map=lambda i: (0, i))],
            out_specs=[],
            core_axis_name=("core", "subcore"),
            dimension_semantics=(pltpu.PARALLEL,),
        )(x_hbm, i_hbm)

    return scatter(x, idx.reshape(1, num))


def _expert_kernel(cstart_ref, nused_ref, valid_ref, w1_ref, w3_ref, w2_ref, xg_ref, y_ref,
                   w1b, w3b, w2b, xbuf, ybuf, xsem, ysem):
    e = pl.program_id(0)
    c0 = cstart_ref[e]
    c1 = cstart_ref[e + 1]
    nused = nused_ref[0]
    nchunks = xg_ref.shape[1] // MOE_BLK

    def rows(g):
        start = g * MOE_BLK
        return pl.ds(start if isinstance(g, int) else pl.multiple_of(start, MOE_BLK), MOE_BLK)

    nplane = xg_ref.shape[0]

    def xcopies(g):
        slot = g % EXP_XBUF
        return [pltpu.make_async_copy(xg_ref.at[pl.ds(0, nplane), rows(g)], xbuf.at[slot], xsem.at[slot])]

    def ycopies(g, slot):
        return [pltpu.make_async_copy(ybuf.at[slot], y_ref.at[pl.ds(0, nplane), rows(g)], ysem.at[slot])]

    @pl.when(e == 0)
    def _():
        for j in range(EXP_AHEAD):
            @pl.when(j < nused)
            def _():
                for cp in xcopies(j):
                    cp.start()

    @pl.when(c1 > c0)
    def _():
        w1b[...] = w1_ref[...].astype(BF16)
        w3b[...] = w3_ref[...].astype(BF16)
        w2b[...] = w2_ref[...].astype(BF16)

        def ffn(g):
            row_id = lax.broadcasted_iota(I32, (MOE_BLK, LANES), 0)
            keep = row_id < valid_ref[g]
            halves = [_unpack_rows(jnp.where(keep, xbuf[g % EXP_XBUF, c], jnp.uint32(0))) for c in range(nplane)]
            xb = jnp.concatenate([h[0] for h in halves] + [h[1] for h in halves], axis=1).astype(BF16)
            hid = _silu(jnp.dot(xb, w1b[...], preferred_element_type=F32)) * jnp.dot(
                xb, w3b[...], preferred_element_type=F32)
            return _pack_rows(jnp.dot(hid.astype(BF16), w2b[...], preferred_element_type=F32))

        def chunk(g, carry):
            @pl.when(g + EXP_AHEAD < nused)
            def _():
                for cp in xcopies(g + EXP_AHEAD):
                    cp.start()

            for cp in xcopies(g):
                cp.wait()
            yp = ffn(g)
            yslot = g % 2

            @pl.when(g >= 2)
            def _():
                for cp in ycopies(g - 2, yslot):
                    cp.wait()

            for c in range(nplane):
                ybuf[yslot, c] = yp[:, c * LANES:(c + 1) * LANES]
            for cp in ycopies(g, yslot):
                cp.start()
            return carry

        lax.fori_loop(c0, c1, chunk, 0)

    @pl.when(e == pl.num_programs(0) - 1)
    def _():
        for back in (2, 1):
            g = nused - back

            @pl.when(g >= 0)
            def _():
                for cp in ycopies(g, g % 2):
                    cp.wait()

        ybuf[0] = jnp.zeros(ybuf.shape[1:], ybuf.dtype)

        def zstart(g, c):
            for cp in ycopies(g, 0):
                cp.start()
            return c

        def zwait(g, c):
            for cp in ycopies(g, 0):
                cp.wait()
            return c

        lax.fori_loop(nused, nchunks, zstart, 0)
        lax.fori_loop(nused, nchunks, zwait, 0)


def _experts(cstart, nused, valid, xg, w1, w3, w2):
    nplane, rows, _ = xg.shape
    dh = nplane * LANES
    ne, d, f = w1.shape
    grid_spec = pltpu.PrefetchScalarGridSpec(
        num_scalar_prefetch=3,
        grid=(ne,),
        in_specs=[pl.BlockSpec((None, d, f), lambda e, cs, nu, va: (e, 0, 0)),
                  pl.BlockSpec((None, d, f), lambda e, cs, nu, va: (e, 0, 0)),
                  pl.BlockSpec((None, f, d), lambda e, cs, nu, va: (e, 0, 0)),
                  pl.BlockSpec(memory_space=pl.ANY)],
        out_specs=pl.BlockSpec(memory_space=pl.ANY),
        scratch_shapes=[pltpu.VMEM((d, f), BF16), pltpu.VMEM((d, f), BF16), pltpu.VMEM((f, d), BF16),
                        pltpu.VMEM((EXP_XBUF, nplane, MOE_BLK, LANES), U32),
                        pltpu.VMEM((2, nplane, MOE_BLK, LANES), U32),
                        pltpu.SemaphoreType.DMA((EXP_XBUF,)), pltpu.SemaphoreType.DMA((2,))],
    )
    return pl.pallas_call(
        _expert_kernel,
        grid_spec=grid_spec,
        out_shape=jax.ShapeDtypeStruct((dh // LANES, rows, LANES), U32),
        compiler_params=_cparams(("arbitrary",)),
        name="experts",
    )(cstart, nused, valid, w1, w3, w2, xg)


def _sc_gather(x, idx):
    num = idx.shape[0]
    mesh = plsc.VectorSubcoreMesh(core_axis_name="core", subcore_axis_name="subcore")

    @pl.kernel(out_type=jax.ShapeDtypeStruct((num, x.shape[1]), x.dtype), mesh=mesh)
    def gather(x_hbm, i_hbm, o_hbm):
        def body(i_vmem, o_vmem):
            pltpu.sync_copy(x_hbm.at[i_vmem.at[0]], o_vmem)

        pltpu.emit_pipeline(
            body,
            grid=(num // SC_WINDOW,),
            in_specs=[pl.BlockSpec((1, SC_WINDOW), index_map=lambda i: (0, i))],
            out_specs=[pl.BlockSpec((SC_WINDOW, x.shape[1]), index_map=lambda i: (i, 0))],
            core_axis_name=("core", "subcore"),
            dimension_semantics=(pltpu.PARALLEL,),
        )(i_hbm, o_hbm)

    return gather(x, idx.reshape(1, num))


def _combine_kernel(yg_ref, gate_ref, x1s_ref, g2_ref, fg_ref, *rest):
    out_ref = rest[-1]
    gates = gate_ref[...]
    nch = yg_ref.shape[0]
    r_lo = [None] * nch
    r_hi = [None] * nch
    for k in range(TOP_K):
        gk = gates[:, k:k + 1]
        for c in range(nch):
            lo, hi = _unpack_rows(yg_ref[c, k])
            r_lo[c] = gk * lo if k == 0 else r_lo[c] + gk * lo
            r_hi[c] = gk * hi if k == 0 else r_hi[c] + gk * hi
    routed = jnp.concatenate(r_lo + r_hi, axis=1)
    x2 = x1s_ref[...] + g2_ref[...] * routed
    out_ref[...] = _rms(x2, fg_ref[...])


def _combine(yg, gates, x1s, mod3, final_g, s, part, prev_out):
    n, d = x1s.shape
    t = COMB_T
    tpb = s // t
    nch, _, npart, _ = yg.shape
    off = part * (npart // t)
    in_specs = [pl.BlockSpec((nch, TOP_K, t, LANES), lambda i: (0, 0, i, 0)),
                pl.BlockSpec((t, TOP_K), lambda i: (i + off, 0)),
                pl.BlockSpec((t, d), lambda i: (i + off, 0)),
                pl.BlockSpec((None, 1, d), lambda i: (((i + off) // tpb) * 6 + 5, 0, 0)),
                pl.BlockSpec((1, d), lambda i: (0, 0))]
    args = [yg, gates, x1s, mod3, final_g]
    aliases = {}
    if prev_out is not None:
        in_specs.append(pl.BlockSpec(memory_space=pl.ANY))
        args.append(prev_out)
        aliases = {len(args) - 1: 0}
    return pl.pallas_call(
        _combine_kernel,
        grid=(npart // t,),
        in_specs=in_specs,
        out_specs=pl.BlockSpec((t, d), lambda i: (i + off, 0)),
        out_shape=jax.ShapeDtypeStruct((n, d), F32),
        input_output_aliases=aliases,
        compiler_params=_cparams(("parallel",)),
        name="combine",
    )(*args)


def _place_heads(w, per_head, keep):
    r = w.shape[0]
    w = w.reshape(r, B_HEADS, per_head)[:, :, :keep]
    return jnp.pad(w, ((0, 0), (0, 0), (0, HEAD_PAD - keep))).reshape(r, B_HEADS * HEAD_PAD)


def _rope_tables(s):
    inv = ROPE_THETA ** (-jnp.arange(0, B_ROPE, 2, dtype=jnp.float32) / B_ROPE)
    ang = jnp.arange(s, dtype=jnp.float32)[:, None] * inv[None, :]
    cos, sin = jnp.cos(ang), jnp.sin(ang)
    ones = jnp.ones((s, B_NOPE), F32)
    zeros = jnp.zeros((s, B_NOPE), F32)
    tail1 = jnp.ones((s, HEAD_PAD - B_NOPE - B_ROPE), F32)
    tail0 = jnp.zeros((s, HEAD_PAD - B_NOPE - B_ROPE), F32)
    return (jnp.concatenate([ones, cos, cos, tail1], axis=1),
            jnp.concatenate([zeros, -sin, sin, tail0], axis=1))


def kernel(x, c, w_ada, b_ada, norm1_g, w_in, q_norm_g, w_uq, kv_norm_g, w_ukv, rel_table, a_out_g, b_out_g,
           w_o, norm2_g, w_router, e_bias, w1, w3, w2, ws1, ws3, ws2, final_g):
    b, s, d = x.shape
    n = b * s
    assert w_ada.shape[0] == 1, "single layer"
    x2 = x.reshape(n, d)

    mod = _ada(c, w_ada[0], b_ada[0])
    mod3 = mod.reshape(b * 6, 1, d)
    bias = _bias_tiles(rel_table)

    wi = w_in[0]
    c_kpe = 3 * A_WIDTH + Q_LORA + KV_LORA
    kpe_cols = jnp.pad(wi[:, c_kpe:], ((0, 0), (B_NOPE, HEAD_PAD - B_NOPE - B_ROPE)))
    w_in_ext = jnp.concatenate([wi[:, :c_kpe], kpe_cols], axis=1).astype(BF16)
    w_uq_p = _place_heads(w_uq[0], B_NOPE + B_ROPE, B_NOPE + B_ROPE).astype(BF16)
    w_uk_p = _place_heads(w_ukv[0], B_NOPE + B_VDIM, B_NOPE).astype(BF16)
    w_v = w_ukv[0].reshape(KV_LORA, B_HEADS, B_NOPE + B_VDIM)[:, :, B_NOPE:].reshape(KV_LORA, B_WIDTH).astype(BF16)
    cos_t, sin_t = _rope_tables(s)

    qa, ka, va, qb, kb, vb = _inproj(x2, mod3, norm1_g, w_in_ext, q_norm_g, w_uq_p, kv_norm_g, w_uk_p, w_v,
                                     cos_t, sin_t, b, s)
    a_out = _dilated(qa.reshape(b, s, A_WIDTH), ka.reshape(b, s, A_WIDTH), va.reshape(b, s, A_WIDTH), bias)
    b_out = _mla(qb, kb, vb)

    w_router_t = _slab_order(w_router[0].T)
    x1s, h2p, scores_t = _outproj(x2, a_out.reshape(n, A_WIDTH), b_out.reshape(n, B_WIDTH), a_out_g, b_out_g,
                                  w_o[0].astype(BF16), mod3, norm2_g, w_router_t,
                                  ws1[0].astype(BF16), ws3[0].astype(BF16), ws2[0].astype(BF16), s)

    eidx_t, gates_t, counts_slab = _route(scores_t, _slab_order(e_bias.reshape(N_EXPERTS, 1)))

    counts = _expert_order(counts_slab)[:, 0]
    padded = (counts + MOE_BLK - 1) // MOE_BLK * MOE_BLK
    pends = jnp.cumsum(padded)
    pstart = pends - padded
    nk = n * TOP_K
    nblk = -(-(nk + N_EXPERTS * (MOE_BLK - 1)) // MOE_BLK)
    rows_total = nblk * MOE_BLK
    blk_row = jnp.arange(nblk, dtype=I32) * MOE_BLK
    blk_e = jnp.minimum(jnp.sum((pends[None, :] <= blk_row[:, None]).astype(I32), axis=1), N_EXPERTS - 1)
    seg_end = (pstart + counts)[blk_e]
    valid = jnp.clip(seg_end - blk_row, 0, MOE_BLK).astype(I32)
    nused = (pends[-1] // MOE_BLK).astype(I32).reshape(1)
    cstart = jnp.concatenate([jnp.zeros((1,), I32), (pends // MOE_BLK).astype(I32)])

    dest_t = _dest(eidx_t, _slab_order(pstart.astype(F32).reshape(N_EXPERTS, 1)))
    nplane = h2p.shape[0]
    gidx = (dest_t.reshape(1, nk) + (jnp.arange(nplane, dtype=I32) * rows_total)[:, None]).reshape(nplane * nk)
    nwin = n // SC_WINDOW
    sidx = dest_t.reshape(TOP_K, nwin, SC_WINDOW).transpose(1, 0, 2)[None] + (
        jnp.arange(nplane, dtype=I32) * rows_total)[:, None, None, None]
    xg = _sc_scatter(h2p.reshape(nplane * n, LANES), sidx.reshape(nplane * nk), nplane * rows_total, nplane, TOP_K)
    y = _experts(cstart, nused, valid, xg.reshape(nplane, rows_total, LANES), w1[0], w3[0], w2[0])
    y_flat = y.reshape(nplane * rows_total, LANES)
    gidx3 = gidx.reshape(nplane, TOP_K, n)
    gates = gates_t.T
    npart = n // COMB_SPLIT
    out = None
    for part in range(COMB_SPLIT):
        pidx = gidx3[:, :, part * npart:(part + 1) * npart].reshape(nplane * TOP_K * npart)
        yg = _sc_gather(y_flat, pidx).reshape(nplane, TOP_K, npart, LANES)
        out = _combine(yg, gates, x1s, mod3, final_g.reshape(1, d), s, part, out)
    return out.reshape(b, s, d)
```

```python
import functools
import math

import jax
import jax.numpy as jnp
from jax import lax
from jax.experimental import pallas as pl
from jax.experimental.pallas import tpu as pltpu
from jax.experimental.pallas import tpu_sc as plsc

F32 = jnp.float32
BF16 = jnp.bfloat16
U32 = jnp.uint32
I32 = jnp.int32
HIGHEST = lax.Precision.HIGHEST

D_MODEL = 1024
A_HEADS = 8
A_HEAD_DIM = 64
A_WIDTH = A_HEADS * A_HEAD_DIM
A_PATTERNS = ((128, 1), (512, 4), (2048, 16))
A_RADIUS = 64
REL_BUCKETS = 32
REL_MAX_DIST = 1024
B_HEADS = 8
B_NOPE = 64
B_ROPE = 32
B_VDIM = 64
B_WIDTH = B_HEADS * B_VDIM
Q_LORA = 384
KV_LORA = 256
ROPE_THETA = 10000.0
N_EXPERTS = 256
TOP_K = 8
N_GROUPS = 8
GROUP_SIZE = N_EXPERTS // N_GROUPS
TOPK_GROUPS = 4
EXPERT_FF = 256
SHARED_FF = 256
ROUTED_SCALE = 2.5
EPS = 1e-6
NEG_INF = -1e30
LOG2E = math.log2(math.e)

LANES = 128
SUBLANES = 8
HEAD_PAD = 128
IN_COLS_EXT = 3 * A_WIDTH + Q_LORA + KV_LORA + HEAD_PAD

TM_INPROJ = 512
TM_OUTPROJ = 1024
DIL_QB = 128
DIL_KW = DIL_QB + 2 * A_RADIUS
DIL_UNROLL = 8
MLA_TQ = 1024
MLA_KC = 2048
ROUTE_T = 512
MOE_BLK = 256
EXP_AHEAD = 6
EXP_XBUF = EXP_AHEAD + 1
SC_WINDOW = 128
COMB_T = 256
COMB_SPLIT = 4
VMEM_LIMIT = 56 * 1024 * 1024


def _cparams(sem):
    return pltpu.CompilerParams(dimension_semantics=sem, vmem_limit_bytes=VMEM_LIMIT)


def _rms(x, g):
    return x * lax.rsqrt(jnp.mean(x * x, axis=-1, keepdims=True) + EPS) * g


def _silu(x):
    return x * jax.nn.sigmoid(x)


def _pack_rows(x):
    half = x.shape[1] // 2
    bits = lax.bitcast_convert_type(x.astype(BF16).astype(F32), U32)
    return (bits[:, :half] >> 16) | bits[:, half:]


def _unpack_rows(w):
    lo = lax.bitcast_convert_type(w << 16, F32)
    hi = lax.bitcast_convert_type(w & jnp.uint32(0xFFFF0000), F32)
    return lo, hi


def _ada_kernel(c_ref, w_ref, b_ref, o_ref):
    o_ref[...] = jnp.dot(_silu(c_ref[...]), w_ref[...], precision=HIGHEST,
                         preferred_element_type=F32) + b_ref[...]


def _ada(c, w_ada, b_ada):
    b, d = c.shape
    n6 = w_ada.shape[1] // d
    return pl.pallas_call(
        _ada_kernel,
        grid=(n6,),
        in_specs=[pl.BlockSpec((b, d), lambda j: (0, 0)),
                  pl.BlockSpec((d, d), lambda j: (0, j)),
                  pl.BlockSpec((1, d), lambda j: (0, j))],
        out_specs=pl.BlockSpec((b, d), lambda j: (0, j)),
        out_shape=jax.ShapeDtypeStruct((b, n6 * d), F32),
        compiler_params=_cparams(("parallel",)),
        name="ada",
    )(c, w_ada, b_ada.reshape(1, -1))


def _t5_bucket(rel):
    half = REL_BUCKETS // 2
    max_exact = half // 2
    ret = jnp.where(rel > 0, half, 0)
    n = jnp.abs(rel)
    nf = jnp.maximum(n, 1).astype(jnp.float32)
    large = max_exact + (jnp.log(nf / max_exact) / math.log(REL_MAX_DIST / max_exact)
                         * (half - max_exact)).astype(jnp.int32)
    large = jnp.minimum(large, half - 1)
    return ret + jnp.where(n < max_exact, n, large)


DIL_SHIFTS = (A_RADIUS, 0, -A_RADIUS)


def _bucket_tiles():
    qi = jnp.arange(DIL_QB, dtype=jnp.int32)[:, None]
    ki = jnp.arange(DIL_KW, dtype=jnp.int32)[None, :]
    tiles = []
    for _, dilation in A_PATTERNS:
        for shift in DIL_SHIFTS:
            off = ki + shift - A_RADIUS - qi
            bkt = _t5_bucket(off * dilation)
            tiles.append(jnp.where(jnp.abs(off) <= A_RADIUS, bkt, -1))
    return jnp.stack(tiles, axis=0)


def _bias_kernel(tab_ref, bkt_ref, o_ref):
    bkt = bkt_ref[0]
    for h in range(A_HEADS):
        acc = jnp.full(bkt.shape, NEG_INF, F32)
        for b in range(REL_BUCKETS):
            acc = jnp.where(bkt == b, tab_ref[b, h] * LOG2E, acc)
        o_ref[0, h] = acc


def _bias_tiles(rel_table):
    bkt = _bucket_tiles()
    nt = bkt.shape[0]
    return pl.pallas_call(
        _bias_kernel,
        grid=(nt,),
        in_specs=[pl.BlockSpec(memory_space=pltpu.SMEM),
                  pl.BlockSpec((1, DIL_QB, DIL_KW), lambda t: (t, 0, 0))],
        out_specs=pl.BlockSpec((1, A_HEADS, DIL_QB, DIL_KW), lambda t: (t, 0, 0, 0)),
        out_shape=jax.ShapeDtypeStruct((nt, A_HEADS, DIL_QB, DIL_KW), F32),
        compiler_params=_cparams(("parallel",)),
        name="bias",
    )(rel_table, bkt)


def _rope(x, cos, sin, lane_lt_mid):
    half = B_ROPE // 2
    rot = jnp.where(lane_lt_mid, pltpu.roll(x, HEAD_PAD - half, 1), pltpu.roll(x, half, 1))
    return x * cos + rot * sin


def _inproj_kernel(x_ref, sc_ref, sh_ref, g1_ref, win_ref, qg_ref, wuq_ref, kvg_ref, wuk_ref, wv_ref,
                   cos_ref, sin_ref, qa_ref, ka_ref, va_ref, qb_ref, kb_ref, vb_ref):
    x = x_ref[...]
    h = _rms(x, g1_ref[...]) * (1.0 + sc_ref[...]) + sh_ref[...]
    proj = jnp.dot(h.astype(BF16), win_ref[...], preferred_element_type=F32)
    aw = A_WIDTH
    qa_ref[...] = proj[:, 0:aw] * (LOG2E / math.sqrt(A_HEAD_DIM))
    ka_ref[...] = proj[:, aw:2 * aw]
    va_ref[...] = proj[:, 2 * aw:3 * aw]
    c0 = 3 * aw
    q_lat = proj[:, c0:c0 + Q_LORA]
    kv_lat = proj[:, c0 + Q_LORA:c0 + Q_LORA + KV_LORA]
    kpe = proj[:, c0 + Q_LORA + KV_LORA:]
    qn = _rms(q_lat, qg_ref[...]).astype(BF16)
    kvn = _rms(kv_lat, kvg_ref[...]).astype(BF16)
    qm = jnp.dot(qn, wuq_ref[...], preferred_element_type=F32)
    kn = jnp.dot(kvn, wuk_ref[...], preferred_element_type=F32)
    vv = jnp.dot(kvn, wv_ref[...], preferred_element_type=F32)
    cos = cos_ref[...]
    sin = sin_ref[...]
    lane = lax.broadcasted_iota(jnp.int32, cos.shape, 1)
    lt_mid = lane < (B_NOPE + B_ROPE // 2)
    qscale = LOG2E / math.sqrt(B_NOPE + B_ROPE)
    for hd in range(B_HEADS):
        sl = slice(hd * HEAD_PAD, (hd + 1) * HEAD_PAD)
        qb_ref[hd] = (_rope(qm[:, sl], cos, sin, lt_mid) * qscale).astype(BF16)
        kb_ref[hd] = _rope(kn[:, sl] + kpe, cos, sin, lt_mid).astype(BF16)
    for p in range(B_HEADS // 2):
        vb_ref[p] = vv[:, p * LANES:(p + 1) * LANES].astype(BF16)


def _inproj(x2, mod3, norm1_g, w_in_ext, q_norm_g, w_uq_p, kv_norm_g, w_uk_p, w_v, cos_t, sin_t, b, s):
    n, d = x2.shape
    tm = TM_INPROJ
    tpb = s // tm
    row = lambda i: (i, 0)
    const = lambda i: (0, 0)
    hm = lambda i: (i // tpb, 0, i % tpb, 0)
    return pl.pallas_call(
        _inproj_kernel,
        grid=(n // tm,),
        in_specs=[pl.BlockSpec((tm, d), row),
                  pl.BlockSpec((None, 1, d), lambda i: ((i // tpb) * 6 + 1, 0, 0)),
                  pl.BlockSpec((None, 1, d), lambda i: ((i // tpb) * 6 + 0, 0, 0)),
                  pl.BlockSpec((1, d), const),
                  pl.BlockSpec(w_in_ext.shape, const),
                  pl.BlockSpec((1, Q_LORA), const),
                  pl.BlockSpec(w_uq_p.shape, const),
                  pl.BlockSpec((1, KV_LORA), const),
                  pl.BlockSpec(w_uk_p.shape, const),
                  pl.BlockSpec(w_v.shape, const),
                  pl.BlockSpec((tm, HEAD_PAD), lambda i: (i % tpb, 0)),
                  pl.BlockSpec((tm, HEAD_PAD), lambda i: (i % tpb, 0))],
        out_specs=[pl.BlockSpec((tm, A_WIDTH), row),
                   pl.BlockSpec((tm, A_WIDTH), row),
                   pl.BlockSpec((tm, A_WIDTH), row),
                   pl.BlockSpec((None, B_HEADS, tm, HEAD_PAD), hm),
                   pl.BlockSpec((None, B_HEADS, tm, HEAD_PAD), hm),
                   pl.BlockSpec((None, B_HEADS // 2, tm, LANES), hm)],
        out_shape=[jax.ShapeDtypeStruct((n, A_WIDTH), F32),
                   jax.ShapeDtypeStruct((n, A_WIDTH), F32),
                   jax.ShapeDtypeStruct((n, A_WIDTH), F32),
                   jax.ShapeDtypeStruct((b, B_HEADS, s, HEAD_PAD), BF16),
                   jax.ShapeDtypeStruct((b, B_HEADS, s, HEAD_PAD), BF16),
                   jax.ShapeDtypeStruct((b, B_HEADS // 2, s, LANES), BF16)],
        compiler_params=_cparams(("parallel",)),
        name="inproj",
    )(x2, mod3, mod3, norm1_g, w_in_ext, q_norm_g, w_uq_p, kv_norm_g, w_uk_p, w_v, cos_t, sin_t)


def _dil_block(q_ref, k_ref, v_ref, bias_ref, o_scr, l_scr, pi, dil, nblk, job):
    sub_len = nblk * DIL_QB
    r = job // nblk
    bi = job % nblk
    q0 = bi * DIL_QB
    ws = jnp.clip(q0 - A_RADIUS, 0, sub_len - DIL_KW)
    var = jnp.where(bi == 0, 0, jnp.where(bi == nblk - 1, 2, 1))
    if dil == 1:
        qsl = pl.ds(pl.multiple_of(q0, DIL_QB), DIL_QB)
        ksl = pl.ds(pl.multiple_of(ws, A_RADIUS), DIL_KW)
    else:
        qsl = pl.ds(r + dil * q0, DIL_QB, stride=dil)
        ksl = pl.ds(r + dil * ws, DIL_KW, stride=dil)
    q = q_ref[qsl, :]
    kw = k_ref[ksl, :].astype(BF16)
    vw = v_ref[ksl, :].astype(BF16)
    lo = lax.broadcasted_iota(jnp.int32, q.shape, 1) < A_HEAD_DIM
    outs, lses = [], []
    for hh in range(2):
        qm = jnp.where(lo if hh == 0 else jnp.logical_not(lo), q, 0.0).astype(BF16)
        sc = lax.dot_general(qm, kw, (((1,), (1,)), ((), ())), preferred_element_type=F32)
        sc = sc + bias_ref[pi * 3 + var, hh]
        m = jnp.max(sc, axis=-1, keepdims=True)
        p = jnp.exp2(sc - m)
        l = jnp.sum(p, axis=-1, keepdims=True)
        o = jnp.dot(p.astype(BF16), vw, preferred_element_type=F32)
        outs.append(o / l)
        lses.append(m + jnp.log2(l))
    o_scr[pi, qsl, :] = jnp.where(lo, outs[0], outs[1])
    l_scr[pi, qsl, :] = jnp.where(lo, lses[0], lses[1])


def _dilated_kernel(q_ref, k_ref, v_ref, bias_ref, out_ref, o_scr, l_scr):
    s = q_ref.shape[0]
    njobs = s // DIL_QB
    for pi, (_, dil) in enumerate(A_PATTERNS):
        blk = functools.partial(_dil_block, q_ref, k_ref, v_ref, bias_ref, o_scr, l_scr, pi, dil,
                                s // dil // DIL_QB)

        def group(g, c, blk=blk):
            for u in range(DIL_UNROLL):
                blk(g * DIL_UNROLL + u)
            return c

        lax.fori_loop(0, njobs // DIL_UNROLL, group, 0)

    chunk = 512

    def comb(i, c):
        rows = pl.ds(pl.multiple_of(i * chunk, chunk), chunk)
        l0, l1, l2 = l_scr[0, rows, :], l_scr[1, rows, :], l_scr[2, rows, :]
        mx = jnp.maximum(jnp.maximum(l0, l1), l2)
        e0, e1, e2 = jnp.exp2(l0 - mx), jnp.exp2(l1 - mx), jnp.exp2(l2 - mx)
        num = e0 * o_scr[0, rows, :] + e1 * o_scr[1, rows, :] + e2 * o_scr[2, rows, :]
        out_ref[rows, :] = num / (e0 + e1 + e2)
        return c

    lax.fori_loop(0, s // chunk, comb, 0)


def _dilated(qa, ka, va, bias):
    b, s, _ = qa.shape
    npair = A_WIDTH // LANES
    assert (s // DIL_QB) % DIL_UNROLL == 0
    for _, dil in A_PATTERNS:
        assert (s // dil) % DIL_QB == 0 and s // dil >= DIL_KW
    blk = pl.BlockSpec((None, s, LANES), lambda bi, p: (bi, 0, p))
    return pl.pallas_call(
        _dilated_kernel,
        grid=(b, npair),
        in_specs=[blk, blk, blk,
                  pl.BlockSpec((bias.shape[0], 2, DIL_QB, DIL_KW), lambda bi, p: (0, p, 0, 0))],
        out_specs=blk,
        out_shape=jax.ShapeDtypeStruct((b, s, A_WIDTH), F32),
        scratch_shapes=[pltpu.VMEM((len(A_PATTERNS), s, LANES), F32),
                        pltpu.VMEM((len(A_PATTERNS), s, LANES), F32)],
        compiler_params=_cparams(("parallel", "parallel")),
        name="dilated",
    )(qa, ka, va, bias)


def _mla_kernel(q_ref, k_ref, v_ref, w1_ref, w3_ref, w2_ref, o_ref, w1b_ref, w3b_ref, w2b_ref):
    w1b_ref[...] = w1_ref[...].astype(BF16)
    w3b_ref[...] = w3_ref[...].astype(BF16)
    w2b_ref[...] = w2_ref[...].astype(BF16)
    tq = q_ref.shape[1]
    nkc = k_ref.shape[1] // MLA_KC
    outs = []
    for hh in range(2):
        q = q_ref[hh]
        m = jnp.full((tq, 1), -jnp.inf, F32)
        l = jnp.zeros((tq, 1), F32)
        acc = jnp.zeros((tq, LANES), F32)
        for c in range(nkc):
            keys = slice(c * MLA_KC, (c + 1) * MLA_KC)
            sc = lax.dot_general(q, k_ref[hh, keys, :], (((1,), (1,)), ((), ())), preferred_element_type=F32)
            m_new = jnp.maximum(m, jnp.max(sc, axis=-1, keepdims=True))
            alpha = jnp.exp2(m - m_new)
            p = jnp.exp2(sc - m_new)
            l = alpha * l + jnp.sum(p, axis=-1, keepdims=True)
            acc = alpha * acc + jnp.dot(p.astype(BF16), v_ref[keys, :], preferred_element_type=F32)
            m = m_new
        outs.append(acc / l)
    lo = lax.broadcasted_iota(jnp.int32, outs[0].shape, 1) < B_VDIM
    o_ref[...] = jnp.where(lo, outs[0], outs[1])


def _mla(qb, kb, vb, w1, w3, w2):
    b, h, s, _ = qb.shape
    npair = h // 2
    tq = MLA_TQ
    nq = s // tq
    nsteps = b * npair * nq
    ne, d, f = w1.shape
    assert ne % nsteps == 0
    epb = ne // nsteps
    wmap = lambda bi, p, qi: ((bi * npair + p) * nq + qi, 0, 0)
    return pl.pallas_call(
        _mla_kernel,
        grid=(b, npair, nq),
        in_specs=[pl.BlockSpec((None, 2, tq, HEAD_PAD), lambda bi, p, qi: (bi, p, qi, 0)),
                  pl.BlockSpec((None, 2, s, HEAD_PAD), lambda bi, p, qi: (bi, p, 0, 0)),
                  pl.BlockSpec((None, None, s, LANES), lambda bi, p, qi: (bi, p, 0, 0)),
                  pl.BlockSpec((epb, d, f), wmap),
                  pl.BlockSpec((epb, d, f), wmap),
                  pl.BlockSpec((epb, f, d), wmap)],
        out_specs=[pl.BlockSpec((None, tq, LANES), lambda bi, p, qi: (bi, qi, p)),
                   pl.BlockSpec((epb, d, f), wmap),
                   pl.BlockSpec((epb, d, f), wmap),
                   pl.BlockSpec((epb, f, d), wmap)],
        out_shape=[jax.ShapeDtypeStruct((b, s, B_WIDTH), F32),
                   jax.ShapeDtypeStruct((ne, d, f), BF16),
                   jax.ShapeDtypeStruct((ne, d, f), BF16),
                   jax.ShapeDtypeStruct((ne, f, d), BF16)],
        compiler_params=_cparams(("parallel", "parallel", "arbitrary")),
        name="mla",
    )(qb, kb, vb, w1, w3, w2)


def _outproj_kernel(x_ref, a_ref, b_ref, ag_ref, bg_ref, wo_ref, g1_ref, n2_ref, sc_ref, sh_ref, g2_ref,
                    wrt_ref, ws1_ref, ws3_ref, ws2_ref, x1s_ref, h2p_ref, scores_ref):
    an = _rms(a_ref[...], ag_ref[...])
    bn = _rms(b_ref[...], bg_ref[...])
    mix = jnp.concatenate([an, bn], axis=-1).astype(BF16)
    x1 = x_ref[...] + g1_ref[...] * jnp.dot(mix, wo_ref[...], preferred_element_type=F32)
    h2 = _rms(x1, n2_ref[...]) * (1.0 + sc_ref[...]) + sh_ref[...]
    packed = _pack_rows(h2)
    for cg in range(h2p_ref.shape[0]):
        h2p_ref[cg] = packed[:, cg * LANES:(cg + 1) * LANES]
    logits = lax.dot_general(wrt_ref[...], h2, (((1,), (1,)), ((), ())), precision=HIGHEST,
                             preferred_element_type=F32)
    scores_ref[...] = jax.nn.sigmoid(logits)
    h2b = h2.astype(BF16)
    hid = _silu(jnp.dot(h2b, ws1_ref[...], preferred_element_type=F32)) * jnp.dot(
        h2b, ws3_ref[...], preferred_element_type=F32)
    shared = jnp.dot(hid.astype(BF16), ws2_ref[...], preferred_element_type=F32)
    x1s_ref[...] = x1 + g2_ref[...] * shared


def _outproj(x2, a_out, b_out, a_out_g, b_out_g, w_o, mod3, norm2_g, w_router_t, ws1, ws3, ws2, s):
    n, d = x2.shape
    tm = TM_OUTPROJ
    tpb = s // tm
    row = lambda i: (i, 0)
    const = lambda i: (0, 0)
    modspec = lambda j: pl.BlockSpec((None, 1, d), lambda i: ((i // tpb) * 6 + j, 0, 0))
    return pl.pallas_call(
        _outproj_kernel,
        grid=(n // tm,),
        in_specs=[pl.BlockSpec((tm, d), row),
                  pl.BlockSpec((tm, A_WIDTH), row),
                  pl.BlockSpec((tm, B_WIDTH), row),
                  pl.BlockSpec((1, A_WIDTH), const),
                  pl.BlockSpec((1, B_WIDTH), const),
                  pl.BlockSpec(w_o.shape, const),
                  modspec(2),
                  pl.BlockSpec((1, d), const),
                  modspec(4), modspec(3), modspec(5),
                  pl.BlockSpec(w_router_t.shape, const),
                  pl.BlockSpec(ws1.shape, const),
                  pl.BlockSpec(ws3.shape, const),
                  pl.BlockSpec(ws2.shape, const)],
        out_specs=[pl.BlockSpec((tm, d), row),
                   pl.BlockSpec((d // 2 // LANES, tm, LANES), lambda i: (0, i, 0)),
                   pl.BlockSpec((N_EXPERTS, tm), lambda i: (0, i))],
        out_shape=[jax.ShapeDtypeStruct((n, d), F32),
                   jax.ShapeDtypeStruct((d // 2 // LANES, n, LANES), U32),
                   jax.ShapeDtypeStruct((N_EXPERTS, n), F32)],
        compiler_params=_cparams(("parallel",)),
        name="outproj",
    )(x2, a_out, b_out, a_out_g, b_out_g, w_o, mod3, norm2_g, mod3, mod3, mod3, w_router_t, ws1, ws3, ws2)


def _slab_order(v):
    return v.reshape((N_GROUPS, GROUP_SIZE) + v.shape[1:]).swapaxes(0, 1).reshape(v.shape)


def _expert_order(v):
    return v.reshape((GROUP_SIZE, N_GROUPS) + v.shape[1:]).swapaxes(0, 1).reshape(v.shape)


def _sublane_all(x, op):
    for sh in (4, 2, 1):
        x = op(x, pltpu.roll(x, sh, 0))
    return x


def _route_kernel(st_ref, bias_ref, eidx_ref, gate_ref, cnt_ref):
    nsl = GROUP_SIZE
    t = st_ref.shape[1]
    sub = lax.broadcasted_iota(I32, (SUBLANES, t), 0)
    ninf = -jnp.inf
    big = jnp.int32(1 << 30)
    sc = [st_ref[j * SUBLANES:(j + 1) * SUBLANES, :] for j in range(nsl)]
    sel = [sc[j] + bias_ref[j * SUBLANES:(j + 1) * SUBLANES, :] for j in range(nsl)]
    eid = [sub * GROUP_SIZE + j for j in range(nsl)]

    m1 = sel[0]
    m2 = jnp.full_like(m1, ninf)
    for j in range(1, nsl):
        m2 = jnp.maximum(m2, jnp.minimum(m1, sel[j]))
        m1 = jnp.maximum(m1, sel[j])
    gs = m1 + m2

    rank = jnp.zeros((SUBLANES, t), I32)
    for sh in range(1, N_GROUPS):
        other = pltpu.roll(gs, sh, 0)
        ahead = (other > gs) | ((other == gs) & (sub >= sh))
        rank = rank + ahead.astype(I32)
    gmask = rank < TOPK_GROUPS

    msel = [jnp.where(gmask, sel[j], ninf) for j in range(nsl)]
    hits = [jnp.zeros((SUBLANES, t), I32) for _ in range(nsl)]
    eidx = jnp.zeros((TOP_K, t), I32)
    gates = jnp.zeros((TOP_K, t), F32)
    for k in range(TOP_K):
        mx = msel[0]
        for j in range(1, nsl):
            mx = jnp.maximum(mx, msel[j])
        mx = _sublane_all(mx, jnp.maximum)
        cand = jnp.where(msel[0] == mx, eid[0], big)
        for j in range(1, nsl):
            cand = jnp.minimum(cand, jnp.where(msel[j] == mx, eid[j], big))
        idx = _sublane_all(cand, jnp.minimum)
        gk = jnp.zeros((SUBLANES, t), F32)
        for j in range(nsl):
            hit = eid[j] == idx
            gk = gk + jnp.where(hit, sc[j], 0.0)
            msel[j] = jnp.where(hit, ninf, msel[j])
            hits[j] = hits[j] + hit.astype(I32)
        gk = _sublane_all(gk, jnp.add)
        eidx = jnp.where(sub == k, idx, eidx)
        gates = jnp.where(sub == k, gk, gates)
    gsum = _sublane_all(gates, jnp.add)
    eidx_ref[...] = eidx
    gate_ref[...] = gates / gsum * ROUTED_SCALE

    @pl.when(pl.program_id(0) == 0)
    def _():
        cnt_ref[...] = jnp.zeros_like(cnt_ref)

    for j in range(nsl):
        cnt_ref[j * SUBLANES:(j + 1) * SUBLANES, :] += jnp.sum(hits[j].astype(F32), axis=1,
                                                               keepdims=True).astype(I32)


def _route(scores_t, e_bias_slab):
    e, n = scores_t.shape
    t = ROUTE_T
    return pl.pallas_call(
        _route_kernel,
        grid=(n // t,),
        in_specs=[pl.BlockSpec((e, t), lambda i: (0, i)),
                  pl.BlockSpec((e, 1), lambda i: (0, 0))],
        out_specs=[pl.BlockSpec((TOP_K, t), lambda i: (0, i)),
                   pl.BlockSpec((TOP_K, t), lambda i: (0, i)),
                   pl.BlockSpec((e, 1), lambda i: (0, 0))],
        out_shape=[jax.ShapeDtypeStruct((TOP_K, n), I32),
                   jax.ShapeDtypeStruct((TOP_K, n), F32),
                   jax.ShapeDtypeStruct((e, 1), I32)],
        compiler_params=_cparams(("arbitrary",)),
        name="route",
    )(scores_t, e_bias_slab)


def _dest_kernel(eidx_ref, pstart_ref, dest_ref, carry_ref):
    @pl.when(pl.program_id(0) == 0)
    def _():
        carry_ref[...] = jnp.zeros_like(carry_ref)

    nsl = GROUP_SIZE
    t = eidx_ref.shape[1]
    sub = lax.broadcasted_iota(I32, (SUBLANES, t), 0)
    eid = [sub * GROUP_SIZE + j for j in range(nsl)]
    ek = [eidx_ref[k:k + 1, :] for k in range(TOP_K)]
    slabs = []
    for j in range(nsl):
        oh = jnp.zeros((SUBLANES, t), F32)
        for k in range(TOP_K):
            oh = oh + (eid[j] == ek[k]).astype(F32)
        slabs.append(oh)
    onehot = jnp.concatenate(slabs, axis=0)
    row = lax.broadcasted_iota(I32, (t, t), 0)
    col = lax.broadcasted_iota(I32, (t, t), 1)
    upper = (row < col).astype(BF16)
    before = jnp.dot(onehot.astype(BF16), upper, preferred_element_type=F32)
    base = before + carry_ref[...] + pstart_ref[...]
    dest = jnp.zeros((TOP_K, t), I32)
    for k in range(TOP_K):
        acc = jnp.zeros((SUBLANES, t), F32)
        for j in range(nsl):
            acc = acc + jnp.where(eid[j] == ek[k], base[j * SUBLANES:(j + 1) * SUBLANES, :], 0.0)
        dk = _sublane_all(acc, jnp.add).astype(I32)
        dest = jnp.where(sub == k, dk, dest)
    dest_ref[...] = dest
    carry_ref[...] += jnp.sum(onehot, axis=1, keepdims=True)


def _dest(eidx_t, pstart_slab):
    n = eidx_t.shape[1]
    t = ROUTE_T
    return pl.pallas_call(
        _dest_kernel,
        grid=(n // t,),
        in_specs=[pl.BlockSpec((TOP_K, t), lambda i: (0, i)),
                  pl.BlockSpec((N_EXPERTS, 1), lambda i: (0, 0))],
        out_specs=pl.BlockSpec((TOP_K, t), lambda i: (0, i)),
        out_shape=jax.ShapeDtypeStruct((TOP_K, n), I32),
        scratch_shapes=[pltpu.VMEM((N_EXPERTS, 1), F32)],
        compiler_params=_cparams(("arbitrary",)),
        name="dest",
    )(eidx_t, pstart_slab)


def _sc_scatter(x, idx, rows_out, nplane, nslot):
    del nplane
    num = idx.shape[0]
    mesh = plsc.VectorSubcoreMesh(core_axis_name="core", subcore_axis_name="subcore")

    @pl.kernel(out_type=jax.ShapeDtypeStruct((rows_out, x.shape[1]), x.dtype), mesh=mesh, scratch_types=[])
    def scatter(x_hbm, i_hbm, o_hbm):
        def body(x_vmem, i_vmem):
            pltpu.sync_copy(x_vmem, o_hbm.at[i_vmem.at[0]])

        pltpu.emit_pipeline(
            body,
            grid=(num // SC_WINDOW,),
            in_specs=[pl.BlockSpec((SC_WINDOW, x.shape[1]), index_map=lambda i: (i // nslot, 0)),
                      pl.BlockSpec((1, SC_WINDOW), index_map=lambda i: (0, i))],
            out_specs=[],
            core_axis_name=("core", "subcore"),
            dimension_semantics=(pltpu.PARALLEL,),
        )(x_hbm, i_hbm)

    return scatter(x, idx.reshape(1, num))


def _expert_kernel(cstart_ref, nused_ref, valid_ref, w1_ref, w3_ref, w2_ref, xg_ref, y_ref,
                   xbuf, ybuf, xsem, ysem):
    e = pl.program_id(0)
    c0 = cstart_ref[e]
    c1 = cstart_ref[e + 1]
    nused = nused_ref[0]
    nchunks = xg_ref.shape[1] // MOE_BLK

    def rows(g):
        start = g * MOE_BLK
        return pl.ds(start if isinstance(g, int) else pl.multiple_of(start, MOE_BLK), MOE_BLK)

    nplane = xg_ref.shape[0]

    def xcopies(g):
        slot = g % EXP_XBUF
        return [pltpu.make_async_copy(xg_ref.at[pl.ds(0, nplane), rows(g)], xbuf.at[slot], xsem.at[slot])]

    def ycopies(g, slot):
        return [pltpu.make_async_copy(ybuf.at[slot], y_ref.at[pl.ds(0, nplane), rows(g)], ysem.at[slot])]

    @pl.when(e == 0)
    def _():
        for j in range(EXP_AHEAD):
            @pl.when(j < nused)
            def _():
                for cp in xcopies(j):
                    cp.start()

    @pl.when(c1 > c0)
    def _():
        def ffn(g):
            row_id = lax.broadcasted_iota(I32, (MOE_BLK, LANES), 0)
            keep = row_id < valid_ref[g]
            halves = [_unpack_rows(jnp.where(keep, xbuf[g % EXP_XBUF, c], jnp.uint32(0))) for c in range(nplane)]
            xb = jnp.concatenate([h[0] for h in halves] + [h[1] for h in halves], axis=1).astype(BF16)
            hid = _silu(jnp.dot(xb, w1_ref[...], preferred_element_type=F32)) * jnp.dot(
                xb, w3_ref[...], preferred_element_type=F32)
            return _pack_rows(jnp.dot(hid.astype(BF16), w2_ref[...], preferred_element_type=F32))

        def chunk(g, carry):
            @pl.when(g + EXP_AHEAD < nused)
            def _():
                for cp in xcopies(g + EXP_AHEAD):
                    cp.start()

            for cp in xcopies(g):
                cp.wait()
            yp = ffn(g)
            yslot = g % 2

            @pl.when(g >= 2)
            def _():
                for cp in ycopies(g - 2, yslot):
                    cp.wait()

            for c in range(nplane):
                ybuf[yslot, c] = yp[:, c * LANES:(c + 1) * LANES]
            for cp in ycopies(g, yslot):
                cp.start()
            return carry

        lax.fori_loop(c0, c1, chunk, 0)

    @pl.when(e == pl.num_programs(0) - 1)
    def _():
        for back in (2, 1):
            g = nused - back

            @pl.when(g >= 0)
            def _():
                for cp in ycopies(g, g % 2):
                    cp.wait()

        ybuf[0] = jnp.zeros(ybuf.shape[1:], ybuf.dtype)

        def zstart(g, c):
            for cp in ycopies(g, 0):
                cp.start()
            return c

        def zwait(g, c):
            for cp in ycopies(g, 0):
                cp.wait()
            return c

        lax.fori_loop(nused, nchunks, zstart, 0)
        lax.fori_loop(nused, nchunks, zwait, 0)


def _experts(cstart, nused, valid, xg, w1, w3, w2):
    nplane, rows, _ = xg.shape
    dh = nplane * LANES
    ne, d, f = w1.shape
    grid_spec = pltpu.PrefetchScalarGridSpec(
        num_scalar_prefetch=3,
        grid=(ne,),
        in_specs=[pl.BlockSpec((None, d, f), lambda e, cs, nu, va: (e, 0, 0)),
                  pl.BlockSpec((None, d, f), lambda e, cs, nu, va: (e, 0, 0)),
                  pl.BlockSpec((None, f, d), lambda e, cs, nu, va: (e, 0, 0)),
                  pl.BlockSpec(memory_space=pl.ANY)],
        out_specs=pl.BlockSpec(memory_space=pl.ANY),
        scratch_shapes=[pltpu.VMEM((EXP_XBUF, nplane, MOE_BLK, LANES), U32),
                        pltpu.VMEM((2, nplane, MOE_BLK, LANES), U32),
                        pltpu.SemaphoreType.DMA((EXP_XBUF,)), pltpu.SemaphoreType.DMA((2,))],
    )
    return pl.pallas_call(
        _expert_kernel,
        grid_spec=grid_spec,
        out_shape=jax.ShapeDtypeStruct((dh // LANES, rows, LANES), U32),
        compiler_params=_cparams(("arbitrary",)),
        name="experts",
    )(cstart, nused, valid, w1, w3, w2, xg)


def _sc_gather(x, idx):
    num = idx.shape[0]
    mesh = plsc.VectorSubcoreMesh(core_axis_name="core", subcore_axis_name="subcore")

    @pl.kernel(out_type=jax.ShapeDtypeStruct((num, x.shape[1]), x.dtype), mesh=mesh)
    def gather(x_hbm, i_hbm, o_hbm):
        def body(i_vmem, o_vmem):
            pltpu.sync_copy(x_hbm.at[i_vmem.at[0]], o_vmem)

        pltpu.emit_pipeline(
            body,
            grid=(num // SC_WINDOW,),
            in_specs=[pl.BlockSpec((1, SC_WINDOW), index_map=lambda i: (0, i))],
            out_specs=[pl.BlockSpec((SC_WINDOW, x.shape[1]), index_map=lambda i: (i, 0))],
            core_axis_name=("core", "subcore"),
            dimension_semantics=(pltpu.PARALLEL,),
        )(i_hbm, o_hbm)

    return gather(x, idx.reshape(1, num))


def _combine_kernel(yg_ref, gate_ref, x1s_ref, g2_ref, fg_ref, *rest):
    out_ref = rest[-1]
    gates = gate_ref[...]
    nch = yg_ref.shape[0]
    r_lo = [None] * nch
    r_hi = [None] * nch
    for k in range(TOP_K):
        gk = gates[:, k:k + 1]
        for c in range(nch):
            lo, hi = _unpack_rows(yg_ref[c, k])
            r_lo[c] = gk * lo if k == 0 else r_lo[c] + gk * lo
            r_hi[c] = gk * hi if k == 0 else r_hi[c] + gk * hi
    routed = jnp.concatenate(r_lo + r_hi, axis=1)
    x2 = x1s_ref[...] + g2_ref[...] * routed
    out_ref[...] = _rms(x2, fg_ref[...])


def _combine(yg, gates, x1s, mod3, final_g, s, part, prev_out):
    n, d = x1s.shape
    t = COMB_T
    tpb = s // t
    nch, _, npart, _ = yg.shape
    off = part * (npart // t)
    in_specs = [pl.BlockSpec((nch, TOP_K, t, LANES), lambda i: (0, 0, i, 0)),
                pl.BlockSpec((t, TOP_K), lambda i: (i + off, 0)),
                pl.BlockSpec((t, d), lambda i: (i + off, 0)),
                pl.BlockSpec((None, 1, d), lambda i: (((i + off) // tpb) * 6 + 5, 0, 0)),
                pl.BlockSpec((1, d), lambda i: (0, 0))]
    args = [yg, gates, x1s, mod3, final_g]
    aliases = {}
    if prev_out is not None:
        in_specs.append(pl.BlockSpec(memory_space=pl.ANY))
        args.append(prev_out)
        aliases = {len(args) - 1: 0}
    return pl.pallas_call(
        _combine_kernel,
        grid=(npart // t,),
        in_specs=in_specs,
        out_specs=pl.BlockSpec((t, d), lambda i: (i + off, 0)),
        out_shape=jax.ShapeDtypeStruct((n, d), F32),
        input_output_aliases=aliases,
        compiler_params=_cparams(("parallel",)),
        name="combine",
    )(*args)


def _place_heads(w, per_head, keep):
    r = w.shape[0]
    w = w.reshape(r, B_HEADS, per_head)[:, :, :keep]
    return jnp.pad(w, ((0, 0), (0, 0), (0, HEAD_PAD - keep))).reshape(r, B_HEADS * HEAD_PAD)


def _rope_tables(s):
    inv = ROPE_THETA ** (-jnp.arange(0, B_ROPE, 2, dtype=jnp.float32) / B_ROPE)
    ang = jnp.arange(s, dtype=jnp.float32)[:, None] * inv[None, :]
    cos, sin = jnp.cos(ang), jnp.sin(ang)
    ones = jnp.ones((s, B_NOPE), F32)
    zeros = jnp.zeros((s, B_NOPE), F32)
    tail1 = jnp.ones((s, HEAD_PAD - B_NOPE - B_ROPE), F32)
    tail0 = jnp.zeros((s, HEAD_PAD - B_NOPE - B_ROPE), F32)
    return (jnp.concatenate([ones, cos, cos, tail1], axis=1),
            jnp.concatenate([zeros, -sin, sin, tail0], axis=1))


def kernel(x, c, w_ada, b_ada, norm1_g, w_in, q_norm_g, w_uq, kv_norm_g, w_ukv, rel_table, a_out_g, b_out_g,
           w_o, norm2_g, w_router, e_bias, w1, w3, w2, ws1, ws3, ws2, final_g):
    b, s, d = x.shape
    n = b * s
    assert w_ada.shape[0] == 1, "single layer"
    x2 = x.reshape(n, d)

    mod = _ada(c, w_ada[0], b_ada[0])
    mod3 = mod.reshape(b * 6, 1, d)
    bias = _bias_tiles(rel_table)

    wi = w_in[0]
    c_kpe = 3 * A_WIDTH + Q_LORA + KV_LORA
    kpe_cols = jnp.pad(wi[:, c_kpe:], ((0, 0), (B_NOPE, HEAD_PAD - B_NOPE - B_ROPE)))
    w_in_ext = jnp.concatenate([wi[:, :c_kpe], kpe_cols], axis=1).astype(BF16)
    w_uq_p = _place_heads(w_uq[0], B_NOPE + B_ROPE, B_NOPE + B_ROPE).astype(BF16)
    w_uk_p = _place_heads(w_ukv[0], B_NOPE + B_VDIM, B_NOPE).astype(BF16)
    w_v = w_ukv[0].reshape(KV_LORA, B_HEADS, B_NOPE + B_VDIM)[:, :, B_NOPE:].reshape(KV_LORA, B_WIDTH).astype(BF16)
    cos_t, sin_t = _rope_tables(s)

    qa, ka, va, qb, kb, vb = _inproj(x2, mod3, norm1_g, w_in_ext, q_norm_g, w_uq_p, kv_norm_g, w_uk_p, w_v,
                                     cos_t, sin_t, b, s)
    a_out = _dilated(qa.reshape(b, s, A_WIDTH), ka.reshape(b, s, A_WIDTH), va.reshape(b, s, A_WIDTH), bias)
    b_out, w1_bf, w3_bf, w2_bf = _mla(qb, kb, vb, w1[0], w3[0], w2[0])

    w_router_t = _slab_order(w_router[0].T)
    x1s, h2p, scores_t = _outproj(x2, a_out.reshape(n, A_WIDTH), b_out.reshape(n, B_WIDTH), a_out_g, b_out_g,
                                  w_o[0].astype(BF16), mod3, norm2_g, w_router_t,
                                  ws1[0].astype(BF16), ws3[0].astype(BF16), ws2[0].astype(BF16), s)

    eidx_t, gates_t, counts_slab = _route(scores_t, _slab_order(e_bias.reshape(N_EXPERTS, 1)))

    counts = _expert_order(counts_slab)[:, 0]
    padded = (counts + MOE_BLK - 1) // MOE_BLK * MOE_BLK
    pends = jnp.cumsum(padded)
    pstart = pends - padded
    nk = n * TOP_K
    nblk = -(-(nk + N_EXPERTS * (MOE_BLK - 1)) // MOE_BLK)
    rows_total = nblk * MOE_BLK
    blk_row = jnp.arange(nblk, dtype=I32) * MOE_BLK
    blk_e = jnp.minimum(jnp.sum((pends[None, :] <= blk_row[:, None]).astype(I32), axis=1), N_EXPERTS - 1)
    seg_end = (pstart + counts)[blk_e]
    valid = jnp.clip(seg_end - blk_row, 0, MOE_BLK).astype(I32)
    nused = (pends[-1] // MOE_BLK).astype(I32).reshape(1)
    cstart = jnp.concatenate([jnp.zeros((1,), I32), (pends // MOE_BLK).astype(I32)])

    dest_t = _dest(eidx_t, _slab_order(pstart.astype(F32).reshape(N_EXPERTS, 1)))
    nplane = h2p.shape[0]
    gidx = (dest_t.reshape(1, nk) + (jnp.arange(nplane, dtype=I32) * rows_total)[:, None]).reshape(nplane * nk)
    nwin = n // SC_WINDOW
    sidx = dest_t.reshape(TOP_K, nwin, SC_WINDOW).transpose(1, 0, 2)[None] + (
        jnp.arange(nplane, dtype=I32) * rows_total)[:, None, None, None]
    xg = _sc_scatter(h2p.reshape(nplane * n, LANES), sidx.reshape(nplane * nk), nplane * rows_total, nplane, TOP_K)
    y = _experts(cstart, nused, valid, xg.reshape(nplane, rows_total, LANES), w1_bf, w3_bf, w2_bf)
    y_flat = y.reshape(nplane * rows_total, LANES)
    gidx3 = gidx.reshape(nplane, TOP_K, n)
    gates = gates_t.T
    npart = n // COMB_SPLIT
    out = None
    for part in range(COMB_SPLIT):
        pidx = gidx3[:, :, part * npart:(part + 1) * npart].reshape(nplane * TOP_K * npart)
        yg = _sc_gather(y_flat, pidx).reshape(nplane, TOP_K, npart, LANES)
        out = _combine(yg, gates, x1s, mod3, final_g.reshape(1, d), s, part, out)
    return out.reshape(b, s, d)
```

```python
import functools
import math

import jax
import jax.numpy as jnp
from jax import lax
from jax.experimental import pallas as pl
from jax.experimental.pallas import tpu as pltpu
from jax.experimental.pallas import tpu_sc as plsc

F32 = jnp.float32
BF16 = jnp.bfloat16
U32 = jnp.uint32
I32 = jnp.int32
HIGHEST = lax.Precision.HIGHEST

D_MODEL = 1024
A_HEADS = 8
A_HEAD_DIM = 64
A_WIDTH = A_HEADS * A_HEAD_DIM
A_PATTERNS = ((128, 1), (512, 4), (2048, 16))
A_RADIUS = 64
REL_BUCKETS = 32
REL_MAX_DIST = 1024
B_HEADS = 8
B_NOPE = 64
B_ROPE = 32
B_VDIM = 64
B_WIDTH = B_HEADS * B_VDIM
Q_LORA = 384
KV_LORA = 256
ROPE_THETA = 10000.0
N_EXPERTS = 256
TOP_K = 8
N_GROUPS = 8
GROUP_SIZE = N_EXPERTS // N_GROUPS
TOPK_GROUPS = 4
EXPERT_FF = 256
SHARED_FF = 256
ROUTED_SCALE = 2.5
EPS = 1e-6
NEG_INF = -1e30
LOG2E = math.log2(math.e)

LANES = 128
SUBLANES = 8
HEAD_PAD = 128
IN_COLS_EXT = 3 * A_WIDTH + Q_LORA + KV_LORA + HEAD_PAD

TM_INPROJ = 512
TM_OUTPROJ = 1024
DIL_QB = 128
DIL_KW = DIL_QB + 2 * A_RADIUS
DIL_UNROLL = 8
MLA_TQ = 1024
MLA_KC = 2048
ROUTE_T = 512
MOE_BLK = 512
EXP_AHEAD = 3
EXP_XBUF = EXP_AHEAD + 1
SC_WINDOW = 128
COMB_T = 256
COMB_SPLIT = 4
VMEM_LIMIT = 56 * 1024 * 1024


def _cparams(sem):
    return pltpu.CompilerParams(dimension_semantics=sem, vmem_limit_bytes=VMEM_LIMIT)


def _rms(x, g):
    return x * lax.rsqrt(jnp.mean(x * x, axis=-1, keepdims=True) + EPS) * g


def _silu(x):
    return x * jax.nn.sigmoid(x)


def _pack_rows(x):
    half = x.shape[1] // 2
    bits = lax.bitcast_convert_type(x.astype(BF16).astype(F32), U32)
    return (bits[:, :half] >> 16) | bits[:, half:]


def _unpack_rows(w):
    lo = lax.bitcast_convert_type(w << 16, F32)
    hi = lax.bitcast_convert_type(w & jnp.uint32(0xFFFF0000), F32)
    return lo, hi


def _ada_kernel(c_ref, w_ref, b_ref, o_ref):
    o_ref[...] = jnp.dot(_silu(c_ref[...]), w_ref[...], precision=HIGHEST,
                         preferred_element_type=F32) + b_ref[...]


def _ada(c, w_ada, b_ada):
    b, d = c.shape
    n6 = w_ada.shape[1] // d
    return pl.pallas_call(
        _ada_kernel,
        grid=(n6,),
        in_specs=[pl.BlockSpec((b, d), lambda j: (0, 0)),
                  pl.BlockSpec((d, d), lambda j: (0, j)),
                  pl.BlockSpec((1, d), lambda j: (0, j))],
        out_specs=pl.BlockSpec((b, d), lambda j: (0, j)),
        out_shape=jax.ShapeDtypeStruct((b, n6 * d), F32),
        compiler_params=_cparams(("parallel",)),
        name="ada",
    )(c, w_ada, b_ada.reshape(1, -1))


def _t5_bucket(rel):
    half = REL_BUCKETS // 2
    max_exact = half // 2
    ret = jnp.where(rel > 0, half, 0)
    n = jnp.abs(rel)
    nf = jnp.maximum(n, 1).astype(jnp.float32)
    large = max_exact + (jnp.log(nf / max_exact) / math.log(REL_MAX_DIST / max_exact)
                         * (half - max_exact)).astype(jnp.int32)
    large = jnp.minimum(large, half - 1)
    return ret + jnp.where(n < max_exact, n, large)


DIL_SHIFTS = (A_RADIUS, 0, -A_RADIUS)


def _bucket_tiles():
    qi = jnp.arange(DIL_QB, dtype=jnp.int32)[:, None]
    ki = jnp.arange(DIL_KW, dtype=jnp.int32)[None, :]
    tiles = []
    for _, dilation in A_PATTERNS:
        for shift in DIL_SHIFTS:
            off = ki + shift - A_RADIUS - qi
            bkt = _t5_bucket(off * dilation)
            tiles.append(jnp.where(jnp.abs(off) <= A_RADIUS, bkt, -1))
    return jnp.stack(tiles, axis=0)


def _bias_kernel(tab_ref, bkt_ref, o_ref):
    bkt = bkt_ref[0]
    for h in range(A_HEADS):
        acc = jnp.full(bkt.shape, NEG_INF, F32)
        for b in range(REL_BUCKETS):
            acc = jnp.where(bkt == b, tab_ref[b, h] * LOG2E, acc)
        o_ref[0, h] = acc


def _bias_tiles(rel_table):
    bkt = _bucket_tiles()
    nt = bkt.shape[0]
    return pl.pallas_call(
        _bias_kernel,
        grid=(nt,),
        in_specs=[pl.BlockSpec(memory_space=pltpu.SMEM),
                  pl.BlockSpec((1, DIL_QB, DIL_KW), lambda t: (t, 0, 0))],
        out_specs=pl.BlockSpec((1, A_HEADS, DIL_QB, DIL_KW), lambda t: (t, 0, 0, 0)),
        out_shape=jax.ShapeDtypeStruct((nt, A_HEADS, DIL_QB, DIL_KW), F32),
        compiler_params=_cparams(("parallel",)),
        name="bias",
    )(rel_table, bkt)


def _rope(x, cos, sin, lane_lt_mid):
    half = B_ROPE // 2
    rot = jnp.where(lane_lt_mid, pltpu.roll(x, HEAD_PAD - half, 1), pltpu.roll(x, half, 1))
    return x * cos + rot * sin


def _inproj_kernel(x_ref, sc_ref, sh_ref, g1_ref, win_ref, qg_ref, wuq_ref, kvg_ref, wuk_ref, wv_ref,
                   cos_ref, sin_ref, qa_ref, ka_ref, va_ref, qb_ref, kb_ref, vb_ref):
    x = x_ref[...]
    h = _rms(x, g1_ref[...]) * (1.0 + sc_ref[...]) + sh_ref[...]
    proj = jnp.dot(h.astype(BF16), win_ref[...], preferred_element_type=F32)
    aw = A_WIDTH
    qa_ref[...] = proj[:, 0:aw] * (LOG2E / math.sqrt(A_HEAD_DIM))
    ka_ref[...] = proj[:, aw:2 * aw]
    va_ref[...] = proj[:, 2 * aw:3 * aw]
    c0 = 3 * aw
    q_lat = proj[:, c0:c0 + Q_LORA]
    kv_lat = proj[:, c0 + Q_LORA:c0 + Q_LORA + KV_LORA]
    kpe = proj[:, c0 + Q_LORA + KV_LORA:]
    qn = _rms(q_lat, qg_ref[...]).astype(BF16)
    kvn = _rms(kv_lat, kvg_ref[...]).astype(BF16)
    qm = jnp.dot(qn, wuq_ref[...], preferred_element_type=F32)
    kn = jnp.dot(kvn, wuk_ref[...], preferred_element_type=F32)
    vv = jnp.dot(kvn, wv_ref[...], preferred_element_type=F32)
    cos = cos_ref[...]
    sin = sin_ref[...]
    lane = lax.broadcasted_iota(jnp.int32, cos.shape, 1)
    lt_mid = lane < (B_NOPE + B_ROPE // 2)
    qscale = LOG2E / math.sqrt(B_NOPE + B_ROPE)
    for hd in range(B_HEADS):
        sl = slice(hd * HEAD_PAD, (hd + 1) * HEAD_PAD)
        qb_ref[hd] = (_rope(qm[:, sl], cos, sin, lt_mid) * qscale).astype(BF16)
        kb_ref[hd] = _rope(kn[:, sl] + kpe, cos, sin, lt_mid).astype(BF16)
    for p in range(B_HEADS // 2):
        vb_ref[p] = vv[:, p * LANES:(p + 1) * LANES].astype(BF16)


def _inproj(x2, mod3, norm1_g, w_in_ext, q_norm_g, w_uq_p, kv_norm_g, w_uk_p, w_v, cos_t, sin_t, b, s):
    n, d = x2.shape
    tm = TM_INPROJ
    tpb = s // tm
    row = lambda i: (i, 0)
    const = lambda i: (0, 0)
    hm = lambda i: (i // tpb, 0, i % tpb, 0)
    return pl.pallas_call(
        _inproj_kernel,
        grid=(n // tm,),
        in_specs=[pl.BlockSpec((tm, d), row),
                  pl.BlockSpec((None, 1, d), lambda i: ((i // tpb) * 6 + 1, 0, 0)),
                  pl.BlockSpec((None, 1, d), lambda i: ((i // tpb) * 6 + 0, 0, 0)),
                  pl.BlockSpec((1, d), const),
                  pl.BlockSpec(w_in_ext.shape, const),
                  pl.BlockSpec((1, Q_LORA), const),
                  pl.BlockSpec(w_uq_p.shape, const),
                  pl.BlockSpec((1, KV_LORA), const),
                  pl.BlockSpec(w_uk_p.shape, const),
                  pl.BlockSpec(w_v.shape, const),
                  pl.BlockSpec((tm, HEAD_PAD), lambda i: (i % tpb, 0)),
                  pl.BlockSpec((tm, HEAD_PAD), lambda i: (i % tpb, 0))],
        out_specs=[pl.BlockSpec((tm, A_WIDTH), row),
                   pl.BlockSpec((tm, A_WIDTH), row),
                   pl.BlockSpec((tm, A_WIDTH), row),
                   pl.BlockSpec((None, B_HEADS, tm, HEAD_PAD), hm),
                   pl.BlockSpec((None, B_HEADS, tm, HEAD_PAD), hm),
                   pl.BlockSpec((None, B_HEADS // 2, tm, LANES), hm)],
        out_shape=[jax.ShapeDtypeStruct((n, A_WIDTH), F32),
                   jax.ShapeDtypeStruct((n, A_WIDTH), F32),
                   jax.ShapeDtypeStruct((n, A_WIDTH), F32),
                   jax.ShapeDtypeStruct((b, B_HEADS, s, HEAD_PAD), BF16),
                   jax.ShapeDtypeStruct((b, B_HEADS, s, HEAD_PAD), BF16),
                   jax.ShapeDtypeStruct((b, B_HEADS // 2, s, LANES), BF16)],
        compiler_params=_cparams(("parallel",)),
        name="inproj",
    )(x2, mod3, mod3, norm1_g, w_in_ext, q_norm_g, w_uq_p, kv_norm_g, w_uk_p, w_v, cos_t, sin_t)


def _dil_block(q_ref, k_ref, v_ref, bias_ref, o_scr, l_scr, pi, dil, nblk, job):
    sub_len = nblk * DIL_QB
    r = job // nblk
    bi = job % nblk
    q0 = bi * DIL_QB
    ws = jnp.clip(q0 - A_RADIUS, 0, sub_len - DIL_KW)
    var = jnp.where(bi == 0, 0, jnp.where(bi == nblk - 1, 2, 1))
    if dil == 1:
        qsl = pl.ds(pl.multiple_of(q0, DIL_QB), DIL_QB)
        ksl = pl.ds(pl.multiple_of(ws, A_RADIUS), DIL_KW)
    else:
        qsl = pl.ds(r + dil * q0, DIL_QB, stride=dil)
        ksl = pl.ds(r + dil * ws, DIL_KW, stride=dil)
    q = q_ref[qsl, :]
    kw = k_ref[ksl, :].astype(BF16)
    vw = v_ref[ksl, :].astype(BF16)
    lo = lax.broadcasted_iota(jnp.int32, q.shape, 1) < A_HEAD_DIM
    outs, lses = [], []
    for hh in range(2):
        qm = jnp.where(lo if hh == 0 else jnp.logical_not(lo), q, 0.0).astype(BF16)
        sc = lax.dot_general(qm, kw, (((1,), (1,)), ((), ())), preferred_element_type=F32)
        sc = sc + bias_ref[pi * 3 + var, hh]
        m = jnp.max(sc, axis=-1, keepdims=True)
        p = jnp.exp2(sc - m)
        l = jnp.sum(p, axis=-1, keepdims=True)
        o = jnp.dot(p.astype(BF16), vw, preferred_element_type=F32)
        outs.append(o / l)
        lses.append(m + jnp.log2(l))
    o_scr[pi, qsl, :] = jnp.where(lo, outs[0], outs[1])
    l_scr[pi, qsl, :] = jnp.where(lo, lses[0], lses[1])


def _dilated_kernel(q_ref, k_ref, v_ref, bias_ref, out_ref, o_scr, l_scr):
    s = q_ref.shape[0]
    njobs = s // DIL_QB
    for pi, (_, dil) in enumerate(A_PATTERNS):
        blk = functools.partial(_dil_block, q_ref, k_ref, v_ref, bias_ref, o_scr, l_scr, pi, dil,
                                s // dil // DIL_QB)

        def group(g, c, blk=blk):
            for u in range(DIL_UNROLL):
                blk(g * DIL_UNROLL + u)
            return c

        lax.fori_loop(0, njobs // DIL_UNROLL, group, 0)

    chunk = 512

    def comb(i, c):
        rows = pl.ds(pl.multiple_of(i * chunk, chunk), chunk)
        l0, l1, l2 = l_scr[0, rows, :], l_scr[1, rows, :], l_scr[2, rows, :]
        mx = jnp.maximum(jnp.maximum(l0, l1), l2)
        e0, e1, e2 = jnp.exp2(l0 - mx), jnp.exp2(l1 - mx), jnp.exp2(l2 - mx)
        num = e0 * o_scr[0, rows, :] + e1 * o_scr[1, rows, :] + e2 * o_scr[2, rows, :]
        out_ref[rows, :] = num / (e0 + e1 + e2)
        return c

    lax.fori_loop(0, s // chunk, comb, 0)


def _dilated(qa, ka, va, bias):
    b, s, _ = qa.shape
    npair = A_WIDTH // LANES
    assert (s // DIL_QB) % DIL_UNROLL == 0
    for _, dil in A_PATTERNS:
        assert (s // dil) % DIL_QB == 0 and s // dil >= DIL_KW
    blk = pl.BlockSpec((None, s, LANES), lambda bi, p: (bi, 0, p))
    return pl.pallas_call(
        _dilated_kernel,
        grid=(b, npair),
        in_specs=[blk, blk, blk,
                  pl.BlockSpec((bias.shape[0], 2, DIL_QB, DIL_KW), lambda bi, p: (0, p, 0, 0))],
        out_specs=blk,
        out_shape=jax.ShapeDtypeStruct((b, s, A_WIDTH), F32),
        scratch_shapes=[pltpu.VMEM((len(A_PATTERNS), s, LANES), F32),
                        pltpu.VMEM((len(A_PATTERNS), s, LANES), F32)],
        compiler_params=_cparams(("parallel", "parallel")),
        name="dilated",
    )(qa, ka, va, bias)


def _mla_kernel(q_ref, k_ref, v_ref, w1_ref, w3_ref, w2_ref, o_ref, w1b_ref, w3b_ref, w2b_ref):
    w1b_ref[...] = w1_ref[...].astype(BF16)
    w3b_ref[...] = w3_ref[...].astype(BF16)
    w2b_ref[...] = w2_ref[...].astype(BF16)
    tq = q_ref.shape[1]
    nkc = k_ref.shape[1] // MLA_KC
    outs = []
    for hh in range(2):
        q = q_ref[hh]
        m = jnp.full((tq, 1), -jnp.inf, F32)
        l = jnp.zeros((tq, 1), F32)
        acc = jnp.zeros((tq, LANES), F32)
        for c in range(nkc):
            keys = slice(c * MLA_KC, (c + 1) * MLA_KC)
            sc = lax.dot_general(q, k_ref[hh, keys, :], (((1,), (1,)), ((), ())), preferred_element_type=F32)
            m_new = jnp.maximum(m, jnp.max(sc, axis=-1, keepdims=True))
            alpha = jnp.exp2(m - m_new)
            p = jnp.exp2(sc - m_new)
            l = alpha * l + jnp.sum(p, axis=-1, keepdims=True)
            acc = alpha * acc + jnp.dot(p.astype(BF16), v_ref[keys, :], preferred_element_type=F32)
            m = m_new
        outs.append(acc / l)
    lo = lax.broadcasted_iota(jnp.int32, outs[0].shape, 1) < B_VDIM
    o_ref[...] = jnp.where(lo, outs[0], outs[1])


def _mla(qb, kb, vb, w1, w3, w2):
    b, h, s, _ = qb.shape
    npair = h // 2
    tq = MLA_TQ
    nq = s // tq
    nsteps = b * npair * nq
    ne, d, f = w1.shape
    assert ne % nsteps == 0
    epb = ne // nsteps
    wmap = lambda bi, p, qi: ((bi * npair + p) * nq + qi, 0, 0)
    return pl.pallas_call(
        _mla_kernel,
        grid=(b, npair, nq),
        in_specs=[pl.BlockSpec((None, 2, tq, HEAD_PAD), lambda bi, p, qi: (bi, p, qi, 0)),
                  pl.BlockSpec((None, 2, s, HEAD_PAD), lambda bi, p, qi: (bi, p, 0, 0)),
                  pl.BlockSpec((None, None, s, LANES), lambda bi, p, qi: (bi, p, 0, 0)),
                  pl.BlockSpec((epb, d, f), wmap),
                  pl.BlockSpec((epb, d, f), wmap),
                  pl.BlockSpec((epb, f, d), wmap)],
        out_specs=[pl.BlockSpec((None, tq, LANES), lambda bi, p, qi: (bi, qi, p)),
                   pl.BlockSpec((epb, d, f), wmap),
                   pl.BlockSpec((epb, d, f), wmap),
                   pl.BlockSpec((epb, f, d), wmap)],
        out_shape=[jax.ShapeDtypeStruct((b, s, B_WIDTH), F32),
                   jax.ShapeDtypeStruct((ne, d, f), BF16),
                   jax.ShapeDtypeStruct((ne, d, f), BF16),
                   jax.ShapeDtypeStruct((ne, f, d), BF16)],
        compiler_params=_cparams(("parallel", "parallel", "arbitrary")),
        name="mla",
    )(qb, kb, vb, w1, w3, w2)


def _outproj_kernel(x_ref, a_ref, b_ref, ag_ref, bg_ref, wo_ref, g1_ref, n2_ref, sc_ref, sh_ref, g2_ref,
                    wrt_ref, ws1_ref, ws3_ref, ws2_ref, x1s_ref, h2p_ref, scores_ref):
    an = _rms(a_ref[...], ag_ref[...])
    bn = _rms(b_ref[...], bg_ref[...])
    mix = jnp.concatenate([an, bn], axis=-1).astype(BF16)
    x1 = x_ref[...] + g1_ref[...] * jnp.dot(mix, wo_ref[...], preferred_element_type=F32)
    h2 = _rms(x1, n2_ref[...]) * (1.0 + sc_ref[...]) + sh_ref[...]
    packed = _pack_rows(h2)
    for cg in range(h2p_ref.shape[0]):
        h2p_ref[cg] = packed[:, cg * LANES:(cg + 1) * LANES]
    logits = lax.dot_general(wrt_ref[...], h2, (((1,), (1,)), ((), ())), precision=HIGHEST,
                             preferred_element_type=F32)
    scores_ref[...] = jax.nn.sigmoid(logits)
    h2b = h2.astype(BF16)
    hid = _silu(jnp.dot(h2b, ws1_ref[...], preferred_element_type=F32)) * jnp.dot(
        h2b, ws3_ref[...], preferred_element_type=F32)
    shared = jnp.dot(hid.astype(BF16), ws2_ref[...], preferred_element_type=F32)
    x1s_ref[...] = x1 + g2_ref[...] * shared


def _outproj(x2, a_out, b_out, a_out_g, b_out_g, w_o, mod3, norm2_g, w_router_t, ws1, ws3, ws2, s):
    n, d = x2.shape
    tm = TM_OUTPROJ
    tpb = s // tm
    row = lambda i: (i, 0)
    const = lambda i: (0, 0)
    modspec = lambda j: pl.BlockSpec((None, 1, d), lambda i: ((i // tpb) * 6 + j, 0, 0))
    return pl.pallas_call(
        _outproj_kernel,
        grid=(n // tm,),
        in_specs=[pl.BlockSpec((tm, d), row),
                  pl.BlockSpec((tm, A_WIDTH), row),
                  pl.BlockSpec((tm, B_WIDTH), row),
                  pl.BlockSpec((1, A_WIDTH), const),
                  pl.BlockSpec((1, B_WIDTH), const),
                  pl.BlockSpec(w_o.shape, const),
                  modspec(2),
                  pl.BlockSpec((1, d), const),
                  modspec(4), modspec(3), modspec(5),
                  pl.BlockSpec(w_router_t.shape, const),
                  pl.BlockSpec(ws1.shape, const),
                  pl.BlockSpec(ws3.shape, const),
                  pl.BlockSpec(ws2.shape, const)],
        out_specs=[pl.BlockSpec((tm, d), row),
                   pl.BlockSpec((d // 2 // LANES, tm, LANES), lambda i: (0, i, 0)),
                   pl.BlockSpec((N_EXPERTS, tm), lambda i: (0, i))],
        out_shape=[jax.ShapeDtypeStruct((n, d), F32),
                   jax.ShapeDtypeStruct((d // 2 // LANES, n, LANES), U32),
                   jax.ShapeDtypeStruct((N_EXPERTS, n), F32)],
        compiler_params=_cparams(("parallel",)),
        name="outproj",
    )(x2, a_out, b_out, a_out_g, b_out_g, w_o, mod3, norm2_g, mod3, mod3, mod3, w_router_t, ws1, ws3, ws2)


def _slab_order(v):
    return v.reshape((N_GROUPS, GROUP_SIZE) + v.shape[1:]).swapaxes(0, 1).reshape(v.shape)


def _expert_order(v):
    return v.reshape((GROUP_SIZE, N_GROUPS) + v.shape[1:]).swapaxes(0, 1).reshape(v.shape)


def _sublane_all(x, op):
    for sh in (4, 2, 1):
        x = op(x, pltpu.roll(x, sh, 0))
    return x


def _route_kernel(st_ref, bias_ref, eidx_ref, gate_ref, cnt_ref):
    nsl = GROUP_SIZE
    t = st_ref.shape[1]
    sub = lax.broadcasted_iota(I32, (SUBLANES, t), 0)
    ninf = -jnp.inf
    big = jnp.int32(1 << 30)
    sc = [st_ref[j * SUBLANES:(j + 1) * SUBLANES, :] for j in range(nsl)]
    sel = [sc[j] + bias_ref[j * SUBLANES:(j + 1) * SUBLANES, :] for j in range(nsl)]
    eid = [sub * GROUP_SIZE + j for j in range(nsl)]

    m1 = sel[0]
    m2 = jnp.full_like(m1, ninf)
    for j in range(1, nsl):
        m2 = jnp.maximum(m2, jnp.minimum(m1, sel[j]))
        m1 = jnp.maximum(m1, sel[j])
    gs = m1 + m2

    rank = jnp.zeros((SUBLANES, t), I32)
    for sh in range(1, N_GROUPS):
        other = pltpu.roll(gs, sh, 0)
        ahead = (other > gs) | ((other == gs) & (sub >= sh))
        rank = rank + ahead.astype(I32)
    gmask = rank < TOPK_GROUPS

    msel = [jnp.where(gmask, sel[j], ninf) for j in range(nsl)]
    hits = [jnp.zeros((SUBLANES, t), I32) for _ in range(nsl)]
    eidx = jnp.zeros((TOP_K, t), I32)
    gates = jnp.zeros((TOP_K, t), F32)
    for k in range(TOP_K):
        mx = msel[0]
        for j in range(1, nsl):
            mx = jnp.maximum(mx, msel[j])
        mx = _sublane_all(mx, jnp.maximum)
        cand = jnp.where(msel[0] == mx, eid[0], big)
        for j in range(1, nsl):
            cand = jnp.minimum(cand, jnp.where(msel[j] == mx, eid[j], big))
        idx = _sublane_all(cand, jnp.minimum)
        gk = jnp.zeros((SUBLANES, t), F32)
        for j in range(nsl):
            hit = eid[j] == idx
            gk = gk + jnp.where(hit, sc[j], 0.0)
            msel[j] = jnp.where(hit, ninf, msel[j])
            hits[j] = hits[j] + hit.astype(I32)
        gk = _sublane_all(gk, jnp.add)
        eidx = jnp.where(sub == k, idx, eidx)
        gates = jnp.where(sub == k, gk, gates)
    gsum = _sublane_all(gates, jnp.add)
    eidx_ref[...] = eidx
    gate_ref[...] = gates / gsum * ROUTED_SCALE

    @pl.when(pl.program_id(0) == 0)
    def _():
        cnt_ref[...] = jnp.zeros_like(cnt_ref)

    for j in range(nsl):
        cnt_ref[j * SUBLANES:(j + 1) * SUBLANES, :] += jnp.sum(hits[j].astype(F32), axis=1,
                                                               keepdims=True).astype(I32)


def _route(scores_t, e_bias_slab):
    e, n = scores_t.shape
    t = ROUTE_T
    return pl.pallas_call(
        _route_kernel,
        grid=(n // t,),
        in_specs=[pl.BlockSpec((e, t), lambda i: (0, i)),
                  pl.BlockSpec((e, 1), lambda i: (0, 0))],
        out_specs=[pl.BlockSpec((TOP_K, t), lambda i: (0, i)),
                   pl.BlockSpec((TOP_K, t), lambda i: (0, i)),
                   pl.BlockSpec((e, 1), lambda i: (0, 0))],
        out_shape=[jax.ShapeDtypeStruct((TOP_K, n), I32),
                   jax.ShapeDtypeStruct((TOP_K, n), F32),
                   jax.ShapeDtypeStruct((e, 1), I32)],
        compiler_params=_cparams(("arbitrary",)),
        name="route",
    )(scores_t, e_bias_slab)


def _dest_kernel(eidx_ref, pstart_ref, dest_ref, carry_ref):
    @pl.when(pl.program_id(0) == 0)
    def _():
        carry_ref[...] = jnp.zeros_like(carry_ref)

    nsl = GROUP_SIZE
    t = eidx_ref.shape[1]
    sub = lax.broadcasted_iota(I32, (SUBLANES, t), 0)
    eid = [sub * GROUP_SIZE + j for j in range(nsl)]
    ek = [eidx_ref[k:k + 1, :] for k in range(TOP_K)]
    slabs = []
    for j in range(nsl):
        oh = jnp.zeros((SUBLANES, t), F32)
        for k in range(TOP_K):
            oh = oh + (eid[j] == ek[k]).astype(F32)
        slabs.append(oh)
    onehot = jnp.concatenate(slabs, axis=0)
    row = lax.broadcasted_iota(I32, (t, t), 0)
    col = lax.broadcasted_iota(I32, (t, t), 1)
    upper = (row < col).astype(BF16)
    before = jnp.dot(onehot.astype(BF16), upper, preferred_element_type=F32)
    base = before + carry_ref[...] + pstart_ref[...]
    dest = jnp.zeros((TOP_K, t), I32)
    for k in range(TOP_K):
        acc = jnp.zeros((SUBLANES, t), F32)
        for j in range(nsl):
            acc = acc + jnp.where(eid[j] == ek[k], base[j * SUBLANES:(j + 1) * SUBLANES, :], 0.0)
        dk = _sublane_all(acc, jnp.add).astype(I32)
        dest = jnp.where(sub == k, dk, dest)
    dest_ref[...] = dest
    carry_ref[...] += jnp.sum(onehot, axis=1, keepdims=True)


def _dest(eidx_t, pstart_slab):
    n = eidx_t.shape[1]
    t = ROUTE_T
    return pl.pallas_call(
        _dest_kernel,
        grid=(n // t,),
        in_specs=[pl.BlockSpec((TOP_K, t), lambda i: (0, i)),
                  pl.BlockSpec((N_EXPERTS, 1), lambda i: (0, 0))],
        out_specs=pl.BlockSpec((TOP_K, t), lambda i: (0, i)),
        out_shape=jax.ShapeDtypeStruct((TOP_K, n), I32),
        scratch_shapes=[pltpu.VMEM((N_EXPERTS, 1), F32)],
        compiler_params=_cparams(("arbitrary",)),
        name="dest",
    )(eidx_t, pstart_slab)


def _sc_scatter(x, idx, rows_out, nplane, nslot):
    del nplane
    num = idx.shape[0]
    mesh = plsc.VectorSubcoreMesh(core_axis_name="core", subcore_axis_name="subcore")

    @pl.kernel(out_type=jax.ShapeDtypeStruct((rows_out, x.shape[1]), x.dtype), mesh=mesh, scratch_types=[])
    def scatter(x_hbm, i_hbm, o_hbm):
        def body(x_vmem, i_vmem):
            pltpu.sync_copy(x_vmem, o_hbm.at[i_vmem.at[0]])

        pltpu.emit_pipeline(
            body,
            grid=(num // SC_WINDOW,),
            in_specs=[pl.BlockSpec((SC_WINDOW, x.shape[1]), index_map=lambda i: (i // nslot, 0)),
                      pl.BlockSpec((1, SC_WINDOW), index_map=lambda i: (0, i))],
            out_specs=[],
            core_axis_name=("core", "subcore"),
            dimension_semantics=(pltpu.PARALLEL,),
        )(x_hbm, i_hbm)

    return scatter(x, idx.reshape(1, num))


def _expert_kernel(cstart_ref, nused_ref, valid_ref, w1_ref, w3_ref, w2_ref, xg_ref, y_ref,
                   xbuf, ybuf, xsem, ysem):
    e = pl.program_id(0)
    c0 = cstart_ref[e]
    c1 = cstart_ref[e + 1]
    nused = nused_ref[0]
    nchunks = xg_ref.shape[1] // MOE_BLK

    def rows(g):
        start = g * MOE_BLK
        return pl.ds(start if isinstance(g, int) else pl.multiple_of(start, MOE_BLK), MOE_BLK)

    nplane = xg_ref.shape[0]

    def xcopies(g):
        slot = g % EXP_XBUF
        return [pltpu.make_async_copy(xg_ref.at[pl.ds(0, nplane), rows(g)], xbuf.at[slot], xsem.at[slot])]

    def ycopies(g, slot):
        return [pltpu.make_async_copy(ybuf.at[slot], y_ref.at[pl.ds(0, nplane), rows(g)], ysem.at[slot])]

    @pl.when(e == 0)
    def _():
        for j in range(EXP_AHEAD):
            @pl.when(j < nused)
            def _():
                for cp in xcopies(j):
                    cp.start()

    @pl.when(c1 > c0)
    def _():
        def ffn(g):
            row_id = lax.broadcasted_iota(I32, (MOE_BLK, LANES), 0)
            keep = row_id < valid_ref[g]
            halves = [_unpack_rows(jnp.where(keep, xbuf[g % EXP_XBUF, c], jnp.uint32(0))) for c in range(nplane)]
            xb = jnp.concatenate([h[0] for h in halves] + [h[1] for h in halves], axis=1).astype(BF16)
            hid = _silu(jnp.dot(xb, w1_ref[...], preferred_element_type=F32)) * jnp.dot(
                xb, w3_ref[...], preferred_element_type=F32)
            return _pack_rows(jnp.dot(hid.astype(BF16), w2_ref[...], preferred_element_type=F32))

        def chunk(g, carry):
            @pl.when(g + EXP_AHEAD < nused)
            def _():
                for cp in xcopies(g + EXP_AHEAD):
                    cp.start()

            for cp in xcopies(g):
                cp.wait()
            yp = ffn(g)
            yslot = g % 2

            @pl.when(g >= 2)
            def _():
                for cp in ycopies(g - 2, yslot):
                    cp.wait()

            for c in range(nplane):
                ybuf[yslot, c] = yp[:, c * LANES:(c + 1) * LANES]
            for cp in ycopies(g, yslot):
                cp.start()
            return carry

        lax.fori_loop(c0, c1, chunk, 0)

    @pl.when(e == pl.num_programs(0) - 1)
    def _():
        for back in (2, 1):
            g = nused - back

            @pl.when(g >= 0)
            def _():
                for cp in ycopies(g, g % 2):
                    cp.wait()

        ybuf[0] = jnp.zeros(ybuf.shape[1:], ybuf.dtype)

        def zstart(g, c):
            for cp in ycopies(g, 0):
                cp.start()
            return c

        def zwait(g, c):
            for cp in ycopies(g, 0):
                cp.wait()
            return c

        lax.fori_loop(nused, nchunks, zstart, 0)
        lax.fori_loop(nused, nchunks, zwait, 0)


def _experts(cstart, nused, valid, xg, w1, w3, w2):
    nplane, rows, _ = xg.shape
    dh = nplane * LANES
    ne, d, f = w1.shape
    grid_spec = pltpu.PrefetchScalarGridSpec(
        num_scalar_prefetch=3,
        grid=(ne,),
        in_specs=[pl.BlockSpec((None, d, f), lambda e, cs, nu, va: (e, 0, 0)),
                  pl.BlockSpec((None, d, f), lambda e, cs, nu, va: (e, 0, 0)),
                  pl.BlockSpec((None, f, d), lambda e, cs, nu, va: (e, 0, 0)),
                  pl.BlockSpec(memory_space=pl.ANY)],
        out_specs=pl.BlockSpec(memory_space=pl.ANY),
        scratch_shapes=[pltpu.VMEM((EXP_XBUF, nplane, MOE_BLK, LANES), U32),
                        pltpu.VMEM((2, nplane, MOE_BLK, LANES), U32),
                        pltpu.SemaphoreType.DMA((EXP_XBUF,)), pltpu.SemaphoreType.DMA((2,))],
    )
    return pl.pallas_call(
        _expert_kernel,
        grid_spec=grid_spec,
        out_shape=jax.ShapeDtypeStruct((dh // LANES, rows, LANES), U32),
        compiler_params=_cparams(("arbitrary",)),
        name="experts",
    )(cstart, nused, valid, w1, w3, w2, xg)


def _sc_gather(x, idx):
    num = idx.shape[0]
    mesh = plsc.VectorSubcoreMesh(core_axis_name="core", subcore_axis_name="subcore")

    @pl.kernel(out_type=jax.ShapeDtypeStruct((num, x.shape[1]), x.dtype), mesh=mesh)
    def gather(x_hbm, i_hbm, o_hbm):
        def body(i_vmem, o_vmem):
            pltpu.sync_copy(x_hbm.at[i_vmem.at[0]], o_vmem)

        pltpu.emit_pipeline(
            body,
            grid=(num // SC_WINDOW,),
            in_specs=[pl.BlockSpec((1, SC_WINDOW), index_map=lambda i: (0, i))],
            out_specs=[pl.BlockSpec((SC_WINDOW, x.shape[1]), index_map=lambda i: (i, 0))],
            core_axis_name=("core", "subcore"),
            dimension_semantics=(pltpu.PARALLEL,),
        )(i_hbm, o_hbm)

    return gather(x, idx.reshape(1, num))


def _combine_kernel(yg_ref, gate_ref, x1s_ref, g2_ref, fg_ref, *rest):
    out_ref = rest[-1]
    gates = gate_ref[...]
    nch = yg_ref.shape[0]
    r_lo = [None] * nch
    r_hi = [None] * nch
    for k in range(TOP_K):
        gk = gates[:, k:k + 1]
        for c in range(nch):
            lo, hi = _unpack_rows(yg_ref[c, k])
            r_lo[c] = gk * lo if k == 0 else r_lo[c] + gk * lo
            r_hi[c] = gk * hi if k == 0 else r_hi[c] + gk * hi
    routed = jnp.concatenate(r_lo + r_hi, axis=1)
    x2 = x1s_ref[...] + g2_ref[...] * routed
    out_ref[...] = _rms(x2, fg_ref[...])


def _combine(yg, gates, x1s, mod3, final_g, s, part, prev_out):
    n, d = x1s.shape
    t = COMB_T
    tpb = s // t
    nch, _, npart, _ = yg.shape
    off = part * (npart // t)
    in_specs = [pl.BlockSpec((nch, TOP_K, t, LANES), lambda i: (0, 0, i, 0)),
                pl.BlockSpec((t, TOP_K), lambda i: (i + off, 0)),
                pl.BlockSpec((t, d), lambda i: (i + off, 0)),
                pl.BlockSpec((None, 1, d), lambda i: (((i + off) // tpb) * 6 + 5, 0, 0)),
                pl.BlockSpec((1, d), lambda i: (0, 0))]
    args = [yg, gates, x1s, mod3, final_g]
    aliases = {}
    if prev_out is not None:
        in_specs.append(pl.BlockSpec(memory_space=pl.ANY))
        args.append(prev_out)
        aliases = {len(args) - 1: 0}
    return pl.pallas_call(
        _combine_kernel,
        grid=(npart // t,),
        in_specs=in_specs,
        out_specs=pl.BlockSpec((t, d), lambda i: (i + off, 0)),
        out_shape=jax.ShapeDtypeStruct((n, d), F32),
        input_output_aliases=aliases,
        compiler_params=_cparams(("parallel",)),
        name="combine",
    )(*args)


def _place_heads(w, per_head, keep):
    r = w.shape[0]
    w = w.reshape(r, B_HEADS, per_head)[:, :, :keep]
    return jnp.pad(w, ((0, 0), (0, 0), (0, HEAD_PAD - keep))).reshape(r, B_HEADS * HEAD_PAD)


def _rope_tables(s):
    inv = ROPE_THETA ** (-jnp.arange(0, B_ROPE, 2, dtype=jnp.float32) / B_ROPE)
    ang = jnp.arange(s, dtype=jnp.float32)[:, None] * inv[None, :]
    cos, sin = jnp.cos(ang), jnp.sin(ang)
    ones = jnp.ones((s, B_NOPE), F32)
    zeros = jnp.zeros((s, B_NOPE), F32)
    tail1 = jnp.ones((s, HEAD_PAD - B_NOPE - B_ROPE), F32)
    tail0 = jnp.zeros((s, HEAD_PAD - B_NOPE - B_ROPE), F32)
    return (jnp.concatenate([ones, cos, cos, tail1], axis=1),
            jnp.concatenate([zeros, -sin, sin, tail0], axis=1))


def kernel(x, c, w_ada, b_ada, norm1_g, w_in, q_norm_g, w_uq, kv_norm_g, w_ukv, rel_table, a_out_g, b_out_g,
           w_o, norm2_g, w_router, e_bias, w1, w3, w2, ws1, ws3, ws2, final_g):
    b, s, d = x.shape
    n = b * s
    assert w_ada.shape[0] == 1, "single layer"
    x2 = x.reshape(n, d)

    mod = _ada(c, w_ada[0], b_ada[0])
    mod3 = mod.reshape(b * 6, 1, d)
    bias = _bias_tiles(rel_table)

    wi = w_in[0]
    c_kpe = 3 * A_WIDTH + Q_LORA + KV_LORA
    kpe_cols = jnp.pad(wi[:, c_kpe:], ((0, 0), (B_NOPE, HEAD_PAD - B_NOPE - B_ROPE)))
    w_in_ext = jnp.concatenate([wi[:, :c_kpe], kpe_cols], axis=1).astype(BF16)
    w_uq_p = _place_heads(w_uq[0], B_NOPE + B_ROPE, B_NOPE + B_ROPE).astype(BF16)
    w_uk_p = _place_heads(w_ukv[0], B_NOPE + B_VDIM, B_NOPE).astype(BF16)
    w_v = w_ukv[0].reshape(KV_LORA, B_HEADS, B_NOPE + B_VDIM)[:, :, B_NOPE:].reshape(KV_LORA, B_WIDTH).astype(BF16)
    cos_t, sin_t = _rope_tables(s)

    qa, ka, va, qb, kb, vb = _inproj(x2, mod3, norm1_g, w_in_ext, q_norm_g, w_uq_p, kv_norm_g, w_uk_p, w_v,
                                     cos_t, sin_t, b, s)
    a_out = _dilated(qa.reshape(b, s, A_WIDTH), ka.reshape(b, s, A_WIDTH), va.reshape(b, s, A_WIDTH), bias)
    b_out, w1_bf, w3_bf, w2_bf = _mla(qb, kb, vb, w1[0], w3[0], w2[0])

    w_router_t = _slab_order(w_router[0].T)
    x1s, h2p, scores_t = _outproj(x2, a_out.reshape(n, A_WIDTH), b_out.reshape(n, B_WIDTH), a_out_g, b_out_g,
                                  w_o[0].astype(BF16), mod3, norm2_g, w_router_t,
                                  ws1[0].astype(BF16), ws3[0].astype(BF16), ws2[0].astype(BF16), s)

    eidx_t, gates_t, counts_slab = _route(scores_t, _slab_order(e_bias.reshape(N_EXPERTS, 1)))

    counts = _expert_order(counts_slab)[:, 0]
    padded = (counts + MOE_BLK - 1) // MOE_BLK * MOE_BLK
    pends = jnp.cumsum(padded)
    pstart = pends - padded
    nk = n * TOP_K
    nblk = -(-(nk + N_EXPERTS * (MOE_BLK - 1)) // MOE_BLK)
    rows_total = nblk * MOE_BLK
    blk_row = jnp.arange(nblk, dtype=I32) * MOE_BLK
    blk_e = jnp.minimum(jnp.sum((pends[None, :] <= blk_row[:, None]).astype(I32), axis=1), N_EXPERTS - 1)
    seg_end = (pstart + counts)[blk_e]
    valid = jnp.clip(seg_end - blk_row, 0, MOE_BLK).astype(I32)
    nused = (pends[-1] // MOE_BLK).astype(I32).reshape(1)
    cstart = jnp.concatenate([jnp.zeros((1,), I32), (pends // MOE_BLK).astype(I32)])

    dest_t = _dest(eidx_t, _slab_order(pstart.astype(F32).reshape(N_EXPERTS, 1)))
    nplane = h2p.shape[0]
    gidx = (dest_t.reshape(1, nk) + (jnp.arange(nplane, dtype=I32) * rows_total)[:, None]).reshape(nplane * nk)
    nwin = n // SC_WINDOW
    sidx = dest_t.reshape(TOP_K, nwin, SC_WINDOW).transpose(1, 0, 2)[None] + (
        jnp.arange(nplane, dtype=I32) * rows_total)[:, None, None, None]
    xg = _sc_scatter(h2p.reshape(nplane * n, LANES), sidx.reshape(nplane * nk), nplane * rows_total, nplane, TOP_K)
    y = _experts(cstart, nused, valid, xg.reshape(nplane, rows_total, LANES), w1_bf, w3_bf, w2_bf)
    y_flat = y.reshape(nplane * rows_total, LANES)
    gidx3 = gidx.reshape(nplane, TOP_K, n)
    gates = gates_t.T
    npart = n // COMB_SPLIT
    out = None
    for part in range(COMB_SPLIT):
        pidx = gidx3[:, :, part * npart:(part + 1) * npart].reshape(nplane * TOP_K * npart)
        yg = _sc_gather(y_flat, pidx).reshape(nplane, TOP_K, npart, LANES)
        out = _combine(yg, gates, x1s, mod3, final_g.reshape(1, d), s, part, out)
    return out.reshape(b, s, d)
```

```python
import functools
import math

import jax
import jax.numpy as jnp
from jax import lax
from jax.experimental import pallas as pl
from jax.experimental.pallas import tpu as pltpu
from jax.experimental.pallas import tpu_sc as plsc

F32 = jnp.float32
BF16 = jnp.bfloat16
U32 = jnp.uint32
I32 = jnp.int32
HIGHEST = lax.Precision.HIGHEST

D_MODEL = 1024
A_HEADS = 8
A_HEAD_DIM = 64
A_WIDTH = A_HEADS * A_HEAD_DIM
A_PATTERNS = ((128, 1), (512, 4), (2048, 16))
A_RADIUS = 64
REL_BUCKETS = 32
REL_MAX_DIST = 1024
B_HEADS = 8
B_NOPE = 64
B_ROPE = 32
B_VDIM = 64
B_WIDTH = B_HEADS * B_VDIM
Q_LORA = 384
KV_LORA = 256
ROPE_THETA = 10000.0
N_EXPERTS = 256
TOP_K = 8
N_GROUPS = 8
GROUP_SIZE = N_EXPERTS // N_GROUPS
TOPK_GROUPS = 4
EXPERT_FF = 256
SHARED_FF = 256
ROUTED_SCALE = 2.5
EPS = 1e-6
NEG_INF = -1e30
LOG2E = math.log2(math.e)

LANES = 128
SUBLANES = 8
HEAD_PAD = 128
IN_COLS_EXT = 3 * A_WIDTH + Q_LORA + KV_LORA + HEAD_PAD

TM_INPROJ = 512
TM_OUTPROJ = 1024
DIL_QB = 128
DIL_KW = DIL_QB + 2 * A_RADIUS
DIL_ITER_ROWS = 1024
MLA_TQ = 1024
MLA_KC = 2048
ROUTE_T = 512
MOE_BLK = 512
EXP_AHEAD = 3
EXP_XBUF = EXP_AHEAD + 1
SC_WINDOW = 128
COMB_T = 256
COMB_SPLIT = 4
VMEM_LIMIT = 56 * 1024 * 1024


def _cparams(sem):
    return pltpu.CompilerParams(dimension_semantics=sem, vmem_limit_bytes=VMEM_LIMIT)


def _rms(x, g):
    return x * lax.rsqrt(jnp.mean(x * x, axis=-1, keepdims=True) + EPS) * g


def _silu(x):
    return x * jax.nn.sigmoid(x)


def _pack_rows(x):
    half = x.shape[1] // 2
    bits = lax.bitcast_convert_type(x.astype(BF16).astype(F32), U32)
    return (bits[:, :half] >> 16) | bits[:, half:]


def _unpack_rows(w):
    lo = lax.bitcast_convert_type(w << 16, F32)
    hi = lax.bitcast_convert_type(w & jnp.uint32(0xFFFF0000), F32)
    return lo, hi


def _ada_kernel(c_ref, w_ref, b_ref, o_ref):
    o_ref[...] = jnp.dot(_silu(c_ref[...]), w_ref[...], precision=HIGHEST,
                         preferred_element_type=F32) + b_ref[...]


def _ada(c, w_ada, b_ada):
    b, d = c.shape
    n6 = w_ada.shape[1] // d
    return pl.pallas_call(
        _ada_kernel,
        grid=(n6,),
        in_specs=[pl.BlockSpec((b, d), lambda j: (0, 0)),
                  pl.BlockSpec((d, d), lambda j: (0, j)),
                  pl.BlockSpec((1, d), lambda j: (0, j))],
        out_specs=pl.BlockSpec((b, d), lambda j: (0, j)),
        out_shape=jax.ShapeDtypeStruct((b, n6 * d), F32),
        compiler_params=_cparams(("parallel",)),
        name="ada",
    )(c, w_ada, b_ada.reshape(1, -1))


def _t5_bucket(rel):
    half = REL_BUCKETS // 2
    max_exact = half // 2
    ret = jnp.where(rel > 0, half, 0)
    n = jnp.abs(rel)
    nf = jnp.maximum(n, 1).astype(jnp.float32)
    large = max_exact + (jnp.log(nf / max_exact) / math.log(REL_MAX_DIST / max_exact)
                         * (half - max_exact)).astype(jnp.int32)
    large = jnp.minimum(large, half - 1)
    return ret + jnp.where(n < max_exact, n, large)


def _dil_geometry(s):
    geo, base = [], 0
    for _, dil in A_PATTERNS:
        sub_len = s // dil
        qb = sub_len if sub_len <= DIL_KW else DIL_QB
        kw = min(qb + 2 * A_RADIUS, sub_len)
        assert sub_len % qb == 0 and (sub_len == qb or kw == qb + 2 * A_RADIUS)
        shifts = (A_RADIUS,) if sub_len == qb else (A_RADIUS, 0, -A_RADIUS)
        geo.append((dil, qb, kw, shifts, base))
        base += len(shifts)
    return geo


def _bucket_tiles(geo):
    qi = jnp.arange(max(g[1] for g in geo), dtype=jnp.int32)[:, None]
    ki = jnp.arange(max(g[2] for g in geo), dtype=jnp.int32)[None, :]
    tiles = []
    for dilation, qb, kw, shifts, _ in geo:
        for shift in shifts:
            off = ki + shift - A_RADIUS - qi
            bkt = _t5_bucket(off * dilation)
            inside = (jnp.abs(off) <= A_RADIUS) & (qi < qb) & (ki < kw)
            tiles.append(jnp.where(inside, bkt, -1))
    return jnp.stack(tiles, axis=0)


def _bias_kernel(tab_ref, bkt_ref, o_ref):
    bkt = bkt_ref[0]
    for h in range(A_HEADS):
        acc = jnp.full(bkt.shape, NEG_INF, F32)
        for b in range(REL_BUCKETS):
            acc = jnp.where(bkt == b, tab_ref[b, h] * LOG2E, acc)
        o_ref[0, h] = acc


def _bias_tiles(rel_table, geo):
    bkt = _bucket_tiles(geo)
    nt = bkt.shape[0]
    return pl.pallas_call(
        _bias_kernel,
        grid=(nt,),
        in_specs=[pl.BlockSpec(memory_space=pltpu.SMEM),
                  pl.BlockSpec((1,) + bkt.shape[1:], lambda t: (t, 0, 0))],
        out_specs=pl.BlockSpec((1, A_HEADS) + bkt.shape[1:], lambda t: (t, 0, 0, 0)),
        out_shape=jax.ShapeDtypeStruct((nt, A_HEADS) + bkt.shape[1:], F32),
        compiler_params=_cparams(("parallel",)),
        name="bias",
    )(rel_table, bkt)


def _rope(x, cos, sin, lane_lt_mid):
    half = B_ROPE // 2
    rot = jnp.where(lane_lt_mid, pltpu.roll(x, HEAD_PAD - half, 1), pltpu.roll(x, half, 1))
    return x * cos + rot * sin


def _inproj_kernel(x_ref, sc_ref, sh_ref, g1_ref, win_ref, qg_ref, wuq_ref, kvg_ref, wuk_ref, wv_ref,
                   cos_ref, sin_ref, qa_ref, ka_ref, va_ref, qb_ref, kb_ref, vb_ref):
    x = x_ref[...]
    h = _rms(x, g1_ref[...]) * (1.0 + sc_ref[...]) + sh_ref[...]
    proj = jnp.dot(h.astype(BF16), win_ref[...], preferred_element_type=F32)
    aw = A_WIDTH
    qa_ref[...] = proj[:, 0:aw] * (LOG2E / math.sqrt(A_HEAD_DIM))
    ka_ref[...] = proj[:, aw:2 * aw]
    va_ref[...] = proj[:, 2 * aw:3 * aw]
    c0 = 3 * aw
    q_lat = proj[:, c0:c0 + Q_LORA]
    kv_lat = proj[:, c0 + Q_LORA:c0 + Q_LORA + KV_LORA]
    kpe = proj[:, c0 + Q_LORA + KV_LORA:]
    qn = _rms(q_lat, qg_ref[...]).astype(BF16)
    kvn = _rms(kv_lat, kvg_ref[...]).astype(BF16)
    qm = jnp.dot(qn, wuq_ref[...], preferred_element_type=F32)
    kn = jnp.dot(kvn, wuk_ref[...], preferred_element_type=F32)
    vv = jnp.dot(kvn, wv_ref[...], preferred_element_type=F32)
    cos = cos_ref[...]
    sin = sin_ref[...]
    lane = lax.broadcasted_iota(jnp.int32, cos.shape, 1)
    lt_mid = lane < (B_NOPE + B_ROPE // 2)
    qscale = LOG2E / math.sqrt(B_NOPE + B_ROPE)
    for hd in range(B_HEADS):
        sl = slice(hd * HEAD_PAD, (hd + 1) * HEAD_PAD)
        qb_ref[hd] = (_rope(qm[:, sl], cos, sin, lt_mid) * qscale).astype(BF16)
        kb_ref[hd] = _rope(kn[:, sl] + kpe, cos, sin, lt_mid).astype(BF16)
    for p in range(B_HEADS // 2):
        vb_ref[p] = vv[:, p * LANES:(p + 1) * LANES].astype(BF16)


def _inproj(x2, mod3, norm1_g, w_in_ext, q_norm_g, w_uq_p, kv_norm_g, w_uk_p, w_v, cos_t, sin_t, b, s):
    n, d = x2.shape
    tm = TM_INPROJ
    tpb = s // tm
    row = lambda i: (i, 0)
    const = lambda i: (0, 0)
    hm = lambda i: (i // tpb, 0, i % tpb, 0)
    return pl.pallas_call(
        _inproj_kernel,
        grid=(n // tm,),
        in_specs=[pl.BlockSpec((tm, d), row),
                  pl.BlockSpec((None, 1, d), lambda i: ((i // tpb) * 6 + 1, 0, 0)),
                  pl.BlockSpec((None, 1, d), lambda i: ((i // tpb) * 6 + 0, 0, 0)),
                  pl.BlockSpec((1, d), const),
                  pl.BlockSpec(w_in_ext.shape, const),
                  pl.BlockSpec((1, Q_LORA), const),
                  pl.BlockSpec(w_uq_p.shape, const),
                  pl.BlockSpec((1, KV_LORA), const),
                  pl.BlockSpec(w_uk_p.shape, const),
                  pl.BlockSpec(w_v.shape, const),
                  pl.BlockSpec((tm, HEAD_PAD), lambda i: (i % tpb, 0)),
                  pl.BlockSpec((tm, HEAD_PAD), lambda i: (i % tpb, 0))],
        out_specs=[pl.BlockSpec((tm, A_WIDTH), row),
                   pl.BlockSpec((tm, A_WIDTH), row),
                   pl.BlockSpec((tm, A_WIDTH), row),
                   pl.BlockSpec((None, B_HEADS, tm, HEAD_PAD), hm),
                   pl.BlockSpec((None, B_HEADS, tm, HEAD_PAD), hm),
                   pl.BlockSpec((None, B_HEADS // 2, tm, LANES), hm)],
        out_shape=[jax.ShapeDtypeStruct((n, A_WIDTH), F32),
                   jax.ShapeDtypeStruct((n, A_WIDTH), F32),
                   jax.ShapeDtypeStruct((n, A_WIDTH), F32),
                   jax.ShapeDtypeStruct((b, B_HEADS, s, HEAD_PAD), BF16),
                   jax.ShapeDtypeStruct((b, B_HEADS, s, HEAD_PAD), BF16),
                   jax.ShapeDtypeStruct((b, B_HEADS // 2, s, LANES), BF16)],
        compiler_params=_cparams(("parallel",)),
        name="inproj",
    )(x2, mod3, mod3, norm1_g, w_in_ext, q_norm_g, w_uq_p, kv_norm_g, w_uk_p, w_v, cos_t, sin_t)


def _dil_block(q_ref, k_ref, v_ref, bias_ref, o_scr, l_scr, pi, geo, nblk, job):
    dil, qb, kwin, shifts, tile0 = geo
    sub_len = nblk * qb
    r = job // nblk
    bi = job % nblk
    q0 = bi * qb
    ws = jnp.clip(q0 - A_RADIUS, 0, sub_len - kwin)
    var = jnp.where(bi == 0, 0, jnp.where(bi == nblk - 1, 2, 1)) if len(shifts) > 1 else 0
    if dil == 1:
        qsl = pl.ds(pl.multiple_of(q0, qb), qb)
        ksl = pl.ds(pl.multiple_of(ws, A_RADIUS), kwin)
    else:
        qsl = pl.ds(r + dil * q0, qb, stride=dil)
        ksl = pl.ds(r + dil * ws, kwin, stride=dil)
    q = q_ref[qsl, :]
    kw = k_ref[ksl, :].astype(BF16)
    vw = v_ref[ksl, :].astype(BF16)
    lo = lax.broadcasted_iota(jnp.int32, q.shape, 1) < A_HEAD_DIM
    outs, lses = [], []
    for hh in range(2):
        qm = jnp.where(lo if hh == 0 else jnp.logical_not(lo), q, 0.0).astype(BF16)
        sc = lax.dot_general(qm, kw, (((1,), (1,)), ((), ())), preferred_element_type=F32)
        sc = sc + bias_ref[tile0 + var, hh, :qb, :kwin]
        m = jnp.max(sc, axis=-1, keepdims=True)
        p = jnp.exp2(sc - m)
        l = jnp.sum(p, axis=-1, keepdims=True)
        o = jnp.dot(p.astype(BF16), vw, preferred_element_type=F32)
        outs.append(o / l)
        lses.append(m + jnp.log2(l))
    o_scr[pi, qsl, :] = jnp.where(lo, outs[0], outs[1])
    l_scr[pi, qsl, :] = jnp.where(lo, lses[0], lses[1])


def _dilated_kernel(q_ref, k_ref, v_ref, bias_ref, out_ref, o_scr, l_scr):
    s = q_ref.shape[0]
    for pi, geo in enumerate(_dil_geometry(s)):
        dil, qb = geo[0], geo[1]
        njobs = s // qb
        unroll = DIL_ITER_ROWS // qb
        assert njobs % unroll == 0
        blk = functools.partial(_dil_block, q_ref, k_ref, v_ref, bias_ref, o_scr, l_scr, pi, geo,
                                s // dil // qb)

        def group(g, c, blk=blk, unroll=unroll):
            for u in range(unroll):
                blk(g * unroll + u)
            return c

        lax.fori_loop(0, njobs // unroll, group, 0)

    chunk = 512

    def comb(i, c):
        rows = pl.ds(pl.multiple_of(i * chunk, chunk), chunk)
        l0, l1, l2 = l_scr[0, rows, :], l_scr[1, rows, :], l_scr[2, rows, :]
        mx = jnp.maximum(jnp.maximum(l0, l1), l2)
        e0, e1, e2 = jnp.exp2(l0 - mx), jnp.exp2(l1 - mx), jnp.exp2(l2 - mx)
        num = e0 * o_scr[0, rows, :] + e1 * o_scr[1, rows, :] + e2 * o_scr[2, rows, :]
        out_ref[rows, :] = num / (e0 + e1 + e2)
        return c

    lax.fori_loop(0, s // chunk, comb, 0)


def _dilated(qa, ka, va, bias):
    b, s, _ = qa.shape
    npair = A_WIDTH // LANES
    blk = pl.BlockSpec((None, s, LANES), lambda bi, p: (bi, 0, p))
    return pl.pallas_call(
        _dilated_kernel,
        grid=(b, npair),
        in_specs=[blk, blk, blk,
                  pl.BlockSpec((bias.shape[0], 2) + bias.shape[2:], lambda bi, p: (0, p, 0, 0))],
        out_specs=blk,
        out_shape=jax.ShapeDtypeStruct((b, s, A_WIDTH), F32),
        scratch_shapes=[pltpu.VMEM((len(A_PATTERNS), s, LANES), F32),
                        pltpu.VMEM((len(A_PATTERNS), s, LANES), F32)],
        compiler_params=_cparams(("parallel", "parallel")),
        name="dilated",
    )(qa, ka, va, bias)


def _mla_kernel(q_ref, k_ref, v_ref, w1_ref, w3_ref, w2_ref, o_ref, w1b_ref, w3b_ref, w2b_ref):
    w1b_ref[...] = w1_ref[...].astype(BF16)
    w3b_ref[...] = w3_ref[...].astype(BF16)
    w2b_ref[...] = w2_ref[...].astype(BF16)
    tq = q_ref.shape[1]
    nkc = k_ref.shape[1] // MLA_KC
    outs = []
    for hh in range(2):
        q = q_ref[hh]
        m = jnp.full((tq, 1), -jnp.inf, F32)
        l = jnp.zeros((tq, 1), F32)
        acc = jnp.zeros((tq, LANES), F32)
        for c in range(nkc):
            keys = slice(c * MLA_KC, (c + 1) * MLA_KC)
            sc = lax.dot_general(q, k_ref[hh, keys, :], (((1,), (1,)), ((), ())), preferred_element_type=F32)
            m_new = jnp.maximum(m, jnp.max(sc, axis=-1, keepdims=True))
            alpha = jnp.exp2(m - m_new)
            p = jnp.exp2(sc - m_new)
            l = alpha * l + jnp.sum(p, axis=-1, keepdims=True)
            acc = alpha * acc + jnp.dot(p.astype(BF16), v_ref[keys, :], preferred_element_type=F32)
            m = m_new
        outs.append(acc / l)
    lo = lax.broadcasted_iota(jnp.int32, outs[0].shape, 1) < B_VDIM
    o_ref[...] = jnp.where(lo, outs[0], outs[1])


def _mla(qb, kb, vb, w1, w3, w2):
    b, h, s, _ = qb.shape
    npair = h // 2
    tq = MLA_TQ
    nq = s // tq
    nsteps = b * npair * nq
    ne, d, f = w1.shape
    assert ne % nsteps == 0
    epb = ne // nsteps
    wmap = lambda bi, p, qi: ((bi * npair + p) * nq + qi, 0, 0)
    return pl.pallas_call(
        _mla_kernel,
        grid=(b, npair, nq),
        in_specs=[pl.BlockSpec((None, 2, tq, HEAD_PAD), lambda bi, p, qi: (bi, p, qi, 0)),
                  pl.BlockSpec((None, 2, s, HEAD_PAD), lambda bi, p, qi: (bi, p, 0, 0)),
                  pl.BlockSpec((None, None, s, LANES), lambda bi, p, qi: (bi, p, 0, 0)),
                  pl.BlockSpec((epb, d, f), wmap),
                  pl.BlockSpec((epb, d, f), wmap),
                  pl.BlockSpec((epb, f, d), wmap)],
        out_specs=[pl.BlockSpec((None, tq, LANES), lambda bi, p, qi: (bi, qi, p)),
                   pl.BlockSpec((epb, d, f), wmap),
                   pl.BlockSpec((epb, d, f), wmap),
                   pl.BlockSpec((epb, f, d), wmap)],
        out_shape=[jax.ShapeDtypeStruct((b, s, B_WIDTH), F32),
                   jax.ShapeDtypeStruct((ne, d, f), BF16),
                   jax.ShapeDtypeStruct((ne, d, f), BF16),
                   jax.ShapeDtypeStruct((ne, f, d), BF16)],
        compiler_params=_cparams(("parallel", "parallel", "arbitrary")),
        name="mla",
    )(qb, kb, vb, w1, w3, w2)


def _outproj_kernel(x_ref, a_ref, b_ref, ag_ref, bg_ref, wo_ref, g1_ref, n2_ref, sc_ref, sh_ref, g2_ref,
                    wrt_ref, ws1_ref, ws3_ref, ws2_ref, x1s_ref, h2p_ref, scores_ref):
    an = _rms(a_ref[...], ag_ref[...])
    bn = _rms(b_ref[...], bg_ref[...])
    mix = jnp.concatenate([an, bn], axis=-1).astype(BF16)
    x1 = x_ref[...] + g1_ref[...] * jnp.dot(mix, wo_ref[...], preferred_element_type=F32)
    h2 = _rms(x1, n2_ref[...]) * (1.0 + sc_ref[...]) + sh_ref[...]
    packed = _pack_rows(h2)
    for cg in range(h2p_ref.shape[0]):
        h2p_ref[cg] = packed[:, cg * LANES:(cg + 1) * LANES]
    logits = lax.dot_general(wrt_ref[...], h2, (((1,), (1,)), ((), ())), precision=HIGHEST,
                             preferred_element_type=F32)
    scores_ref[...] = jax.nn.sigmoid(logits)
    h2b = h2.astype(BF16)
    hid = _silu(jnp.dot(h2b, ws1_ref[...], preferred_element_type=F32)) * jnp.dot(
        h2b, ws3_ref[...], preferred_element_type=F32)
    shared = jnp.dot(hid.astype(BF16), ws2_ref[...], preferred_element_type=F32)
    x1s_ref[...] = x1 + g2_ref[...] * shared


def _outproj(x2, a_out, b_out, a_out_g, b_out_g, w_o, mod3, norm2_g, w_router_t, ws1, ws3, ws2, s):
    n, d = x2.shape
    tm = TM_OUTPROJ
    tpb = s // tm
    row = lambda i: (i, 0)
    const = lambda i: (0, 0)
    modspec = lambda j: pl.BlockSpec((None, 1, d), lambda i: ((i // tpb) * 6 + j, 0, 0))
    return pl.pallas_call(
        _outproj_kernel,
        grid=(n // tm,),
        in_specs=[pl.BlockSpec((tm, d), row),
                  pl.BlockSpec((tm, A_WIDTH), row),
                  pl.BlockSpec((tm, B_WIDTH), row),
                  pl.BlockSpec((1, A_WIDTH), const),
                  pl.BlockSpec((1, B_WIDTH), const),
                  pl.BlockSpec(w_o.shape, const),
                  modspec(2),
                  pl.BlockSpec((1, d), const),
                  modspec(4), modspec(3), modspec(5),
                  pl.BlockSpec(w_router_t.shape, const),
                  pl.BlockSpec(ws1.shape, const),
                  pl.BlockSpec(ws3.shape, const),
                  pl.BlockSpec(ws2.shape, const)],
        out_specs=[pl.BlockSpec((tm, d), row),
                   pl.BlockSpec((d // 2 // LANES, tm, LANES), lambda i: (0, i, 0)),
                   pl.BlockSpec((N_EXPERTS, tm), lambda i: (0, i))],
        out_shape=[jax.ShapeDtypeStruct((n, d), F32),
                   jax.ShapeDtypeStruct((d // 2 // LANES, n, LANES), U32),
                   jax.ShapeDtypeStruct((N_EXPERTS, n), F32)],
        compiler_params=_cparams(("parallel",)),
        name="outproj",
    )(x2, a_out, b_out, a_out_g, b_out_g, w_o, mod3, norm2_g, mod3, mod3, mod3, w_router_t, ws1, ws3, ws2)


def _slab_order(v):
    return v.reshape((N_GROUPS, GROUP_SIZE) + v.shape[1:]).swapaxes(0, 1).reshape(v.shape)


def _expert_order(v):
    return v.reshape((GROUP_SIZE, N_GROUPS) + v.shape[1:]).swapaxes(0, 1).reshape(v.shape)


def _sublane_all(x, op):
    for sh in (4, 2, 1):
        x = op(x, pltpu.roll(x, sh, 0))
    return x


def _route_kernel(st_ref, bias_ref, eidx_ref, gate_ref, cnt_ref):
    nsl = GROUP_SIZE
    t = st_ref.shape[1]
    sub = lax.broadcasted_iota(I32, (SUBLANES, t), 0)
    ninf = -jnp.inf
    big = jnp.int32(1 << 30)
    sc = [st_ref[j * SUBLANES:(j + 1) * SUBLANES, :] for j in range(nsl)]
    sel = [sc[j] + bias_ref[j * SUBLANES:(j + 1) * SUBLANES, :] for j in range(nsl)]
    eid = [sub * GROUP_SIZE + j for j in range(nsl)]

    m1 = sel[0]
    m2 = jnp.full_like(m1, ninf)
    for j in range(1, nsl):
        m2 = jnp.maximum(m2, jnp.minimum(m1, sel[j]))
        m1 = jnp.maximum(m1, sel[j])
    gs = m1 + m2

    rank = jnp.zeros((SUBLANES, t), I32)
    for sh in range(1, N_GROUPS):
        other = pltpu.roll(gs, sh, 0)
        ahead = (other > gs) | ((other == gs) & (sub >= sh))
        rank = rank + ahead.astype(I32)
    gmask = rank < TOPK_GROUPS

    msel = [jnp.where(gmask, sel[j], ninf) for j in range(nsl)]
    hits = [jnp.zeros((SUBLANES, t), I32) for _ in range(nsl)]
    eidx = jnp.zeros((TOP_K, t), I32)
    gates = jnp.zeros((TOP_K, t), F32)
    for k in range(TOP_K):
        mx = msel[0]
        for j in range(1, nsl):
            mx = jnp.maximum(mx, msel[j])
        mx = _sublane_all(mx, jnp.maximum)
        cand = jnp.where(msel[0] == mx, eid[0], big)
        for j in range(1, nsl):
            cand = jnp.minimum(cand, jnp.where(msel[j] == mx, eid[j], big))
        idx = _sublane_all(cand, jnp.minimum)
        gk = jnp.zeros((SUBLANES, t), F32)
        for j in range(nsl):
            hit = eid[j] == idx
            gk = gk + jnp.where(hit, sc[j], 0.0)
            msel[j] = jnp.where(hit, ninf, msel[j])
            hits[j] = hits[j] + hit.astype(I32)
        gk = _sublane_all(gk, jnp.add)
        eidx = jnp.where(sub == k, idx, eidx)
        gates = jnp.where(sub == k, gk, gates)
    gsum = _sublane_all(gates, jnp.add)
    eidx_ref[...] = eidx
    gate_ref[...] = gates / gsum * ROUTED_SCALE

    @pl.when(pl.program_id(0) == 0)
    def _():
        cnt_ref[...] = jnp.zeros_like(cnt_ref)

    for j in range(nsl):
        cnt_ref[j * SUBLANES:(j + 1) * SUBLANES, :] += jnp.sum(hits[j].astype(F32), axis=1,
                                                               keepdims=True).astype(I32)


def _route(scores_t, e_bias_slab):
    e, n = scores_t.shape
    t = ROUTE_T
    return pl.pallas_call(
        _route_kernel,
        grid=(n // t,),
        in_specs=[pl.BlockSpec((e, t), lambda i: (0, i)),
                  pl.BlockSpec((e, 1), lambda i: (0, 0))],
        out_specs=[pl.BlockSpec((TOP_K, t), lambda i: (0, i)),
                   pl.BlockSpec((TOP_K, t), lambda i: (0, i)),
                   pl.BlockSpec((e, 1), lambda i: (0, 0))],
        out_shape=[jax.ShapeDtypeStruct((TOP_K, n), I32),
                   jax.ShapeDtypeStruct((TOP_K, n), F32),
                   jax.ShapeDtypeStruct((e, 1), I32)],
        compiler_params=_cparams(("arbitrary",)),
        name="route",
    )(scores_t, e_bias_slab)


def _dest_kernel(eidx_ref, pstart_ref, dest_ref, carry_ref):
    @pl.when(pl.program_id(0) == 0)
    def _():
        carry_ref[...] = jnp.zeros_like(carry_ref)

    nsl = GROUP_SIZE
    t = eidx_ref.shape[1]
    sub = lax.broadcasted_iota(I32, (SUBLANES, t), 0)
    eid = [sub * GROUP_SIZE + j for j in range(nsl)]
    ek = [eidx_ref[k:k + 1, :] for k in range(TOP_K)]
    slabs = []
    for j in range(nsl):
        oh = jnp.zeros((SUBLANES, t), F32)
        for k in range(TOP_K):
            oh = oh + (eid[j] == ek[k]).astype(F32)
        slabs.append(oh)
    onehot = jnp.concatenate(slabs, axis=0)
    row = lax.broadcasted_iota(I32, (t, t), 0)
    col = lax.broadcasted_iota(I32, (t, t), 1)
    upper = (row < col).astype(BF16)
    before = jnp.dot(onehot.astype(BF16), upper, preferred_element_type=F32)
    base = before + carry_ref[...] + pstart_ref[...]
    dest = jnp.zeros((TOP_K, t), I32)
    for k in range(TOP_K):
        acc = jnp.zeros((SUBLANES, t), F32)
        for j in range(nsl):
            acc = acc + jnp.where(eid[j] == ek[k], base[j * SUBLANES:(j + 1) * SUBLANES, :], 0.0)
        dk = _sublane_all(acc, jnp.add).astype(I32)
        dest = jnp.where(sub == k, dk, dest)
    dest_ref[...] = dest
    carry_ref[...] += jnp.sum(onehot, axis=1, keepdims=True)


def _dest(eidx_t, pstart_slab):
    n = eidx_t.shape[1]
    t = ROUTE_T
    return pl.pallas_call(
        _dest_kernel,
        grid=(n // t,),
        in_specs=[pl.BlockSpec((TOP_K, t), lambda i: (0, i)),
                  pl.BlockSpec((N_EXPERTS, 1), lambda i: (0, 0))],
        out_specs=pl.BlockSpec((TOP_K, t), lambda i: (0, i)),
        out_shape=jax.ShapeDtypeStruct((TOP_K, n), I32),
        scratch_shapes=[pltpu.VMEM((N_EXPERTS, 1), F32)],
        compiler_params=_cparams(("arbitrary",)),
        name="dest",
    )(eidx_t, pstart_slab)


def _sc_scatter(x, idx, rows_out, nplane, nslot):
    del nplane
    num = idx.shape[0]
    mesh = plsc.VectorSubcoreMesh(core_axis_name="core", subcore_axis_name="subcore")

    @pl.kernel(out_type=jax.ShapeDtypeStruct((rows_out, x.shape[1]), x.dtype), mesh=mesh, scratch_types=[])
    def scatter(x_hbm, i_hbm, o_hbm):
        def body(x_vmem, i_vmem):
            pltpu.sync_copy(x_vmem, o_hbm.at[i_vmem.at[0]])

        pltpu.emit_pipeline(
            body,
            grid=(num // SC_WINDOW,),
            in_specs=[pl.BlockSpec((SC_WINDOW, x.shape[1]), index_map=lambda i: (i // nslot, 0)),
                      pl.BlockSpec((1, SC_WINDOW), index_map=lambda i: (0, i))],
            out_specs=[],
            core_axis_name=("core", "subcore"),
            dimension_semantics=(pltpu.PARALLEL,),
        )(x_hbm, i_hbm)

    return scatter(x, idx.reshape(1, num))


def _expert_kernel(cstart_ref, nused_ref, valid_ref, w1_ref, w3_ref, w2_ref, xg_ref, y_ref,
                   xbuf, ybuf, xsem, ysem):
    e = pl.program_id(0)
    c0 = cstart_ref[e]
    c1 = cstart_ref[e + 1]
    nused = nused_ref[0]
    nchunks = xg_ref.shape[1] // MOE_BLK

    def rows(g):
        start = g * MOE_BLK
        return pl.ds(start if isinstance(g, int) else pl.multiple_of(start, MOE_BLK), MOE_BLK)

    nplane = xg_ref.shape[0]

    def xcopies(g):
        slot = g % EXP_XBUF
        return [pltpu.make_async_copy(xg_ref.at[pl.ds(0, nplane), rows(g)], xbuf.at[slot], xsem.at[slot])]

    def ycopies(g, slot):
        return [pltpu.make_async_copy(ybuf.at[slot], y_ref.at[pl.ds(0, nplane), rows(g)], ysem.at[slot])]

    @pl.when(e == 0)
    def _():
        for j in range(EXP_AHEAD):
            @pl.when(j < nused)
            def _():
                for cp in xcopies(j):
                    cp.start()

    @pl.when(c1 > c0)
    def _():
        def ffn(g):
            row_id = lax.broadcasted_iota(I32, (MOE_BLK, LANES), 0)
            keep = row_id < valid_ref[g]
            halves = [_unpack_rows(jnp.where(keep, xbuf[g % EXP_XBUF, c], jnp.uint32(0))) for c in range(nplane)]
            xb = jnp.concatenate([h[0] for h in halves] + [h[1] for h in halves], axis=1).astype(BF16)
            hid = _silu(jnp.dot(xb, w1_ref[...], preferred_element_type=F32)) * jnp.dot(
                xb, w3_ref[...], preferred_element_type=F32)
            return _pack_rows(jnp.dot(hid.astype(BF16), w2_ref[...], preferred_element_type=F32))

        def chunk(g, carry):
            @pl.when(g + EXP_AHEAD < nused)
            def _():
                for cp in xcopies(g + EXP_AHEAD):
                    cp.start()

            for cp in xcopies(g):
                cp.wait()
            yp = ffn(g)
            yslot = g % 2

            @pl.when(g >= 2)
            def _():
                for cp in ycopies(g - 2, yslot):
                    cp.wait()

            for c in range(nplane):
                ybuf[yslot, c] = yp[:, c * LANES:(c + 1) * LANES]
            for cp in ycopies(g, yslot):
                cp.start()
            return carry

        lax.fori_loop(c0, c1, chunk, 0)

    @pl.when(e == pl.num_programs(0) - 1)
    def _():
        for back in (2, 1):
            g = nused - back

            @pl.when(g >= 0)
            def _():
                for cp in ycopies(g, g % 2):
                    cp.wait()

        ybuf[0] = jnp.zeros(ybuf.shape[1:], ybuf.dtype)

        def zstart(g, c):
            for cp in ycopies(g, 0):
                cp.start()
            return c

        def zwait(g, c):
            for cp in ycopies(g, 0):
                cp.wait()
            return c

        lax.fori_loop(nused, nchunks, zstart, 0)
        lax.fori_loop(nused, nchunks, zwait, 0)


def _experts(cstart, nused, valid, xg, w1, w3, w2):
    nplane, rows, _ = xg.shape
    dh = nplane * LANES
    ne, d, f = w1.shape
    grid_spec = pltpu.PrefetchScalarGridSpec(
        num_scalar_prefetch=3,
        grid=(ne,),
        in_specs=[pl.BlockSpec((None, d, f), lambda e, cs, nu, va: (e, 0, 0)),
                  pl.BlockSpec((None, d, f), lambda e, cs, nu, va: (e, 0, 0)),
                  pl.BlockSpec((None, f, d), lambda e, cs, nu, va: (e, 0, 0)),
                  pl.BlockSpec(memory_space=pl.ANY)],
        out_specs=pl.BlockSpec(memory_space=pl.ANY),
        scratch_shapes=[pltpu.VMEM((EXP_XBUF, nplane, MOE_BLK, LANES), U32),
                        pltpu.VMEM((2, nplane, MOE_BLK, LANES), U32),
                        pltpu.SemaphoreType.DMA((EXP_XBUF,)), pltpu.SemaphoreType.DMA((2,))],
    )
    return pl.pallas_call(
        _expert_kernel,
        grid_spec=grid_spec,
        out_shape=jax.ShapeDtypeStruct((dh // LANES, rows, LANES), U32),
        compiler_params=_cparams(("arbitrary",)),
        name="experts",
    )(cstart, nused, valid, w1, w3, w2, xg)


def _sc_gather(x, idx):
    num = idx.shape[0]
    mesh = plsc.VectorSubcoreMesh(core_axis_name="core", subcore_axis_name="subcore")

    @pl.kernel(out_type=jax.ShapeDtypeStruct((num, x.shape[1]), x.dtype), mesh=mesh)
    def gather(x_hbm, i_hbm, o_hbm):
        def body(i_vmem, o_vmem):
            pltpu.sync_copy(x_hbm.at[i_vmem.at[0]], o_vmem)

        pltpu.emit_pipeline(
            body,
            grid=(num // SC_WINDOW,),
            in_specs=[pl.BlockSpec((1, SC_WINDOW), index_map=lambda i: (0, i))],
            out_specs=[pl.BlockSpec((SC_WINDOW, x.shape[1]), index_map=lambda i: (i, 0))],
            core_axis_name=("core", "subcore"),
            dimension_semantics=(pltpu.PARALLEL,),
        )(i_hbm, o_hbm)

    return gather(x, idx.reshape(1, num))


def _combine_kernel(yg_ref, gate_ref, x1s_ref, g2_ref, fg_ref, *rest):
    out_ref = rest[-1]
    gates = gate_ref[...]
    nch = yg_ref.shape[0]
    r_lo = [None] * nch
    r_hi = [None] * nch
    for k in range(TOP_K):
        gk = gates[:, k:k + 1]
        for c in range(nch):
            lo, hi = _unpack_rows(yg_ref[c, k])
            r_lo[c] = gk * lo if k == 0 else r_lo[c] + gk * lo
            r_hi[c] = gk * hi if k == 0 else r_hi[c] + gk * hi
    routed = jnp.concatenate(r_lo + r_hi, axis=1)
    x2 = x1s_ref[...] + g2_ref[...] * routed
    out_ref[...] = _rms(x2, fg_ref[...])


def _combine(yg, gates, x1s, mod3, final_g, s, part, prev_out):
    n, d = x1s.shape
    t = COMB_T
    tpb = s // t
    nch, _, npart, _ = yg.shape
    off = part * (npart // t)
    in_specs = [pl.BlockSpec((nch, TOP_K, t, LANES), lambda i: (0, 0, i, 0)),
                pl.BlockSpec((t, TOP_K), lambda i: (i + off, 0)),
                pl.BlockSpec((t, d), lambda i: (i + off, 0)),
                pl.BlockSpec((None, 1, d), lambda i: (((i + off) // tpb) * 6 + 5, 0, 0)),
                pl.BlockSpec((1, d), lambda i: (0, 0))]
    args = [yg, gates, x1s, mod3, final_g]
    aliases = {}
    if prev_out is not None:
        in_specs.append(pl.BlockSpec(memory_space=pl.ANY))
        args.append(prev_out)
        aliases = {len(args) - 1: 0}
    return pl.pallas_call(
        _combine_kernel,
        grid=(npart // t,),
        in_specs=in_specs,
        out_specs=pl.BlockSpec((t, d), lambda i: (i + off, 0)),
        out_shape=jax.ShapeDtypeStruct((n, d), F32),
        input_output_aliases=aliases,
        compiler_params=_cparams(("parallel",)),
        name="combine",
    )(*args)


def _place_heads(w, per_head, keep):
    r = w.shape[0]
    w = w.reshape(r, B_HEADS, per_head)[:, :, :keep]
    return jnp.pad(w, ((0, 0), (0, 0), (0, HEAD_PAD - keep))).reshape(r, B_HEADS * HEAD_PAD)


def _rope_tables(s):
    inv = ROPE_THETA ** (-jnp.arange(0, B_ROPE, 2, dtype=jnp.float32) / B_ROPE)
    ang = jnp.arange(s, dtype=jnp.float32)[:, None] * inv[None, :]
    cos, sin = jnp.cos(ang), jnp.sin(ang)
    ones = jnp.ones((s, B_NOPE), F32)
    zeros = jnp.zeros((s, B_NOPE), F32)
    tail1 = jnp.ones((s, HEAD_PAD - B_NOPE - B_ROPE), F32)
    tail0 = jnp.zeros((s, HEAD_PAD - B_NOPE - B_ROPE), F32)
    return (jnp.concatenate([ones, cos, cos, tail1], axis=1),
            jnp.concatenate([zeros, -sin, sin, tail0], axis=1))


def kernel(x, c, w_ada, b_ada, norm1_g, w_in, q_norm_g, w_uq, kv_norm_g, w_ukv, rel_table, a_out_g, b_out_g,
           w_o, norm2_g, w_router, e_bias, w1, w3, w2, ws1, ws3, ws2, final_g):
    b, s, d = x.shape
    n = b * s
    assert w_ada.shape[0] == 1, "single layer"
    x2 = x.reshape(n, d)

    mod = _ada(c, w_ada[0], b_ada[0])
    mod3 = mod.reshape(b * 6, 1, d)
    bias = _bias_tiles(rel_table, _dil_geometry(s))

    wi = w_in[0]
    c_kpe = 3 * A_WIDTH + Q_LORA + KV_LORA
    kpe_cols = jnp.pad(wi[:, c_kpe:], ((0, 0), (B_NOPE, HEAD_PAD - B_NOPE - B_ROPE)))
    w_in_ext = jnp.concatenate([wi[:, :c_kpe], kpe_cols], axis=1).astype(BF16)
    w_uq_p = _place_heads(w_uq[0], B_NOPE + B_ROPE, B_NOPE + B_ROPE).astype(BF16)
    w_uk_p = _place_heads(w_ukv[0], B_NOPE + B_VDIM, B_NOPE).astype(BF16)
    w_v = w_ukv[0].reshape(KV_LORA, B_HEADS, B_NOPE + B_VDIM)[:, :, B_NOPE:].reshape(KV_LORA, B_WIDTH).astype(BF16)
    cos_t, sin_t = _rope_tables(s)

    qa, ka, va, qb, kb, vb = _inproj(x2, mod3, norm1_g, w_in_ext, q_norm_g, w_uq_p, kv_norm_g, w_uk_p, w_v,
                                     cos_t, sin_t, b, s)
    a_out = _dilated(qa.reshape(b, s, A_WIDTH), ka.reshape(b, s, A_WIDTH), va.reshape(b, s, A_WIDTH), bias)
    b_out, w1_bf, w3_bf, w2_bf = _mla(qb, kb, vb, w1[0], w3[0], w2[0])

    w_router_t = _slab_order(w_router[0].T)
    x1s, h2p, scores_t = _outproj(x2, a_out.reshape(n, A_WIDTH), b_out.reshape(n, B_WIDTH), a_out_g, b_out_g,
                                  w_o[0].astype(BF16), mod3, norm2_g, w_router_t,
                                  ws1[0].astype(BF16), ws3[0].astype(BF16), ws2[0].astype(BF16), s)

    eidx_t, gates_t, counts_slab = _route(scores_t, _slab_order(e_bias.reshape(N_EXPERTS, 1)))

    counts = _expert_order(counts_slab)[:, 0]
    padded = (counts + MOE_BLK - 1) // MOE_BLK * MOE_BLK
    pends = jnp.cumsum(padded)
    pstart = pends - padded
    nk = n * TOP_K
    nblk = -(-(nk + N_EXPERTS * (MOE_BLK - 1)) // MOE_BLK)
    rows_total = nblk * MOE_BLK
    blk_row = jnp.arange(nblk, dtype=I32) * MOE_BLK
    blk_e = jnp.minimum(jnp.sum((pends[None, :] <= blk_row[:, None]).astype(I32), axis=1), N_EXPERTS - 1)
    seg_end = (pstart + counts)[blk_e]
    valid = jnp.clip(seg_end - blk_row, 0, MOE_BLK).astype(I32)
    nused = (pends[-1] // MOE_BLK).astype(I32).reshape(1)
    cstart = jnp.concatenate([jnp.zeros((1,), I32), (pends // MOE_BLK).astype(I32)])

    dest_t = _dest(eidx_t, _slab_order(pstart.astype(F32).reshape(N_EXPERTS, 1)))
    nplane = h2p.shape[0]
    gidx = (dest_t.reshape(1, nk) + (jnp.arange(nplane, dtype=I32) * rows_total)[:, None]).reshape(nplane * nk)
    nwin = n // SC_WINDOW
    sidx = dest_t.reshape(TOP_K, nwin, SC_WINDOW).transpose(1, 0, 2)[None] + (
        jnp.arange(nplane, dtype=I32) * rows_total)[:, None, None, None]
    xg = _sc_scatter(h2p.reshape(nplane * n, LANES), sidx.reshape(nplane * nk), nplane * rows_total, nplane, TOP_K)
    y = _experts(cstart, nused, valid, xg.reshape(nplane, rows_total, LANES), w1_bf, w3_bf, w2_bf)
    y_flat = y.reshape(nplane * rows_total, LANES)
    gidx3 = gidx.reshape(nplane, TOP_K, n)
    gates = gates_t.T
    npart = n // COMB_SPLIT
    out = None
    for part in range(COMB_SPLIT):
        pidx = gidx3[:, :, part * npart:(part + 1) * npart].reshape(nplane * TOP_K * npart)
        yg = _sc_gather(y_flat, pidx).reshape(nplane, TOP_K, npart, LANES)
        out = _combine(yg, gates, x1s, mod3, final_g.reshape(1, d), s, part, out)
    return out.reshape(b, s, d)
```

```python
import functools
import math

import jax
import jax.numpy as jnp
from jax import lax
from jax.experimental import pallas as pl
from jax.experimental.pallas import tpu as pltpu
from jax.experimental.pallas import tpu_sc as plsc

F32 = jnp.float32
BF16 = jnp.bfloat16
U32 = jnp.uint32
I32 = jnp.int32
HIGHEST = lax.Precision.HIGHEST

D_MODEL = 1024
A_HEADS = 8
A_HEAD_DIM = 64
A_WIDTH = A_HEADS * A_HEAD_DIM
A_PATTERNS = ((128, 1), (512, 4), (2048, 16))
A_RADIUS = 64
REL_BUCKETS = 32
REL_MAX_DIST = 1024
B_HEADS = 8
B_NOPE = 64
B_ROPE = 32
B_VDIM = 64
B_WIDTH = B_HEADS * B_VDIM
Q_LORA = 384
KV_LORA = 256
ROPE_THETA = 10000.0
N_EXPERTS = 256
TOP_K = 8
N_GROUPS = 8
GROUP_SIZE = N_EXPERTS // N_GROUPS
TOPK_GROUPS = 4
EXPERT_FF = 256
SHARED_FF = 256
ROUTED_SCALE = 2.5
EPS = 1e-6
NEG_INF = -1e30
LOG2E = math.log2(math.e)

LANES = 128
SUBLANES = 8
HEAD_PAD = 128
IN_COLS_EXT = 3 * A_WIDTH + Q_LORA + KV_LORA + HEAD_PAD

TM_INPROJ = 512
TM_OUTPROJ = 1024
DIL_QB = 128
DIL_KW = DIL_QB + 2 * A_RADIUS
DIL_ITER_ROWS = 1024
MLA_TQ = 1024
MLA_KC = 2048
ROUTE_T = 512
MOE_BLK = 512
EXP_AHEAD = 3
EXP_XBUF = EXP_AHEAD + 1
SC_WINDOW = 128
COMB_T = 256
COMB_SPLIT = 4
VMEM_LIMIT = 56 * 1024 * 1024


def _cparams(sem):
    return pltpu.CompilerParams(dimension_semantics=sem, vmem_limit_bytes=VMEM_LIMIT)


def _rms(x, g):
    return x * lax.rsqrt(jnp.mean(x * x, axis=-1, keepdims=True) + EPS) * g


def _silu(x):
    return x * jax.nn.sigmoid(x)


def _split_bf16(x):
    hi = lax.bitcast_convert_type(lax.bitcast_convert_type(x, U32) & jnp.uint32(0xFFFF0000), F32)
    return hi.astype(BF16), (x - hi).astype(BF16)


def _pack_rows(x):
    half = x.shape[1] // 2
    bits = lax.bitcast_convert_type(x.astype(BF16).astype(F32), U32)
    return (bits[:, :half] >> 16) | bits[:, half:]


def _unpack_rows(w):
    lo = lax.bitcast_convert_type(w << 16, F32)
    hi = lax.bitcast_convert_type(w & jnp.uint32(0xFFFF0000), F32)
    return lo, hi


def _ada_kernel(c_ref, w_ref, b_ref, o_ref):
    o_ref[...] = jnp.dot(_silu(c_ref[...]), w_ref[...], precision=HIGHEST,
                         preferred_element_type=F32) + b_ref[...]


def _ada(c, w_ada, b_ada):
    b, d = c.shape
    n6 = w_ada.shape[1] // d
    return pl.pallas_call(
        _ada_kernel,
        grid=(n6,),
        in_specs=[pl.BlockSpec((b, d), lambda j: (0, 0)),
                  pl.BlockSpec((d, d), lambda j: (0, j)),
                  pl.BlockSpec((1, d), lambda j: (0, j))],
        out_specs=pl.BlockSpec((b, d), lambda j: (0, j)),
        out_shape=jax.ShapeDtypeStruct((b, n6 * d), F32),
        compiler_params=_cparams(("parallel",)),
        name="ada",
    )(c, w_ada, b_ada.reshape(1, -1))


def _t5_bucket(rel):
    half = REL_BUCKETS // 2
    max_exact = half // 2
    ret = jnp.where(rel > 0, half, 0)
    n = jnp.abs(rel)
    nf = jnp.maximum(n, 1).astype(jnp.float32)
    large = max_exact + (jnp.log(nf / max_exact) / math.log(REL_MAX_DIST / max_exact)
                         * (half - max_exact)).astype(jnp.int32)
    large = jnp.minimum(large, half - 1)
    return ret + jnp.where(n < max_exact, n, large)


def _dil_geometry(s):
    geo, base = [], 0
    for _, dil in A_PATTERNS:
        sub_len = s // dil
        qb = sub_len if sub_len <= DIL_KW else DIL_QB
        kw = min(qb + 2 * A_RADIUS, sub_len)
        assert sub_len % qb == 0 and (sub_len == qb or kw == qb + 2 * A_RADIUS)
        shifts = (A_RADIUS,) if sub_len == qb else (A_RADIUS, 0, -A_RADIUS)
        geo.append((dil, qb, kw, shifts, base))
        base += len(shifts)
    return geo


def _bucket_tiles(geo):
    qi = jnp.arange(max(g[1] for g in geo), dtype=jnp.int32)[:, None]
    ki = jnp.arange(max(g[2] for g in geo), dtype=jnp.int32)[None, :]
    tiles = []
    for dilation, qb, kw, shifts, _ in geo:
        for shift in shifts:
            off = ki + shift - A_RADIUS - qi
            bkt = _t5_bucket(off * dilation)
            inside = (jnp.abs(off) <= A_RADIUS) & (qi < qb) & (ki < kw)
            tiles.append(jnp.where(inside, bkt, -1))
    return jnp.stack(tiles, axis=0)


def _bias_kernel(tab_ref, bkt_ref, o_ref):
    bkt = bkt_ref[0]
    for h in range(A_HEADS):
        acc = jnp.full(bkt.shape, NEG_INF, F32)
        for b in range(REL_BUCKETS):
            acc = jnp.where(bkt == b, tab_ref[b, h] * LOG2E, acc)
        o_ref[0, h] = acc


def _bias_tiles(rel_table, geo):
    bkt = _bucket_tiles(geo)
    nt = bkt.shape[0]
    return pl.pallas_call(
        _bias_kernel,
        grid=(nt,),
        in_specs=[pl.BlockSpec(memory_space=pltpu.SMEM),
                  pl.BlockSpec((1,) + bkt.shape[1:], lambda t: (t, 0, 0))],
        out_specs=pl.BlockSpec((1, A_HEADS) + bkt.shape[1:], lambda t: (t, 0, 0, 0)),
        out_shape=jax.ShapeDtypeStruct((nt, A_HEADS) + bkt.shape[1:], F32),
        compiler_params=_cparams(("parallel",)),
        name="bias",
    )(rel_table, bkt)


def _head_lanes():
    half = B_ROPE // 2
    lanes = []
    for j in range(B_NOPE):
        lanes.append(half + j if j < HEAD_PAD // 2 - half else HEAD_PAD // 2 + half + (j - (HEAD_PAD // 2 - half)))
    for r in range(B_ROPE):
        lanes.append(r if r < half else HEAD_PAD // 2 + (r - half))
    return lanes


def _rope(x, cos, sin):
    return x * cos + pltpu.roll(x, HEAD_PAD // 2, 1) * sin


def _inproj_kernel(x_ref, sc_ref, sh_ref, g1_ref, win_ref, qg_ref, wuq_ref, kvg_ref, wuk_ref, wv_ref,
                   cos_ref, sin_ref, qa_ref, ka_ref, va_ref, qb_ref, kb_ref, vb_ref):
    x = x_ref[...]
    h = _rms(x, g1_ref[...]) * (1.0 + sc_ref[...]) + sh_ref[...]
    proj = jnp.dot(h.astype(BF16), win_ref[...], preferred_element_type=F32)
    aw = A_WIDTH
    qa_ref[...] = proj[:, 0:aw] * (LOG2E / math.sqrt(A_HEAD_DIM))
    ka_ref[...] = proj[:, aw:2 * aw]
    va_ref[...] = proj[:, 2 * aw:3 * aw]
    c0 = 3 * aw
    q_lat = proj[:, c0:c0 + Q_LORA]
    kv_lat = proj[:, c0 + Q_LORA:c0 + Q_LORA + KV_LORA]
    kpe = proj[:, c0 + Q_LORA + KV_LORA:]
    qn = _rms(q_lat, qg_ref[...]).astype(BF16)
    kvn = _rms(kv_lat, kvg_ref[...]).astype(BF16)
    qm = jnp.dot(qn, wuq_ref[...], preferred_element_type=F32)
    kn = jnp.dot(kvn, wuk_ref[...], preferred_element_type=F32)
    vv = jnp.dot(kvn, wv_ref[...], preferred_element_type=F32)
    cos = cos_ref[...]
    sin = sin_ref[...]
    qscale = LOG2E / math.sqrt(B_NOPE + B_ROPE)
    cos_q = cos * qscale
    sin_q = sin * qscale
    kpe_rot = _rope(kpe, cos, sin)
    for hd in range(B_HEADS):
        sl = slice(hd * HEAD_PAD, (hd + 1) * HEAD_PAD)
        qb_ref[hd] = _rope(qm[:, sl], cos_q, sin_q).astype(BF16)
        kb_ref[hd] = (kn[:, sl] + kpe_rot).astype(BF16)
    for p in range(B_HEADS // 2):
        vb_ref[p] = vv[:, p * LANES:(p + 1) * LANES].astype(BF16)


def _inproj(x2, mod3, norm1_g, w_in_ext, q_norm_g, w_uq_p, kv_norm_g, w_uk_p, w_v, cos_t, sin_t, b, s):
    n, d = x2.shape
    tm = TM_INPROJ
    tpb = s // tm
    row = lambda i: (i, 0)
    const = lambda i: (0, 0)
    hm = lambda i: (i // tpb, 0, i % tpb, 0)
    return pl.pallas_call(
        _inproj_kernel,
        grid=(n // tm,),
        in_specs=[pl.BlockSpec((tm, d), row),
                  pl.BlockSpec((None, 1, d), lambda i: ((i // tpb) * 6 + 1, 0, 0)),
                  pl.BlockSpec((None, 1, d), lambda i: ((i // tpb) * 6 + 0, 0, 0)),
                  pl.BlockSpec((1, d), const),
                  pl.BlockSpec(w_in_ext.shape, const),
                  pl.BlockSpec((1, Q_LORA), const),
                  pl.BlockSpec(w_uq_p.shape, const),
                  pl.BlockSpec((1, KV_LORA), const),
                  pl.BlockSpec(w_uk_p.shape, const),
                  pl.BlockSpec(w_v.shape, const),
                  pl.BlockSpec((tm, HEAD_PAD), lambda i: (i % tpb, 0)),
                  pl.BlockSpec((tm, HEAD_PAD), lambda i: (i % tpb, 0))],
        out_specs=[pl.BlockSpec((tm, A_WIDTH), row),
                   pl.BlockSpec((tm, A_WIDTH), row),
                   pl.BlockSpec((tm, A_WIDTH), row),
                   pl.BlockSpec((None, B_HEADS, tm, HEAD_PAD), hm),
                   pl.BlockSpec((None, B_HEADS, tm, HEAD_PAD), hm),
                   pl.BlockSpec((None, B_HEADS // 2, tm, LANES), hm)],
        out_shape=[jax.ShapeDtypeStruct((n, A_WIDTH), F32),
                   jax.ShapeDtypeStruct((n, A_WIDTH), F32),
                   jax.ShapeDtypeStruct((n, A_WIDTH), F32),
                   jax.ShapeDtypeStruct((b, B_HEADS, s, HEAD_PAD), BF16),
                   jax.ShapeDtypeStruct((b, B_HEADS, s, HEAD_PAD), BF16),
                   jax.ShapeDtypeStruct((b, B_HEADS // 2, s, LANES), BF16)],
        compiler_params=_cparams(("parallel",)),
        name="inproj",
    )(x2, mod3, mod3, norm1_g, w_in_ext, q_norm_g, w_uq_p, kv_norm_g, w_uk_p, w_v, cos_t, sin_t)


def _dil_block(q_ref, k_ref, v_ref, bias_ref, o_scr, l_scr, pi, geo, nblk, job):
    dil, qb, kwin, shifts, tile0 = geo
    sub_len = nblk * qb
    r = job // nblk
    bi = job % nblk
    q0 = bi * qb
    ws = jnp.clip(q0 - A_RADIUS, 0, sub_len - kwin)
    var = jnp.where(bi == 0, 0, jnp.where(bi == nblk - 1, 2, 1)) if len(shifts) > 1 else 0
    if dil == 1:
        qsl = pl.ds(pl.multiple_of(q0, qb), qb)
        ksl = pl.ds(pl.multiple_of(ws, A_RADIUS), kwin)
    else:
        qsl = pl.ds(r + dil * q0, qb, stride=dil)
        ksl = pl.ds(r + dil * ws, kwin, stride=dil)
    q = q_ref[qsl, :]
    kw = k_ref[ksl, :].astype(BF16)
    vw = v_ref[ksl, :].astype(BF16)
    lo = lax.broadcasted_iota(jnp.int32, q.shape, 1) < A_HEAD_DIM
    outs, lses = [], []
    for hh in range(2):
        qm = jnp.where(lo if hh == 0 else jnp.logical_not(lo), q, 0.0).astype(BF16)
        sc = lax.dot_general(qm, kw, (((1,), (1,)), ((), ())), preferred_element_type=F32)
        sc = sc + bias_ref[tile0 + var, hh, :qb, :kwin]
        m = jnp.max(sc, axis=-1, keepdims=True)
        p = jnp.exp2(sc - m)
        l = jnp.sum(p, axis=-1, keepdims=True)
        o = jnp.dot(p.astype(BF16), vw, preferred_element_type=F32)
        outs.append(o / l)
        lses.append(m + jnp.log2(l))
    o_scr[pi, qsl, :] = jnp.where(lo, outs[0], outs[1])
    l_scr[pi, qsl, :] = jnp.where(lo, lses[0], lses[1])


def _dilated_kernel(q_ref, k_ref, v_ref, bias_ref, out_ref, o_scr, l_scr):
    s = q_ref.shape[0]
    for pi, geo in enumerate(_dil_geometry(s)):
        dil, qb = geo[0], geo[1]
        njobs = s // qb
        unroll = DIL_ITER_ROWS // qb
        assert njobs % unroll == 0
        blk = functools.partial(_dil_block, q_ref, k_ref, v_ref, bias_ref, o_scr, l_scr, pi, geo,
                                s // dil // qb)

        def group(g, c, blk=blk, unroll=unroll):
            for u in range(unroll):
                blk(g * unroll + u)
            return c

        lax.fori_loop(0, njobs // unroll, group, 0)

    chunk = 512

    def comb(i, c):
        rows = pl.ds(pl.multiple_of(i * chunk, chunk), chunk)
        l0, l1, l2 = l_scr[0, rows, :], l_scr[1, rows, :], l_scr[2, rows, :]
        mx = jnp.maximum(jnp.maximum(l0, l1), l2)
        e0, e1, e2 = jnp.exp2(l0 - mx), jnp.exp2(l1 - mx), jnp.exp2(l2 - mx)
        num = e0 * o_scr[0, rows, :] + e1 * o_scr[1, rows, :] + e2 * o_scr[2, rows, :]
        out_ref[rows, :] = num / (e0 + e1 + e2)
        return c

    lax.fori_loop(0, s // chunk, comb, 0)


def _dilated(qa, ka, va, bias):
    b, s, _ = qa.shape
    npair = A_WIDTH // LANES
    blk = pl.BlockSpec((None, s, LANES), lambda bi, p: (bi, 0, p))
    return pl.pallas_call(
        _dilated_kernel,
        grid=(b, npair),
        in_specs=[blk, blk, blk,
                  pl.BlockSpec((bias.shape[0], 2) + bias.shape[2:], lambda bi, p: (0, p, 0, 0))],
        out_specs=blk,
        out_shape=jax.ShapeDtypeStruct((b, s, A_WIDTH), F32),
        scratch_shapes=[pltpu.VMEM((len(A_PATTERNS), s, LANES), F32),
                        pltpu.VMEM((len(A_PATTERNS), s, LANES), F32)],
        compiler_params=_cparams(("parallel", "parallel")),
        name="dilated",
    )(qa, ka, va, bias)


def _mla_kernel(q_ref, k_ref, v_ref, w1_ref, w3_ref, w2_ref, o_ref, w1b_ref, w3b_ref, w2b_ref):
    w1b_ref[...] = w1_ref[...].astype(BF16)
    w3b_ref[...] = w3_ref[...].astype(BF16)
    w2b_ref[...] = w2_ref[...].astype(BF16)
    tq = q_ref.shape[1]
    nkc = k_ref.shape[1] // MLA_KC
    outs = []
    for hh in range(2):
        q = q_ref[hh]
        m = jnp.full((tq, 1), -jnp.inf, F32)
        l = jnp.zeros((tq, 1), F32)
        acc = jnp.zeros((tq, LANES), F32)
        for c in range(nkc):
            keys = slice(c * MLA_KC, (c + 1) * MLA_KC)
            sc = lax.dot_general(q, k_ref[hh, keys, :], (((1,), (1,)), ((), ())), preferred_element_type=F32)
            m_new = jnp.maximum(m, jnp.max(sc, axis=-1, keepdims=True))
            alpha = jnp.exp2(m - m_new)
            p = jnp.exp2(sc - m_new)
            l = alpha * l + jnp.sum(p, axis=-1, keepdims=True)
            acc = alpha * acc + jnp.dot(p.astype(BF16), v_ref[keys, :], preferred_element_type=F32)
            m = m_new
        outs.append(acc / l)
    lo = lax.broadcasted_iota(jnp.int32, outs[0].shape, 1) < B_VDIM
    o_ref[...] = jnp.where(lo, outs[0], outs[1])


def _mla(qb, kb, vb, w1, w3, w2):
    b, h, s, _ = qb.shape
    npair = h // 2
    tq = MLA_TQ
    nq = s // tq
    nsteps = b * npair * nq
    ne, d, f = w1.shape
    assert ne % nsteps == 0
    epb = ne // nsteps
    wmap = lambda bi, p, qi: ((bi * npair + p) * nq + qi, 0, 0)
    return pl.pallas_call(
        _mla_kernel,
        grid=(b, npair, nq),
        in_specs=[pl.BlockSpec((None, 2, tq, HEAD_PAD), lambda bi, p, qi: (bi, p, qi, 0)),
                  pl.BlockSpec((None, 2, s, HEAD_PAD), lambda bi, p, qi: (bi, p, 0, 0)),
                  pl.BlockSpec((None, None, s, LANES), lambda bi, p, qi: (bi, p, 0, 0)),
                  pl.BlockSpec((epb, d, f), wmap),
                  pl.BlockSpec((epb, d, f), wmap),
                  pl.BlockSpec((epb, f, d), wmap)],
        out_specs=[pl.BlockSpec((None, tq, LANES), lambda bi, p, qi: (bi, qi, p)),
                   pl.BlockSpec((epb, d, f), wmap),
                   pl.BlockSpec((epb, d, f), wmap),
                   pl.BlockSpec((epb, f, d), wmap)],
        out_shape=[jax.ShapeDtypeStruct((b, s, B_WIDTH), F32),
                   jax.ShapeDtypeStruct((ne, d, f), BF16),
                   jax.ShapeDtypeStruct((ne, d, f), BF16),
                   jax.ShapeDtypeStruct((ne, f, d), BF16)],
        compiler_params=_cparams(("parallel", "parallel", "arbitrary")),
        name="mla",
    )(qb, kb, vb, w1, w3, w2)


def _outproj_kernel(x_ref, a_ref, b_ref, ag_ref, bg_ref, wo_ref, g1_ref, n2_ref, sc_ref, sh_ref, g2_ref,
                    wrt_ref, ws1_ref, ws3_ref, ws2_ref, x1s_ref, h2p_ref, scores_ref):
    an = _rms(a_ref[...], ag_ref[...])
    bn = _rms(b_ref[...], bg_ref[...])
    mix = jnp.concatenate([an, bn], axis=-1).astype(BF16)
    x1 = x_ref[...] + g1_ref[...] * jnp.dot(mix, wo_ref[...], preferred_element_type=F32)
    h2 = _rms(x1, n2_ref[...]) * (1.0 + sc_ref[...]) + sh_ref[...]
    packed = _pack_rows(h2)
    for cg in range(h2p_ref.shape[0]):
        h2p_ref[cg] = packed[:, cg * LANES:(cg + 1) * LANES]
    h_hi, h_lo = _split_bf16(h2)
    logits = lax.dot_general(wrt_ref[...], jnp.concatenate([h_hi, h_lo, h_hi], axis=1),
                             (((1,), (1,)), ((), ())), preferred_element_type=F32)
    scores_ref[...] = jax.nn.sigmoid(logits)
    h2b = h2.astype(BF16)
    hid = _silu(jnp.dot(h2b, ws1_ref[...], preferred_element_type=F32)) * jnp.dot(
        h2b, ws3_ref[...], preferred_element_type=F32)
    shared = jnp.dot(hid.astype(BF16), ws2_ref[...], preferred_element_type=F32)
    x1s_ref[...] = x1 + g2_ref[...] * shared


def _outproj(x2, a_out, b_out, a_out_g, b_out_g, w_o, mod3, norm2_g, w_router_t, ws1, ws3, ws2, s):
    n, d = x2.shape
    tm = TM_OUTPROJ
    tpb = s // tm
    row = lambda i: (i, 0)
    const = lambda i: (0, 0)
    modspec = lambda j: pl.BlockSpec((None, 1, d), lambda i: ((i // tpb) * 6 + j, 0, 0))
    return pl.pallas_call(
        _outproj_kernel,
        grid=(n // tm,),
        in_specs=[pl.BlockSpec((tm, d), row),
                  pl.BlockSpec((tm, A_WIDTH), row),
                  pl.BlockSpec((tm, B_WIDTH), row),
                  pl.BlockSpec((1, A_WIDTH), const),
                  pl.BlockSpec((1, B_WIDTH), const),
                  pl.BlockSpec(w_o.shape, const),
                  modspec(2),
                  pl.BlockSpec((1, d), const),
                  modspec(4), modspec(3), modspec(5),
                  pl.BlockSpec(w_router_t.shape, const),
                  pl.BlockSpec(ws1.shape, const),
                  pl.BlockSpec(ws3.shape, const),
                  pl.BlockSpec(ws2.shape, const)],
        out_specs=[pl.BlockSpec((tm, d), row),
                   pl.BlockSpec((d // 2 // LANES, tm, LANES), lambda i: (0, i, 0)),
                   pl.BlockSpec((N_EXPERTS, tm), lambda i: (0, i))],
        out_shape=[jax.ShapeDtypeStruct((n, d), F32),
                   jax.ShapeDtypeStruct((d // 2 // LANES, n, LANES), U32),
                   jax.ShapeDtypeStruct((N_EXPERTS, n), F32)],
        compiler_params=_cparams(("parallel",)),
        name="outproj",
    )(x2, a_out, b_out, a_out_g, b_out_g, w_o, mod3, norm2_g, mod3, mod3, mod3, w_router_t, ws1, ws3, ws2)


def _slab_order(v):
    return v.reshape((N_GROUPS, GROUP_SIZE) + v.shape[1:]).swapaxes(0, 1).reshape(v.shape)


def _expert_order(v):
    return v.reshape((GROUP_SIZE, N_GROUPS) + v.shape[1:]).swapaxes(0, 1).reshape(v.shape)


def _sublane_all(x, op):
    for sh in (4, 2, 1):
        x = op(x, pltpu.roll(x, sh, 0))
    return x


def _route_kernel(st_ref, bias_ref, eidx_ref, gate_ref, cnt_ref):
    nsl = GROUP_SIZE
    t = st_ref.shape[1]
    sub = lax.broadcasted_iota(I32, (SUBLANES, t), 0)
    ninf = -jnp.inf
    big = jnp.int32(1 << 30)
    sc = [st_ref[j * SUBLANES:(j + 1) * SUBLANES, :] for j in range(nsl)]
    sel = [sc[j] + bias_ref[j * SUBLANES:(j + 1) * SUBLANES, :] for j in range(nsl)]
    eid = [sub * GROUP_SIZE + j for j in range(nsl)]

    m1 = sel[0]
    m2 = jnp.full_like(m1, ninf)
    for j in range(1, nsl):
        m2 = jnp.maximum(m2, jnp.minimum(m1, sel[j]))
        m1 = jnp.maximum(m1, sel[j])
    gs = m1 + m2

    rank = jnp.zeros((SUBLANES, t), I32)
    for sh in range(1, N_GROUPS):
        other = pltpu.roll(gs, sh, 0)
        ahead = (other > gs) | ((other == gs) & (sub >= sh))
        rank = rank + ahead.astype(I32)
    gmask = rank < TOPK_GROUPS

    msel = [jnp.where(gmask, sel[j], ninf) for j in range(nsl)]
    hits = [jnp.zeros((SUBLANES, t), I32) for _ in range(nsl)]
    eidx = jnp.zeros((TOP_K, t), I32)
    gates = jnp.zeros((TOP_K, t), F32)
    for k in range(TOP_K):
        mx = msel[0]
        for j in range(1, nsl):
            mx = jnp.maximum(mx, msel[j])
        mx = _sublane_all(mx, jnp.maximum)
        cand = jnp.where(msel[0] == mx, eid[0], big)
        for j in range(1, nsl):
            cand = jnp.minimum(cand, jnp.where(msel[j] == mx, eid[j], big))
        idx = _sublane_all(cand, jnp.minimum)
        gk = jnp.zeros((SUBLANES, t), F32)
        for j in range(nsl):
            hit = eid[j] == idx
            gk = gk + jnp.where(hit, sc[j], 0.0)
            msel[j] = jnp.where(hit, ninf, msel[j])
            hits[j] = hits[j] + hit.astype(I32)
        gk = _sublane_all(gk, jnp.add)
        eidx = jnp.where(sub == k, idx, eidx)
        gates = jnp.where(sub == k, gk, gates)
    gsum = _sublane_all(gates, jnp.add)
    eidx_ref[...] = eidx
    gate_ref[...] = gates / gsum * ROUTED_SCALE

    @pl.when(pl.program_id(0) == 0)
    def _():
        cnt_ref[...] = jnp.zeros_like(cnt_ref)

    for j in range(nsl):
        cnt_ref[j * SUBLANES:(j + 1) * SUBLANES, :] += jnp.sum(hits[j].astype(F32), axis=1,
                                                               keepdims=True).astype(I32)


def _route(scores_t, e_bias_slab):
    e, n = scores_t.shape
    t = ROUTE_T
    return pl.pallas_call(
        _route_kernel,
        grid=(n // t,),
        in_specs=[pl.BlockSpec((e, t), lambda i: (0, i)),
                  pl.BlockSpec((e, 1), lambda i: (0, 0))],
        out_specs=[pl.BlockSpec((TOP_K, t), lambda i: (0, i)),
                   pl.BlockSpec((TOP_K, t), lambda i: (0, i)),
                   pl.BlockSpec((e, 1), lambda i: (0, 0))],
        out_shape=[jax.ShapeDtypeStruct((TOP_K, n), I32),
                   jax.ShapeDtypeStruct((TOP_K, n), F32),
                   jax.ShapeDtypeStruct((e, 1), I32)],
        compiler_params=_cparams(("arbitrary",)),
        name="route",
    )(scores_t, e_bias_slab)


def _dest_kernel(eidx_ref, pstart_ref, dest_ref, carry_ref):
    @pl.when(pl.program_id(0) == 0)
    def _():
        carry_ref[...] = jnp.zeros_like(carry_ref)

    nsl = GROUP_SIZE
    t = eidx_ref.shape[1]
    sub = lax.broadcasted_iota(I32, (SUBLANES, t), 0)
    eid = [sub * GROUP_SIZE + j for j in range(nsl)]
    ek = [eidx_ref[k:k + 1, :] for k in range(TOP_K)]
    slabs = []
    for j in range(nsl):
        oh = jnp.zeros((SUBLANES, t), F32)
        for k in range(TOP_K):
            oh = oh + (eid[j] == ek[k]).astype(F32)
        slabs.append(oh)
    onehot = jnp.concatenate(slabs, axis=0)
    row = lax.broadcasted_iota(I32, (t, t), 0)
    col = lax.broadcasted_iota(I32, (t, t), 1)
    upper = (row < col).astype(BF16)
    before = jnp.dot(onehot.astype(BF16), upper, preferred_element_type=F32)
    base = before + carry_ref[...] + pstart_ref[...]
    dest = jnp.zeros((TOP_K, t), I32)
    for k in range(TOP_K):
        acc = jnp.zeros((SUBLANES, t), F32)
        for j in range(nsl):
            acc = acc + jnp.where(eid[j] == ek[k], base[j * SUBLANES:(j + 1) * SUBLANES, :], 0.0)
        dk = _sublane_all(acc, jnp.add).astype(I32)
        dest = jnp.where(sub == k, dk, dest)
    dest_ref[...] = dest
    carry_ref[...] += jnp.sum(onehot, axis=1, keepdims=True)


def _dest(eidx_t, pstart_slab):
    n = eidx_t.shape[1]
    t = ROUTE_T
    return pl.pallas_call(
        _dest_kernel,
        grid=(n // t,),
        in_specs=[pl.BlockSpec((TOP_K, t), lambda i: (0, i)),
                  pl.BlockSpec((N_EXPERTS, 1), lambda i: (0, 0))],
        out_specs=pl.BlockSpec((TOP_K, t), lambda i: (0, i)),
        out_shape=jax.ShapeDtypeStruct((TOP_K, n), I32),
        scratch_shapes=[pltpu.VMEM((N_EXPERTS, 1), F32)],
        compiler_params=_cparams(("arbitrary",)),
        name="dest",
    )(eidx_t, pstart_slab)


def _sc_scatter(x, idx, rows_out, nplane, nslot):
    del nplane
    num = idx.shape[0]
    mesh = plsc.VectorSubcoreMesh(core_axis_name="core", subcore_axis_name="subcore")

    @pl.kernel(out_type=jax.ShapeDtypeStruct((rows_out, x.shape[1]), x.dtype), mesh=mesh, scratch_types=[])
    def scatter(x_hbm, i_hbm, o_hbm):
        def body(x_vmem, i_vmem):
            pltpu.sync_copy(x_vmem, o_hbm.at[i_vmem.at[0]])

        pltpu.emit_pipeline(
            body,
            grid=(num // SC_WINDOW,),
            in_specs=[pl.BlockSpec((SC_WINDOW, x.shape[1]), index_map=lambda i: (i // nslot, 0)),
                      pl.BlockSpec((1, SC_WINDOW), index_map=lambda i: (0, i))],
            out_specs=[],
            core_axis_name=("core", "subcore"),
            dimension_semantics=(pltpu.PARALLEL,),
        )(x_hbm, i_hbm)

    return scatter(x, idx.reshape(1, num))


def _expert_kernel(cstart_ref, nused_ref, valid_ref, w1_ref, w3_ref, w2_ref, xg_ref, y_ref,
                   xbuf, ybuf, xsem, ysem):
    e = pl.program_id(0)
    c0 = cstart_ref[e]
    c1 = cstart_ref[e + 1]
    nused = nused_ref[0]
    nchunks = xg_ref.shape[1] // MOE_BLK

    def rows(g):
        start = g * MOE_BLK
        return pl.ds(start if isinstance(g, int) else pl.multiple_of(start, MOE_BLK), MOE_BLK)

    nplane = xg_ref.shape[0]

    def xcopies(g):
        slot = g % EXP_XBUF
        return [pltpu.make_async_copy(xg_ref.at[pl.ds(0, nplane), rows(g)], xbuf.at[slot], xsem.at[slot])]

    def ycopies(g, slot):
        return [pltpu.make_async_copy(ybuf.at[slot], y_ref.at[pl.ds(0, nplane), rows(g)], ysem.at[slot])]

    @pl.when(e == 0)
    def _():
        for j in range(EXP_AHEAD):
            @pl.when(j < nused)
            def _():
                for cp in xcopies(j):
                    cp.start()

    @pl.when(c1 > c0)
    def _():
        def ffn(g):
            row_id = lax.broadcasted_iota(I32, (MOE_BLK, LANES), 0)
            keep = row_id < valid_ref[g]
            halves = [_unpack_rows(jnp.where(keep, xbuf[g % EXP_XBUF, c], jnp.uint32(0))) for c in range(nplane)]
            xb = jnp.concatenate([h[0] for h in halves] + [h[1] for h in halves], axis=1).astype(BF16)
            hid = _silu(jnp.dot(xb, w1_ref[...], preferred_element_type=F32)) * jnp.dot(
                xb, w3_ref[...], preferred_element_type=F32)
            return _pack_rows(jnp.dot(hid.astype(BF16), w2_ref[...], preferred_element_type=F32))

        def chunk(g, carry):
            @pl.when(g + EXP_AHEAD < nused)
            def _():
                for cp in xcopies(g + EXP_AHEAD):
                    cp.start()

            for cp in xcopies(g):
                cp.wait()
            yp = ffn(g)
            yslot = g % 2

            @pl.when(g >= 2)
            def _():
                for cp in ycopies(g - 2, yslot):
                    cp.wait()

            for c in range(nplane):
                ybuf[yslot, c] = yp[:, c * LANES:(c + 1) * LANES]
            for cp in ycopies(g, yslot):
                cp.start()
            return carry

        lax.fori_loop(c0, c1, chunk, 0)

    @pl.when(e == pl.num_programs(0) - 1)
    def _():
        for back in (2, 1):
            g = nused - back

            @pl.when(g >= 0)
            def _():
                for cp in ycopies(g, g % 2):
                    cp.wait()

        ybuf[0] = jnp.zeros(ybuf.shape[1:], ybuf.dtype)

        def zstart(g, c):
            for cp in ycopies(g, 0):
                cp.start()
            return c

        def zwait(g, c):
            for cp in ycopies(g, 0):
                cp.wait()
            return c

        lax.fori_loop(nused, nchunks, zstart, 0)
        lax.fori_loop(nused, nchunks, zwait, 0)


def _experts(cstart, nused, valid, xg, w1, w3, w2):
    nplane, rows, _ = xg.shape
    dh = nplane * LANES
    ne, d, f = w1.shape
    grid_spec = pltpu.PrefetchScalarGridSpec(
        num_scalar_prefetch=3,
        grid=(ne,),
        in_specs=[pl.BlockSpec((None, d, f), lambda e, cs, nu, va: (e, 0, 0)),
                  pl.BlockSpec((None, d, f), lambda e, cs, nu, va: (e, 0, 0)),
                  pl.BlockSpec((None, f, d), lambda e, cs, nu, va: (e, 0, 0)),
                  pl.BlockSpec(memory_space=pl.ANY)],
        out_specs=pl.BlockSpec(memory_space=pl.ANY),
        scratch_shapes=[pltpu.VMEM((EXP_XBUF, nplane, MOE_BLK, LANES), U32),
                        pltpu.VMEM((2, nplane, MOE_BLK, LANES), U32),
                        pltpu.SemaphoreType.DMA((EXP_XBUF,)), pltpu.SemaphoreType.DMA((2,))],
    )
    return pl.pallas_call(
        _expert_kernel,
        grid_spec=grid_spec,
        out_shape=jax.ShapeDtypeStruct((dh // LANES, rows, LANES), U32),
        compiler_params=_cparams(("arbitrary",)),
        name="experts",
    )(cstart, nused, valid, w1, w3, w2, xg)


def _sc_gather(x, idx):
    num = idx.shape[0]
    mesh = plsc.VectorSubcoreMesh(core_axis_name="core", subcore_axis_name="subcore")

    @pl.kernel(out_type=jax.ShapeDtypeStruct((num, x.shape[1]), x.dtype), mesh=mesh)
    def gather(x_hbm, i_hbm, o_hbm):
        def body(i_vmem, o_vmem):
            pltpu.sync_copy(x_hbm.at[i_vmem.at[0]], o_vmem)

        pltpu.emit_pipeline(
            body,
            grid=(num // SC_WINDOW,),
            in_specs=[pl.BlockSpec((1, SC_WINDOW), index_map=lambda i: (0, i))],
            out_specs=[pl.BlockSpec((SC_WINDOW, x.shape[1]), index_map=lambda i: (i, 0))],
            core_axis_name=("core", "subcore"),
            dimension_semantics=(pltpu.PARALLEL,),
        )(i_hbm, o_hbm)

    return gather(x, idx.reshape(1, num))


def _combine_kernel(yg_ref, gate_ref, x1s_ref, g2_ref, fg_ref, *rest):
    out_ref = rest[-1]
    gates = gate_ref[...]
    nch = yg_ref.shape[0]
    r_lo = [None] * nch
    r_hi = [None] * nch
    for k in range(TOP_K):
        gk = gates[:, k:k + 1]
        for c in range(nch):
            lo, hi = _unpack_rows(yg_ref[c, k])
            r_lo[c] = gk * lo if k == 0 else r_lo[c] + gk * lo
            r_hi[c] = gk * hi if k == 0 else r_hi[c] + gk * hi
    routed = jnp.concatenate(r_lo + r_hi, axis=1)
    x2 = x1s_ref[...] + g2_ref[...] * routed
    out_ref[...] = _rms(x2, fg_ref[...])


def _combine(yg, gates, x1s, mod3, final_g, s, part, prev_out):
    n, d = x1s.shape
    t = COMB_T
    tpb = s // t
    nch, _, npart, _ = yg.shape
    off = part * (npart // t)
    in_specs = [pl.BlockSpec((nch, TOP_K, t, LANES), lambda i: (0, 0, i, 0)),
                pl.BlockSpec((t, TOP_K), lambda i: (i + off, 0)),
                pl.BlockSpec((t, d), lambda i: (i + off, 0)),
                pl.BlockSpec((None, 1, d), lambda i: (((i + off) // tpb) * 6 + 5, 0, 0)),
                pl.BlockSpec((1, d), lambda i: (0, 0))]
    args = [yg, gates, x1s, mod3, final_g]
    aliases = {}
    if prev_out is not None:
        in_specs.append(pl.BlockSpec(memory_space=pl.ANY))
        args.append(prev_out)
        aliases = {len(args) - 1: 0}
    return pl.pallas_call(
        _combine_kernel,
        grid=(npart // t,),
        in_specs=in_specs,
        out_specs=pl.BlockSpec((t, d), lambda i: (i + off, 0)),
        out_shape=jax.ShapeDtypeStruct((n, d), F32),
        input_output_aliases=aliases,
        compiler_params=_cparams(("parallel",)),
        name="combine",
    )(*args)


def _place_cols(w, lanes):
    r, h, _ = w.shape
    out = jnp.zeros((r, h, HEAD_PAD), w.dtype).at[:, :, jnp.asarray(lanes)].set(w)
    return out.reshape(r, h * HEAD_PAD)


def _rope_tables(s):
    half = B_ROPE // 2
    rope_lanes = jnp.asarray(_head_lanes()[B_NOPE:])
    inv = ROPE_THETA ** (-jnp.arange(0, B_ROPE, 2, dtype=jnp.float32) / B_ROPE)
    ang = jnp.arange(s, dtype=jnp.float32)[:, None] * inv[None, :]
    cos, sin = jnp.cos(ang), jnp.sin(ang)
    assert cos.shape[1] == half
    cos_t = jnp.ones((s, HEAD_PAD), F32).at[:, rope_lanes].set(jnp.concatenate([cos, cos], axis=1))
    sin_t = jnp.zeros((s, HEAD_PAD), F32).at[:, rope_lanes].set(jnp.concatenate([-sin, sin], axis=1))
    return cos_t, sin_t


def kernel(x, c, w_ada, b_ada, norm1_g, w_in, q_norm_g, w_uq, kv_norm_g, w_ukv, rel_table, a_out_g, b_out_g,
           w_o, norm2_g, w_router, e_bias, w1, w3, w2, ws1, ws3, ws2, final_g):
    b, s, d = x.shape
    n = b * s
    assert w_ada.shape[0] == 1, "single layer"
    x2 = x.reshape(n, d)

    mod = _ada(c, w_ada[0], b_ada[0])
    mod3 = mod.reshape(b * 6, 1, d)
    bias = _bias_tiles(rel_table, _dil_geometry(s))

    wi = w_in[0]
    c_kpe = 3 * A_WIDTH + Q_LORA + KV_LORA
    lanes = _head_lanes()
    kpe_cols = _place_cols(wi[:, None, c_kpe:], lanes[B_NOPE:])
    w_in_ext = jnp.concatenate([wi[:, :c_kpe], kpe_cols], axis=1).astype(BF16)
    w_uq_p = _place_cols(w_uq[0].reshape(Q_LORA, B_HEADS, B_NOPE + B_ROPE), lanes).astype(BF16)
    w_uk_p = _place_cols(w_ukv[0].reshape(KV_LORA, B_HEADS, B_NOPE + B_VDIM)[:, :, :B_NOPE],
                         lanes[:B_NOPE]).astype(BF16)
    w_v = w_ukv[0].reshape(KV_LORA, B_HEADS, B_NOPE + B_VDIM)[:, :, B_NOPE:].reshape(KV_LORA, B_WIDTH).astype(BF16)
    cos_t, sin_t = _rope_tables(s)

    qa, ka, va, qb, kb, vb = _inproj(x2, mod3, norm1_g, w_in_ext, q_norm_g, w_uq_p, kv_norm_g, w_uk_p, w_v,
                                     cos_t, sin_t, b, s)
    a_out = _dilated(qa.reshape(b, s, A_WIDTH), ka.reshape(b, s, A_WIDTH), va.reshape(b, s, A_WIDTH), bias)
    b_out, w1_bf, w3_bf, w2_bf = _mla(qb, kb, vb, w1[0], w3[0], w2[0])

    wr = _slab_order(w_router[0].T)
    wr_hi, wr_lo = _split_bf16(wr)
    w_router_t = jnp.concatenate([wr_hi, wr_hi, wr_lo], axis=1)
    x1s, h2p, scores_t = _outproj(x2, a_out.reshape(n, A_WIDTH), b_out.reshape(n, B_WIDTH), a_out_g, b_out_g,
                                  w_o[0].astype(BF16), mod3, norm2_g, w_router_t,
                                  ws1[0].astype(BF16), ws3[0].astype(BF16), ws2[0].astype(BF16), s)

    eidx_t, gates_t, counts_slab = _route(scores_t, _slab_order(e_bias.reshape(N_EXPERTS, 1)))

    counts = _expert_order(counts_slab)[:, 0]
    padded = (counts + MOE_BLK - 1) // MOE_BLK * MOE_BLK
    pends = jnp.cumsum(padded)
    pstart = pends - padded
    nk = n * TOP_K
    nblk = -(-(nk + N_EXPERTS * (MOE_BLK - 1)) // MOE_BLK)
    rows_total = nblk * MOE_BLK
    blk_row = jnp.arange(nblk, dtype=I32) * MOE_BLK
    blk_e = jnp.minimum(jnp.sum((pends[None, :] <= blk_row[:, None]).astype(I32), axis=1), N_EXPERTS - 1)
    seg_end = (pstart + counts)[blk_e]
    valid = jnp.clip(seg_end - blk_row, 0, MOE_BLK).astype(I32)
    nused = (pends[-1] // MOE_BLK).astype(I32).reshape(1)
    cstart = jnp.concatenate([jnp.zeros((1,), I32), (pends // MOE_BLK).astype(I32)])

    dest_t = _dest(eidx_t, _slab_order(pstart.astype(F32).reshape(N_EXPERTS, 1)))
    nplane = h2p.shape[0]
    gidx = (dest_t.reshape(1, nk) + (jnp.arange(nplane, dtype=I32) * rows_total)[:, None]).reshape(nplane * nk)
    nwin = n // SC_WINDOW
    sidx = dest_t.reshape(TOP_K, nwin, SC_WINDOW).transpose(1, 0, 2)[None] + (
        jnp.arange(nplane, dtype=I32) * rows_total)[:, None, None, None]
    xg = _sc_scatter(h2p.reshape(nplane * n, LANES), sidx.reshape(nplane * nk), nplane * rows_total, nplane, TOP_K)
    y = _experts(cstart, nused, valid, xg.reshape(nplane, rows_total, LANES), w1_bf, w3_bf, w2_bf)
    y_flat = y.reshape(nplane * rows_total, LANES)
    gidx3 = gidx.reshape(nplane, TOP_K, n)
    gates = gates_t.T
    npart = n // COMB_SPLIT
    out = None
    for part in range(COMB_SPLIT):
        pidx = gidx3[:, :, part * npart:(part + 1) * npart].reshape(nplane * TOP_K * npart)
        yg = _sc_gather(y_flat, pidx).reshape(nplane, TOP_K, npart, LANES)
        out = _combine(yg, gates, x1s, mod3, final_g.reshape(1, d), s, part, out)
    return out.reshape(b, s, d)
```

```python
import functools
import math

import jax
import jax.numpy as jnp
from jax import lax
from jax.experimental import pallas as pl
from jax.experimental.pallas import tpu as pltpu
from jax.experimental.pallas import tpu_sc as plsc

F32 = jnp.float32
BF16 = jnp.bfloat16
U32 = jnp.uint32
I32 = jnp.int32
HIGHEST = lax.Precision.HIGHEST

D_MODEL = 1024
A_HEADS = 8
A_HEAD_DIM = 64
A_WIDTH = A_HEADS * A_HEAD_DIM
A_PATTERNS = ((128, 1), (512, 4), (2048, 16))
A_RADIUS = 64
REL_BUCKETS = 32
REL_MAX_DIST = 1024
B_HEADS = 8
B_NOPE = 64
B_ROPE = 32
B_VDIM = 64
B_WIDTH = B_HEADS * B_VDIM
Q_LORA = 384
KV_LORA = 256
ROPE_THETA = 10000.0
N_EXPERTS = 256
TOP_K = 8
N_GROUPS = 8
GROUP_SIZE = N_EXPERTS // N_GROUPS
TOPK_GROUPS = 4
EXPERT_FF = 256
SHARED_FF = 256
ROUTED_SCALE = 2.5
EPS = 1e-6
NEG_INF = -1e30
LOG2E = math.log2(math.e)

LANES = 128
SUBLANES = 8
HEAD_PAD = 128
IN_COLS_EXT = 3 * A_WIDTH + Q_LORA + KV_LORA + HEAD_PAD

TM_INPROJ = 512
TM_OUTPROJ = 1024
DIL_QB = 128
DIL_KW = DIL_QB + 2 * A_RADIUS
DIL_ITER_ROWS = 1024
MLA_TQ = 1024
MLA_KC = 2048
ROUTE_T = 512
MOE_BLK = 512
EXP_AHEAD = 3
EXP_XBUF = EXP_AHEAD + 1
SC_WINDOW = 128
COMB_T = 256
COMB_SPLIT = 8
VMEM_LIMIT = 56 * 1024 * 1024


def _cparams(sem):
    return pltpu.CompilerParams(dimension_semantics=sem, vmem_limit_bytes=VMEM_LIMIT)


def _rms(x, g):
    return x * lax.rsqrt(jnp.mean(x * x, axis=-1, keepdims=True) + EPS) * g


def _silu(x):
    return x * jax.nn.sigmoid(x)


def _split_bf16(x):
    hi = lax.bitcast_convert_type(lax.bitcast_convert_type(x, U32) & jnp.uint32(0xFFFF0000), F32)
    return hi.astype(BF16), (x - hi).astype(BF16)


def _pack_rows(x):
    half = x.shape[1] // 2
    bits = lax.bitcast_convert_type(x.astype(BF16).astype(F32), U32)
    return (bits[:, :half] >> 16) | bits[:, half:]


def _unpack_rows(w):
    lo = lax.bitcast_convert_type(w << 16, F32)
    hi = lax.bitcast_convert_type(w & jnp.uint32(0xFFFF0000), F32)
    return lo, hi


def _ada_kernel(c_ref, w_ref, b_ref, o_ref):
    o_ref[...] = jnp.dot(_silu(c_ref[...]), w_ref[...], precision=HIGHEST,
                         preferred_element_type=F32) + b_ref[...]


def _ada(c, w_ada, b_ada):
    b, d = c.shape
    n6 = w_ada.shape[1] // d
    return pl.pallas_call(
        _ada_kernel,
        grid=(n6,),
        in_specs=[pl.BlockSpec((b, d), lambda j: (0, 0)),
                  pl.BlockSpec((d, d), lambda j: (0, j)),
                  pl.BlockSpec((1, d), lambda j: (0, j))],
        out_specs=pl.BlockSpec((b, d), lambda j: (0, j)),
        out_shape=jax.ShapeDtypeStruct((b, n6 * d), F32),
        compiler_params=_cparams(("parallel",)),
        name="ada",
    )(c, w_ada, b_ada.reshape(1, -1))


def _t5_bucket(rel):
    half = REL_BUCKETS // 2
    max_exact = half // 2
    ret = jnp.where(rel > 0, half, 0)
    n = jnp.abs(rel)
    nf = jnp.maximum(n, 1).astype(jnp.float32)
    large = max_exact + (jnp.log(nf / max_exact) / math.log(REL_MAX_DIST / max_exact)
                         * (half - max_exact)).astype(jnp.int32)
    large = jnp.minimum(large, half - 1)
    return ret + jnp.where(n < max_exact, n, large)


def _dil_geometry(s):
    geo, base = [], 0
    for _, dil in A_PATTERNS:
        sub_len = s // dil
        qb = sub_len if sub_len <= DIL_KW else DIL_QB
        kw = min(qb + 2 * A_RADIUS, sub_len)
        assert sub_len % qb == 0 and (sub_len == qb or kw == qb + 2 * A_RADIUS)
        shifts = (A_RADIUS,) if sub_len == qb else (A_RADIUS, 0, -A_RADIUS)
        geo.append((dil, qb, kw, shifts, base))
        base += len(shifts)
    return geo


def _bucket_tiles(geo):
    qi = jnp.arange(max(g[1] for g in geo), dtype=jnp.int32)[:, None]
    ki = jnp.arange(max(g[2] for g in geo), dtype=jnp.int32)[None, :]
    tiles = []
    for dilation, qb, kw, shifts, _ in geo:
        for shift in shifts:
            off = ki + shift - A_RADIUS - qi
            bkt = _t5_bucket(off * dilation)
            inside = (jnp.abs(off) <= A_RADIUS) & (qi < qb) & (ki < kw)
            tiles.append(jnp.where(inside, bkt, -1))
    return jnp.stack(tiles, axis=0)


def _bias_kernel(tab_ref, bkt_ref, o_ref):
    bkt = bkt_ref[0]
    for h in range(A_HEADS):
        acc = jnp.full(bkt.shape, NEG_INF, F32)
        for b in range(REL_BUCKETS):
            acc = jnp.where(bkt == b, tab_ref[b, h] * LOG2E, acc)
        o_ref[0, h] = acc


def _bias_tiles(rel_table, geo):
    bkt = _bucket_tiles(geo)
    nt = bkt.shape[0]
    return pl.pallas_call(
        _bias_kernel,
        grid=(nt,),
        in_specs=[pl.BlockSpec(memory_space=pltpu.SMEM),
                  pl.BlockSpec((1,) + bkt.shape[1:], lambda t: (t, 0, 0))],
        out_specs=pl.BlockSpec((1, A_HEADS) + bkt.shape[1:], lambda t: (t, 0, 0, 0)),
        out_shape=jax.ShapeDtypeStruct((nt, A_HEADS) + bkt.shape[1:], F32),
        compiler_params=_cparams(("parallel",)),
        name="bias",
    )(rel_table, bkt)


def _head_lanes():
    half = B_ROPE // 2
    lanes = []
    for j in range(B_NOPE):
        lanes.append(half + j if j < HEAD_PAD // 2 - half else HEAD_PAD // 2 + half + (j - (HEAD_PAD // 2 - half)))
    for r in range(B_ROPE):
        lanes.append(r if r < half else HEAD_PAD // 2 + (r - half))
    return lanes


def _rope(x, cos, sin):
    return x * cos + pltpu.roll(x, HEAD_PAD // 2, 1) * sin


def _inproj_kernel(x_ref, sc_ref, sh_ref, g1_ref, win_ref, qg_ref, wuq_ref, kvg_ref, wuk_ref, wv_ref,
                   cos_ref, sin_ref, qa_ref, ka_ref, va_ref, qb_ref, kb_ref, vb_ref):
    x = x_ref[...]
    h = _rms(x, g1_ref[...]) * (1.0 + sc_ref[...]) + sh_ref[...]
    proj = jnp.dot(h.astype(BF16), win_ref[...], preferred_element_type=F32)
    aw = A_WIDTH
    qa_ref[...] = proj[:, 0:aw] * (LOG2E / math.sqrt(A_HEAD_DIM))
    ka_ref[...] = proj[:, aw:2 * aw]
    va_ref[...] = proj[:, 2 * aw:3 * aw]
    c0 = 3 * aw
    q_lat = proj[:, c0:c0 + Q_LORA]
    kv_lat = proj[:, c0 + Q_LORA:c0 + Q_LORA + KV_LORA]
    kpe = proj[:, c0 + Q_LORA + KV_LORA:]
    qn = _rms(q_lat, qg_ref[...]).astype(BF16)
    kvn = _rms(kv_lat, kvg_ref[...]).astype(BF16)
    qm = jnp.dot(qn, wuq_ref[...], preferred_element_type=F32)
    kn = jnp.dot(kvn, wuk_ref[...], preferred_element_type=F32)
    vv = jnp.dot(kvn, wv_ref[...], preferred_element_type=F32)
    cos = cos_ref[...]
    sin = sin_ref[...]
    qscale = LOG2E / math.sqrt(B_NOPE + B_ROPE)
    cos_q = cos * qscale
    sin_q = sin * qscale
    kpe_rot = _rope(kpe, cos, sin)
    for hd in range(B_HEADS):
        sl = slice(hd * HEAD_PAD, (hd + 1) * HEAD_PAD)
        qb_ref[hd] = _rope(qm[:, sl], cos_q, sin_q).astype(BF16)
        kb_ref[hd] = (kn[:, sl] + kpe_rot).astype(BF16)
    for p in range(B_HEADS // 2):
        vb_ref[p] = vv[:, p * LANES:(p + 1) * LANES].astype(BF16)


def _inproj(x2, mod3, norm1_g, w_in_ext, q_norm_g, w_uq_p, kv_norm_g, w_uk_p, w_v, cos_t, sin_t, b, s):
    n, d = x2.shape
    tm = TM_INPROJ
    tpb = s // tm
    row = lambda i: (i, 0)
    const = lambda i: (0, 0)
    hm = lambda i: (i // tpb, 0, i % tpb, 0)
    return pl.pallas_call(
        _inproj_kernel,
        grid=(n // tm,),
        in_specs=[pl.BlockSpec((tm, d), row),
                  pl.BlockSpec((None, 1, d), lambda i: ((i // tpb) * 6 + 1, 0, 0)),
                  pl.BlockSpec((None, 1, d), lambda i: ((i // tpb) * 6 + 0, 0, 0)),
                  pl.BlockSpec((1, d), const),
                  pl.BlockSpec(w_in_ext.shape, const),
                  pl.BlockSpec((1, Q_LORA), const),
                  pl.BlockSpec(w_uq_p.shape, const),
                  pl.BlockSpec((1, KV_LORA), const),
                  pl.BlockSpec(w_uk_p.shape, const),
                  pl.BlockSpec(w_v.shape, const),
                  pl.BlockSpec((tm, HEAD_PAD), lambda i: (i % tpb, 0)),
                  pl.BlockSpec((tm, HEAD_PAD), lambda i: (i % tpb, 0))],
        out_specs=[pl.BlockSpec((tm, A_WIDTH), row),
                   pl.BlockSpec((tm, A_WIDTH), row),
                   pl.BlockSpec((tm, A_WIDTH), row),
                   pl.BlockSpec((None, B_HEADS, tm, HEAD_PAD), hm),
                   pl.BlockSpec((None, B_HEADS, tm, HEAD_PAD), hm),
                   pl.BlockSpec((None, B_HEADS // 2, tm, LANES), hm)],
        out_shape=[jax.ShapeDtypeStruct((n, A_WIDTH), F32),
                   jax.ShapeDtypeStruct((n, A_WIDTH), F32),
                   jax.ShapeDtypeStruct((n, A_WIDTH), F32),
                   jax.ShapeDtypeStruct((b, B_HEADS, s, HEAD_PAD), BF16),
                   jax.ShapeDtypeStruct((b, B_HEADS, s, HEAD_PAD), BF16),
                   jax.ShapeDtypeStruct((b, B_HEADS // 2, s, LANES), BF16)],
        compiler_params=_cparams(("parallel",)),
        name="inproj",
    )(x2, mod3, mod3, norm1_g, w_in_ext, q_norm_g, w_uq_p, kv_norm_g, w_uk_p, w_v, cos_t, sin_t)


def _dil_block(q_ref, k_ref, v_ref, bias_ref, o_scr, l_scr, pi, geo, nblk, job):
    dil, qb, kwin, shifts, tile0 = geo
    sub_len = nblk * qb
    r = job // nblk
    bi = job % nblk
    q0 = bi * qb
    ws = jnp.clip(q0 - A_RADIUS, 0, sub_len - kwin)
    var = jnp.where(bi == 0, 0, jnp.where(bi == nblk - 1, 2, 1)) if len(shifts) > 1 else 0
    if dil == 1:
        qsl = pl.ds(pl.multiple_of(q0, qb), qb)
        ksl = pl.ds(pl.multiple_of(ws, A_RADIUS), kwin)
    else:
        qsl = pl.ds(r + dil * q0, qb, stride=dil)
        ksl = pl.ds(r + dil * ws, kwin, stride=dil)
    q = q_ref[qsl, :]
    kw = k_ref[ksl, :].astype(BF16)
    vw = v_ref[ksl, :].astype(BF16)
    lo = lax.broadcasted_iota(jnp.int32, q.shape, 1) < A_HEAD_DIM
    outs, lses = [], []
    for hh in range(2):
        qm = jnp.where(lo if hh == 0 else jnp.logical_not(lo), q, 0.0).astype(BF16)
        sc = lax.dot_general(qm, kw, (((1,), (1,)), ((), ())), preferred_element_type=F32)
        sc = sc + bias_ref[tile0 + var, hh, :qb, :kwin]
        m = jnp.max(sc, axis=-1, keepdims=True)
        p = jnp.exp2(sc - m)
        l = jnp.sum(p, axis=-1, keepdims=True)
        o = jnp.dot(p.astype(BF16), vw, preferred_element_type=F32)
        outs.append(o / l)
        lses.append(m + jnp.log2(l))
    o_scr[pi, qsl, :] = jnp.where(lo, outs[0], outs[1])
    l_scr[pi, qsl, :] = jnp.where(lo, lses[0], lses[1])


def _dilated_kernel(q_ref, k_ref, v_ref, bias_ref, out_ref, o_scr, l_scr):
    s = q_ref.shape[0]
    for pi, geo in enumerate(_dil_geometry(s)):
        dil, qb = geo[0], geo[1]
        njobs = s // qb
        unroll = DIL_ITER_ROWS // qb
        assert njobs % unroll == 0
        blk = functools.partial(_dil_block, q_ref, k_ref, v_ref, bias_ref, o_scr, l_scr, pi, geo,
                                s // dil // qb)

        def group(g, c, blk=blk, unroll=unroll):
            for u in range(unroll):
                blk(g * unroll + u)
            return c

        lax.fori_loop(0, njobs // unroll, group, 0)

    chunk = 512

    def comb(i, c):
        rows = pl.ds(pl.multiple_of(i * chunk, chunk), chunk)
        l0, l1, l2 = l_scr[0, rows, :], l_scr[1, rows, :], l_scr[2, rows, :]
        mx = jnp.maximum(jnp.maximum(l0, l1), l2)
        e0, e1, e2 = jnp.exp2(l0 - mx), jnp.exp2(l1 - mx), jnp.exp2(l2 - mx)
        num = e0 * o_scr[0, rows, :] + e1 * o_scr[1, rows, :] + e2 * o_scr[2, rows, :]
        out_ref[rows, :] = num / (e0 + e1 + e2)
        return c

    lax.fori_loop(0, s // chunk, comb, 0)


def _dilated(qa, ka, va, bias):
    b, s, _ = qa.shape
    npair = A_WIDTH // LANES
    blk = pl.BlockSpec((None, s, LANES), lambda bi, p: (bi, 0, p))
    return pl.pallas_call(
        _dilated_kernel,
        grid=(b, npair),
        in_specs=[blk, blk, blk,
                  pl.BlockSpec((bias.shape[0], 2) + bias.shape[2:], lambda bi, p: (0, p, 0, 0))],
        out_specs=blk,
        out_shape=jax.ShapeDtypeStruct((b, s, A_WIDTH), F32),
        scratch_shapes=[pltpu.VMEM((len(A_PATTERNS), s, LANES), F32),
                        pltpu.VMEM((len(A_PATTERNS), s, LANES), F32)],
        compiler_params=_cparams(("parallel", "parallel")),
        name="dilated",
    )(qa, ka, va, bias)


def _mla_kernel(q_ref, k_ref, v_ref, w1_ref, w3_ref, w2_ref, o_ref, w1b_ref, w3b_ref, w2b_ref):
    w1b_ref[...] = w1_ref[...].astype(BF16)
    w3b_ref[...] = w3_ref[...].astype(BF16)
    w2b_ref[...] = w2_ref[...].astype(BF16)
    tq = q_ref.shape[1]
    nkc = k_ref.shape[1] // MLA_KC
    outs = []
    for hh in range(2):
        q = q_ref[hh]
        m = jnp.full((tq, 1), -jnp.inf, F32)
        l = jnp.zeros((tq, 1), F32)
        acc = jnp.zeros((tq, LANES), F32)
        for c in range(nkc):
            keys = slice(c * MLA_KC, (c + 1) * MLA_KC)
            sc = lax.dot_general(q, k_ref[hh, keys, :], (((1,), (1,)), ((), ())), preferred_element_type=F32)
            m_new = jnp.maximum(m, jnp.max(sc, axis=-1, keepdims=True))
            alpha = jnp.exp2(m - m_new)
            p = jnp.exp2(sc - m_new)
            l = alpha * l + jnp.sum(p, axis=-1, keepdims=True)
            acc = alpha * acc + jnp.dot(p.astype(BF16), v_ref[keys, :], preferred_element_type=F32)
            m = m_new
        outs.append(acc / l)
    lo = lax.broadcasted_iota(jnp.int32, outs[0].shape, 1) < B_VDIM
    o_ref[...] = jnp.where(lo, outs[0], outs[1])


def _mla(qb, kb, vb, w1, w3, w2):
    b, h, s, _ = qb.shape
    npair = h // 2
    tq = MLA_TQ
    nq = s // tq
    nsteps = b * npair * nq
    ne, d, f = w1.shape
    assert ne % nsteps == 0
    epb = ne // nsteps
    wmap = lambda bi, p, qi: ((bi * npair + p) * nq + qi, 0, 0)
    return pl.pallas_call(
        _mla_kernel,
        grid=(b, npair, nq),
        in_specs=[pl.BlockSpec((None, 2, tq, HEAD_PAD), lambda bi, p, qi: (bi, p, qi, 0)),
                  pl.BlockSpec((None, 2, s, HEAD_PAD), lambda bi, p, qi: (bi, p, 0, 0)),
                  pl.BlockSpec((None, None, s, LANES), lambda bi, p, qi: (bi, p, 0, 0)),
                  pl.BlockSpec((epb, d, f), wmap),
                  pl.BlockSpec((epb, d, f), wmap),
                  pl.BlockSpec((epb, f, d), wmap)],
        out_specs=[pl.BlockSpec((None, tq, LANES), lambda bi, p, qi: (bi, qi, p)),
                   pl.BlockSpec((epb, d, f), wmap),
                   pl.BlockSpec((epb, d, f), wmap),
                   pl.BlockSpec((epb, f, d), wmap)],
        out_shape=[jax.ShapeDtypeStruct((b, s, B_WIDTH), F32),
                   jax.ShapeDtypeStruct((ne, d, f), BF16),
                   jax.ShapeDtypeStruct((ne, d, f), BF16),
                   jax.ShapeDtypeStruct((ne, f, d), BF16)],
        compiler_params=_cparams(("parallel", "parallel", "arbitrary")),
        name="mla",
    )(qb, kb, vb, w1, w3, w2)


def _outproj_kernel(x_ref, a_ref, b_ref, ag_ref, bg_ref, wo_ref, g1_ref, n2_ref, sc_ref, sh_ref, g2_ref,
                    wrt_ref, ws1_ref, ws3_ref, ws2_ref, x1s_ref, h2p_ref, scores_ref):
    an = _rms(a_ref[...], ag_ref[...])
    bn = _rms(b_ref[...], bg_ref[...])
    mix = jnp.concatenate([an, bn], axis=-1).astype(BF16)
    x1 = x_ref[...] + g1_ref[...] * jnp.dot(mix, wo_ref[...], preferred_element_type=F32)
    h2 = _rms(x1, n2_ref[...]) * (1.0 + sc_ref[...]) + sh_ref[...]
    packed = _pack_rows(h2)
    for cg in range(h2p_ref.shape[0]):
        h2p_ref[cg] = packed[:, cg * LANES:(cg + 1) * LANES]
    h_hi, h_lo = _split_bf16(h2)
    logits = lax.dot_general(wrt_ref[...], jnp.concatenate([h_hi, h_lo, h_hi], axis=1),
                             (((1,), (1,)), ((), ())), preferred_element_type=F32)
    scores_ref[...] = jax.nn.sigmoid(logits)
    h2b = h2.astype(BF16)
    hid = _silu(jnp.dot(h2b, ws1_ref[...], preferred_element_type=F32)) * jnp.dot(
        h2b, ws3_ref[...], preferred_element_type=F32)
    shared = jnp.dot(hid.astype(BF16), ws2_ref[...], preferred_element_type=F32)
    x1s_ref[...] = x1 + g2_ref[...] * shared


def _outproj(x2, a_out, b_out, a_out_g, b_out_g, w_o, mod3, norm2_g, w_router_t, ws1, ws3, ws2, s):
    n, d = x2.shape
    tm = TM_OUTPROJ
    tpb = s // tm
    row = lambda i: (i, 0)
    const = lambda i: (0, 0)
    modspec = lambda j: pl.BlockSpec((None, 1, d), lambda i: ((i // tpb) * 6 + j, 0, 0))
    return pl.pallas_call(
        _outproj_kernel,
        grid=(n // tm,),
        in_specs=[pl.BlockSpec((tm, d), row),
                  pl.BlockSpec((tm, A_WIDTH), row),
                  pl.BlockSpec((tm, B_WIDTH), row),
                  pl.BlockSpec((1, A_WIDTH), const),
                  pl.BlockSpec((1, B_WIDTH), const),
                  pl.BlockSpec(w_o.shape, const),
                  modspec(2),
                  pl.BlockSpec((1, d), const),
                  modspec(4), modspec(3), modspec(5),
                  pl.BlockSpec(w_router_t.shape, const),
                  pl.BlockSpec(ws1.shape, const),
                  pl.BlockSpec(ws3.shape, const),
                  pl.BlockSpec(ws2.shape, const)],
        out_specs=[pl.BlockSpec((tm, d), row),
                   pl.BlockSpec((d // 2 // LANES, tm, LANES), lambda i: (0, i, 0)),
                   pl.BlockSpec((N_EXPERTS, tm), lambda i: (0, i))],
        out_shape=[jax.ShapeDtypeStruct((n, d), F32),
                   jax.ShapeDtypeStruct((d // 2 // LANES, n, LANES), U32),
                   jax.ShapeDtypeStruct((N_EXPERTS, n), F32)],
        compiler_params=_cparams(("parallel",)),
        name="outproj",
    )(x2, a_out, b_out, a_out_g, b_out_g, w_o, mod3, norm2_g, mod3, mod3, mod3, w_router_t, ws1, ws3, ws2)


def _slab_order(v):
    return v.reshape((N_GROUPS, GROUP_SIZE) + v.shape[1:]).swapaxes(0, 1).reshape(v.shape)


def _expert_order(v):
    return v.reshape((GROUP_SIZE, N_GROUPS) + v.shape[1:]).swapaxes(0, 1).reshape(v.shape)


def _sublane_all(x, op):
    for sh in (4, 2, 1):
        x = op(x, pltpu.roll(x, sh, 0))
    return x


def _route_kernel(st_ref, bias_ref, eidx_ref, gate_ref, cnt_ref):
    nsl = GROUP_SIZE
    t = st_ref.shape[1]
    sub = lax.broadcasted_iota(I32, (SUBLANES, t), 0)
    ninf = -jnp.inf
    big = jnp.int32(1 << 30)
    sc = [st_ref[j * SUBLANES:(j + 1) * SUBLANES, :] for j in range(nsl)]
    sel = [sc[j] + bias_ref[j * SUBLANES:(j + 1) * SUBLANES, :] for j in range(nsl)]
    eid = [sub * GROUP_SIZE + j for j in range(nsl)]

    m1 = sel[0]
    m2 = jnp.full_like(m1, ninf)
    for j in range(1, nsl):
        m2 = jnp.maximum(m2, jnp.minimum(m1, sel[j]))
        m1 = jnp.maximum(m1, sel[j])
    gs = m1 + m2

    rank = jnp.zeros((SUBLANES, t), I32)
    for sh in range(1, N_GROUPS):
        other = pltpu.roll(gs, sh, 0)
        ahead = (other > gs) | ((other == gs) & (sub >= sh))
        rank = rank + ahead.astype(I32)
    gmask = rank < TOPK_GROUPS

    msel = [jnp.where(gmask, sel[j], ninf) for j in range(nsl)]
    hits = [jnp.zeros((SUBLANES, t), I32) for _ in range(nsl)]
    eidx = jnp.zeros((TOP_K, t), I32)
    gates = jnp.zeros((TOP_K, t), F32)
    for k in range(TOP_K):
        mx = msel[0]
        for j in range(1, nsl):
            mx = jnp.maximum(mx, msel[j])
        mx = _sublane_all(mx, jnp.maximum)
        cand = jnp.where(msel[0] == mx, eid[0], big)
        for j in range(1, nsl):
            cand = jnp.minimum(cand, jnp.where(msel[j] == mx, eid[j], big))
        idx = _sublane_all(cand, jnp.minimum)
        gk = jnp.zeros((SUBLANES, t), F32)
        for j in range(nsl):
            hit = eid[j] == idx
            gk = gk + jnp.where(hit, sc[j], 0.0)
            msel[j] = jnp.where(hit, ninf, msel[j])
            hits[j] = hits[j] + hit.astype(I32)
        gk = _sublane_all(gk, jnp.add)
        eidx = jnp.where(sub == k, idx, eidx)
        gates = jnp.where(sub == k, gk, gates)
    gsum = _sublane_all(gates, jnp.add)
    eidx_ref[...] = eidx
    gate_ref[...] = gates / gsum * ROUTED_SCALE

    @pl.when(pl.program_id(0) == 0)
    def _():
        cnt_ref[...] = jnp.zeros_like(cnt_ref)

    for j in range(nsl):
        cnt_ref[j * SUBLANES:(j + 1) * SUBLANES, :] += jnp.sum(hits[j].astype(F32), axis=1,
                                                               keepdims=True).astype(I32)


def _route(scores_t, e_bias_slab):
    e, n = scores_t.shape
    t = ROUTE_T
    return pl.pallas_call(
        _route_kernel,
        grid=(n // t,),
        in_specs=[pl.BlockSpec((e, t), lambda i: (0, i)),
                  pl.BlockSpec((e, 1), lambda i: (0, 0))],
        out_specs=[pl.BlockSpec((TOP_K, t), lambda i: (0, i)),
                   pl.BlockSpec((TOP_K, t), lambda i: (0, i)),
                   pl.BlockSpec((e, 1), lambda i: (0, 0))],
        out_shape=[jax.ShapeDtypeStruct((TOP_K, n), I32),
                   jax.ShapeDtypeStruct((TOP_K, n), F32),
                   jax.ShapeDtypeStruct((e, 1), I32)],
        compiler_params=_cparams(("arbitrary",)),
        name="route",
    )(scores_t, e_bias_slab)


def _dest_kernel(eidx_ref, pstart_ref, dest_ref, carry_ref):
    @pl.when(pl.program_id(0) == 0)
    def _():
        carry_ref[...] = jnp.zeros_like(carry_ref)

    nsl = GROUP_SIZE
    t = eidx_ref.shape[1]
    sub = lax.broadcasted_iota(I32, (SUBLANES, t), 0)
    eid = [sub * GROUP_SIZE + j for j in range(nsl)]
    ek = [eidx_ref[k:k + 1, :] for k in range(TOP_K)]
    slabs = []
    for j in range(nsl):
        oh = jnp.zeros((SUBLANES, t), F32)
        for k in range(TOP_K):
            oh = oh + (eid[j] == ek[k]).astype(F32)
        slabs.append(oh)
    onehot = jnp.concatenate(slabs, axis=0)
    row = lax.broadcasted_iota(I32, (t, t), 0)
    col = lax.broadcasted_iota(I32, (t, t), 1)
    upper = (row < col).astype(BF16)
    before = jnp.dot(onehot.astype(BF16), upper, preferred_element_type=F32)
    base = before + carry_ref[...] + pstart_ref[...]
    dest = jnp.zeros((TOP_K, t), I32)
    for k in range(TOP_K):
        acc = jnp.zeros((SUBLANES, t), F32)
        for j in range(nsl):
            acc = acc + jnp.where(eid[j] == ek[k], base[j * SUBLANES:(j + 1) * SUBLANES, :], 0.0)
        dk = _sublane_all(acc, jnp.add).astype(I32)
        dest = jnp.where(sub == k, dk, dest)
    dest_ref[...] = dest
    carry_ref[...] += jnp.sum(onehot, axis=1, keepdims=True)


def _dest(eidx_t, pstart_slab):
    n = eidx_t.shape[1]
    t = ROUTE_T
    return pl.pallas_call(
        _dest_kernel,
        grid=(n // t,),
        in_specs=[pl.BlockSpec((TOP_K, t), lambda i: (0, i)),
                  pl.BlockSpec((N_EXPERTS, 1), lambda i: (0, 0))],
        out_specs=pl.BlockSpec((TOP_K, t), lambda i: (0, i)),
        out_shape=jax.ShapeDtypeStruct((TOP_K, n), I32),
        scratch_shapes=[pltpu.VMEM((N_EXPERTS, 1), F32)],
        compiler_params=_cparams(("arbitrary",)),
        name="dest",
    )(eidx_t, pstart_slab)


def _sc_scatter(x, idx, rows_out, nplane, nslot):
    del nplane
    num = idx.shape[0]
    mesh = plsc.VectorSubcoreMesh(core_axis_name="core", subcore_axis_name="subcore")

    @pl.kernel(out_type=jax.ShapeDtypeStruct((rows_out, x.shape[1]), x.dtype), mesh=mesh, scratch_types=[])
    def scatter(x_hbm, i_hbm, o_hbm):
        def body(x_vmem, i_vmem):
            pltpu.sync_copy(x_vmem, o_hbm.at[i_vmem.at[0]])

        pltpu.emit_pipeline(
            body,
            grid=(num // SC_WINDOW,),
            in_specs=[pl.BlockSpec((SC_WINDOW, x.shape[1]), index_map=lambda i: (i // nslot, 0)),
                      pl.BlockSpec((1, SC_WINDOW), index_map=lambda i: (0, i))],
            out_specs=[],
            core_axis_name=("core", "subcore"),
            dimension_semantics=(pltpu.PARALLEL,),
        )(x_hbm, i_hbm)

    return scatter(x, idx.reshape(1, num))


def _expert_kernel(cstart_ref, nused_ref, valid_ref, w1_ref, w3_ref, w2_ref, xg_ref, y_ref,
                   xbuf, ybuf, xsem, ysem):
    e = pl.program_id(0)
    c0 = cstart_ref[e]
    c1 = cstart_ref[e + 1]
    nused = nused_ref[0]
    nchunks = xg_ref.shape[1] // MOE_BLK

    def rows(g):
        start = g * MOE_BLK
        return pl.ds(start if isinstance(g, int) else pl.multiple_of(start, MOE_BLK), MOE_BLK)

    nplane = xg_ref.shape[0]

    def xcopies(g):
        slot = g % EXP_XBUF
        return [pltpu.make_async_copy(xg_ref.at[pl.ds(0, nplane), rows(g)], xbuf.at[slot], xsem.at[slot])]

    def ycopies(g, slot):
        return [pltpu.make_async_copy(ybuf.at[slot], y_ref.at[pl.ds(0, nplane), rows(g)], ysem.at[slot])]

    @pl.when(e == 0)
    def _():
        ybuf[...] = jnp.zeros(ybuf.shape, ybuf.dtype)
        for j in range(EXP_AHEAD):
            @pl.when(j < nused)
            def _():
                for cp in xcopies(j):
                    cp.start()

    @pl.when(c1 > c0)
    def _():
        def ffn(g, yslot, nrows):
            row_id = lax.broadcasted_iota(I32, (nrows, LANES), 0)
            keep = row_id < valid_ref[g]
            halves = [_unpack_rows(jnp.where(keep, xbuf[g % EXP_XBUF, c, :nrows], jnp.uint32(0)))
                      for c in range(nplane)]
            xb = jnp.concatenate([h[0] for h in halves] + [h[1] for h in halves], axis=1).astype(BF16)
            hid = _silu(jnp.dot(xb, w1_ref[...], preferred_element_type=F32)) * jnp.dot(
                xb, w3_ref[...], preferred_element_type=F32)
            yp = _pack_rows(jnp.dot(hid.astype(BF16), w2_ref[...], preferred_element_type=F32))
            for c in range(nplane):
                ybuf[yslot, c, :nrows] = yp[:, c * LANES:(c + 1) * LANES]

        def chunk(g, carry):
            @pl.when(g + EXP_AHEAD < nused)
            def _():
                for cp in xcopies(g + EXP_AHEAD):
                    cp.start()

            for cp in xcopies(g):
                cp.wait()
            yslot = g % 2

            @pl.when(g >= 2)
            def _():
                for cp in ycopies(g - 2, yslot):
                    cp.wait()

            half = MOE_BLK // 2

            @pl.when(valid_ref[g] > half)
            def _():
                ffn(g, yslot, MOE_BLK)

            @pl.when(valid_ref[g] <= half)
            def _():
                ffn(g, yslot, half)

            for cp in ycopies(g, yslot):
                cp.start()
            return carry

        lax.fori_loop(c0, c1, chunk, 0)

    @pl.when(e == pl.num_programs(0) - 1)
    def _():
        for back in (2, 1):
            g = nused - back

            @pl.when(g >= 0)
            def _():
                for cp in ycopies(g, g % 2):
                    cp.wait()

        ybuf[0] = jnp.zeros(ybuf.shape[1:], ybuf.dtype)

        def zstart(g, c):
            for cp in ycopies(g, 0):
                cp.start()
            return c

        def zwait(g, c):
            for cp in ycopies(g, 0):
                cp.wait()
            return c

        lax.fori_loop(nused, nchunks, zstart, 0)
        lax.fori_loop(nused, nchunks, zwait, 0)


def _experts(cstart, nused, valid, xg, w1, w3, w2):
    nplane, rows, _ = xg.shape
    dh = nplane * LANES
    ne, d, f = w1.shape
    grid_spec = pltpu.PrefetchScalarGridSpec(
        num_scalar_prefetch=3,
        grid=(ne,),
        in_specs=[pl.BlockSpec((None, d, f), lambda e, cs, nu, va: (e, 0, 0)),
                  pl.BlockSpec((None, d, f), lambda e, cs, nu, va: (e, 0, 0)),
                  pl.BlockSpec((None, f, d), lambda e, cs, nu, va: (e, 0, 0)),
                  pl.BlockSpec(memory_space=pl.ANY)],
        out_specs=pl.BlockSpec(memory_space=pl.ANY),
        scratch_shapes=[pltpu.VMEM((EXP_XBUF, nplane, MOE_BLK, LANES), U32),
                        pltpu.VMEM((2, nplane, MOE_BLK, LANES), U32),
                        pltpu.SemaphoreType.DMA((EXP_XBUF,)), pltpu.SemaphoreType.DMA((2,))],
    )
    return pl.pallas_call(
        _expert_kernel,
        grid_spec=grid_spec,
        out_shape=jax.ShapeDtypeStruct((dh // LANES, rows, LANES), U32),
        compiler_params=_cparams(("arbitrary",)),
        name="experts",
    )(cstart, nused, valid, w1, w3, w2, xg)


def _sc_gather(x, idx):
    num = idx.shape[0]
    mesh = plsc.VectorSubcoreMesh(core_axis_name="core", subcore_axis_name="subcore")

    @pl.kernel(out_type=jax.ShapeDtypeStruct((num, x.shape[1]), x.dtype), mesh=mesh)
    def gather(x_hbm, i_hbm, o_hbm):
        def body(i_vmem, o_vmem):
            pltpu.sync_copy(x_hbm.at[i_vmem.at[0]], o_vmem)

        pltpu.emit_pipeline(
            body,
            grid=(num // SC_WINDOW,),
            in_specs=[pl.BlockSpec((1, SC_WINDOW), index_map=lambda i: (0, i))],
            out_specs=[pl.BlockSpec((SC_WINDOW, x.shape[1]), index_map=lambda i: (i, 0))],
            core_axis_name=("core", "subcore"),
            dimension_semantics=(pltpu.PARALLEL,),
        )(i_hbm, o_hbm)

    return gather(x, idx.reshape(1, num))


def _combine_kernel(yg_ref, gate_ref, x1s_ref, g2_ref, fg_ref, *rest):
    out_ref = rest[-1]
    gates = gate_ref[...]
    nch = yg_ref.shape[0]
    r_lo = [None] * nch
    r_hi = [None] * nch
    for k in range(TOP_K):
        gk = gates[:, k:k + 1]
        for c in range(nch):
            lo, hi = _unpack_rows(yg_ref[c, k])
            r_lo[c] = gk * lo if k == 0 else r_lo[c] + gk * lo
            r_hi[c] = gk * hi if k == 0 else r_hi[c] + gk * hi
    routed = jnp.concatenate(r_lo + r_hi, axis=1)
    x2 = x1s_ref[...] + g2_ref[...] * routed
    out_ref[...] = _rms(x2, fg_ref[...])


def _combine(yg, gates, x1s, mod3, final_g, s, part, prev_out):
    n, d = x1s.shape
    t = COMB_T
    tpb = s // t
    nch, _, npart, _ = yg.shape
    off = part * (npart // t)
    in_specs = [pl.BlockSpec((nch, TOP_K, t, LANES), lambda i: (0, 0, i, 0)),
                pl.BlockSpec((t, TOP_K), lambda i: (i + off, 0)),
                pl.BlockSpec((t, d), lambda i: (i + off, 0)),
                pl.BlockSpec((None, 1, d), lambda i: (((i + off) // tpb) * 6 + 5, 0, 0)),
                pl.BlockSpec((1, d), lambda i: (0, 0))]
    args = [yg, gates, x1s, mod3, final_g]
    aliases = {}
    if prev_out is not None:
        in_specs.append(pl.BlockSpec(memory_space=pl.ANY))
        args.append(prev_out)
        aliases = {len(args) - 1: 0}
    return pl.pallas_call(
        _combine_kernel,
        grid=(npart // t,),
        in_specs=in_specs,
        out_specs=pl.BlockSpec((t, d), lambda i: (i + off, 0)),
        out_shape=jax.ShapeDtypeStruct((n, d), F32),
        input_output_aliases=aliases,
        compiler_params=_cparams(("parallel",)),
        name="combine",
    )(*args)


def _place_cols(w, lanes):
    r, h, _ = w.shape
    src = {lane: j for j, lane in enumerate(lanes)}
    pieces, lane = [], 0
    while lane < HEAD_PAD:
        end = lane + 1
        if lane in src:
            while end < HEAD_PAD and end in src and src[end] == src[end - 1] + 1:
                end += 1
            pieces.append(w[:, :, src[lane]:src[lane] + end - lane])
        else:
            while end < HEAD_PAD and end not in src:
                end += 1
            pieces.append(jnp.zeros((r, h, end - lane), w.dtype))
        lane = end
    return jnp.concatenate(pieces, axis=2).reshape(r, h * HEAD_PAD)


def _rope_tables(s):
    half = B_ROPE // 2
    rope_lanes = _head_lanes()[B_NOPE:]
    inv = ROPE_THETA ** (-jnp.arange(0, B_ROPE, 2, dtype=jnp.float32) / B_ROPE)
    ang = jnp.arange(s, dtype=jnp.float32)[:, None] * inv[None, :]
    cos, sin = jnp.cos(ang), jnp.sin(ang)
    assert cos.shape[1] == half
    cos_t = 1.0 + _place_cols((jnp.concatenate([cos, cos], axis=1) - 1.0)[:, None, :], rope_lanes)
    sin_t = _place_cols(jnp.concatenate([-sin, sin], axis=1)[:, None, :], rope_lanes)
    return cos_t, sin_t


def kernel(x, c, w_ada, b_ada, norm1_g, w_in, q_norm_g, w_uq, kv_norm_g, w_ukv, rel_table, a_out_g, b_out_g,
           w_o, norm2_g, w_router, e_bias, w1, w3, w2, ws1, ws3, ws2, final_g):
    b, s, d = x.shape
    n = b * s
    assert w_ada.shape[0] == 1, "single layer"
    x2 = x.reshape(n, d)

    mod = _ada(c, w_ada[0], b_ada[0])
    mod3 = mod.reshape(b * 6, 1, d)
    bias = _bias_tiles(rel_table, _dil_geometry(s))

    wi = w_in[0]
    c_kpe = 3 * A_WIDTH + Q_LORA + KV_LORA
    lanes = _head_lanes()
    kpe_cols = _place_cols(wi[:, None, c_kpe:], lanes[B_NOPE:])
    w_in_ext = jnp.concatenate([wi[:, :c_kpe], kpe_cols], axis=1).astype(BF16)
    w_uq_p = _place_cols(w_uq[0].reshape(Q_LORA, B_HEADS, B_NOPE + B_ROPE), lanes).astype(BF16)
    w_uk_p = _place_cols(w_ukv[0].reshape(KV_LORA, B_HEADS, B_NOPE + B_VDIM)[:, :, :B_NOPE],
                         lanes[:B_NOPE]).astype(BF16)
    w_v = w_ukv[0].reshape(KV_LORA, B_HEADS, B_NOPE + B_VDIM)[:, :, B_NOPE:].reshape(KV_LORA, B_WIDTH).astype(BF16)
    cos_t, sin_t = _rope_tables(s)

    qa, ka, va, qb, kb, vb = _inproj(x2, mod3, norm1_g, w_in_ext, q_norm_g, w_uq_p, kv_norm_g, w_uk_p, w_v,
                                     cos_t, sin_t, b, s)
    a_out = _dilated(qa.reshape(b, s, A_WIDTH), ka.reshape(b, s, A_WIDTH), va.reshape(b, s, A_WIDTH), bias)
    b_out, w1_bf, w3_bf, w2_bf = _mla(qb, kb, vb, w1[0], w3[0], w2[0])

    wr = _slab_order(w_router[0].T)
    wr_hi, wr_lo = _split_bf16(wr)
    w_router_t = jnp.concatenate([wr_hi, wr_hi, wr_lo], axis=1)
    x1s, h2p, scores_t = _outproj(x2, a_out.reshape(n, A_WIDTH), b_out.reshape(n, B_WIDTH), a_out_g, b_out_g,
                                  w_o[0].astype(BF16), mod3, norm2_g, w_router_t,
                                  ws1[0].astype(BF16), ws3[0].astype(BF16), ws2[0].astype(BF16), s)

    eidx_t, gates_t, counts_slab = _route(scores_t, _slab_order(e_bias.reshape(N_EXPERTS, 1)))

    counts = _expert_order(counts_slab)[:, 0]
    padded = (counts + MOE_BLK - 1) // MOE_BLK * MOE_BLK
    pends = jnp.cumsum(padded)
    pstart = pends - padded
    nk = n * TOP_K
    nblk = -(-(nk + N_EXPERTS * (MOE_BLK - 1)) // MOE_BLK)
    rows_total = nblk * MOE_BLK
    blk_row = jnp.arange(nblk, dtype=I32) * MOE_BLK
    blk_e = jnp.minimum(jnp.sum((pends[None, :] <= blk_row[:, None]).astype(I32), axis=1), N_EXPERTS - 1)
    seg_end = (pstart + counts)[blk_e]
    valid = jnp.clip(seg_end - blk_row, 0, MOE_BLK).astype(I32)
    nused = (pends[-1] // MOE_BLK).astype(I32).reshape(1)
    cstart = jnp.concatenate([jnp.zeros((1,), I32), (pends // MOE_BLK).astype(I32)])

    dest_t = _dest(eidx_t, _slab_order(pstart.astype(F32).reshape(N_EXPERTS, 1)))
    nplane = h2p.shape[0]
    gidx = (dest_t.reshape(1, nk) + (jnp.arange(nplane, dtype=I32) * rows_total)[:, None]).reshape(nplane * nk)
    nwin = n // SC_WINDOW
    sidx = dest_t.reshape(TOP_K, nwin, SC_WINDOW).transpose(1, 0, 2)[None] + (
        jnp.arange(nplane, dtype=I32) * rows_total)[:, None, None, None]
    xg = _sc_scatter(h2p.reshape(nplane * n, LANES), sidx.reshape(nplane * nk), nplane * rows_total, nplane, TOP_K)
    y = _experts(cstart, nused, valid, xg.reshape(nplane, rows_total, LANES), w1_bf, w3_bf, w2_bf)
    y_flat = y.reshape(nplane * rows_total, LANES)
    gidx3 = gidx.reshape(nplane, TOP_K, n)
    gates = gates_t.T
    npart = n // COMB_SPLIT
    out = None
    for part in range(COMB_SPLIT):
        pidx = gidx3[:, :, part * npart:(part + 1) * npart].reshape(nplane * TOP_K * npart)
        yg = _sc_gather(y_flat, pidx).reshape(nplane, TOP_K, npart, LANES)
        out = _combine(yg, gates, x1s, mod3, final_g.reshape(1, d), s, part, out)
    return out.reshape(b, s, d)
```

```python
import functools
import math

import jax
import jax.numpy as jnp
from jax import lax
from jax.experimental import pallas as pl
from jax.experimental.pallas import tpu as pltpu
from jax.experimental.pallas import tpu_sc as plsc

F32 = jnp.float32
BF16 = jnp.bfloat16
U32 = jnp.uint32
I32 = jnp.int32
HIGHEST = lax.Precision.HIGHEST

D_MODEL = 1024
A_HEADS = 8
A_HEAD_DIM = 64
A_WIDTH = A_HEADS * A_HEAD_DIM
A_PATTERNS = ((128, 1), (512, 4), (2048, 16))
A_RADIUS = 64
REL_BUCKETS = 32
REL_MAX_DIST = 1024
B_HEADS = 8
B_NOPE = 64
B_ROPE = 32
B_VDIM = 64
B_WIDTH = B_HEADS * B_VDIM
Q_LORA = 384
KV_LORA = 256
ROPE_THETA = 10000.0
N_EXPERTS = 256
TOP_K = 8
N_GROUPS = 8
GROUP_SIZE = N_EXPERTS // N_GROUPS
TOPK_GROUPS = 4
EXPERT_FF = 256
SHARED_FF = 256
ROUTED_SCALE = 2.5
EPS = 1e-6
NEG_INF = -1e30
LOG2E = math.log2(math.e)

LANES = 128
SUBLANES = 8
HEAD_PAD = 128
IN_COLS_EXT = 3 * A_WIDTH + Q_LORA + KV_LORA + HEAD_PAD

TM_INPROJ = 512
TM_OUTPROJ = 1024
DIL_QB = 128
DIL_KW = DIL_QB + 2 * A_RADIUS
DIL_ITER_ROWS = 2048
MLA_TQ = 1024
MLA_KC = 2048
ROUTE_T = 512
MOE_BLK = 512
EXP_AHEAD = 3
EXP_XBUF = EXP_AHEAD + 1
SC_WINDOW = 128
COMB_T = 256
COMB_SPLIT = 8
VMEM_LIMIT = 56 * 1024 * 1024


def _cparams(sem):
    return pltpu.CompilerParams(dimension_semantics=sem, vmem_limit_bytes=VMEM_LIMIT)


def _rms(x, g):
    return x * lax.rsqrt(jnp.mean(x * x, axis=-1, keepdims=True) + EPS) * g


def _silu(x):
    return x * jax.nn.sigmoid(x)


def _split_bf16(x):
    hi = lax.bitcast_convert_type(lax.bitcast_convert_type(x, U32) & jnp.uint32(0xFFFF0000), F32)
    return hi.astype(BF16), (x - hi).astype(BF16)


def _pack_rows(x):
    half = x.shape[1] // 2
    bits = lax.bitcast_convert_type(x.astype(BF16).astype(F32), U32)
    return (bits[:, :half] >> 16) | bits[:, half:]


def _unpack_rows(w):
    lo = lax.bitcast_convert_type(w << 16, F32)
    hi = lax.bitcast_convert_type(w & jnp.uint32(0xFFFF0000), F32)
    return lo, hi


def _ada_kernel(c_ref, w_ref, b_ref, o_ref):
    o_ref[...] = jnp.dot(_silu(c_ref[...]), w_ref[...], precision=HIGHEST,
                         preferred_element_type=F32) + b_ref[...]


def _ada(c, w_ada, b_ada):
    b, d = c.shape
    n6 = w_ada.shape[1] // d
    return pl.pallas_call(
        _ada_kernel,
        grid=(n6,),
        in_specs=[pl.BlockSpec((b, d), lambda j: (0, 0)),
                  pl.BlockSpec((d, d), lambda j: (0, j)),
                  pl.BlockSpec((1, d), lambda j: (0, j))],
        out_specs=pl.BlockSpec((b, d), lambda j: (0, j)),
        out_shape=jax.ShapeDtypeStruct((b, n6 * d), F32),
        compiler_params=_cparams(("parallel",)),
        name="ada",
    )(c, w_ada, b_ada.reshape(1, -1))


def _t5_bucket(rel):
    half = REL_BUCKETS // 2
    max_exact = half // 2
    ret = jnp.where(rel > 0, half, 0)
    n = jnp.abs(rel)
    nf = jnp.maximum(n, 1).astype(jnp.float32)
    large = max_exact + (jnp.log(nf / max_exact) / math.log(REL_MAX_DIST / max_exact)
                         * (half - max_exact)).astype(jnp.int32)
    large = jnp.minimum(large, half - 1)
    return ret + jnp.where(n < max_exact, n, large)


def _dil_geometry(s):
    geo, base = [], 0
    for _, dil in A_PATTERNS:
        sub_len = s // dil
        qb = sub_len if sub_len <= DIL_KW else DIL_QB
        kw = min(qb + 2 * A_RADIUS, sub_len)
        assert sub_len % qb == 0 and (sub_len == qb or kw == qb + 2 * A_RADIUS)
        shifts = (A_RADIUS,) if sub_len == qb else (A_RADIUS, 0, -A_RADIUS)
        geo.append((dil, qb, kw, shifts, base))
        base += len(shifts)
    return geo


def _bucket_tiles(geo):
    qi = jnp.arange(max(g[1] for g in geo), dtype=jnp.int32)[:, None]
    ki = jnp.arange(max(g[2] for g in geo), dtype=jnp.int32)[None, :]
    tiles = []
    for dilation, qb, kw, shifts, _ in geo:
        for shift in shifts:
            off = ki + shift - A_RADIUS - qi
            bkt = _t5_bucket(off * dilation)
            inside = (jnp.abs(off) <= A_RADIUS) & (qi < qb) & (ki < kw)
            tiles.append(jnp.where(inside, bkt, -1))
    return jnp.stack(tiles, axis=0)


def _bias_kernel(tab_ref, bkt_ref, o_ref):
    bkt = bkt_ref[0]
    for h in range(A_HEADS):
        acc = jnp.full(bkt.shape, NEG_INF, F32)
        for b in range(REL_BUCKETS):
            acc = jnp.where(bkt == b, tab_ref[b, h] * LOG2E, acc)
        o_ref[0, h] = acc


def _bias_tiles(rel_table, geo):
    bkt = _bucket_tiles(geo)
    nt = bkt.shape[0]
    return pl.pallas_call(
        _bias_kernel,
        grid=(nt,),
        in_specs=[pl.BlockSpec(memory_space=pltpu.SMEM),
                  pl.BlockSpec((1,) + bkt.shape[1:], lambda t: (t, 0, 0))],
        out_specs=pl.BlockSpec((1, A_HEADS) + bkt.shape[1:], lambda t: (t, 0, 0, 0)),
        out_shape=jax.ShapeDtypeStruct((nt, A_HEADS) + bkt.shape[1:], F32),
        compiler_params=_cparams(("parallel",)),
        name="bias",
    )(rel_table, bkt)


def _head_lanes():
    half = B_ROPE // 2
    lanes = []
    for j in range(B_NOPE):
        lanes.append(half + j if j < HEAD_PAD // 2 - half else HEAD_PAD // 2 + half + (j - (HEAD_PAD // 2 - half)))
    for r in range(B_ROPE):
        lanes.append(r if r < half else HEAD_PAD // 2 + (r - half))
    return lanes


def _rope(x, cos, sin):
    return x * cos + pltpu.roll(x, HEAD_PAD // 2, 1) * sin


def _inproj_kernel(x_ref, sc_ref, sh_ref, g1_ref, win_ref, qg_ref, wuq_ref, kvg_ref, wuk_ref, wv_ref,
                   cos_ref, sin_ref, qa_ref, ka_ref, va_ref, qb_ref, kb_ref, vb_ref):
    x = x_ref[...]
    h = _rms(x, g1_ref[...]) * (1.0 + sc_ref[...]) + sh_ref[...]
    proj = jnp.dot(h.astype(BF16), win_ref[...], preferred_element_type=F32)
    aw = A_WIDTH
    qa_ref[...] = proj[:, 0:aw] * (LOG2E / math.sqrt(A_HEAD_DIM))
    ka_ref[...] = proj[:, aw:2 * aw]
    va_ref[...] = proj[:, 2 * aw:3 * aw]
    c0 = 3 * aw
    q_lat = proj[:, c0:c0 + Q_LORA]
    kv_lat = proj[:, c0 + Q_LORA:c0 + Q_LORA + KV_LORA]
    kpe = proj[:, c0 + Q_LORA + KV_LORA:]
    qn = _rms(q_lat, qg_ref[...]).astype(BF16)
    kvn = _rms(kv_lat, kvg_ref[...]).astype(BF16)
    qm = jnp.dot(qn, wuq_ref[...], preferred_element_type=F32)
    kn = jnp.dot(kvn, wuk_ref[...], preferred_element_type=F32)
    vv = jnp.dot(kvn, wv_ref[...], preferred_element_type=F32)
    cos = cos_ref[...]
    sin = sin_ref[...]
    qscale = LOG2E / math.sqrt(B_NOPE + B_ROPE)
    cos_q = cos * qscale
    sin_q = sin * qscale
    kpe_rot = _rope(kpe, cos, sin)
    for hd in range(B_HEADS):
        sl = slice(hd * HEAD_PAD, (hd + 1) * HEAD_PAD)
        qb_ref[hd] = _rope(qm[:, sl], cos_q, sin_q).astype(BF16)
        kb_ref[hd] = (kn[:, sl] + kpe_rot).astype(BF16)
    for p in range(B_HEADS // 2):
        vb_ref[p] = vv[:, p * LANES:(p + 1) * LANES].astype(BF16)


def _inproj(x2, mod3, norm1_g, w_in_ext, q_norm_g, w_uq_p, kv_norm_g, w_uk_p, w_v, cos_t, sin_t, b, s):
    n, d = x2.shape
    tm = TM_INPROJ
    tpb = s // tm
    row = lambda i: (i, 0)
    const = lambda i: (0, 0)
    hm = lambda i: (i // tpb, 0, i % tpb, 0)
    return pl.pallas_call(
        _inproj_kernel,
        grid=(n // tm,),
        in_specs=[pl.BlockSpec((tm, d), row),
                  pl.BlockSpec((None, 1, d), lambda i: ((i // tpb) * 6 + 1, 0, 0)),
                  pl.BlockSpec((None, 1, d), lambda i: ((i // tpb) * 6 + 0, 0, 0)),
                  pl.BlockSpec((1, d), const),
                  pl.BlockSpec(w_in_ext.shape, const),
                  pl.BlockSpec((1, Q_LORA), const),
                  pl.BlockSpec(w_uq_p.shape, const),
                  pl.BlockSpec((1, KV_LORA), const),
                  pl.BlockSpec(w_uk_p.shape, const),
                  pl.BlockSpec(w_v.shape, const),
                  pl.BlockSpec((tm, HEAD_PAD), lambda i: (i % tpb, 0)),
                  pl.BlockSpec((tm, HEAD_PAD), lambda i: (i % tpb, 0))],
        out_specs=[pl.BlockSpec((tm, A_WIDTH), row),
                   pl.BlockSpec((tm, A_WIDTH), row),
                   pl.BlockSpec((tm, A_WIDTH), row),
                   pl.BlockSpec((None, B_HEADS, tm, HEAD_PAD), hm),
                   pl.BlockSpec((None, B_HEADS, tm, HEAD_PAD), hm),
                   pl.BlockSpec((None, B_HEADS // 2, tm, LANES), hm)],
        out_shape=[jax.ShapeDtypeStruct((n, A_WIDTH), F32),
                   jax.ShapeDtypeStruct((n, A_WIDTH), F32),
                   jax.ShapeDtypeStruct((n, A_WIDTH), F32),
                   jax.ShapeDtypeStruct((b, B_HEADS, s, HEAD_PAD), BF16),
                   jax.ShapeDtypeStruct((b, B_HEADS, s, HEAD_PAD), BF16),
                   jax.ShapeDtypeStruct((b, B_HEADS // 2, s, LANES), BF16)],
        compiler_params=_cparams(("parallel",)),
        name="inproj",
    )(x2, mod3, mod3, norm1_g, w_in_ext, q_norm_g, w_uq_p, kv_norm_g, w_uk_p, w_v, cos_t, sin_t)


def _dil_block(q_ref, k_ref, v_ref, bias_ref, o_scr, m_scr, l_scr, pi, geo, nblk, job):
    dil, qb, kwin, shifts, tile0 = geo
    sub_len = nblk * qb
    r = job // nblk
    bi = job % nblk
    q0 = bi * qb
    ws = jnp.clip(q0 - A_RADIUS, 0, sub_len - kwin)
    var = jnp.where(bi == 0, 0, jnp.where(bi == nblk - 1, 2, 1)) if len(shifts) > 1 else 0
    if dil == 1:
        qsl = pl.ds(pl.multiple_of(q0, qb), qb)
        ksl = pl.ds(pl.multiple_of(ws, A_RADIUS), kwin)
    else:
        qsl = pl.ds(r + dil * q0, qb, stride=dil)
        ksl = pl.ds(r + dil * ws, kwin, stride=dil)
    q = q_ref[qsl, :]
    kw = k_ref[ksl, :].astype(BF16)
    vw = v_ref[ksl, :].astype(BF16)
    lo = lax.broadcasted_iota(jnp.int32, q.shape, 1) < A_HEAD_DIM
    outs, ms, ls = [], [], []
    for hh in range(2):
        qm = jnp.where(lo if hh == 0 else jnp.logical_not(lo), q, 0.0).astype(BF16)
        sc = lax.dot_general(qm, kw, (((1,), (1,)), ((), ())), preferred_element_type=F32)
        sc = sc + bias_ref[tile0 + var, hh, :qb, :kwin]
        m = jnp.max(sc, axis=-1, keepdims=True)
        p = jnp.exp2(sc - m)
        l = jnp.sum(p, axis=-1, keepdims=True)
        outs.append(jnp.dot(p.astype(BF16), vw, preferred_element_type=F32))
        ms.append(m)
        ls.append(l)
    o_scr[pi, qsl, :] = jnp.where(lo, outs[0], outs[1])
    m_scr[pi, qsl, :] = jnp.where(lo, ms[0], ms[1])
    l_scr[pi, qsl, :] = jnp.where(lo, ls[0], ls[1])


def _dilated_kernel(q_ref, k_ref, v_ref, bias_ref, out_ref, o_scr, m_scr, l_scr):
    s = q_ref.shape[0]
    for pi, geo in enumerate(_dil_geometry(s)):
        dil, qb = geo[0], geo[1]
        njobs = s // qb
        unroll = DIL_ITER_ROWS // qb
        assert njobs % unroll == 0
        blk = functools.partial(_dil_block, q_ref, k_ref, v_ref, bias_ref, o_scr, m_scr, l_scr, pi, geo,
                                s // dil // qb)

        def group(g, c, blk=blk, unroll=unroll):
            for u in range(unroll):
                blk(g * unroll + u)
            return c

        lax.fori_loop(0, njobs // unroll, group, 0)

    chunk = 512

    def comb(i, c):
        rows = pl.ds(pl.multiple_of(i * chunk, chunk), chunk)
        m0, m1, m2 = m_scr[0, rows, :], m_scr[1, rows, :], m_scr[2, rows, :]
        mx = jnp.maximum(jnp.maximum(m0, m1), m2)
        e0, e1, e2 = jnp.exp2(m0 - mx), jnp.exp2(m1 - mx), jnp.exp2(m2 - mx)
        num = e0 * o_scr[0, rows, :] + e1 * o_scr[1, rows, :] + e2 * o_scr[2, rows, :]
        den = e0 * l_scr[0, rows, :] + e1 * l_scr[1, rows, :] + e2 * l_scr[2, rows, :]
        out_ref[rows, :] = num / den
        return c

    lax.fori_loop(0, s // chunk, comb, 0)


def _dilated(qa, ka, va, bias):
    b, s, _ = qa.shape
    npair = A_WIDTH // LANES
    blk = pl.BlockSpec((None, s, LANES), lambda bi, p: (bi, 0, p))
    return pl.pallas_call(
        _dilated_kernel,
        grid=(b, npair),
        in_specs=[blk, blk, blk,
                  pl.BlockSpec((bias.shape[0], 2) + bias.shape[2:], lambda bi, p: (0, p, 0, 0))],
        out_specs=blk,
        out_shape=jax.ShapeDtypeStruct((b, s, A_WIDTH), F32),
        scratch_shapes=[pltpu.VMEM((len(A_PATTERNS), s, LANES), F32)] * 3,
        compiler_params=_cparams(("parallel", "parallel")),
        name="dilated",
    )(qa, ka, va, bias)


def _mla_kernel(q_ref, k_ref, v_ref, w1_ref, w3_ref, w2_ref, o_ref, w1b_ref, w3b_ref, w2b_ref):
    w1b_ref[...] = w1_ref[...].astype(BF16)
    w3b_ref[...] = w3_ref[...].astype(BF16)
    w2b_ref[...] = w2_ref[...].astype(BF16)
    tq = q_ref.shape[1]
    nkc = k_ref.shape[1] // MLA_KC
    outs = []
    for hh in range(2):
        q = q_ref[hh]
        m = jnp.full((tq, 1), -jnp.inf, F32)
        l = jnp.zeros((tq, 1), F32)
        acc = jnp.zeros((tq, LANES), F32)
        for c in range(nkc):
            keys = slice(c * MLA_KC, (c + 1) * MLA_KC)
            sc = lax.dot_general(q, k_ref[hh, keys, :], (((1,), (1,)), ((), ())), preferred_element_type=F32)
            m_new = jnp.maximum(m, jnp.max(sc, axis=-1, keepdims=True))
            alpha = jnp.exp2(m - m_new)
            p = jnp.exp2(sc - m_new)
            l = alpha * l + jnp.sum(p, axis=-1, keepdims=True)
            acc = alpha * acc + jnp.dot(p.astype(BF16), v_ref[keys, :], preferred_element_type=F32)
            m = m_new
        outs.append(acc / l)
    lo = lax.broadcasted_iota(jnp.int32, outs[0].shape, 1) < B_VDIM
    o_ref[...] = jnp.where(lo, outs[0], outs[1])


def _mla(qb, kb, vb, w1, w3, w2):
    b, h, s, _ = qb.shape
    npair = h // 2
    tq = MLA_TQ
    nq = s // tq
    nsteps = b * npair * nq
    ne, d, f = w1.shape
    assert ne % nsteps == 0
    epb = ne // nsteps
    wmap = lambda bi, p, qi: ((bi * npair + p) * nq + qi, 0, 0)
    return pl.pallas_call(
        _mla_kernel,
        grid=(b, npair, nq),
        in_specs=[pl.BlockSpec((None, 2, tq, HEAD_PAD), lambda bi, p, qi: (bi, p, qi, 0)),
                  pl.BlockSpec((None, 2, s, HEAD_PAD), lambda bi, p, qi: (bi, p, 0, 0)),
                  pl.BlockSpec((None, None, s, LANES), lambda bi, p, qi: (bi, p, 0, 0)),
                  pl.BlockSpec((epb, d, f), wmap),
                  pl.BlockSpec((epb, d, f), wmap),
                  pl.BlockSpec((epb, f, d), wmap)],
        out_specs=[pl.BlockSpec((None, tq, LANES), lambda bi, p, qi: (bi, qi, p)),
                   pl.BlockSpec((epb, d, f), wmap),
                   pl.BlockSpec((epb, d, f), wmap),
                   pl.BlockSpec((epb, f, d), wmap)],
        out_shape=[jax.ShapeDtypeStruct((b, s, B_WIDTH), F32),
                   jax.ShapeDtypeStruct((ne, d, f), BF16),
                   jax.ShapeDtypeStruct((ne, d, f), BF16),
                   jax.ShapeDtypeStruct((ne, f, d), BF16)],
        compiler_params=_cparams(("parallel", "parallel", "arbitrary")),
        name="mla",
    )(qb, kb, vb, w1, w3, w2)


def _outproj_kernel(x_ref, a_ref, b_ref, ag_ref, bg_ref, wo_ref, g1_ref, n2_ref, sc_ref, sh_ref, g2_ref,
                    wrt_ref, ws1_ref, ws3_ref, ws2_ref, x1s_ref, h2p_ref, scores_ref):
    an = _rms(a_ref[...], ag_ref[...])
    bn = _rms(b_ref[...], bg_ref[...])
    mix = jnp.concatenate([an, bn], axis=-1).astype(BF16)
    x1 = x_ref[...] + g1_ref[...] * jnp.dot(mix, wo_ref[...], preferred_element_type=F32)
    h2 = _rms(x1, n2_ref[...]) * (1.0 + sc_ref[...]) + sh_ref[...]
    packed = _pack_rows(h2)
    for cg in range(h2p_ref.shape[0]):
        h2p_ref[cg] = packed[:, cg * LANES:(cg + 1) * LANES]
    h_hi, h_lo = _split_bf16(h2)
    logits = lax.dot_general(wrt_ref[...], jnp.concatenate([h_hi, h_lo, h_hi], axis=1),
                             (((1,), (1,)), ((), ())), preferred_element_type=F32)
    scores_ref[...] = jax.nn.sigmoid(logits)
    h2b = h2.astype(BF16)
    hid = _silu(jnp.dot(h2b, ws1_ref[...], preferred_element_type=F32)) * jnp.dot(
        h2b, ws3_ref[...], preferred_element_type=F32)
    shared = jnp.dot(hid.astype(BF16), ws2_ref[...], preferred_element_type=F32)
    x1s_ref[...] = x1 + g2_ref[...] * shared


def _outproj(x2, a_out, b_out, a_out_g, b_out_g, w_o, mod3, norm2_g, w_router_t, ws1, ws3, ws2, s):
    n, d = x2.shape
    tm = TM_OUTPROJ
    tpb = s // tm
    row = lambda i: (i, 0)
    const = lambda i: (0, 0)
    modspec = lambda j: pl.BlockSpec((None, 1, d), lambda i: ((i // tpb) * 6 + j, 0, 0))
    return pl.pallas_call(
        _outproj_kernel,
        grid=(n // tm,),
        in_specs=[pl.BlockSpec((tm, d), row),
                  pl.BlockSpec((tm, A_WIDTH), row),
                  pl.BlockSpec((tm, B_WIDTH), row),
                  pl.BlockSpec((1, A_WIDTH), const),
                  pl.BlockSpec((1, B_WIDTH), const),
                  pl.BlockSpec(w_o.shape, const),
                  modspec(2),
                  pl.BlockSpec((1, d), const),
                  modspec(4), modspec(3), modspec(5),
                  pl.BlockSpec(w_router_t.shape, const),
                  pl.BlockSpec(ws1.shape, const),
                  pl.BlockSpec(ws3.shape, const),
                  pl.BlockSpec(ws2.shape, const)],
        out_specs=[pl.BlockSpec((tm, d), row),
                   pl.BlockSpec((d // 2 // LANES, tm, LANES), lambda i: (0, i, 0)),
                   pl.BlockSpec((N_EXPERTS, tm), lambda i: (0, i))],
        out_shape=[jax.ShapeDtypeStruct((n, d), F32),
                   jax.ShapeDtypeStruct((d // 2 // LANES, n, LANES), U32),
                   jax.ShapeDtypeStruct((N_EXPERTS, n), F32)],
        compiler_params=_cparams(("parallel",)),
        name="outproj",
    )(x2, a_out, b_out, a_out_g, b_out_g, w_o, mod3, norm2_g, mod3, mod3, mod3, w_router_t, ws1, ws3, ws2)


def _slab_order(v):
    return v.reshape((N_GROUPS, GROUP_SIZE) + v.shape[1:]).swapaxes(0, 1).reshape(v.shape)


def _expert_order(v):
    return v.reshape((GROUP_SIZE, N_GROUPS) + v.shape[1:]).swapaxes(0, 1).reshape(v.shape)


def _sublane_all(x, op):
    for sh in (4, 2, 1):
        x = op(x, pltpu.roll(x, sh, 0))
    return x


def _route_kernel(st_ref, bias_ref, eidx_ref, gate_ref, cnt_ref):
    nsl = GROUP_SIZE
    t = st_ref.shape[1]
    sub = lax.broadcasted_iota(I32, (SUBLANES, t), 0)
    ninf = -jnp.inf
    big = jnp.int32(1 << 30)
    sc = [st_ref[j * SUBLANES:(j + 1) * SUBLANES, :] for j in range(nsl)]
    sel = [sc[j] + bias_ref[j * SUBLANES:(j + 1) * SUBLANES, :] for j in range(nsl)]
    eid = [sub * GROUP_SIZE + j for j in range(nsl)]

    m1 = sel[0]
    m2 = jnp.full_like(m1, ninf)
    for j in range(1, nsl):
        m2 = jnp.maximum(m2, jnp.minimum(m1, sel[j]))
        m1 = jnp.maximum(m1, sel[j])
    gs = m1 + m2

    rank = jnp.zeros((SUBLANES, t), I32)
    for sh in range(1, N_GROUPS):
        other = pltpu.roll(gs, sh, 0)
        ahead = (other > gs) | ((other == gs) & (sub >= sh))
        rank = rank + ahead.astype(I32)
    gmask = rank < TOPK_GROUPS

    msel = [jnp.where(gmask, sel[j], ninf) for j in range(nsl)]
    hits = [jnp.zeros((SUBLANES, t), I32) for _ in range(nsl)]
    eidx = jnp.zeros((TOP_K, t), I32)
    gates = jnp.zeros((TOP_K, t), F32)
    for k in range(TOP_K):
        mx = msel[0]
        for j in range(1, nsl):
            mx = jnp.maximum(mx, msel[j])
        mx = _sublane_all(mx, jnp.maximum)
        cand = jnp.where(msel[0] == mx, eid[0], big)
        for j in range(1, nsl):
            cand = jnp.minimum(cand, jnp.where(msel[j] == mx, eid[j], big))
        idx = _sublane_all(cand, jnp.minimum)
        gk = jnp.zeros((SUBLANES, t), F32)
        for j in range(nsl):
            hit = eid[j] == idx
            gk = gk + jnp.where(hit, sc[j], 0.0)
            msel[j] = jnp.where(hit, ninf, msel[j])
            hits[j] = hits[j] + hit.astype(I32)
        gk = _sublane_all(gk, jnp.add)
        eidx = jnp.where(sub == k, idx, eidx)
        gates = jnp.where(sub == k, gk, gates)
    gsum = _sublane_all(gates, jnp.add)
    eidx_ref[...] = eidx
    gate_ref[...] = gates / gsum * ROUTED_SCALE

    @pl.when(pl.program_id(0) == 0)
    def _():
        cnt_ref[...] = jnp.zeros_like(cnt_ref)

    for j in range(nsl):
        cnt_ref[j * SUBLANES:(j + 1) * SUBLANES, :] += jnp.sum(hits[j].astype(F32), axis=1,
                                                               keepdims=True).astype(I32)


def _route(scores_t, e_bias_slab):
    e, n = scores_t.shape
    t = ROUTE_T
    return pl.pallas_call(
        _route_kernel,
        grid=(n // t,),
        in_specs=[pl.BlockSpec((e, t), lambda i: (0, i)),
                  pl.BlockSpec((e, 1), lambda i: (0, 0))],
        out_specs=[pl.BlockSpec((TOP_K, t), lambda i: (0, i)),
                   pl.BlockSpec((TOP_K, t), lambda i: (0, i)),
                   pl.BlockSpec((e, 1), lambda i: (0, 0))],
        out_shape=[jax.ShapeDtypeStruct((TOP_K, n), I32),
                   jax.ShapeDtypeStruct((TOP_K, n), F32),
                   jax.ShapeDtypeStruct((e, 1), I32)],
        compiler_params=_cparams(("arbitrary",)),
        name="route",
    )(scores_t, e_bias_slab)


def _dest_kernel(eidx_ref, pstart_ref, dest_ref, carry_ref):
    @pl.when(pl.program_id(0) == 0)
    def _():
        carry_ref[...] = jnp.zeros_like(carry_ref)

    nsl = GROUP_SIZE
    t = eidx_ref.shape[1]
    sub = lax.broadcasted_iota(I32, (SUBLANES, t), 0)
    eid = [sub * GROUP_SIZE + j for j in range(nsl)]
    ek = [eidx_ref[k:k + 1, :] for k in range(TOP_K)]
    slabs = []
    for j in range(nsl):
        oh = jnp.zeros((SUBLANES, t), F32)
        for k in range(TOP_K):
            oh = oh + (eid[j] == ek[k]).astype(F32)
        slabs.append(oh)
    onehot = jnp.concatenate(slabs, axis=0)
    row = lax.broadcasted_iota(I32, (t, t), 0)
    col = lax.broadcasted_iota(I32, (t, t), 1)
    upper = (row < col).astype(BF16)
    before = jnp.dot(onehot.astype(BF16), upper, preferred_element_type=F32)
    base = before + carry_ref[...] + pstart_ref[...]
    dest = jnp.zeros((TOP_K, t), I32)
    for k in range(TOP_K):
        acc = jnp.zeros((SUBLANES, t), F32)
        for j in range(nsl):
            acc = acc + jnp.where(eid[j] == ek[k], base[j * SUBLANES:(j + 1) * SUBLANES, :], 0.0)
        dk = _sublane_all(acc, jnp.add).astype(I32)
        dest = jnp.where(sub == k, dk, dest)
    dest_ref[...] = dest
    carry_ref[...] += jnp.sum(onehot, axis=1, keepdims=True)


def _dest(eidx_t, pstart_slab):
    n = eidx_t.shape[1]
    t = ROUTE_T
    return pl.pallas_call(
        _dest_kernel,
        grid=(n // t,),
        in_specs=[pl.BlockSpec((TOP_K, t), lambda i: (0, i)),
                  pl.BlockSpec((N_EXPERTS, 1), lambda i: (0, 0))],
        out_specs=pl.BlockSpec((TOP_K, t), lambda i: (0, i)),
        out_shape=jax.ShapeDtypeStruct((TOP_K, n), I32),
        scratch_shapes=[pltpu.VMEM((N_EXPERTS, 1), F32)],
        compiler_params=_cparams(("arbitrary",)),
        name="dest",
    )(eidx_t, pstart_slab)


def _sc_scatter(x, idx, rows_out, nplane, nslot):
    del nplane
    num = idx.shape[0]
    mesh = plsc.VectorSubcoreMesh(core_axis_name="core", subcore_axis_name="subcore")

    @pl.kernel(out_type=jax.ShapeDtypeStruct((rows_out, x.shape[1]), x.dtype), mesh=mesh, scratch_types=[])
    def scatter(x_hbm, i_hbm, o_hbm):
        def body(x_vmem, i_vmem):
            pltpu.sync_copy(x_vmem, o_hbm.at[i_vmem.at[0]])

        pltpu.emit_pipeline(
            body,
            grid=(num // SC_WINDOW,),
            in_specs=[pl.BlockSpec((SC_WINDOW, x.shape[1]), index_map=lambda i: (i // nslot, 0)),
                      pl.BlockSpec((1, SC_WINDOW), index_map=lambda i: (0, i))],
            out_specs=[],
            core_axis_name=("core", "subcore"),
            dimension_semantics=(pltpu.PARALLEL,),
        )(x_hbm, i_hbm)

    return scatter(x, idx.reshape(1, num))


def _expert_kernel(cstart_ref, nused_ref, valid_ref, w1_ref, w3_ref, w2_ref, xg_ref, y_ref,
                   xbuf, ybuf, xsem, ysem):
    e = pl.program_id(0)
    c0 = cstart_ref[e]
    c1 = cstart_ref[e + 1]
    nused = nused_ref[0]
    nchunks = xg_ref.shape[1] // MOE_BLK

    def rows(g):
        start = g * MOE_BLK
        return pl.ds(start if isinstance(g, int) else pl.multiple_of(start, MOE_BLK), MOE_BLK)

    nplane = xg_ref.shape[0]

    def xcopies(g):
        slot = g % EXP_XBUF
        return [pltpu.make_async_copy(xg_ref.at[pl.ds(0, nplane), rows(g)], xbuf.at[slot], xsem.at[slot])]

    def ycopies(g, slot):
        return [pltpu.make_async_copy(ybuf.at[slot], y_ref.at[pl.ds(0, nplane), rows(g)], ysem.at[slot])]

    @pl.when(e == 0)
    def _():
        ybuf[...] = jnp.zeros(ybuf.shape, ybuf.dtype)
        for j in range(EXP_AHEAD):
            @pl.when(j < nused)
            def _():
                for cp in xcopies(j):
                    cp.start()

    @pl.when(c1 > c0)
    def _():
        def ffn(g, yslot, nrows):
            row_id = lax.broadcasted_iota(I32, (nrows, LANES), 0)
            keep = row_id < valid_ref[g]
            halves = [_unpack_rows(jnp.where(keep, xbuf[g % EXP_XBUF, c, :nrows], jnp.uint32(0)))
                      for c in range(nplane)]
            xb = jnp.concatenate([h[0] for h in halves] + [h[1] for h in halves], axis=1).astype(BF16)
            hid = _silu(jnp.dot(xb, w1_ref[...], preferred_element_type=F32)) * jnp.dot(
                xb, w3_ref[...], preferred_element_type=F32)
            yp = _pack_rows(jnp.dot(hid.astype(BF16), w2_ref[...], preferred_element_type=F32))
            for c in range(nplane):
                ybuf[yslot, c, :nrows] = yp[:, c * LANES:(c + 1) * LANES]

        def chunk(g, carry):
            @pl.when(g + EXP_AHEAD < nused)
            def _():
                for cp in xcopies(g + EXP_AHEAD):
                    cp.start()

            for cp in xcopies(g):
                cp.wait()
            yslot = g % 2

            @pl.when(g >= 2)
            def _():
                for cp in ycopies(g - 2, yslot):
                    cp.wait()

            half = MOE_BLK // 2

            @pl.when(valid_ref[g] > half)
            def _():
                ffn(g, yslot, MOE_BLK)

            @pl.when(valid_ref[g] <= half)
            def _():
                ffn(g, yslot, half)

            for cp in ycopies(g, yslot):
                cp.start()
            return carry

        lax.fori_loop(c0, c1, chunk, 0)

    @pl.when(e == pl.num_programs(0) - 1)
    def _():
        for back in (2, 1):
            g = nused - back

            @pl.when(g >= 0)
            def _():
                for cp in ycopies(g, g % 2):
                    cp.wait()

        ybuf[0] = jnp.zeros(ybuf.shape[1:], ybuf.dtype)

        def zstart(g, c):
            for cp in ycopies(g, 0):
                cp.start()
            return c

        def zwait(g, c):
            for cp in ycopies(g, 0):
                cp.wait()
            return c

        lax.fori_loop(nused, nchunks, zstart, 0)
        lax.fori_loop(nused, nchunks, zwait, 0)


def _experts(cstart, nused, valid, xg, w1, w3, w2):
    nplane, rows, _ = xg.shape
    dh = nplane * LANES
    ne, d, f = w1.shape
    grid_spec = pltpu.PrefetchScalarGridSpec(
        num_scalar_prefetch=3,
        grid=(ne,),
        in_specs=[pl.BlockSpec((None, d, f), lambda e, cs, nu, va: (e, 0, 0)),
                  pl.BlockSpec((None, d, f), lambda e, cs, nu, va: (e, 0, 0)),
                  pl.BlockSpec((None, f, d), lambda e, cs, nu, va: (e, 0, 0)),
                  pl.BlockSpec(memory_space=pl.ANY)],
        out_specs=pl.BlockSpec(memory_space=pl.ANY),
        scratch_shapes=[pltpu.VMEM((EXP_XBUF, nplane, MOE_BLK, LANES), U32),
                        pltpu.VMEM((2, nplane, MOE_BLK, LANES), U32),
                        pltpu.SemaphoreType.DMA((EXP_XBUF,)), pltpu.SemaphoreType.DMA((2,))],
    )
    return pl.pallas_call(
        _expert_kernel,
        grid_spec=grid_spec,
        out_shape=jax.ShapeDtypeStruct((dh // LANES, rows, LANES), U32),
        compiler_params=_cparams(("arbitrary",)),
        name="experts",
    )(cstart, nused, valid, w1, w3, w2, xg)


def _sc_gather(x, idx):
    num = idx.shape[0]
    mesh = plsc.VectorSubcoreMesh(core_axis_name="core", subcore_axis_name="subcore")

    @pl.kernel(out_type=jax.ShapeDtypeStruct((num, x.shape[1]), x.dtype), mesh=mesh)
    def gather(x_hbm, i_hbm, o_hbm):
        def body(i_vmem, o_vmem):
            pltpu.sync_copy(x_hbm.at[i_vmem.at[0]], o_vmem)

        pltpu.emit_pipeline(
            body,
            grid=(num // SC_WINDOW,),
            in_specs=[pl.BlockSpec((1, SC_WINDOW), index_map=lambda i: (0, i))],
            out_specs=[pl.BlockSpec((SC_WINDOW, x.shape[1]), index_map=lambda i: (i, 0))],
            core_axis_name=("core", "subcore"),
            dimension_semantics=(pltpu.PARALLEL,),
        )(i_hbm, o_hbm)

    return gather(x, idx.reshape(1, num))


def _combine_kernel(yg_ref, gate_ref, x1s_ref, g2_ref, fg_ref, *rest):
    out_ref = rest[-1]
    gates = gate_ref[...]
    nch = yg_ref.shape[0]
    r_lo = [None] * nch
    r_hi = [None] * nch
    for k in range(TOP_K):
        gk = gates[:, k:k + 1]
        for c in range(nch):
            lo, hi = _unpack_rows(yg_ref[c, k])
            r_lo[c] = gk * lo if k == 0 else r_lo[c] + gk * lo
            r_hi[c] = gk * hi if k == 0 else r_hi[c] + gk * hi
    routed = jnp.concatenate(r_lo + r_hi, axis=1)
    x2 = x1s_ref[...] + g2_ref[...] * routed
    out_ref[...] = _rms(x2, fg_ref[...])


def _combine(yg, gates, x1s, mod3, final_g, s, part, prev_out):
    n, d = x1s.shape
    t = COMB_T
    tpb = s // t
    nch, _, npart, _ = yg.shape
    off = part * (npart // t)
    in_specs = [pl.BlockSpec((nch, TOP_K, t, LANES), lambda i: (0, 0, i, 0)),
                pl.BlockSpec((t, TOP_K), lambda i: (i + off, 0)),
                pl.BlockSpec((t, d), lambda i: (i + off, 0)),
                pl.BlockSpec((None, 1, d), lambda i: (((i + off) // tpb) * 6 + 5, 0, 0)),
                pl.BlockSpec((1, d), lambda i: (0, 0))]
    args = [yg, gates, x1s, mod3, final_g]
    aliases = {}
    if prev_out is not None:
        in_specs.append(pl.BlockSpec(memory_space=pl.ANY))
        args.append(prev_out)
        aliases = {len(args) - 1: 0}
    return pl.pallas_call(
        _combine_kernel,
        grid=(npart // t,),
        in_specs=in_specs,
        out_specs=pl.BlockSpec((t, d), lambda i: (i + off, 0)),
        out_shape=jax.ShapeDtypeStruct((n, d), F32),
        input_output_aliases=aliases,
        compiler_params=_cparams(("parallel",)),
        name="combine",
    )(*args)


def _place_cols(w, lanes):
    r, h, _ = w.shape
    src = {lane: j for j, lane in enumerate(lanes)}
    pieces, lane = [], 0
    while lane < HEAD_PAD:
        end = lane + 1
        if lane in src:
            while end < HEAD_PAD and end in src and src[end] == src[end - 1] + 1:
                end += 1
            pieces.append(w[:, :, src[lane]:src[lane] + end - lane])
        else:
            while end < HEAD_PAD and end not in src:
                end += 1
            pieces.append(jnp.zeros((r, h, end - lane), w.dtype))
        lane = end
    return jnp.concatenate(pieces, axis=2).reshape(r, h * HEAD_PAD)


def _rope_tables(s):
    half = B_ROPE // 2
    rope_lanes = _head_lanes()[B_NOPE:]
    inv = ROPE_THETA ** (-jnp.arange(0, B_ROPE, 2, dtype=jnp.float32) / B_ROPE)
    ang = jnp.arange(s, dtype=jnp.float32)[:, None] * inv[None, :]
    cos, sin = jnp.cos(ang), jnp.sin(ang)
    assert cos.shape[1] == half
    cos_t = 1.0 + _place_cols((jnp.concatenate([cos, cos], axis=1) - 1.0)[:, None, :], rope_lanes)
    sin_t = _place_cols(jnp.concatenate([-sin, sin], axis=1)[:, None, :], rope_lanes)
    return cos_t, sin_t


def kernel(x, c, w_ada, b_ada, norm1_g, w_in, q_norm_g, w_uq, kv_norm_g, w_ukv, rel_table, a_out_g, b_out_g,
           w_o, norm2_g, w_router, e_bias, w1, w3, w2, ws1, ws3, ws2, final_g):
    b, s, d = x.shape
    n = b * s
    assert w_ada.shape[0] == 1, "single layer"
    x2 = x.reshape(n, d)

    mod = _ada(c, w_ada[0], b_ada[0])
    mod3 = mod.reshape(b * 6, 1, d)
    bias = _bias_tiles(rel_table, _dil_geometry(s))

    wi = w_in[0]
    c_kpe = 3 * A_WIDTH + Q_LORA + KV_LORA
    lanes = _head_lanes()
    kpe_cols = _place_cols(wi[:, None, c_kpe:], lanes[B_NOPE:])
    w_in_ext = jnp.concatenate([wi[:, :c_kpe], kpe_cols], axis=1).astype(BF16)
    w_uq_p = _place_cols(w_uq[0].reshape(Q_LORA, B_HEADS, B_NOPE + B_ROPE), lanes).astype(BF16)
    w_uk_p = _place_cols(w_ukv[0].reshape(KV_LORA, B_HEADS, B_NOPE + B_VDIM)[:, :, :B_NOPE],
                         lanes[:B_NOPE]).astype(BF16)
    w_v = w_ukv[0].reshape(KV_LORA, B_HEADS, B_NOPE + B_VDIM)[:, :, B_NOPE:].reshape(KV_LORA, B_WIDTH).astype(BF16)
    cos_t, sin_t = _rope_tables(s)

    qa, ka, va, qb, kb, vb = _inproj(x2, mod3, norm1_g, w_in_ext, q_norm_g, w_uq_p, kv_norm_g, w_uk_p, w_v,
                                     cos_t, sin_t, b, s)
    a_out = _dilated(qa.reshape(b, s, A_WIDTH), ka.reshape(b, s, A_WIDTH), va.reshape(b, s, A_WIDTH), bias)
    b_out, w1_bf, w3_bf, w2_bf = _mla(qb, kb, vb, w1[0], w3[0], w2[0])

    wr = _slab_order(w_router[0].T)
    wr_hi, wr_lo = _split_bf16(wr)
    w_router_t = jnp.concatenate([wr_hi, wr_hi, wr_lo], axis=1)
    x1s, h2p, scores_t = _outproj(x2, a_out.reshape(n, A_WIDTH), b_out.reshape(n, B_WIDTH), a_out_g, b_out_g,
                                  w_o[0].astype(BF16), mod3, norm2_g, w_router_t,
                                  ws1[0].astype(BF16), ws3[0].astype(BF16), ws2[0].astype(BF16), s)

    eidx_t, gates_t, counts_slab = _route(scores_t, _slab_order(e_bias.reshape(N_EXPERTS, 1)))

    counts = _expert_order(counts_slab)[:, 0]
    padded = (counts + MOE_BLK - 1) // MOE_BLK * MOE_BLK
    pends = jnp.cumsum(padded)
    pstart = pends - padded
    nk = n * TOP_K
    nblk = -(-(nk + N_EXPERTS * (MOE_BLK - 1)) // MOE_BLK)
    rows_total = nblk * MOE_BLK
    blk_row = jnp.arange(nblk, dtype=I32) * MOE_BLK
    blk_e = jnp.minimum(jnp.sum((pends[None, :] <= blk_row[:, None]).astype(I32), axis=1), N_EXPERTS - 1)
    seg_end = (pstart + counts)[blk_e]
    valid = jnp.clip(seg_end - blk_row, 0, MOE_BLK).astype(I32)
    nused = (pends[-1] // MOE_BLK).astype(I32).reshape(1)
    cstart = jnp.concatenate([jnp.zeros((1,), I32), (pends // MOE_BLK).astype(I32)])

    dest_t = _dest(eidx_t, _slab_order(pstart.astype(F32).reshape(N_EXPERTS, 1)))
    nplane = h2p.shape[0]
    gidx = (dest_t.reshape(1, nk) + (jnp.arange(nplane, dtype=I32) * rows_total)[:, None]).reshape(nplane * nk)
    nwin = n // SC_WINDOW
    sidx = dest_t.reshape(TOP_K, nwin, SC_WINDOW).transpose(1, 0, 2)[None] + (
        jnp.arange(nplane, dtype=I32) * rows_total)[:, None, None, None]
    xg = _sc_scatter(h2p.reshape(nplane * n, LANES), sidx.reshape(nplane * nk), nplane * rows_total, nplane, TOP_K)
    y = _experts(cstart, nused, valid, xg.reshape(nplane, rows_total, LANES), w1_bf, w3_bf, w2_bf)
    y_flat = y.reshape(nplane * rows_total, LANES)
    gidx3 = gidx.reshape(nplane, TOP_K, n)
    gates = gates_t.T
    npart = n // COMB_SPLIT
    out = None
    for part in range(COMB_SPLIT):
        pidx = gidx3[:, :, part * npart:(part + 1) * npart].reshape(nplane * TOP_K * npart)
        yg = _sc_gather(y_flat, pidx).reshape(nplane, TOP_K, npart, LANES)
        out = _combine(yg, gates, x1s, mod3, final_g.reshape(1, d), s, part, out)
    return out.reshape(b, s, d)
```

```python
import functools
import math

import jax
import jax.numpy as jnp
from jax import lax
from jax.experimental import pallas as pl
from jax.experimental.pallas import tpu as pltpu
from jax.experimental.pallas import tpu_sc as plsc

F32 = jnp.float32
BF16 = jnp.bfloat16
U32 = jnp.uint32
I32 = jnp.int32
HIGHEST = lax.Precision.HIGHEST

D_MODEL = 1024
A_HEADS = 8
A_HEAD_DIM = 64
A_WIDTH = A_HEADS * A_HEAD_DIM
A_PATTERNS = ((128, 1), (512, 4), (2048, 16))
A_RADIUS = 64
REL_BUCKETS = 32
REL_MAX_DIST = 1024
B_HEADS = 8
B_NOPE = 64
B_ROPE = 32
B_VDIM = 64
B_WIDTH = B_HEADS * B_VDIM
Q_LORA = 384
KV_LORA = 256
ROPE_THETA = 10000.0
N_EXPERTS = 256
TOP_K = 8
N_GROUPS = 8
GROUP_SIZE = N_EXPERTS // N_GROUPS
TOPK_GROUPS = 4
EXPERT_FF = 256
SHARED_FF = 256
ROUTED_SCALE = 2.5
EPS = 1e-6
NEG_INF = -1e30
LOG2E = math.log2(math.e)

LANES = 128
SUBLANES = 8
HEAD_PAD = 128
IN_COLS_EXT = 3 * A_WIDTH + Q_LORA + KV_LORA + HEAD_PAD

TM_INPROJ = 512
TM_OUTPROJ = 1024
DIL_QB = 128
DIL_KW = DIL_QB + 2 * A_RADIUS
DIL_ITER_ROWS = 4096
MLA_TQ = 1024
MLA_KC = 2048
ROUTE_T = 512
MOE_BLK = 512
EXP_AHEAD = 3
EXP_XBUF = EXP_AHEAD + 1
SC_WINDOW = 128
COMB_T = 256
COMB_SPLIT = 8
VMEM_LIMIT = 56 * 1024 * 1024


def _cparams(sem):
    return pltpu.CompilerParams(dimension_semantics=sem, vmem_limit_bytes=VMEM_LIMIT)


def _rms(x, g):
    return x * lax.rsqrt(jnp.mean(x * x, axis=-1, keepdims=True) + EPS) * g


def _silu(x):
    return x * jax.nn.sigmoid(x)


def _split_bf16(x):
    hi = lax.bitcast_convert_type(lax.bitcast_convert_type(x, U32) & jnp.uint32(0xFFFF0000), F32)
    return hi.astype(BF16), (x - hi).astype(BF16)


def _pack_rows(x):
    half = x.shape[1] // 2
    bits = lax.bitcast_convert_type(x.astype(BF16).astype(F32), U32)
    return (bits[:, :half] >> 16) | bits[:, half:]


def _unpack_rows(w):
    lo = lax.bitcast_convert_type(w << 16, F32)
    hi = lax.bitcast_convert_type(w & jnp.uint32(0xFFFF0000), F32)
    return lo, hi


def _ada_kernel(c_ref, w_ref, b_ref, o_ref):
    o_ref[...] = jnp.dot(_silu(c_ref[...]), w_ref[...], precision=HIGHEST,
                         preferred_element_type=F32) + b_ref[...]


def _ada(c, w_ada, b_ada):
    b, d = c.shape
    n6 = w_ada.shape[1] // d
    return pl.pallas_call(
        _ada_kernel,
        grid=(n6,),
        in_specs=[pl.BlockSpec((b, d), lambda j: (0, 0)),
                  pl.BlockSpec((d, d), lambda j: (0, j)),
                  pl.BlockSpec((1, d), lambda j: (0, j))],
        out_specs=pl.BlockSpec((b, d), lambda j: (0, j)),
        out_shape=jax.ShapeDtypeStruct((b, n6 * d), F32),
        compiler_params=_cparams(("parallel",)),
        name="ada",
    )(c, w_ada, b_ada.reshape(1, -1))


def _t5_bucket(rel):
    half = REL_BUCKETS // 2
    max_exact = half // 2
    ret = jnp.where(rel > 0, half, 0)
    n = jnp.abs(rel)
    nf = jnp.maximum(n, 1).astype(jnp.float32)
    large = max_exact + (jnp.log(nf / max_exact) / math.log(REL_MAX_DIST / max_exact)
                         * (half - max_exact)).astype(jnp.int32)
    large = jnp.minimum(large, half - 1)
    return ret + jnp.where(n < max_exact, n, large)


def _dil_geometry(s):
    geo, base = [], 0
    for _, dil in A_PATTERNS:
        sub_len = s // dil
        qb = sub_len if sub_len <= DIL_KW else DIL_QB
        kw = min(qb + 2 * A_RADIUS, sub_len)
        assert sub_len % qb == 0 and (sub_len == qb or kw == qb + 2 * A_RADIUS)
        shifts = (A_RADIUS,) if sub_len == qb else (A_RADIUS, 0, -A_RADIUS)
        geo.append((dil, qb, kw, shifts, base))
        base += len(shifts)
    return geo


def _bucket_tiles(geo):
    qi = jnp.arange(max(g[1] for g in geo), dtype=jnp.int32)[:, None]
    ki = jnp.arange(max(g[2] for g in geo), dtype=jnp.int32)[None, :]
    tiles = []
    for dilation, qb, kw, shifts, _ in geo:
        for shift in shifts:
            off = ki + shift - A_RADIUS - qi
            bkt = _t5_bucket(off * dilation)
            inside = (jnp.abs(off) <= A_RADIUS) & (qi < qb) & (ki < kw)
            tiles.append(jnp.where(inside, bkt, -1))
    return jnp.stack(tiles, axis=0)


def _bias_kernel(tab_ref, bkt_ref, o_ref):
    bkt = bkt_ref[0]
    for h in range(A_HEADS):
        acc = jnp.full(bkt.shape, NEG_INF, F32)
        for b in range(REL_BUCKETS):
            acc = jnp.where(bkt == b, tab_ref[b, h] * LOG2E, acc)
        o_ref[0, h] = acc


def _bias_tiles(rel_table, geo):
    bkt = _bucket_tiles(geo)
    nt = bkt.shape[0]
    return pl.pallas_call(
        _bias_kernel,
        grid=(nt,),
        in_specs=[pl.BlockSpec(memory_space=pltpu.SMEM),
                  pl.BlockSpec((1,) + bkt.shape[1:], lambda t: (t, 0, 0))],
        out_specs=pl.BlockSpec((1, A_HEADS) + bkt.shape[1:], lambda t: (t, 0, 0, 0)),
        out_shape=jax.ShapeDtypeStruct((nt, A_HEADS) + bkt.shape[1:], F32),
        compiler_params=_cparams(("parallel",)),
        name="bias",
    )(rel_table, bkt)


def _head_lanes():
    half = B_ROPE // 2
    lanes = []
    for j in range(B_NOPE):
        lanes.append(half + j if j < HEAD_PAD // 2 - half else HEAD_PAD // 2 + half + (j - (HEAD_PAD // 2 - half)))
    for r in range(B_ROPE):
        lanes.append(r if r < half else HEAD_PAD // 2 + (r - half))
    return lanes


def _rope(x, cos, sin):
    return x * cos + pltpu.roll(x, HEAD_PAD // 2, 1) * sin


def _inproj_kernel(x_ref, sc_ref, sh_ref, g1_ref, win_ref, qg_ref, wuq_ref, kvg_ref, wuk_ref, wv_ref,
                   cos_ref, sin_ref, qa_ref, ka_ref, va_ref, qb_ref, kb_ref, vb_ref):
    x = x_ref[...]
    h = _rms(x, g1_ref[...]) * (1.0 + sc_ref[...]) + sh_ref[...]
    proj = jnp.dot(h.astype(BF16), win_ref[...], preferred_element_type=F32)
    aw = A_WIDTH
    qa_ref[...] = proj[:, 0:aw] * (LOG2E / math.sqrt(A_HEAD_DIM))
    ka_ref[...] = proj[:, aw:2 * aw]
    va_ref[...] = proj[:, 2 * aw:3 * aw]
    c0 = 3 * aw
    q_lat = proj[:, c0:c0 + Q_LORA]
    kv_lat = proj[:, c0 + Q_LORA:c0 + Q_LORA + KV_LORA]
    kpe = proj[:, c0 + Q_LORA + KV_LORA:]
    qn = _rms(q_lat, qg_ref[...]).astype(BF16)
    kvn = _rms(kv_lat, kvg_ref[...]).astype(BF16)
    qm = jnp.dot(qn, wuq_ref[...], preferred_element_type=F32)
    kn = jnp.dot(kvn, wuk_ref[...], preferred_element_type=F32)
    vv = jnp.dot(kvn, wv_ref[...], preferred_element_type=F32)
    cos = cos_ref[...]
    sin = sin_ref[...]
    qscale = LOG2E / math.sqrt(B_NOPE + B_ROPE)
    cos_q = cos * qscale
    sin_q = sin * qscale
    kpe_rot = _rope(kpe, cos, sin)
    for hd in range(B_HEADS):
        sl = slice(hd * HEAD_PAD, (hd + 1) * HEAD_PAD)
        qb_ref[hd] = _rope(qm[:, sl], cos_q, sin_q).astype(BF16)
        kb_ref[hd] = (kn[:, sl] + kpe_rot).astype(BF16)
    for p in range(B_HEADS // 2):
        vb_ref[p] = vv[:, p * LANES:(p + 1) * LANES].astype(BF16)


def _inproj(x2, mod3, norm1_g, w_in_ext, q_norm_g, w_uq_p, kv_norm_g, w_uk_p, w_v, cos_t, sin_t, b, s):
    n, d = x2.shape
    tm = TM_INPROJ
    tpb = s // tm
    row = lambda i: (i, 0)
    const = lambda i: (0, 0)
    hm = lambda i: (i // tpb, 0, i % tpb, 0)
    return pl.pallas_call(
        _inproj_kernel,
        grid=(n // tm,),
        in_specs=[pl.BlockSpec((tm, d), row),
                  pl.BlockSpec((None, 1, d), lambda i: ((i // tpb) * 6 + 1, 0, 0)),
                  pl.BlockSpec((None, 1, d), lambda i: ((i // tpb) * 6 + 0, 0, 0)),
                  pl.BlockSpec((1, d), const),
                  pl.BlockSpec(w_in_ext.shape, const),
                  pl.BlockSpec((1, Q_LORA), const),
                  pl.BlockSpec(w_uq_p.shape, const),
                  pl.BlockSpec((1, KV_LORA), const),
                  pl.BlockSpec(w_uk_p.shape, const),
                  pl.BlockSpec(w_v.shape, const),
                  pl.BlockSpec((tm, HEAD_PAD), lambda i: (i % tpb, 0)),
                  pl.BlockSpec((tm, HEAD_PAD), lambda i: (i % tpb, 0))],
        out_specs=[pl.BlockSpec((tm, A_WIDTH), row),
                   pl.BlockSpec((tm, A_WIDTH), row),
                   pl.BlockSpec((tm, A_WIDTH), row),
                   pl.BlockSpec((None, B_HEADS, tm, HEAD_PAD), hm),
                   pl.BlockSpec((None, B_HEADS, tm, HEAD_PAD), hm),
                   pl.BlockSpec((None, B_HEADS // 2, tm, LANES), hm)],
        out_shape=[jax.ShapeDtypeStruct((n, A_WIDTH), F32),
                   jax.ShapeDtypeStruct((n, A_WIDTH), F32),
                   jax.ShapeDtypeStruct((n, A_WIDTH), F32),
                   jax.ShapeDtypeStruct((b, B_HEADS, s, HEAD_PAD), BF16),
                   jax.ShapeDtypeStruct((b, B_HEADS, s, HEAD_PAD), BF16),
                   jax.ShapeDtypeStruct((b, B_HEADS // 2, s, LANES), BF16)],
        compiler_params=_cparams(("parallel",)),
        name="inproj",
    )(x2, mod3, mod3, norm1_g, w_in_ext, q_norm_g, w_uq_p, kv_norm_g, w_uk_p, w_v, cos_t, sin_t)


def _dil_block(q_ref, k_ref, v_ref, bias_ref, o_scr, m_scr, l_scr, pi, geo, nblk, job):
    dil, qb, kwin, shifts, tile0 = geo
    sub_len = nblk * qb
    r = job // nblk
    bi = job % nblk
    q0 = bi * qb
    ws = jnp.clip(q0 - A_RADIUS, 0, sub_len - kwin)
    var = jnp.where(bi == 0, 0, jnp.where(bi == nblk - 1, 2, 1)) if len(shifts) > 1 else 0
    if dil == 1:
        qsl = pl.ds(pl.multiple_of(q0, qb), qb)
        ksl = pl.ds(pl.multiple_of(ws, A_RADIUS), kwin)
    else:
        qsl = pl.ds(r + dil * q0, qb, stride=dil)
        ksl = pl.ds(r + dil * ws, kwin, stride=dil)
    q = q_ref[qsl, :]
    kw = k_ref[ksl, :].astype(BF16)
    vw = v_ref[ksl, :].astype(BF16)
    lo = lax.broadcasted_iota(jnp.int32, q.shape, 1) < A_HEAD_DIM
    outs, ms, ls = [], [], []
    for hh in range(2):
        qm = jnp.where(lo if hh == 0 else jnp.logical_not(lo), q, 0.0).astype(BF16)
        sc = lax.dot_general(qm, kw, (((1,), (1,)), ((), ())), preferred_element_type=F32)
        sc = sc + bias_ref[tile0 + var, hh, :qb, :kwin]
        m = jnp.max(sc, axis=-1, keepdims=True)
        p = jnp.exp2(sc - m)
        l = jnp.sum(p, axis=-1, keepdims=True)
        outs.append(jnp.dot(p.astype(BF16), vw, preferred_element_type=F32))
        ms.append(m)
        ls.append(l)
    o_scr[pi, qsl, :] = jnp.where(lo, outs[0], outs[1])
    m_scr[pi, qsl, :] = jnp.where(lo, ms[0], ms[1])
    l_scr[pi, qsl, :] = jnp.where(lo, ls[0], ls[1])


def _dilated_kernel(q_ref, k_ref, v_ref, bias_ref, out_ref, o_scr, m_scr, l_scr):
    s = q_ref.shape[0]
    for pi, geo in enumerate(_dil_geometry(s)):
        dil, qb = geo[0], geo[1]
        njobs = s // qb
        unroll = DIL_ITER_ROWS // qb
        assert njobs % unroll == 0
        blk = functools.partial(_dil_block, q_ref, k_ref, v_ref, bias_ref, o_scr, m_scr, l_scr, pi, geo,
                                s // dil // qb)

        def group(g, c, blk=blk, unroll=unroll):
            for u in range(unroll):
                blk(g * unroll + u)
            return c

        lax.fori_loop(0, njobs // unroll, group, 0)

    chunk = 512

    def comb(i, c):
        rows = pl.ds(pl.multiple_of(i * chunk, chunk), chunk)
        m0, m1, m2 = m_scr[0, rows, :], m_scr[1, rows, :], m_scr[2, rows, :]
        mx = jnp.maximum(jnp.maximum(m0, m1), m2)
        e0, e1, e2 = jnp.exp2(m0 - mx), jnp.exp2(m1 - mx), jnp.exp2(m2 - mx)
        num = e0 * o_scr[0, rows, :] + e1 * o_scr[1, rows, :] + e2 * o_scr[2, rows, :]
        den = e0 * l_scr[0, rows, :] + e1 * l_scr[1, rows, :] + e2 * l_scr[2, rows, :]
        out_ref[rows, :] = num / den
        return c

    lax.fori_loop(0, s // chunk, comb, 0)


def _dilated(qa, ka, va, bias):
    b, s, _ = qa.shape
    npair = A_WIDTH // LANES
    blk = pl.BlockSpec((None, s, LANES), lambda bi, p: (bi, 0, p))
    return pl.pallas_call(
        _dilated_kernel,
        grid=(b, npair),
        in_specs=[blk, blk, blk,
                  pl.BlockSpec((bias.shape[0], 2) + bias.shape[2:], lambda bi, p: (0, p, 0, 0))],
        out_specs=blk,
        out_shape=jax.ShapeDtypeStruct((b, s, A_WIDTH), F32),
        scratch_shapes=[pltpu.VMEM((len(A_PATTERNS), s, LANES), F32)] * 3,
        compiler_params=_cparams(("parallel", "parallel")),
        name="dilated",
    )(qa, ka, va, bias)


def _mla_kernel(q_ref, k_ref, v_ref, w1_ref, w3_ref, w2_ref, o_ref, w1b_ref, w3b_ref, w2b_ref):
    w1b_ref[...] = w1_ref[...].astype(BF16)
    w3b_ref[...] = w3_ref[...].astype(BF16)
    w2b_ref[...] = w2_ref[...].astype(BF16)
    tq = q_ref.shape[1]
    nkc = k_ref.shape[1] // MLA_KC
    outs = []
    for hh in range(2):
        q = q_ref[hh]
        m = jnp.full((tq, 1), -jnp.inf, F32)
        l = jnp.zeros((tq, 1), F32)
        acc = jnp.zeros((tq, LANES), F32)
        for c in range(nkc):
            keys = slice(c * MLA_KC, (c + 1) * MLA_KC)
            sc = lax.dot_general(q, k_ref[hh, keys, :], (((1,), (1,)), ((), ())), preferred_element_type=F32)
            m_new = jnp.maximum(m, jnp.max(sc, axis=-1, keepdims=True))
            alpha = jnp.exp2(m - m_new)
            p = jnp.exp2(sc - m_new)
            l = alpha * l + jnp.sum(p, axis=-1, keepdims=True)
            acc = alpha * acc + jnp.dot(p.astype(BF16), v_ref[keys, :], preferred_element_type=F32)
            m = m_new
        outs.append(acc / l)
    lo = lax.broadcasted_iota(jnp.int32, outs[0].shape, 1) < B_VDIM
    o_ref[...] = jnp.where(lo, outs[0], outs[1])


def _mla(qb, kb, vb, w1, w3, w2):
    b, h, s, _ = qb.shape
    npair = h // 2
    tq = MLA_TQ
    nq = s // tq
    nsteps = b * npair * nq
    ne, d, f = w1.shape
    assert ne % nsteps == 0
    epb = ne // nsteps
    wmap = lambda bi, p, qi: ((bi * npair + p) * nq + qi, 0, 0)
    return pl.pallas_call(
        _mla_kernel,
        grid=(b, npair, nq),
        in_specs=[pl.BlockSpec((None, 2, tq, HEAD_PAD), lambda bi, p, qi: (bi, p, qi, 0)),
                  pl.BlockSpec((None, 2, s, HEAD_PAD), lambda bi, p, qi: (bi, p, 0, 0)),
                  pl.BlockSpec((None, None, s, LANES), lambda bi, p, qi: (bi, p, 0, 0)),
                  pl.BlockSpec((epb, d, f), wmap),
                  pl.BlockSpec((epb, d, f), wmap),
                  pl.BlockSpec((epb, f, d), wmap)],
        out_specs=[pl.BlockSpec((None, tq, LANES), lambda bi, p, qi: (bi, qi, p)),
                   pl.BlockSpec((epb, d, f), wmap),
                   pl.BlockSpec((epb, d, f), wmap),
                   pl.BlockSpec((epb, f, d), wmap)],
        out_shape=[jax.ShapeDtypeStruct((b, s, B_WIDTH), F32),
                   jax.ShapeDtypeStruct((ne, d, f), BF16),
                   jax.ShapeDtypeStruct((ne, d, f), BF16),
                   jax.ShapeDtypeStruct((ne, f, d), BF16)],
        compiler_params=_cparams(("parallel", "parallel", "arbitrary")),
        name="mla",
    )(qb, kb, vb, w1, w3, w2)


def _outproj_kernel(x_ref, a_ref, b_ref, ag_ref, bg_ref, wo_ref, g1_ref, n2_ref, sc_ref, sh_ref, g2_ref,
                    wrt_ref, ws1_ref, ws3_ref, ws2_ref, x1s_ref, h2p_ref, scores_ref):
    an = _rms(a_ref[...], ag_ref[...])
    bn = _rms(b_ref[...], bg_ref[...])
    mix = jnp.concatenate([an, bn], axis=-1).astype(BF16)
    x1 = x_ref[...] + g1_ref[...] * jnp.dot(mix, wo_ref[...], preferred_element_type=F32)
    h2 = _rms(x1, n2_ref[...]) * (1.0 + sc_ref[...]) + sh_ref[...]
    packed = _pack_rows(h2)
    for cg in range(h2p_ref.shape[0]):
        h2p_ref[cg] = packed[:, cg * LANES:(cg + 1) * LANES]
    h_hi, h_lo = _split_bf16(h2)
    logits = lax.dot_general(wrt_ref[...], jnp.concatenate([h_hi, h_lo, h_hi], axis=1),
                             (((1,), (1,)), ((), ())), preferred_element_type=F32)
    scores_ref[...] = jax.nn.sigmoid(logits)
    h2b = h2.astype(BF16)
    hid = _silu(jnp.dot(h2b, ws1_ref[...], preferred_element_type=F32)) * jnp.dot(
        h2b, ws3_ref[...], preferred_element_type=F32)
    shared = jnp.dot(hid.astype(BF16), ws2_ref[...], preferred_element_type=F32)
    x1s_ref[...] = x1 + g2_ref[...] * shared


def _outproj(x2, a_out, b_out, a_out_g, b_out_g, w_o, mod3, norm2_g, w_router_t, ws1, ws3, ws2, s):
    n, d = x2.shape
    tm = TM_OUTPROJ
    tpb = s // tm
    row = lambda i: (i, 0)
    const = lambda i: (0, 0)
    modspec = lambda j: pl.BlockSpec((None, 1, d), lambda i: ((i // tpb) * 6 + j, 0, 0))
    return pl.pallas_call(
        _outproj_kernel,
        grid=(n // tm,),
        in_specs=[pl.BlockSpec((tm, d), row),
                  pl.BlockSpec((tm, A_WIDTH), row),
                  pl.BlockSpec((tm, B_WIDTH), row),
                  pl.BlockSpec((1, A_WIDTH), const),
                  pl.BlockSpec((1, B_WIDTH), const),
                  pl.BlockSpec(w_o.shape, const),
                  modspec(2),
                  pl.BlockSpec((1, d), const),
                  modspec(4), modspec(3), modspec(5),
                  pl.BlockSpec(w_router_t.shape, const),
                  pl.BlockSpec(ws1.shape, const),
                  pl.BlockSpec(ws3.shape, const),
                  pl.BlockSpec(ws2.shape, const)],
        out_specs=[pl.BlockSpec((tm, d), row),
                   pl.BlockSpec((d // 2 // LANES, tm, LANES), lambda i: (0, i, 0)),
                   pl.BlockSpec((N_EXPERTS, tm), lambda i: (0, i))],
        out_shape=[jax.ShapeDtypeStruct((n, d), F32),
                   jax.ShapeDtypeStruct((d // 2 // LANES, n, LANES), U32),
                   jax.ShapeDtypeStruct((N_EXPERTS, n), F32)],
        compiler_params=_cparams(("parallel",)),
        name="outproj",
    )(x2, a_out, b_out, a_out_g, b_out_g, w_o, mod3, norm2_g, mod3, mod3, mod3, w_router_t, ws1, ws3, ws2)


def _slab_order(v):
    return v.reshape((N_GROUPS, GROUP_SIZE) + v.shape[1:]).swapaxes(0, 1).reshape(v.shape)


def _expert_order(v):
    return v.reshape((GROUP_SIZE, N_GROUPS) + v.shape[1:]).swapaxes(0, 1).reshape(v.shape)


def _sublane_all(x, op):
    for sh in (4, 2, 1):
        x = op(x, pltpu.roll(x, sh, 0))
    return x


def _route_kernel(st_ref, bias_ref, eidx_ref, gate_ref, cnt_ref):
    nsl = GROUP_SIZE
    t = st_ref.shape[1]
    sub = lax.broadcasted_iota(I32, (SUBLANES, t), 0)
    ninf = -jnp.inf
    big = jnp.int32(1 << 30)
    sc = [st_ref[j * SUBLANES:(j + 1) * SUBLANES, :] for j in range(nsl)]
    sel = [sc[j] + bias_ref[j * SUBLANES:(j + 1) * SUBLANES, :] for j in range(nsl)]
    eid = [sub * GROUP_SIZE + j for j in range(nsl)]

    m1 = sel[0]
    m2 = jnp.full_like(m1, ninf)
    for j in range(1, nsl):
        m2 = jnp.maximum(m2, jnp.minimum(m1, sel[j]))
        m1 = jnp.maximum(m1, sel[j])
    gs = m1 + m2

    rank = jnp.zeros((SUBLANES, t), I32)
    for sh in range(1, N_GROUPS):
        other = pltpu.roll(gs, sh, 0)
        ahead = (other > gs) | ((other == gs) & (sub >= sh))
        rank = rank + ahead.astype(I32)
    gmask = rank < TOPK_GROUPS

    msel = [jnp.where(gmask, sel[j], ninf) for j in range(nsl)]
    hits = [jnp.zeros((SUBLANES, t), I32) for _ in range(nsl)]
    eidx = jnp.zeros((TOP_K, t), I32)
    gates = jnp.zeros((TOP_K, t), F32)
    for k in range(TOP_K):
        mx = msel[0]
        for j in range(1, nsl):
            mx = jnp.maximum(mx, msel[j])
        mx = _sublane_all(mx, jnp.maximum)
        cand = jnp.where(msel[0] == mx, eid[0], big)
        for j in range(1, nsl):
            cand = jnp.minimum(cand, jnp.where(msel[j] == mx, eid[j], big))
        idx = _sublane_all(cand, jnp.minimum)
        gk = jnp.zeros((SUBLANES, t), F32)
        for j in range(nsl):
            hit = eid[j] == idx
            gk = gk + jnp.where(hit, sc[j], 0.0)
            msel[j] = jnp.where(hit, ninf, msel[j])
            hits[j] = hits[j] + hit.astype(I32)
        gk = _sublane_all(gk, jnp.add)
        eidx = jnp.where(sub == k, idx, eidx)
        gates = jnp.where(sub == k, gk, gates)
    gsum = _sublane_all(gates, jnp.add)
    eidx_ref[...] = eidx
    gate_ref[...] = gates / gsum * ROUTED_SCALE

    @pl.when(pl.program_id(0) == 0)
    def _():
        cnt_ref[...] = jnp.zeros_like(cnt_ref)

    for j in range(nsl):
        cnt_ref[j * SUBLANES:(j + 1) * SUBLANES, :] += jnp.sum(hits[j].astype(F32), axis=1,
                                                               keepdims=True).astype(I32)


def _route(scores_t, e_bias_slab):
    e, n = scores_t.shape
    t = ROUTE_T
    return pl.pallas_call(
        _route_kernel,
        grid=(n // t,),
        in_specs=[pl.BlockSpec((e, t), lambda i: (0, i)),
                  pl.BlockSpec((e, 1), lambda i: (0, 0))],
        out_specs=[pl.BlockSpec((TOP_K, t), lambda i: (0, i)),
                   pl.BlockSpec((TOP_K, t), lambda i: (0, i)),
                   pl.BlockSpec((e, 1), lambda i: (0, 0))],
        out_shape=[jax.ShapeDtypeStruct((TOP_K, n), I32),
                   jax.ShapeDtypeStruct((TOP_K, n), F32),
                   jax.ShapeDtypeStruct((e, 1), I32)],
        compiler_params=_cparams(("arbitrary",)),
        name="route",
    )(scores_t, e_bias_slab)


def _dest_kernel(eidx_ref, pstart_ref, dest_ref, carry_ref):
    @pl.when(pl.program_id(0) == 0)
    def _():
        carry_ref[...] = jnp.zeros_like(carry_ref)

    nsl = GROUP_SIZE
    t = eidx_ref.shape[1]
    sub = lax.broadcasted_iota(I32, (SUBLANES, t), 0)
    eid = [sub * GROUP_SIZE + j for j in range(nsl)]
    ek = [eidx_ref[k:k + 1, :] for k in range(TOP_K)]
    slabs = []
    for j in range(nsl):
        oh = jnp.zeros((SUBLANES, t), F32)
        for k in range(TOP_K):
            oh = oh + (eid[j] == ek[k]).astype(F32)
        slabs.append(oh)
    onehot = jnp.concatenate(slabs, axis=0)
    row = lax.broadcasted_iota(I32, (t, t), 0)
    col = lax.broadcasted_iota(I32, (t, t), 1)
    upper = (row < col).astype(BF16)
    before = jnp.dot(onehot.astype(BF16), upper, preferred_element_type=F32)
    base = before + carry_ref[...] + pstart_ref[...]
    dest = jnp.zeros((TOP_K, t), I32)
    for k in range(TOP_K):
        acc = jnp.zeros((SUBLANES, t), F32)
        for j in range(nsl):
            acc = acc + jnp.where(eid[j] == ek[k], base[j * SUBLANES:(j + 1) * SUBLANES, :], 0.0)
        dk = _sublane_all(acc, jnp.add).astype(I32)
        dest = jnp.where(sub == k, dk, dest)
    dest_ref[...] = dest
    carry_ref[...] += jnp.sum(onehot, axis=1, keepdims=True)


def _dest(eidx_t, pstart_slab):
    n = eidx_t.shape[1]
    t = ROUTE_T
    return pl.pallas_call(
        _dest_kernel,
        grid=(n // t,),
        in_specs=[pl.BlockSpec((TOP_K, t), lambda i: (0, i)),
                  pl.BlockSpec((N_EXPERTS, 1), lambda i: (0, 0))],
        out_specs=pl.BlockSpec((TOP_K, t), lambda i: (0, i)),
        out_shape=jax.ShapeDtypeStruct((TOP_K, n), I32),
        scratch_shapes=[pltpu.VMEM((N_EXPERTS, 1), F32)],
        compiler_params=_cparams(("arbitrary",)),
        name="dest",
    )(eidx_t, pstart_slab)


def _sc_scatter(x, idx, rows_out, nplane, nslot):
    del nplane
    num = idx.shape[0]
    mesh = plsc.VectorSubcoreMesh(core_axis_name="core", subcore_axis_name="subcore")

    @pl.kernel(out_type=jax.ShapeDtypeStruct((rows_out, x.shape[1]), x.dtype), mesh=mesh, scratch_types=[])
    def scatter(x_hbm, i_hbm, o_hbm):
        def body(x_vmem, i_vmem):
            pltpu.sync_copy(x_vmem, o_hbm.at[i_vmem.at[0]])

        pltpu.emit_pipeline(
            body,
            grid=(num // SC_WINDOW,),
            in_specs=[pl.BlockSpec((SC_WINDOW, x.shape[1]), index_map=lambda i: (i // nslot, 0)),
                      pl.BlockSpec((1, SC_WINDOW), index_map=lambda i: (0, i))],
            out_specs=[],
            core_axis_name=("core", "subcore"),
            dimension_semantics=(pltpu.PARALLEL,),
        )(x_hbm, i_hbm)

    return scatter(x, idx.reshape(1, num))


def _expert_kernel(cstart_ref, nused_ref, valid_ref, w1_ref, w3_ref, w2_ref, xg_ref, y_ref,
                   xbuf, ybuf, xsem, ysem):
    e = pl.program_id(0)
    c0 = cstart_ref[e]
    c1 = cstart_ref[e + 1]
    nused = nused_ref[0]
    nchunks = xg_ref.shape[1] // MOE_BLK

    def rows(g):
        start = g * MOE_BLK
        return pl.ds(start if isinstance(g, int) else pl.multiple_of(start, MOE_BLK), MOE_BLK)

    nplane = xg_ref.shape[0]

    def xcopies(g):
        slot = g % EXP_XBUF
        return [pltpu.make_async_copy(xg_ref.at[pl.ds(0, nplane), rows(g)], xbuf.at[slot], xsem.at[slot])]

    def ycopies(g, slot):
        return [pltpu.make_async_copy(ybuf.at[slot], y_ref.at[pl.ds(0, nplane), rows(g)], ysem.at[slot])]

    @pl.when(e == 0)
    def _():
        ybuf[...] = jnp.zeros(ybuf.shape, ybuf.dtype)
        for j in range(EXP_AHEAD):
            @pl.when(j < nused)
            def _():
                for cp in xcopies(j):
                    cp.start()

    @pl.when(c1 > c0)
    def _():
        def ffn(g, yslot, nrows):
            row_id = lax.broadcasted_iota(I32, (nrows, LANES), 0)
            keep = row_id < valid_ref[g]
            halves = [_unpack_rows(jnp.where(keep, xbuf[g % EXP_XBUF, c, :nrows], jnp.uint32(0)))
                      for c in range(nplane)]
            xb = jnp.concatenate([h[0] for h in halves] + [h[1] for h in halves], axis=1).astype(BF16)
            hid = _silu(jnp.dot(xb, w1_ref[...], preferred_element_type=F32)) * jnp.dot(
                xb, w3_ref[...], preferred_element_type=F32)
            yp = _pack_rows(jnp.dot(hid.astype(BF16), w2_ref[...], preferred_element_type=F32))
            for c in range(nplane):
                ybuf[yslot, c, :nrows] = yp[:, c * LANES:(c + 1) * LANES]

        def chunk(g, carry):
            @pl.when(g + EXP_AHEAD < nused)
            def _():
                for cp in xcopies(g + EXP_AHEAD):
                    cp.start()

            for cp in xcopies(g):
                cp.wait()
            yslot = g % 2

            @pl.when(g >= 2)
            def _():
                for cp in ycopies(g - 2, yslot):
                    cp.wait()

            half = MOE_BLK // 2

            @pl.when(valid_ref[g] > half)
            def _():
                ffn(g, yslot, MOE_BLK)

            @pl.when(valid_ref[g] <= half)
            def _():
                ffn(g, yslot, half)

            for cp in ycopies(g, yslot):
                cp.start()
            return carry

        lax.fori_loop(c0, c1, chunk, 0)

    @pl.when(e == pl.num_programs(0) - 1)
    def _():
        for back in (2, 1):
            g = nused - back

            @pl.when(g >= 0)
            def _():
                for cp in ycopies(g, g % 2):
                    cp.wait()

        ybuf[0] = jnp.zeros(ybuf.shape[1:], ybuf.dtype)

        def zstart(g, c):
            for cp in ycopies(g, 0):
                cp.start()
            return c

        def zwait(g, c):
            for cp in ycopies(g, 0):
                cp.wait()
            return c

        lax.fori_loop(nused, nchunks, zstart, 0)
        lax.fori_loop(nused, nchunks, zwait, 0)


def _experts(cstart, nused, valid, xg, w1, w3, w2):
    nplane, rows, _ = xg.shape
    dh = nplane * LANES
    ne, d, f = w1.shape
    grid_spec = pltpu.PrefetchScalarGridSpec(
        num_scalar_prefetch=3,
        grid=(ne,),
        in_specs=[pl.BlockSpec((None, d, f), lambda e, cs, nu, va: (e, 0, 0)),
                  pl.BlockSpec((None, d, f), lambda e, cs, nu, va: (e, 0, 0)),
                  pl.BlockSpec((None, f, d), lambda e, cs, nu, va: (e, 0, 0)),
                  pl.BlockSpec(memory_space=pl.ANY)],
        out_specs=pl.BlockSpec(memory_space=pl.ANY),
        scratch_shapes=[pltpu.VMEM((EXP_XBUF, nplane, MOE_BLK, LANES), U32),
                        pltpu.VMEM((2, nplane, MOE_BLK, LANES), U32),
                        pltpu.SemaphoreType.DMA((EXP_XBUF,)), pltpu.SemaphoreType.DMA((2,))],
    )
    return pl.pallas_call(
        _expert_kernel,
        grid_spec=grid_spec,
        out_shape=jax.ShapeDtypeStruct((dh // LANES, rows, LANES), U32),
        compiler_params=_cparams(("arbitrary",)),
        name="experts",
    )(cstart, nused, valid, w1, w3, w2, xg)


def _sc_gather(x, idx):
    num = idx.shape[0]
    mesh = plsc.VectorSubcoreMesh(core_axis_name="core", subcore_axis_name="subcore")

    @pl.kernel(out_type=jax.ShapeDtypeStruct((num, x.shape[1]), x.dtype), mesh=mesh)
    def gather(x_hbm, i_hbm, o_hbm):
        def body(i_vmem, o_vmem):
            pltpu.sync_copy(x_hbm.at[i_vmem.at[0]], o_vmem)

        pltpu.emit_pipeline(
            body,
            grid=(num // SC_WINDOW,),
            in_specs=[pl.BlockSpec((1, SC_WINDOW), index_map=lambda i: (0, i))],
            out_specs=[pl.BlockSpec((SC_WINDOW, x.shape[1]), index_map=lambda i: (i, 0))],
            core_axis_name=("core", "subcore"),
            dimension_semantics=(pltpu.PARALLEL,),
        )(i_hbm, o_hbm)

    return gather(x, idx.reshape(1, num))


def _combine_kernel(yg_ref, gate_ref, x1s_ref, g2_ref, fg_ref, *rest):
    out_ref = rest[-1]
    gates = gate_ref[...]
    nch = yg_ref.shape[0]
    r_lo = [None] * nch
    r_hi = [None] * nch
    for k in range(TOP_K):
        gk = gates[:, k:k + 1]
        for c in range(nch):
            lo, hi = _unpack_rows(yg_ref[c, k])
            r_lo[c] = gk * lo if k == 0 else r_lo[c] + gk * lo
            r_hi[c] = gk * hi if k == 0 else r_hi[c] + gk * hi
    routed = jnp.concatenate(r_lo + r_hi, axis=1)
    x2 = x1s_ref[...] + g2_ref[...] * routed
    out_ref[...] = _rms(x2, fg_ref[...])


def _combine(yg, gates, x1s, mod3, final_g, s, part, prev_out):
    n, d = x1s.shape
    t = COMB_T
    tpb = s // t
    nch, _, npart, _ = yg.shape
    off = part * (npart // t)
    in_specs = [pl.BlockSpec((nch, TOP_K, t, LANES), lambda i: (0, 0, i, 0)),
                pl.BlockSpec((t, TOP_K), lambda i: (i + off, 0)),
                pl.BlockSpec((t, d), lambda i: (i + off, 0)),
                pl.BlockSpec((None, 1, d), lambda i: (((i + off) // tpb) * 6 + 5, 0, 0)),
                pl.BlockSpec((1, d), lambda i: (0, 0))]
    args = [yg, gates, x1s, mod3, final_g]
    aliases = {}
    if prev_out is not None:
        in_specs.append(pl.BlockSpec(memory_space=pl.ANY))
        args.append(prev_out)
        aliases = {len(args) - 1: 0}
    return pl.pallas_call(
        _combine_kernel,
        grid=(npart // t,),
        in_specs=in_specs,
        out_specs=pl.BlockSpec((t, d), lambda i: (i + off, 0)),
        out_shape=jax.ShapeDtypeStruct((n, d), F32),
        input_output_aliases=aliases,
        compiler_params=_cparams(("parallel",)),
        name="combine",
    )(*args)


def _place_cols(w, lanes):
    r, h, _ = w.shape
    src = {lane: j for j, lane in enumerate(lanes)}
    pieces, lane = [], 0
    while lane < HEAD_PAD:
        end = lane + 1
        if lane in src:
            while end < HEAD_PAD and end in src and src[end] == src[end - 1] + 1:
                end += 1
            pieces.append(w[:, :, src[lane]:src[lane] + end - lane])
        else:
            while end < HEAD_PAD and end not in src:
                end += 1
            pieces.append(jnp.zeros((r, h, end - lane), w.dtype))
        lane = end
    return jnp.concatenate(pieces, axis=2).reshape(r, h * HEAD_PAD)


def _rope_tables(s):
    half = B_ROPE // 2
    rope_lanes = _head_lanes()[B_NOPE:]
    inv = ROPE_THETA ** (-jnp.arange(0, B_ROPE, 2, dtype=jnp.float32) / B_ROPE)
    ang = jnp.arange(s, dtype=jnp.float32)[:, None] * inv[None, :]
    cos, sin = jnp.cos(ang), jnp.sin(ang)
    assert cos.shape[1] == half
    cos_t = 1.0 + _place_cols((jnp.concatenate([cos, cos], axis=1) - 1.0)[:, None, :], rope_lanes)
    sin_t = _place_cols(jnp.concatenate([-sin, sin], axis=1)[:, None, :], rope_lanes)
    return cos_t, sin_t


def kernel(x, c, w_ada, b_ada, norm1_g, w_in, q_norm_g, w_uq, kv_norm_g, w_ukv, rel_table, a_out_g, b_out_g,
           w_o, norm2_g, w_router, e_bias, w1, w3, w2, ws1, ws3, ws2, final_g):
    b, s, d = x.shape
    n = b * s
    assert w_ada.shape[0] == 1, "single layer"
    x2 = x.reshape(n, d)

    mod = _ada(c, w_ada[0], b_ada[0])
    mod3 = mod.reshape(b * 6, 1, d)
    bias = _bias_tiles(rel_table, _dil_geometry(s))

    wi = w_in[0]
    c_kpe = 3 * A_WIDTH + Q_LORA + KV_LORA
    lanes = _head_lanes()
    kpe_cols = _place_cols(wi[:, None, c_kpe:], lanes[B_NOPE:])
    w_in_ext = jnp.concatenate([wi[:, :c_kpe], kpe_cols], axis=1).astype(BF16)
    w_uq_p = _place_cols(w_uq[0].reshape(Q_LORA, B_HEADS, B_NOPE + B_ROPE), lanes).astype(BF16)
    w_uk_p = _place_cols(w_ukv[0].reshape(KV_LORA, B_HEADS, B_NOPE + B_VDIM)[:, :, :B_NOPE],
                         lanes[:B_NOPE]).astype(BF16)
    w_v = w_ukv[0].reshape(KV_LORA, B_HEADS, B_NOPE + B_VDIM)[:, :, B_NOPE:].reshape(KV_LORA, B_WIDTH).astype(BF16)
    cos_t, sin_t = _rope_tables(s)

    qa, ka, va, qb, kb, vb = _inproj(x2, mod3, norm1_g, w_in_ext, q_norm_g, w_uq_p, kv_norm_g, w_uk_p, w_v,
                                     cos_t, sin_t, b, s)
    a_out = _dilated(qa.reshape(b, s, A_WIDTH), ka.reshape(b, s, A_WIDTH), va.reshape(b, s, A_WIDTH), bias)
    b_out, w1_bf, w3_bf, w2_bf = _mla(qb, kb, vb, w1[0], w3[0], w2[0])

    wr = _slab_order(w_router[0].T)
    wr_hi, wr_lo = _split_bf16(wr)
    w_router_t = jnp.concatenate([wr_hi, wr_hi, wr_lo], axis=1)
    x1s, h2p, scores_t = _outproj(x2, a_out.reshape(n, A_WIDTH), b_out.reshape(n, B_WIDTH), a_out_g, b_out_g,
                                  w_o[0].astype(BF16), mod3, norm2_g, w_router_t,
                                  ws1[0].astype(BF16), ws3[0].astype(BF16), ws2[0].astype(BF16), s)

    eidx_t, gates_t, counts_slab = _route(scores_t, _slab_order(e_bias.reshape(N_EXPERTS, 1)))

    counts = _expert_order(counts_slab)[:, 0]
    padded = (counts + MOE_BLK - 1) // MOE_BLK * MOE_BLK
    pends = jnp.cumsum(padded)
    pstart = pends - padded
    nk = n * TOP_K
    nblk = -(-(nk + N_EXPERTS * (MOE_BLK - 1)) // MOE_BLK)
    rows_total = nblk * MOE_BLK
    blk_row = jnp.arange(nblk, dtype=I32) * MOE_BLK
    blk_e = jnp.minimum(jnp.sum((pends[None, :] <= blk_row[:, None]).astype(I32), axis=1), N_EXPERTS - 1)
    seg_end = (pstart + counts)[blk_e]
    valid = jnp.clip(seg_end - blk_row, 0, MOE_BLK).astype(I32)
    nused = (pends[-1] // MOE_BLK).astype(I32).reshape(1)
    cstart = jnp.concatenate([jnp.zeros((1,), I32), (pends // MOE_BLK).astype(I32)])

    dest_t = _dest(eidx_t, _slab_order(pstart.astype(F32).reshape(N_EXPERTS, 1)))
    nplane = h2p.shape[0]
    gidx = (dest_t.reshape(1, nk) + (jnp.arange(nplane, dtype=I32) * rows_total)[:, None]).reshape(nplane * nk)
    nwin = n // SC_WINDOW
    sidx = dest_t.reshape(TOP_K, nwin, SC_WINDOW).transpose(1, 0, 2)[None] + (
        jnp.arange(nplane, dtype=I32) * rows_total)[:, None, None, None]
    xg = _sc_scatter(h2p.reshape(nplane * n, LANES), sidx.reshape(nplane * nk), nplane * rows_total, nplane, TOP_K)
    y = _experts(cstart, nused, valid, xg.reshape(nplane, rows_total, LANES), w1_bf, w3_bf, w2_bf)
    y_flat = y.reshape(nplane * rows_total, LANES)
    gidx3 = gidx.reshape(nplane, TOP_K, n)
    gates = gates_t.T
    npart = n // COMB_SPLIT
    out = None
    for part in range(COMB_SPLIT):
        pidx = gidx3[:, :, part * npart:(part + 1) * npart].reshape(nplane * TOP_K * npart)
        yg = _sc_gather(y_flat, pidx).reshape(nplane, TOP_K, npart, LANES)
        out = _combine(yg, gates, x1s, mod3, final_g.reshape(1, d), s, part, out)
    return out.reshape(b, s, d)
```

```python
import functools
import math

import jax
import jax.numpy as jnp
from jax import lax
from jax.experimental import pallas as pl
from jax.experimental.pallas import tpu as pltpu
from jax.experimental.pallas import tpu_sc as plsc

F32 = jnp.float32
BF16 = jnp.bfloat16
U32 = jnp.uint32
I32 = jnp.int32
HIGHEST = lax.Precision.HIGHEST

D_MODEL = 1024
A_HEADS = 8
A_HEAD_DIM = 64
A_WIDTH = A_HEADS * A_HEAD_DIM
A_PATTERNS = ((128, 1), (512, 4), (2048, 16))
A_RADIUS = 64
REL_BUCKETS = 32
REL_MAX_DIST = 1024
B_HEADS = 8
B_NOPE = 64
B_ROPE = 32
B_VDIM = 64
B_WIDTH = B_HEADS * B_VDIM
Q_LORA = 384
KV_LORA = 256
ROPE_THETA = 10000.0
N_EXPERTS = 256
TOP_K = 8
N_GROUPS = 8
GROUP_SIZE = N_EXPERTS // N_GROUPS
TOPK_GROUPS = 4
EXPERT_FF = 256
SHARED_FF = 256
ROUTED_SCALE = 2.5
EPS = 1e-6
NEG_INF = -1e30
LOG2E = math.log2(math.e)

LANES = 128
SUBLANES = 8
HEAD_PAD = 128
IN_COLS_EXT = 3 * A_WIDTH + Q_LORA + KV_LORA + HEAD_PAD

TM_INPROJ = 512
TM_OUTPROJ = 1024
DIL_QB = 128
DIL_KW = DIL_QB + 2 * A_RADIUS
DIL_ITER_ROWS = 4096
MLA_TQ = 1024
MLA_KC = 2048
ROUTE_T = 512
MOE_BLK = 512
EXP_AHEAD = 3
EXP_XBUF = EXP_AHEAD + 2
SC_WINDOW = 128
COMB_T = 256
COMB_SPLIT = 8
VMEM_LIMIT = 56 * 1024 * 1024


def _cparams(sem):
    return pltpu.CompilerParams(dimension_semantics=sem, vmem_limit_bytes=VMEM_LIMIT)


def _rms(x, g):
    return x * lax.rsqrt(jnp.mean(x * x, axis=-1, keepdims=True) + EPS) * g


def _silu(x):
    return x * jax.nn.sigmoid(x)


def _split_bf16(x):
    hi = lax.bitcast_convert_type(lax.bitcast_convert_type(x, U32) & jnp.uint32(0xFFFF0000), F32)
    return hi.astype(BF16), (x - hi).astype(BF16)


def _pack_rows(x):
    half = x.shape[1] // 2
    bits = lax.bitcast_convert_type(x.astype(BF16).astype(F32), U32)
    return (bits[:, :half] >> 16) | bits[:, half:]


def _unpack_rows(w):
    lo = lax.bitcast_convert_type(w << 16, F32)
    hi = lax.bitcast_convert_type(w & jnp.uint32(0xFFFF0000), F32)
    return lo, hi


def _ada_kernel(c_ref, w_ref, b_ref, o_ref):
    o_ref[...] = jnp.dot(_silu(c_ref[...]), w_ref[...], precision=HIGHEST,
                         preferred_element_type=F32) + b_ref[...]


def _ada(c, w_ada, b_ada):
    b, d = c.shape
    n6 = w_ada.shape[1] // d
    return pl.pallas_call(
        _ada_kernel,
        grid=(n6,),
        in_specs=[pl.BlockSpec((b, d), lambda j: (0, 0)),
                  pl.BlockSpec((d, d), lambda j: (0, j)),
                  pl.BlockSpec((1, d), lambda j: (0, j))],
        out_specs=pl.BlockSpec((b, d), lambda j: (0, j)),
        out_shape=jax.ShapeDtypeStruct((b, n6 * d), F32),
        compiler_params=_cparams(("parallel",)),
        name="ada",
    )(c, w_ada, b_ada.reshape(1, -1))


def _t5_bucket(rel):
    half = REL_BUCKETS // 2
    max_exact = half // 2
    ret = jnp.where(rel > 0, half, 0)
    n = jnp.abs(rel)
    nf = jnp.maximum(n, 1).astype(jnp.float32)
    large = max_exact + (jnp.log(nf / max_exact) / math.log(REL_MAX_DIST / max_exact)
                         * (half - max_exact)).astype(jnp.int32)
    large = jnp.minimum(large, half - 1)
    return ret + jnp.where(n < max_exact, n, large)


def _dil_geometry(s):
    geo, base = [], 0
    for _, dil in A_PATTERNS:
        sub_len = s // dil
        qb = sub_len if sub_len <= DIL_KW else DIL_QB
        kw = min(qb + 2 * A_RADIUS, sub_len)
        assert sub_len % qb == 0 and (sub_len == qb or kw == qb + 2 * A_RADIUS)
        shifts = (A_RADIUS,) if sub_len == qb else (A_RADIUS, 0, -A_RADIUS)
        geo.append((dil, qb, kw, shifts, base))
        base += len(shifts)
    return geo


def _bucket_tiles(geo):
    qi = jnp.arange(max(g[1] for g in geo), dtype=jnp.int32)[:, None]
    ki = jnp.arange(max(g[2] for g in geo), dtype=jnp.int32)[None, :]
    tiles = []
    for dilation, qb, kw, shifts, _ in geo:
        for shift in shifts:
            off = ki + shift - A_RADIUS - qi
            bkt = _t5_bucket(off * dilation)
            inside = (jnp.abs(off) <= A_RADIUS) & (qi < qb) & (ki < kw)
            tiles.append(jnp.where(inside, bkt, -1))
    return jnp.stack(tiles, axis=0)


def _bias_kernel(tab_ref, bkt_ref, o_ref):
    bkt = bkt_ref[0]
    for h in range(A_HEADS):
        acc = jnp.full(bkt.shape, NEG_INF, F32)
        for b in range(REL_BUCKETS):
            acc = jnp.where(bkt == b, tab_ref[b, h] * LOG2E, acc)
        o_ref[0, h] = acc


def _bias_tiles(rel_table, geo):
    bkt = _bucket_tiles(geo)
    nt = bkt.shape[0]
    return pl.pallas_call(
        _bias_kernel,
        grid=(nt,),
        in_specs=[pl.BlockSpec(memory_space=pltpu.SMEM),
                  pl.BlockSpec((1,) + bkt.shape[1:], lambda t: (t, 0, 0))],
        out_specs=pl.BlockSpec((1, A_HEADS) + bkt.shape[1:], lambda t: (t, 0, 0, 0)),
        out_shape=jax.ShapeDtypeStruct((nt, A_HEADS) + bkt.shape[1:], F32),
        compiler_params=_cparams(("parallel",)),
        name="bias",
    )(rel_table, bkt)


def _head_lanes():
    half = B_ROPE // 2
    lanes = []
    for j in range(B_NOPE):
        lanes.append(half + j if j < HEAD_PAD // 2 - half else HEAD_PAD // 2 + half + (j - (HEAD_PAD // 2 - half)))
    for r in range(B_ROPE):
        lanes.append(r if r < half else HEAD_PAD // 2 + (r - half))
    return lanes


def _rope(x, cos, sin):
    return x * cos + pltpu.roll(x, HEAD_PAD // 2, 1) * sin


def _inproj_kernel(x_ref, sc_ref, sh_ref, g1_ref, win_ref, qg_ref, wuq_ref, kvg_ref, wuk_ref, wv_ref,
                   cos_ref, sin_ref, qa_ref, ka_ref, va_ref, qb_ref, kb_ref, vb_ref):
    x = x_ref[...]
    h = _rms(x, g1_ref[...]) * (1.0 + sc_ref[...]) + sh_ref[...]
    proj = jnp.dot(h.astype(BF16), win_ref[...], preferred_element_type=F32)
    aw = A_WIDTH
    qa_ref[...] = proj[:, 0:aw] * (LOG2E / math.sqrt(A_HEAD_DIM))
    ka_ref[...] = proj[:, aw:2 * aw]
    va_ref[...] = proj[:, 2 * aw:3 * aw]
    c0 = 3 * aw
    q_lat = proj[:, c0:c0 + Q_LORA]
    kv_lat = proj[:, c0 + Q_LORA:c0 + Q_LORA + KV_LORA]
    kpe = proj[:, c0 + Q_LORA + KV_LORA:]
    qn = _rms(q_lat, qg_ref[...]).astype(BF16)
    kvn = _rms(kv_lat, kvg_ref[...]).astype(BF16)
    qm = jnp.dot(qn, wuq_ref[...], preferred_element_type=F32)
    kn = jnp.dot(kvn, wuk_ref[...], preferred_element_type=F32)
    vv = jnp.dot(kvn, wv_ref[...], preferred_element_type=F32)
    cos = cos_ref[...]
    sin = sin_ref[...]
    qscale = LOG2E / math.sqrt(B_NOPE + B_ROPE)
    cos_q = cos * qscale
    sin_q = sin * qscale
    kpe_rot = _rope(kpe, cos, sin)
    for hd in range(B_HEADS):
        sl = slice(hd * HEAD_PAD, (hd + 1) * HEAD_PAD)
        qb_ref[hd] = _rope(qm[:, sl], cos_q, sin_q).astype(BF16)
        kb_ref[hd] = (kn[:, sl] + kpe_rot).astype(BF16)
    for p in range(B_HEADS // 2):
        vb_ref[p] = vv[:, p * LANES:(p + 1) * LANES].astype(BF16)


def _inproj(x2, mod3, norm1_g, w_in_ext, q_norm_g, w_uq_p, kv_norm_g, w_uk_p, w_v, cos_t, sin_t, b, s):
    n, d = x2.shape
    tm = TM_INPROJ
    tpb = s // tm
    row = lambda i: (i, 0)
    const = lambda i: (0, 0)
    hm = lambda i: (i // tpb, 0, i % tpb, 0)
    return pl.pallas_call(
        _inproj_kernel,
        grid=(n // tm,),
        in_specs=[pl.BlockSpec((tm, d), row),
                  pl.BlockSpec((None, 1, d), lambda i: ((i // tpb) * 6 + 1, 0, 0)),
                  pl.BlockSpec((None, 1, d), lambda i: ((i // tpb) * 6 + 0, 0, 0)),
                  pl.BlockSpec((1, d), const),
                  pl.BlockSpec(w_in_ext.shape, const),
                  pl.BlockSpec((1, Q_LORA), const),
                  pl.BlockSpec(w_uq_p.shape, const),
                  pl.BlockSpec((1, KV_LORA), const),
                  pl.BlockSpec(w_uk_p.shape, const),
                  pl.BlockSpec(w_v.shape, const),
                  pl.BlockSpec((tm, HEAD_PAD), lambda i: (i % tpb, 0)),
                  pl.BlockSpec((tm, HEAD_PAD), lambda i: (i % tpb, 0))],
        out_specs=[pl.BlockSpec((tm, A_WIDTH), row),
                   pl.BlockSpec((tm, A_WIDTH), row),
                   pl.BlockSpec((tm, A_WIDTH), row),
                   pl.BlockSpec((None, B_HEADS, tm, HEAD_PAD), hm),
                   pl.BlockSpec((None, B_HEADS, tm, HEAD_PAD), hm),
                   pl.BlockSpec((None, B_HEADS // 2, tm, LANES), hm)],
        out_shape=[jax.ShapeDtypeStruct((n, A_WIDTH), F32),
                   jax.ShapeDtypeStruct((n, A_WIDTH), F32),
                   jax.ShapeDtypeStruct((n, A_WIDTH), F32),
                   jax.ShapeDtypeStruct((b, B_HEADS, s, HEAD_PAD), BF16),
                   jax.ShapeDtypeStruct((b, B_HEADS, s, HEAD_PAD), BF16),
                   jax.ShapeDtypeStruct((b, B_HEADS // 2, s, LANES), BF16)],
        compiler_params=_cparams(("parallel",)),
        name="inproj",
    )(x2, mod3, mod3, norm1_g, w_in_ext, q_norm_g, w_uq_p, kv_norm_g, w_uk_p, w_v, cos_t, sin_t)


def _dil_block(q_ref, k_ref, v_ref, bias_ref, o_scr, m_scr, l_scr, pi, geo, nblk, job):
    dil, qb, kwin, shifts, tile0 = geo
    sub_len = nblk * qb
    r = job // nblk
    bi = job % nblk
    q0 = bi * qb
    ws = jnp.clip(q0 - A_RADIUS, 0, sub_len - kwin)
    var = jnp.where(bi == 0, 0, jnp.where(bi == nblk - 1, 2, 1)) if len(shifts) > 1 else 0
    if dil == 1:
        qsl = pl.ds(pl.multiple_of(q0, qb), qb)
        ksl = pl.ds(pl.multiple_of(ws, A_RADIUS), kwin)
    else:
        qsl = pl.ds(r + dil * q0, qb, stride=dil)
        ksl = pl.ds(r + dil * ws, kwin, stride=dil)
    q = q_ref[qsl, :]
    kw = k_ref[ksl, :].astype(BF16)
    vw = v_ref[ksl, :].astype(BF16)
    lo = lax.broadcasted_iota(jnp.int32, q.shape, 1) < A_HEAD_DIM
    outs, ms, ls = [], [], []
    for hh in range(2):
        qm = jnp.where(lo if hh == 0 else jnp.logical_not(lo), q, 0.0).astype(BF16)
        sc = lax.dot_general(qm, kw, (((1,), (1,)), ((), ())), preferred_element_type=F32)
        sc = sc + bias_ref[tile0 + var, hh, :qb, :kwin]
        m = jnp.max(sc, axis=-1, keepdims=True)
        p = jnp.exp2(sc - m)
        l = jnp.sum(p, axis=-1, keepdims=True)
        outs.append(jnp.dot(p.astype(BF16), vw, preferred_element_type=F32))
        ms.append(m)
        ls.append(l)
    o_scr[pi, qsl, :] = jnp.where(lo, outs[0], outs[1])
    m_scr[pi, qsl, :] = jnp.where(lo, ms[0], ms[1])
    l_scr[pi, qsl, :] = jnp.where(lo, ls[0], ls[1])


def _dilated_kernel(q_ref, k_ref, v_ref, bias_ref, out_ref, o_scr, m_scr, l_scr):
    s = q_ref.shape[0]
    for pi, geo in enumerate(_dil_geometry(s)):
        dil, qb = geo[0], geo[1]
        njobs = s // qb
        unroll = DIL_ITER_ROWS // qb
        assert njobs % unroll == 0
        blk = functools.partial(_dil_block, q_ref, k_ref, v_ref, bias_ref, o_scr, m_scr, l_scr, pi, geo,
                                s // dil // qb)

        def group(g, c, blk=blk, unroll=unroll):
            for u in range(unroll):
                blk(g * unroll + u)
            return c

        lax.fori_loop(0, njobs // unroll, group, 0)

    chunk = 512

    def comb(i, c):
        rows = pl.ds(pl.multiple_of(i * chunk, chunk), chunk)
        m0, m1, m2 = m_scr[0, rows, :], m_scr[1, rows, :], m_scr[2, rows, :]
        mx = jnp.maximum(jnp.maximum(m0, m1), m2)
        e0, e1, e2 = jnp.exp2(m0 - mx), jnp.exp2(m1 - mx), jnp.exp2(m2 - mx)
        num = e0 * o_scr[0, rows, :] + e1 * o_scr[1, rows, :] + e2 * o_scr[2, rows, :]
        den = e0 * l_scr[0, rows, :] + e1 * l_scr[1, rows, :] + e2 * l_scr[2, rows, :]
        out_ref[rows, :] = num / den
        return c

    lax.fori_loop(0, s // chunk, comb, 0)


def _dilated(qa, ka, va, bias):
    b, s, _ = qa.shape
    npair = A_WIDTH // LANES
    blk = pl.BlockSpec((None, s, LANES), lambda bi, p: (bi, 0, p))
    return pl.pallas_call(
        _dilated_kernel,
        grid=(b, npair),
        in_specs=[blk, blk, blk,
                  pl.BlockSpec((bias.shape[0], 2) + bias.shape[2:], lambda bi, p: (0, p, 0, 0))],
        out_specs=blk,
        out_shape=jax.ShapeDtypeStruct((b, s, A_WIDTH), F32),
        scratch_shapes=[pltpu.VMEM((len(A_PATTERNS), s, LANES), F32)] * 3,
        compiler_params=_cparams(("parallel", "parallel")),
        name="dilated",
    )(qa, ka, va, bias)


def _mla_kernel(q_ref, k_ref, v_ref, w1_ref, w3_ref, w2_ref, o_ref, w1b_ref, w3b_ref, w2b_ref):
    w1b_ref[...] = w1_ref[...].astype(BF16)
    w3b_ref[...] = w3_ref[...].astype(BF16)
    w2b_ref[...] = w2_ref[...].astype(BF16)
    tq = q_ref.shape[1]
    nkc = k_ref.shape[1] // MLA_KC
    outs = []
    for hh in range(2):
        q = q_ref[hh]
        m = jnp.full((tq, 1), -jnp.inf, F32)
        l = jnp.zeros((tq, 1), F32)
        acc = jnp.zeros((tq, LANES), F32)
        for c in range(nkc):
            keys = slice(c * MLA_KC, (c + 1) * MLA_KC)
            sc = lax.dot_general(q, k_ref[hh, keys, :], (((1,), (1,)), ((), ())), preferred_element_type=F32)
            m_new = jnp.maximum(m, jnp.max(sc, axis=-1, keepdims=True))
            alpha = jnp.exp2(m - m_new)
            p = jnp.exp2(sc - m_new)
            l = alpha * l + jnp.sum(p, axis=-1, keepdims=True)
            acc = alpha * acc + jnp.dot(p.astype(BF16), v_ref[keys, :], preferred_element_type=F32)
            m = m_new
        outs.append(acc / l)
    lo = lax.broadcasted_iota(jnp.int32, outs[0].shape, 1) < B_VDIM
    o_ref[...] = jnp.where(lo, outs[0], outs[1])


def _mla(qb, kb, vb, w1, w3, w2):
    b, h, s, _ = qb.shape
    npair = h // 2
    tq = MLA_TQ
    nq = s // tq
    nsteps = b * npair * nq
    ne, d, f = w1.shape
    assert ne % nsteps == 0
    epb = ne // nsteps
    wmap = lambda bi, p, qi: ((bi * npair + p) * nq + qi, 0, 0)
    return pl.pallas_call(
        _mla_kernel,
        grid=(b, npair, nq),
        in_specs=[pl.BlockSpec((None, 2, tq, HEAD_PAD), lambda bi, p, qi: (bi, p, qi, 0)),
                  pl.BlockSpec((None, 2, s, HEAD_PAD), lambda bi, p, qi: (bi, p, 0, 0)),
                  pl.BlockSpec((None, None, s, LANES), lambda bi, p, qi: (bi, p, 0, 0)),
                  pl.BlockSpec((epb, d, f), wmap),
                  pl.BlockSpec((epb, d, f), wmap),
                  pl.BlockSpec((epb, f, d), wmap)],
        out_specs=[pl.BlockSpec((None, tq, LANES), lambda bi, p, qi: (bi, qi, p)),
                   pl.BlockSpec((epb, d, f), wmap),
                   pl.BlockSpec((epb, d, f), wmap),
                   pl.BlockSpec((epb, f, d), wmap)],
        out_shape=[jax.ShapeDtypeStruct((b, s, B_WIDTH), F32),
                   jax.ShapeDtypeStruct((ne, d, f), BF16),
                   jax.ShapeDtypeStruct((ne, d, f), BF16),
                   jax.ShapeDtypeStruct((ne, f, d), BF16)],
        compiler_params=_cparams(("parallel", "parallel", "arbitrary")),
        name="mla",
    )(qb, kb, vb, w1, w3, w2)


def _outproj_kernel(x_ref, a_ref, b_ref, ag_ref, bg_ref, wo_ref, g1_ref, n2_ref, sc_ref, sh_ref, g2_ref,
                    wrt_ref, ws1_ref, ws3_ref, ws2_ref, x1s_ref, h2p_ref, scores_ref):
    an = _rms(a_ref[...], ag_ref[...])
    bn = _rms(b_ref[...], bg_ref[...])
    mix = jnp.concatenate([an, bn], axis=-1).astype(BF16)
    x1 = x_ref[...] + g1_ref[...] * jnp.dot(mix, wo_ref[...], preferred_element_type=F32)
    h2 = _rms(x1, n2_ref[...]) * (1.0 + sc_ref[...]) + sh_ref[...]
    packed = _pack_rows(h2)
    for cg in range(h2p_ref.shape[0]):
        h2p_ref[cg] = packed[:, cg * LANES:(cg + 1) * LANES]
    h_hi, h_lo = _split_bf16(h2)
    logits = lax.dot_general(wrt_ref[...], jnp.concatenate([h_hi, h_lo, h_hi], axis=1),
                             (((1,), (1,)), ((), ())), preferred_element_type=F32)
    scores_ref[...] = jax.nn.sigmoid(logits)
    h2b = h2.astype(BF16)
    hid = _silu(jnp.dot(h2b, ws1_ref[...], preferred_element_type=F32)) * jnp.dot(
        h2b, ws3_ref[...], preferred_element_type=F32)
    shared = jnp.dot(hid.astype(BF16), ws2_ref[...], preferred_element_type=F32)
    x1s_ref[...] = x1 + g2_ref[...] * shared


def _outproj(x2, a_out, b_out, a_out_g, b_out_g, w_o, mod3, norm2_g, w_router_t, ws1, ws3, ws2, s):
    n, d = x2.shape
    tm = TM_OUTPROJ
    tpb = s // tm
    row = lambda i: (i, 0)
    const = lambda i: (0, 0)
    modspec = lambda j: pl.BlockSpec((None, 1, d), lambda i: ((i // tpb) * 6 + j, 0, 0))
    return pl.pallas_call(
        _outproj_kernel,
        grid=(n // tm,),
        in_specs=[pl.BlockSpec((tm, d), row),
                  pl.BlockSpec((tm, A_WIDTH), row),
                  pl.BlockSpec((tm, B_WIDTH), row),
                  pl.BlockSpec((1, A_WIDTH), const),
                  pl.BlockSpec((1, B_WIDTH), const),
                  pl.BlockSpec(w_o.shape, const),
                  modspec(2),
                  pl.BlockSpec((1, d), const),
                  modspec(4), modspec(3), modspec(5),
                  pl.BlockSpec(w_router_t.shape, const),
                  pl.BlockSpec(ws1.shape, const),
                  pl.BlockSpec(ws3.shape, const),
                  pl.BlockSpec(ws2.shape, const)],
        out_specs=[pl.BlockSpec((tm, d), row),
                   pl.BlockSpec((d // 2 // LANES, tm, LANES), lambda i: (0, i, 0)),
                   pl.BlockSpec((N_EXPERTS, tm), lambda i: (0, i))],
        out_shape=[jax.ShapeDtypeStruct((n, d), F32),
                   jax.ShapeDtypeStruct((d // 2 // LANES, n, LANES), U32),
                   jax.ShapeDtypeStruct((N_EXPERTS, n), F32)],
        compiler_params=_cparams(("parallel",)),
        name="outproj",
    )(x2, a_out, b_out, a_out_g, b_out_g, w_o, mod3, norm2_g, mod3, mod3, mod3, w_router_t, ws1, ws3, ws2)


def _slab_order(v):
    return v.reshape((N_GROUPS, GROUP_SIZE) + v.shape[1:]).swapaxes(0, 1).reshape(v.shape)


def _expert_order(v):
    return v.reshape((GROUP_SIZE, N_GROUPS) + v.shape[1:]).swapaxes(0, 1).reshape(v.shape)


def _sublane_all(x, op):
    for sh in (4, 2, 1):
        x = op(x, pltpu.roll(x, sh, 0))
    return x


def _route_kernel(st_ref, bias_ref, eidx_ref, gate_ref, cnt_ref):
    nsl = GROUP_SIZE
    t = st_ref.shape[1]
    sub = lax.broadcasted_iota(I32, (SUBLANES, t), 0)
    ninf = -jnp.inf
    big = jnp.int32(1 << 30)
    sc = [st_ref[j * SUBLANES:(j + 1) * SUBLANES, :] for j in range(nsl)]
    sel = [sc[j] + bias_ref[j * SUBLANES:(j + 1) * SUBLANES, :] for j in range(nsl)]
    eid = [sub * GROUP_SIZE + j for j in range(nsl)]

    m1 = sel[0]
    m2 = jnp.full_like(m1, ninf)
    for j in range(1, nsl):
        m2 = jnp.maximum(m2, jnp.minimum(m1, sel[j]))
        m1 = jnp.maximum(m1, sel[j])
    gs = m1 + m2

    rank = jnp.zeros((SUBLANES, t), I32)
    for sh in range(1, N_GROUPS):
        other = pltpu.roll(gs, sh, 0)
        ahead = (other > gs) | ((other == gs) & (sub >= sh))
        rank = rank + ahead.astype(I32)
    gmask = rank < TOPK_GROUPS

    msel = [jnp.where(gmask, sel[j], ninf) for j in range(nsl)]
    hits = [jnp.zeros((SUBLANES, t), I32) for _ in range(nsl)]
    eidx = jnp.zeros((TOP_K, t), I32)
    gates = jnp.zeros((TOP_K, t), F32)
    for k in range(TOP_K):
        mx = msel[0]
        for j in range(1, nsl):
            mx = jnp.maximum(mx, msel[j])
        mx = _sublane_all(mx, jnp.maximum)
        cand = jnp.where(msel[0] == mx, eid[0], big)
        for j in range(1, nsl):
            cand = jnp.minimum(cand, jnp.where(msel[j] == mx, eid[j], big))
        idx = _sublane_all(cand, jnp.minimum)
        gk = jnp.zeros((SUBLANES, t), F32)
        for j in range(nsl):
            hit = eid[j] == idx
            gk = gk + jnp.where(hit, sc[j], 0.0)
            msel[j] = jnp.where(hit, ninf, msel[j])
            hits[j] = hits[j] + hit.astype(I32)
        gk = _sublane_all(gk, jnp.add)
        eidx = jnp.where(sub == k, idx, eidx)
        gates = jnp.where(sub == k, gk, gates)
    gsum = _sublane_all(gates, jnp.add)
    eidx_ref[...] = eidx
    gate_ref[...] = gates / gsum * ROUTED_SCALE

    @pl.when(pl.program_id(0) == 0)
    def _():
        cnt_ref[...] = jnp.zeros_like(cnt_ref)

    for j in range(nsl):
        cnt_ref[j * SUBLANES:(j + 1) * SUBLANES, :] += jnp.sum(hits[j].astype(F32), axis=1,
                                                               keepdims=True).astype(I32)


def _route(scores_t, e_bias_slab):
    e, n = scores_t.shape
    t = ROUTE_T
    return pl.pallas_call(
        _route_kernel,
        grid=(n // t,),
        in_specs=[pl.BlockSpec((e, t), lambda i: (0, i)),
                  pl.BlockSpec((e, 1), lambda i: (0, 0))],
        out_specs=[pl.BlockSpec((TOP_K, t), lambda i: (0, i)),
                   pl.BlockSpec((TOP_K, t), lambda i: (0, i)),
                   pl.BlockSpec((e, 1), lambda i: (0, 0))],
        out_shape=[jax.ShapeDtypeStruct((TOP_K, n), I32),
                   jax.ShapeDtypeStruct((TOP_K, n), F32),
                   jax.ShapeDtypeStruct((e, 1), I32)],
        compiler_params=_cparams(("arbitrary",)),
        name="route",
    )(scores_t, e_bias_slab)


def _dest_kernel(eidx_ref, pstart_ref, dest_ref, carry_ref):
    @pl.when(pl.program_id(0) == 0)
    def _():
        carry_ref[...] = jnp.zeros_like(carry_ref)

    nsl = GROUP_SIZE
    t = eidx_ref.shape[1]
    sub = lax.broadcasted_iota(I32, (SUBLANES, t), 0)
    eid = [sub * GROUP_SIZE + j for j in range(nsl)]
    ek = [eidx_ref[k:k + 1, :] for k in range(TOP_K)]
    slabs = []
    for j in range(nsl):
        oh = jnp.zeros((SUBLANES, t), F32)
        for k in range(TOP_K):
            oh = oh + (eid[j] == ek[k]).astype(F32)
        slabs.append(oh)
    onehot = jnp.concatenate(slabs, axis=0)
    row = lax.broadcasted_iota(I32, (t, t), 0)
    col = lax.broadcasted_iota(I32, (t, t), 1)
    upper = (row < col).astype(BF16)
    before = jnp.dot(onehot.astype(BF16), upper, preferred_element_type=F32)
    base = before + carry_ref[...] + pstart_ref[...]
    dest = jnp.zeros((TOP_K, t), I32)
    for k in range(TOP_K):
        acc = jnp.zeros((SUBLANES, t), F32)
        for j in range(nsl):
            acc = acc + jnp.where(eid[j] == ek[k], base[j * SUBLANES:(j + 1) * SUBLANES, :], 0.0)
        dk = _sublane_all(acc, jnp.add).astype(I32)
        dest = jnp.where(sub == k, dk, dest)
    dest_ref[...] = dest
    carry_ref[...] += jnp.sum(onehot, axis=1, keepdims=True)


def _dest(eidx_t, pstart_slab):
    n = eidx_t.shape[1]
    t = ROUTE_T
    return pl.pallas_call(
        _dest_kernel,
        grid=(n // t,),
        in_specs=[pl.BlockSpec((TOP_K, t), lambda i: (0, i)),
                  pl.BlockSpec((N_EXPERTS, 1), lambda i: (0, 0))],
        out_specs=pl.BlockSpec((TOP_K, t), lambda i: (0, i)),
        out_shape=jax.ShapeDtypeStruct((TOP_K, n), I32),
        scratch_shapes=[pltpu.VMEM((N_EXPERTS, 1), F32)],
        compiler_params=_cparams(("arbitrary",)),
        name="dest",
    )(eidx_t, pstart_slab)


def _sc_scatter(x, idx, rows_out, nplane, nslot):
    del nplane
    num = idx.shape[0]
    mesh = plsc.VectorSubcoreMesh(core_axis_name="core", subcore_axis_name="subcore")

    @pl.kernel(out_type=jax.ShapeDtypeStruct((rows_out, x.shape[1]), x.dtype), mesh=mesh, scratch_types=[])
    def scatter(x_hbm, i_hbm, o_hbm):
        def body(x_vmem, i_vmem):
            pltpu.sync_copy(x_vmem, o_hbm.at[i_vmem.at[0]])

        pltpu.emit_pipeline(
            body,
            grid=(num // SC_WINDOW,),
            in_specs=[pl.BlockSpec((SC_WINDOW, x.shape[1]), index_map=lambda i: (i // nslot, 0)),
                      pl.BlockSpec((1, SC_WINDOW), index_map=lambda i: (0, i))],
            out_specs=[],
            core_axis_name=("core", "subcore"),
            dimension_semantics=(pltpu.PARALLEL,),
        )(x_hbm, i_hbm)

    return scatter(x, idx.reshape(1, num))


def _expert_kernel(cstart_ref, nused_ref, valid_ref, w1_ref, w3_ref, w2_ref, xg_ref, y_ref,
                   xbuf, ybuf, xsem, ysem):
    e = pl.program_id(0)
    c0 = cstart_ref[e]
    c1 = cstart_ref[e + 1]
    nused = nused_ref[0]
    nchunks = xg_ref.shape[1] // MOE_BLK

    def rows(g):
        start = g * MOE_BLK
        return pl.ds(start if isinstance(g, int) else pl.multiple_of(start, MOE_BLK), MOE_BLK)

    nplane = xg_ref.shape[0]

    def xcopies(g):
        slot = g % EXP_XBUF
        return [pltpu.make_async_copy(xg_ref.at[pl.ds(0, nplane), rows(g)], xbuf.at[slot], xsem.at[slot])]

    def ycopies(g, slot):
        return [pltpu.make_async_copy(ybuf.at[slot], y_ref.at[pl.ds(0, nplane), rows(g)], ysem.at[slot])]

    @pl.when(e == 0)
    def _():
        ybuf[...] = jnp.zeros(ybuf.shape, ybuf.dtype)
        for j in range(EXP_AHEAD):
            @pl.when(j < nused)
            def _():
                for cp in xcopies(j):
                    cp.start()

    @pl.when(c1 > c0)
    def _():
        def ffn(g, yslot, nrows):
            row_id = lax.broadcasted_iota(I32, (nrows, LANES), 0)
            keep = row_id < valid_ref[g]
            halves = [_unpack_rows(jnp.where(keep, xbuf[g % EXP_XBUF, c, :nrows], jnp.uint32(0)))
                      for c in range(nplane)]
            xb = jnp.concatenate([h[0] for h in halves] + [h[1] for h in halves], axis=1).astype(BF16)
            hid = _silu(jnp.dot(xb, w1_ref[...], preferred_element_type=F32)) * jnp.dot(
                xb, w3_ref[...], preferred_element_type=F32)
            yp = _pack_rows(jnp.dot(hid.astype(BF16), w2_ref[...], preferred_element_type=F32))
            for c in range(nplane):
                ybuf[yslot, c, :nrows] = yp[:, c * LANES:(c + 1) * LANES]

        def chunk(g, carry):
            @pl.when(g + EXP_AHEAD < nused)
            def _():
                for cp in xcopies(g + EXP_AHEAD):
                    cp.start()

            for cp in xcopies(g):
                cp.wait()
            yslot = g % 2

            @pl.when(g >= 2)
            def _():
                for cp in ycopies(g - 2, yslot):
                    cp.wait()

            half = MOE_BLK // 2

            @pl.when(valid_ref[g] > half)
            def _():
                ffn(g, yslot, MOE_BLK)

            @pl.when(valid_ref[g] <= half)
            def _():
                ffn(g, yslot, half)

            for cp in ycopies(g, yslot):
                cp.start()
            return carry

        def pair(i, carry):
            gs = (c0 + 2 * i, c0 + 2 * i + 1)
            for g in gs:
                @pl.when(g + EXP_AHEAD < nused)
                def _():
                    for cp in xcopies(g + EXP_AHEAD):
                        cp.start()
            for g in gs:
                for cp in xcopies(g):
                    cp.wait()
            for g in gs:
                @pl.when(g >= 2)
                def _():
                    for cp in ycopies(g - 2, g % 2):
                        cp.wait()
            for g in gs:
                ffn(g, g % 2, MOE_BLK)
            for g in gs:
                for cp in ycopies(g, g % 2):
                    cp.start()
            return carry

        npairs = (c1 - c0 - 1) // 2
        lax.fori_loop(0, npairs, pair, 0)
        lax.fori_loop(c0 + 2 * npairs, c1, chunk, 0)

    @pl.when(e == pl.num_programs(0) - 1)
    def _():
        for back in (2, 1):
            g = nused - back

            @pl.when(g >= 0)
            def _():
                for cp in ycopies(g, g % 2):
                    cp.wait()

        ybuf[0] = jnp.zeros(ybuf.shape[1:], ybuf.dtype)

        def zstart(g, c):
            for cp in ycopies(g, 0):
                cp.start()
            return c

        def zwait(g, c):
            for cp in ycopies(g, 0):
                cp.wait()
            return c

        lax.fori_loop(nused, nchunks, zstart, 0)
        lax.fori_loop(nused, nchunks, zwait, 0)


def _experts(cstart, nused, valid, xg, w1, w3, w2):
    nplane, rows, _ = xg.shape
    dh = nplane * LANES
    ne, d, f = w1.shape
    grid_spec = pltpu.PrefetchScalarGridSpec(
        num_scalar_prefetch=3,
        grid=(ne,),
        in_specs=[pl.BlockSpec((None, d, f), lambda e, cs, nu, va: (e, 0, 0)),
                  pl.BlockSpec((None, d, f), lambda e, cs, nu, va: (e, 0, 0)),
                  pl.BlockSpec((None, f, d), lambda e, cs, nu, va: (e, 0, 0)),
                  pl.BlockSpec(memory_space=pl.ANY)],
        out_specs=pl.BlockSpec(memory_space=pl.ANY),
        scratch_shapes=[pltpu.VMEM((EXP_XBUF, nplane, MOE_BLK, LANES), U32),
                        pltpu.VMEM((2, nplane, MOE_BLK, LANES), U32),
                        pltpu.SemaphoreType.DMA((EXP_XBUF,)), pltpu.SemaphoreType.DMA((2,))],
    )
    return pl.pallas_call(
        _expert_kernel,
        grid_spec=grid_spec,
        out_shape=jax.ShapeDtypeStruct((dh // LANES, rows, LANES), U32),
        compiler_params=_cparams(("arbitrary",)),
        name="experts",
    )(cstart, nused, valid, w1, w3, w2, xg)


def _sc_gather(x, idx):
    num = idx.shape[0]
    mesh = plsc.VectorSubcoreMesh(core_axis_name="core", subcore_axis_name="subcore")

    @pl.kernel(out_type=jax.ShapeDtypeStruct((num, x.shape[1]), x.dtype), mesh=mesh)
    def gather(x_hbm, i_hbm, o_hbm):
        def body(i_vmem, o_vmem):
            pltpu.sync_copy(x_hbm.at[i_vmem.at[0]], o_vmem)

        pltpu.emit_pipeline(
            body,
            grid=(num // SC_WINDOW,),
            in_specs=[pl.BlockSpec((1, SC_WINDOW), index_map=lambda i: (0, i))],
            out_specs=[pl.BlockSpec((SC_WINDOW, x.shape[1]), index_map=lambda i: (i, 0))],
            core_axis_name=("core", "subcore"),
            dimension_semantics=(pltpu.PARALLEL,),
        )(i_hbm, o_hbm)

    return gather(x, idx.reshape(1, num))


def _combine_kernel(yg_ref, gate_ref, x1s_ref, g2_ref, fg_ref, *rest):
    out_ref = rest[-1]
    gates = gate_ref[...]
    nch = yg_ref.shape[0]
    r_lo = [None] * nch
    r_hi = [None] * nch
    for k in range(TOP_K):
        gk = gates[:, k:k + 1]
        for c in range(nch):
            lo, hi = _unpack_rows(yg_ref[c, k])
            r_lo[c] = gk * lo if k == 0 else r_lo[c] + gk * lo
            r_hi[c] = gk * hi if k == 0 else r_hi[c] + gk * hi
    routed = jnp.concatenate(r_lo + r_hi, axis=1)
    x2 = x1s_ref[...] + g2_ref[...] * routed
    out_ref[...] = _rms(x2, fg_ref[...])


def _combine(yg, gates, x1s, mod3, final_g, s, part, prev_out):
    n, d = x1s.shape
    t = COMB_T
    tpb = s // t
    nch, _, npart, _ = yg.shape
    off = part * (npart // t)
    in_specs = [pl.BlockSpec((nch, TOP_K, t, LANES), lambda i: (0, 0, i, 0)),
                pl.BlockSpec((t, TOP_K), lambda i: (i + off, 0)),
                pl.BlockSpec((t, d), lambda i: (i + off, 0)),
                pl.BlockSpec((None, 1, d), lambda i: (((i + off) // tpb) * 6 + 5, 0, 0)),
                pl.BlockSpec((1, d), lambda i: (0, 0))]
    args = [yg, gates, x1s, mod3, final_g]
    aliases = {}
    if prev_out is not None:
        in_specs.append(pl.BlockSpec(memory_space=pl.ANY))
        args.append(prev_out)
        aliases = {len(args) - 1: 0}
    return pl.pallas_call(
        _combine_kernel,
        grid=(npart // t,),
        in_specs=in_specs,
        out_specs=pl.BlockSpec((t, d), lambda i: (i + off, 0)),
        out_shape=jax.ShapeDtypeStruct((n, d), F32),
        input_output_aliases=aliases,
        compiler_params=_cparams(("parallel",)),
        name="combine",
    )(*args)


def _place_cols(w, lanes):
    r, h, _ = w.shape
    src = {lane: j for j, lane in enumerate(lanes)}
    pieces, lane = [], 0
    while lane < HEAD_PAD:
        end = lane + 1
        if lane in src:
            while end < HEAD_PAD and end in src and src[end] == src[end - 1] + 1:
                end += 1
            pieces.append(w[:, :, src[lane]:src[lane] + end - lane])
        else:
            while end < HEAD_PAD and end not in src:
                end += 1
            pieces.append(jnp.zeros((r, h, end - lane), w.dtype))
        lane = end
    return jnp.concatenate(pieces, axis=2).reshape(r, h * HEAD_PAD)


def _rope_tables(s):
    half = B_ROPE // 2
    rope_lanes = _head_lanes()[B_NOPE:]
    inv = ROPE_THETA ** (-jnp.arange(0, B_ROPE, 2, dtype=jnp.float32) / B_ROPE)
    ang = jnp.arange(s, dtype=jnp.float32)[:, None] * inv[None, :]
    cos, sin = jnp.cos(ang), jnp.sin(ang)
    assert cos.shape[1] == half
    cos_t = 1.0 + _place_cols((jnp.concatenate([cos, cos], axis=1) - 1.0)[:, None, :], rope_lanes)
    sin_t = _place_cols(jnp.concatenate([-sin, sin], axis=1)[:, None, :], rope_lanes)
    return cos_t, sin_t


def kernel(x, c, w_ada, b_ada, norm1_g, w_in, q_norm_g, w_uq, kv_norm_g, w_ukv, rel_table, a_out_g, b_out_g,
           w_o, norm2_g, w_router, e_bias, w1, w3, w2, ws1, ws3, ws2, final_g):
    b, s, d = x.shape
    n = b * s
    assert w_ada.shape[0] == 1, "single layer"
    x2 = x.reshape(n, d)

    mod = _ada(c, w_ada[0], b_ada[0])
    mod3 = mod.reshape(b * 6, 1, d)
    bias = _bias_tiles(rel_table, _dil_geometry(s))

    wi = w_in[0]
    c_kpe = 3 * A_WIDTH + Q_LORA + KV_LORA
    lanes = _head_lanes()
    kpe_cols = _place_cols(wi[:, None, c_kpe:], lanes[B_NOPE:])
    w_in_ext = jnp.concatenate([wi[:, :c_kpe], kpe_cols], axis=1).astype(BF16)
    w_uq_p = _place_cols(w_uq[0].reshape(Q_LORA, B_HEADS, B_NOPE + B_ROPE), lanes).astype(BF16)
    w_uk_p = _place_cols(w_ukv[0].reshape(KV_LORA, B_HEADS, B_NOPE + B_VDIM)[:, :, :B_NOPE],
                         lanes[:B_NOPE]).astype(BF16)
    w_v = w_ukv[0].reshape(KV_LORA, B_HEADS, B_NOPE + B_VDIM)[:, :, B_NOPE:].reshape(KV_LORA, B_WIDTH).astype(BF16)
    cos_t, sin_t = _rope_tables(s)

    qa, ka, va, qb, kb, vb = _inproj(x2, mod3, norm1_g, w_in_ext, q_norm_g, w_uq_p, kv_norm_g, w_uk_p, w_v,
                                     cos_t, sin_t, b, s)
    a_out = _dilated(qa.reshape(b, s, A_WIDTH), ka.reshape(b, s, A_WIDTH), va.reshape(b, s, A_WIDTH), bias)
    b_out, w1_bf, w3_bf, w2_bf = _mla(qb, kb, vb, w1[0], w3[0], w2[0])

    wr = _slab_order(w_router[0].T)
    wr_hi, wr_lo = _split_bf16(wr)
    w_router_t = jnp.concatenate([wr_hi, wr_hi, wr_lo], axis=1)
    x1s, h2p, scores_t = _outproj(x2, a_out.reshape(n, A_WIDTH), b_out.reshape(n, B_WIDTH), a_out_g, b_out_g,
                                  w_o[0].astype(BF16), mod3, norm2_g, w_router_t,
                                  ws1[0].astype(BF16), ws3[0].astype(BF16), ws2[0].astype(BF16), s)

    eidx_t, gates_t, counts_slab = _route(scores_t, _slab_order(e_bias.reshape(N_EXPERTS, 1)))

    counts = _expert_order(counts_slab)[:, 0]
    padded = (counts + MOE_BLK - 1) // MOE_BLK * MOE_BLK
    pends = jnp.cumsum(padded)
    pstart = pends - padded
    nk = n * TOP_K
    nblk = -(-(nk + N_EXPERTS * (MOE_BLK - 1)) // MOE_BLK)
    rows_total = nblk * MOE_BLK
    blk_row = jnp.arange(nblk, dtype=I32) * MOE_BLK
    blk_e = jnp.minimum(jnp.sum((pends[None, :] <= blk_row[:, None]).astype(I32), axis=1), N_EXPERTS - 1)
    seg_end = (pstart + counts)[blk_e]
    valid = jnp.clip(seg_end - blk_row, 0, MOE_BLK).astype(I32)
    nused = (pends[-1] // MOE_BLK).astype(I32).reshape(1)
    cstart = jnp.concatenate([jnp.zeros((1,), I32), (pends // MOE_BLK).astype(I32)])

    dest_t = _dest(eidx_t, _slab_order(pstart.astype(F32).reshape(N_EXPERTS, 1)))
    nplane = h2p.shape[0]
    gidx = (dest_t.reshape(1, nk) + (jnp.arange(nplane, dtype=I32) * rows_total)[:, None]).reshape(nplane * nk)
    nwin = n // SC_WINDOW
    sidx = dest_t.reshape(TOP_K, nwin, SC_WINDOW).transpose(1, 0, 2)[None] + (
        jnp.arange(nplane, dtype=I32) * rows_total)[:, None, None, None]
    xg = _sc_scatter(h2p.reshape(nplane * n, LANES), sidx.reshape(nplane * nk), nplane * rows_total, nplane, TOP_K)
    y = _experts(cstart, nused, valid, xg.reshape(nplane, rows_total, LANES), w1_bf, w3_bf, w2_bf)
    y_flat = y.reshape(nplane * rows_total, LANES)
    gidx3 = gidx.reshape(nplane, TOP_K, n)
    gates = gates_t.T
    npart = n // COMB_SPLIT
    out = None
    for part in range(COMB_SPLIT):
        pidx = gidx3[:, :, part * npart:(part + 1) * npart].reshape(nplane * TOP_K * npart)
        yg = _sc_gather(y_flat, pidx).reshape(nplane, TOP_K, npart, LANES)
        out = _combine(yg, gates, x1s, mod3, final_g.reshape(1, d), s, part, out)
    return out.reshape(b, s, d)
```

```python
import functools
import math

import jax
import jax.numpy as jnp
from jax import lax
from jax.experimental import pallas as pl
from jax.experimental.pallas import tpu as pltpu
from jax.experimental.pallas import tpu_sc as plsc

F32 = jnp.float32
BF16 = jnp.bfloat16
U32 = jnp.uint32
I32 = jnp.int32
HIGHEST = lax.Precision.HIGHEST

A_HEADS = 8
A_HEAD_DIM = 64
A_WIDTH = A_HEADS * A_HEAD_DIM
A_PATTERNS = ((128, 1), (512, 4), (2048, 16))
A_RADIUS = 64
REL_BUCKETS = 32
REL_MAX_DIST = 1024
B_HEADS = 8
B_NOPE = 64
B_ROPE = 32
B_VDIM = 64
B_WIDTH = B_HEADS * B_VDIM
Q_LORA = 384
KV_LORA = 256
ROPE_THETA = 10000.0
N_EXPERTS = 256
TOP_K = 8
N_GROUPS = 8
GROUP_SIZE = N_EXPERTS // N_GROUPS
TOPK_GROUPS = 4
ROUTED_SCALE = 2.5
EPS = 1e-6
NEG_INF = -1e30
LOG2E = math.log2(math.e)

LANES = 128
SUBLANES = 8
HEAD_PAD = 128

TM_INPROJ = 512
TM_OUTPROJ = 1024
DIL_QB = 128
DIL_KW = DIL_QB + 2 * A_RADIUS
DIL_ITER_ROWS = 4096
MLA_TQ = 1024
MLA_KC = 2048
ROUTE_T = 512
MOE_BLK = 512
EXP_AHEAD = 3
EXP_XBUF = EXP_AHEAD + 1
SC_WINDOW = 128
COMB_T = 256
COMB_SPLIT = 8
VMEM_LIMIT = 56 * 1024 * 1024


def _cparams(sem):
    return pltpu.CompilerParams(dimension_semantics=sem, vmem_limit_bytes=VMEM_LIMIT)


def _rms(x, g):
    return x * lax.rsqrt(jnp.mean(x * x, axis=-1, keepdims=True) + EPS) * g


def _silu(x):
    return x * jax.nn.sigmoid(x)


def _split_bf16(x):
    hi = lax.bitcast_convert_type(lax.bitcast_convert_type(x, U32) & jnp.uint32(0xFFFF0000), F32)
    return hi.astype(BF16), (x - hi).astype(BF16)


def _pack_rows(x):
    half = x.shape[1] // 2
    bits = lax.bitcast_convert_type(x.astype(BF16).astype(F32), U32)
    return (bits[:, :half] >> 16) | bits[:, half:]


def _unpack_rows(w):
    lo = lax.bitcast_convert_type(w << 16, F32)
    hi = lax.bitcast_convert_type(w & jnp.uint32(0xFFFF0000), F32)
    return lo, hi


def _ada_kernel(c_ref, w_ref, b_ref, o_ref):
    o_ref[...] = jnp.dot(_silu(c_ref[...]), w_ref[...], precision=HIGHEST,
                         preferred_element_type=F32) + b_ref[...]


def _ada(c, w_ada, b_ada):
    b, d = c.shape
    n6 = w_ada.shape[1] // d
    return pl.pallas_call(
        _ada_kernel,
        grid=(n6,),
        in_specs=[pl.BlockSpec((b, d), lambda j: (0, 0)),
                  pl.BlockSpec((d, d), lambda j: (0, j)),
                  pl.BlockSpec((1, d), lambda j: (0, j))],
        out_specs=pl.BlockSpec((b, d), lambda j: (0, j)),
        out_shape=jax.ShapeDtypeStruct((b, n6 * d), F32),
        compiler_params=_cparams(("parallel",)),
        name="ada",
    )(c, w_ada, b_ada.reshape(1, -1))


def _t5_bucket(rel):
    half = REL_BUCKETS // 2
    max_exact = half // 2
    ret = jnp.where(rel > 0, half, 0)
    n = jnp.abs(rel)
    nf = jnp.maximum(n, 1).astype(jnp.float32)
    large = max_exact + (jnp.log(nf / max_exact) / math.log(REL_MAX_DIST / max_exact)
                         * (half - max_exact)).astype(jnp.int32)
    large = jnp.minimum(large, half - 1)
    return ret + jnp.where(n < max_exact, n, large)


def _dil_geometry(s):
    geo, base = [], 0
    for _, dil in A_PATTERNS:
        sub_len = s // dil
        qb = sub_len if sub_len <= DIL_KW else DIL_QB
        kw = min(qb + 2 * A_RADIUS, sub_len)
        assert sub_len % qb == 0 and (sub_len == qb or kw == qb + 2 * A_RADIUS)
        shifts = (A_RADIUS,) if sub_len == qb else (A_RADIUS, 0, -A_RADIUS)
        geo.append((dil, qb, kw, shifts, base))
        base += len(shifts)
    return geo


def _bucket_tiles(geo):
    qi = jnp.arange(max(g[1] for g in geo), dtype=jnp.int32)[:, None]
    ki = jnp.arange(max(g[2] for g in geo), dtype=jnp.int32)[None, :]
    tiles = []
    for dilation, qb, kw, shifts, _ in geo:
        for shift in shifts:
            off = ki + shift - A_RADIUS - qi
            bkt = _t5_bucket(off * dilation)
            inside = (jnp.abs(off) <= A_RADIUS) & (qi < qb) & (ki < kw)
            tiles.append(jnp.where(inside, bkt, -1))
    return jnp.stack(tiles, axis=0)


def _bias_kernel(tab_ref, bkt_ref, o_ref):
    bkt = bkt_ref[0]
    for h in range(A_HEADS):
        acc = jnp.full(bkt.shape, NEG_INF, F32)
        for b in range(REL_BUCKETS):
            acc = jnp.where(bkt == b, tab_ref[b, h] * LOG2E, acc)
        o_ref[0, h] = acc


def _bias_tiles(rel_table, geo):
    bkt = _bucket_tiles(geo)
    nt = bkt.shape[0]
    return pl.pallas_call(
        _bias_kernel,
        grid=(nt,),
        in_specs=[pl.BlockSpec(memory_space=pltpu.SMEM),
                  pl.BlockSpec((1,) + bkt.shape[1:], lambda t: (t, 0, 0))],
        out_specs=pl.BlockSpec((1, A_HEADS) + bkt.shape[1:], lambda t: (t, 0, 0, 0)),
        out_shape=jax.ShapeDtypeStruct((nt, A_HEADS) + bkt.shape[1:], F32),
        compiler_params=_cparams(("parallel",)),
        name="bias",
    )(rel_table, bkt)


def _head_lanes():
    half = B_ROPE // 2
    lanes = []
    for j in range(B_NOPE):
        lanes.append(half + j if j < HEAD_PAD // 2 - half else HEAD_PAD // 2 + half + (j - (HEAD_PAD // 2 - half)))
    for r in range(B_ROPE):
        lanes.append(r if r < half else HEAD_PAD // 2 + (r - half))
    return lanes


def _rope(x, cos, sin):
    return x * cos + pltpu.roll(x, HEAD_PAD // 2, 1) * sin


def _inproj_kernel(x_ref, sc_ref, sh_ref, g1_ref, win_ref, qg_ref, wuq_ref, kvg_ref, wuk_ref, wv_ref,
                   cos_ref, sin_ref, qa_ref, ka_ref, va_ref, qb_ref, kb_ref, vb_ref):
    x = x_ref[...]
    h = _rms(x, g1_ref[...]) * (1.0 + sc_ref[...]) + sh_ref[...]
    proj = jnp.dot(h.astype(BF16), win_ref[...], preferred_element_type=F32)
    aw = A_WIDTH
    qa_ref[...] = proj[:, 0:aw] * (LOG2E / math.sqrt(A_HEAD_DIM))
    ka_ref[...] = proj[:, aw:2 * aw]
    va_ref[...] = proj[:, 2 * aw:3 * aw]
    c0 = 3 * aw
    q_lat = proj[:, c0:c0 + Q_LORA]
    kv_lat = proj[:, c0 + Q_LORA:c0 + Q_LORA + KV_LORA]
    kpe = proj[:, c0 + Q_LORA + KV_LORA:]
    qn = _rms(q_lat, qg_ref[...]).astype(BF16)
    kvn = _rms(kv_lat, kvg_ref[...]).astype(BF16)
    qm = jnp.dot(qn, wuq_ref[...], preferred_element_type=F32)
    kn = jnp.dot(kvn, wuk_ref[...], preferred_element_type=F32)
    vv = jnp.dot(kvn, wv_ref[...], preferred_element_type=F32)
    cos = cos_ref[...]
    sin = sin_ref[...]
    qscale = LOG2E / math.sqrt(B_NOPE + B_ROPE)
    cos_q = cos * qscale
    sin_q = sin * qscale
    kpe_rot = _rope(kpe, cos, sin)
    for hd in range(B_HEADS):
        sl = slice(hd * HEAD_PAD, (hd + 1) * HEAD_PAD)
        qb_ref[hd] = _rope(qm[:, sl], cos_q, sin_q).astype(BF16)
        kb_ref[hd] = (kn[:, sl] + kpe_rot).astype(BF16)
    for p in range(B_HEADS // 2):
        vb_ref[p] = vv[:, p * LANES:(p + 1) * LANES].astype(BF16)


def _inproj(x2, mod3, norm1_g, w_in_ext, q_norm_g, w_uq_p, kv_norm_g, w_uk_p, w_v, cos_t, sin_t, b, s):
    n, d = x2.shape
    tm = TM_INPROJ
    tpb = s // tm
    row = lambda i: (i, 0)
    const = lambda i: (0, 0)
    hm = lambda i: (i // tpb, 0, i % tpb, 0)
    return pl.pallas_call(
        _inproj_kernel,
        grid=(n // tm,),
        in_specs=[pl.BlockSpec((tm, d), row),
                  pl.BlockSpec((None, 1, d), lambda i: ((i // tpb) * 6 + 1, 0, 0)),
                  pl.BlockSpec((None, 1, d), lambda i: ((i // tpb) * 6 + 0, 0, 0)),
                  pl.BlockSpec((1, d), const),
                  pl.BlockSpec(w_in_ext.shape, const),
                  pl.BlockSpec((1, Q_LORA), const),
                  pl.BlockSpec(w_uq_p.shape, const),
                  pl.BlockSpec((1, KV_LORA), const),
                  pl.BlockSpec(w_uk_p.shape, const),
                  pl.BlockSpec(w_v.shape, const),
                  pl.BlockSpec((tm, HEAD_PAD), lambda i: (i % tpb, 0)),
                  pl.BlockSpec((tm, HEAD_PAD), lambda i: (i % tpb, 0))],
        out_specs=[pl.BlockSpec((tm, A_WIDTH), row),
                   pl.BlockSpec((tm, A_WIDTH), row),
                   pl.BlockSpec((tm, A_WIDTH), row),
                   pl.BlockSpec((None, B_HEADS, tm, HEAD_PAD), hm),
                   pl.BlockSpec((None, B_HEADS, tm, HEAD_PAD), hm),
                   pl.BlockSpec((None, B_HEADS // 2, tm, LANES), hm)],
        out_shape=[jax.ShapeDtypeStruct((n, A_WIDTH), F32),
                   jax.ShapeDtypeStruct((n, A_WIDTH), F32),
                   jax.ShapeDtypeStruct((n, A_WIDTH), F32),
                   jax.ShapeDtypeStruct((b, B_HEADS, s, HEAD_PAD), BF16),
                   jax.ShapeDtypeStruct((b, B_HEADS, s, HEAD_PAD), BF16),
                   jax.ShapeDtypeStruct((b, B_HEADS // 2, s, LANES), BF16)],
        compiler_params=_cparams(("parallel",)),
        name="inproj",
    )(x2, mod3, mod3, norm1_g, w_in_ext, q_norm_g, w_uq_p, kv_norm_g, w_uk_p, w_v, cos_t, sin_t)


def _dil_block(q_ref, k_ref, v_ref, bias_ref, o_scr, m_scr, l_scr, pi, geo, nblk, job):
    dil, qb, kwin, shifts, tile0 = geo
    sub_len = nblk * qb
    r = job // nblk
    bi = job % nblk
    q0 = bi * qb
    ws = jnp.clip(q0 - A_RADIUS, 0, sub_len - kwin)
    var = jnp.where(bi == 0, 0, jnp.where(bi == nblk - 1, 2, 1)) if len(shifts) > 1 else 0
    if dil == 1:
        qsl = pl.ds(pl.multiple_of(q0, qb), qb)
        ksl = pl.ds(pl.multiple_of(ws, A_RADIUS), kwin)
    else:
        qsl = pl.ds(r + dil * q0, qb, stride=dil)
        ksl = pl.ds(r + dil * ws, kwin, stride=dil)
    q = q_ref[qsl, :]
    kw = k_ref[ksl, :].astype(BF16)
    vw = v_ref[ksl, :].astype(BF16)
    lo = lax.broadcasted_iota(jnp.int32, q.shape, 1) < A_HEAD_DIM
    outs, ms, ls = [], [], []
    for hh in range(2):
        qm = jnp.where(lo if hh == 0 else jnp.logical_not(lo), q, 0.0).astype(BF16)
        sc = lax.dot_general(qm, kw, (((1,), (1,)), ((), ())), preferred_element_type=F32)
        sc = sc + bias_ref[tile0 + var, hh, :qb, :kwin]
        m = jnp.max(sc, axis=-1, keepdims=True)
        p = jnp.exp2(sc - m)
        l = jnp.sum(p, axis=-1, keepdims=True)
        outs.append(jnp.dot(p.astype(BF16), vw, preferred_element_type=F32))
        ms.append(m)
        ls.append(l)
    o_scr[pi, qsl, :] = jnp.where(lo, outs[0], outs[1])
    m_scr[pi, qsl, :] = jnp.where(lo, ms[0], ms[1])
    l_scr[pi, qsl, :] = jnp.where(lo, ls[0], ls[1])


def _dilated_kernel(q_ref, k_ref, v_ref, bias_ref, out_ref, o_scr, m_scr, l_scr):
    s = q_ref.shape[0]
    for pi, geo in enumerate(_dil_geometry(s)):
        dil, qb = geo[0], geo[1]
        njobs = s // qb
        unroll = DIL_ITER_ROWS // qb
        assert njobs % unroll == 0
        blk = functools.partial(_dil_block, q_ref, k_ref, v_ref, bias_ref, o_scr, m_scr, l_scr, pi, geo,
                                s // dil // qb)

        def group(g, c, blk=blk, unroll=unroll):
            for u in range(unroll):
                blk(g * unroll + u)
            return c

        lax.fori_loop(0, njobs // unroll, group, 0)

    chunk = 512

    def comb(i, c):
        rows = pl.ds(pl.multiple_of(i * chunk, chunk), chunk)
        m0, m1, m2 = m_scr[0, rows, :], m_scr[1, rows, :], m_scr[2, rows, :]
        mx = jnp.maximum(jnp.maximum(m0, m1), m2)
        e0, e1, e2 = jnp.exp2(m0 - mx), jnp.exp2(m1 - mx), jnp.exp2(m2 - mx)
        num = e0 * o_scr[0, rows, :] + e1 * o_scr[1, rows, :] + e2 * o_scr[2, rows, :]
        den = e0 * l_scr[0, rows, :] + e1 * l_scr[1, rows, :] + e2 * l_scr[2, rows, :]
        out_ref[rows, :] = num / den
        return c

    lax.fori_loop(0, s // chunk, comb, 0)


def _dilated(qa, ka, va, bias):
    b, s, _ = qa.shape
    npair = A_WIDTH // LANES
    blk = pl.BlockSpec((None, s, LANES), lambda bi, p: (bi, 0, p))
    return pl.pallas_call(
        _dilated_kernel,
        grid=(b, npair),
        in_specs=[blk, blk, blk,
                  pl.BlockSpec((bias.shape[0], 2) + bias.shape[2:], lambda bi, p: (0, p, 0, 0))],
        out_specs=blk,
        out_shape=jax.ShapeDtypeStruct((b, s, A_WIDTH), F32),
        scratch_shapes=[pltpu.VMEM((len(A_PATTERNS), s, LANES), F32)] * 3,
        compiler_params=_cparams(("parallel", "parallel")),
        name="dilated",
    )(qa, ka, va, bias)


def _mla_kernel(q_ref, k_ref, v_ref, w1_ref, w3_ref, w2_ref, o_ref, w1b_ref, w3b_ref, w2b_ref):
    w1b_ref[...] = w1_ref[...].astype(BF16)
    w3b_ref[...] = w3_ref[...].astype(BF16)
    w2b_ref[...] = w2_ref[...].astype(BF16)
    tq = q_ref.shape[1]
    nkc = k_ref.shape[1] // MLA_KC
    outs = []
    for hh in range(2):
        q = q_ref[hh]
        m = jnp.full((tq, 1), -jnp.inf, F32)
        l = jnp.zeros((tq, 1), F32)
        acc = jnp.zeros((tq, LANES), F32)
        for c in range(nkc):
            keys = slice(c * MLA_KC, (c + 1) * MLA_KC)
            sc = lax.dot_general(q, k_ref[hh, keys, :], (((1,), (1,)), ((), ())), preferred_element_type=F32)
            m_new = jnp.maximum(m, jnp.max(sc, axis=-1, keepdims=True))
            alpha = jnp.exp2(m - m_new)
            p = jnp.exp2(sc - m_new)
            l = alpha * l + jnp.sum(p, axis=-1, keepdims=True)
            acc = alpha * acc + jnp.dot(p.astype(BF16), v_ref[keys, :], preferred_element_type=F32)
            m = m_new
        outs.append(acc / l)
    lo = lax.broadcasted_iota(jnp.int32, outs[0].shape, 1) < B_VDIM
    o_ref[...] = jnp.where(lo, outs[0], outs[1])


def _mla(qb, kb, vb, w1, w3, w2):
    b, h, s, _ = qb.shape
    npair = h // 2
    tq = MLA_TQ
    nq = s // tq
    nsteps = b * npair * nq
    ne, d, f = w1.shape
    assert ne % nsteps == 0
    epb = ne // nsteps
    wmap = lambda bi, p, qi: ((bi * npair + p) * nq + qi, 0, 0)
    return pl.pallas_call(
        _mla_kernel,
        grid=(b, npair, nq),
        in_specs=[pl.BlockSpec((None, 2, tq, HEAD_PAD), lambda bi, p, qi: (bi, p, qi, 0)),
                  pl.BlockSpec((None, 2, s, HEAD_PAD), lambda bi, p, qi: (bi, p, 0, 0)),
                  pl.BlockSpec((None, None, s, LANES), lambda bi, p, qi: (bi, p, 0, 0)),
                  pl.BlockSpec((epb, d, f), wmap),
                  pl.BlockSpec((epb, d, f), wmap),
                  pl.BlockSpec((epb, f, d), wmap)],
        out_specs=[pl.BlockSpec((None, tq, LANES), lambda bi, p, qi: (bi, qi, p)),
                   pl.BlockSpec((epb, d, f), wmap),
                   pl.BlockSpec((epb, d, f), wmap),
                   pl.BlockSpec((epb, f, d), wmap)],
        out_shape=[jax.ShapeDtypeStruct((b, s, B_WIDTH), F32),
                   jax.ShapeDtypeStruct((ne, d, f), BF16),
                   jax.ShapeDtypeStruct((ne, d, f), BF16),
                   jax.ShapeDtypeStruct((ne, f, d), BF16)],
        compiler_params=_cparams(("parallel", "parallel", "arbitrary")),
        name="mla",
    )(qb, kb, vb, w1, w3, w2)


def _outproj_kernel(x_ref, a_ref, b_ref, ag_ref, bg_ref, wo_ref, g1_ref, n2_ref, sc_ref, sh_ref, g2_ref,
                    wrt_ref, ws1_ref, ws3_ref, ws2_ref, x1s_ref, h2p_ref, scores_ref):
    an = _rms(a_ref[...], ag_ref[...])
    bn = _rms(b_ref[...], bg_ref[...])
    mix = jnp.concatenate([an, bn], axis=-1).astype(BF16)
    x1 = x_ref[...] + g1_ref[...] * jnp.dot(mix, wo_ref[...], preferred_element_type=F32)
    h2 = _rms(x1, n2_ref[...]) * (1.0 + sc_ref[...]) + sh_ref[...]
    packed = _pack_rows(h2)
    for cg in range(h2p_ref.shape[0]):
        h2p_ref[cg] = packed[:, cg * LANES:(cg + 1) * LANES]
    h_hi, h_lo = _split_bf16(h2)
    logits = lax.dot_general(wrt_ref[...], jnp.concatenate([h_hi, h_lo, h_hi], axis=1),
                             (((1,), (1,)), ((), ())), preferred_element_type=F32)
    scores_ref[...] = jax.nn.sigmoid(logits)
    h2b = h2.astype(BF16)
    hid = _silu(jnp.dot(h2b, ws1_ref[...], preferred_element_type=F32)) * jnp.dot(
        h2b, ws3_ref[...], preferred_element_type=F32)
    shared = jnp.dot(hid.astype(BF16), ws2_ref[...], preferred_element_type=F32)
    x1s_ref[...] = x1 + g2_ref[...] * shared


def _outproj(x2, a_out, b_out, a_out_g, b_out_g, w_o, mod3, norm2_g, w_router_t, ws1, ws3, ws2, s):
    n, d = x2.shape
    tm = TM_OUTPROJ
    tpb = s // tm
    row = lambda i: (i, 0)
    const = lambda i: (0, 0)
    modspec = lambda j: pl.BlockSpec((None, 1, d), lambda i: ((i // tpb) * 6 + j, 0, 0))
    return pl.pallas_call(
        _outproj_kernel,
        grid=(n // tm,),
        in_specs=[pl.BlockSpec((tm, d), row),
                  pl.BlockSpec((tm, A_WIDTH), row),
                  pl.BlockSpec((tm, B_WIDTH), row),
                  pl.BlockSpec((1, A_WIDTH), const),
                  pl.BlockSpec((1, B_WIDTH), const),
                  pl.BlockSpec(w_o.shape, const),
                  modspec(2),
                  pl.BlockSpec((1, d), const),
                  modspec(4), modspec(3), modspec(5),
                  pl.BlockSpec(w_router_t.shape, const),
                  pl.BlockSpec(ws1.shape, const),
                  pl.BlockSpec(ws3.shape, const),
                  pl.BlockSpec(ws2.shape, const)],
        out_specs=[pl.BlockSpec((tm, d), row),
                   pl.BlockSpec((d // 2 // LANES, tm, LANES), lambda i: (0, i, 0)),
                   pl.BlockSpec((N_EXPERTS, tm), lambda i: (0, i))],
        out_shape=[jax.ShapeDtypeStruct((n, d), F32),
                   jax.ShapeDtypeStruct((d // 2 // LANES, n, LANES), U32),
                   jax.ShapeDtypeStruct((N_EXPERTS, n), F32)],
        compiler_params=_cparams(("parallel",)),
        name="outproj",
    )(x2, a_out, b_out, a_out_g, b_out_g, w_o, mod3, norm2_g, mod3, mod3, mod3, w_router_t, ws1, ws3, ws2)


def _slab_order(v):
    return v.reshape((N_GROUPS, GROUP_SIZE) + v.shape[1:]).swapaxes(0, 1).reshape(v.shape)


def _expert_order(v):
    return v.reshape((GROUP_SIZE, N_GROUPS) + v.shape[1:]).swapaxes(0, 1).reshape(v.shape)


def _sublane_all(x, op):
    for sh in (4, 2, 1):
        x = op(x, pltpu.roll(x, sh, 0))
    return x


def _route_kernel(st_ref, bias_ref, eidx_ref, gate_ref, cnt_ref):
    nsl = GROUP_SIZE
    t = st_ref.shape[1]
    sub = lax.broadcasted_iota(I32, (SUBLANES, t), 0)
    ninf = -jnp.inf
    big = jnp.int32(1 << 30)
    sc = [st_ref[j * SUBLANES:(j + 1) * SUBLANES, :] for j in range(nsl)]
    sel = [sc[j] + bias_ref[j * SUBLANES:(j + 1) * SUBLANES, :] for j in range(nsl)]
    eid = [sub * GROUP_SIZE + j for j in range(nsl)]

    m1 = sel[0]
    m2 = jnp.full_like(m1, ninf)
    for j in range(1, nsl):
        m2 = jnp.maximum(m2, jnp.minimum(m1, sel[j]))
        m1 = jnp.maximum(m1, sel[j])
    gs = m1 + m2

    rank = jnp.zeros((SUBLANES, t), I32)
    for sh in range(1, N_GROUPS):
        other = pltpu.roll(gs, sh, 0)
        ahead = (other > gs) | ((other == gs) & (sub >= sh))
        rank = rank + ahead.astype(I32)
    gmask = rank < TOPK_GROUPS

    msel = [jnp.where(gmask, sel[j], ninf) for j in range(nsl)]
    hits = [jnp.zeros((SUBLANES, t), I32) for _ in range(nsl)]
    eidx = jnp.zeros((TOP_K, t), I32)
    gates = jnp.zeros((TOP_K, t), F32)
    for k in range(TOP_K):
        mx = msel[0]
        for j in range(1, nsl):
            mx = jnp.maximum(mx, msel[j])
        mx = _sublane_all(mx, jnp.maximum)
        cand = jnp.where(msel[0] == mx, eid[0], big)
        for j in range(1, nsl):
            cand = jnp.minimum(cand, jnp.where(msel[j] == mx, eid[j], big))
        idx = _sublane_all(cand, jnp.minimum)
        gk = jnp.zeros((SUBLANES, t), F32)
        for j in range(nsl):
            hit = eid[j] == idx
            gk = gk + jnp.where(hit, sc[j], 0.0)
            msel[j] = jnp.where(hit, ninf, msel[j])
            hits[j] = hits[j] + hit.astype(I32)
        gk = _sublane_all(gk, jnp.add)
        eidx = jnp.where(sub == k, idx, eidx)
        gates = jnp.where(sub == k, gk, gates)
    gsum = _sublane_all(gates, jnp.add)
    eidx_ref[...] = eidx
    gate_ref[...] = gates / gsum * ROUTED_SCALE

    @pl.when(pl.program_id(0) == 0)
    def _():
        cnt_ref[...] = jnp.zeros_like(cnt_ref)

    for j in range(nsl):
        cnt_ref[j * SUBLANES:(j + 1) * SUBLANES, :] += jnp.sum(hits[j].astype(F32), axis=1,
                                                               keepdims=True).astype(I32)


def _route(scores_t, e_bias_slab):
    e, n = scores_t.shape
    t = ROUTE_T
    return pl.pallas_call(
        _route_kernel,
        grid=(n // t,),
        in_specs=[pl.BlockSpec((e, t), lambda i: (0, i)),
                  pl.BlockSpec((e, 1), lambda i: (0, 0))],
        out_specs=[pl.BlockSpec((TOP_K, t), lambda i: (0, i)),
                   pl.BlockSpec((TOP_K, t), lambda i: (0, i)),
                   pl.BlockSpec((e, 1), lambda i: (0, 0))],
        out_shape=[jax.ShapeDtypeStruct((TOP_K, n), I32),
                   jax.ShapeDtypeStruct((TOP_K, n), F32),
                   jax.ShapeDtypeStruct((e, 1), I32)],
        compiler_params=_cparams(("arbitrary",)),
        name="route",
    )(scores_t, e_bias_slab)


def _dest_kernel(eidx_ref, pstart_ref, dest_ref, carry_ref):
    @pl.when(pl.program_id(0) == 0)
    def _():
        carry_ref[...] = jnp.zeros_like(carry_ref)

    nsl = GROUP_SIZE
    t = eidx_ref.shape[1]
    sub = lax.broadcasted_iota(I32, (SUBLANES, t), 0)
    eid = [sub * GROUP_SIZE + j for j in range(nsl)]
    ek = [eidx_ref[k:k + 1, :] for k in range(TOP_K)]
    slabs = []
    for j in range(nsl):
        oh = jnp.zeros((SUBLANES, t), F32)
        for k in range(TOP_K):
            oh = oh + (eid[j] == ek[k]).astype(F32)
        slabs.append(oh)
    onehot = jnp.concatenate(slabs, axis=0)
    row = lax.broadcasted_iota(I32, (t, t), 0)
    col = lax.broadcasted_iota(I32, (t, t), 1)
    upper = (row < col).astype(BF16)
    before = jnp.dot(onehot.astype(BF16), upper, preferred_element_type=F32)
    base = before + carry_ref[...] + pstart_ref[...]
    dest = jnp.zeros((TOP_K, t), I32)
    for k in range(TOP_K):
        acc = jnp.zeros((SUBLANES, t), F32)
        for j in range(nsl):
            acc = acc + jnp.where(eid[j] == ek[k], base[j * SUBLANES:(j + 1) * SUBLANES, :], 0.0)
        dk = _sublane_all(acc, jnp.add).astype(I32)
        dest = jnp.where(sub == k, dk, dest)
    dest_ref[...] = dest
    carry_ref[...] += jnp.sum(onehot, axis=1, keepdims=True)


def _dest(eidx_t, pstart_slab):
    n = eidx_t.shape[1]
    t = ROUTE_T
    return pl.pallas_call(
        _dest_kernel,
        grid=(n // t,),
        in_specs=[pl.BlockSpec((TOP_K, t), lambda i: (0, i)),
                  pl.BlockSpec((N_EXPERTS, 1), lambda i: (0, 0))],
        out_specs=pl.BlockSpec((TOP_K, t), lambda i: (0, i)),
        out_shape=jax.ShapeDtypeStruct((TOP_K, n), I32),
        scratch_shapes=[pltpu.VMEM((N_EXPERTS, 1), F32)],
        compiler_params=_cparams(("arbitrary",)),
        name="dest",
    )(eidx_t, pstart_slab)


def _sc_scatter(x, idx, rows_out, nslot):
    num = idx.shape[0]
    mesh = plsc.VectorSubcoreMesh(core_axis_name="core", subcore_axis_name="subcore")

    @pl.kernel(out_type=jax.ShapeDtypeStruct((rows_out, x.shape[1]), x.dtype), mesh=mesh, scratch_types=[])
    def scatter(x_hbm, i_hbm, o_hbm):
        def body(x_vmem, i_vmem):
            pltpu.sync_copy(x_vmem, o_hbm.at[i_vmem.at[0]])

        pltpu.emit_pipeline(
            body,
            grid=(num // SC_WINDOW,),
            in_specs=[pl.BlockSpec((SC_WINDOW, x.shape[1]), index_map=lambda i: (i // nslot, 0)),
                      pl.BlockSpec((1, SC_WINDOW), index_map=lambda i: (0, i))],
            out_specs=[],
            core_axis_name=("core", "subcore"),
            dimension_semantics=(pltpu.PARALLEL,),
        )(x_hbm, i_hbm)

    return scatter(x, idx.reshape(1, num))


def _expert_kernel(cstart_ref, nused_ref, valid_ref, w1_ref, w3_ref, w2_ref, xg_ref, y_ref,
                   xbuf, ybuf, xsem, ysem):
    e = pl.program_id(0)
    c0 = cstart_ref[e]
    c1 = cstart_ref[e + 1]
    nused = nused_ref[0]
    nchunks = xg_ref.shape[1] // MOE_BLK

    def rows(g):
        start = g * MOE_BLK
        return pl.ds(start if isinstance(g, int) else pl.multiple_of(start, MOE_BLK), MOE_BLK)

    nplane = xg_ref.shape[0]

    def xcopies(g):
        slot = g % EXP_XBUF
        return [pltpu.make_async_copy(xg_ref.at[pl.ds(0, nplane), rows(g)], xbuf.at[slot], xsem.at[slot])]

    def ycopies(g, slot):
        return [pltpu.make_async_copy(ybuf.at[slot], y_ref.at[pl.ds(0, nplane), rows(g)], ysem.at[slot])]

    @pl.when(e == 0)
    def _():
        ybuf[...] = jnp.zeros(ybuf.shape, ybuf.dtype)
        for j in range(EXP_AHEAD):
            @pl.when(j < nused)
            def _():
                for cp in xcopies(j):
                    cp.start()

    @pl.when(c1 > c0)
    def _():
        def ffn(g, yslot, nrows):
            row_id = lax.broadcasted_iota(I32, (nrows, LANES), 0)
            keep = row_id < valid_ref[g]
            halves = [_unpack_rows(jnp.where(keep, xbuf[g % EXP_XBUF, c, :nrows], jnp.uint32(0)))
                      for c in range(nplane)]
            xb = jnp.concatenate([h[0] for h in halves] + [h[1] for h in halves], axis=1).astype(BF16)
            hid = _silu(jnp.dot(xb, w1_ref[...], preferred_element_type=F32)) * jnp.dot(
                xb, w3_ref[...], preferred_element_type=F32)
            yp = _pack_rows(jnp.dot(hid.astype(BF16), w2_ref[...], preferred_element_type=F32))
            for c in range(nplane):
                ybuf[yslot, c, :nrows] = yp[:, c * LANES:(c + 1) * LANES]

        def chunk(g, carry):
            @pl.when(g + EXP_AHEAD < nused)
            def _():
                for cp in xcopies(g + EXP_AHEAD):
                    cp.start()

            for cp in xcopies(g):
                cp.wait()
            yslot = g % 2

            @pl.when(g >= 2)
            def _():
                for cp in ycopies(g - 2, yslot):
                    cp.wait()

            half = MOE_BLK // 2

            @pl.when(valid_ref[g] > half)
            def _():
                ffn(g, yslot, MOE_BLK)

            @pl.when(valid_ref[g] <= half)
            def _():
                ffn(g, yslot, half)

            for cp in ycopies(g, yslot):
                cp.start()
            return carry

        lax.fori_loop(c0, c1, chunk, 0)

    @pl.when(e == pl.num_programs(0) - 1)
    def _():
        for back in (2, 1):
            g = nused - back

            @pl.when(g >= 0)
            def _():
                for cp in ycopies(g, g % 2):
                    cp.wait()

        ybuf[0] = jnp.zeros(ybuf.shape[1:], ybuf.dtype)

        def zstart(g, c):
            for cp in ycopies(g, 0):
                cp.start()
            return c

        def zwait(g, c):
            for cp in ycopies(g, 0):
                cp.wait()
            return c

        lax.fori_loop(nused, nchunks, zstart, 0)
        lax.fori_loop(nused, nchunks, zwait, 0)


def _experts(cstart, nused, valid, xg, w1, w3, w2):
    nplane, rows, _ = xg.shape
    dh = nplane * LANES
    ne, d, f = w1.shape
    grid_spec = pltpu.PrefetchScalarGridSpec(
        num_scalar_prefetch=3,
        grid=(ne,),
        in_specs=[pl.BlockSpec((None, d, f), lambda e, cs, nu, va: (e, 0, 0)),
                  pl.BlockSpec((None, d, f), lambda e, cs, nu, va: (e, 0, 0)),
                  pl.BlockSpec((None, f, d), lambda e, cs, nu, va: (e, 0, 0)),
                  pl.BlockSpec(memory_space=pl.ANY)],
        out_specs=pl.BlockSpec(memory_space=pl.ANY),
        scratch_shapes=[pltpu.VMEM((EXP_XBUF, nplane, MOE_BLK, LANES), U32),
                        pltpu.VMEM((2, nplane, MOE_BLK, LANES), U32),
                        pltpu.SemaphoreType.DMA((EXP_XBUF,)), pltpu.SemaphoreType.DMA((2,))],
    )
    return pl.pallas_call(
        _expert_kernel,
        grid_spec=grid_spec,
        out_shape=jax.ShapeDtypeStruct((dh // LANES, rows, LANES), U32),
        compiler_params=_cparams(("arbitrary",)),
        name="experts",
    )(cstart, nused, valid, w1, w3, w2, xg)


def _sc_gather(x, idx):
    num = idx.shape[0]
    mesh = plsc.VectorSubcoreMesh(core_axis_name="core", subcore_axis_name="subcore")

    @pl.kernel(out_type=jax.ShapeDtypeStruct((num, x.shape[1]), x.dtype), mesh=mesh)
    def gather(x_hbm, i_hbm, o_hbm):
        def body(i_vmem, o_vmem):
            pltpu.sync_copy(x_hbm.at[i_vmem.at[0]], o_vmem)

        pltpu.emit_pipeline(
            body,
            grid=(num // SC_WINDOW,),
            in_specs=[pl.BlockSpec((1, SC_WINDOW), index_map=lambda i: (0, i))],
            out_specs=[pl.BlockSpec((SC_WINDOW, x.shape[1]), index_map=lambda i: (i, 0))],
            core_axis_name=("core", "subcore"),
            dimension_semantics=(pltpu.PARALLEL,),
        )(i_hbm, o_hbm)

    return gather(x, idx.reshape(1, num))


def _combine_kernel(yg_ref, gate_ref, x1s_ref, g2_ref, fg_ref, *rest):
    out_ref = rest[-1]
    gates = gate_ref[...]
    nch = yg_ref.shape[0]
    r_lo = [None] * nch
    r_hi = [None] * nch
    for k in range(TOP_K):
        gk = gates[:, k:k + 1]
        for c in range(nch):
            lo, hi = _unpack_rows(yg_ref[c, k])
            r_lo[c] = gk * lo if k == 0 else r_lo[c] + gk * lo
            r_hi[c] = gk * hi if k == 0 else r_hi[c] + gk * hi
    routed = jnp.concatenate(r_lo + r_hi, axis=1)
    x2 = x1s_ref[...] + g2_ref[...] * routed
    out_ref[...] = _rms(x2, fg_ref[...])


def _combine(yg, gates, x1s, mod3, final_g, s, part, prev_out):
    n, d = x1s.shape
    t = COMB_T
    tpb = s // t
    nch, _, npart, _ = yg.shape
    off = part * (npart // t)
    in_specs = [pl.BlockSpec((nch, TOP_K, t, LANES), lambda i: (0, 0, i, 0)),
                pl.BlockSpec((t, TOP_K), lambda i: (i + off, 0)),
                pl.BlockSpec((t, d), lambda i: (i + off, 0)),
                pl.BlockSpec((None, 1, d), lambda i: (((i + off) // tpb) * 6 + 5, 0, 0)),
                pl.BlockSpec((1, d), lambda i: (0, 0))]
    args = [yg, gates, x1s, mod3, final_g]
    aliases = {}
    if prev_out is not None:
        in_specs.append(pl.BlockSpec(memory_space=pl.ANY))
        args.append(prev_out)
        aliases = {len(args) - 1: 0}
    return pl.pallas_call(
        _combine_kernel,
        grid=(npart // t,),
        in_specs=in_specs,
        out_specs=pl.BlockSpec((t, d), lambda i: (i + off, 0)),
        out_shape=jax.ShapeDtypeStruct((n, d), F32),
        input_output_aliases=aliases,
        compiler_params=_cparams(("parallel",)),
        name="combine",
    )(*args)


def _place_cols(w, lanes):
    r, h, _ = w.shape
    src = {lane: j for j, lane in enumerate(lanes)}
    pieces, lane = [], 0
    while lane < HEAD_PAD:
        end = lane + 1
        if lane in src:
            while end < HEAD_PAD and end in src and src[end] == src[end - 1] + 1:
                end += 1
            pieces.append(w[:, :, src[lane]:src[lane] + end - lane])
        else:
            while end < HEAD_PAD and end not in src:
                end += 1
            pieces.append(jnp.zeros((r, h, end - lane), w.dtype))
        lane = end
    return jnp.concatenate(pieces, axis=2).reshape(r, h * HEAD_PAD)


def _rope_tables(s):
    half = B_ROPE // 2
    rope_lanes = _head_lanes()[B_NOPE:]
    inv = ROPE_THETA ** (-jnp.arange(0, B_ROPE, 2, dtype=jnp.float32) / B_ROPE)
    ang = jnp.arange(s, dtype=jnp.float32)[:, None] * inv[None, :]
    cos, sin = jnp.cos(ang), jnp.sin(ang)
    assert cos.shape[1] == half
    cos_t = 1.0 + _place_cols((jnp.concatenate([cos, cos], axis=1) - 1.0)[:, None, :], rope_lanes)
    sin_t = _place_cols(jnp.concatenate([-sin, sin], axis=1)[:, None, :], rope_lanes)
    return cos_t, sin_t


def kernel(x, c, w_ada, b_ada, norm1_g, w_in, q_norm_g, w_uq, kv_norm_g, w_ukv, rel_table, a_out_g, b_out_g,
           w_o, norm2_g, w_router, e_bias, w1, w3, w2, ws1, ws3, ws2, final_g):
    b, s, d = x.shape
    n = b * s
    assert w_ada.shape[0] == 1, "single layer"
    x2 = x.reshape(n, d)

    mod = _ada(c, w_ada[0], b_ada[0])
    mod3 = mod.reshape(b * 6, 1, d)
    bias = _bias_tiles(rel_table, _dil_geometry(s))

    wi = w_in[0]
    c_kpe = 3 * A_WIDTH + Q_LORA + KV_LORA
    lanes = _head_lanes()
    kpe_cols = _place_cols(wi[:, None, c_kpe:], lanes[B_NOPE:])
    w_in_ext = jnp.concatenate([wi[:, :c_kpe], kpe_cols], axis=1).astype(BF16)
    w_uq_p = _place_cols(w_uq[0].reshape(Q_LORA, B_HEADS, B_NOPE + B_ROPE), lanes).astype(BF16)
    w_uk_p = _place_cols(w_ukv[0].reshape(KV_LORA, B_HEADS, B_NOPE + B_VDIM)[:, :, :B_NOPE],
                         lanes[:B_NOPE]).astype(BF16)
    w_v = w_ukv[0].reshape(KV_LORA, B_HEADS, B_NOPE + B_VDIM)[:, :, B_NOPE:].reshape(KV_LORA, B_WIDTH).astype(BF16)
    cos_t, sin_t = _rope_tables(s)

    qa, ka, va, qb, kb, vb = _inproj(x2, mod3, norm1_g, w_in_ext, q_norm_g, w_uq_p, kv_norm_g, w_uk_p, w_v,
                                     cos_t, sin_t, b, s)
    a_out = _dilated(qa.reshape(b, s, A_WIDTH), ka.reshape(b, s, A_WIDTH), va.reshape(b, s, A_WIDTH), bias)
    b_out, w1_bf, w3_bf, w2_bf = _mla(qb, kb, vb, w1[0], w3[0], w2[0])

    wr = _slab_order(w_router[0].T)
    wr_hi, wr_lo = _split_bf16(wr)
    w_router_t = jnp.concatenate([wr_hi, wr_hi, wr_lo], axis=1)
    x1s, h2p, scores_t = _outproj(x2, a_out.reshape(n, A_WIDTH), b_out.reshape(n, B_WIDTH), a_out_g, b_out_g,
                                  w_o[0].astype(BF16), mod3, norm2_g, w_router_t,
                                  ws1[0].astype(BF16), ws3[0].astype(BF16), ws2[0].astype(BF16), s)

    eidx_t, gates_t, counts_slab = _route(scores_t, _slab_order(e_bias.reshape(N_EXPERTS, 1)))

    counts = _expert_order(counts_slab)[:, 0]
    padded = (counts + MOE_BLK - 1) // MOE_BLK * MOE_BLK
    pends = jnp.cumsum(padded)
    pstart = pends - padded
    nk = n * TOP_K
    nblk = -(-(nk + N_EXPERTS * (MOE_BLK - 1)) // MOE_BLK)
    rows_total = nblk * MOE_BLK
    blk_row = jnp.arange(nblk, dtype=I32) * MOE_BLK
    blk_e = jnp.minimum(jnp.sum((pends[None, :] <= blk_row[:, None]).astype(I32), axis=1), N_EXPERTS - 1)
    seg_end = (pstart + counts)[blk_e]
    valid = jnp.clip(seg_end - blk_row, 0, MOE_BLK).astype(I32)
    nused = (pends[-1] // MOE_BLK).astype(I32).reshape(1)
    cstart = jnp.concatenate([jnp.zeros((1,), I32), (pends // MOE_BLK).astype(I32)])

    dest_t = _dest(eidx_t, _slab_order(pstart.astype(F32).reshape(N_EXPERTS, 1)))
    nplane = h2p.shape[0]
    plane_off = (jnp.arange(nplane, dtype=I32) * rows_total)[:, None, None, None]
    nwin = n // SC_WINDOW
    sidx = dest_t.reshape(TOP_K, nwin, SC_WINDOW).transpose(1, 0, 2)[None] + plane_off
    xg = _sc_scatter(h2p.reshape(nplane * n, LANES), sidx.reshape(nplane * nk), nplane * rows_total, TOP_K)
    y = _experts(cstart, nused, valid, xg.reshape(nplane, rows_total, LANES), w1_bf, w3_bf, w2_bf)
    y_flat = y.reshape(nplane * rows_total, LANES)
    npart = n // COMB_SPLIT
    gidx = (dest_t.reshape(1, TOP_K, COMB_SPLIT, npart) + plane_off).transpose(2, 0, 1, 3)
    gidx = gidx.reshape(COMB_SPLIT, nplane * TOP_K * npart)
    gates = gates_t.T
    out = None
    for part in range(COMB_SPLIT):
        yg = _sc_gather(y_flat, gidx[part]).reshape(nplane, TOP_K, npart, LANES)
        out = _combine(yg, gates, x1s, mod3, final_g.reshape(1, d), s, part, out)
    return out.reshape(b, s, d)
```

```python
import functools
import math

import jax
import jax.numpy as jnp
from jax import lax
from jax.experimental import pallas as pl
from jax.experimental.pallas import tpu as pltpu
from jax.experimental.pallas import tpu_sc as plsc

F32 = jnp.float32
BF16 = jnp.bfloat16
U32 = jnp.uint32
I32 = jnp.int32
HIGHEST = lax.Precision.HIGHEST

A_HEADS = 8
A_HEAD_DIM = 64
A_WIDTH = A_HEADS * A_HEAD_DIM
A_PATTERNS = ((128, 1), (512, 4), (2048, 16))
A_RADIUS = 64
REL_BUCKETS = 32
REL_MAX_DIST = 1024
B_HEADS = 8
B_NOPE = 64
B_ROPE = 32
B_VDIM = 64
B_WIDTH = B_HEADS * B_VDIM
Q_LORA = 384
KV_LORA = 256
ROPE_THETA = 10000.0
N_EXPERTS = 256
TOP_K = 8
N_GROUPS = 8
GROUP_SIZE = N_EXPERTS // N_GROUPS
TOPK_GROUPS = 4
ROUTED_SCALE = 2.5
EPS = 1e-6
NEG_INF = -1e30
LOG2E = math.log2(math.e)

LANES = 128
SUBLANES = 8
HEAD_PAD = 128

TM_INPROJ = 512
TM_OUTPROJ = 1024
DIL_QB = 128
DIL_KW = DIL_QB + 2 * A_RADIUS
DIL_ITER_ROWS = 4096
MLA_TQ = 1024
MLA_KC = 2048
ROUTE_T = 512
MOE_BLK = 512
EXP_AHEAD = 3
EXP_XBUF = EXP_AHEAD + 1
SC_WINDOW = 128
COMB_T = 256
COMB_SPLIT = 8
VMEM_LIMIT = 56 * 1024 * 1024


def _cparams(sem):
    return pltpu.CompilerParams(dimension_semantics=sem, vmem_limit_bytes=VMEM_LIMIT)


def _rms(x, g):
    return x * lax.rsqrt(jnp.mean(x * x, axis=-1, keepdims=True) + EPS) * g


def _silu(x):
    return x * jax.nn.sigmoid(x)


def _split_bf16(x):
    hi = lax.bitcast_convert_type(lax.bitcast_convert_type(x, U32) & jnp.uint32(0xFFFF0000), F32)
    return hi.astype(BF16), (x - hi).astype(BF16)


def _pack_rows(x):
    half = x.shape[1] // 2
    bits = lax.bitcast_convert_type(x.astype(BF16).astype(F32), U32)
    return (bits[:, :half] >> 16) | bits[:, half:]


def _unpack_rows(w):
    lo = lax.bitcast_convert_type(w << 16, F32)
    hi = lax.bitcast_convert_type(w & jnp.uint32(0xFFFF0000), F32)
    return lo, hi


def _ada_kernel(c_ref, w_ref, b_ref, o_ref):
    o_ref[...] = jnp.dot(_silu(c_ref[...]), w_ref[...], precision=HIGHEST,
                         preferred_element_type=F32) + b_ref[...]


def _ada(c, w_ada, b_ada):
    b, d = c.shape
    n6 = w_ada.shape[1] // d
    return pl.pallas_call(
        _ada_kernel,
        grid=(n6,),
        in_specs=[pl.BlockSpec((b, d), lambda j: (0, 0)),
                  pl.BlockSpec((d, d), lambda j: (0, j)),
                  pl.BlockSpec((1, d), lambda j: (0, j))],
        out_specs=pl.BlockSpec((b, d), lambda j: (0, j)),
        out_shape=jax.ShapeDtypeStruct((b, n6 * d), F32),
        compiler_params=_cparams(("parallel",)),
        name="ada",
    )(c, w_ada, b_ada.reshape(1, -1))


def _t5_bucket(rel):
    half = REL_BUCKETS // 2
    max_exact = half // 2
    ret = jnp.where(rel > 0, half, 0)
    n = jnp.abs(rel)
    nf = jnp.maximum(n, 1).astype(jnp.float32)
    large = max_exact + (jnp.log(nf / max_exact) / math.log(REL_MAX_DIST / max_exact)
                         * (half - max_exact)).astype(jnp.int32)
    large = jnp.minimum(large, half - 1)
    return ret + jnp.where(n < max_exact, n, large)


def _dil_geometry(s):
    geo, base = [], 0
    for _, dil in A_PATTERNS:
        sub_len = s // dil
        qb = sub_len if sub_len <= DIL_KW else DIL_QB
        kw = min(qb + 2 * A_RADIUS, sub_len)
        assert sub_len % qb == 0 and (sub_len == qb or kw == qb + 2 * A_RADIUS)
        shifts = (A_RADIUS,) if sub_len == qb else (A_RADIUS, 0, -A_RADIUS)
        geo.append((dil, qb, kw, shifts, base))
        base += len(shifts)
    return geo


def _bucket_tiles(geo):
    qi = jnp.arange(max(g[1] for g in geo), dtype=jnp.int32)[:, None]
    ki = jnp.arange(max(g[2] for g in geo), dtype=jnp.int32)[None, :]
    tiles = []
    for dilation, qb, kw, shifts, _ in geo:
        for shift in shifts:
            off = ki + shift - A_RADIUS - qi
            bkt = _t5_bucket(off * dilation)
            inside = (jnp.abs(off) <= A_RADIUS) & (qi < qb) & (ki < kw)
            tiles.append(jnp.where(inside, bkt, -1))
    return jnp.stack(tiles, axis=0)


def _bias_kernel(tab_ref, bkt_ref, o_ref):
    bkt = bkt_ref[0]
    for h in range(A_HEADS):
        acc = jnp.full(bkt.shape, NEG_INF, F32)
        for b in range(REL_BUCKETS):
            acc = jnp.where(bkt == b, tab_ref[b, h] * LOG2E, acc)
        o_ref[0, h] = acc


def _bias_tiles(rel_table, geo):
    bkt = _bucket_tiles(geo)
    nt = bkt.shape[0]
    return pl.pallas_call(
        _bias_kernel,
        grid=(nt,),
        in_specs=[pl.BlockSpec(memory_space=pltpu.SMEM),
                  pl.BlockSpec((1,) + bkt.shape[1:], lambda t: (t, 0, 0))],
        out_specs=pl.BlockSpec((1, A_HEADS) + bkt.shape[1:], lambda t: (t, 0, 0, 0)),
        out_shape=jax.ShapeDtypeStruct((nt, A_HEADS) + bkt.shape[1:], F32),
        compiler_params=_cparams(("parallel",)),
        name="bias",
    )(rel_table, bkt)


def _head_lanes():
    half = B_ROPE // 2
    lanes = []
    for j in range(B_NOPE):
        lanes.append(half + j if j < HEAD_PAD // 2 - half else HEAD_PAD // 2 + half + (j - (HEAD_PAD // 2 - half)))
    for r in range(B_ROPE):
        lanes.append(r if r < half else HEAD_PAD // 2 + (r - half))
    return lanes


def _rope(x, cos, sin):
    return x * cos + pltpu.roll(x, HEAD_PAD // 2, 1) * sin


def _inproj_kernel(x_ref, sc_ref, sh_ref, g1_ref, win_ref, qg_ref, wuq_ref, kvg_ref, wuk_ref, wv_ref,
                   cos_ref, sin_ref, qa_ref, ka_ref, va_ref, qb_ref, kb_ref, vb_ref):
    x = x_ref[...]
    h = _rms(x, g1_ref[...]) * (1.0 + sc_ref[...]) + sh_ref[...]
    proj = jnp.dot(h.astype(BF16), win_ref[...], preferred_element_type=F32)
    aw = A_WIDTH
    qa_ref[...] = proj[:, 0:aw] * (LOG2E / math.sqrt(A_HEAD_DIM))
    ka_ref[...] = proj[:, aw:2 * aw]
    va_ref[...] = proj[:, 2 * aw:3 * aw]
    c0 = 3 * aw
    q_lat = proj[:, c0:c0 + Q_LORA]
    kv_lat = proj[:, c0 + Q_LORA:c0 + Q_LORA + KV_LORA]
    kpe = proj[:, c0 + Q_LORA + KV_LORA:]
    qn = _rms(q_lat, qg_ref[...]).astype(BF16)
    kvn = _rms(kv_lat, kvg_ref[...]).astype(BF16)
    qm = jnp.dot(qn, wuq_ref[...], preferred_element_type=F32)
    kn = jnp.dot(kvn, wuk_ref[...], preferred_element_type=F32)
    vv = jnp.dot(kvn, wv_ref[...], preferred_element_type=F32)
    cos = cos_ref[...]
    sin = sin_ref[...]
    qscale = LOG2E / math.sqrt(B_NOPE + B_ROPE)
    cos_q = cos * qscale
    sin_q = sin * qscale
    kpe_rot = _rope(kpe, cos, sin)
    for hd in range(B_HEADS):
        sl = slice(hd * HEAD_PAD, (hd + 1) * HEAD_PAD)
        qb_ref[hd] = _rope(qm[:, sl], cos_q, sin_q).astype(BF16)
        kb_ref[hd] = (kn[:, sl] + kpe_rot).astype(BF16)
    for p in range(B_HEADS // 2):
        vb_ref[p] = vv[:, p * LANES:(p + 1) * LANES].astype(BF16)


def _inproj(x2, mod3, norm1_g, w_in_ext, q_norm_g, w_uq_p, kv_norm_g, w_uk_p, w_v, cos_t, sin_t, b, s):
    n, d = x2.shape
    tm = TM_INPROJ
    tpb = s // tm
    row = lambda i: (i, 0)
    const = lambda i: (0, 0)
    hm = lambda i: (i // tpb, 0, i % tpb, 0)
    return pl.pallas_call(
        _inproj_kernel,
        grid=(n // tm,),
        in_specs=[pl.BlockSpec((tm, d), row),
                  pl.BlockSpec((None, 1, d), lambda i: ((i // tpb) * 6 + 1, 0, 0)),
                  pl.BlockSpec((None, 1, d), lambda i: ((i // tpb) * 6 + 0, 0, 0)),
                  pl.BlockSpec((1, d), const),
                  pl.BlockSpec(w_in_ext.shape, const),
                  pl.BlockSpec((1, Q_LORA), const),
                  pl.BlockSpec(w_uq_p.shape, const),
                  pl.BlockSpec((1, KV_LORA), const),
                  pl.BlockSpec(w_uk_p.shape, const),
                  pl.BlockSpec(w_v.shape, const),
                  pl.BlockSpec((tm, HEAD_PAD), lambda i: (i % tpb, 0)),
                  pl.BlockSpec((tm, HEAD_PAD), lambda i: (i % tpb, 0))],
        out_specs=[pl.BlockSpec((tm, A_WIDTH), row),
                   pl.BlockSpec((tm, A_WIDTH), row),
                   pl.BlockSpec((tm, A_WIDTH), row),
                   pl.BlockSpec((None, B_HEADS, tm, HEAD_PAD), hm),
                   pl.BlockSpec((None, B_HEADS, tm, HEAD_PAD), hm),
                   pl.BlockSpec((None, B_HEADS // 2, tm, LANES), hm)],
        out_shape=[jax.ShapeDtypeStruct((n, A_WIDTH), F32),
                   jax.ShapeDtypeStruct((n, A_WIDTH), F32),
                   jax.ShapeDtypeStruct((n, A_WIDTH), F32),
                   jax.ShapeDtypeStruct((b, B_HEADS, s, HEAD_PAD), BF16),
                   jax.ShapeDtypeStruct((b, B_HEADS, s, HEAD_PAD), BF16),
                   jax.ShapeDtypeStruct((b, B_HEADS // 2, s, LANES), BF16)],
        compiler_params=_cparams(("parallel",)),
        name="inproj",
    )(x2, mod3, mod3, norm1_g, w_in_ext, q_norm_g, w_uq_p, kv_norm_g, w_uk_p, w_v, cos_t, sin_t)


def _dil_block(q_ref, k_ref, v_ref, bias_ref, o_scr, m_scr, l_scr, pi, geo, nblk, job):
    dil, qb, kwin, shifts, tile0 = geo
    sub_len = nblk * qb
    r = job // nblk
    bi = job % nblk
    q0 = bi * qb
    ws = jnp.clip(q0 - A_RADIUS, 0, sub_len - kwin)
    var = jnp.where(bi == 0, 0, jnp.where(bi == nblk - 1, 2, 1)) if len(shifts) > 1 else 0
    if dil == 1:
        qsl = pl.ds(pl.multiple_of(q0, qb), qb)
        ksl = pl.ds(pl.multiple_of(ws, A_RADIUS), kwin)
    else:
        qsl = pl.ds(r + dil * q0, qb, stride=dil)
        ksl = pl.ds(r + dil * ws, kwin, stride=dil)
    q = q_ref[qsl, :]
    kw = k_ref[ksl, :].astype(BF16)
    vw = v_ref[ksl, :].astype(BF16)
    lo = lax.broadcasted_iota(jnp.int32, q.shape, 1) < A_HEAD_DIM
    outs, ms, ls = [], [], []
    for hh in range(2):
        qm = jnp.where(lo if hh == 0 else jnp.logical_not(lo), q, 0.0).astype(BF16)
        sc = lax.dot_general(qm, kw, (((1,), (1,)), ((), ())), preferred_element_type=F32)
        sc = sc + bias_ref[tile0 + var, hh, :qb, :kwin]
        m = jnp.max(sc, axis=-1, keepdims=True)
        p = jnp.exp2(sc - m)
        l = jnp.sum(p, axis=-1, keepdims=True)
        outs.append(jnp.dot(p.astype(BF16), vw, preferred_element_type=F32))
        ms.append(m)
        ls.append(l)
    o_scr[pi, qsl, :] = jnp.where(lo, outs[0], outs[1])
    m_scr[pi, qsl, :] = jnp.where(lo, ms[0], ms[1])
    l_scr[pi, qsl, :] = jnp.where(lo, ls[0], ls[1])


def _dilated_kernel(q_ref, k_ref, v_ref, bias_ref, out_ref, o_scr, m_scr, l_scr):
    s = q_ref.shape[0]
    for pi, geo in enumerate(_dil_geometry(s)):
        dil, qb = geo[0], geo[1]
        njobs = s // qb
        unroll = DIL_ITER_ROWS // qb
        assert njobs % unroll == 0
        blk = functools.partial(_dil_block, q_ref, k_ref, v_ref, bias_ref, o_scr, m_scr, l_scr, pi, geo,
                                s // dil // qb)

        def group(g, c, blk=blk, unroll=unroll):
            for u in range(unroll):
                blk(g * unroll + u)
            return c

        lax.fori_loop(0, njobs // unroll, group, 0)

    chunk = 512

    def comb(i, c):
        rows = pl.ds(pl.multiple_of(i * chunk, chunk), chunk)
        m0, m1, m2 = m_scr[0, rows, :], m_scr[1, rows, :], m_scr[2, rows, :]
        mx = jnp.maximum(jnp.maximum(m0, m1), m2)
        e0, e1, e2 = jnp.exp2(m0 - mx), jnp.exp2(m1 - mx), jnp.exp2(m2 - mx)
        num = e0 * o_scr[0, rows, :] + e1 * o_scr[1, rows, :] + e2 * o_scr[2, rows, :]
        den = e0 * l_scr[0, rows, :] + e1 * l_scr[1, rows, :] + e2 * l_scr[2, rows, :]
        out_ref[rows, :] = num / den
        return c

    lax.fori_loop(0, s // chunk, comb, 0)


def _dilated(qa, ka, va, bias):
    b, s, _ = qa.shape
    npair = A_WIDTH // LANES
    blk = pl.BlockSpec((None, s, LANES), lambda bi, p: (bi, 0, p))
    return pl.pallas_call(
        _dilated_kernel,
        grid=(b, npair),
        in_specs=[blk, blk, blk,
                  pl.BlockSpec((bias.shape[0], 2) + bias.shape[2:], lambda bi, p: (0, p, 0, 0))],
        out_specs=blk,
        out_shape=jax.ShapeDtypeStruct((b, s, A_WIDTH), F32),
        scratch_shapes=[pltpu.VMEM((len(A_PATTERNS), s, LANES), F32)] * 3,
        compiler_params=_cparams(("parallel", "parallel")),
        name="dilated",
    )(qa, ka, va, bias)


def _mla_kernel(q_ref, k_ref, v_ref, w1_ref, w3_ref, w2_ref, o_ref, w1b_ref, w3b_ref, w2b_ref):
    w1b_ref[...] = w1_ref[...].astype(BF16)
    w3b_ref[...] = w3_ref[...].astype(BF16)
    w2b_ref[...] = w2_ref[...].astype(BF16)
    tq = q_ref.shape[1]
    nkc = k_ref.shape[1] // MLA_KC
    outs = []
    for hh in range(2):
        q = q_ref[hh]
        m = jnp.full((tq, 1), -jnp.inf, F32)
        l = jnp.zeros((tq, 1), F32)
        acc = jnp.zeros((tq, LANES), F32)
        for c in range(nkc):
            keys = slice(c * MLA_KC, (c + 1) * MLA_KC)
            sc = lax.dot_general(q, k_ref[hh, keys, :], (((1,), (1,)), ((), ())), preferred_element_type=F32)
            m_new = jnp.maximum(m, jnp.max(sc, axis=-1, keepdims=True))
            alpha = jnp.exp2(m - m_new)
            p = jnp.exp2(sc - m_new)
            l = alpha * l + jnp.sum(p, axis=-1, keepdims=True)
            acc = alpha * acc + jnp.dot(p.astype(BF16), v_ref[keys, :], preferred_element_type=F32)
            m = m_new
        outs.append(acc / l)
    lo = lax.broadcasted_iota(jnp.int32, outs[0].shape, 1) < B_VDIM
    o_ref[...] = jnp.where(lo, outs[0], outs[1])


def _mla(qb, kb, vb, w1, w3, w2):
    b, h, s, _ = qb.shape
    npair = h // 2
    tq = MLA_TQ
    nq = s // tq
    nsteps = b * npair * nq
    ne, d, f = w1.shape
    assert ne % nsteps == 0
    epb = ne // nsteps
    wmap = lambda bi, p, qi: ((bi * npair + p) * nq + qi, 0, 0)
    return pl.pallas_call(
        _mla_kernel,
        grid=(b, npair, nq),
        in_specs=[pl.BlockSpec((None, 2, tq, HEAD_PAD), lambda bi, p, qi: (bi, p, qi, 0)),
                  pl.BlockSpec((None, 2, s, HEAD_PAD), lambda bi, p, qi: (bi, p, 0, 0)),
                  pl.BlockSpec((None, None, s, LANES), lambda bi, p, qi: (bi, p, 0, 0)),
                  pl.BlockSpec((epb, d, f), wmap),
                  pl.BlockSpec((epb, d, f), wmap),
                  pl.BlockSpec((epb, f, d), wmap)],
        out_specs=[pl.BlockSpec((None, tq, LANES), lambda bi, p, qi: (bi, qi, p)),
                   pl.BlockSpec((epb, d, f), wmap),
                   pl.BlockSpec((epb, d, f), wmap),
                   pl.BlockSpec((epb, f, d), wmap)],
        out_shape=[jax.ShapeDtypeStruct((b, s, B_WIDTH), F32),
                   jax.ShapeDtypeStruct((ne, d, f), BF16),
                   jax.ShapeDtypeStruct((ne, d, f), BF16),
                   jax.ShapeDtypeStruct((ne, f, d), BF16)],
        compiler_params=_cparams(("parallel", "parallel", "arbitrary")),
        name="mla",
    )(qb, kb, vb, w1, w3, w2)


def _outproj_kernel(x_ref, a_ref, b_ref, ag_ref, bg_ref, wo_ref, g1_ref, n2_ref, sc_ref, sh_ref,
                    wrt_ref, x1_ref, h2p_ref, scores_ref):
    an = _rms(a_ref[...], ag_ref[...])
    bn = _rms(b_ref[...], bg_ref[...])
    mix = jnp.concatenate([an, bn], axis=-1).astype(BF16)
    x1 = x_ref[...] + g1_ref[...] * jnp.dot(mix, wo_ref[...], preferred_element_type=F32)
    x1_ref[...] = x1
    h2 = _rms(x1, n2_ref[...]) * (1.0 + sc_ref[...]) + sh_ref[...]
    packed = _pack_rows(h2)
    for cg in range(h2p_ref.shape[0]):
        h2p_ref[cg] = packed[:, cg * LANES:(cg + 1) * LANES]
    h_hi, h_lo = _split_bf16(h2)
    logits = lax.dot_general(wrt_ref[...], jnp.concatenate([h_hi, h_lo, h_hi], axis=1),
                             (((1,), (1,)), ((), ())), preferred_element_type=F32)
    scores_ref[...] = jax.nn.sigmoid(logits)


def _outproj(x2, a_out, b_out, a_out_g, b_out_g, w_o, mod3, norm2_g, w_router_t, s):
    n, d = x2.shape
    tm = TM_OUTPROJ
    tpb = s // tm
    row = lambda i: (i, 0)
    const = lambda i: (0, 0)
    modspec = lambda j: pl.BlockSpec((None, 1, d), lambda i: ((i // tpb) * 6 + j, 0, 0))
    return pl.pallas_call(
        _outproj_kernel,
        grid=(n // tm,),
        in_specs=[pl.BlockSpec((tm, d), row),
                  pl.BlockSpec((tm, A_WIDTH), row),
                  pl.BlockSpec((tm, B_WIDTH), row),
                  pl.BlockSpec((1, A_WIDTH), const),
                  pl.BlockSpec((1, B_WIDTH), const),
                  pl.BlockSpec(w_o.shape, const),
                  modspec(2),
                  pl.BlockSpec((1, d), const),
                  modspec(4), modspec(3),
                  pl.BlockSpec(w_router_t.shape, const)],
        out_specs=[pl.BlockSpec((tm, d), row),
                   pl.BlockSpec((d // 2 // LANES, tm, LANES), lambda i: (0, i, 0)),
                   pl.BlockSpec((N_EXPERTS, tm), lambda i: (0, i))],
        out_shape=[jax.ShapeDtypeStruct((n, d), F32),
                   jax.ShapeDtypeStruct((d // 2 // LANES, n, LANES), U32),
                   jax.ShapeDtypeStruct((N_EXPERTS, n), F32)],
        compiler_params=_cparams(("parallel",)),
        name="outproj",
    )(x2, a_out, b_out, a_out_g, b_out_g, w_o, mod3, norm2_g, mod3, mod3, w_router_t)


def _shared_kernel(x1_ref, h2p_ref, g2_ref, ws1_ref, ws3_ref, ws2_ref, x1s_ref):
    halves = [_unpack_rows(h2p_ref[c]) for c in range(h2p_ref.shape[0])]
    h2b = jnp.concatenate([h[0] for h in halves] + [h[1] for h in halves], axis=1).astype(BF16)
    hid = _silu(jnp.dot(h2b, ws1_ref[...], preferred_element_type=F32)) * jnp.dot(
        h2b, ws3_ref[...], preferred_element_type=F32)
    shared = jnp.dot(hid.astype(BF16), ws2_ref[...], preferred_element_type=F32)
    x1s_ref[...] = x1_ref[...] + g2_ref[...] * shared


def _shared(x1, h2p, mod3, ws1, ws3, ws2, s):
    n, d = x1.shape
    tm = TM_OUTPROJ
    tpb = s // tm
    const = lambda i: (0, 0)
    return pl.pallas_call(
        _shared_kernel,
        grid=(n // tm,),
        in_specs=[pl.BlockSpec((tm, d), lambda i: (i, 0)),
                  pl.BlockSpec((h2p.shape[0], tm, LANES), lambda i: (0, i, 0)),
                  pl.BlockSpec((None, 1, d), lambda i: ((i // tpb) * 6 + 5, 0, 0)),
                  pl.BlockSpec(ws1.shape, const),
                  pl.BlockSpec(ws3.shape, const),
                  pl.BlockSpec(ws2.shape, const)],
        out_specs=pl.BlockSpec((tm, d), lambda i: (i, 0)),
        out_shape=jax.ShapeDtypeStruct((n, d), F32),
        input_output_aliases={0: 0},
        compiler_params=_cparams(("parallel",)),
        name="shared",
    )(x1, h2p, mod3, ws1, ws3, ws2)


def _slab_order(v):
    return v.reshape((N_GROUPS, GROUP_SIZE) + v.shape[1:]).swapaxes(0, 1).reshape(v.shape)


def _expert_order(v):
    return v.reshape((GROUP_SIZE, N_GROUPS) + v.shape[1:]).swapaxes(0, 1).reshape(v.shape)


def _sublane_all(x, op):
    for sh in (4, 2, 1):
        x = op(x, pltpu.roll(x, sh, 0))
    return x


def _route_kernel(st_ref, bias_ref, eidx_ref, gate_ref, cnt_ref):
    nsl = GROUP_SIZE
    t = st_ref.shape[1]
    sub = lax.broadcasted_iota(I32, (SUBLANES, t), 0)
    ninf = -jnp.inf
    big = jnp.int32(1 << 30)
    sc = [st_ref[j * SUBLANES:(j + 1) * SUBLANES, :] for j in range(nsl)]
    sel = [sc[j] + bias_ref[j * SUBLANES:(j + 1) * SUBLANES, :] for j in range(nsl)]
    eid = [sub * GROUP_SIZE + j for j in range(nsl)]

    m1 = sel[0]
    m2 = jnp.full_like(m1, ninf)
    for j in range(1, nsl):
        m2 = jnp.maximum(m2, jnp.minimum(m1, sel[j]))
        m1 = jnp.maximum(m1, sel[j])
    gs = m1 + m2

    rank = jnp.zeros((SUBLANES, t), I32)
    for sh in range(1, N_GROUPS):
        other = pltpu.roll(gs, sh, 0)
        ahead = (other > gs) | ((other == gs) & (sub >= sh))
        rank = rank + ahead.astype(I32)
    gmask = rank < TOPK_GROUPS

    msel = [jnp.where(gmask, sel[j], ninf) for j in range(nsl)]
    hits = [jnp.zeros((SUBLANES, t), I32) for _ in range(nsl)]
    eidx = jnp.zeros((TOP_K, t), I32)
    gates = jnp.zeros((TOP_K, t), F32)
    for k in range(TOP_K):
        mx = msel[0]
        for j in range(1, nsl):
            mx = jnp.maximum(mx, msel[j])
        mx = _sublane_all(mx, jnp.maximum)
        cand = jnp.where(msel[0] == mx, eid[0], big)
        for j in range(1, nsl):
            cand = jnp.minimum(cand, jnp.where(msel[j] == mx, eid[j], big))
        idx = _sublane_all(cand, jnp.minimum)
        gk = jnp.zeros((SUBLANES, t), F32)
        for j in range(nsl):
            hit = eid[j] == idx
            gk = gk + jnp.where(hit, sc[j], 0.0)
            msel[j] = jnp.where(hit, ninf, msel[j])
            hits[j] = hits[j] + hit.astype(I32)
        gk = _sublane_all(gk, jnp.add)
        eidx = jnp.where(sub == k, idx, eidx)
        gates = jnp.where(sub == k, gk, gates)
    gsum = _sublane_all(gates, jnp.add)
    eidx_ref[...] = eidx
    gate_ref[...] = gates / gsum * ROUTED_SCALE

    @pl.when(pl.program_id(0) == 0)
    def _():
        cnt_ref[...] = jnp.zeros_like(cnt_ref)

    for j in range(nsl):
        cnt_ref[j * SUBLANES:(j + 1) * SUBLANES, :] += jnp.sum(hits[j].astype(F32), axis=1,
                                                               keepdims=True).astype(I32)


def _route(scores_t, e_bias_slab):
    e, n = scores_t.shape
    t = ROUTE_T
    return pl.pallas_call(
        _route_kernel,
        grid=(n // t,),
        in_specs=[pl.BlockSpec((e, t), lambda i: (0, i)),
                  pl.BlockSpec((e, 1), lambda i: (0, 0))],
        out_specs=[pl.BlockSpec((TOP_K, t), lambda i: (0, i)),
                   pl.BlockSpec((TOP_K, t), lambda i: (0, i)),
                   pl.BlockSpec((e, 1), lambda i: (0, 0))],
        out_shape=[jax.ShapeDtypeStruct((TOP_K, n), I32),
                   jax.ShapeDtypeStruct((TOP_K, n), F32),
                   jax.ShapeDtypeStruct((e, 1), I32)],
        compiler_params=_cparams(("arbitrary",)),
        name="route",
    )(scores_t, e_bias_slab)


def _dest_kernel(eidx_ref, pstart_ref, dest_ref, carry_ref):
    @pl.when(pl.program_id(0) == 0)
    def _():
        carry_ref[...] = jnp.zeros_like(carry_ref)

    nsl = GROUP_SIZE
    t = eidx_ref.shape[1]
    sub = lax.broadcasted_iota(I32, (SUBLANES, t), 0)
    eid = [sub * GROUP_SIZE + j for j in range(nsl)]
    ek = [eidx_ref[k:k + 1, :] for k in range(TOP_K)]
    slabs = []
    for j in range(nsl):
        oh = jnp.zeros((SUBLANES, t), F32)
        for k in range(TOP_K):
            oh = oh + (eid[j] == ek[k]).astype(F32)
        slabs.append(oh)
    onehot = jnp.concatenate(slabs, axis=0)
    row = lax.broadcasted_iota(I32, (t, t), 0)
    col = lax.broadcasted_iota(I32, (t, t), 1)
    upper = (row < col).astype(BF16)
    before = jnp.dot(onehot.astype(BF16), upper, preferred_element_type=F32)
    base = before + carry_ref[...] + pstart_ref[...]
    dest = jnp.zeros((TOP_K, t), I32)
    for k in range(TOP_K):
        acc = jnp.zeros((SUBLANES, t), F32)
        for j in range(nsl):
            acc = acc + jnp.where(eid[j] == ek[k], base[j * SUBLANES:(j + 1) * SUBLANES, :], 0.0)
        dk = _sublane_all(acc, jnp.add).astype(I32)
        dest = jnp.where(sub == k, dk, dest)
    dest_ref[...] = dest
    carry_ref[...] += jnp.sum(onehot, axis=1, keepdims=True)


def _dest(eidx_t, pstart_slab):
    n = eidx_t.shape[1]
    t = ROUTE_T
    return pl.pallas_call(
        _dest_kernel,
        grid=(n // t,),
        in_specs=[pl.BlockSpec((TOP_K, t), lambda i: (0, i)),
                  pl.BlockSpec((N_EXPERTS, 1), lambda i: (0, 0))],
        out_specs=pl.BlockSpec((TOP_K, t), lambda i: (0, i)),
        out_shape=jax.ShapeDtypeStruct((TOP_K, n), I32),
        scratch_shapes=[pltpu.VMEM((N_EXPERTS, 1), F32)],
        compiler_params=_cparams(("arbitrary",)),
        name="dest",
    )(eidx_t, pstart_slab)


def _sc_scatter(x, idx, rows_out, nslot):
    num = idx.shape[0]
    mesh = plsc.VectorSubcoreMesh(core_axis_name="core", subcore_axis_name="subcore")

    @pl.kernel(out_type=jax.ShapeDtypeStruct((rows_out, x.shape[1]), x.dtype), mesh=mesh, scratch_types=[])
    def scatter(x_hbm, i_hbm, o_hbm):
        def body(x_vmem, i_vmem):
            pltpu.sync_copy(x_vmem, o_hbm.at[i_vmem.at[0]])

        pltpu.emit_pipeline(
            body,
            grid=(num // SC_WINDOW,),
            in_specs=[pl.BlockSpec((SC_WINDOW, x.shape[1]), index_map=lambda i: (i // nslot, 0)),
                      pl.BlockSpec((1, SC_WINDOW), index_map=lambda i: (0, i))],
            out_specs=[],
            core_axis_name=("core", "subcore"),
            dimension_semantics=(pltpu.PARALLEL,),
        )(x_hbm, i_hbm)

    return scatter(x, idx.reshape(1, num))


def _expert_kernel(cstart_ref, nused_ref, valid_ref, w1_ref, w3_ref, w2_ref, xg_ref, y_ref,
                   xbuf, ybuf, xsem, ysem):
    e = pl.program_id(0)
    c0 = cstart_ref[e]
    c1 = cstart_ref[e + 1]
    nused = nused_ref[0]
    nchunks = xg_ref.shape[1] // MOE_BLK

    def rows(g):
        start = g * MOE_BLK
        return pl.ds(start if isinstance(g, int) else pl.multiple_of(start, MOE_BLK), MOE_BLK)

    nplane = xg_ref.shape[0]

    def xcopies(g):
        slot = g % EXP_XBUF
        return [pltpu.make_async_copy(xg_ref.at[pl.ds(0, nplane), rows(g)], xbuf.at[slot], xsem.at[slot])]

    def ycopies(g, slot):
        return [pltpu.make_async_copy(ybuf.at[slot], y_ref.at[pl.ds(0, nplane), rows(g)], ysem.at[slot])]

    @pl.when(e == 0)
    def _():
        ybuf[...] = jnp.zeros(ybuf.shape, ybuf.dtype)
        for j in range(EXP_AHEAD):
            @pl.when(j < nused)
            def _():
                for cp in xcopies(j):
                    cp.start()

    @pl.when(c1 > c0)
    def _():
        def ffn(g, yslot, nrows):
            row_id = lax.broadcasted_iota(I32, (nrows, LANES), 0)
            keep = row_id < valid_ref[g]
            halves = [_unpack_rows(jnp.where(keep, xbuf[g % EXP_XBUF, c, :nrows], jnp.uint32(0)))
                      for c in range(nplane)]
            xb = jnp.concatenate([h[0] for h in halves] + [h[1] for h in halves], axis=1).astype(BF16)
            hid = _silu(jnp.dot(xb, w1_ref[...], preferred_element_type=F32)) * jnp.dot(
                xb, w3_ref[...], preferred_element_type=F32)
            yp = _pack_rows(jnp.dot(hid.astype(BF16), w2_ref[...], preferred_element_type=F32))
            for c in range(nplane):
                ybuf[yslot, c, :nrows] = yp[:, c * LANES:(c + 1) * LANES]

        def chunk(g, carry):
            @pl.when(g + EXP_AHEAD < nused)
            def _():
                for cp in xcopies(g + EXP_AHEAD):
                    cp.start()

            for cp in xcopies(g):
                cp.wait()
            yslot = g % 2

            @pl.when(g >= 2)
            def _():
                for cp in ycopies(g - 2, yslot):
                    cp.wait()

            half = MOE_BLK // 2

            @pl.when(valid_ref[g] > half)
            def _():
                ffn(g, yslot, MOE_BLK)

            @pl.when(valid_ref[g] <= half)
            def _():
                ffn(g, yslot, half)

            for cp in ycopies(g, yslot):
                cp.start()
            return carry

        lax.fori_loop(c0, c1, chunk, 0)

    @pl.when(e == pl.num_programs(0) - 1)
    def _():
        for back in (2, 1):
            g = nused - back

            @pl.when(g >= 0)
            def _():
                for cp in ycopies(g, g % 2):
                    cp.wait()

        ybuf[0] = jnp.zeros(ybuf.shape[1:], ybuf.dtype)

        def zstart(g, c):
            for cp in ycopies(g, 0):
                cp.start()
            return c

        def zwait(g, c):
            for cp in ycopies(g, 0):
                cp.wait()
            return c

        lax.fori_loop(nused, nchunks, zstart, 0)
        lax.fori_loop(nused, nchunks, zwait, 0)


def _experts(cstart, nused, valid, xg, w1, w3, w2):
    nplane, rows, _ = xg.shape
    dh = nplane * LANES
    ne, d, f = w1.shape
    grid_spec = pltpu.PrefetchScalarGridSpec(
        num_scalar_prefetch=3,
        grid=(ne,),
        in_specs=[pl.BlockSpec((None, d, f), lambda e, cs, nu, va: (e, 0, 0)),
                  pl.BlockSpec((None, d, f), lambda e, cs, nu, va: (e, 0, 0)),
                  pl.BlockSpec((None, f, d), lambda e, cs, nu, va: (e, 0, 0)),
                  pl.BlockSpec(memory_space=pl.ANY)],
        out_specs=pl.BlockSpec(memory_space=pl.ANY),
        scratch_shapes=[pltpu.VMEM((EXP_XBUF, nplane, MOE_BLK, LANES), U32),
                        pltpu.VMEM((2, nplane, MOE_BLK, LANES), U32),
                        pltpu.SemaphoreType.DMA((EXP_XBUF,)), pltpu.SemaphoreType.DMA((2,))],
    )
    return pl.pallas_call(
        _expert_kernel,
        grid_spec=grid_spec,
        out_shape=jax.ShapeDtypeStruct((dh // LANES, rows, LANES), U32),
        compiler_params=_cparams(("arbitrary",)),
        name="experts",
    )(cstart, nused, valid, w1, w3, w2, xg)


def _sc_gather(x, idx):
    num = idx.shape[0]
    mesh = plsc.VectorSubcoreMesh(core_axis_name="core", subcore_axis_name="subcore")

    @pl.kernel(out_type=jax.ShapeDtypeStruct((num, x.shape[1]), x.dtype), mesh=mesh)
    def gather(x_hbm, i_hbm, o_hbm):
        def body(i_vmem, o_vmem):
            pltpu.sync_copy(x_hbm.at[i_vmem.at[0]], o_vmem)

        pltpu.emit_pipeline(
            body,
            grid=(num // SC_WINDOW,),
            in_specs=[pl.BlockSpec((1, SC_WINDOW), index_map=lambda i: (0, i))],
            out_specs=[pl.BlockSpec((SC_WINDOW, x.shape[1]), index_map=lambda i: (i, 0))],
            core_axis_name=("core", "subcore"),
            dimension_semantics=(pltpu.PARALLEL,),
        )(i_hbm, o_hbm)

    return gather(x, idx.reshape(1, num))


def _combine_kernel(yg_ref, gate_ref, x1s_ref, g2_ref, fg_ref, *rest):
    out_ref = rest[-1]
    gates = gate_ref[...]
    nch = yg_ref.shape[0]
    r_lo = [None] * nch
    r_hi = [None] * nch
    for k in range(TOP_K):
        gk = gates[:, k:k + 1]
        for c in range(nch):
            lo, hi = _unpack_rows(yg_ref[c, k])
            r_lo[c] = gk * lo if k == 0 else r_lo[c] + gk * lo
            r_hi[c] = gk * hi if k == 0 else r_hi[c] + gk * hi
    routed = jnp.concatenate(r_lo + r_hi, axis=1)
    x2 = x1s_ref[...] + g2_ref[...] * routed
    out_ref[...] = _rms(x2, fg_ref[...])


def _combine(yg, gates, x1s, mod3, final_g, s, part, prev_out):
    n, d = x1s.shape
    t = COMB_T
    tpb = s // t
    nch, _, npart, _ = yg.shape
    off = part * (npart // t)
    in_specs = [pl.BlockSpec((nch, TOP_K, t, LANES), lambda i: (0, 0, i, 0)),
                pl.BlockSpec((t, TOP_K), lambda i: (i + off, 0)),
                pl.BlockSpec((t, d), lambda i: (i + off, 0)),
                pl.BlockSpec((None, 1, d), lambda i: (((i + off) // tpb) * 6 + 5, 0, 0)),
                pl.BlockSpec((1, d), lambda i: (0, 0))]
    args = [yg, gates, x1s, mod3, final_g]
    aliases = {}
    if prev_out is not None:
        in_specs.append(pl.BlockSpec(memory_space=pl.ANY))
        args.append(prev_out)
        aliases = {len(args) - 1: 0}
    return pl.pallas_call(
        _combine_kernel,
        grid=(npart // t,),
        in_specs=in_specs,
        out_specs=pl.BlockSpec((t, d), lambda i: (i + off, 0)),
        out_shape=jax.ShapeDtypeStruct((n, d), F32),
        input_output_aliases=aliases,
        compiler_params=_cparams(("parallel",)),
        name="combine",
    )(*args)


def _place_cols(w, lanes):
    r, h, _ = w.shape
    src = {lane: j for j, lane in enumerate(lanes)}
    pieces, lane = [], 0
    while lane < HEAD_PAD:
        end = lane + 1
        if lane in src:
            while end < HEAD_PAD and end in src and src[end] == src[end - 1] + 1:
                end += 1
            pieces.append(w[:, :, src[lane]:src[lane] + end - lane])
        else:
            while end < HEAD_PAD and end not in src:
                end += 1
            pieces.append(jnp.zeros((r, h, end - lane), w.dtype))
        lane = end
    return jnp.concatenate(pieces, axis=2).reshape(r, h * HEAD_PAD)


def _rope_tables(s):
    half = B_ROPE // 2
    rope_lanes = _head_lanes()[B_NOPE:]
    inv = ROPE_THETA ** (-jnp.arange(0, B_ROPE, 2, dtype=jnp.float32) / B_ROPE)
    ang = jnp.arange(s, dtype=jnp.float32)[:, None] * inv[None, :]
    cos, sin = jnp.cos(ang), jnp.sin(ang)
    assert cos.shape[1] == half
    cos_t = 1.0 + _place_cols((jnp.concatenate([cos, cos], axis=1) - 1.0)[:, None, :], rope_lanes)
    sin_t = _place_cols(jnp.concatenate([-sin, sin], axis=1)[:, None, :], rope_lanes)
    return cos_t, sin_t


def kernel(x, c, w_ada, b_ada, norm1_g, w_in, q_norm_g, w_uq, kv_norm_g, w_ukv, rel_table, a_out_g, b_out_g,
           w_o, norm2_g, w_router, e_bias, w1, w3, w2, ws1, ws3, ws2, final_g):
    b, s, d = x.shape
    n = b * s
    assert w_ada.shape[0] == 1, "single layer"
    x2 = x.reshape(n, d)

    mod = _ada(c, w_ada[0], b_ada[0])
    mod3 = mod.reshape(b * 6, 1, d)
    bias = _bias_tiles(rel_table, _dil_geometry(s))

    wi = w_in[0]
    c_kpe = 3 * A_WIDTH + Q_LORA + KV_LORA
    lanes = _head_lanes()
    kpe_cols = _place_cols(wi[:, None, c_kpe:], lanes[B_NOPE:])
    w_in_ext = jnp.concatenate([wi[:, :c_kpe], kpe_cols], axis=1).astype(BF16)
    w_uq_p = _place_cols(w_uq[0].reshape(Q_LORA, B_HEADS, B_NOPE + B_ROPE), lanes).astype(BF16)
    w_uk_p = _place_cols(w_ukv[0].reshape(KV_LORA, B_HEADS, B_NOPE + B_VDIM)[:, :, :B_NOPE],
                         lanes[:B_NOPE]).astype(BF16)
    w_v = w_ukv[0].reshape(KV_LORA, B_HEADS, B_NOPE + B_VDIM)[:, :, B_NOPE:].reshape(KV_LORA, B_WIDTH).astype(BF16)
    cos_t, sin_t = _rope_tables(s)

    qa, ka, va, qb, kb, vb = _inproj(x2, mod3, norm1_g, w_in_ext, q_norm_g, w_uq_p, kv_norm_g, w_uk_p, w_v,
                                     cos_t, sin_t, b, s)
    a_out = _dilated(qa.reshape(b, s, A_WIDTH), ka.reshape(b, s, A_WIDTH), va.reshape(b, s, A_WIDTH), bias)
    b_out, w1_bf, w3_bf, w2_bf = _mla(qb, kb, vb, w1[0], w3[0], w2[0])

    wr = _slab_order(w_router[0].T)
    wr_hi, wr_lo = _split_bf16(wr)
    w_router_t = jnp.concatenate([wr_hi, wr_hi, wr_lo], axis=1)
    x1, h2p, scores_t = _outproj(x2, a_out.reshape(n, A_WIDTH), b_out.reshape(n, B_WIDTH), a_out_g, b_out_g,
                                 w_o[0].astype(BF16), mod3, norm2_g, w_router_t, s)

    eidx_t, gates_t, counts_slab = _route(scores_t, _slab_order(e_bias.reshape(N_EXPERTS, 1)))

    counts = _expert_order(counts_slab)[:, 0]
    padded = (counts + MOE_BLK - 1) // MOE_BLK * MOE_BLK
    pends = jnp.cumsum(padded)
    pstart = pends - padded
    nk = n * TOP_K
    nblk = -(-(nk + N_EXPERTS * (MOE_BLK - 1)) // MOE_BLK)
    rows_total = nblk * MOE_BLK
    blk_row = jnp.arange(nblk, dtype=I32) * MOE_BLK
    blk_e = jnp.minimum(jnp.sum((pends[None, :] <= blk_row[:, None]).astype(I32), axis=1), N_EXPERTS - 1)
    seg_end = (pstart + counts)[blk_e]
    valid = jnp.clip(seg_end - blk_row, 0, MOE_BLK).astype(I32)
    nused = (pends[-1] // MOE_BLK).astype(I32).reshape(1)
    cstart = jnp.concatenate([jnp.zeros((1,), I32), (pends // MOE_BLK).astype(I32)])

    dest_t = _dest(eidx_t, _slab_order(pstart.astype(F32).reshape(N_EXPERTS, 1)))
    nplane = h2p.shape[0]
    plane_off = (jnp.arange(nplane, dtype=I32) * rows_total)[:, None, None, None]
    nwin = n // SC_WINDOW
    sidx = dest_t.reshape(TOP_K, nwin, SC_WINDOW).transpose(1, 0, 2)[None] + plane_off
    xg = _sc_scatter(h2p.reshape(nplane * n, LANES), sidx.reshape(nplane * nk), nplane * rows_total, TOP_K)
    x1s = _shared(x1, h2p, mod3, ws1[0].astype(BF16), ws3[0].astype(BF16), ws2[0].astype(BF16), s)
    y = _experts(cstart, nused, valid, xg.reshape(nplane, rows_total, LANES), w1_bf, w3_bf, w2_bf)
    y_flat = y.reshape(nplane * rows_total, LANES)
    npart = n // COMB_SPLIT
    gidx = (dest_t.reshape(1, TOP_K, COMB_SPLIT, npart) + plane_off).transpose(2, 0, 1, 3)
    gidx = gidx.reshape(COMB_SPLIT, nplane * TOP_K * npart)
    gates = gates_t.T
    out = None
    for part in range(COMB_SPLIT):
        yg = _sc_gather(y_flat, gidx[part]).reshape(nplane, TOP_K, npart, LANES)
        out = _combine(yg, gates, x1s, mod3, final_g.reshape(1, d), s, part, out)
    return out.reshape(b, s, d)
```

```python
import functools
import math

import jax
import jax.numpy as jnp
from jax import lax
from jax.experimental import pallas as pl
from jax.experimental.pallas import tpu as pltpu
from jax.experimental.pallas import tpu_sc as plsc

F32 = jnp.float32
BF16 = jnp.bfloat16
U32 = jnp.uint32
I32 = jnp.int32
HIGHEST = lax.Precision.HIGHEST

A_HEADS = 8
A_HEAD_DIM = 64
A_WIDTH = A_HEADS * A_HEAD_DIM
A_PATTERNS = ((128, 1), (512, 4), (2048, 16))
A_RADIUS = 64
REL_BUCKETS = 32
REL_MAX_DIST = 1024
B_HEADS = 8
B_NOPE = 64
B_ROPE = 32
B_VDIM = 64
B_WIDTH = B_HEADS * B_VDIM
Q_LORA = 384
KV_LORA = 256
ROPE_THETA = 10000.0
N_EXPERTS = 256
TOP_K = 8
N_GROUPS = 8
GROUP_SIZE = N_EXPERTS // N_GROUPS
TOPK_GROUPS = 4
ROUTED_SCALE = 2.5
EPS = 1e-6
NEG_INF = -1e30
LOG2E = math.log2(math.e)

LANES = 128
SUBLANES = 8
HEAD_PAD = 128

TM_INPROJ = 512
TM_OUTPROJ = 1024
DIL_QB = 128
DIL_KW = DIL_QB + 2 * A_RADIUS
DIL_ITER_ROWS = 4096
MLA_TQ = 1024
MLA_KC = 2048
ROUTE_T = 512
MOE_BLK = 512
EXP_AHEAD = 3
EXP_XBUF = EXP_AHEAD + 1
SC_WINDOW = 128
COMB_T = 512
COMB_SPLIT = 8
VMEM_LIMIT = 56 * 1024 * 1024


def _cparams(sem):
    return pltpu.CompilerParams(dimension_semantics=sem, vmem_limit_bytes=VMEM_LIMIT)


def _rms(x, g):
    return x * lax.rsqrt(jnp.mean(x * x, axis=-1, keepdims=True) + EPS) * g


def _silu(x):
    return x * jax.nn.sigmoid(x)


def _split_bf16(x):
    hi = lax.bitcast_convert_type(lax.bitcast_convert_type(x, U32) & jnp.uint32(0xFFFF0000), F32)
    return hi.astype(BF16), (x - hi).astype(BF16)


def _pack_rows(x):
    half = x.shape[1] // 2
    bits = lax.bitcast_convert_type(x.astype(BF16).astype(F32), U32)
    return (bits[:, :half] >> 16) | bits[:, half:]


def _unpack_rows(w):
    lo = lax.bitcast_convert_type(w << 16, F32)
    hi = lax.bitcast_convert_type(w & jnp.uint32(0xFFFF0000), F32)
    return lo, hi


def _ada_kernel(c_ref, w_ref, b_ref, o_ref):
    o_ref[...] = jnp.dot(_silu(c_ref[...]), w_ref[...], precision=HIGHEST,
                         preferred_element_type=F32) + b_ref[...]


def _ada(c, w_ada, b_ada):
    b, d = c.shape
    n6 = w_ada.shape[1] // d
    return pl.pallas_call(
        _ada_kernel,
        grid=(n6,),
        in_specs=[pl.BlockSpec((b, d), lambda j: (0, 0)),
                  pl.BlockSpec((d, d), lambda j: (0, j)),
                  pl.BlockSpec((1, d), lambda j: (0, j))],
        out_specs=pl.BlockSpec((b, d), lambda j: (0, j)),
        out_shape=jax.ShapeDtypeStruct((b, n6 * d), F32),
        compiler_params=_cparams(("parallel",)),
        name="ada",
    )(c, w_ada, b_ada.reshape(1, -1))


def _t5_bucket(rel):
    half = REL_BUCKETS // 2
    max_exact = half // 2
    ret = jnp.where(rel > 0, half, 0)
    n = jnp.abs(rel)
    nf = jnp.maximum(n, 1).astype(jnp.float32)
    large = max_exact + (jnp.log(nf / max_exact) / math.log(REL_MAX_DIST / max_exact)
                         * (half - max_exact)).astype(jnp.int32)
    large = jnp.minimum(large, half - 1)
    return ret + jnp.where(n < max_exact, n, large)


def _dil_geometry(s):
    geo, base = [], 0
    for _, dil in A_PATTERNS:
        sub_len = s // dil
        qb = sub_len if sub_len <= DIL_KW else DIL_QB
        kw = min(qb + 2 * A_RADIUS, sub_len)
        assert sub_len % qb == 0 and (sub_len == qb or kw == qb + 2 * A_RADIUS)
        shifts = (A_RADIUS,) if sub_len == qb else (A_RADIUS, 0, -A_RADIUS)
        geo.append((dil, qb, kw, shifts, base))
        base += len(shifts)
    return geo


def _bucket_tiles(geo):
    qi = jnp.arange(max(g[1] for g in geo), dtype=jnp.int32)[:, None]
    ki = jnp.arange(max(g[2] for g in geo), dtype=jnp.int32)[None, :]
    tiles = []
    for dilation, qb, kw, shifts, _ in geo:
        for shift in shifts:
            off = ki + shift - A_RADIUS - qi
            bkt = _t5_bucket(off * dilation)
            inside = (jnp.abs(off) <= A_RADIUS) & (qi < qb) & (ki < kw)
            tiles.append(jnp.where(inside, bkt, -1))
    return jnp.stack(tiles, axis=0)


def _bias_kernel(tab_ref, bkt_ref, o_ref):
    bkt = bkt_ref[0]
    for h in range(A_HEADS):
        acc = jnp.full(bkt.shape, NEG_INF, F32)
        for b in range(REL_BUCKETS):
            acc = jnp.where(bkt == b, tab_ref[b, h] * LOG2E, acc)
        o_ref[0, h] = acc


def _bias_tiles(rel_table, geo):
    bkt = _bucket_tiles(geo)
    nt = bkt.shape[0]
    return pl.pallas_call(
        _bias_kernel,
        grid=(nt,),
        in_specs=[pl.BlockSpec(memory_space=pltpu.SMEM),
                  pl.BlockSpec((1,) + bkt.shape[1:], lambda t: (t, 0, 0))],
        out_specs=pl.BlockSpec((1, A_HEADS) + bkt.shape[1:], lambda t: (t, 0, 0, 0)),
        out_shape=jax.ShapeDtypeStruct((nt, A_HEADS) + bkt.shape[1:], F32),
        compiler_params=_cparams(("parallel",)),
        name="bias",
    )(rel_table, bkt)


def _head_lanes():
    half = B_ROPE // 2
    lanes = []
    for j in range(B_NOPE):
        lanes.append(half + j if j < HEAD_PAD // 2 - half else HEAD_PAD // 2 + half + (j - (HEAD_PAD // 2 - half)))
    for r in range(B_ROPE):
        lanes.append(r if r < half else HEAD_PAD // 2 + (r - half))
    return lanes


def _rope(x, cos, sin):
    return x * cos + pltpu.roll(x, HEAD_PAD // 2, 1) * sin


def _inproj_kernel(x_ref, sc_ref, sh_ref, g1_ref, win_ref, qg_ref, wuq_ref, kvg_ref, wuk_ref, wv_ref,
                   cos_ref, sin_ref, qa_ref, ka_ref, va_ref, qb_ref, kb_ref, vb_ref):
    x = x_ref[...]
    h = _rms(x, g1_ref[...]) * (1.0 + sc_ref[...]) + sh_ref[...]
    proj = jnp.dot(h.astype(BF16), win_ref[...], preferred_element_type=F32)
    aw = A_WIDTH
    qa_ref[...] = proj[:, 0:aw] * (LOG2E / math.sqrt(A_HEAD_DIM))
    ka_ref[...] = proj[:, aw:2 * aw]
    va_ref[...] = proj[:, 2 * aw:3 * aw]
    c0 = 3 * aw
    q_lat = proj[:, c0:c0 + Q_LORA]
    kv_lat = proj[:, c0 + Q_LORA:c0 + Q_LORA + KV_LORA]
    kpe = proj[:, c0 + Q_LORA + KV_LORA:]
    qn = _rms(q_lat, qg_ref[...]).astype(BF16)
    kvn = _rms(kv_lat, kvg_ref[...]).astype(BF16)
    qm = jnp.dot(qn, wuq_ref[...], preferred_element_type=F32)
    kn = jnp.dot(kvn, wuk_ref[...], preferred_element_type=F32)
    vv = jnp.dot(kvn, wv_ref[...], preferred_element_type=F32)
    cos = cos_ref[...]
    sin = sin_ref[...]
    qscale = LOG2E / math.sqrt(B_NOPE + B_ROPE)
    cos_q = cos * qscale
    sin_q = sin * qscale
    kpe_rot = _rope(kpe, cos, sin)
    for hd in range(B_HEADS):
        sl = slice(hd * HEAD_PAD, (hd + 1) * HEAD_PAD)
        qb_ref[hd] = _rope(qm[:, sl], cos_q, sin_q).astype(BF16)
        kb_ref[hd] = (kn[:, sl] + kpe_rot).astype(BF16)
    for p in range(B_HEADS // 2):
        vb_ref[p] = vv[:, p * LANES:(p + 1) * LANES].astype(BF16)


def _inproj(x2, mod3, norm1_g, w_in_ext, q_norm_g, w_uq_p, kv_norm_g, w_uk_p, w_v, cos_t, sin_t, b, s):
    n, d = x2.shape
    tm = TM_INPROJ
    tpb = s // tm
    row = lambda i: (i, 0)
    const = lambda i: (0, 0)
    hm = lambda i: (i // tpb, 0, i % tpb, 0)
    return pl.pallas_call(
        _inproj_kernel,
        grid=(n // tm,),
        in_specs=[pl.BlockSpec((tm, d), row),
                  pl.BlockSpec((None, 1, d), lambda i: ((i // tpb) * 6 + 1, 0, 0)),
                  pl.BlockSpec((None, 1, d), lambda i: ((i // tpb) * 6 + 0, 0, 0)),
                  pl.BlockSpec((1, d), const),
                  pl.BlockSpec(w_in_ext.shape, const),
                  pl.BlockSpec((1, Q_LORA), const),
                  pl.BlockSpec(w_uq_p.shape, const),
                  pl.BlockSpec((1, KV_LORA), const),
                  pl.BlockSpec(w_uk_p.shape, const),
                  pl.BlockSpec(w_v.shape, const),
                  pl.BlockSpec((tm, HEAD_PAD), lambda i: (i % tpb, 0)),
                  pl.BlockSpec((tm, HEAD_PAD), lambda i: (i % tpb, 0))],
        out_specs=[pl.BlockSpec((tm, A_WIDTH), row),
                   pl.BlockSpec((tm, A_WIDTH), row),
                   pl.BlockSpec((tm, A_WIDTH), row),
                   pl.BlockSpec((None, B_HEADS, tm, HEAD_PAD), hm),
                   pl.BlockSpec((None, B_HEADS, tm, HEAD_PAD), hm),
                   pl.BlockSpec((None, B_HEADS // 2, tm, LANES), hm)],
        out_shape=[jax.ShapeDtypeStruct((n, A_WIDTH), F32),
                   jax.ShapeDtypeStruct((n, A_WIDTH), F32),
                   jax.ShapeDtypeStruct((n, A_WIDTH), F32),
                   jax.ShapeDtypeStruct((b, B_HEADS, s, HEAD_PAD), BF16),
                   jax.ShapeDtypeStruct((b, B_HEADS, s, HEAD_PAD), BF16),
                   jax.ShapeDtypeStruct((b, B_HEADS // 2, s, LANES), BF16)],
        compiler_params=_cparams(("parallel",)),
        name="inproj",
    )(x2, mod3, mod3, norm1_g, w_in_ext, q_norm_g, w_uq_p, kv_norm_g, w_uk_p, w_v, cos_t, sin_t)


def _dil_block(q_ref, k_ref, v_ref, bias_ref, o_scr, m_scr, l_scr, pi, geo, nblk, job):
    dil, qb, kwin, shifts, tile0 = geo
    sub_len = nblk * qb
    r = job // nblk
    bi = job % nblk
    q0 = bi * qb
    ws = jnp.clip(q0 - A_RADIUS, 0, sub_len - kwin)
    var = jnp.where(bi == 0, 0, jnp.where(bi == nblk - 1, 2, 1)) if len(shifts) > 1 else 0
    if dil == 1:
        qsl = pl.ds(pl.multiple_of(q0, qb), qb)
        ksl = pl.ds(pl.multiple_of(ws, A_RADIUS), kwin)
    else:
        qsl = pl.ds(r + dil * q0, qb, stride=dil)
        ksl = pl.ds(r + dil * ws, kwin, stride=dil)
    q = q_ref[qsl, :]
    kw = k_ref[ksl, :].astype(BF16)
    vw = v_ref[ksl, :].astype(BF16)
    lo = lax.broadcasted_iota(jnp.int32, q.shape, 1) < A_HEAD_DIM
    outs, ms, ls = [], [], []
    for hh in range(2):
        qm = jnp.where(lo if hh == 0 else jnp.logical_not(lo), q, 0.0).astype(BF16)
        sc = lax.dot_general(qm, kw, (((1,), (1,)), ((), ())), preferred_element_type=F32)
        sc = sc + bias_ref[tile0 + var, hh, :qb, :kwin]
        m = jnp.max(sc, axis=-1, keepdims=True)
        p = jnp.exp2(sc - m)
        l = jnp.sum(p, axis=-1, keepdims=True)
        outs.append(jnp.dot(p.astype(BF16), vw, preferred_element_type=F32))
        ms.append(m)
        ls.append(l)
    o_scr[pi, qsl, :] = jnp.where(lo, outs[0], outs[1])
    m_scr[pi, qsl, :] = jnp.where(lo, ms[0], ms[1])
    l_scr[pi, qsl, :] = jnp.where(lo, ls[0], ls[1])


def _dilated_kernel(q_ref, k_ref, v_ref, bias_ref, out_ref, o_scr, m_scr, l_scr):
    s = q_ref.shape[0]
    for pi, geo in enumerate(_dil_geometry(s)):
        dil, qb = geo[0], geo[1]
        njobs = s // qb
        unroll = DIL_ITER_ROWS // qb
        assert njobs % unroll == 0
        blk = functools.partial(_dil_block, q_ref, k_ref, v_ref, bias_ref, o_scr, m_scr, l_scr, pi, geo,
                                s // dil // qb)

        def group(g, c, blk=blk, unroll=unroll):
            for u in range(unroll):
                blk(g * unroll + u)
            return c

        lax.fori_loop(0, njobs // unroll, group, 0)

    chunk = 512

    def comb(i, c):
        rows = pl.ds(pl.multiple_of(i * chunk, chunk), chunk)
        m0, m1, m2 = m_scr[0, rows, :], m_scr[1, rows, :], m_scr[2, rows, :]
        mx = jnp.maximum(jnp.maximum(m0, m1), m2)
        e0, e1, e2 = jnp.exp2(m0 - mx), jnp.exp2(m1 - mx), jnp.exp2(m2 - mx)
        num = e0 * o_scr[0, rows, :] + e1 * o_scr[1, rows, :] + e2 * o_scr[2, rows, :]
        den = e0 * l_scr[0, rows, :] + e1 * l_scr[1, rows, :] + e2 * l_scr[2, rows, :]
        out_ref[rows, :] = num / den
        return c

    lax.fori_loop(0, s // chunk, comb, 0)


def _dilated(qa, ka, va, bias):
    b, s, _ = qa.shape
    npair = A_WIDTH // LANES
    blk = pl.BlockSpec((None, s, LANES), lambda bi, p: (bi, 0, p))
    return pl.pallas_call(
        _dilated_kernel,
        grid=(b, npair),
        in_specs=[blk, blk, blk,
                  pl.BlockSpec((bias.shape[0], 2) + bias.shape[2:], lambda bi, p: (0, p, 0, 0))],
        out_specs=blk,
        out_shape=jax.ShapeDtypeStruct((b, s, A_WIDTH), F32),
        scratch_shapes=[pltpu.VMEM((len(A_PATTERNS), s, LANES), F32)] * 3,
        compiler_params=_cparams(("parallel", "parallel")),
        name="dilated",
    )(qa, ka, va, bias)


def _mla_kernel(q_ref, k_ref, v_ref, w1_ref, w3_ref, w2_ref, o_ref, w1b_ref, w3b_ref, w2b_ref):
    w1b_ref[...] = w1_ref[...].astype(BF16)
    w3b_ref[...] = w3_ref[...].astype(BF16)
    w2b_ref[...] = w2_ref[...].astype(BF16)
    tq = q_ref.shape[1]
    nkc = k_ref.shape[1] // MLA_KC
    outs = []
    for hh in range(2):
        q = q_ref[hh]
        m = jnp.full((tq, 1), -jnp.inf, F32)
        l = jnp.zeros((tq, 1), F32)
        acc = jnp.zeros((tq, LANES), F32)
        for c in range(nkc):
            keys = slice(c * MLA_KC, (c + 1) * MLA_KC)
            sc = lax.dot_general(q, k_ref[hh, keys, :], (((1,), (1,)), ((), ())), preferred_element_type=F32)
            m_new = jnp.maximum(m, jnp.max(sc, axis=-1, keepdims=True))
            alpha = jnp.exp2(m - m_new)
            p = jnp.exp2(sc - m_new)
            l = alpha * l + jnp.sum(p, axis=-1, keepdims=True)
            acc = alpha * acc + jnp.dot(p.astype(BF16), v_ref[keys, :], preferred_element_type=F32)
            m = m_new
        outs.append(acc / l)
    lo = lax.broadcasted_iota(jnp.int32, outs[0].shape, 1) < B_VDIM
    o_ref[...] = jnp.where(lo, outs[0], outs[1])


def _mla(qb, kb, vb, w1, w3, w2):
    b, h, s, _ = qb.shape
    npair = h // 2
    tq = MLA_TQ
    nq = s // tq
    nsteps = b * npair * nq
    ne, d, f = w1.shape
    assert ne % nsteps == 0
    epb = ne // nsteps
    wmap = lambda bi, p, qi: ((bi * npair + p) * nq + qi, 0, 0)
    return pl.pallas_call(
        _mla_kernel,
        grid=(b, npair, nq),
        in_specs=[pl.BlockSpec((None, 2, tq, HEAD_PAD), lambda bi, p, qi: (bi, p, qi, 0)),
                  pl.BlockSpec((None, 2, s, HEAD_PAD), lambda bi, p, qi: (bi, p, 0, 0)),
                  pl.BlockSpec((None, None, s, LANES), lambda bi, p, qi: (bi, p, 0, 0)),
                  pl.BlockSpec((epb, d, f), wmap),
                  pl.BlockSpec((epb, d, f), wmap),
                  pl.BlockSpec((epb, f, d), wmap)],
        out_specs=[pl.BlockSpec((None, tq, LANES), lambda bi, p, qi: (bi, qi, p)),
                   pl.BlockSpec((epb, d, f), wmap),
                   pl.BlockSpec((epb, d, f), wmap),
                   pl.BlockSpec((epb, f, d), wmap)],
        out_shape=[jax.ShapeDtypeStruct((b, s, B_WIDTH), F32),
                   jax.ShapeDtypeStruct((ne, d, f), BF16),
                   jax.ShapeDtypeStruct((ne, d, f), BF16),
                   jax.ShapeDtypeStruct((ne, f, d), BF16)],
        compiler_params=_cparams(("parallel", "parallel", "arbitrary")),
        name="mla",
    )(qb, kb, vb, w1, w3, w2)


def _outproj_kernel(x_ref, a_ref, b_ref, ag_ref, bg_ref, wo_ref, g1_ref, n2_ref, sc_ref, sh_ref, g2_ref,
                    wrt_ref, ws1_ref, ws3_ref, ws2_ref, x1s_ref, h2p_ref, scores_ref):
    an = _rms(a_ref[...], ag_ref[...])
    bn = _rms(b_ref[...], bg_ref[...])
    mix = jnp.concatenate([an, bn], axis=-1).astype(BF16)
    x1 = x_ref[...] + g1_ref[...] * jnp.dot(mix, wo_ref[...], preferred_element_type=F32)
    h2 = _rms(x1, n2_ref[...]) * (1.0 + sc_ref[...]) + sh_ref[...]
    packed = _pack_rows(h2)
    for cg in range(h2p_ref.shape[0]):
        h2p_ref[cg] = packed[:, cg * LANES:(cg + 1) * LANES]
    h_hi, h_lo = _split_bf16(h2)
    logits = lax.dot_general(wrt_ref[...], jnp.concatenate([h_hi, h_lo, h_hi], axis=1),
                             (((1,), (1,)), ((), ())), preferred_element_type=F32)
    scores_ref[...] = jax.nn.sigmoid(logits)
    h2b = h2.astype(BF16)
    hid = _silu(jnp.dot(h2b, ws1_ref[...], preferred_element_type=F32)) * jnp.dot(
        h2b, ws3_ref[...], preferred_element_type=F32)
    shared = jnp.dot(hid.astype(BF16), ws2_ref[...], preferred_element_type=F32)
    x1s_ref[...] = x1 + g2_ref[...] * shared


def _outproj(x2, a_out, b_out, a_out_g, b_out_g, w_o, mod3, norm2_g, w_router_t, ws1, ws3, ws2, s):
    n, d = x2.shape
    tm = TM_OUTPROJ
    tpb = s // tm
    row = lambda i: (i, 0)
    const = lambda i: (0, 0)
    modspec = lambda j: pl.BlockSpec((None, 1, d), lambda i: ((i // tpb) * 6 + j, 0, 0))
    return pl.pallas_call(
        _outproj_kernel,
        grid=(n // tm,),
        in_specs=[pl.BlockSpec((tm, d), row),
                  pl.BlockSpec((tm, A_WIDTH), row),
                  pl.BlockSpec((tm, B_WIDTH), row),
                  pl.BlockSpec((1, A_WIDTH), const),
                  pl.BlockSpec((1, B_WIDTH), const),
                  pl.BlockSpec(w_o.shape, const),
                  modspec(2),
                  pl.BlockSpec((1, d), const),
                  modspec(4), modspec(3), modspec(5),
                  pl.BlockSpec(w_router_t.shape, const),
                  pl.BlockSpec(ws1.shape, const),
                  pl.BlockSpec(ws3.shape, const),
                  pl.BlockSpec(ws2.shape, const)],
        out_specs=[pl.BlockSpec((tm, d), row),
                   pl.BlockSpec((d // 2 // LANES, tm, LANES), lambda i: (0, i, 0)),
                   pl.BlockSpec((N_EXPERTS, tm), lambda i: (0, i))],
        out_shape=[jax.ShapeDtypeStruct((n, d), F32),
                   jax.ShapeDtypeStruct((d // 2 // LANES, n, LANES), U32),
                   jax.ShapeDtypeStruct((N_EXPERTS, n), F32)],
        compiler_params=_cparams(("parallel",)),
        name="outproj",
    )(x2, a_out, b_out, a_out_g, b_out_g, w_o, mod3, norm2_g, mod3, mod3, mod3, w_router_t, ws1, ws3, ws2)


def _slab_order(v):
    return v.reshape((N_GROUPS, GROUP_SIZE) + v.shape[1:]).swapaxes(0, 1).reshape(v.shape)


def _expert_order(v):
    return v.reshape((GROUP_SIZE, N_GROUPS) + v.shape[1:]).swapaxes(0, 1).reshape(v.shape)


def _sublane_all(x, op):
    for sh in (4, 2, 1):
        x = op(x, pltpu.roll(x, sh, 0))
    return x


def _route_kernel(st_ref, bias_ref, eidx_ref, gate_ref, cnt_ref):
    nsl = GROUP_SIZE
    t = st_ref.shape[1]
    sub = lax.broadcasted_iota(I32, (SUBLANES, t), 0)
    ninf = -jnp.inf
    big = jnp.int32(1 << 30)
    sc = [st_ref[j * SUBLANES:(j + 1) * SUBLANES, :] for j in range(nsl)]
    sel = [sc[j] + bias_ref[j * SUBLANES:(j + 1) * SUBLANES, :] for j in range(nsl)]
    eid = [sub * GROUP_SIZE + j for j in range(nsl)]

    m1 = sel[0]
    m2 = jnp.full_like(m1, ninf)
    for j in range(1, nsl):
        m2 = jnp.maximum(m2, jnp.minimum(m1, sel[j]))
        m1 = jnp.maximum(m1, sel[j])
    gs = m1 + m2

    rank = jnp.zeros((SUBLANES, t), I32)
    for sh in range(1, N_GROUPS):
        other = pltpu.roll(gs, sh, 0)
        ahead = (other > gs) | ((other == gs) & (sub >= sh))
        rank = rank + ahead.astype(I32)
    gmask = rank < TOPK_GROUPS

    msel = [jnp.where(gmask, sel[j], ninf) for j in range(nsl)]
    eidx = jnp.zeros((TOP_K, t), I32)
    gates = jnp.zeros((TOP_K, t), F32)
    for k in range(TOP_K):
        mx = msel[0]
        for j in range(1, nsl):
            mx = jnp.maximum(mx, msel[j])
        mx = _sublane_all(mx, jnp.maximum)
        cand = jnp.where(msel[0] == mx, eid[0], big)
        for j in range(1, nsl):
            cand = jnp.minimum(cand, jnp.where(msel[j] == mx, eid[j], big))
        idx = _sublane_all(cand, jnp.minimum)
        gk = jnp.zeros((SUBLANES, t), F32)
        for j in range(nsl):
            hit = eid[j] == idx
            gk = gk + jnp.where(hit, sc[j], 0.0)
            msel[j] = jnp.where(hit, ninf, msel[j])
        gk = _sublane_all(gk, jnp.add)
        eidx = jnp.where(sub == k, idx, eidx)
        gates = jnp.where(sub == k, gk, gates)
    gsum = _sublane_all(gates, jnp.add)
    eidx_ref[...] = eidx
    gate_ref[...] = gates / gsum * ROUTED_SCALE

    @pl.when(pl.program_id(0) == 0)
    def _():
        cnt_ref[...] = jnp.zeros_like(cnt_ref)

    for j in range(nsl):
        chosen = jnp.where(gmask & (msel[j] == ninf), 1.0, 0.0)
        cnt_ref[j * SUBLANES:(j + 1) * SUBLANES, :] += jnp.sum(chosen, axis=1, keepdims=True).astype(I32)


def _route(scores_t, e_bias_slab):
    e, n = scores_t.shape
    t = ROUTE_T
    return pl.pallas_call(
        _route_kernel,
        grid=(n // t,),
        in_specs=[pl.BlockSpec((e, t), lambda i: (0, i)),
                  pl.BlockSpec((e, 1), lambda i: (0, 0))],
        out_specs=[pl.BlockSpec((TOP_K, t), lambda i: (0, i)),
                   pl.BlockSpec((TOP_K, t), lambda i: (0, i)),
                   pl.BlockSpec((e, 1), lambda i: (0, 0))],
        out_shape=[jax.ShapeDtypeStruct((TOP_K, n), I32),
                   jax.ShapeDtypeStruct((TOP_K, n), F32),
                   jax.ShapeDtypeStruct((e, 1), I32)],
        compiler_params=_cparams(("arbitrary",)),
        name="route",
    )(scores_t, e_bias_slab)


def _dest_kernel(eidx_ref, pstart_ref, dest_ref, carry_ref):
    @pl.when(pl.program_id(0) == 0)
    def _():
        carry_ref[...] = jnp.zeros_like(carry_ref)

    nsl = GROUP_SIZE
    t = eidx_ref.shape[1]
    sub = lax.broadcasted_iota(I32, (SUBLANES, t), 0)
    eid = [sub * GROUP_SIZE + j for j in range(nsl)]
    ek = [eidx_ref[k:k + 1, :] for k in range(TOP_K)]
    slabs = []
    for j in range(nsl):
        oh = jnp.zeros((SUBLANES, t), F32)
        for k in range(TOP_K):
            oh = oh + (eid[j] == ek[k]).astype(F32)
        slabs.append(oh)
    onehot = jnp.concatenate(slabs, axis=0)
    row = lax.broadcasted_iota(I32, (t, t), 0)
    col = lax.broadcasted_iota(I32, (t, t), 1)
    upper = (row < col).astype(BF16)
    before = jnp.dot(onehot.astype(BF16), upper, preferred_element_type=F32)
    base = before + carry_ref[...] + pstart_ref[...]
    dest = jnp.zeros((TOP_K, t), I32)
    for k in range(TOP_K):
        acc = jnp.zeros((SUBLANES, t), F32)
        for j in range(nsl):
            acc = acc + jnp.where(eid[j] == ek[k], base[j * SUBLANES:(j + 1) * SUBLANES, :], 0.0)
        dk = _sublane_all(acc, jnp.add).astype(I32)
        dest = jnp.where(sub == k, dk, dest)
    dest_ref[...] = dest
    carry_ref[...] += jnp.sum(onehot, axis=1, keepdims=True)


def _dest(eidx_t, pstart_slab):
    n = eidx_t.shape[1]
    t = ROUTE_T
    return pl.pallas_call(
        _dest_kernel,
        grid=(n // t,),
        in_specs=[pl.BlockSpec((TOP_K, t), lambda i: (0, i)),
                  pl.BlockSpec((N_EXPERTS, 1), lambda i: (0, 0))],
        out_specs=pl.BlockSpec((TOP_K, t), lambda i: (0, i)),
        out_shape=jax.ShapeDtypeStruct((TOP_K, n), I32),
        scratch_shapes=[pltpu.VMEM((N_EXPERTS, 1), F32)],
        compiler_params=_cparams(("arbitrary",)),
        name="dest",
    )(eidx_t, pstart_slab)


def _sc_scatter(x, idx, rows_out, nslot):
    num = idx.shape[0]
    mesh = plsc.VectorSubcoreMesh(core_axis_name="core", subcore_axis_name="subcore")

    @pl.kernel(out_type=jax.ShapeDtypeStruct((rows_out, x.shape[1]), x.dtype), mesh=mesh, scratch_types=[])
    def scatter(x_hbm, i_hbm, o_hbm):
        def body(x_vmem, i_vmem):
            pltpu.sync_copy(x_vmem, o_hbm.at[i_vmem.at[0]])

        pltpu.emit_pipeline(
            body,
            grid=(num // SC_WINDOW,),
            in_specs=[pl.BlockSpec((SC_WINDOW, x.shape[1]), index_map=lambda i: (i // nslot, 0)),
                      pl.BlockSpec((1, SC_WINDOW), index_map=lambda i: (0, i))],
            out_specs=[],
            core_axis_name=("core", "subcore"),
            dimension_semantics=(pltpu.PARALLEL,),
        )(x_hbm, i_hbm)

    return scatter(x, idx.reshape(1, num))


def _expert_kernel(cstart_ref, nused_ref, valid_ref, w1_ref, w3_ref, w2_ref, xg_ref, y_ref,
                   xbuf, ybuf, xsem, ysem):
    e = pl.program_id(0)
    c0 = cstart_ref[e]
    c1 = cstart_ref[e + 1]
    nused = nused_ref[0]
    nchunks = xg_ref.shape[1] // MOE_BLK

    def rows(g):
        start = g * MOE_BLK
        return pl.ds(start if isinstance(g, int) else pl.multiple_of(start, MOE_BLK), MOE_BLK)

    nplane = xg_ref.shape[0]

    def xcopies(g):
        slot = g % EXP_XBUF
        return [pltpu.make_async_copy(xg_ref.at[pl.ds(0, nplane), rows(g)], xbuf.at[slot], xsem.at[slot])]

    def ycopies(g, slot):
        return [pltpu.make_async_copy(ybuf.at[slot], y_ref.at[pl.ds(0, nplane), rows(g)], ysem.at[slot])]

    @pl.when(e == 0)
    def _():
        ybuf[...] = jnp.zeros(ybuf.shape, ybuf.dtype)
        for j in range(EXP_AHEAD):
            @pl.when(j < nused)
            def _():
                for cp in xcopies(j):
                    cp.start()

    @pl.when(c1 > c0)
    def _():
        def ffn(g, yslot, nrows):
            row_id = lax.broadcasted_iota(I32, (nrows, LANES), 0)
            keep = row_id < valid_ref[g]
            halves = [_unpack_rows(jnp.where(keep, xbuf[g % EXP_XBUF, c, :nrows], jnp.uint32(0)))
                      for c in range(nplane)]
            xb = jnp.concatenate([h[0] for h in halves] + [h[1] for h in halves], axis=1).astype(BF16)
            hid = _silu(jnp.dot(xb, w1_ref[...], preferred_element_type=F32)) * jnp.dot(
                xb, w3_ref[...], preferred_element_type=F32)
            yp = _pack_rows(jnp.dot(hid.astype(BF16), w2_ref[...], preferred_element_type=F32))
            for c in range(nplane):
                ybuf[yslot, c, :nrows] = yp[:, c * LANES:(c + 1) * LANES]

        def chunk(g, carry):
            @pl.when(g + EXP_AHEAD < nused)
            def _():
                for cp in xcopies(g + EXP_AHEAD):
                    cp.start()

            for cp in xcopies(g):
                cp.wait()
            yslot = g % 2

            @pl.when(g >= 2)
            def _():
                for cp in ycopies(g - 2, yslot):
                    cp.wait()

            half = MOE_BLK // 2

            @pl.when(valid_ref[g] > half)
            def _():
                ffn(g, yslot, MOE_BLK)

            @pl.when(valid_ref[g] <= half)
            def _():
                ffn(g, yslot, half)

            for cp in ycopies(g, yslot):
                cp.start()
            return carry

        lax.fori_loop(c0, c1, chunk, 0)

    @pl.when(e == pl.num_programs(0) - 1)
    def _():
        for back in (2, 1):
            g = nused - back

            @pl.when(g >= 0)
            def _():
                for cp in ycopies(g, g % 2):
                    cp.wait()

        ybuf[0] = jnp.zeros(ybuf.shape[1:], ybuf.dtype)

        def zstart(g, c):
            for cp in ycopies(g, 0):
                cp.start()
            return c

        def zwait(g, c):
            for cp in ycopies(g, 0):
                cp.wait()
            return c

        lax.fori_loop(nused, nchunks, zstart, 0)
        lax.fori_loop(nused, nchunks, zwait, 0)


def _experts(cstart, nused, valid, xg, w1, w3, w2):
    nplane, rows, _ = xg.shape
    dh = nplane * LANES
    ne, d, f = w1.shape
    grid_spec = pltpu.PrefetchScalarGridSpec(
        num_scalar_prefetch=3,
        grid=(ne,),
        in_specs=[pl.BlockSpec((None, d, f), lambda e, cs, nu, va: (e, 0, 0)),
                  pl.BlockSpec((None, d, f), lambda e, cs, nu, va: (e, 0, 0)),
                  pl.BlockSpec((None, f, d), lambda e, cs, nu, va: (e, 0, 0)),
                  pl.BlockSpec(memory_space=pl.ANY)],
        out_specs=pl.BlockSpec(memory_space=pl.ANY),
        scratch_shapes=[pltpu.VMEM((EXP_XBUF, nplane, MOE_BLK, LANES), U32),
                        pltpu.VMEM((2, nplane, MOE_BLK, LANES), U32),
                        pltpu.SemaphoreType.DMA((EXP_XBUF,)), pltpu.SemaphoreType.DMA((2,))],
    )
    return pl.pallas_call(
        _expert_kernel,
        grid_spec=grid_spec,
        out_shape=jax.ShapeDtypeStruct((dh // LANES, rows, LANES), U32),
        compiler_params=_cparams(("arbitrary",)),
        name="experts",
    )(cstart, nused, valid, w1, w3, w2, xg)


def _sc_gather(x, idx):
    num = idx.shape[0]
    mesh = plsc.VectorSubcoreMesh(core_axis_name="core", subcore_axis_name="subcore")

    @pl.kernel(out_type=jax.ShapeDtypeStruct((num, x.shape[1]), x.dtype), mesh=mesh)
    def gather(x_hbm, i_hbm, o_hbm):
        def body(i_vmem, o_vmem):
            pltpu.sync_copy(x_hbm.at[i_vmem.at[0]], o_vmem)

        pltpu.emit_pipeline(
            body,
            grid=(num // SC_WINDOW,),
            in_specs=[pl.BlockSpec((1, SC_WINDOW), index_map=lambda i: (0, i))],
            out_specs=[pl.BlockSpec((SC_WINDOW, x.shape[1]), index_map=lambda i: (i, 0))],
            core_axis_name=("core", "subcore"),
            dimension_semantics=(pltpu.PARALLEL,),
        )(i_hbm, o_hbm)

    return gather(x, idx.reshape(1, num))


def _combine_kernel(yg_ref, gate_ref, x1s_ref, g2_ref, fg_ref, *rest):
    out_ref = rest[-1]
    gates = gate_ref[...]
    nch = yg_ref.shape[0]
    r_lo = [None] * nch
    r_hi = [None] * nch
    for k in range(TOP_K):
        gk = gates[:, k:k + 1]
        for c in range(nch):
            lo, hi = _unpack_rows(yg_ref[c, k])
            r_lo[c] = gk * lo if k == 0 else r_lo[c] + gk * lo
            r_hi[c] = gk * hi if k == 0 else r_hi[c] + gk * hi
    routed = jnp.concatenate(r_lo + r_hi, axis=1)
    x2 = x1s_ref[...] + g2_ref[...] * routed
    out_ref[...] = _rms(x2, fg_ref[...])


def _combine(yg, gates, x1s, mod3, final_g, s, part, prev_out):
    n, d = x1s.shape
    t = COMB_T
    tpb = s // t
    nch, _, npart, _ = yg.shape
    off = part * (npart // t)
    in_specs = [pl.BlockSpec((nch, TOP_K, t, LANES), lambda i: (0, 0, i, 0)),
                pl.BlockSpec((t, TOP_K), lambda i: (i + off, 0)),
                pl.BlockSpec((t, d), lambda i: (i + off, 0)),
                pl.BlockSpec((None, 1, d), lambda i: (((i + off) // tpb) * 6 + 5, 0, 0)),
                pl.BlockSpec((1, d), lambda i: (0, 0))]
    args = [yg, gates, x1s, mod3, final_g]
    aliases = {}
    if prev_out is not None:
        in_specs.append(pl.BlockSpec(memory_space=pl.ANY))
        args.append(prev_out)
        aliases = {len(args) - 1: 0}
    return pl.pallas_call(
        _combine_kernel,
        grid=(npart // t,),
        in_specs=in_specs,
        out_specs=pl.BlockSpec((t, d), lambda i: (i + off, 0)),
        out_shape=jax.ShapeDtypeStruct((n, d), F32),
        input_output_aliases=aliases,
        compiler_params=_cparams(("parallel",)),
        name="combine",
    )(*args)


def _place_cols(w, lanes):
    r, h, _ = w.shape
    src = {lane: j for j, lane in enumerate(lanes)}
    pieces, lane = [], 0
    while lane < HEAD_PAD:
        end = lane + 1
        if lane in src:
            while end < HEAD_PAD and end in src and src[end] == src[end - 1] + 1:
                end += 1
            pieces.append(w[:, :, src[lane]:src[lane] + end - lane])
        else:
            while end < HEAD_PAD and end not in src:
                end += 1
            pieces.append(jnp.zeros((r, h, end - lane), w.dtype))
        lane = end
    return jnp.concatenate(pieces, axis=2).reshape(r, h * HEAD_PAD)


def _rope_tables(s):
    half = B_ROPE // 2
    rope_lanes = _head_lanes()[B_NOPE:]
    inv = ROPE_THETA ** (-jnp.arange(0, B_ROPE, 2, dtype=jnp.float32) / B_ROPE)
    ang = jnp.arange(s, dtype=jnp.float32)[:, None] * inv[None, :]
    cos, sin = jnp.cos(ang), jnp.sin(ang)
    assert cos.shape[1] == half
    cos_t = 1.0 + _place_cols((jnp.concatenate([cos, cos], axis=1) - 1.0)[:, None, :], rope_lanes)
    sin_t = _place_cols(jnp.concatenate([-sin, sin], axis=1)[:, None, :], rope_lanes)
    return cos_t, sin_t


def kernel(x, c, w_ada, b_ada, norm1_g, w_in, q_norm_g, w_uq, kv_norm_g, w_ukv, rel_table, a_out_g, b_out_g,
           w_o, norm2_g, w_router, e_bias, w1, w3, w2, ws1, ws3, ws2, final_g):
    b, s, d = x.shape
    n = b * s
    assert w_ada.shape[0] == 1, "single layer"
    x2 = x.reshape(n, d)

    mod = _ada(c, w_ada[0], b_ada[0])
    mod3 = mod.reshape(b * 6, 1, d)
    bias = _bias_tiles(rel_table, _dil_geometry(s))

    wi = w_in[0]
    c_kpe = 3 * A_WIDTH + Q_LORA + KV_LORA
    lanes = _head_lanes()
    kpe_cols = _place_cols(wi[:, None, c_kpe:], lanes[B_NOPE:])
    w_in_ext = jnp.concatenate([wi[:, :c_kpe], kpe_cols], axis=1).astype(BF16)
    w_uq_p = _place_cols(w_uq[0].reshape(Q_LORA, B_HEADS, B_NOPE + B_ROPE), lanes).astype(BF16)
    w_uk_p = _place_cols(w_ukv[0].reshape(KV_LORA, B_HEADS, B_NOPE + B_VDIM)[:, :, :B_NOPE],
                         lanes[:B_NOPE]).astype(BF16)
    w_v = w_ukv[0].reshape(KV_LORA, B_HEADS, B_NOPE + B_VDIM)[:, :, B_NOPE:].reshape(KV_LORA, B_WIDTH).astype(BF16)
    cos_t, sin_t = _rope_tables(s)

    qa, ka, va, qb, kb, vb = _inproj(x2, mod3, norm1_g, w_in_ext, q_norm_g, w_uq_p, kv_norm_g, w_uk_p, w_v,
                                     cos_t, sin_t, b, s)
    a_out = _dilated(qa.reshape(b, s, A_WIDTH), ka.reshape(b, s, A_WIDTH), va.reshape(b, s, A_WIDTH), bias)
    b_out, w1_bf, w3_bf, w2_bf = _mla(qb, kb, vb, w1[0], w3[0], w2[0])

    wr = _slab_order(w_router[0].T)
    wr_hi, wr_lo = _split_bf16(wr)
    w_router_t = jnp.concatenate([wr_hi, wr_hi, wr_lo], axis=1)
    x1s, h2p, scores_t = _outproj(x2, a_out.reshape(n, A_WIDTH), b_out.reshape(n, B_WIDTH), a_out_g, b_out_g,
                                  w_o[0].astype(BF16), mod3, norm2_g, w_router_t,
                                  ws1[0].astype(BF16), ws3[0].astype(BF16), ws2[0].astype(BF16), s)

    eidx_t, gates_t, counts_slab = _route(scores_t, _slab_order(e_bias.reshape(N_EXPERTS, 1)))

    counts = _expert_order(counts_slab)[:, 0]
    padded = (counts + MOE_BLK - 1) // MOE_BLK * MOE_BLK
    pends = jnp.cumsum(padded)
    pstart = pends - padded
    nk = n * TOP_K
    nblk = -(-(nk + N_EXPERTS * (MOE_BLK - 1)) // MOE_BLK)
    rows_total = nblk * MOE_BLK
    blk_row = jnp.arange(nblk, dtype=I32) * MOE_BLK
    blk_e = jnp.minimum(jnp.sum((pends[None, :] <= blk_row[:, None]).astype(I32), axis=1), N_EXPERTS - 1)
    seg_end = (pstart + counts)[blk_e]
    valid = jnp.clip(seg_end - blk_row, 0, MOE_BLK).astype(I32)
    nused = (pends[-1] // MOE_BLK).astype(I32).reshape(1)
    cstart = jnp.concatenate([jnp.zeros((1,), I32), (pends // MOE_BLK).astype(I32)])

    dest_t = _dest(eidx_t, _slab_order(pstart.astype(F32).reshape(N_EXPERTS, 1)))
    nplane = h2p.shape[0]
    plane_off = (jnp.arange(nplane, dtype=I32) * rows_total)[:, None, None, None]
    nwin = n // SC_WINDOW
    sidx = dest_t.reshape(TOP_K, nwin, SC_WINDOW).transpose(1, 0, 2)[None] + plane_off
    xg = _sc_scatter(h2p.reshape(nplane * n, LANES), sidx.reshape(nplane * nk), nplane * rows_total, TOP_K)
    y = _experts(cstart, nused, valid, xg.reshape(nplane, rows_total, LANES), w1_bf, w3_bf, w2_bf)
    y_flat = y.reshape(nplane * rows_total, LANES)
    npart = n // COMB_SPLIT
    gidx = (dest_t.reshape(1, TOP_K, COMB_SPLIT, npart) + plane_off).transpose(2, 0, 1, 3)
    gidx = gidx.reshape(COMB_SPLIT, nplane * TOP_K * npart)
    gates = gates_t.T
    out = None
    for part in range(COMB_SPLIT):
        yg = _sc_gather(y_flat, gidx[part]).reshape(nplane, TOP_K, npart, LANES)
        out = _combine(yg, gates, x1s, mod3, final_g.reshape(1, d), s, part, out)
    return out.reshape(b, s, d)
```
